```python
import jax
import jax.numpy as jnp
from jax import lax
import numpy as np

D_MODEL = 1024
BATCH = 32
SEQ = 256
DEPTH = 1
DEC_BATCH = 8
DEC_SEQ = 2048
PAST_LEN = 512

GRID_W = 64
D_CONV = D_MODEL
D_RNN = D_MODEL
N_RNN_HEADS = 8
RNN_HEAD_DIM = D_RNN // N_RNN_HEADS
CONV_MIX_W = 3
CONV_MIX_LEFT = 1
RNN_CONV_W = 4
RNN_CONV_LEFT = 2
RG_C = 8.0
N_GROUPS = 4
EXPERTS_PER_GROUP = 8
N_EXPERTS = N_GROUPS * EXPERTS_PER_GROUP
TOP_K = 2
D_EXPERT = 512
EPS = 1e-6
D_IN = 3 * D_CONV + 2 * D_RNN + 2 * D_MODEL
IN_SPLITS = (D_CONV, 2 * D_CONV, 3 * D_CONV, 3 * D_CONV + D_RNN, 3 * D_CONV + 2 * D_RNN,
             3 * D_CONV + 2 * D_RNN + D_MODEL)
F32 = jnp.float32

kernel_name = 'hybrid_conv_rglru_hmoe_diffusion_step'


def rmsnorm(x, g):
    xf = x.astype(F32)
    y = xf * lax.rsqrt(jnp.mean(xf * xf, axis=-1, keepdims=True) + EPS)
    return (y * g.astype(F32)).astype(x.dtype)


def dwconv(x, w, left):
    k_w = w.shape[0]
    n_pos = x.shape[1]
    xp = jnp.pad(x, ((0, 0), (left, k_w - 1 - left), (0, 0)))
    y = xp[:, 0:n_pos] * w[0]
    for k in range(1, k_w):
        y = y + xp[:, k:k + n_pos] * w[k]
    return y


def conv_over_tokens(x, w, left, on_grid):
    if on_grid:
        b, n, ch = x.shape
        rows = n // GRID_W
        return dwconv(x.reshape(b * rows, GRID_W, ch), w, left).reshape(b, n, ch)
    return dwconv(x, w, left)


def linear_scan(a, u, h0, reverse):
    first = -1 if reverse else 0
    u = u.at[:, first].add(a[:, first] * h0)

    def combine(l, r):
        a_l, u_l = l
        a_r, u_r = r
        return a_l * a_r, a_r * u_l + u_r

    _, h = lax.associative_scan(combine, (a, u), axis=1, reverse=reverse)
    return h


def rg_lru(xc, h0, w_a, b_a, w_x, b_x, lam, reverse):
    bsz, n, _ = xc.shape
    xh = xc.reshape(bsz, n, N_RNN_HEADS, RNN_HEAD_DIM)
    r = jax.nn.sigmoid(jnp.einsum('bnhi,hij->bnhj', xh, w_a).reshape(bsz, n, D_RNN) + b_a).astype(F32)
    i = jax.nn.sigmoid(jnp.einsum('bnhi,hij->bnhj', xh, w_x).reshape(bsz, n, D_RNN) + b_x).astype(F32)
    log_a = -RG_C * r * jax.nn.softplus(-lam.astype(F32))
    a = jnp.exp(log_a)
    u = jnp.sqrt(-jnp.expm1(2.0 * log_a)) * (i * xc.astype(F32))
    h = linear_scan(a, u, h0.astype(F32), reverse)
    h_last = h[:, 0] if reverse else h[:, -1]
    return h, h_last


def token_mixer(h, h0, on_grid, p):
    u = h @ p['w_in']
    b_c, c_c, v_c, x_r, g_r, m_a, m_b = jnp.split(u, IN_SPLITS, axis=-1)
    y_a = (b_c * conv_over_tokens(c_c * v_c, p['conv_w'], CONV_MIX_LEFT, on_grid)) @ p['w_conv_out']
    x_c = conv_over_tokens(x_r, p['rnn_conv_w'], RNN_CONV_LEFT, False) + p['rnn_conv_b']
    h_f, last_f = rg_lru(x_c, h0[:, 0], p['w_gate_a'][0], p['b_gate_a'][0], p['w_gate_x'][0],
                         p['b_gate_x'][0], p['lam'][0], False)
    h_b, last_b = rg_lru(x_c, h0[:, 1], p['w_gate_a'][1], p['b_gate_a'][1], p['w_gate_x'][1],
                         p['b_gate_x'][1], p['lam'][1], True)
    y_b = ((h_f + h_b).astype(h.dtype) * jax.nn.gelu(g_r)) @ p['w_rnn_out']
    merged = jax.nn.sigmoid(m_a) * y_a + jax.nn.sigmoid(m_b) * y_b
    return merged @ p['w_o'], jnp.stack([last_f, last_b], axis=1)


def hier_moe(h, p):
    bsz, n, d = h.shape
    t = h.reshape(bsz * n, d)
    lg = (t @ p['w_router_group'] + p['b_router_group']).astype(F32)
    pg = jax.nn.softmax(lg, axis=-1)
    grp = jnp.argmax(lg, axis=-1)
    p_grp = jnp.take_along_axis(pg, grp[:, None], axis=-1)
    le = (t @ p['w_router_expert'] + p['b_router_expert']).astype(F32)
    le = le.reshape(-1, N_GROUPS, EXPERTS_PER_GROUP)
    le_g = jnp.take_along_axis(le, grp[:, None, None], axis=1)[:, 0]
    pe = jax.nn.softmax(le_g, axis=-1)
    top_p, top_i = lax.top_k(pe, TOP_K)
    wts = p_grp * top_p / jnp.sum(top_p, axis=-1, keepdims=True)
    ids = grp[:, None] * EXPERTS_PER_GROUP + top_i
    gates = jnp.sum(jax.nn.one_hot(ids, N_EXPERTS, dtype=F32) * wts[..., None], axis=1).astype(t.dtype)
    y = jnp.zeros_like(t)
    for e in range(N_EXPERTS):
        z = jax.nn.silu(t @ p['w1'][e]) * (t @ p['w3'][e])
        y = y + gates[:, e:e + 1] * (z @ p['w2'][e])
    return y.reshape(bsz, n, d)


def layer(x, cond, h0, on_grid, p):
    mod = (jax.nn.silu(cond) @ p['w_ada'] + p['b_ada'])[:, None, :]
    sh1, sc1, g1, sh2, sc2, g2 = jnp.split(mod, 6, axis=-1)
    h = rmsnorm(x, p['g_norm1']) * (1 + sc1) + sh1
    mix, h_last = token_mixer(h, h0, on_grid, p)
    x = x + g1 * mix
    h = rmsnorm(x, p['g_norm2']) * (1 + sc2) + sh2
    x = x + g2 * hier_moe(h, p)
    return x, h_last


def setup_inputs(seed: int = 0) -> dict:
    key = jax.random.key(seed)
    ks = jax.random.split(key, 32)
    nrm = jax.random.normal
    s_d = D_MODEL ** -0.5
    a0 = jax.random.uniform(ks[14], (DEPTH, 2, D_RNN), F32, minval=0.9, maxval=0.999)
    s = a0 ** (1.0 / RG_C)
    lam = jnp.log(s) - jnp.log1p(-s)
    return {
        'x_prompt': nrm(ks[0], (BATCH, SEQ, D_MODEL), F32),
        'x_sample': nrm(ks[1], (DEC_BATCH, DEC_SEQ, D_MODEL), F32),
        'state_rnn': 0.5 * nrm(ks[2], (DEC_BATCH, DEPTH, 2, D_RNN), F32),
        'c': nrm(ks[3], (DEC_BATCH, D_MODEL), F32),
        'c_ctx': nrm(ks[4], (D_MODEL,), F32),
        'w_ada': s_d * nrm(ks[5], (DEPTH, D_MODEL, 6 * D_MODEL), F32),
        'b_ada': 0.02 * nrm(ks[6], (DEPTH, 6 * D_MODEL), F32),
        'g_norm1': 1.0 + 0.02 * nrm(ks[7], (DEPTH, D_MODEL), F32),
        'g_norm2': 1.0 + 0.02 * nrm(ks[8], (DEPTH, D_MODEL), F32),
        'w_in': s_d * nrm(ks[9], (DEPTH, D_MODEL, D_IN), F32),
        'conv_w': (CONV_MIX_W ** -0.5) * nrm(ks[10], (DEPTH, CONV_MIX_W, D_CONV), F32),
        'w_conv_out': (D_CONV ** -0.5) * nrm(ks[11], (DEPTH, D_CONV, D_MODEL), F32),
        'rnn_conv_w': (RNN_CONV_W ** -0.5) * nrm(ks[12], (DEPTH, RNN_CONV_W, D_RNN), F32),
        'rnn_conv_b': 0.02 * nrm(ks[13], (DEPTH, D_RNN), F32),
        'w_gate_a': (RNN_HEAD_DIM ** -0.5) * nrm(ks[15], (DEPTH, 2, N_RNN_HEADS, RNN_HEAD_DIM, RNN_HEAD_DIM), F32),
        'b_gate_a': 0.02 * nrm(ks[16], (DEPTH, 2, D_RNN), F32),
        'w_gate_x': (RNN_HEAD_DIM ** -0.5) * nrm(ks[17], (DEPTH, 2, N_RNN_HEADS, RNN_HEAD_DIM, RNN_HEAD_DIM), F32),
        'b_gate_x': 0.02 * nrm(ks[18], (DEPTH, 2, D_RNN), F32),
        'lam': lam,
        'w_rnn_out': (D_RNN ** -0.5) * nrm(ks[19], (DEPTH, D_RNN, D_MODEL), F32),
        'w_o': s_d * nrm(ks[20], (DEPTH, D_MODEL, D_MODEL), F32),
        'w_router_group': s_d * nrm(ks[21], (DEPTH, D_MODEL, N_GROUPS), F32),
        'b_router_group': 0.01 * nrm(ks[22], (DEPTH, N_GROUPS), F32),
        'w_router_expert': s_d * nrm(ks[23], (DEPTH, D_MODEL, N_EXPERTS), F32),
        'b_router_expert': 0.01 * nrm(ks[24], (DEPTH, N_EXPERTS), F32),
        'w1': s_d * nrm(ks[25], (DEPTH, N_EXPERTS, D_MODEL, D_EXPERT), F32),
        'w3': s_d * nrm(ks[26], (DEPTH, N_EXPERTS, D_MODEL, D_EXPERT), F32),
        'w2': (D_EXPERT ** -0.5) * nrm(ks[27], (DEPTH, N_EXPERTS, D_EXPERT, D_MODEL), F32),
        'g_final': 1.0 + 0.02 * nrm(ks[28], (D_MODEL,), F32),
    }


def reference(x_prompt, x_sample, state_rnn, c, c_ctx, w_ada, b_ada, g_norm1, g_norm2, w_in,
              conv_w, w_conv_out, rnn_conv_w, rnn_conv_b, w_gate_a, b_gate_a, w_gate_x, b_gate_x,
              lam, w_rnn_out, w_o, w_router_group, b_router_group, w_router_expert,
              b_router_expert, w1, w3, w2, g_final):
    layers = [dict(w_ada=w_ada[l], b_ada=b_ada[l], g_norm1=g_norm1[l], g_norm2=g_norm2[l],
                   w_in=w_in[l], conv_w=conv_w[l], w_conv_out=w_conv_out[l],
                   rnn_conv_w=rnn_conv_w[l], rnn_conv_b=rnn_conv_b[l], w_gate_a=w_gate_a[l],
                   b_gate_a=b_gate_a[l], w_gate_x=w_gate_x[l], b_gate_x=b_gate_x[l], lam=lam[l],
                   w_rnn_out=w_rnn_out[l], w_o=w_o[l], w_router_group=w_router_group[l],
                   b_router_group=b_router_group[l], w_router_expert=w_router_expert[l],
                   b_router_expert=b_router_expert[l], w1=w1[l], w3=w3[l], w2=w2[l])
              for l in range(DEPTH)]

    cond_ctx = c_ctx[None, :]
    h0_ctx = jnp.zeros((x_prompt.shape[0], 2, D_RNN), F32)
    xp = x_prompt
    finals = []
    for l in range(DEPTH):
        xp, h_last = layer(xp, cond_ctx, h0_ctx, False, layers[l])
        finals.append(h_last)
    state_rnn_new = jnp.stack(finals, axis=1).astype(x_prompt.dtype)
    y_prompt = rmsnorm(xp, g_final)

    xs = x_sample
    for l in range(DEPTH):
        xs, _ = layer(xs, c, state_rnn[:, l], True, layers[l])
    y_sample = rmsnorm(xs, g_final)

    return (y_prompt, y_sample, state_rnn_new)
```

```python
import functools

import jax
import jax.numpy as jnp
from jax import lax
from jax.experimental import pallas as pl
from jax.experimental.pallas import tpu as pltpu

D = 1024
N_HEADS = 8
HEAD = D // N_HEADS
GRID_W = 64
RG_C = 8.0
N_GROUPS = 4
EPG = 8
N_EXPERTS = N_GROUPS * EPG
D_EXPERT = 512
EPS = 1e-6
F32 = jnp.float32
BF16 = jnp.bfloat16

V7X_LANES = 128
V7X_SUBLANES = 8
V7X_VMEM_LIMIT_BYTES = 56 * 1024 * 1024

SEQ_TILE = 2048
XR_CHUNK = 256
SCAN_T = 64
SCAN_B = V7X_SUBLANES
MIX_TM = 256
FIN_TM = 512
ROUTE_PAD = 128
NEG_BIG = -1e30


def _sigmoid(x):
    return 0.5 * jnp.tanh(0.5 * x) + 0.5


def _norm_mod(x, g, scale, shift):
    ms = jnp.mean(x * x, axis=-1, keepdims=True)
    return (x * lax.rsqrt(ms + EPS) * g) * (1.0 + scale) + shift


def _dot(a, b):
    return jnp.dot(a, b, preferred_element_type=F32)


def _params(sem, vmem=V7X_VMEM_LIMIT_BYTES):
    return pltpu.CompilerParams(dimension_semantics=sem, vmem_limit_bytes=vmem)


def _const_spec(shape):
    zeros = (0,) * len(shape)
    return pl.BlockSpec(shape, lambda *_: zeros, pipeline_mode=pl.Buffered(1))


def _ada_body(c_ref, w_ref, b_ref, o_ref):
    c = c_ref[...]
    s = (c * _sigmoid(c)).astype(BF16)
    o_ref[...] = _dot(s, w_ref[...].astype(BF16)) + b_ref[...]


def _ada(cond, w_ada, b_ada):
    rows = cond.shape[0]
    n_out = w_ada.shape[1]
    return pl.pallas_call(
        _ada_body,
        grid=(n_out // D,),
        in_specs=[pl.BlockSpec((rows, D), lambda i: (0, 0)),
                  pl.BlockSpec((D, D), lambda i: (0, i)),
                  pl.BlockSpec((1, D), lambda i: (0, i))],
        out_specs=pl.BlockSpec((rows, D), lambda i: (0, i)),
        out_shape=jax.ShapeDtypeStruct((rows, n_out), F32),
        compiler_params=_params(("parallel",)),
        name="ada",
    )(cond, w_ada, b_ada.reshape(1, n_out))


def _xr_body(x_ref, mod_ref, g_ref, w_ref, cw_ref, cb_ref, o_ref, xr_s, *, seq_len):
    n = x_ref.shape[0]
    halo = V7X_SUBLANES
    ch = XR_CHUNK
    xr_s[0:halo, :] = jnp.zeros((halo, D), F32)
    xr_s[halo + n:2 * halo + n, :] = jnp.zeros((halo, D), F32)
    for r0 in range(0, n, ch):
        x = x_ref[r0:r0 + ch, :]
        h = _norm_mod(x, g_ref[...], mod_ref[0, 1:2, :], mod_ref[0, 0:1, :]).astype(BF16)
        xr_s[halo + r0:halo + r0 + ch, :] = _dot(h, w_ref[...])
    win = ch + 2 * halo
    for r0 in range(0, n, ch):
        w = xr_s[r0:r0 + win, :]
        pos = (lax.broadcasted_iota(jnp.int32, (ch, 1), 0) + r0) % seq_len
        y = w[halo:halo + ch] * cw_ref[2:3, :] + cb_ref[...]
        y = y + jnp.where(pos >= 2, pltpu.roll(w, 2, 0)[halo:halo + ch], 0.0) * cw_ref[0:1, :]
        y = y + jnp.where(pos >= 1, pltpu.roll(w, 1, 0)[halo:halo + ch], 0.0) * cw_ref[1:2, :]
        y = y + jnp.where(pos <= seq_len - 2,
                          pltpu.roll(w, win - 1, 0)[halo:halo + ch], 0.0) * cw_ref[3:4, :]
        o_ref[r0:r0 + ch, :] = y


def _xr_conv(x, mod, mod_per_tile, g1, w_xr, cw, cb, seq_len):
    n = x.shape[0]
    assert SEQ_TILE % seq_len == 0 and n % SEQ_TILE == 0
    mod_map = (lambda i: (i, 0, 0)) if mod_per_tile else (lambda i: (0, 0, 0))
    return pl.pallas_call(
        functools.partial(_xr_body, seq_len=seq_len),
        grid=(n // SEQ_TILE,),
        in_specs=[pl.BlockSpec((SEQ_TILE, D), lambda i: (i, 0)),
                  pl.BlockSpec((1, 6, D), mod_map),
                  _const_spec((1, D)),
                  _const_spec((D, D)),
                  _const_spec((4, D)),
                  _const_spec((1, D))],
        out_specs=pl.BlockSpec((SEQ_TILE, D), lambda i: (i, 0)),
        out_shape=jax.ShapeDtypeStruct((n, D), F32),
        scratch_shapes=[pltpu.VMEM((SEQ_TILE + 2 * V7X_SUBLANES, D), F32)],
        compiler_params=_params(("parallel",)),
        name="xr_conv",
    )(x, mod, g1, w_xr, cw, cb)


def _scan_body(xf_ref, xb_ref, h0_ref, wg_ref, ba_ref, bx_ref, lam_ref,
               hf_ref, hb_ref, last_ref, a_s, u_s, hc):
    j = pl.program_id(1)
    n_t = pl.num_programs(1)
    t_len = xf_ref.shape[1]
    rows = SCAN_B * t_len

    @pl.when(j == 0)
    def _():
        for d in range(2):
            for s in range(N_HEADS):
                hc[d, s] = h0_ref[0, d, :, s * HEAD:(s + 1) * HEAD]

    for d, x_ref in ((0, xf_ref), (1, xb_ref)):
        z = -lam_ref[d:d + 1, :]
        sp = jnp.maximum(z, 0.0) + jnp.log(1.0 + jnp.exp(-jnp.abs(z)))
        for hd in range(N_HEADS):
            sl = slice(hd * HEAD, (hd + 1) * HEAD)
            xh = x_ref[:, :, sl].reshape(rows, HEAD)
            g = _dot(xh.astype(BF16), wg_ref[d, hd])
            r = _sigmoid(g[:, :HEAD] + ba_ref[d:d + 1, sl])
            i = _sigmoid(g[:, HEAD:] + bx_ref[d:d + 1, sl])
            a = jnp.exp((-RG_C) * r * sp[:, sl])
            u = jnp.sqrt(1.0 - a * a) * (i * xh)
            for b in range(SCAN_B):
                a_s[d, hd, pl.ds(b, t_len, stride=SCAN_B), :] = a[b * t_len:(b + 1) * t_len]
                u_s[d, hd, pl.ds(b, t_len, stride=SCAN_B), :] = u[b * t_len:(b + 1) * t_len]

    def step(t, carry):
        h_f, h_b = carry
        rf = pl.multiple_of(t * SCAN_B, SCAN_B)
        rb = pl.multiple_of((t_len - 1 - t) * SCAN_B, SCAN_B)
        h_f = a_s[0, :, pl.ds(rf, SCAN_B), :] * h_f + u_s[0, :, pl.ds(rf, SCAN_B), :]
        h_b = a_s[1, :, pl.ds(rb, SCAN_B), :] * h_b + u_s[1, :, pl.ds(rb, SCAN_B), :]
        a_s[0, :, pl.ds(rf, SCAN_B), :] = h_f
        a_s[1, :, pl.ds(rb, SCAN_B), :] = h_b
        return h_f, h_b

    h_f, h_b = lax.fori_loop(0, t_len, step, (hc[0], hc[1]))
    hc[0] = h_f
    hc[1] = h_b

    for d, o_ref in ((0, hf_ref), (1, hb_ref)):
        for hd in range(N_HEADS):
            for b in range(SCAN_B):
                o_ref[b, :, hd * HEAD:(hd + 1) * HEAD] = (
                    a_s[d, hd, pl.ds(b, t_len, stride=SCAN_B), :].astype(BF16))

    @pl.when(j == n_t - 1)
    def _():
        for d in range(2):
            for s in range(N_HEADS):
                last_ref[0, d, :, s * HEAD:(s + 1) * HEAD] = hc[d, s]


def _rglru_scan(xc, h0, wg, ba, bx, lam):
    n_b, seq_len, _ = xc.shape
    n_g = n_b // SCAN_B
    n_t = seq_len // SCAN_T
    blk = (SCAN_B, SCAN_T, D)
    return pl.pallas_call(
        _scan_body,
        grid=(n_g, n_t),
        in_specs=[pl.BlockSpec(blk, lambda g, j: (g, j, 0)),
                  pl.BlockSpec(blk, lambda g, j: (g, n_t - 1 - j, 0)),
                  pl.BlockSpec((1, 2, SCAN_B, D), lambda g, j: (g, 0, 0, 0)),
                  _const_spec((2, N_HEADS, HEAD, 2 * HEAD)),
                  _const_spec((2, D)),
                  _const_spec((2, D)),
                  _const_spec((2, D))],
        out_specs=[pl.BlockSpec(blk, lambda g, j: (g, j, 0)),
                   pl.BlockSpec(blk, lambda g, j: (g, n_t - 1 - j, 0)),
                   pl.BlockSpec((1, 2, SCAN_B, D), lambda g, j: (g, 0, 0, 0))],
        out_shape=[jax.ShapeDtypeStruct((n_b, seq_len, D), BF16),
                   jax.ShapeDtypeStruct((n_b, seq_len, D), BF16),
                   jax.ShapeDtypeStruct((n_g, 2, SCAN_B, D), F32)],
        scratch_shapes=[pltpu.VMEM((2, N_HEADS, SCAN_T * SCAN_B, HEAD), F32),
                        pltpu.VMEM((2, N_HEADS, SCAN_T * SCAN_B, HEAD), F32),
                        pltpu.VMEM((2, N_HEADS, SCAN_B, HEAD), F32)],
        compiler_params=_params(("parallel", "arbitrary")),
        name="rglru_scan",
    )(xc, xc, h0, wg, ba, bx, lam)


def _mix_body(x_ref, mod_ref, hf_ref, hb_ref, g1_ref, g2_ref, win_ref, cw_ref,
              wco_ref, wro_ref, wo_ref, wr_ref, br_ref,
              x1_ref, h2_ref, ids_ref, wts_ref, *, row_len):
    x = x_ref[...]
    tm = x.shape[0]
    h = _norm_mod(x, g1_ref[...], mod_ref[0, 1:2, :], mod_ref[0, 0:1, :]).astype(BF16)

    def proj(k):
        return _dot(h, win_ref[:, k * D:(k + 1) * D])

    cv = proj(1) * proj(2)
    pos = lax.broadcasted_iota(jnp.int32, (tm, 1), 0) % row_len
    conv = cv * cw_ref[1:2, :]
    conv = conv + jnp.where(pos >= 1, pltpu.roll(cv, 1, 0), 0.0) * cw_ref[0:1, :]
    conv = conv + jnp.where(pos <= row_len - 2, pltpu.roll(cv, tm - 1, 0), 0.0) * cw_ref[2:3, :]
    y_a = _dot((proj(0) * conv).astype(BF16), wco_ref[...])
    merged = _sigmoid(proj(5)) * y_a

    hs = hf_ref[...].astype(F32) + hb_ref[...].astype(F32)
    y_b = _dot((hs * jax.nn.gelu(proj(4))).astype(BF16), wro_ref[...])
    merged = merged + _sigmoid(proj(6)) * y_b

    mix = _dot(merged.astype(BF16), wo_ref[...])
    x1 = x + mod_ref[0, 2:3, :] * mix
    x1_ref[...] = x1
    h2 = _norm_mod(x1, g2_ref[...], mod_ref[0, 4:5, :], mod_ref[0, 3:4, :])
    h2_ref[...] = h2.astype(BF16)

    logits = jnp.dot(h2, wr_ref[...], preferred_element_type=F32,
                     precision=lax.Precision.HIGHEST) + br_ref[...]
    lt = logits.T
    row = lax.broadcasted_iota(jnp.int32, (EPG, tm), 0)
    lg = lt[0:EPG]
    mg = jnp.max(lg, axis=0, keepdims=True)
    p_grp = 1.0 / jnp.sum(jnp.exp(lg - mg), axis=0, keepdims=True)
    grp = jnp.min(jnp.where(lg == mg, row, EPG), axis=0, keepdims=True)
    le = lt[EPG * N_GROUPS:EPG * (N_GROUPS + 1)]
    for g in range(N_GROUPS - 2, -1, -1):
        le = jnp.where(grp == g, lt[EPG * (g + 1):EPG * (g + 2)], le)
    me = jnp.max(le, axis=0, keepdims=True)
    ee = jnp.exp(le - me)
    pe = ee / jnp.sum(ee, axis=0, keepdims=True)
    p1 = jnp.max(pe, axis=0, keepdims=True)
    i1 = jnp.min(jnp.where(pe == p1, row, EPG), axis=0, keepdims=True)
    pe2 = jnp.where(row == i1, -1.0, pe)
    p2 = jnp.max(pe2, axis=0, keepdims=True)
    i2 = jnp.min(jnp.where(pe2 == p2, row, EPG), axis=0, keepdims=True)
    den = p1 + p2
    ids_ref[...] = jnp.where(row == 0, grp * EPG + i1, jnp.where(row == 1, grp * EPG + i2, 0))
    wts_ref[...] = jnp.where(row == 0, p_grp * p1 / den,
                             jnp.where(row == 1, p_grp * p2 / den, 0.0))


def _mixer(x, mod, tiles_per_mod, hf, hb, g1, g2, w_in, cw, wco, wro, wo, wr, br, row_len):
    n = x.shape[0]
    tm = MIX_TM
    assert tm % row_len == 0 and n % tm == 0
    mod_map = lambda i: (i // tiles_per_mod, 0, 0)
    tok = lambda i: (i, 0)
    col = lambda i: (0, i)
    return pl.pallas_call(
        functools.partial(_mix_body, row_len=row_len),
        grid=(n // tm,),
        in_specs=[pl.BlockSpec((tm, D), tok),
                  pl.BlockSpec((1, 6, D), mod_map),
                  pl.BlockSpec((tm, D), tok),
                  pl.BlockSpec((tm, D), tok),
                  _const_spec((1, D)),
                  _const_spec((1, D)),
                  _const_spec(w_in.shape),
                  _const_spec((3, D)),
                  _const_spec((D, D)),
                  _const_spec((D, D)),
                  _const_spec((D, D)),
                  _const_spec((D, ROUTE_PAD)),
                  _const_spec((1, ROUTE_PAD))],
        out_specs=[pl.BlockSpec((tm, D), tok),
                   pl.BlockSpec((tm, D), tok),
                   pl.BlockSpec((EPG, tm), col),
                   pl.BlockSpec((EPG, tm), col)],
        out_shape=[jax.ShapeDtypeStruct((n, D), F32),
                   jax.ShapeDtypeStruct((n, D), BF16),
                   jax.ShapeDtypeStruct((EPG, n), jnp.int32),
                   jax.ShapeDtypeStruct((EPG, n), F32)],
        compiler_params=_params(("parallel",)),
        name="mixer",
    )(x, mod, hf, hb, g1, g2, w_in, cw, wco, wro, wo, wr, br)


def _expert_body(te_ref, nu_ref, xs_ref, sw_ref, w1_ref, w3_ref, w2_ref, o_ref,
                 w1b, w3b, w2b):
    i = pl.program_id(0)
    prev = te_ref[jnp.maximum(i - 1, 0)]

    @pl.when(jnp.logical_or(i == 0, te_ref[i] != prev))
    def _():
        w1b[...] = w1_ref[0].astype(BF16)
        w3b[...] = w3_ref[0].astype(BF16)
        w2b[...] = w2_ref[0].astype(BF16)

    @pl.when(i < nu_ref[0])
    def _():
        xs = xs_ref[...]
        a = _dot(xs, w1b[...])
        z = (a * _sigmoid(a)) * _dot(xs, w3b[...])
        o_ref[...] = _dot(z.astype(BF16), w2b[...]) * sw_ref[...]

    @pl.when(i >= nu_ref[0])
    def _():
        o_ref[...] = jnp.zeros_like(o_ref)


def _experts(tile_e, n_used, xs, slot_w, w1, w3, w2, tme):
    n_slots = xs.shape[0]
    grid_spec = pltpu.PrefetchScalarGridSpec(
        num_scalar_prefetch=2,
        grid=(n_slots // tme,),
        in_specs=[pl.BlockSpec((tme, D), lambda i, te, nu: (i, 0)),
                  pl.BlockSpec((tme, 1), lambda i, te, nu: (i, 0)),
                  pl.BlockSpec((1, D, D_EXPERT), lambda i, te, nu: (te[i], 0, 0)),
                  pl.BlockSpec((1, D, D_EXPERT), lambda i, te, nu: (te[i], 0, 0)),
                  pl.BlockSpec((1, D_EXPERT, D), lambda i, te, nu: (te[i], 0, 0))],
        out_specs=pl.BlockSpec((tme, D), lambda i, te, nu: (i, 0)),
        scratch_shapes=[pltpu.VMEM((D, D_EXPERT), BF16),
                        pltpu.VMEM((D, D_EXPERT), BF16),
                        pltpu.VMEM((D_EXPERT, D), BF16)],
    )
    return pl.pallas_call(
        _expert_body,
        grid_spec=grid_spec,
        out_shape=jax.ShapeDtypeStruct((n_slots, D), F32),
        compiler_params=_params(("arbitrary",)),
        name="experts",
    )(tile_e, n_used, xs, slot_w, w1, w3, w2)


def _final_body(x1_ref, mod_ref, y0_ref, y1_ref, gf_ref, o_ref):
    x2 = x1_ref[...] + mod_ref[0, 5:6, :] * (y0_ref[0] + y1_ref[0])
    ms = jnp.mean(x2 * x2, axis=-1, keepdims=True)
    o_ref[...] = x2 * lax.rsqrt(ms + EPS) * gf_ref[...]


def _final(x1, mod, tiles_per_mod, yg, g_final):
    n = x1.shape[0]
    tm = FIN_TM
    return pl.pallas_call(
        _final_body,
        grid=(n // tm,),
        in_specs=[pl.BlockSpec((tm, D), lambda i: (i, 0)),
                  pl.BlockSpec((1, 6, D), lambda i: (i // tiles_per_mod, 0, 0)),
                  pl.BlockSpec((1, tm, D), lambda i: (0, i, 0)),
                  pl.BlockSpec((1, tm, D), lambda i: (1, i, 0)),
                  _const_spec((1, D))],
        out_specs=pl.BlockSpec((tm, D), lambda i: (i, 0)),
        out_shape=jax.ShapeDtypeStruct((n, D), F32),
        compiler_params=_params(("parallel",)),
        name="final",
    )(x1, mod, yg, yg, g_final)


def _route_plan(ids, wts, n, tme):
    e = ids[:2].reshape(-1)
    w = wts[:2].reshape(-1)
    n_pairs = 2 * n
    tok = jnp.tile(jnp.arange(n, dtype=jnp.int32), 2)
    order = jnp.argsort(e, stable=True).astype(jnp.int32)
    e_sorted = e[order]
    counts = jnp.sum((e[:, None] == jnp.arange(N_EXPERTS, dtype=jnp.int32)[None, :]).astype(jnp.int32), axis=0)
    padded = ((counts + tme - 1) // tme) * tme
    pend = jnp.cumsum(padded)
    pstart = pend - padded
    cstart = jnp.cumsum(counts) - counts
    dest = pstart[e_sorted] + jnp.arange(n_pairs, dtype=jnp.int32) - cstart[e_sorted]
    n_slots = ((n_pairs + N_EXPERTS * (tme - 1)) // tme) * tme
    slot_tok = jnp.zeros((n_slots,), jnp.int32).at[dest].set(tok[order])
    slot_w = jnp.zeros((n_slots,), F32).at[dest].set(w[order])
    pos = jnp.zeros((n_pairs,), jnp.int32).at[order].set(dest)
    tile_start = jnp.arange(n_slots // tme, dtype=jnp.int32) * tme
    tile_e = jnp.minimum(jnp.searchsorted(pend, tile_start, side="right"), N_EXPERTS - 1).astype(jnp.int32)
    n_used = (pend[-1] // tme).astype(jnp.int32).reshape(1)
    return slot_tok, slot_w.reshape(n_slots, 1), pos, tile_e, n_used


def _group(x, mod, mod_per_seq, h0, p, row_len, tme):
    n_b, seq_len, _ = x.shape
    n = n_b * seq_len
    xt = x.reshape(n, D)
    xc = _xr_conv(xt, mod, mod_per_seq and seq_len == SEQ_TILE, p["g1"], p["w_xr"],
                  p["rnn_conv_w"], p["rnn_conv_b"], seq_len)
    hf, hb, last = _rglru_scan(xc.reshape(n_b, seq_len, D), h0, p["wg"], p["ba"], p["bx"], p["lam"])
    tiles_per_mod = (seq_len // MIX_TM) if mod_per_seq else (n // MIX_TM)
    x1, h2, ids, wts = _mixer(xt, mod, tiles_per_mod, hf.reshape(n, D), hb.reshape(n, D),
                              p["g1"], p["g2"], p["w_in"], p["conv_w"], p["wco"], p["wro"],
                              p["wo"], p["wr"], p["br"], row_len)
    slot_tok, slot_w, pos, tile_e, n_used = _route_plan(ids, wts, n, tme)
    xs = jnp.take(h2, slot_tok, axis=0)
    ys = _experts(tile_e, n_used, xs, slot_w, p["w1"], p["w3"], p["w2"], tme)
    yg = jnp.take(ys, pos, axis=0).reshape(2, n, D)
    tiles_per_mod_f = (seq_len // FIN_TM) if mod_per_seq else (n // FIN_TM)
    y = _final(x1, mod, tiles_per_mod_f, yg, p["g_final"])
    return y.reshape(n_b, seq_len, D), last


def kernel(x_prompt, x_sample, state_rnn, c, c_ctx, w_ada, b_ada, g_norm1, g_norm2, w_in, conv_w, w_conv_out, rnn_conv_w, rnn_conv_b, w_gate_a, b_gate_a, w_gate_x, b_gate_x, lam, w_rnn_out, w_o, w_router_group, b_router_group, w_router_expert, b_router_expert, w1, w3, w2, g_final):
    assert w_ada.shape[0] == 1, "single layer"
    n_pb, n_sb = x_prompt.shape[0], x_sample.shape[0]

    cond = jnp.zeros((16, D), F32).at[0].set(c_ctx).at[1:1 + n_sb].set(c)
    mod = _ada(cond, w_ada[0], b_ada[0]).reshape(16, 6, D)

    w_in_b = w_in[0].astype(BF16)
    wr = jnp.zeros((D, ROUTE_PAD), F32)
    wr = wr.at[:, :N_GROUPS].set(w_router_group[0]).at[:, EPG:EPG + N_EXPERTS].set(w_router_expert[0])
    br = jnp.zeros((1, ROUTE_PAD), F32).at[0, N_GROUPS:EPG].set(NEG_BIG)
    br = br.at[0, :N_GROUPS].set(b_router_group[0]).at[0, EPG:EPG + N_EXPERTS].set(b_router_expert[0])
    p = dict(
        g1=g_norm1, g2=g_norm2, w_in=w_in_b, w_xr=w_in_b[:, 3 * D:4 * D],
        conv_w=conv_w[0], rnn_conv_w=rnn_conv_w[0], rnn_conv_b=rnn_conv_b,
        wg=jnp.concatenate([w_gate_a[0], w_gate_x[0]], axis=-1).astype(BF16),
        ba=b_gate_a[0], bx=b_gate_x[0], lam=lam[0],
        wco=w_conv_out[0].astype(BF16), wro=w_rnn_out[0].astype(BF16), wo=w_o[0].astype(BF16),
        wr=wr, br=br, w1=w1[0], w3=w3[0], w2=w2[0], g_final=g_final.reshape(1, D),
    )

    h0_p = jnp.zeros((n_pb // SCAN_B, 2, SCAN_B, D), F32)
    y_prompt, last = _group(x_prompt, mod[0:1], False, h0_p, p, x_prompt.shape[1], 256)
    state_new = last.transpose(0, 2, 1, 3).reshape(n_pb, 1, 2, D)

    h0_s = state_rnn[:, 0].reshape(n_sb // SCAN_B, SCAN_B, 2, D).transpose(0, 2, 1, 3)
    y_sample, _ = _group(x_sample, mod[1:1 + n_sb], True, h0_s, p, GRID_W, 512)
    return (y_prompt, y_sample, state_new)
```

```python
import functools

import jax
import jax.numpy as jnp
from jax import lax
from jax.experimental import pallas as pl
from jax.experimental.pallas import tpu as pltpu
from jax.experimental.pallas import tpu_sc as plsc

D = 1024
N_HEADS = 8
HEAD = D // N_HEADS
GRID_W = 64
RG_C = 8.0
N_GROUPS = 4
EPG = 8
N_EXPERTS = N_GROUPS * EPG
D_EXPERT = 512
EPS = 1e-6
F32 = jnp.float32
BF16 = jnp.bfloat16

V7X_LANES = 128
V7X_SUBLANES = 8
V7X_VMEM_LIMIT_BYTES = 56 * 1024 * 1024
V7X_SC_CORES = 2
V7X_SC_SUBCORES = 16
V7X_SC_WORKERS = V7X_SC_CORES * V7X_SC_SUBCORES
SC_WINDOW = 128

SEQ_TILE = 2048
XR_CHUNK = 256
SCAN_T = 64
SCAN_B = V7X_SUBLANES
MIX_TM = 256
FIN_TM = 512
ROUTE_PAD = 128
NEG_BIG = -1e30


def _sigmoid(x):
    return 0.5 * jnp.tanh(0.5 * x) + 0.5


def _norm_mod(x, g, scale, shift):
    ms = jnp.mean(x * x, axis=-1, keepdims=True)
    return (x * lax.rsqrt(ms + EPS) * g) * (1.0 + scale) + shift


def _dot(a, b):
    return jnp.dot(a, b, preferred_element_type=F32)


def _params(sem, vmem=V7X_VMEM_LIMIT_BYTES):
    return pltpu.CompilerParams(dimension_semantics=sem, vmem_limit_bytes=vmem)


def _const_spec(shape):
    zeros = (0,) * len(shape)
    return pl.BlockSpec(shape, lambda *_: zeros, pipeline_mode=pl.Buffered(1))


def _ada_body(c_ref, w_ref, b_ref, o_ref):
    c = c_ref[...]
    s = (c * _sigmoid(c)).astype(BF16)
    o_ref[...] = _dot(s, w_ref[...].astype(BF16)) + b_ref[...]


def _ada(cond, w_ada, b_ada):
    rows = cond.shape[0]
    n_out = w_ada.shape[1]
    return pl.pallas_call(
        _ada_body,
        grid=(n_out // D,),
        in_specs=[pl.BlockSpec((rows, D), lambda i: (0, 0)),
                  pl.BlockSpec((D, D), lambda i: (0, i)),
                  pl.BlockSpec((1, D), lambda i: (0, i))],
        out_specs=pl.BlockSpec((rows, D), lambda i: (0, i)),
        out_shape=jax.ShapeDtypeStruct((rows, n_out), F32),
        compiler_params=_params(("parallel",)),
        name="ada",
    )(cond, w_ada, b_ada.reshape(1, n_out))


def _xr_body(x_ref, mod_ref, g_ref, w_ref, cw_ref, cb_ref, o_ref, xr_s, *, seq_len):
    n = x_ref.shape[0]
    halo = V7X_SUBLANES
    ch = XR_CHUNK
    xr_s[0:halo, :] = jnp.zeros((halo, D), F32)
    xr_s[halo + n:2 * halo + n, :] = jnp.zeros((halo, D), F32)
    for r0 in range(0, n, ch):
        x = x_ref[r0:r0 + ch, :]
        h = _norm_mod(x, g_ref[...], mod_ref[0, 1:2, :], mod_ref[0, 0:1, :]).astype(BF16)
        xr_s[halo + r0:halo + r0 + ch, :] = _dot(h, w_ref[...])
    win = ch + 2 * halo
    for r0 in range(0, n, ch):
        w = xr_s[r0:r0 + win, :]
        pos = (lax.broadcasted_iota(jnp.int32, (ch, 1), 0) + r0) % seq_len
        y = w[halo:halo + ch] * cw_ref[2:3, :] + cb_ref[...]
        y = y + jnp.where(pos >= 2, pltpu.roll(w, 2, 0)[halo:halo + ch], 0.0) * cw_ref[0:1, :]
        y = y + jnp.where(pos >= 1, pltpu.roll(w, 1, 0)[halo:halo + ch], 0.0) * cw_ref[1:2, :]
        y = y + jnp.where(pos <= seq_len - 2,
                          pltpu.roll(w, win - 1, 0)[halo:halo + ch], 0.0) * cw_ref[3:4, :]
        o_ref[r0:r0 + ch, :] = y


def _xr_conv(x, mod, mod_per_tile, g1, w_xr, cw, cb, seq_len):
    n = x.shape[0]
    assert SEQ_TILE % seq_len == 0 and n % SEQ_TILE == 0
    mod_map = (lambda i: (i, 0, 0)) if mod_per_tile else (lambda i: (0, 0, 0))
    return pl.pallas_call(
        functools.partial(_xr_body, seq_len=seq_len),
        grid=(n // SEQ_TILE,),
        in_specs=[pl.BlockSpec((SEQ_TILE, D), lambda i: (i, 0)),
                  pl.BlockSpec((1, 6, D), mod_map),
                  _const_spec((1, D)),
                  _const_spec((D, D)),
                  _const_spec((4, D)),
                  _const_spec((1, D))],
        out_specs=pl.BlockSpec((SEQ_TILE, D), lambda i: (i, 0)),
        out_shape=jax.ShapeDtypeStruct((n, D), F32),
        scratch_shapes=[pltpu.VMEM((SEQ_TILE + 2 * V7X_SUBLANES, D), F32)],
        compiler_params=_params(("parallel",)),
        name="xr_conv",
    )(x, mod, g1, w_xr, cw, cb)


def _scan_body(xf_ref, xb_ref, h0_ref, wg_ref, ba_ref, bx_ref, lam_ref,
               hf_ref, hb_ref, last_ref, a_s, u_s, hc):
    j = pl.program_id(1)
    n_t = pl.num_programs(1)
    t_len = xf_ref.shape[1]
    rows = SCAN_B * t_len

    @pl.when(j == 0)
    def _():
        for d in range(2):
            for s in range(N_HEADS):
                hc[d, s] = h0_ref[0, d, :, s * HEAD:(s + 1) * HEAD]

    for d, x_ref in ((0, xf_ref), (1, xb_ref)):
        z = -lam_ref[d:d + 1, :]
        sp = jnp.maximum(z, 0.0) + jnp.log(1.0 + jnp.exp(-jnp.abs(z)))
        for hd in range(N_HEADS):
            sl = slice(hd * HEAD, (hd + 1) * HEAD)
            xh = x_ref[:, :, sl].reshape(rows, HEAD)
            g = _dot(xh.astype(BF16), wg_ref[d, hd])
            r = _sigmoid(g[:, :HEAD] + ba_ref[d:d + 1, sl])
            i = _sigmoid(g[:, HEAD:] + bx_ref[d:d + 1, sl])
            a = jnp.exp((-RG_C) * r * sp[:, sl])
            u = jnp.sqrt(1.0 - a * a) * (i * xh)
            for b in range(SCAN_B):
                a_s[d, hd, pl.ds(b, t_len, stride=SCAN_B), :] = a[b * t_len:(b + 1) * t_len]
                u_s[d, hd, pl.ds(b, t_len, stride=SCAN_B), :] = u[b * t_len:(b + 1) * t_len]

    def step(t, carry):
        h_f, h_b = carry
        rf = pl.multiple_of(t * SCAN_B, SCAN_B)
        rb = pl.multiple_of((t_len - 1 - t) * SCAN_B, SCAN_B)
        h_f = a_s[0, :, pl.ds(rf, SCAN_B), :] * h_f + u_s[0, :, pl.ds(rf, SCAN_B), :]
        h_b = a_s[1, :, pl.ds(rb, SCAN_B), :] * h_b + u_s[1, :, pl.ds(rb, SCAN_B), :]
        a_s[0, :, pl.ds(rf, SCAN_B), :] = h_f
        a_s[1, :, pl.ds(rb, SCAN_B), :] = h_b
        return h_f, h_b

    h_f, h_b = lax.fori_loop(0, t_len, step, (hc[0], hc[1]))
    hc[0] = h_f
    hc[1] = h_b

    for d, o_ref in ((0, hf_ref), (1, hb_ref)):
        for hd in range(N_HEADS):
            for b in range(SCAN_B):
                o_ref[b, :, hd * HEAD:(hd + 1) * HEAD] = (
                    a_s[d, hd, pl.ds(b, t_len, stride=SCAN_B), :].astype(BF16))

    @pl.when(j == n_t - 1)
    def _():
        for d in range(2):
            for s in range(N_HEADS):
                last_ref[0, d, :, s * HEAD:(s + 1) * HEAD] = hc[d, s]


def _rglru_scan(xc, h0, wg, ba, bx, lam):
    n_b, seq_len, _ = xc.shape
    n_g = n_b // SCAN_B
    n_t = seq_len // SCAN_T
    blk = (SCAN_B, SCAN_T, D)
    return pl.pallas_call(
        _scan_body,
        grid=(n_g, n_t),
        in_specs=[pl.BlockSpec(blk, lambda g, j: (g, j, 0)),
                  pl.BlockSpec(blk, lambda g, j: (g, n_t - 1 - j, 0)),
                  pl.BlockSpec((1, 2, SCAN_B, D), lambda g, j: (g, 0, 0, 0)),
                  _const_spec((2, N_HEADS, HEAD, 2 * HEAD)),
                  _const_spec((2, D)),
                  _const_spec((2, D)),
                  _const_spec((2, D))],
        out_specs=[pl.BlockSpec(blk, lambda g, j: (g, j, 0)),
                   pl.BlockSpec(blk, lambda g, j: (g, n_t - 1 - j, 0)),
                   pl.BlockSpec((1, 2, SCAN_B, D), lambda g, j: (g, 0, 0, 0))],
        out_shape=[jax.ShapeDtypeStruct((n_b, seq_len, D), BF16),
                   jax.ShapeDtypeStruct((n_b, seq_len, D), BF16),
                   jax.ShapeDtypeStruct((n_g, 2, SCAN_B, D), F32)],
        scratch_shapes=[pltpu.VMEM((2, N_HEADS, SCAN_T * SCAN_B, HEAD), F32),
                        pltpu.VMEM((2, N_HEADS, SCAN_T * SCAN_B, HEAD), F32),
                        pltpu.VMEM((2, N_HEADS, SCAN_B, HEAD), F32)],
        compiler_params=_params(("parallel", "arbitrary")),
        name="rglru_scan",
    )(xc, xc, h0, wg, ba, bx, lam)


def _pack_halves(v):
    half = v.shape[1] // 2
    lo = lax.bitcast_convert_type(v[:, :half].astype(BF16).astype(F32), jnp.uint32)
    hi = lax.bitcast_convert_type(v[:, half:].astype(BF16).astype(F32), jnp.uint32)
    return lax.bitcast_convert_type((lo >> 16) | (hi & jnp.uint32(0xFFFF0000)), jnp.int32)


def _unpack_halves(p):
    u = lax.bitcast_convert_type(p, jnp.uint32)
    lo = lax.bitcast_convert_type(u << 16, F32)
    hi = lax.bitcast_convert_type(u & jnp.uint32(0xFFFF0000), F32)
    return lo, hi


def _mix_body(x_ref, mod_ref, hf_ref, hb_ref, g1_ref, g2_ref, win_ref, cw_ref,
              wco_ref, wro_ref, wo_ref, wr_ref, br_ref,
              x1_ref, h2_ref, route_ref, wts_ref, cnt_ref, seen, *, row_len):
    x = x_ref[...]
    tm = x.shape[0]
    h = _norm_mod(x, g1_ref[...], mod_ref[0, 1:2, :], mod_ref[0, 0:1, :]).astype(BF16)

    def proj(k):
        return _dot(h, win_ref[:, k * D:(k + 1) * D])

    cv = proj(1) * proj(2)
    pos = lax.broadcasted_iota(jnp.int32, (tm, 1), 0) % row_len
    conv = cv * cw_ref[1:2, :]
    conv = conv + jnp.where(pos >= 1, pltpu.roll(cv, 1, 0), 0.0) * cw_ref[0:1, :]
    conv = conv + jnp.where(pos <= row_len - 2, pltpu.roll(cv, tm - 1, 0), 0.0) * cw_ref[2:3, :]
    y_a = _dot((proj(0) * conv).astype(BF16), wco_ref[...])
    merged = _sigmoid(proj(5)) * y_a

    hs = hf_ref[...].astype(F32) + hb_ref[...].astype(F32)
    y_b = _dot((hs * jax.nn.gelu(proj(4))).astype(BF16), wro_ref[...])
    merged = merged + _sigmoid(proj(6)) * y_b

    mix = _dot(merged.astype(BF16), wo_ref[...])
    x1 = x + mod_ref[0, 2:3, :] * mix
    x1_ref[...] = x1
    h2 = _norm_mod(x1, g2_ref[...], mod_ref[0, 4:5, :], mod_ref[0, 3:4, :])
    h2_ref[...] = _pack_halves(h2)

    logits = jnp.dot(h2, wr_ref[...], preferred_element_type=F32,
                     precision=lax.Precision.HIGHEST) + br_ref[...]
    lt = logits.T
    row = lax.broadcasted_iota(jnp.int32, (EPG, tm), 0)
    lg = lt[0:EPG]
    mg = jnp.max(lg, axis=0, keepdims=True)
    p_grp = 1.0 / jnp.sum(jnp.exp(lg - mg), axis=0, keepdims=True)
    grp = jnp.min(jnp.where(lg == mg, row, EPG), axis=0, keepdims=True)
    le = lt[EPG * N_GROUPS:EPG * (N_GROUPS + 1)]
    for g in range(N_GROUPS - 2, -1, -1):
        le = jnp.where(grp == g, lt[EPG * (g + 1):EPG * (g + 2)], le)
    me = jnp.max(le, axis=0, keepdims=True)
    ee = jnp.exp(le - me)
    pe = ee / jnp.sum(ee, axis=0, keepdims=True)
    p1 = jnp.max(pe, axis=0, keepdims=True)
    i1 = jnp.min(jnp.where(pe == p1, row, EPG), axis=0, keepdims=True)
    pe2 = jnp.where(row == i1, -1.0, pe)
    p2 = jnp.max(pe2, axis=0, keepdims=True)
    i2 = jnp.min(jnp.where(pe2 == p2, row, EPG), axis=0, keepdims=True)
    den = p1 + p2
    e1 = grp * EPG + i1
    e2 = grp * EPG + i2

    @pl.when(pl.program_id(0) == 0)
    def _():
        seen[...] = jnp.zeros_like(seen)

    erow = lax.broadcasted_iota(jnp.int32, (N_EXPERTS, tm), 0)
    hit1 = erow == e1
    hit2 = erow == e2
    both = jnp.where(jnp.logical_or(hit1, hit2), 1.0, 0.0)
    tri = jnp.where(lax.broadcasted_iota(jnp.int32, (tm, tm), 0)
                    <= lax.broadcasted_iota(jnp.int32, (tm, tm), 1), 1.0, 0.0).astype(BF16)
    before = _dot(both.astype(BF16), tri) - both + seen[...]
    r1 = jnp.sum(jnp.where(hit1, before, 0.0), axis=0, keepdims=True).astype(jnp.int32)
    r2 = jnp.sum(jnp.where(hit2, before, 0.0), axis=0, keepdims=True).astype(jnp.int32)
    total = seen[...] + jnp.sum(both, axis=1, keepdims=True)
    seen[...] = total
    cnt_ref[...] = total[:, :V7X_LANES].astype(jnp.int32)

    route_ref[...] = jnp.where(row == 0, e1, jnp.where(row == 1, e2, jnp.where(
        row == 2, r1, jnp.where(row == 3, r2, 0))))
    w8 = jnp.where(row == 0, p_grp * p1 / den, jnp.where(row == 1, p_grp * p2 / den, 0.0))
    wts_ref[...] = jnp.concatenate([w8, jnp.zeros((V7X_LANES - EPG, tm), F32)], axis=0).T


def _mixer(x, mod, tiles_per_mod, hf, hb, g1, g2, w_in, cw, wco, wro, wo, wr, br, row_len):
    n = x.shape[0]
    tm = MIX_TM
    assert tm % row_len == 0 and n % tm == 0
    mod_map = lambda i: (i // tiles_per_mod, 0, 0)
    tok = lambda i: (i, 0)
    col = lambda i: (0, i)
    return pl.pallas_call(
        functools.partial(_mix_body, row_len=row_len),
        grid=(n // tm,),
        in_specs=[pl.BlockSpec((tm, D), tok),
                  pl.BlockSpec((1, 6, D), mod_map),
                  pl.BlockSpec((tm, D), tok),
                  pl.BlockSpec((tm, D), tok),
                  _const_spec((1, D)),
                  _const_spec((1, D)),
                  _const_spec(w_in.shape),
                  _const_spec((3, D)),
                  _const_spec((D, D)),
                  _const_spec((D, D)),
                  _const_spec((D, D)),
                  _const_spec((D, ROUTE_PAD)),
                  _const_spec((1, ROUTE_PAD))],
        out_specs=[pl.BlockSpec((tm, D), tok),
                   pl.BlockSpec((tm, D // 2), tok),
                   pl.BlockSpec((EPG, tm), col),
                   pl.BlockSpec((tm, V7X_LANES), tok),
                   pl.BlockSpec((N_EXPERTS, V7X_LANES), lambda i: (0, 0))],
        out_shape=[jax.ShapeDtypeStruct((n, D), F32),
                   jax.ShapeDtypeStruct((n, D // 2), jnp.int32),
                   jax.ShapeDtypeStruct((EPG, n), jnp.int32),
                   jax.ShapeDtypeStruct((n, V7X_LANES), F32),
                   jax.ShapeDtypeStruct((N_EXPERTS, V7X_LANES), jnp.int32)],
        scratch_shapes=[pltpu.VMEM((N_EXPERTS, tm), F32)],
        compiler_params=_params(("arbitrary",)),
        name="mixer",
    )(x, mod, hf, hb, g1, g2, w_in, cw, wco, wro, wo, wr, br)


def _sc_mesh():
    return plsc.VectorSubcoreMesh(core_axis_name="c", subcore_axis_name="s",
                                  num_cores=V7X_SC_CORES, num_subcores=V7X_SC_SUBCORES)


def _sc_worker_id():
    return lax.axis_index("s") * V7X_SC_CORES + lax.axis_index("c")


def _sc_dispatch(rows, dest, n_slots):
    n, width = rows.shape
    per_w = n // V7X_SC_WORKERS
    n_ch = per_w // SC_WINDOW
    assert n_ch * SC_WINDOW * V7X_SC_WORKERS == n
    idx = dest.reshape(2, V7X_SC_WORKERS, n_ch, SC_WINDOW).transpose(1, 0, 2, 3)

    def body(x_hbm, d_hbm, o_hbm, idx_v, rows_v):
        wid = _sc_worker_id()
        pltpu.sync_copy(d_hbm.at[wid], idx_v)
        for j in range(n_ch):
            pltpu.sync_copy(x_hbm.at[pl.ds(wid * per_w + j * SC_WINDOW, SC_WINDOW)], rows_v)
            pltpu.sync_copy(rows_v, o_hbm.at[idx_v.at[0, j]])
            pltpu.sync_copy(rows_v, o_hbm.at[idx_v.at[1, j]])

    return pl.kernel(
        body,
        out_type=jax.ShapeDtypeStruct((n_slots, width), jnp.int32),
        mesh=_sc_mesh(),
        scratch_types=[pltpu.VMEM((2, n_ch, SC_WINDOW), jnp.int32),
                       pltpu.VMEM((SC_WINDOW, width), jnp.int32)],
        name="sc_dispatch",
    )(rows, idx)


def _sc_collect(rows, dest):
    n = dest.shape[1]
    width = rows.shape[1]
    per_w = n // V7X_SC_WORKERS
    n_ch = per_w // SC_WINDOW
    assert n_ch * SC_WINDOW * V7X_SC_WORKERS == n
    idx = dest.reshape(2, V7X_SC_WORKERS, n_ch, SC_WINDOW).transpose(1, 0, 2, 3)

    def body(y_hbm, d_hbm, o_hbm, idx_v, rows_v):
        wid = _sc_worker_id()
        pltpu.sync_copy(d_hbm.at[wid], idx_v)
        for k in range(2):
            for j in range(n_ch):
                pltpu.sync_copy(y_hbm.at[idx_v.at[k, j]], rows_v)
                pltpu.sync_copy(rows_v, o_hbm.at[pl.ds(k * n + wid * per_w + j * SC_WINDOW, SC_WINDOW)])

    return pl.kernel(
        body,
        out_type=jax.ShapeDtypeStruct((2 * n, width), jnp.int32),
        mesh=_sc_mesh(),
        scratch_types=[pltpu.VMEM((2, n_ch, SC_WINDOW), jnp.int32),
                       pltpu.VMEM((SC_WINDOW, width), jnp.int32)],
        name="sc_collect",
    )(rows, idx)


def _expert_body(te_ref, nu_ref, xs_ref, w1_ref, w3_ref, w2_ref, o_ref, w1b, w3b, w2b):
    i = pl.program_id(0)
    prev = te_ref[jnp.maximum(i - 1, 0)]

    @pl.when(jnp.logical_or(i == 0, te_ref[i] != prev))
    def _():
        w1b[...] = w1_ref[0].astype(BF16)
        w3b[...] = w3_ref[0].astype(BF16)
        w2b[...] = w2_ref[0].astype(BF16)

    @pl.when(i < nu_ref[0])
    def _():
        lo, hi = _unpack_halves(xs_ref[...])
        lo = lo.astype(BF16)
        hi = hi.astype(BF16)
        half = D // 2
        a = _dot(lo, w1b[0:half, :]) + _dot(hi, w1b[half:D, :])
        b = _dot(lo, w3b[0:half, :]) + _dot(hi, w3b[half:D, :])
        z = (a * _sigmoid(a)) * b
        o_ref[...] = _pack_halves(_dot(z.astype(BF16), w2b[...]))

    @pl.when(i >= nu_ref[0])
    def _():
        o_ref[...] = jnp.zeros_like(o_ref)


def _experts(tile_e, n_used, xs, w1, w3, w2, tme):
    n_slots = xs.shape[0]
    grid_spec = pltpu.PrefetchScalarGridSpec(
        num_scalar_prefetch=2,
        grid=(n_slots // tme,),
        in_specs=[pl.BlockSpec((tme, D // 2), lambda i, te, nu: (i, 0)),
                  pl.BlockSpec((1, D, D_EXPERT), lambda i, te, nu: (te[i], 0, 0)),
                  pl.BlockSpec((1, D, D_EXPERT), lambda i, te, nu: (te[i], 0, 0)),
                  pl.BlockSpec((1, D_EXPERT, D), lambda i, te, nu: (te[i], 0, 0))],
        out_specs=pl.BlockSpec((tme, D // 2), lambda i, te, nu: (i, 0)),
        scratch_shapes=[pltpu.VMEM((D, D_EXPERT), BF16),
                        pltpu.VMEM((D, D_EXPERT), BF16),
                        pltpu.VMEM((D_EXPERT, D), BF16)],
    )
    return pl.pallas_call(
        _expert_body,
        grid_spec=grid_spec,
        out_shape=jax.ShapeDtypeStruct((n_slots, D // 2), jnp.int32),
        compiler_params=_params(("arbitrary",)),
        name="experts",
    )(tile_e, n_used, xs, w1, w3, w2)


def _final_body(x1_ref, mod_ref, y0_ref, y1_ref, wt_ref, gf_ref, o_ref):
    w0 = wt_ref[:, 0:1]
    w1 = wt_ref[:, 1:2]
    lo0, hi0 = _unpack_halves(y0_ref[...])
    lo1, hi1 = _unpack_halves(y1_ref[...])
    moe = jnp.concatenate([w0 * lo0 + w1 * lo1, w0 * hi0 + w1 * hi1], axis=1)
    x2 = x1_ref[...] + mod_ref[0, 5:6, :] * moe
    ms = jnp.mean(x2 * x2, axis=-1, keepdims=True)
    o_ref[...] = x2 * lax.rsqrt(ms + EPS) * gf_ref[...]


def _final(x1, mod, tiles_per_mod, yg, wts, g_final):
    n = x1.shape[0]
    tm = FIN_TM
    nt = n // tm
    return pl.pallas_call(
        _final_body,
        grid=(nt,),
        in_specs=[pl.BlockSpec((tm, D), lambda i: (i, 0)),
                  pl.BlockSpec((1, 6, D), lambda i: (i // tiles_per_mod, 0, 0)),
                  pl.BlockSpec((tm, D // 2), lambda i: (i, 0)),
                  pl.BlockSpec((tm, D // 2), lambda i: (i + nt, 0)),
                  pl.BlockSpec((tm, V7X_LANES), lambda i: (i, 0)),
                  _const_spec((1, D))],
        out_specs=pl.BlockSpec((tm, D), lambda i: (i, 0)),
        out_shape=jax.ShapeDtypeStruct((n, D), F32),
        compiler_params=_params(("parallel",)),
        name="final",
    )(x1, mod, yg, yg, wts, g_final)


def _slot_plan(route, cnt, n, tme):
    counts = cnt[:, 0]
    padded = ((counts + tme - 1) // tme) * tme
    pend = jnp.cumsum(padded)
    pstart = pend - padded
    onehot = route[0:2, :, None] == jnp.arange(N_EXPERTS, dtype=jnp.int32)[None, None, :]
    dest = jnp.sum(jnp.where(onehot, pstart[None, None, :], 0), axis=-1) + route[2:4]
    n_slots = ((2 * n + N_EXPERTS * (tme - 1)) // tme) * tme
    tile_start = jnp.arange(n_slots // tme, dtype=jnp.int32) * tme
    tile_e = jnp.sum((tile_start[:, None] >= pend[None, :]).astype(jnp.int32), axis=1)
    tile_e = jnp.minimum(tile_e, N_EXPERTS - 1)
    n_used = (pend[-1] // tme).astype(jnp.int32).reshape(1)
    return dest.astype(jnp.int32), tile_e, n_used, n_slots


def _group(x, mod, mod_per_seq, h0, p, row_len, tme):
    n_b, seq_len, _ = x.shape
    n = n_b * seq_len
    xt = x.reshape(n, D)
    xc = _xr_conv(xt, mod, mod_per_seq and seq_len == SEQ_TILE, p["g1"], p["w_xr"],
                  p["rnn_conv_w"], p["rnn_conv_b"], seq_len)
    hf, hb, last = _rglru_scan(xc.reshape(n_b, seq_len, D), h0, p["wg"], p["ba"], p["bx"], p["lam"])
    tiles_per_mod = (seq_len // MIX_TM) if mod_per_seq else (n // MIX_TM)
    x1, h2, route, wts, cnt = _mixer(xt, mod, tiles_per_mod, hf.reshape(n, D), hb.reshape(n, D),
                                     p["g1"], p["g2"], p["w_in"], p["conv_w"], p["wco"], p["wro"],
                                     p["wo"], p["wr"], p["br"], row_len)
    dest, tile_e, n_used, n_slots = _slot_plan(route, cnt, n, tme)
    xs = _sc_dispatch(h2, dest, n_slots)
    ys = _experts(tile_e, n_used, xs, p["w1"], p["w3"], p["w2"], tme)
    yg = _sc_collect(ys, dest)
    tiles_per_mod_f = (seq_len // FIN_TM) if mod_per_seq else (n // FIN_TM)
    y = _final(x1, mod, tiles_per_mod_f, yg, wts, p["g_final"])
    return y.reshape(n_b, seq_len, D), last


def kernel(x_prompt, x_sample, state_rnn, c, c_ctx, w_ada, b_ada, g_norm1, g_norm2, w_in, conv_w, w_conv_out, rnn_conv_w, rnn_conv_b, w_gate_a, b_gate_a, w_gate_x, b_gate_x, lam, w_rnn_out, w_o, w_router_group, b_router_group, w_router_expert, b_router_expert, w1, w3, w2, g_final):
    assert w_ada.shape[0] == 1, "single layer"
    n_pb, n_sb = x_prompt.shape[0], x_sample.shape[0]

    cond = jnp.zeros((16, D), F32).at[0].set(c_ctx).at[1:1 + n_sb].set(c)
    mod = _ada(cond, w_ada[0], b_ada[0]).reshape(16, 6, D)

    w_in_b = w_in[0].astype(BF16)
    wr = jnp.zeros((D, ROUTE_PAD), F32)
    wr = wr.at[:, :N_GROUPS].set(w_router_group[0]).at[:, EPG:EPG + N_EXPERTS].set(w_router_expert[0])
    br = jnp.zeros((1, ROUTE_PAD), F32).at[0, N_GROUPS:EPG].set(NEG_BIG)
    br = br.at[0, :N_GROUPS].set(b_router_group[0]).at[0, EPG:EPG + N_EXPERTS].set(b_router_expert[0])
    p = dict(
        g1=g_norm1, g2=g_norm2, w_in=w_in_b, w_xr=w_in_b[:, 3 * D:4 * D],
        conv_w=conv_w[0], rnn_conv_w=rnn_conv_w[0], rnn_conv_b=rnn_conv_b,
        wg=jnp.concatenate([w_gate_a[0], w_gate_x[0]], axis=-1).astype(BF16),
        ba=b_gate_a[0], bx=b_gate_x[0], lam=lam[0],
        wco=w_conv_out[0].astype(BF16), wro=w_rnn_out[0].astype(BF16), wo=w_o[0].astype(BF16),
        wr=wr, br=br, w1=w1[0], w3=w3[0], w2=w2[0], g_final=g_final.reshape(1, D),
    )

    h0_p = jnp.zeros((n_pb // SCAN_B, 2, SCAN_B, D), F32)
    y_prompt, last = _group(x_prompt, mod[0:1], False, h0_p, p, x_prompt.shape[1], 256)
    state_new = last.transpose(0, 2, 1, 3).reshape(n_pb, 1, 2, D)

    h0_s = state_rnn[:, 0].reshape(n_sb // SCAN_B, SCAN_B, 2, D).transpose(0, 2, 1, 3)
    y_sample, _ = _group(x_sample, mod[1:1 + n_sb], True, h0_s, p, GRID_W, 512)
    return (y_prompt, y_sample, state_new)
```

```python
import functools

import jax
import jax.numpy as jnp
from jax import lax
from jax.experimental import pallas as pl
from jax.experimental.pallas import tpu as pltpu
from jax.experimental.pallas import tpu_sc as plsc

D = 1024
N_HEADS = 8
HEAD = D // N_HEADS
GRID_W = 64
RG_C = 8.0
N_GROUPS = 4
EPG = 8
N_EXPERTS = N_GROUPS * EPG
D_EXPERT = 512
EPS = 1e-6
F32 = jnp.float32
BF16 = jnp.bfloat16

V7X_LANES = 128
V7X_SUBLANES = 8
V7X_VMEM_LIMIT_BYTES = 56 * 1024 * 1024
V7X_SC_CORES = 2
V7X_SC_SUBCORES = 16
V7X_SC_WORKERS = V7X_SC_CORES * V7X_SC_SUBCORES
SC_WINDOW = 128

SEQ_TILE = 2048
XR_CHUNK = 256
SCAN_T = 64
SCAN_B = V7X_SUBLANES
MIX_TM = 512
FIN_TM = 512
ROUTE_PAD = 128
NEG_BIG = -1e30


def _sigmoid(x):
    return 0.5 * jnp.tanh(0.5 * x) + 0.5


def _norm_mod(x, g, scale, shift):
    ms = jnp.mean(x * x, axis=-1, keepdims=True)
    return (x * lax.rsqrt(ms + EPS) * g) * (1.0 + scale) + shift


def _dot(a, b):
    return jnp.dot(a, b, preferred_element_type=F32)


def _params(sem, vmem=V7X_VMEM_LIMIT_BYTES):
    return pltpu.CompilerParams(dimension_semantics=sem, vmem_limit_bytes=vmem)


def _const_spec(shape):
    zeros = (0,) * len(shape)
    return pl.BlockSpec(shape, lambda *_: zeros, pipeline_mode=pl.Buffered(1))


def _ada_body(c_ref, w_ref, b_ref, o_ref):
    c = c_ref[...]
    s = (c * _sigmoid(c)).astype(BF16)
    o_ref[...] = _dot(s, w_ref[...].astype(BF16)) + b_ref[...]


def _ada(cond, w_ada, b_ada):
    rows = cond.shape[0]
    n_out = w_ada.shape[1]
    return pl.pallas_call(
        _ada_body,
        grid=(n_out // D,),
        in_specs=[pl.BlockSpec((rows, D), lambda i: (0, 0)),
                  pl.BlockSpec((D, D), lambda i: (0, i)),
                  pl.BlockSpec((1, D), lambda i: (0, i))],
        out_specs=pl.BlockSpec((rows, D), lambda i: (0, i)),
        out_shape=jax.ShapeDtypeStruct((rows, n_out), F32),
        compiler_params=_params(("parallel",)),
        name="ada",
    )(cond, w_ada, b_ada.reshape(1, n_out))


def _xr_body(x_ref, mod_ref, g_ref, w_ref, cw_ref, cb_ref, o_ref, xr_s, *, seq_len):
    n = x_ref.shape[0]
    halo = V7X_SUBLANES
    ch = XR_CHUNK
    xr_s[0:halo, :] = jnp.zeros((halo, D), F32)
    xr_s[halo + n:2 * halo + n, :] = jnp.zeros((halo, D), F32)
    for r0 in range(0, n, ch):
        x = x_ref[r0:r0 + ch, :]
        h = _norm_mod(x, g_ref[...], mod_ref[0, 1:2, :], mod_ref[0, 0:1, :]).astype(BF16)
        xr_s[halo + r0:halo + r0 + ch, :] = _dot(h, w_ref[...])
    win = ch + 2 * halo
    for r0 in range(0, n, ch):
        w = xr_s[r0:r0 + win, :]
        pos = (lax.broadcasted_iota(jnp.int32, (ch, 1), 0) + r0) % seq_len
        y = w[halo:halo + ch] * cw_ref[2:3, :] + cb_ref[...]
        y = y + jnp.where(pos >= 2, pltpu.roll(w, 2, 0)[halo:halo + ch], 0.0) * cw_ref[0:1, :]
        y = y + jnp.where(pos >= 1, pltpu.roll(w, 1, 0)[halo:halo + ch], 0.0) * cw_ref[1:2, :]
        y = y + jnp.where(pos <= seq_len - 2,
                          pltpu.roll(w, win - 1, 0)[halo:halo + ch], 0.0) * cw_ref[3:4, :]
        o_ref[r0:r0 + ch, :] = y


def _xr_conv(x, mod, mod_per_tile, g1, w_xr, cw, cb, seq_len):
    n = x.shape[0]
    assert SEQ_TILE % seq_len == 0 and n % SEQ_TILE == 0
    mod_map = (lambda i: (i, 0, 0)) if mod_per_tile else (lambda i: (0, 0, 0))
    return pl.pallas_call(
        functools.partial(_xr_body, seq_len=seq_len),
        grid=(n // SEQ_TILE,),
        in_specs=[pl.BlockSpec((SEQ_TILE, D), lambda i: (i, 0)),
                  pl.BlockSpec((1, 6, D), mod_map),
                  _const_spec((1, D)),
                  _const_spec((D, D)),
                  _const_spec((4, D)),
                  _const_spec((1, D))],
        out_specs=pl.BlockSpec((SEQ_TILE, D), lambda i: (i, 0)),
        out_shape=jax.ShapeDtypeStruct((n, D), F32),
        scratch_shapes=[pltpu.VMEM((SEQ_TILE + 2 * V7X_SUBLANES, D), F32)],
        compiler_params=_params(("parallel",)),
        name="xr_conv",
    )(x, mod, g1, w_xr, cw, cb)


def _scan_body(xf_ref, xb_ref, h0_ref, wg_ref, ba_ref, bx_ref, lam_ref,
               hf_ref, hb_ref, last_ref, a_s, u_s, hc):
    j = pl.program_id(1)
    n_t = pl.num_programs(1)
    t_len = xf_ref.shape[1]
    rows = SCAN_B * t_len

    @pl.when(j == 0)
    def _():
        for d in range(2):
            for s in range(N_HEADS):
                hc[d, s] = h0_ref[0, d, :, s * HEAD:(s + 1) * HEAD]

    for d, x_ref in ((0, xf_ref), (1, xb_ref)):
        z = -lam_ref[d:d + 1, :]
        sp = jnp.maximum(z, 0.0) + jnp.log(1.0 + jnp.exp(-jnp.abs(z)))
        for hd in range(N_HEADS):
            sl = slice(hd * HEAD, (hd + 1) * HEAD)
            xh = x_ref[:, :, sl].reshape(rows, HEAD)
            g = _dot(xh.astype(BF16), wg_ref[d, hd])
            r = _sigmoid(g[:, :HEAD] + ba_ref[d:d + 1, sl])
            i = _sigmoid(g[:, HEAD:] + bx_ref[d:d + 1, sl])
            a = jnp.exp((-RG_C) * r * sp[:, sl])
            u = jnp.sqrt(1.0 - a * a) * (i * xh)
            for b in range(SCAN_B):
                a_s[d, hd, pl.ds(b, t_len, stride=SCAN_B), :] = a[b * t_len:(b + 1) * t_len]
                u_s[d, hd, pl.ds(b, t_len, stride=SCAN_B), :] = u[b * t_len:(b + 1) * t_len]

    def step(t, carry):
        h_f, h_b = carry
        rf = pl.multiple_of(t * SCAN_B, SCAN_B)
        rb = pl.multiple_of((t_len - 1 - t) * SCAN_B, SCAN_B)
        h_f = a_s[0, :, pl.ds(rf, SCAN_B), :] * h_f + u_s[0, :, pl.ds(rf, SCAN_B), :]
        h_b = a_s[1, :, pl.ds(rb, SCAN_B), :] * h_b + u_s[1, :, pl.ds(rb, SCAN_B), :]
        a_s[0, :, pl.ds(rf, SCAN_B), :] = h_f
        a_s[1, :, pl.ds(rb, SCAN_B), :] = h_b
        return h_f, h_b

    h_f, h_b = lax.fori_loop(0, t_len, step, (hc[0], hc[1]))
    hc[0] = h_f
    hc[1] = h_b

    for d, o_ref in ((0, hf_ref), (1, hb_ref)):
        for hd in range(N_HEADS):
            for b in range(SCAN_B):
                o_ref[b, :, hd * HEAD:(hd + 1) * HEAD] = (
                    a_s[d, hd, pl.ds(b, t_len, stride=SCAN_B), :].astype(BF16))

    @pl.when(j == n_t - 1)
    def _():
        for d in range(2):
            for s in range(N_HEADS):
                last_ref[0, d, :, s * HEAD:(s + 1) * HEAD] = hc[d, s]


def _rglru_scan(xc, h0, wg, ba, bx, lam):
    n_b, seq_len, _ = xc.shape
    n_g = n_b // SCAN_B
    n_t = seq_len // SCAN_T
    blk = (SCAN_B, SCAN_T, D)
    return pl.pallas_call(
        _scan_body,
        grid=(n_g, n_t),
        in_specs=[pl.BlockSpec(blk, lambda g, j: (g, j, 0)),
                  pl.BlockSpec(blk, lambda g, j: (g, n_t - 1 - j, 0)),
                  pl.BlockSpec((1, 2, SCAN_B, D), lambda g, j: (g, 0, 0, 0)),
                  _const_spec((2, N_HEADS, HEAD, 2 * HEAD)),
                  _const_spec((2, D)),
                  _const_spec((2, D)),
                  _const_spec((2, D))],
        out_specs=[pl.BlockSpec(blk, lambda g, j: (g, j, 0)),
                   pl.BlockSpec(blk, lambda g, j: (g, n_t - 1 - j, 0)),
                   pl.BlockSpec((1, 2, SCAN_B, D), lambda g, j: (g, 0, 0, 0))],
        out_shape=[jax.ShapeDtypeStruct((n_b, seq_len, D), BF16),
                   jax.ShapeDtypeStruct((n_b, seq_len, D), BF16),
                   jax.ShapeDtypeStruct((n_g, 2, SCAN_B, D), F32)],
        scratch_shapes=[pltpu.VMEM((2, N_HEADS, SCAN_T * SCAN_B, HEAD), F32),
                        pltpu.VMEM((2, N_HEADS, SCAN_T * SCAN_B, HEAD), F32),
                        pltpu.VMEM((2, N_HEADS, SCAN_B, HEAD), F32)],
        compiler_params=_params(("parallel", "arbitrary")),
        name="rglru_scan",
    )(xc, xc, h0, wg, ba, bx, lam)


def _pack_halves(v):
    half = v.shape[1] // 2
    lo = lax.bitcast_convert_type(v[:, :half].astype(BF16).astype(F32), jnp.uint32)
    hi = lax.bitcast_convert_type(v[:, half:].astype(BF16).astype(F32), jnp.uint32)
    return lax.bitcast_convert_type((lo >> 16) | (hi & jnp.uint32(0xFFFF0000)), jnp.int32)


def _unpack_halves(p):
    u = lax.bitcast_convert_type(p, jnp.uint32)
    lo = lax.bitcast_convert_type(u << 16, F32)
    hi = lax.bitcast_convert_type(u & jnp.uint32(0xFFFF0000), F32)
    return lo, hi


def _mix_body(x_ref, mod_ref, hf_ref, hb_ref, g1_ref, g2_ref, win_ref, cw_ref,
              wco_ref, wro_ref, wo_ref, wr_ref, br_ref,
              x1_ref, h2_ref, route_ref, wts_ref, cnt_ref, seen, *, row_len):
    x = x_ref[...]
    tm = x.shape[0]
    h = _norm_mod(x, g1_ref[...], mod_ref[0, 1:2, :], mod_ref[0, 0:1, :]).astype(BF16)

    def proj(k):
        return _dot(h, win_ref[:, k * D:(k + 1) * D])

    cv = proj(1) * proj(2)
    pos = lax.broadcasted_iota(jnp.int32, (tm, 1), 0) % row_len
    conv = cv * cw_ref[1:2, :]
    conv = conv + jnp.where(pos >= 1, pltpu.roll(cv, 1, 0), 0.0) * cw_ref[0:1, :]
    conv = conv + jnp.where(pos <= row_len - 2, pltpu.roll(cv, tm - 1, 0), 0.0) * cw_ref[2:3, :]
    y_a = _dot((proj(0) * conv).astype(BF16), wco_ref[...])
    merged = _sigmoid(proj(5)) * y_a

    hs = hf_ref[...].astype(F32) + hb_ref[...].astype(F32)
    y_b = _dot((hs * jax.nn.gelu(proj(4))).astype(BF16), wro_ref[...])
    merged = merged + _sigmoid(proj(6)) * y_b

    mix = _dot(merged.astype(BF16), wo_ref[...])
    x1 = x + mod_ref[0, 2:3, :] * mix
    x1_ref[...] = x1
    h2 = _norm_mod(x1, g2_ref[...], mod_ref[0, 4:5, :], mod_ref[0, 3:4, :])
    h2_ref[...] = _pack_halves(h2)

    h2_hi = h2.astype(BF16)
    h2_lo = (h2 - h2_hi.astype(F32)).astype(BF16)
    big = _dot(h2_hi, wr_ref[...])
    logits = (big[:, :ROUTE_PAD] + big[:, ROUTE_PAD:]
              + _dot(h2_lo, wr_ref[:, :ROUTE_PAD]) + br_ref[...])
    lt = logits.T
    row = lax.broadcasted_iota(jnp.int32, (EPG, tm), 0)
    lg = lt[0:EPG]
    mg = jnp.max(lg, axis=0, keepdims=True)
    p_grp = 1.0 / jnp.sum(jnp.exp(lg - mg), axis=0, keepdims=True)
    grp = jnp.min(jnp.where(lg == mg, row, EPG), axis=0, keepdims=True)
    le = lt[EPG * N_GROUPS:EPG * (N_GROUPS + 1)]
    for g in range(N_GROUPS - 2, -1, -1):
        le = jnp.where(grp == g, lt[EPG * (g + 1):EPG * (g + 2)], le)
    me = jnp.max(le, axis=0, keepdims=True)
    ee = jnp.exp(le - me)
    pe = ee / jnp.sum(ee, axis=0, keepdims=True)
    p1 = jnp.max(pe, axis=0, keepdims=True)
    i1 = jnp.min(jnp.where(pe == p1, row, EPG), axis=0, keepdims=True)
    pe2 = jnp.where(row == i1, -1.0, pe)
    p2 = jnp.max(pe2, axis=0, keepdims=True)
    i2 = jnp.min(jnp.where(pe2 == p2, row, EPG), axis=0, keepdims=True)
    den = p1 + p2
    e1 = grp * EPG + i1
    e2 = grp * EPG + i2

    @pl.when(pl.program_id(0) == 0)
    def _():
        seen[...] = jnp.zeros_like(seen)

    erow = lax.broadcasted_iota(jnp.int32, (N_EXPERTS, tm), 0)
    hit1 = erow == e1
    hit2 = erow == e2
    both = jnp.where(jnp.logical_or(hit1, hit2), 1.0, 0.0)
    tri = jnp.where(lax.broadcasted_iota(jnp.int32, (tm, tm), 0)
                    <= lax.broadcasted_iota(jnp.int32, (tm, tm), 1), 1.0, 0.0).astype(BF16)
    before = _dot(both.astype(BF16), tri) - both + seen[...]
    r1 = jnp.sum(jnp.where(hit1, before, 0.0), axis=0, keepdims=True).astype(jnp.int32)
    r2 = jnp.sum(jnp.where(hit2, before, 0.0), axis=0, keepdims=True).astype(jnp.int32)
    total = seen[...] + jnp.sum(both, axis=1, keepdims=True)
    seen[...] = total
    cnt_ref[...] = total[:, :V7X_LANES].astype(jnp.int32)

    route_ref[...] = jnp.where(row == 0, e1, jnp.where(row == 1, e2, jnp.where(
        row == 2, r1, jnp.where(row == 3, r2, 0))))
    w8 = jnp.where(row == 0, p_grp * p1 / den, jnp.where(row == 1, p_grp * p2 / den, 0.0))
    wts_ref[...] = jnp.concatenate([w8, jnp.zeros((V7X_LANES - EPG, tm), F32)], axis=0).T


def _mixer(x, mod, tiles_per_mod, hf, hb, g1, g2, w_in, cw, wco, wro, wo, wr, br, row_len):
    n = x.shape[0]
    tm = MIX_TM
    assert tm % row_len == 0 and n % tm == 0
    mod_map = lambda i: (i // tiles_per_mod, 0, 0)
    tok = lambda i: (i, 0)
    col = lambda i: (0, i)
    return pl.pallas_call(
        functools.partial(_mix_body, row_len=row_len),
        grid=(n // tm,),
        in_specs=[pl.BlockSpec((tm, D), tok),
                  pl.BlockSpec((1, 6, D), mod_map),
                  pl.BlockSpec((tm, D), tok),
                  pl.BlockSpec((tm, D), tok),
                  _const_spec((1, D)),
                  _const_spec((1, D)),
                  _const_spec(w_in.shape),
                  _const_spec((3, D)),
                  _const_spec((D, D)),
                  _const_spec((D, D)),
                  _const_spec((D, D)),
                  _const_spec((D, 2 * ROUTE_PAD)),
                  _const_spec((1, ROUTE_PAD))],
        out_specs=[pl.BlockSpec((tm, D), tok),
                   pl.BlockSpec((tm, D // 2), tok),
                   pl.BlockSpec((EPG, tm), col),
                   pl.BlockSpec((tm, V7X_LANES), tok),
                   pl.BlockSpec((N_EXPERTS, V7X_LANES), lambda i: (0, 0))],
        out_shape=[jax.ShapeDtypeStruct((n, D), F32),
                   jax.ShapeDtypeStruct((n, D // 2), jnp.int32),
                   jax.ShapeDtypeStruct((EPG, n), jnp.int32),
                   jax.ShapeDtypeStruct((n, V7X_LANES), F32),
                   jax.ShapeDtypeStruct((N_EXPERTS, V7X_LANES), jnp.int32)],
        scratch_shapes=[pltpu.VMEM((N_EXPERTS, tm), F32)],
        compiler_params=_params(("arbitrary",)),
        name="mixer",
    )(x, mod, hf, hb, g1, g2, w_in, cw, wco, wro, wo, wr, br)


def _sc_mesh():
    return plsc.VectorSubcoreMesh(core_axis_name="c", subcore_axis_name="s",
                                  num_cores=V7X_SC_CORES, num_subcores=V7X_SC_SUBCORES)


def _sc_worker_id():
    return lax.axis_index("s") * V7X_SC_CORES + lax.axis_index("c")


def _sc_dispatch(rows, dest, n_slots):
    n, width = rows.shape
    per_w = n // V7X_SC_WORKERS
    n_ch = per_w // SC_WINDOW
    assert n_ch * SC_WINDOW * V7X_SC_WORKERS == n
    idx = dest.reshape(2, V7X_SC_WORKERS, n_ch, SC_WINDOW).transpose(1, 0, 2, 3)

    def body(x_hbm, d_hbm, o_hbm, idx_v, rows_v):
        wid = _sc_worker_id()
        pltpu.sync_copy(d_hbm.at[wid], idx_v)
        for j in range(n_ch):
            pltpu.sync_copy(x_hbm.at[pl.ds(wid * per_w + j * SC_WINDOW, SC_WINDOW)], rows_v)
            pltpu.sync_copy(rows_v, o_hbm.at[idx_v.at[0, j]])
            pltpu.sync_copy(rows_v, o_hbm.at[idx_v.at[1, j]])

    return pl.kernel(
        body,
        out_type=jax.ShapeDtypeStruct((n_slots, width), jnp.int32),
        mesh=_sc_mesh(),
        scratch_types=[pltpu.VMEM((2, n_ch, SC_WINDOW), jnp.int32),
                       pltpu.VMEM((SC_WINDOW, width), jnp.int32)],
        name="sc_dispatch",
    )(rows, idx)


def _sc_collect(rows, dest):
    n = dest.shape[1]
    width = rows.shape[1]
    per_w = n // V7X_SC_WORKERS
    n_ch = per_w // SC_WINDOW
    assert n_ch * SC_WINDOW * V7X_SC_WORKERS == n
    idx = dest.reshape(2, V7X_SC_WORKERS, n_ch, SC_WINDOW).transpose(1, 0, 2, 3)

    def body(y_hbm, d_hbm, o_hbm, idx_v, rows_v):
        wid = _sc_worker_id()
        pltpu.sync_copy(d_hbm.at[wid], idx_v)
        for k in range(2):
            for j in range(n_ch):
                pltpu.sync_copy(y_hbm.at[idx_v.at[k, j]], rows_v)
                pltpu.sync_copy(rows_v, o_hbm.at[pl.ds(k * n + wid * per_w + j * SC_WINDOW, SC_WINDOW)])

    return pl.kernel(
        body,
        out_type=jax.ShapeDtypeStruct((2 * n, width), jnp.int32),
        mesh=_sc_mesh(),
        scratch_types=[pltpu.VMEM((2, n_ch, SC_WINDOW), jnp.int32),
                       pltpu.VMEM((SC_WINDOW, width), jnp.int32)],
        name="sc_collect",
    )(rows, idx)


def _expert_body(te_ref, nu_ref, xs_ref, w1_ref, w3_ref, w2_ref, o_ref, w1b, w3b, w2b):
    i = pl.program_id(0)
    prev = te_ref[jnp.maximum(i - 1, 0)]

    @pl.when(jnp.logical_or(i == 0, te_ref[i] != prev))
    def _():
        w1b[...] = w1_ref[0].astype(BF16)
        w3b[...] = w3_ref[0].astype(BF16)
        w2b[...] = w2_ref[0].astype(BF16)

    @pl.when(i < nu_ref[0])
    def _():
        lo, hi = _unpack_halves(xs_ref[...])
        lo = lo.astype(BF16)
        hi = hi.astype(BF16)
        half = D // 2
        a = _dot(lo, w1b[0:half, :]) + _dot(hi, w1b[half:D, :])
        b = _dot(lo, w3b[0:half, :]) + _dot(hi, w3b[half:D, :])
        z = (a * _sigmoid(a)) * b
        o_ref[...] = _pack_halves(_dot(z.astype(BF16), w2b[...]))

    @pl.when(i >= nu_ref[0])
    def _():
        o_ref[...] = jnp.zeros_like(o_ref)


def _experts(tile_e, n_used, xs, w1, w3, w2, tme):
    n_slots = xs.shape[0]
    grid_spec = pltpu.PrefetchScalarGridSpec(
        num_scalar_prefetch=2,
        grid=(n_slots // tme,),
        in_specs=[pl.BlockSpec((tme, D // 2), lambda i, te, nu: (i, 0)),
                  pl.BlockSpec((1, D, D_EXPERT), lambda i, te, nu: (te[i], 0, 0)),
                  pl.BlockSpec((1, D, D_EXPERT), lambda i, te, nu: (te[i], 0, 0)),
                  pl.BlockSpec((1, D_EXPERT, D), lambda i, te, nu: (te[i], 0, 0))],
        out_specs=pl.BlockSpec((tme, D // 2), lambda i, te, nu: (i, 0)),
        scratch_shapes=[pltpu.VMEM((D, D_EXPERT), BF16),
                        pltpu.VMEM((D, D_EXPERT), BF16),
                        pltpu.VMEM((D_EXPERT, D), BF16)],
    )
    return pl.pallas_call(
        _expert_body,
        grid_spec=grid_spec,
        out_shape=jax.ShapeDtypeStruct((n_slots, D // 2), jnp.int32),
        compiler_params=_params(("arbitrary",)),
        name="experts",
    )(tile_e, n_used, xs, w1, w3, w2)


def _final_body(x1_ref, mod_ref, y0_ref, y1_ref, wt_ref, gf_ref, o_ref):
    w0 = wt_ref[:, 0:1]
    w1 = wt_ref[:, 1:2]
    lo0, hi0 = _unpack_halves(y0_ref[...])
    lo1, hi1 = _unpack_halves(y1_ref[...])
    moe = jnp.concatenate([w0 * lo0 + w1 * lo1, w0 * hi0 + w1 * hi1], axis=1)
    x2 = x1_ref[...] + mod_ref[0, 5:6, :] * moe
    ms = jnp.mean(x2 * x2, axis=-1, keepdims=True)
    o_ref[...] = x2 * lax.rsqrt(ms + EPS) * gf_ref[...]


def _final(x1, mod, tiles_per_mod, yg, wts, g_final):
    n = x1.shape[0]
    tm = FIN_TM
    nt = n // tm
    return pl.pallas_call(
        _final_body,
        grid=(nt,),
        in_specs=[pl.BlockSpec((tm, D), lambda i: (i, 0)),
                  pl.BlockSpec((1, 6, D), lambda i: (i // tiles_per_mod, 0, 0)),
                  pl.BlockSpec((tm, D // 2), lambda i: (i, 0)),
                  pl.BlockSpec((tm, D // 2), lambda i: (i + nt, 0)),
                  pl.BlockSpec((tm, V7X_LANES), lambda i: (i, 0)),
                  _const_spec((1, D))],
        out_specs=pl.BlockSpec((tm, D), lambda i: (i, 0)),
        out_shape=jax.ShapeDtypeStruct((n, D), F32),
        compiler_params=_params(("parallel",)),
        name="final",
    )(x1, mod, yg, yg, wts, g_final)


def _slot_plan(route, cnt, n, tme):
    counts = cnt[:, 0]
    padded = ((counts + tme - 1) // tme) * tme
    pend = jnp.cumsum(padded)
    pstart = pend - padded
    onehot = route[0:2, :, None] == jnp.arange(N_EXPERTS, dtype=jnp.int32)[None, None, :]
    dest = jnp.sum(jnp.where(onehot, pstart[None, None, :], 0), axis=-1) + route[2:4]
    n_slots = ((2 * n + N_EXPERTS * (tme - 1)) // tme) * tme
    tile_start = jnp.arange(n_slots // tme, dtype=jnp.int32) * tme
    tile_e = jnp.sum((tile_start[:, None] >= pend[None, :]).astype(jnp.int32), axis=1)
    tile_e = jnp.minimum(tile_e, N_EXPERTS - 1)
    n_used = (pend[-1] // tme).astype(jnp.int32).reshape(1)
    return dest.astype(jnp.int32), tile_e, n_used, n_slots


def _group(x, mod, mod_per_seq, h0, p, row_len, tme):
    n_b, seq_len, _ = x.shape
    n = n_b * seq_len
    xt = x.reshape(n, D)
    xc = _xr_conv(xt, mod, mod_per_seq and seq_len == SEQ_TILE, p["g1"], p["w_xr"],
                  p["rnn_conv_w"], p["rnn_conv_b"], seq_len)
    hf, hb, last = _rglru_scan(xc.reshape(n_b, seq_len, D), h0, p["wg"], p["ba"], p["bx"], p["lam"])
    tiles_per_mod = (seq_len // MIX_TM) if mod_per_seq else (n // MIX_TM)
    x1, h2, route, wts, cnt = _mixer(xt, mod, tiles_per_mod, hf.reshape(n, D), hb.reshape(n, D),
                                     p["g1"], p["g2"], p["w_in"], p["conv_w"], p["wco"], p["wro"],
                                     p["wo"], p["wr"], p["br"], row_len)
    dest, tile_e, n_used, n_slots = _slot_plan(route, cnt, n, tme)
    xs = _sc_dispatch(h2, dest, n_slots)
    ys = _experts(tile_e, n_used, xs, p["w1"], p["w3"], p["w2"], tme)
    yg = _sc_collect(ys, dest)
    tiles_per_mod_f = (seq_len // FIN_TM) if mod_per_seq else (n // FIN_TM)
    y = _final(x1, mod, tiles_per_mod_f, yg, wts, p["g_final"])
    return y.reshape(n_b, seq_len, D), last


def kernel(x_prompt, x_sample, state_rnn, c, c_ctx, w_ada, b_ada, g_norm1, g_norm2, w_in, conv_w, w_conv_out, rnn_conv_w, rnn_conv_b, w_gate_a, b_gate_a, w_gate_x, b_gate_x, lam, w_rnn_out, w_o, w_router_group, b_router_group, w_router_expert, b_router_expert, w1, w3, w2, g_final):
    assert w_ada.shape[0] == 1, "single layer"
    n_pb, n_sb = x_prompt.shape[0], x_sample.shape[0]

    cond = jnp.zeros((16, D), F32).at[0].set(c_ctx).at[1:1 + n_sb].set(c)
    mod = _ada(cond, w_ada[0], b_ada[0]).reshape(16, 6, D)

    w_in_b = w_in[0].astype(BF16)
    wr = jnp.zeros((D, ROUTE_PAD), F32)
    wr = wr.at[:, :N_GROUPS].set(w_router_group[0]).at[:, EPG:EPG + N_EXPERTS].set(w_router_expert[0])
    br = jnp.zeros((1, ROUTE_PAD), F32).at[0, N_GROUPS:EPG].set(NEG_BIG)
    br = br.at[0, :N_GROUPS].set(b_router_group[0]).at[0, EPG:EPG + N_EXPERTS].set(b_router_expert[0])
    wr_hi = wr.astype(BF16)
    p = dict(
        g1=g_norm1, g2=g_norm2, w_in=w_in_b, w_xr=w_in_b[:, 3 * D:4 * D],
        conv_w=conv_w[0], rnn_conv_w=rnn_conv_w[0], rnn_conv_b=rnn_conv_b,
        wg=jnp.concatenate([w_gate_a[0], w_gate_x[0]], axis=-1).astype(BF16),
        ba=b_gate_a[0], bx=b_gate_x[0], lam=lam[0],
        wco=w_conv_out[0].astype(BF16), wro=w_rnn_out[0].astype(BF16), wo=w_o[0].astype(BF16),
        wr=jnp.concatenate([wr_hi, (wr - wr_hi.astype(F32)).astype(BF16)], axis=1), br=br, w1=w1[0], w3=w3[0], w2=w2[0], g_final=g_final.reshape(1, D),
    )

    h0_p = jnp.zeros((n_pb // SCAN_B, 2, SCAN_B, D), F32)
    y_prompt, last = _group(x_prompt, mod[0:1], False, h0_p, p, x_prompt.shape[1], 256)
    state_new = last.transpose(0, 2, 1, 3).reshape(n_pb, 1, 2, D)

    h0_s = state_rnn[:, 0].reshape(n_sb // SCAN_B, SCAN_B, 2, D).transpose(0, 2, 1, 3)
    y_sample, _ = _group(x_sample, mod[1:1 + n_sb], True, h0_s, p, GRID_W, 512)
    return (y_prompt, y_sample, state_new)
```

```python
import functools

import jax
import jax.numpy as jnp
from jax import lax
from jax.experimental import pallas as pl
from jax.experimental.pallas import tpu as pltpu
from jax.experimental.pallas import tpu_sc as plsc

D = 1024
N_HEADS = 8
HEAD = D // N_HEADS
GRID_W = 64
RG_C = 8.0
N_GROUPS = 4
EPG = 8
N_EXPERTS = N_GROUPS * EPG
D_EXPERT = 512
EPS = 1e-6
F32 = jnp.float32
BF16 = jnp.bfloat16

V7X_LANES = 128
V7X_SUBLANES = 8
V7X_VMEM_LIMIT_BYTES = 56 * 1024 * 1024
V7X_SC_CORES = 2
V7X_SC_SUBCORES = 16
V7X_SC_WORKERS = V7X_SC_CORES * V7X_SC_SUBCORES
SC_WINDOW = 128

SEQ_TILE = 2048
XR_CHUNK = 256
SCAN_T = 64
SCAN_B = V7X_SUBLANES
SCAN_UNROLL = 8
LOG2_E = 1.4426950408889634
TINY = 1e-30
MIX_TM = 512
FIN_TM = 512
ROUTE_PAD = 128
NEG_BIG = -1e30


def _sigmoid(x):
    return 0.5 * jnp.tanh(0.5 * x) + 0.5


def _norm_mod(x, g, scale, shift):
    ms = jnp.mean(x * x, axis=-1, keepdims=True)
    return (x * lax.rsqrt(ms + EPS) * g) * (1.0 + scale) + shift


def _dot(a, b):
    return jnp.dot(a, b, preferred_element_type=F32)


def _params(sem, vmem=V7X_VMEM_LIMIT_BYTES):
    return pltpu.CompilerParams(dimension_semantics=sem, vmem_limit_bytes=vmem)


def _const_spec(shape):
    zeros = (0,) * len(shape)
    return pl.BlockSpec(shape, lambda *_: zeros, pipeline_mode=pl.Buffered(1))


def _ada_body(c_ref, w_ref, b_ref, o_ref):
    c = c_ref[...]
    s = (c * _sigmoid(c)).astype(BF16)
    o_ref[...] = _dot(s, w_ref[...].astype(BF16)) + b_ref[...]


def _ada(cond, w_ada, b_ada):
    rows = cond.shape[0]
    n_out = w_ada.shape[1]
    return pl.pallas_call(
        _ada_body,
        grid=(n_out // D,),
        in_specs=[pl.BlockSpec((rows, D), lambda i: (0, 0)),
                  pl.BlockSpec((D, D), lambda i: (0, i)),
                  pl.BlockSpec((1, D), lambda i: (0, i))],
        out_specs=pl.BlockSpec((rows, D), lambda i: (0, i)),
        out_shape=jax.ShapeDtypeStruct((rows, n_out), F32),
        compiler_params=_params(("parallel",)),
        name="ada",
    )(cond, w_ada, b_ada.reshape(1, n_out))


def _xr_body(x_ref, mod_ref, g_ref, w_ref, cw_ref, cb_ref, o_ref, xr_s, *, seq_len):
    n = x_ref.shape[0]
    halo = V7X_SUBLANES
    ch = XR_CHUNK
    xr_s[0:halo, :] = jnp.zeros((halo, D), F32)
    xr_s[halo + n:2 * halo + n, :] = jnp.zeros((halo, D), F32)
    for r0 in range(0, n, ch):
        x = x_ref[r0:r0 + ch, :]
        h = _norm_mod(x, g_ref[...], mod_ref[0, 1:2, :], mod_ref[0, 0:1, :]).astype(BF16)
        xr_s[halo + r0:halo + r0 + ch, :] = _dot(h, w_ref[...])
    win = ch + 2 * halo
    for r0 in range(0, n, ch):
        w = xr_s[r0:r0 + win, :]
        pos = (lax.broadcasted_iota(jnp.int32, (ch, 1), 0) + r0) % seq_len
        y = w[halo:halo + ch] * cw_ref[2:3, :] + cb_ref[...]
        y = y + jnp.where(pos >= 2, pltpu.roll(w, 2, 0)[halo:halo + ch], 0.0) * cw_ref[0:1, :]
        y = y + jnp.where(pos >= 1, pltpu.roll(w, 1, 0)[halo:halo + ch], 0.0) * cw_ref[1:2, :]
        y = y + jnp.where(pos <= seq_len - 2,
                          pltpu.roll(w, win - 1, 0)[halo:halo + ch], 0.0) * cw_ref[3:4, :]
        o_ref[r0:r0 + ch, :] = y


def _xr_conv(x, mod, mod_per_tile, g1, w_xr, cw, cb, seq_len):
    n = x.shape[0]
    assert SEQ_TILE % seq_len == 0 and n % SEQ_TILE == 0
    mod_map = (lambda i: (i, 0, 0)) if mod_per_tile else (lambda i: (0, 0, 0))
    return pl.pallas_call(
        functools.partial(_xr_body, seq_len=seq_len),
        grid=(n // SEQ_TILE,),
        in_specs=[pl.BlockSpec((SEQ_TILE, D), lambda i: (i, 0)),
                  pl.BlockSpec((1, 6, D), mod_map),
                  _const_spec((1, D)),
                  _const_spec((D, D)),
                  _const_spec((4, D)),
                  _const_spec((1, D))],
        out_specs=pl.BlockSpec((SEQ_TILE, D), lambda i: (i, 0)),
        out_shape=jax.ShapeDtypeStruct((n, D), F32),
        scratch_shapes=[pltpu.VMEM((SEQ_TILE + 2 * V7X_SUBLANES, D), F32)],
        compiler_params=_params(("parallel",)),
        name="xr_conv",
    )(x, mod, g1, w_xr, cw, cb)


def _scan_body(xf_ref, xb_ref, h0_ref, wg_ref, ba_ref, bx_ref, lam_ref,
               hf_ref, hb_ref, last_ref, a_s, u_s, hc):
    j = pl.program_id(1)
    n_t = pl.num_programs(1)
    t_len = xf_ref.shape[1]
    rows = SCAN_B * t_len

    @pl.when(j == 0)
    def _():
        for d in range(2):
            for s in range(N_HEADS):
                hc[d, s] = h0_ref[0, d, :, s * HEAD:(s + 1) * HEAD]

    for d, x_ref in ((0, xf_ref), (1, xb_ref)):
        z = -lam_ref[d:d + 1, :]
        sp = jnp.maximum(z, 0.0) + jnp.log(1.0 + jnp.exp(-jnp.abs(z)))
        c2 = (-0.5 * RG_C * LOG2_E) * sp
        for hd in range(N_HEADS):
            sl = slice(hd * HEAD, (hd + 1) * HEAD)
            xh = x_ref[:, :, sl].reshape(rows, HEAD)
            g = _dot(xh.astype(BF16), wg_ref[d, hd])
            t_r = jnp.tanh(g[:, :HEAD] + 0.5 * ba_ref[d:d + 1, sl])
            t_i = jnp.tanh(g[:, HEAD:] + 0.5 * bx_ref[d:d + 1, sl])
            a = jnp.exp2(c2[:, sl] * t_r + c2[:, sl])
            q = 0.25 - 0.25 * (a * a)
            half_mult = q * lax.rsqrt(jnp.maximum(q, TINY))
            u = half_mult * ((t_i + 1.0) * xh)
            for b in range(SCAN_B):
                a_s[d, hd, pl.ds(b, t_len, stride=SCAN_B), :] = a[b * t_len:(b + 1) * t_len]
                u_s[d, hd, pl.ds(b, t_len, stride=SCAN_B), :] = u[b * t_len:(b + 1) * t_len]

    def step(tb, carry):
        h_f, h_b = carry
        for k in range(SCAN_UNROLL):
            t = tb * SCAN_UNROLL + k
            rf = pl.multiple_of(t * SCAN_B, SCAN_B)
            rb = pl.multiple_of((t_len - 1 - t) * SCAN_B, SCAN_B)
            h_f = a_s[0, :, pl.ds(rf, SCAN_B), :] * h_f + u_s[0, :, pl.ds(rf, SCAN_B), :]
            h_b = a_s[1, :, pl.ds(rb, SCAN_B), :] * h_b + u_s[1, :, pl.ds(rb, SCAN_B), :]
            a_s[0, :, pl.ds(rf, SCAN_B), :] = h_f
            a_s[1, :, pl.ds(rb, SCAN_B), :] = h_b
        return h_f, h_b

    h_f, h_b = lax.fori_loop(0, t_len // SCAN_UNROLL, step, (hc[0], hc[1]))
    hc[0] = h_f
    hc[1] = h_b

    for d, o_ref in ((0, hf_ref), (1, hb_ref)):
        for hd in range(N_HEADS):
            for b in range(SCAN_B):
                o_ref[b, :, hd * HEAD:(hd + 1) * HEAD] = (
                    a_s[d, hd, pl.ds(b, t_len, stride=SCAN_B), :].astype(BF16))

    @pl.when(j == n_t - 1)
    def _():
        for d in range(2):
            for s in range(N_HEADS):
                last_ref[0, d, :, s * HEAD:(s + 1) * HEAD] = hc[d, s]


def _rglru_scan(xc, h0, wg, ba, bx, lam):
    n_b, seq_len, _ = xc.shape
    n_g = n_b // SCAN_B
    n_t = seq_len // SCAN_T
    blk = (SCAN_B, SCAN_T, D)
    return pl.pallas_call(
        _scan_body,
        grid=(n_g, n_t),
        in_specs=[pl.BlockSpec(blk, lambda g, j: (g, j, 0)),
                  pl.BlockSpec(blk, lambda g, j: (g, n_t - 1 - j, 0)),
                  pl.BlockSpec((1, 2, SCAN_B, D), lambda g, j: (g, 0, 0, 0)),
                  _const_spec((2, N_HEADS, HEAD, 2 * HEAD)),
                  _const_spec((2, D)),
                  _const_spec((2, D)),
                  _const_spec((2, D))],
        out_specs=[pl.BlockSpec(blk, lambda g, j: (g, j, 0)),
                   pl.BlockSpec(blk, lambda g, j: (g, n_t - 1 - j, 0)),
                   pl.BlockSpec((1, 2, SCAN_B, D), lambda g, j: (g, 0, 0, 0))],
        out_shape=[jax.ShapeDtypeStruct((n_b, seq_len, D), BF16),
                   jax.ShapeDtypeStruct((n_b, seq_len, D), BF16),
                   jax.ShapeDtypeStruct((n_g, 2, SCAN_B, D), F32)],
        scratch_shapes=[pltpu.VMEM((2, N_HEADS, SCAN_T * SCAN_B, HEAD), F32),
                        pltpu.VMEM((2, N_HEADS, SCAN_T * SCAN_B, HEAD), F32),
                        pltpu.VMEM((2, N_HEADS, SCAN_B, HEAD), F32)],
        compiler_params=_params(("parallel", "arbitrary")),
        name="rglru_scan",
    )(xc, xc, h0, wg, ba, bx, lam)


def _pack_halves(v):
    half = v.shape[1] // 2
    lo = lax.bitcast_convert_type(v[:, :half].astype(BF16).astype(F32), jnp.uint32)
    hi = lax.bitcast_convert_type(v[:, half:].astype(BF16).astype(F32), jnp.uint32)
    return lax.bitcast_convert_type((lo >> 16) | (hi & jnp.uint32(0xFFFF0000)), jnp.int32)


def _unpack_halves(p):
    u = lax.bitcast_convert_type(p, jnp.uint32)
    lo = lax.bitcast_convert_type(u << 16, F32)
    hi = lax.bitcast_convert_type(u & jnp.uint32(0xFFFF0000), F32)
    return lo, hi


def _mix_body(x_ref, mod_ref, hf_ref, hb_ref, g1_ref, g2_ref, win_ref, cw_ref,
              wco_ref, wro_ref, wo_ref, wr_ref, br_ref,
              x1_ref, h2_ref, route_ref, wts_ref, cnt_ref, seen, *, row_len):
    x = x_ref[...]
    tm = x.shape[0]
    h = _norm_mod(x, g1_ref[...], mod_ref[0, 1:2, :], mod_ref[0, 0:1, :]).astype(BF16)

    def proj(k):
        return _dot(h, win_ref[:, k * D:(k + 1) * D])

    cv = proj(1) * proj(2)
    pos = lax.broadcasted_iota(jnp.int32, (tm, 1), 0) % row_len
    conv = cv * cw_ref[1:2, :]
    conv = conv + jnp.where(pos >= 1, pltpu.roll(cv, 1, 0), 0.0) * cw_ref[0:1, :]
    conv = conv + jnp.where(pos <= row_len - 2, pltpu.roll(cv, tm - 1, 0), 0.0) * cw_ref[2:3, :]
    y_a = _dot((proj(0) * conv).astype(BF16), wco_ref[...])
    merged = _sigmoid(proj(5)) * y_a

    hs = hf_ref[...].astype(F32) + hb_ref[...].astype(F32)
    y_b = _dot((hs * jax.nn.gelu(proj(4))).astype(BF16), wro_ref[...])
    merged = merged + _sigmoid(proj(6)) * y_b

    mix = _dot(merged.astype(BF16), wo_ref[...])
    x1 = x + mod_ref[0, 2:3, :] * mix
    x1_ref[...] = x1
    h2 = _norm_mod(x1, g2_ref[...], mod_ref[0, 4:5, :], mod_ref[0, 3:4, :])
    h2_ref[...] = _pack_halves(h2)

    h2_hi = h2.astype(BF16)
    h2_lo = (h2 - h2_hi.astype(F32)).astype(BF16)
    big = _dot(h2_hi, wr_ref[...])
    logits = (big[:, :ROUTE_PAD] + big[:, ROUTE_PAD:]
              + _dot(h2_lo, wr_ref[:, :ROUTE_PAD]) + br_ref[...])
    lt = logits.T
    row = lax.broadcasted_iota(jnp.int32, (EPG, tm), 0)
    lg = lt[0:EPG]
    mg = jnp.max(lg, axis=0, keepdims=True)
    p_grp = 1.0 / jnp.sum(jnp.exp(lg - mg), axis=0, keepdims=True)
    grp = jnp.min(jnp.where(lg == mg, row, EPG), axis=0, keepdims=True)
    le = lt[EPG * N_GROUPS:EPG * (N_GROUPS + 1)]
    for g in range(N_GROUPS - 2, -1, -1):
        le = jnp.where(grp == g, lt[EPG * (g + 1):EPG * (g + 2)], le)
    me = jnp.max(le, axis=0, keepdims=True)
    ee = jnp.exp(le - me)
    pe = ee / jnp.sum(ee, axis=0, keepdims=True)
    p1 = jnp.max(pe, axis=0, keepdims=True)
    i1 = jnp.min(jnp.where(pe == p1, row, EPG), axis=0, keepdims=True)
    pe2 = jnp.where(row == i1, -1.0, pe)
    p2 = jnp.max(pe2, axis=0, keepdims=True)
    i2 = jnp.min(jnp.where(pe2 == p2, row, EPG), axis=0, keepdims=True)
    den = p1 + p2
    e1 = grp * EPG + i1
    e2 = grp * EPG + i2

    @pl.when(pl.program_id(0) == 0)
    def _():
        seen[...] = jnp.zeros_like(seen)

    erow = lax.broadcasted_iota(jnp.int32, (N_EXPERTS, tm), 0)
    hit1 = erow == e1
    hit2 = erow == e2
    both = jnp.where(jnp.logical_or(hit1, hit2), 1.0, 0.0)
    tri = jnp.where(lax.broadcasted_iota(jnp.int32, (tm, tm), 0)
                    <= lax.broadcasted_iota(jnp.int32, (tm, tm), 1), 1.0, 0.0).astype(BF16)
    before = _dot(both.astype(BF16), tri) - both + seen[...]
    r1 = jnp.sum(jnp.where(hit1, before, 0.0), axis=0, keepdims=True).astype(jnp.int32)
    r2 = jnp.sum(jnp.where(hit2, before, 0.0), axis=0, keepdims=True).astype(jnp.int32)
    total = seen[...] + jnp.sum(both, axis=1, keepdims=True)
    seen[...] = total
    cnt_ref[...] = total[:, :V7X_LANES].astype(jnp.int32)

    route_ref[...] = jnp.where(row == 0, e1, jnp.where(row == 1, e2, jnp.where(
        row == 2, r1, jnp.where(row == 3, r2, 0))))
    w8 = jnp.where(row == 0, p_grp * p1 / den, jnp.where(row == 1, p_grp * p2 / den, 0.0))
    wts_ref[...] = jnp.concatenate([w8, jnp.zeros((V7X_LANES - EPG, tm), F32)], axis=0).T


def _mixer(x, mod, tiles_per_mod, hf, hb, g1, g2, w_in, cw, wco, wro, wo, wr, br, row_len):
    n = x.shape[0]
    tm = MIX_TM
    assert tm % row_len == 0 and n % tm == 0
    mod_map = lambda i: (i // tiles_per_mod, 0, 0)
    tok = lambda i: (i, 0)
    col = lambda i: (0, i)
    return pl.pallas_call(
        functools.partial(_mix_body, row_len=row_len),
        grid=(n // tm,),
        in_specs=[pl.BlockSpec((tm, D), tok),
                  pl.BlockSpec((1, 6, D), mod_map),
                  pl.BlockSpec((tm, D), tok),
                  pl.BlockSpec((tm, D), tok),
                  _const_spec((1, D)),
                  _const_spec((1, D)),
                  _const_spec(w_in.shape),
                  _const_spec((3, D)),
                  _const_spec((D, D)),
                  _const_spec((D, D)),
                  _const_spec((D, D)),
                  _const_spec((D, 2 * ROUTE_PAD)),
                  _const_spec((1, ROUTE_PAD))],
        out_specs=[pl.BlockSpec((tm, D), tok),
                   pl.BlockSpec((tm, D // 2), tok),
                   pl.BlockSpec((EPG, tm), col),
                   pl.BlockSpec((tm, V7X_LANES), tok),
                   pl.BlockSpec((N_EXPERTS, V7X_LANES), lambda i: (0, 0))],
        out_shape=[jax.ShapeDtypeStruct((n, D), F32),
                   jax.ShapeDtypeStruct((n, D // 2), jnp.int32),
                   jax.ShapeDtypeStruct((EPG, n), jnp.int32),
                   jax.ShapeDtypeStruct((n, V7X_LANES), F32),
                   jax.ShapeDtypeStruct((N_EXPERTS, V7X_LANES), jnp.int32)],
        scratch_shapes=[pltpu.VMEM((N_EXPERTS, tm), F32)],
        compiler_params=_params(("arbitrary",)),
        name="mixer",
    )(x, mod, hf, hb, g1, g2, w_in, cw, wco, wro, wo, wr, br)


def _sc_mesh():
    return plsc.VectorSubcoreMesh(core_axis_name="c", subcore_axis_name="s",
                                  num_cores=V7X_SC_CORES, num_subcores=V7X_SC_SUBCORES)


def _sc_worker_id():
    return lax.axis_index("s") * V7X_SC_CORES + lax.axis_index("c")


def _sc_dispatch(rows, dest, n_slots):
    n, width = rows.shape
    per_w = n // V7X_SC_WORKERS
    n_ch = per_w // SC_WINDOW
    assert n_ch * SC_WINDOW * V7X_SC_WORKERS == n
    idx = dest.reshape(2, V7X_SC_WORKERS, n_ch, SC_WINDOW).transpose(1, 0, 2, 3)

    def body(x_hbm, d_hbm, o_hbm, idx_v, rows_v):
        wid = _sc_worker_id()
        pltpu.sync_copy(d_hbm.at[wid], idx_v)
        for j in range(n_ch):
            pltpu.sync_copy(x_hbm.at[pl.ds(wid * per_w + j * SC_WINDOW, SC_WINDOW)], rows_v)
            pltpu.sync_copy(rows_v, o_hbm.at[idx_v.at[0, j]])
            pltpu.sync_copy(rows_v, o_hbm.at[idx_v.at[1, j]])

    return pl.kernel(
        body,
        out_type=jax.ShapeDtypeStruct((n_slots, width), jnp.int32),
        mesh=_sc_mesh(),
        scratch_types=[pltpu.VMEM((2, n_ch, SC_WINDOW), jnp.int32),
                       pltpu.VMEM((SC_WINDOW, width), jnp.int32)],
        name="sc_dispatch",
    )(rows, idx)


def _sc_collect(rows, dest):
    n = dest.shape[1]
    width = rows.shape[1]
    per_w = n // V7X_SC_WORKERS
    n_ch = per_w // SC_WINDOW
    assert n_ch * SC_WINDOW * V7X_SC_WORKERS == n
    idx = dest.reshape(2, V7X_SC_WORKERS, n_ch, SC_WINDOW).transpose(1, 0, 2, 3)

    def body(y_hbm, d_hbm, o_hbm, idx_v, rows_v):
        wid = _sc_worker_id()
        pltpu.sync_copy(d_hbm.at[wid], idx_v)
        for k in range(2):
            for j in range(n_ch):
                pltpu.sync_copy(y_hbm.at[idx_v.at[k, j]], rows_v)
                pltpu.sync_copy(rows_v, o_hbm.at[pl.ds(k * n + wid * per_w + j * SC_WINDOW, SC_WINDOW)])

    return pl.kernel(
        body,
        out_type=jax.ShapeDtypeStruct((2 * n, width), jnp.int32),
        mesh=_sc_mesh(),
        scratch_types=[pltpu.VMEM((2, n_ch, SC_WINDOW), jnp.int32),
                       pltpu.VMEM((SC_WINDOW, width), jnp.int32)],
        name="sc_collect",
    )(rows, idx)


def _expert_body(te_ref, nu_ref, xs_ref, w1_ref, w3_ref, w2_ref, o_ref, w1b, w3b, w2b):
    i = pl.program_id(0)
    prev = te_ref[jnp.maximum(i - 1, 0)]

    @pl.when(jnp.logical_or(i == 0, te_ref[i] != prev))
    def _():
        w1b[...] = w1_ref[0].astype(BF16)
        w3b[...] = w3_ref[0].astype(BF16)
        w2b[...] = w2_ref[0].astype(BF16)

    @pl.when(i < nu_ref[0])
    def _():
        lo, hi = _unpack_halves(xs_ref[...])
        lo = lo.astype(BF16)
        hi = hi.astype(BF16)
        half = D // 2
        a = _dot(lo, w1b[0:half, :]) + _dot(hi, w1b[half:D, :])
        b = _dot(lo, w3b[0:half, :]) + _dot(hi, w3b[half:D, :])
        z = (a * _sigmoid(a)) * b
        o_ref[...] = _pack_halves(_dot(z.astype(BF16), w2b[...]))

    @pl.when(i >= nu_ref[0])
    def _():
        o_ref[...] = jnp.zeros_like(o_ref)


def _experts(tile_e, n_used, xs, w1, w3, w2, tme):
    n_slots = xs.shape[0]
    grid_spec = pltpu.PrefetchScalarGridSpec(
        num_scalar_prefetch=2,
        grid=(n_slots // tme,),
        in_specs=[pl.BlockSpec((tme, D // 2), lambda i, te, nu: (i, 0)),
                  pl.BlockSpec((1, D, D_EXPERT), lambda i, te, nu: (te[i], 0, 0)),
                  pl.BlockSpec((1, D, D_EXPERT), lambda i, te, nu: (te[i], 0, 0)),
                  pl.BlockSpec((1, D_EXPERT, D), lambda i, te, nu: (te[i], 0, 0))],
        out_specs=pl.BlockSpec((tme, D // 2), lambda i, te, nu: (i, 0)),
        scratch_shapes=[pltpu.VMEM((D, D_EXPERT), BF16),
                        pltpu.VMEM((D, D_EXPERT), BF16),
                        pltpu.VMEM((D_EXPERT, D), BF16)],
    )
    return pl.pallas_call(
        _expert_body,
        grid_spec=grid_spec,
        out_shape=jax.ShapeDtypeStruct((n_slots, D // 2), jnp.int32),
        compiler_params=_params(("arbitrary",)),
        name="experts",
    )(tile_e, n_used, xs, w1, w3, w2)


def _final_body(x1_ref, mod_ref, y0_ref, y1_ref, wt_ref, gf_ref, o_ref):
    w0 = wt_ref[:, 0:1]
    w1 = wt_ref[:, 1:2]
    lo0, hi0 = _unpack_halves(y0_ref[...])
    lo1, hi1 = _unpack_halves(y1_ref[...])
    moe = jnp.concatenate([w0 * lo0 + w1 * lo1, w0 * hi0 + w1 * hi1], axis=1)
    x2 = x1_ref[...] + mod_ref[0, 5:6, :] * moe
    ms = jnp.mean(x2 * x2, axis=-1, keepdims=True)
    o_ref[...] = x2 * lax.rsqrt(ms + EPS) * gf_ref[...]


def _final(x1, mod, tiles_per_mod, yg, wts, g_final):
    n = x1.shape[0]
    tm = FIN_TM
    nt = n // tm
    return pl.pallas_call(
        _final_body,
        grid=(nt,),
        in_specs=[pl.BlockSpec((tm, D), lambda i: (i, 0)),
                  pl.BlockSpec((1, 6, D), lambda i: (i // tiles_per_mod, 0, 0)),
                  pl.BlockSpec((tm, D // 2), lambda i: (i, 0)),
                  pl.BlockSpec((tm, D // 2), lambda i: (i + nt, 0)),
                  pl.BlockSpec((tm, V7X_LANES), lambda i: (i, 0)),
                  _const_spec((1, D))],
        out_specs=pl.BlockSpec((tm, D), lambda i: (i, 0)),
        out_shape=jax.ShapeDtypeStruct((n, D), F32),
        compiler_params=_params(("parallel",)),
        name="final",
    )(x1, mod, yg, yg, wts, g_final)


def _slot_plan(route, cnt, n, tme):
    counts = cnt[:, 0]
    padded = ((counts + tme - 1) // tme) * tme
    pend = jnp.cumsum(padded)
    pstart = pend - padded
    onehot = route[0:2, :, None] == jnp.arange(N_EXPERTS, dtype=jnp.int32)[None, None, :]
    dest = jnp.sum(jnp.where(onehot, pstart[None, None, :], 0), axis=-1) + route[2:4]
    n_slots = ((2 * n + N_EXPERTS * (tme - 1)) // tme) * tme
    tile_start = jnp.arange(n_slots // tme, dtype=jnp.int32) * tme
    tile_e = jnp.sum((tile_start[:, None] >= pend[None, :]).astype(jnp.int32), axis=1)
    tile_e = jnp.minimum(tile_e, N_EXPERTS - 1)
    n_used = (pend[-1] // tme).astype(jnp.int32).reshape(1)
    return dest.astype(jnp.int32), tile_e, n_used, n_slots


def _group(x, mod, mod_per_seq, h0, p, row_len, tme):
    n_b, seq_len, _ = x.shape
    n = n_b * seq_len
    xt = x.reshape(n, D)
    xc = _xr_conv(xt, mod, mod_per_seq and seq_len == SEQ_TILE, p["g1"], p["w_xr"],
                  p["rnn_conv_w"], p["rnn_conv_b"], seq_len)
    hf, hb, last = _rglru_scan(xc.reshape(n_b, seq_len, D), h0, p["wg"], p["ba"], p["bx"], p["lam"])
    tiles_per_mod = (seq_len // MIX_TM) if mod_per_seq else (n // MIX_TM)
    x1, h2, route, wts, cnt = _mixer(xt, mod, tiles_per_mod, hf.reshape(n, D), hb.reshape(n, D),
                                     p["g1"], p["g2"], p["w_in"], p["conv_w"], p["wco"], p["wro"],
                                     p["wo"], p["wr"], p["br"], row_len)
    dest, tile_e, n_used, n_slots = _slot_plan(route, cnt, n, tme)
    xs = _sc_dispatch(h2, dest, n_slots)
    ys = _experts(tile_e, n_used, xs, p["w1"], p["w3"], p["w2"], tme)
    yg = _sc_collect(ys, dest)
    tiles_per_mod_f = (seq_len // FIN_TM) if mod_per_seq else (n // FIN_TM)
    y = _final(x1, mod, tiles_per_mod_f, yg, wts, p["g_final"])
    return y.reshape(n_b, seq_len, D), last


def kernel(x_prompt, x_sample, state_rnn, c, c_ctx, w_ada, b_ada, g_norm1, g_norm2, w_in, conv_w, w_conv_out, rnn_conv_w, rnn_conv_b, w_gate_a, b_gate_a, w_gate_x, b_gate_x, lam, w_rnn_out, w_o, w_router_group, b_router_group, w_router_expert, b_router_expert, w1, w3, w2, g_final):
    assert w_ada.shape[0] == 1, "single layer"
    n_pb, n_sb = x_prompt.shape[0], x_sample.shape[0]

    cond = jnp.zeros((16, D), F32).at[0].set(c_ctx).at[1:1 + n_sb].set(c)
    mod = _ada(cond, w_ada[0], b_ada[0]).reshape(16, 6, D)

    w_in_b = w_in[0].astype(BF16)
    wr = jnp.zeros((D, ROUTE_PAD), F32)
    wr = wr.at[:, :N_GROUPS].set(w_router_group[0]).at[:, EPG:EPG + N_EXPERTS].set(w_router_expert[0])
    br = jnp.zeros((1, ROUTE_PAD), F32).at[0, N_GROUPS:EPG].set(NEG_BIG)
    br = br.at[0, :N_GROUPS].set(b_router_group[0]).at[0, EPG:EPG + N_EXPERTS].set(b_router_expert[0])
    wr_hi = wr.astype(BF16)
    p = dict(
        g1=g_norm1, g2=g_norm2, w_in=w_in_b, w_xr=w_in_b[:, 3 * D:4 * D],
        conv_w=conv_w[0], rnn_conv_w=rnn_conv_w[0], rnn_conv_b=rnn_conv_b,
        wg=(0.5 * jnp.concatenate([w_gate_a[0], w_gate_x[0]], axis=-1)).astype(BF16),
        ba=b_gate_a[0], bx=b_gate_x[0], lam=lam[0],
        wco=w_conv_out[0].astype(BF16), wro=w_rnn_out[0].astype(BF16), wo=w_o[0].astype(BF16),
        wr=jnp.concatenate([wr_hi, (wr - wr_hi.astype(F32)).astype(BF16)], axis=1), br=br, w1=w1[0], w3=w3[0], w2=w2[0], g_final=g_final.reshape(1, D),
    )

    h0_p = jnp.zeros((n_pb // SCAN_B, 2, SCAN_B, D), F32)
    y_prompt, last = _group(x_prompt, mod[0:1], False, h0_p, p, x_prompt.shape[1], 256)
    state_new = last.transpose(0, 2, 1, 3).reshape(n_pb, 1, 2, D)

    h0_s = state_rnn[:, 0].reshape(n_sb // SCAN_B, SCAN_B, 2, D).transpose(0, 2, 1, 3)
    y_sample, _ = _group(x_sample, mod[1:1 + n_sb], True, h0_s, p, GRID_W, 512)
    return (y_prompt, y_sample, state_new)
```

```python
import functools

import jax
import jax.numpy as jnp
from jax import lax
from jax.experimental import pallas as pl
from jax.experimental.pallas import tpu as pltpu
from jax.experimental.pallas import tpu_sc as plsc

D = 1024
N_HEADS = 8
HEAD = D // N_HEADS
GRID_W = 64
RG_C = 8.0
N_GROUPS = 4
EPG = 8
N_EXPERTS = N_GROUPS * EPG
D_EXPERT = 512
EPS = 1e-6
F32 = jnp.float32
BF16 = jnp.bfloat16

V7X_LANES = 128
V7X_SUBLANES = 8
V7X_VMEM_LIMIT_BYTES = 56 * 1024 * 1024
V7X_SC_CORES = 2
V7X_SC_SUBCORES = 16
V7X_SC_WORKERS = V7X_SC_CORES * V7X_SC_SUBCORES
SC_WINDOW = 128

SEQ_TILE = 2048
XR_CHUNK = 256
SCAN_T = 64
SCAN_B = V7X_SUBLANES
SCAN_UNROLL = 8
LOG2_E = 1.4426950408889634
TINY = 1e-30
MIX_TM = 512
FIN_TM = 512
ROUTE_PAD = 128
NEG_BIG = -1e30


def _sigmoid(x):
    return 0.5 * jnp.tanh(0.5 * x) + 0.5


def _norm_mod(x, g, scale, shift):
    ms = jnp.mean(x * x, axis=-1, keepdims=True)
    return (x * lax.rsqrt(ms + EPS) * g) * (1.0 + scale) + shift


def _dot(a, b):
    return jnp.dot(a, b, preferred_element_type=F32)


def _params(sem, vmem=V7X_VMEM_LIMIT_BYTES):
    return pltpu.CompilerParams(dimension_semantics=sem, vmem_limit_bytes=vmem)


def _const_spec(shape):
    zeros = (0,) * len(shape)
    return pl.BlockSpec(shape, lambda *_: zeros, pipeline_mode=pl.Buffered(1))


def _ada_body(c_ref, w_ref, b_ref, o_ref):
    c = c_ref[...]
    s = (c * _sigmoid(c)).astype(BF16)
    o_ref[...] = _dot(s, w_ref[...].astype(BF16)) + b_ref[...]


def _ada(cond, w_ada, b_ada):
    rows = cond.shape[0]
    n_out = w_ada.shape[1]
    return pl.pallas_call(
        _ada_body,
        grid=(n_out // D,),
        in_specs=[pl.BlockSpec((rows, D), lambda i: (0, 0)),
                  pl.BlockSpec((D, D), lambda i: (0, i)),
                  pl.BlockSpec((1, D), lambda i: (0, i))],
        out_specs=pl.BlockSpec((rows, D), lambda i: (0, i)),
        out_shape=jax.ShapeDtypeStruct((rows, n_out), F32),
        compiler_params=_params(("parallel",)),
        name="ada",
    )(cond, w_ada, b_ada.reshape(1, n_out))


def _xr_body(x_ref, mod_ref, g_ref, w_ref, cw_ref, cb_ref, o_ref, xr_s, *, seq_len):
    n = x_ref.shape[0]
    halo = V7X_SUBLANES
    ch = XR_CHUNK
    xr_s[0:halo, :] = jnp.zeros((halo, D), F32)
    xr_s[halo + n:2 * halo + n, :] = jnp.zeros((halo, D), F32)
    for r0 in range(0, n, ch):
        x = x_ref[r0:r0 + ch, :]
        h = _norm_mod(x, g_ref[...], mod_ref[0, 1:2, :], mod_ref[0, 0:1, :]).astype(BF16)
        xr_s[halo + r0:halo + r0 + ch, :] = _dot(h, w_ref[...])
    win = ch + 2 * halo
    for r0 in range(0, n, ch):
        w = xr_s[r0:r0 + win, :]
        pos = (lax.broadcasted_iota(jnp.int32, (ch, 1), 0) + r0) % seq_len
        y = w[halo:halo + ch] * cw_ref[2:3, :] + cb_ref[...]
        y = y + jnp.where(pos >= 2, pltpu.roll(w, 2, 0)[halo:halo + ch], 0.0) * cw_ref[0:1, :]
        y = y + jnp.where(pos >= 1, pltpu.roll(w, 1, 0)[halo:halo + ch], 0.0) * cw_ref[1:2, :]
        y = y + jnp.where(pos <= seq_len - 2,
                          pltpu.roll(w, win - 1, 0)[halo:halo + ch], 0.0) * cw_ref[3:4, :]
        o_ref[r0:r0 + ch, :] = y


def _xr_conv(x, mod, mod_per_tile, g1, w_xr, cw, cb, seq_len):
    n = x.shape[0]
    assert SEQ_TILE % seq_len == 0 and n % SEQ_TILE == 0
    mod_map = (lambda i: (i, 0, 0)) if mod_per_tile else (lambda i: (0, 0, 0))
    return pl.pallas_call(
        functools.partial(_xr_body, seq_len=seq_len),
        grid=(n // SEQ_TILE,),
        in_specs=[pl.BlockSpec((SEQ_TILE, D), lambda i: (i, 0)),
                  pl.BlockSpec((1, 6, D), mod_map),
                  _const_spec((1, D)),
                  _const_spec((D, D)),
                  _const_spec((4, D)),
                  _const_spec((1, D))],
        out_specs=pl.BlockSpec((SEQ_TILE, D), lambda i: (i, 0)),
        out_shape=jax.ShapeDtypeStruct((n, D), F32),
        scratch_shapes=[pltpu.VMEM((SEQ_TILE + 2 * V7X_SUBLANES, D), F32)],
        compiler_params=_params(("parallel",)),
        name="xr_conv",
    )(x, mod, g1, w_xr, cw, cb)


def _scan_body(xf_ref, xb_ref, h0_ref, wg_ref, ba_ref, bx_ref, lam_ref,
               hf_ref, hb_ref, last_ref, a_s, u_s, hc):
    j = pl.program_id(1)
    n_t = pl.num_programs(1)
    t_len = xf_ref.shape[1]
    rows = SCAN_B * t_len

    @pl.when(j == 0)
    def _():
        for d in range(2):
            for s in range(N_HEADS):
                hc[d, s] = h0_ref[0, d, :, s * HEAD:(s + 1) * HEAD]

    for d, x_ref in ((0, xf_ref), (1, xb_ref)):
        z = -lam_ref[d:d + 1, :]
        sp = jnp.maximum(z, 0.0) + jnp.log(1.0 + jnp.exp(-jnp.abs(z)))
        c2 = (-0.5 * RG_C * LOG2_E) * sp
        for hd in range(N_HEADS):
            sl = slice(hd * HEAD, (hd + 1) * HEAD)
            xh = x_ref[:, :, sl].reshape(rows, HEAD)
            g = _dot(xh.astype(BF16), wg_ref[d, hd])
            t_r = jnp.tanh(g[:, :HEAD] + 0.5 * ba_ref[d:d + 1, sl])
            t_i = jnp.tanh(g[:, HEAD:] + 0.5 * bx_ref[d:d + 1, sl])
            a = jnp.exp2(c2[:, sl] * t_r + c2[:, sl])
            q = 0.25 - 0.25 * (a * a)
            half_mult = q * lax.rsqrt(jnp.maximum(q, TINY))
            u = half_mult * ((t_i + 1.0) * xh)
            for b in range(SCAN_B):
                a_s[d, hd, pl.ds(b, t_len, stride=SCAN_B), :] = a[b * t_len:(b + 1) * t_len]
                u_s[d, hd, pl.ds(b, t_len, stride=SCAN_B), :] = u[b * t_len:(b + 1) * t_len]

    def step(tb, carry):
        h_f, h_b = carry
        for k in range(SCAN_UNROLL):
            t = tb * SCAN_UNROLL + k
            rf = pl.multiple_of(t * SCAN_B, SCAN_B)
            rb = pl.multiple_of((t_len - 1 - t) * SCAN_B, SCAN_B)
            h_f = a_s[0, :, pl.ds(rf, SCAN_B), :] * h_f + u_s[0, :, pl.ds(rf, SCAN_B), :]
            h_b = a_s[1, :, pl.ds(rb, SCAN_B), :] * h_b + u_s[1, :, pl.ds(rb, SCAN_B), :]
            a_s[0, :, pl.ds(rf, SCAN_B), :] = h_f
            a_s[1, :, pl.ds(rb, SCAN_B), :] = h_b
        return h_f, h_b

    h_f, h_b = lax.fori_loop(0, t_len // SCAN_UNROLL, step, (hc[0], hc[1]))
    hc[0] = h_f
    hc[1] = h_b

    for d, o_ref in ((0, hf_ref), (1, hb_ref)):
        for hd in range(N_HEADS):
            for b in range(SCAN_B):
                o_ref[b, :, hd * HEAD:(hd + 1) * HEAD] = (
                    a_s[d, hd, pl.ds(b, t_len, stride=SCAN_B), :].astype(BF16))

    @pl.when(j == n_t - 1)
    def _():
        for d in range(2):
            for s in range(N_HEADS):
                last_ref[0, d, :, s * HEAD:(s + 1) * HEAD] = hc[d, s]


def _rglru_scan(xc, h0, wg, ba, bx, lam):
    n_b, seq_len, _ = xc.shape
    n_g = n_b // SCAN_B
    n_t = seq_len // SCAN_T
    blk = (SCAN_B, SCAN_T, D)
    return pl.pallas_call(
        _scan_body,
        grid=(n_g, n_t),
        in_specs=[pl.BlockSpec(blk, lambda g, j: (g, j, 0)),
                  pl.BlockSpec(blk, lambda g, j: (g, n_t - 1 - j, 0)),
                  pl.BlockSpec((1, 2, SCAN_B, D), lambda g, j: (g, 0, 0, 0)),
                  _const_spec((2, N_HEADS, HEAD, 2 * HEAD)),
                  _const_spec((2, D)),
                  _const_spec((2, D)),
                  _const_spec((2, D))],
        out_specs=[pl.BlockSpec(blk, lambda g, j: (g, j, 0)),
                   pl.BlockSpec(blk, lambda g, j: (g, n_t - 1 - j, 0)),
                   pl.BlockSpec((1, 2, SCAN_B, D), lambda g, j: (g, 0, 0, 0))],
        out_shape=[jax.ShapeDtypeStruct((n_b, seq_len, D), BF16),
                   jax.ShapeDtypeStruct((n_b, seq_len, D), BF16),
                   jax.ShapeDtypeStruct((n_g, 2, SCAN_B, D), F32)],
        scratch_shapes=[pltpu.VMEM((2, N_HEADS, SCAN_T * SCAN_B, HEAD), F32),
                        pltpu.VMEM((2, N_HEADS, SCAN_T * SCAN_B, HEAD), F32),
                        pltpu.VMEM((2, N_HEADS, SCAN_B, HEAD), F32)],
        compiler_params=_params(("parallel", "arbitrary")),
        name="rglru_scan",
    )(xc, xc, h0, wg, ba, bx, lam)


def _pack_halves(v):
    half = v.shape[1] // 2
    lo = lax.bitcast_convert_type(v[:, :half].astype(BF16).astype(F32), jnp.uint32)
    hi = lax.bitcast_convert_type(v[:, half:].astype(BF16).astype(F32), jnp.uint32)
    return lax.bitcast_convert_type((lo >> 16) | (hi & jnp.uint32(0xFFFF0000)), jnp.int32)


def _unpack_halves(p):
    u = lax.bitcast_convert_type(p, jnp.uint32)
    lo = lax.bitcast_convert_type(u << 16, F32)
    hi = lax.bitcast_convert_type(u & jnp.uint32(0xFFFF0000), F32)
    return lo, hi


def _mix_body(x_ref, mod_ref, hf_ref, hb_ref, g1_ref, g2_ref, win_ref, cw_ref,
              wco_ref, wro_ref, wo_ref, wr_ref, br_ref,
              x1_ref, h2_ref, route_ref, wts_ref, cnt_ref, seen, *, row_len):
    x = x_ref[...]
    tm = x.shape[0]
    h = _norm_mod(x, g1_ref[...], mod_ref[0, 1:2, :], mod_ref[0, 0:1, :]).astype(BF16)

    def proj(k):
        return _dot(h, win_ref[:, k * D:(k + 1) * D])

    cv = proj(1) * proj(2)
    pos = lax.broadcasted_iota(jnp.int32, (tm, 1), 0) % row_len
    conv = cv * cw_ref[1:2, :]
    conv = conv + jnp.where(pos >= 1, pltpu.roll(cv, 1, 0), 0.0) * cw_ref[0:1, :]
    conv = conv + jnp.where(pos <= row_len - 2, pltpu.roll(cv, tm - 1, 0), 0.0) * cw_ref[2:3, :]
    y_a = _dot((proj(0) * conv).astype(BF16), wco_ref[...])
    merged = _sigmoid(proj(5)) * y_a

    hs = hf_ref[...].astype(F32) + hb_ref[...].astype(F32)
    y_b = _dot((hs * jax.nn.gelu(proj(4))).astype(BF16), wro_ref[...])
    merged = merged + _sigmoid(proj(6)) * y_b

    mix = _dot(merged.astype(BF16), wo_ref[...])
    x1 = x + mod_ref[0, 2:3, :] * mix
    x1_ref[...] = x1
    h2 = _norm_mod(x1, g2_ref[...], mod_ref[0, 4:5, :], mod_ref[0, 3:4, :])
    h2_ref[...] = _pack_halves(h2)

    h2_hi = h2.astype(BF16)
    h2_lo = (h2 - h2_hi.astype(F32)).astype(BF16)
    big = _dot(h2_hi, wr_ref[...])
    logits = (big[:, :ROUTE_PAD] + big[:, ROUTE_PAD:]
              + _dot(h2_lo, wr_ref[:, :ROUTE_PAD]) + br_ref[...])
    lt = logits.T
    row = lax.broadcasted_iota(jnp.int32, (EPG, tm), 0)
    lg = lt[0:EPG]
    mg = jnp.max(lg, axis=0, keepdims=True)
    p_grp = 1.0 / jnp.sum(jnp.exp(lg - mg), axis=0, keepdims=True)
    grp = jnp.min(jnp.where(lg == mg, row, EPG), axis=0, keepdims=True)
    le = lt[EPG * N_GROUPS:EPG * (N_GROUPS + 1)]
    for g in range(N_GROUPS - 2, -1, -1):
        le = jnp.where(grp == g, lt[EPG * (g + 1):EPG * (g + 2)], le)
    me = jnp.max(le, axis=0, keepdims=True)
    ee = jnp.exp(le - me)
    pe = ee / jnp.sum(ee, axis=0, keepdims=True)
    p1 = jnp.max(pe, axis=0, keepdims=True)
    i1 = jnp.min(jnp.where(pe == p1, row, EPG), axis=0, keepdims=True)
    pe2 = jnp.where(row == i1, -1.0, pe)
    p2 = jnp.max(pe2, axis=0, keepdims=True)
    i2 = jnp.min(jnp.where(pe2 == p2, row, EPG), axis=0, keepdims=True)
    den = p1 + p2
    e1 = grp * EPG + i1
    e2 = grp * EPG + i2

    @pl.when(pl.program_id(0) == 0)
    def _():
        seen[...] = jnp.zeros_like(seen)

    erow = lax.broadcasted_iota(jnp.int32, (N_EXPERTS, tm), 0)
    hit1 = erow == e1
    hit2 = erow == e2
    both = jnp.where(jnp.logical_or(hit1, hit2), 1.0, 0.0)
    tri = jnp.where(lax.broadcasted_iota(jnp.int32, (tm, tm), 0)
                    <= lax.broadcasted_iota(jnp.int32, (tm, tm), 1), 1.0, 0.0).astype(BF16)
    before = _dot(both.astype(BF16), tri) - both + seen[...]
    r1 = jnp.sum(jnp.where(hit1, before, 0.0), axis=0, keepdims=True).astype(jnp.int32)
    r2 = jnp.sum(jnp.where(hit2, before, 0.0), axis=0, keepdims=True).astype(jnp.int32)
    total = seen[...] + jnp.sum(both, axis=1, keepdims=True)
    seen[...] = total
    cnt_ref[...] = total[:, :V7X_LANES].astype(jnp.int32)

    route_ref[...] = jnp.where(row == 0, e1, jnp.where(row == 1, e2, jnp.where(
        row == 2, r1, jnp.where(row == 3, r2, 0))))
    w8 = jnp.where(row == 0, p_grp * p1 / den, jnp.where(row == 1, p_grp * p2 / den, 0.0))
    wts_ref[...] = jnp.concatenate([w8, jnp.zeros((V7X_LANES - EPG, tm), F32)], axis=0).T


def _mixer(x, mod, tiles_per_mod, hf, hb, g1, g2, w_in, cw, wco, wro, wo, wr, br, row_len):
    n = x.shape[0]
    tm = MIX_TM
    assert tm % row_len == 0 and n % tm == 0
    mod_map = lambda i: (i // tiles_per_mod, 0, 0)
    tok = lambda i: (i, 0)
    col = lambda i: (0, i)
    return pl.pallas_call(
        functools.partial(_mix_body, row_len=row_len),
        grid=(n // tm,),
        in_specs=[pl.BlockSpec((tm, D), tok),
                  pl.BlockSpec((1, 6, D), mod_map),
                  pl.BlockSpec((tm, D), tok),
                  pl.BlockSpec((tm, D), tok),
                  _const_spec((1, D)),
                  _const_spec((1, D)),
                  _const_spec(w_in.shape),
                  _const_spec((3, D)),
                  _const_spec((D, D)),
                  _const_spec((D, D)),
                  _const_spec((D, D)),
                  _const_spec((D, 2 * ROUTE_PAD)),
                  _const_spec((1, ROUTE_PAD))],
        out_specs=[pl.BlockSpec((tm, D), tok),
                   pl.BlockSpec((tm, D // 2), tok),
                   pl.BlockSpec((EPG, tm), col),
                   pl.BlockSpec((tm, V7X_LANES), tok),
                   pl.BlockSpec((N_EXPERTS, V7X_LANES), lambda i: (0, 0))],
        out_shape=[jax.ShapeDtypeStruct((n, D), F32),
                   jax.ShapeDtypeStruct((n, D // 2), jnp.int32),
                   jax.ShapeDtypeStruct((EPG, n), jnp.int32),
                   jax.ShapeDtypeStruct((n, V7X_LANES), F32),
                   jax.ShapeDtypeStruct((N_EXPERTS, V7X_LANES), jnp.int32)],
        scratch_shapes=[pltpu.VMEM((N_EXPERTS, tm), F32)],
        compiler_params=_params(("arbitrary",)),
        name="mixer",
    )(x, mod, hf, hb, g1, g2, w_in, cw, wco, wro, wo, wr, br)


def _sc_mesh():
    return plsc.VectorSubcoreMesh(core_axis_name="c", subcore_axis_name="s",
                                  num_cores=V7X_SC_CORES, num_subcores=V7X_SC_SUBCORES)


def _sc_worker_id():
    return lax.axis_index("s") * V7X_SC_CORES + lax.axis_index("c")


def _sc_dispatch(rows, dest, n_slots):
    n, width = rows.shape
    per_w = n // V7X_SC_WORKERS
    n_ch = per_w // SC_WINDOW
    assert n_ch * SC_WINDOW * V7X_SC_WORKERS == n
    idx = dest.reshape(2, V7X_SC_WORKERS, n_ch, SC_WINDOW).transpose(1, 0, 2, 3)

    def body(x_hbm, d_hbm, o_hbm, idx_v, rows_v):
        wid = _sc_worker_id()
        pltpu.sync_copy(d_hbm.at[wid], idx_v)
        for j in range(n_ch):
            pltpu.sync_copy(x_hbm.at[pl.ds(wid * per_w + j * SC_WINDOW, SC_WINDOW)], rows_v)
            pltpu.sync_copy(rows_v, o_hbm.at[idx_v.at[0, j]])
            pltpu.sync_copy(rows_v, o_hbm.at[idx_v.at[1, j]])

    return pl.kernel(
        body,
        out_type=jax.ShapeDtypeStruct((n_slots, width), jnp.int32),
        mesh=_sc_mesh(),
        scratch_types=[pltpu.VMEM((2, n_ch, SC_WINDOW), jnp.int32),
                       pltpu.VMEM((SC_WINDOW, width), jnp.int32)],
        name="sc_dispatch",
    )(rows, idx)


def _sc_collect(rows, dest):
    n = dest.shape[1]
    width = rows.shape[1]
    per_w = n // V7X_SC_WORKERS
    n_ch = per_w // SC_WINDOW
    assert n_ch * SC_WINDOW * V7X_SC_WORKERS == n
    idx = dest.reshape(2, V7X_SC_WORKERS, n_ch, SC_WINDOW).transpose(1, 0, 2, 3)

    def body(y_hbm, d_hbm, o_hbm, idx_v, rows_v):
        wid = _sc_worker_id()
        pltpu.sync_copy(d_hbm.at[wid], idx_v)
        for k in range(2):
            for j in range(n_ch):
                pltpu.sync_copy(y_hbm.at[idx_v.at[k, j]], rows_v)
                pltpu.sync_copy(rows_v, o_hbm.at[pl.ds(k * n + wid * per_w + j * SC_WINDOW, SC_WINDOW)])

    return pl.kernel(
        body,
        out_type=jax.ShapeDtypeStruct((2 * n, width), jnp.int32),
        mesh=_sc_mesh(),
        scratch_types=[pltpu.VMEM((2, n_ch, SC_WINDOW), jnp.int32),
                       pltpu.VMEM((SC_WINDOW, width), jnp.int32)],
        name="sc_collect",
    )(rows, idx)


def _expert_body(tr_ref, re_ref, nr_ref, nu_ref, xs_ref, w1_hbm, w3_hbm, w2_hbm, o_ref,
                 w1f, w3f, w2f, w1b, w3b, w2b, sem):
    i = pl.program_id(0)
    run = tr_ref[i]
    first = jnp.logical_or(i == 0, run != tr_ref[jnp.maximum(i - 1, 0)])

    def weight_copies(r, slot):
        e = re_ref[r]
        return (pltpu.make_async_copy(w1_hbm.at[e], w1f.at[slot], sem.at[0, slot]),
                pltpu.make_async_copy(w3_hbm.at[e], w3f.at[slot], sem.at[1, slot]),
                pltpu.make_async_copy(w2_hbm.at[e], w2f.at[slot], sem.at[2, slot]))

    @pl.when(i == 0)
    def _():
        for cp in weight_copies(0, 0):
            cp.start()

    @pl.when(first)
    def _():
        slot = run % 2
        for cp in weight_copies(run, slot):
            cp.wait()

        @pl.when(run + 1 < nr_ref[0])
        def _():
            for cp in weight_copies(run + 1, 1 - slot):
                cp.start()

        w1b[...] = w1f[slot].astype(BF16)
        w3b[...] = w3f[slot].astype(BF16)
        w2b[...] = w2f[slot].astype(BF16)

    @pl.when(i < nu_ref[0])
    def _():
        lo, hi = _unpack_halves(xs_ref[...])
        lo = lo.astype(BF16)
        hi = hi.astype(BF16)
        half = D // 2
        a = _dot(lo, w1b[0:half, :]) + _dot(hi, w1b[half:D, :])
        b = _dot(lo, w3b[0:half, :]) + _dot(hi, w3b[half:D, :])
        z = (a * _sigmoid(a)) * b
        o_ref[...] = _pack_halves(_dot(z.astype(BF16), w2b[...]))

    @pl.when(i >= nu_ref[0])
    def _():
        o_ref[...] = jnp.zeros_like(o_ref)


def _experts(tile_run, run_e, n_runs, n_used, xs, w1, w3, w2, tme):
    n_slots = xs.shape[0]
    grid_spec = pltpu.PrefetchScalarGridSpec(
        num_scalar_prefetch=4,
        grid=(n_slots // tme,),
        in_specs=[pl.BlockSpec((tme, D // 2), lambda i, *_: (i, 0)),
                  pl.BlockSpec(memory_space=pl.ANY),
                  pl.BlockSpec(memory_space=pl.ANY),
                  pl.BlockSpec(memory_space=pl.ANY)],
        out_specs=pl.BlockSpec((tme, D // 2), lambda i, *_: (i, 0)),
        scratch_shapes=[pltpu.VMEM((2, D, D_EXPERT), F32),
                        pltpu.VMEM((2, D, D_EXPERT), F32),
                        pltpu.VMEM((2, D_EXPERT, D), F32),
                        pltpu.VMEM((D, D_EXPERT), BF16),
                        pltpu.VMEM((D, D_EXPERT), BF16),
                        pltpu.VMEM((D_EXPERT, D), BF16),
                        pltpu.SemaphoreType.DMA((3, 2))],
    )
    return pl.pallas_call(
        _expert_body,
        grid_spec=grid_spec,
        out_shape=jax.ShapeDtypeStruct((n_slots, D // 2), jnp.int32),
        compiler_params=_params(("arbitrary",)),
        name="experts",
    )(tile_run, run_e, n_runs, n_used, xs, w1, w3, w2)


def _final_body(x1_ref, mod_ref, y0_ref, y1_ref, wt_ref, gf_ref, o_ref):
    w0 = wt_ref[:, 0:1]
    w1 = wt_ref[:, 1:2]
    lo0, hi0 = _unpack_halves(y0_ref[...])
    lo1, hi1 = _unpack_halves(y1_ref[...])
    moe = jnp.concatenate([w0 * lo0 + w1 * lo1, w0 * hi0 + w1 * hi1], axis=1)
    x2 = x1_ref[...] + mod_ref[0, 5:6, :] * moe
    ms = jnp.mean(x2 * x2, axis=-1, keepdims=True)
    o_ref[...] = x2 * lax.rsqrt(ms + EPS) * gf_ref[...]


def _final(x1, mod, tiles_per_mod, yg, wts, g_final):
    n = x1.shape[0]
    tm = FIN_TM
    nt = n // tm
    return pl.pallas_call(
        _final_body,
        grid=(nt,),
        in_specs=[pl.BlockSpec((tm, D), lambda i: (i, 0)),
                  pl.BlockSpec((1, 6, D), lambda i: (i // tiles_per_mod, 0, 0)),
                  pl.BlockSpec((tm, D // 2), lambda i: (i, 0)),
                  pl.BlockSpec((tm, D // 2), lambda i: (i + nt, 0)),
                  pl.BlockSpec((tm, V7X_LANES), lambda i: (i, 0)),
                  _const_spec((1, D))],
        out_specs=pl.BlockSpec((tm, D), lambda i: (i, 0)),
        out_shape=jax.ShapeDtypeStruct((n, D), F32),
        compiler_params=_params(("parallel",)),
        name="final",
    )(x1, mod, yg, yg, wts, g_final)


def _slot_plan(route, cnt, n, tme):
    counts = cnt[:, 0]
    padded = ((counts + tme - 1) // tme) * tme
    pend = jnp.cumsum(padded)
    pstart = pend - padded
    onehot = route[0:2, :, None] == jnp.arange(N_EXPERTS, dtype=jnp.int32)[None, None, :]
    dest = jnp.sum(jnp.where(onehot, pstart[None, None, :], 0), axis=-1) + route[2:4]
    n_slots = ((2 * n + N_EXPERTS * (tme - 1)) // tme) * tme
    tile_start = jnp.arange(n_slots // tme, dtype=jnp.int32) * tme
    tile_e = jnp.sum((tile_start[:, None] >= pend[None, :]).astype(jnp.int32), axis=1)
    tile_e = jnp.minimum(tile_e, N_EXPERTS - 1)
    n_used = (pend[-1] // tme).astype(jnp.int32).reshape(1)
    used = counts > 0
    run_of_e = jnp.cumsum(used.astype(jnp.int32)) - 1
    eids = jnp.arange(N_EXPERTS, dtype=jnp.int32)
    run_e = jnp.sum(jnp.where(used[None, :] & (run_of_e[None, :] == eids[:, None]), eids[None, :], 0), axis=1)
    tile_run = jnp.sum(jnp.where(tile_e[:, None] == eids[None, :], run_of_e[None, :], 0), axis=1)
    n_runs = jnp.sum(used.astype(jnp.int32)).reshape(1)
    plan = (tile_run.astype(jnp.int32), run_e.astype(jnp.int32), n_runs, n_used)
    return dest.astype(jnp.int32), plan, n_slots


def _group(x, mod, mod_per_seq, h0, p, row_len, tme):
    n_b, seq_len, _ = x.shape
    n = n_b * seq_len
    xt = x.reshape(n, D)
    xc = _xr_conv(xt, mod, mod_per_seq and seq_len == SEQ_TILE, p["g1"], p["w_xr"],
                  p["rnn_conv_w"], p["rnn_conv_b"], seq_len)
    hf, hb, last = _rglru_scan(xc.reshape(n_b, seq_len, D), h0, p["wg"], p["ba"], p["bx"], p["lam"])
    tiles_per_mod = (seq_len // MIX_TM) if mod_per_seq else (n // MIX_TM)
    x1, h2, route, wts, cnt = _mixer(xt, mod, tiles_per_mod, hf.reshape(n, D), hb.reshape(n, D),
                                     p["g1"], p["g2"], p["w_in"], p["conv_w"], p["wco"], p["wro"],
                                     p["wo"], p["wr"], p["br"], row_len)
    dest, plan, n_slots = _slot_plan(route, cnt, n, tme)
    xs = _sc_dispatch(h2, dest, n_slots)
    ys = _experts(*plan, xs, p["w1"], p["w3"], p["w2"], tme)
    yg = _sc_collect(ys, dest)
    tiles_per_mod_f = (seq_len // FIN_TM) if mod_per_seq else (n // FIN_TM)
    y = _final(x1, mod, tiles_per_mod_f, yg, wts, p["g_final"])
    return y.reshape(n_b, seq_len, D), last


def kernel(x_prompt, x_sample, state_rnn, c, c_ctx, w_ada, b_ada, g_norm1, g_norm2, w_in, conv_w, w_conv_out, rnn_conv_w, rnn_conv_b, w_gate_a, b_gate_a, w_gate_x, b_gate_x, lam, w_rnn_out, w_o, w_router_group, b_router_group, w_router_expert, b_router_expert, w1, w3, w2, g_final):
    assert w_ada.shape[0] == 1, "single layer"
    n_pb, n_sb = x_prompt.shape[0], x_sample.shape[0]

    cond = jnp.zeros((16, D), F32).at[0].set(c_ctx).at[1:1 + n_sb].set(c)
    mod = _ada(cond, w_ada[0], b_ada[0]).reshape(16, 6, D)

    w_in_b = w_in[0].astype(BF16)
    wr = jnp.zeros((D, ROUTE_PAD), F32)
    wr = wr.at[:, :N_GROUPS].set(w_router_group[0]).at[:, EPG:EPG + N_EXPERTS].set(w_router_expert[0])
    br = jnp.zeros((1, ROUTE_PAD), F32).at[0, N_GROUPS:EPG].set(NEG_BIG)
    br = br.at[0, :N_GROUPS].set(b_router_group[0]).at[0, EPG:EPG + N_EXPERTS].set(b_router_expert[0])
    wr_hi = wr.astype(BF16)
    p = dict(
        g1=g_norm1, g2=g_norm2, w_in=w_in_b, w_xr=w_in_b[:, 3 * D:4 * D],
        conv_w=conv_w[0], rnn_conv_w=rnn_conv_w[0], rnn_conv_b=rnn_conv_b,
        wg=(0.5 * jnp.concatenate([w_gate_a[0], w_gate_x[0]], axis=-1)).astype(BF16),
        ba=b_gate_a[0], bx=b_gate_x[0], lam=lam[0],
        wco=w_conv_out[0].astype(BF16), wro=w_rnn_out[0].astype(BF16), wo=w_o[0].astype(BF16),
        wr=jnp.concatenate([wr_hi, (wr - wr_hi.astype(F32)).astype(BF16)], axis=1), br=br, w1=w1[0], w3=w3[0], w2=w2[0], g_final=g_final.reshape(1, D),
    )

    h0_p = jnp.zeros((n_pb // SCAN_B, 2, SCAN_B, D), F32)
    y_prompt, last = _group(x_prompt, mod[0:1], False, h0_p, p, x_prompt.shape[1], 256)
    state_new = last.transpose(0, 2, 1, 3).reshape(n_pb, 1, 2, D)

    h0_s = state_rnn[:, 0].reshape(n_sb // SCAN_B, SCAN_B, 2, D).transpose(0, 2, 1, 3)
    y_sample, _ = _group(x_sample, mod[1:1 + n_sb], True, h0_s, p, GRID_W, 512)
    return (y_prompt, y_sample, state_new)
```

```python
import functools

import jax
import jax.numpy as jnp
from jax import lax
from jax.experimental import pallas as pl
from jax.experimental.pallas import tpu as pltpu
from jax.experimental.pallas import tpu_sc as plsc

D = 1024
N_HEADS = 8
HEAD = D // N_HEADS
GRID_W = 64
RG_C = 8.0
N_GROUPS = 4
EPG = 8
N_EXPERTS = N_GROUPS * EPG
D_EXPERT = 512
EPS = 1e-6
F32 = jnp.float32
BF16 = jnp.bfloat16

V7X_LANES = 128
V7X_SUBLANES = 8
V7X_VMEM_LIMIT_BYTES = 56 * 1024 * 1024
V7X_SC_CORES = 2
V7X_SC_SUBCORES = 16
V7X_SC_WORKERS = V7X_SC_CORES * V7X_SC_SUBCORES
SC_WINDOW = 128

SEQ_TILE = 2048
XR_CHUNK = 256
SCAN_T = 64
SCAN_B = V7X_SUBLANES
SCAN_UNROLL = 8
LOG2_E = 1.4426950408889634
TINY = 1e-30
MIX_TM = 512
FIN_TM = 512
ROUTE_PAD = 128
NEG_BIG = -1e30


def _sigmoid(x):
    return 0.5 * jnp.tanh(0.5 * x) + 0.5


def _norm_mod(x, g, scale, shift):
    ms = jnp.mean(x * x, axis=-1, keepdims=True)
    return (x * lax.rsqrt(ms + EPS) * g) * (1.0 + scale) + shift


def _dot(a, b):
    return jnp.dot(a, b, preferred_element_type=F32)


def _params(sem, vmem=V7X_VMEM_LIMIT_BYTES):
    return pltpu.CompilerParams(dimension_semantics=sem, vmem_limit_bytes=vmem)


def _const_spec(shape):
    zeros = (0,) * len(shape)
    return pl.BlockSpec(shape, lambda *_: zeros, pipeline_mode=pl.Buffered(1))


def _ada_body(c_ref, w_ref, b_ref, o_ref):
    c = c_ref[...]
    s = (c * _sigmoid(c)).astype(BF16)
    o_ref[...] = _dot(s, w_ref[...].astype(BF16)) + b_ref[...]


def _ada(cond, w_ada, b_ada):
    rows = cond.shape[0]
    n_out = w_ada.shape[1]
    return pl.pallas_call(
        _ada_body,
        grid=(n_out // D,),
        in_specs=[pl.BlockSpec((rows, D), lambda i: (0, 0)),
                  pl.BlockSpec((D, D), lambda i: (0, i)),
                  pl.BlockSpec((1, D), lambda i: (0, i))],
        out_specs=pl.BlockSpec((rows, D), lambda i: (0, i)),
        out_shape=jax.ShapeDtypeStruct((rows, n_out), F32),
        compiler_params=_params(("parallel",)),
        name="ada",
    )(cond, w_ada, b_ada.reshape(1, n_out))


def _xr_body(x_ref, mod_ref, g_ref, w_ref, cw_ref, cb_ref, o_ref, xr_s, *, seq_len):
    n = x_ref.shape[0]
    halo = V7X_SUBLANES
    ch = XR_CHUNK
    xr_s[0:halo, :] = jnp.zeros((halo, D), F32)
    xr_s[halo + n:2 * halo + n, :] = jnp.zeros((halo, D), F32)
    for r0 in range(0, n, ch):
        x = x_ref[r0:r0 + ch, :]
        h = _norm_mod(x, g_ref[...], mod_ref[0, 1:2, :], mod_ref[0, 0:1, :]).astype(BF16)
        xr_s[halo + r0:halo + r0 + ch, :] = _dot(h, w_ref[...])
    win = ch + 2 * halo
    zeros = jnp.zeros((halo, D), F32)
    for r0 in range(0, n, ch):
        if seq_len == ch:
            w = jnp.concatenate([zeros, xr_s[halo + r0:halo + r0 + ch, :], zeros], axis=0)
        else:
            w = xr_s[r0:r0 + win, :]
        y = w[halo:halo + ch] * cw_ref[2:3, :] + cb_ref[...]
        y = y + pltpu.roll(w, 2, 0)[halo:halo + ch] * cw_ref[0:1, :]
        y = y + pltpu.roll(w, 1, 0)[halo:halo + ch] * cw_ref[1:2, :]
        y = y + pltpu.roll(w, win - 1, 0)[halo:halo + ch] * cw_ref[3:4, :]
        o_ref[r0:r0 + ch, :] = y


def _xr_conv(x, mod, mod_per_tile, g1, w_xr, cw, cb, seq_len):
    n = x.shape[0]
    assert seq_len in (XR_CHUNK, SEQ_TILE) and n % SEQ_TILE == 0
    mod_map = (lambda i: (i, 0, 0)) if mod_per_tile else (lambda i: (0, 0, 0))
    return pl.pallas_call(
        functools.partial(_xr_body, seq_len=seq_len),
        grid=(n // SEQ_TILE,),
        in_specs=[pl.BlockSpec((SEQ_TILE, D), lambda i: (i, 0)),
                  pl.BlockSpec((1, 6, D), mod_map),
                  _const_spec((1, D)),
                  _const_spec((D, D)),
                  _const_spec((4, D)),
                  _const_spec((1, D))],
        out_specs=pl.BlockSpec((SEQ_TILE, D), lambda i: (i, 0)),
        out_shape=jax.ShapeDtypeStruct((n, D), F32),
        scratch_shapes=[pltpu.VMEM((SEQ_TILE + 2 * V7X_SUBLANES, D), F32)],
        compiler_params=_params(("parallel",)),
        name="xr_conv",
    )(x, mod, g1, w_xr, cw, cb)


def _scan_body(xf_ref, xb_ref, h0_ref, wg_ref, ba_ref, bx_ref, lam_ref,
               hf_ref, hb_ref, last_ref, a_s, u_s, hc):
    j = pl.program_id(1)
    n_t = pl.num_programs(1)
    t_len = xf_ref.shape[1]
    rows = SCAN_B * t_len

    @pl.when(j == 0)
    def _():
        for d in range(2):
            for s in range(N_HEADS):
                hc[d, s] = h0_ref[0, d, :, s * HEAD:(s + 1) * HEAD]

    for d, x_ref in ((0, xf_ref), (1, xb_ref)):
        z = -lam_ref[d:d + 1, :]
        sp = jnp.maximum(z, 0.0) + jnp.log(1.0 + jnp.exp(-jnp.abs(z)))
        c2 = (-0.5 * RG_C * LOG2_E) * sp
        for hd in range(N_HEADS):
            sl = slice(hd * HEAD, (hd + 1) * HEAD)
            xh = x_ref[:, :, sl].reshape(rows, HEAD)
            g = _dot(xh.astype(BF16), wg_ref[d, hd])
            t_r = jnp.tanh(g[:, :HEAD] + 0.5 * ba_ref[d:d + 1, sl])
            t_i = jnp.tanh(g[:, HEAD:] + 0.5 * bx_ref[d:d + 1, sl])
            a = jnp.exp2(c2[:, sl] * t_r + c2[:, sl])
            q = 0.25 - 0.25 * (a * a)
            half_mult = q * lax.rsqrt(jnp.maximum(q, TINY))
            u = half_mult * ((t_i + 1.0) * xh)
            for b in range(SCAN_B):
                a_s[d, hd, pl.ds(b, t_len, stride=SCAN_B), :] = a[b * t_len:(b + 1) * t_len]
                u_s[d, hd, pl.ds(b, t_len, stride=SCAN_B), :] = u[b * t_len:(b + 1) * t_len]

    def step(tb, carry):
        h_f, h_b = carry
        for k in range(SCAN_UNROLL):
            t = tb * SCAN_UNROLL + k
            rf = pl.multiple_of(t * SCAN_B, SCAN_B)
            rb = pl.multiple_of((t_len - 1 - t) * SCAN_B, SCAN_B)
            h_f = a_s[0, :, pl.ds(rf, SCAN_B), :] * h_f + u_s[0, :, pl.ds(rf, SCAN_B), :]
            h_b = a_s[1, :, pl.ds(rb, SCAN_B), :] * h_b + u_s[1, :, pl.ds(rb, SCAN_B), :]
            a_s[0, :, pl.ds(rf, SCAN_B), :] = h_f
            a_s[1, :, pl.ds(rb, SCAN_B), :] = h_b
        return h_f, h_b

    h_f, h_b = lax.fori_loop(0, t_len // SCAN_UNROLL, step, (hc[0], hc[1]))
    hc[0] = h_f
    hc[1] = h_b

    for d, o_ref in ((0, hf_ref), (1, hb_ref)):
        for hd in range(N_HEADS):
            for b in range(SCAN_B):
                o_ref[b, :, hd * HEAD:(hd + 1) * HEAD] = (
                    a_s[d, hd, pl.ds(b, t_len, stride=SCAN_B), :].astype(BF16))

    @pl.when(j == n_t - 1)
    def _():
        for d in range(2):
            for s in range(N_HEADS):
                last_ref[0, d, :, s * HEAD:(s + 1) * HEAD] = hc[d, s]


def _rglru_scan(xc, h0, wg, ba, bx, lam):
    n_b, seq_len, _ = xc.shape
    n_g = n_b // SCAN_B
    n_t = seq_len // SCAN_T
    blk = (SCAN_B, SCAN_T, D)
    return pl.pallas_call(
        _scan_body,
        grid=(n_g, n_t),
        in_specs=[pl.BlockSpec(blk, lambda g, j: (g, j, 0)),
                  pl.BlockSpec(blk, lambda g, j: (g, n_t - 1 - j, 0)),
                  pl.BlockSpec((1, 2, SCAN_B, D), lambda g, j: (g, 0, 0, 0)),
                  _const_spec((2, N_HEADS, HEAD, 2 * HEAD)),
                  _const_spec((2, D)),
                  _const_spec((2, D)),
                  _const_spec((2, D))],
        out_specs=[pl.BlockSpec(blk, lambda g, j: (g, j, 0)),
                   pl.BlockSpec(blk, lambda g, j: (g, n_t - 1 - j, 0)),
                   pl.BlockSpec((1, 2, SCAN_B, D), lambda g, j: (g, 0, 0, 0))],
        out_shape=[jax.ShapeDtypeStruct((n_b, seq_len, D), BF16),
                   jax.ShapeDtypeStruct((n_b, seq_len, D), BF16),
                   jax.ShapeDtypeStruct((n_g, 2, SCAN_B, D), F32)],
        scratch_shapes=[pltpu.VMEM((2, N_HEADS, SCAN_T * SCAN_B, HEAD), F32),
                        pltpu.VMEM((2, N_HEADS, SCAN_T * SCAN_B, HEAD), F32),
                        pltpu.VMEM((2, N_HEADS, SCAN_B, HEAD), F32)],
        compiler_params=_params(("parallel", "arbitrary")),
        name="rglru_scan",
    )(xc, xc, h0, wg, ba, bx, lam)


def _pack_halves(v):
    half = v.shape[1] // 2
    lo = lax.bitcast_convert_type(v[:, :half].astype(BF16).astype(F32), jnp.uint32)
    hi = lax.bitcast_convert_type(v[:, half:].astype(BF16).astype(F32), jnp.uint32)
    return lax.bitcast_convert_type((lo >> 16) | (hi & jnp.uint32(0xFFFF0000)), jnp.int32)


def _unpack_halves(p):
    u = lax.bitcast_convert_type(p, jnp.uint32)
    lo = lax.bitcast_convert_type(u << 16, F32)
    hi = lax.bitcast_convert_type(u & jnp.uint32(0xFFFF0000), F32)
    return lo, hi


def _mix_body(x_ref, mod_ref, hf_ref, hb_ref, g1_ref, g2_ref, win_ref, cw_ref,
              wco_ref, wro_ref, wo_ref, wr_ref, br_ref, *rest, row_len, n_cast):
    cast_in = rest[:n_cast]
    x1_ref, h2_ref, route_ref, wts_ref, cnt_ref = rest[n_cast:n_cast + 5]
    cast_out = rest[n_cast + 5:2 * n_cast + 5]
    seen = rest[2 * n_cast + 5]
    for src, dst in zip(cast_in, cast_out):
        dst[...] = src[...].astype(BF16)
    x = x_ref[...]
    tm = x.shape[0]
    h = _norm_mod(x, g1_ref[...], mod_ref[0, 1:2, :], mod_ref[0, 0:1, :]).astype(BF16)

    def proj(k):
        return _dot(h, win_ref[:, k * D:(k + 1) * D])

    cv = proj(1) * proj(2)
    pos = lax.broadcasted_iota(jnp.int32, (tm, 1), 0) % row_len
    conv = cv * cw_ref[1:2, :]
    conv = conv + jnp.where(pos >= 1, pltpu.roll(cv, 1, 0), 0.0) * cw_ref[0:1, :]
    conv = conv + jnp.where(pos <= row_len - 2, pltpu.roll(cv, tm - 1, 0), 0.0) * cw_ref[2:3, :]
    y_a = _dot((proj(0) * conv).astype(BF16), wco_ref[...])
    merged = _sigmoid(proj(5)) * y_a

    hs = hf_ref[...].astype(F32) + hb_ref[...].astype(F32)
    y_b = _dot((hs * jax.nn.gelu(proj(4))).astype(BF16), wro_ref[...])
    merged = merged + _sigmoid(proj(6)) * y_b

    mix = _dot(merged.astype(BF16), wo_ref[...])
    x1 = x + mod_ref[0, 2:3, :] * mix
    x1_ref[...] = x1
    h2 = _norm_mod(x1, g2_ref[...], mod_ref[0, 4:5, :], mod_ref[0, 3:4, :])
    h2_ref[...] = _pack_halves(h2)

    h2_hi = h2.astype(BF16)
    h2_lo = (h2 - h2_hi.astype(F32)).astype(BF16)
    big = _dot(h2_hi, wr_ref[...])
    logits = (big[:, :ROUTE_PAD] + big[:, ROUTE_PAD:]
              + _dot(h2_lo, wr_ref[:, :ROUTE_PAD]) + br_ref[...])
    lt = logits.T
    row = lax.broadcasted_iota(jnp.int32, (EPG, tm), 0)
    lg = lt[0:EPG]
    mg = jnp.max(lg, axis=0, keepdims=True)
    p_grp = 1.0 / jnp.sum(jnp.exp(lg - mg), axis=0, keepdims=True)
    grp = jnp.min(jnp.where(lg == mg, row, EPG), axis=0, keepdims=True)
    le = lt[EPG * N_GROUPS:EPG * (N_GROUPS + 1)]
    for g in range(N_GROUPS - 2, -1, -1):
        le = jnp.where(grp == g, lt[EPG * (g + 1):EPG * (g + 2)], le)
    me = jnp.max(le, axis=0, keepdims=True)
    ee = jnp.exp(le - me)
    pe = ee / jnp.sum(ee, axis=0, keepdims=True)
    p1 = jnp.max(pe, axis=0, keepdims=True)
    i1 = jnp.min(jnp.where(pe == p1, row, EPG), axis=0, keepdims=True)
    pe2 = jnp.where(row == i1, -1.0, pe)
    p2 = jnp.max(pe2, axis=0, keepdims=True)
    i2 = jnp.min(jnp.where(pe2 == p2, row, EPG), axis=0, keepdims=True)
    den = p1 + p2
    e1 = grp * EPG + i1
    e2 = grp * EPG + i2

    @pl.when(pl.program_id(0) == 0)
    def _():
        seen[...] = jnp.zeros_like(seen)

    erow = lax.broadcasted_iota(jnp.int32, (N_EXPERTS, tm), 0)
    hit1 = erow == e1
    hit2 = erow == e2
    both = jnp.where(jnp.logical_or(hit1, hit2), 1.0, 0.0)
    tri = jnp.where(lax.broadcasted_iota(jnp.int32, (tm, tm), 0)
                    <= lax.broadcasted_iota(jnp.int32, (tm, tm), 1), 1.0, 0.0).astype(BF16)
    before = _dot(both.astype(BF16), tri) - both + seen[...]
    r1 = jnp.sum(jnp.where(hit1, before, 0.0), axis=0, keepdims=True).astype(jnp.int32)
    r2 = jnp.sum(jnp.where(hit2, before, 0.0), axis=0, keepdims=True).astype(jnp.int32)
    total = seen[...] + jnp.sum(both, axis=1, keepdims=True)
    seen[...] = total
    cnt_ref[...] = total[:, :V7X_LANES].astype(jnp.int32)

    route_ref[...] = jnp.where(row == 0, e1, jnp.where(row == 1, e2, jnp.where(
        row == 2, r1, jnp.where(row == 3, r2, 0))))
    w8 = jnp.where(row == 0, p_grp * p1 / den, jnp.where(row == 1, p_grp * p2 / den, 0.0))
    wts_ref[...] = jnp.concatenate([w8, jnp.zeros((V7X_LANES - EPG, tm), F32)], axis=0).T


def _mixer(x, mod, tiles_per_mod, hf, hb, g1, g2, w_in, cw, wco, wro, wo, wr, br, row_len, cast_ws):
    n = x.shape[0]
    tm = MIX_TM
    assert tm % row_len == 0 and n % tm == 0
    steps = n // tm
    assert N_EXPERTS % steps == 0
    epb = N_EXPERTS // steps
    cast_specs = [pl.BlockSpec((epb,) + w.shape[1:], lambda i: (i, 0, 0)) for w in cast_ws]
    cast_shapes = [jax.ShapeDtypeStruct(w.shape, BF16) for w in cast_ws]
    mod_map = lambda i: (i // tiles_per_mod, 0, 0)
    tok = lambda i: (i, 0)
    col = lambda i: (0, i)
    return pl.pallas_call(
        functools.partial(_mix_body, row_len=row_len, n_cast=len(cast_ws)),
        grid=(steps,),
        in_specs=[pl.BlockSpec((tm, D), tok),
                  pl.BlockSpec((1, 6, D), mod_map),
                  pl.BlockSpec((tm, D), tok),
                  pl.BlockSpec((tm, D), tok),
                  _const_spec((1, D)),
                  _const_spec((1, D)),
                  _const_spec(w_in.shape),
                  _const_spec((3, D)),
                  _const_spec((D, D)),
                  _const_spec((D, D)),
                  _const_spec((D, D)),
                  _const_spec((D, 2 * ROUTE_PAD)),
                  _const_spec((1, ROUTE_PAD))] + cast_specs,
        out_specs=[pl.BlockSpec((tm, D), tok),
                   pl.BlockSpec((tm, D // 2), tok),
                   pl.BlockSpec((EPG, tm), col),
                   pl.BlockSpec((tm, V7X_LANES), tok),
                   pl.BlockSpec((N_EXPERTS, V7X_LANES), lambda i: (0, 0))] + cast_specs,
        out_shape=[jax.ShapeDtypeStruct((n, D), F32),
                   jax.ShapeDtypeStruct((n, D // 2), jnp.int32),
                   jax.ShapeDtypeStruct((EPG, n), jnp.int32),
                   jax.ShapeDtypeStruct((n, V7X_LANES), F32),
                   jax.ShapeDtypeStruct((N_EXPERTS, V7X_LANES), jnp.int32)] + cast_shapes,
        scratch_shapes=[pltpu.VMEM((N_EXPERTS, tm), F32)],
        compiler_params=_params(("arbitrary",)),
        name="mixer",
    )(x, mod, hf, hb, g1, g2, w_in, cw, wco, wro, wo, wr, br, *cast_ws)


def _sc_mesh():
    return plsc.VectorSubcoreMesh(core_axis_name="c", subcore_axis_name="s",
                                  num_cores=V7X_SC_CORES, num_subcores=V7X_SC_SUBCORES)


def _sc_worker_id():
    return lax.axis_index("s") * V7X_SC_CORES + lax.axis_index("c")


def _sc_dispatch(rows, dest, n_slots):
    n, width = rows.shape
    per_w = n // V7X_SC_WORKERS
    n_ch = per_w // SC_WINDOW
    assert n_ch * SC_WINDOW * V7X_SC_WORKERS == n
    idx = dest.reshape(2, V7X_SC_WORKERS, n_ch, SC_WINDOW).transpose(1, 0, 2, 3)

    def body(x_hbm, d_hbm, o_hbm, idx_v, rows_v):
        wid = _sc_worker_id()
        pltpu.sync_copy(d_hbm.at[wid], idx_v)
        for j in range(n_ch):
            pltpu.sync_copy(x_hbm.at[pl.ds(wid * per_w + j * SC_WINDOW, SC_WINDOW)], rows_v)
            pltpu.sync_copy(rows_v, o_hbm.at[idx_v.at[0, j]])
            pltpu.sync_copy(rows_v, o_hbm.at[idx_v.at[1, j]])

    return pl.kernel(
        body,
        out_type=jax.ShapeDtypeStruct((n_slots, width), jnp.int32),
        mesh=_sc_mesh(),
        scratch_types=[pltpu.VMEM((2, n_ch, SC_WINDOW), jnp.int32),
                       pltpu.VMEM((SC_WINDOW, width), jnp.int32)],
        name="sc_dispatch",
    )(rows, idx)


def _sc_collect(rows, dest):
    n = dest.shape[1]
    width = rows.shape[1]
    per_w = n // V7X_SC_WORKERS
    n_ch = per_w // SC_WINDOW
    assert n_ch * SC_WINDOW * V7X_SC_WORKERS == n
    idx = dest.reshape(2, V7X_SC_WORKERS, n_ch, SC_WINDOW).transpose(1, 0, 2, 3)

    def body(y_hbm, d_hbm, o_hbm, idx_v, rows_v):
        wid = _sc_worker_id()
        pltpu.sync_copy(d_hbm.at[wid], idx_v)
        for k in range(2):
            for j in range(n_ch):
                pltpu.sync_copy(y_hbm.at[idx_v.at[k, j]], rows_v)
                pltpu.sync_copy(rows_v, o_hbm.at[pl.ds(k * n + wid * per_w + j * SC_WINDOW, SC_WINDOW)])

    return pl.kernel(
        body,
        out_type=jax.ShapeDtypeStruct((2 * n, width), jnp.int32),
        mesh=_sc_mesh(),
        scratch_types=[pltpu.VMEM((2, n_ch, SC_WINDOW), jnp.int32),
                       pltpu.VMEM((SC_WINDOW, width), jnp.int32)],
        name="sc_collect",
    )(rows, idx)


def _expert_body(tr_ref, re_ref, nr_ref, nu_ref, xs_ref, w1_hbm, w3_hbm, w2_hbm, o_ref,
                 w1s, w3s, w2s, sem):
    i = pl.program_id(0)
    run = tr_ref[i]
    first = jnp.logical_or(i == 0, run != tr_ref[jnp.maximum(i - 1, 0)])
    slot = run % 2

    def weight_copies(r, dst_slot):
        e = re_ref[r]
        return (pltpu.make_async_copy(w1_hbm.at[e], w1s.at[dst_slot], sem.at[0, dst_slot]),
                pltpu.make_async_copy(w3_hbm.at[e], w3s.at[dst_slot], sem.at[1, dst_slot]),
                pltpu.make_async_copy(w2_hbm.at[e], w2s.at[dst_slot], sem.at[2, dst_slot]))

    @pl.when(i == 0)
    def _():
        for cp in weight_copies(0, 0):
            cp.start()

    @pl.when(first)
    def _():
        for cp in weight_copies(run, slot):
            cp.wait()

        @pl.when(run + 1 < nr_ref[0])
        def _():
            for cp in weight_copies(run + 1, 1 - slot):
                cp.start()

    @pl.when(i < nu_ref[0])
    def _():
        lo, hi = _unpack_halves(xs_ref[...])
        lo = lo.astype(BF16)
        hi = hi.astype(BF16)
        half = D // 2
        a = _dot(lo, w1s[slot, 0:half, :]) + _dot(hi, w1s[slot, half:D, :])
        b = _dot(lo, w3s[slot, 0:half, :]) + _dot(hi, w3s[slot, half:D, :])
        z = (a * _sigmoid(a)) * b
        o_ref[...] = _pack_halves(_dot(z.astype(BF16), w2s[slot]))

    @pl.when(i >= nu_ref[0])
    def _():
        o_ref[...] = jnp.zeros_like(o_ref)


def _experts(tile_run, run_e, n_runs, n_used, xs, w1, w3, w2, tme):
    n_slots = xs.shape[0]
    grid_spec = pltpu.PrefetchScalarGridSpec(
        num_scalar_prefetch=4,
        grid=(n_slots // tme,),
        in_specs=[pl.BlockSpec((tme, D // 2), lambda i, *_: (i, 0)),
                  pl.BlockSpec(memory_space=pl.ANY),
                  pl.BlockSpec(memory_space=pl.ANY),
                  pl.BlockSpec(memory_space=pl.ANY)],
        out_specs=pl.BlockSpec((tme, D // 2), lambda i, *_: (i, 0)),
        scratch_shapes=[pltpu.VMEM((2, D, D_EXPERT), BF16),
                        pltpu.VMEM((2, D, D_EXPERT), BF16),
                        pltpu.VMEM((2, D_EXPERT, D), BF16),
                        pltpu.SemaphoreType.DMA((3, 2))],
    )
    return pl.pallas_call(
        _expert_body,
        grid_spec=grid_spec,
        out_shape=jax.ShapeDtypeStruct((n_slots, D // 2), jnp.int32),
        compiler_params=_params(("arbitrary",)),
        name="experts",
    )(tile_run, run_e, n_runs, n_used, xs, w1, w3, w2)


def _final_body(x1_ref, mod_ref, y0_ref, y1_ref, wt_ref, gf_ref, o_ref):
    w0 = wt_ref[:, 0:1]
    w1 = wt_ref[:, 1:2]
    lo0, hi0 = _unpack_halves(y0_ref[...])
    lo1, hi1 = _unpack_halves(y1_ref[...])
    moe = jnp.concatenate([w0 * lo0 + w1 * lo1, w0 * hi0 + w1 * hi1], axis=1)
    x2 = x1_ref[...] + mod_ref[0, 5:6, :] * moe
    ms = jnp.mean(x2 * x2, axis=-1, keepdims=True)
    o_ref[...] = x2 * lax.rsqrt(ms + EPS) * gf_ref[...]


def _final(x1, mod, tiles_per_mod, yg, wts, g_final):
    n = x1.shape[0]
    tm = FIN_TM
    nt = n // tm
    return pl.pallas_call(
        _final_body,
        grid=(nt,),
        in_specs=[pl.BlockSpec((tm, D), lambda i: (i, 0)),
                  pl.BlockSpec((1, 6, D), lambda i: (i // tiles_per_mod, 0, 0)),
                  pl.BlockSpec((tm, D // 2), lambda i: (i, 0)),
                  pl.BlockSpec((tm, D // 2), lambda i: (i + nt, 0)),
                  pl.BlockSpec((tm, V7X_LANES), lambda i: (i, 0)),
                  _const_spec((1, D))],
        out_specs=pl.BlockSpec((tm, D), lambda i: (i, 0)),
        out_shape=jax.ShapeDtypeStruct((n, D), F32),
        compiler_params=_params(("parallel",)),
        name="final",
    )(x1, mod, yg, yg, wts, g_final)


def _slot_plan(route, cnt, n, tme):
    counts = cnt[:, 0]
    padded = ((counts + tme - 1) // tme) * tme
    pend = jnp.cumsum(padded)
    pstart = pend - padded
    onehot = route[0:2, :, None] == jnp.arange(N_EXPERTS, dtype=jnp.int32)[None, None, :]
    dest = jnp.sum(jnp.where(onehot, pstart[None, None, :], 0), axis=-1) + route[2:4]
    n_slots = ((2 * n + N_EXPERTS * (tme - 1)) // tme) * tme
    tile_start = jnp.arange(n_slots // tme, dtype=jnp.int32) * tme
    tile_e = jnp.sum((tile_start[:, None] >= pend[None, :]).astype(jnp.int32), axis=1)
    tile_e = jnp.minimum(tile_e, N_EXPERTS - 1)
    n_used = (pend[-1] // tme).astype(jnp.int32).reshape(1)
    used = counts > 0
    run_of_e = jnp.cumsum(used.astype(jnp.int32)) - 1
    eids = jnp.arange(N_EXPERTS, dtype=jnp.int32)
    run_e = jnp.sum(jnp.where(used[None, :] & (run_of_e[None, :] == eids[:, None]), eids[None, :], 0), axis=1)
    tile_run = jnp.sum(jnp.where(tile_e[:, None] == eids[None, :], run_of_e[None, :], 0), axis=1)
    n_runs = jnp.sum(used.astype(jnp.int32)).reshape(1)
    plan = (tile_run.astype(jnp.int32), run_e.astype(jnp.int32), n_runs, n_used)
    return dest.astype(jnp.int32), plan, n_slots


def _group(x, mod, mod_per_seq, h0, p, row_len, tme, cast_ws):
    n_b, seq_len, _ = x.shape
    n = n_b * seq_len
    xt = x.reshape(n, D)
    xc = _xr_conv(xt, mod, mod_per_seq and seq_len == SEQ_TILE, p["g1"], p["w_xr"],
                  p["rnn_conv_w"], p["rnn_conv_b"], seq_len)
    hf, hb, last = _rglru_scan(xc.reshape(n_b, seq_len, D), h0, p["wg"], p["ba"], p["bx"], p["lam"])
    tiles_per_mod = (seq_len // MIX_TM) if mod_per_seq else (n // MIX_TM)
    outs = _mixer(xt, mod, tiles_per_mod, hf.reshape(n, D), hb.reshape(n, D),
                                     p["g1"], p["g2"], p["w_in"], p["conv_w"], p["wco"], p["wro"],
                                     p["wo"], p["wr"], p["br"], row_len, cast_ws)
    x1, h2, route, wts, cnt = outs[:5]
    casts = outs[5:]
    dest, plan, n_slots = _slot_plan(route, cnt, n, tme)
    xs = _sc_dispatch(h2, dest, n_slots)

    def finish(w1b, w3b, w2b):
        ys = _experts(*plan, xs, w1b, w3b, w2b, tme)
        yg = _sc_collect(ys, dest)
        tiles_per_mod_f = (seq_len // FIN_TM) if mod_per_seq else (n // FIN_TM)
        y = _final(x1, mod, tiles_per_mod_f, yg, wts, p["g_final"])
        return y.reshape(n_b, seq_len, D)

    return finish, last, casts


def kernel(x_prompt, x_sample, state_rnn, c, c_ctx, w_ada, b_ada, g_norm1, g_norm2, w_in, conv_w, w_conv_out, rnn_conv_w, rnn_conv_b, w_gate_a, b_gate_a, w_gate_x, b_gate_x, lam, w_rnn_out, w_o, w_router_group, b_router_group, w_router_expert, b_router_expert, w1, w3, w2, g_final):
    assert w_ada.shape[0] == 1, "single layer"
    n_pb, n_sb = x_prompt.shape[0], x_sample.shape[0]

    cond = jnp.zeros((16, D), F32).at[0].set(c_ctx).at[1:1 + n_sb].set(c)
    mod = _ada(cond, w_ada[0], b_ada[0]).reshape(16, 6, D)

    w_in_b = w_in[0].astype(BF16)
    wr = jnp.zeros((D, ROUTE_PAD), F32)
    wr = wr.at[:, :N_GROUPS].set(w_router_group[0]).at[:, EPG:EPG + N_EXPERTS].set(w_router_expert[0])
    br = jnp.zeros((1, ROUTE_PAD), F32).at[0, N_GROUPS:EPG].set(NEG_BIG)
    br = br.at[0, :N_GROUPS].set(b_router_group[0]).at[0, EPG:EPG + N_EXPERTS].set(b_router_expert[0])
    wr_hi = wr.astype(BF16)
    p = dict(
        g1=g_norm1, g2=g_norm2, w_in=w_in_b, w_xr=w_in_b[:, 3 * D:4 * D],
        conv_w=conv_w[0], rnn_conv_w=rnn_conv_w[0], rnn_conv_b=rnn_conv_b,
        wg=(0.5 * jnp.concatenate([w_gate_a[0], w_gate_x[0]], axis=-1)).astype(BF16),
        ba=b_gate_a[0], bx=b_gate_x[0], lam=lam[0],
        wco=w_conv_out[0].astype(BF16), wro=w_rnn_out[0].astype(BF16), wo=w_o[0].astype(BF16),
        wr=jnp.concatenate([wr_hi, (wr - wr_hi.astype(F32)).astype(BF16)], axis=1), br=br,
        g_final=g_final.reshape(1, D),
    )

    h0_p = jnp.zeros((n_pb // SCAN_B, 2, SCAN_B, D), F32)
    finish_p, last, (w2b,) = _group(x_prompt, mod[0:1], False, h0_p, p, x_prompt.shape[1], 256,
                                    [w2[0]])
    state_new = last.transpose(0, 2, 1, 3).reshape(n_pb, 1, 2, D)

    h0_s = state_rnn[:, 0].reshape(n_sb // SCAN_B, SCAN_B, 2, D).transpose(0, 2, 1, 3)
    finish_s, _, (w1b, w3b) = _group(x_sample, mod[1:1 + n_sb], True, h0_s, p, GRID_W, 512,
                                     [w1[0], w3[0]])
    return (finish_p(w1b, w3b, w2b), finish_s(w1b, w3b, w2b), state_new)
```

```python
import functools

import jax
import jax.numpy as jnp
from jax import lax
from jax.experimental import pallas as pl
from jax.experimental.pallas import tpu as pltpu
from jax.experimental.pallas import tpu_sc as plsc

D = 1024
N_HEADS = 8
HEAD = D // N_HEADS
GRID_W = 64
RG_C = 8.0
N_GROUPS = 4
EPG = 8
N_EXPERTS = N_GROUPS * EPG
D_EXPERT = 512
EPS = 1e-6
F32 = jnp.float32
BF16 = jnp.bfloat16

V7X_LANES = 128
V7X_SUBLANES = 8
V7X_VMEM_LIMIT_BYTES = 56 * 1024 * 1024
V7X_SC_CORES = 2
V7X_SC_SUBCORES = 16
V7X_SC_WORKERS = V7X_SC_CORES * V7X_SC_SUBCORES
SC_WINDOW = 128

XR_T = 256
XR_SUB = 32
XR_LEFT = 2
SCAN_T = 64
SCAN_B = V7X_SUBLANES
SCAN_UNROLL = 8
LOG2_E = 1.4426950408889634
TINY = 1e-30
MIX_TM = 512
FIN_TM = 1024
ROUTE_PAD = 128
NEG_BIG = -1e30


def _sigmoid(x):
    return 0.5 * jnp.tanh(0.5 * x) + 0.5


def _norm_mod(x, g, scale, shift):
    ms = jnp.mean(x * x, axis=-1, keepdims=True)
    return (x * lax.rsqrt(ms + EPS) * g) * (1.0 + scale) + shift


def _dot(a, b):
    return jnp.dot(a, b, preferred_element_type=F32)


def _params(sem, vmem=V7X_VMEM_LIMIT_BYTES):
    return pltpu.CompilerParams(dimension_semantics=sem, vmem_limit_bytes=vmem)


def _const_spec(shape):
    zeros = (0,) * len(shape)
    return pl.BlockSpec(shape, lambda *_: zeros, pipeline_mode=pl.Buffered(1))


def _ada_body(c_ref, w_ref, b_ref, o_ref):
    c = c_ref[...]
    s = (c * _sigmoid(c)).astype(BF16)
    o_ref[...] = _dot(s, w_ref[...].astype(BF16)) + b_ref[...]


def _ada(cond, w_ada, b_ada):
    rows = cond.shape[0]
    n_out = w_ada.shape[1]
    return pl.pallas_call(
        _ada_body,
        grid=(n_out // D,),
        in_specs=[pl.BlockSpec((rows, D), lambda i: (0, 0)),
                  pl.BlockSpec((D, D), lambda i: (0, i)),
                  pl.BlockSpec((1, D), lambda i: (0, i))],
        out_specs=pl.BlockSpec((rows, D), lambda i: (0, i)),
        out_shape=jax.ShapeDtypeStruct((rows, n_out), F32),
        compiler_params=_params(("parallel",)),
        name="ada",
    )(cond, w_ada, b_ada.reshape(1, n_out))


def _xr_body(x_ref, xn_ref, mod_ref, g_ref, w_ref, cw_ref, cb_ref, o_ref, xt):
    j = pl.program_id(1)
    n_t = pl.num_programs(1)
    t_len = x_ref.shape[1]
    nb = SCAN_B
    body0 = XR_LEFT * nb

    @pl.when(j == 0)
    def _():
        xt[:, 0:body0, :] = jnp.zeros((N_HEADS, body0, HEAD), F32)

    @pl.when(j > 0)
    def _():
        xt[:, 0:body0, :] = xt[:, t_len * nb:t_len * nb + body0, :]

    g = g_ref[...]
    for b in range(nb):
        h = _norm_mod(x_ref[b], g, mod_ref[b, 1:2, :], mod_ref[b, 0:1, :]).astype(BF16)
        r = _dot(h, w_ref[...])
        for s in range(N_HEADS):
            xt[s, pl.ds(body0 + b, t_len, stride=nb), :] = r[:, s * HEAD:(s + 1) * HEAD]

    hn = _norm_mod(xn_ref[:, 0, :], g, mod_ref[:, 1, :], mod_ref[:, 0, :]).astype(BF16)
    rn = jnp.where(j < n_t - 1, _dot(hn, w_ref[...]), 0.0)
    tail = body0 + t_len * nb
    for s in range(N_HEADS):
        xt[s, tail:tail + nb, :] = rn[:, s * HEAD:(s + 1) * HEAD]

    sub = XR_SUB
    for s in range(N_HEADS):
        sl = slice(s * HEAD, (s + 1) * HEAD)
        for t0 in range(0, t_len, sub):
            y = cb_ref[:, sl]
            for k in range(4):
                r0 = (t0 + k) * nb
                y = y + xt[s, r0:r0 + sub * nb, :] * cw_ref[k:k + 1, sl]
            o_ref[t0:t0 + sub, :, sl] = y.reshape(sub, nb, HEAD)


def _xr_conv(x, mod_seq, g1, w_xr, cw, cb):
    n_b, seq_len, _ = x.shape
    t_len = min(XR_T, seq_len)
    assert n_b % SCAN_B == 0 and seq_len % t_len == 0 and t_len % XR_SUB == 0
    n_t = seq_len // t_len
    nxt = t_len // V7X_SUBLANES
    last_blk = seq_len // V7X_SUBLANES - 1
    return pl.pallas_call(
        _xr_body,
        grid=(n_b // SCAN_B, n_t),
        in_specs=[pl.BlockSpec((SCAN_B, t_len, D), lambda g, j: (g, j, 0)),
                  pl.BlockSpec((SCAN_B, V7X_SUBLANES, D),
                               lambda g, j: (g, jnp.minimum((j + 1) * nxt, last_blk), 0)),
                  pl.BlockSpec((SCAN_B, 6, D), lambda g, j: (g, 0, 0)),
                  _const_spec((1, D)),
                  _const_spec((D, D)),
                  _const_spec((4, D)),
                  _const_spec((1, D))],
        out_specs=pl.BlockSpec((t_len, SCAN_B, D), lambda g, j: (j, g, 0)),
        out_shape=jax.ShapeDtypeStruct((seq_len, n_b, D), F32),
        scratch_shapes=[pltpu.VMEM((N_HEADS, (t_len + XR_LEFT + 1) * SCAN_B, HEAD), F32)],
        compiler_params=_params(("parallel", "arbitrary")),
        name="xr_conv",
    )(x, x, mod_seq, g1, w_xr, cw, cb)


def _scan_body(xf_ref, xb_ref, h0_ref, wg_ref, ba_ref, bx_ref, lam_ref,
               hf_ref, hb_ref, last_ref, a_s, u_s, h_s, hc):
    j = pl.program_id(1)
    n_t = pl.num_programs(1)
    t_len = xf_ref.shape[0]
    rows = SCAN_B * t_len

    @pl.when(j == 0)
    def _():
        for d in range(2):
            for s in range(N_HEADS):
                hc[d, s] = h0_ref[0, d, :, s * HEAD:(s + 1) * HEAD]

    for d, x_ref in ((0, xf_ref), (1, xb_ref)):
        z = -lam_ref[d:d + 1, :]
        sp = jnp.maximum(z, 0.0) + jnp.log(1.0 + jnp.exp(-jnp.abs(z)))
        c2 = (-0.5 * RG_C * LOG2_E) * sp
        for hd in range(N_HEADS):
            sl = slice(hd * HEAD, (hd + 1) * HEAD)
            xh = x_ref[:, :, sl].reshape(rows, HEAD)
            g = _dot(xh.astype(BF16), wg_ref[d, hd])
            t_r = jnp.tanh(g[:, :HEAD] + 0.5 * ba_ref[d:d + 1, sl])
            t_i = jnp.tanh(g[:, HEAD:] + 0.5 * bx_ref[d:d + 1, sl])
            a = jnp.exp2(c2[:, sl] * t_r + c2[:, sl])
            q = 0.25 - 0.25 * (a * a)
            half_mult = q * lax.rsqrt(jnp.maximum(q, TINY))
            u = half_mult * ((t_i + 1.0) * xh)
            a_s[d, hd] = a
            u_s[d, hd] = u

    def step(tb, carry):
        h_f, h_b = carry
        for k in range(SCAN_UNROLL):
            t = tb * SCAN_UNROLL + k
            rf = pl.multiple_of(t * SCAN_B, SCAN_B)
            rb = pl.multiple_of((t_len - 1 - t) * SCAN_B, SCAN_B)
            h_f = a_s[0, :, pl.ds(rf, SCAN_B), :] * h_f + u_s[0, :, pl.ds(rf, SCAN_B), :]
            h_b = a_s[1, :, pl.ds(rb, SCAN_B), :] * h_b + u_s[1, :, pl.ds(rb, SCAN_B), :]
            h_s[0, :, pl.ds(rf, SCAN_B), :] = h_f
            h_s[1, :, pl.ds(rb, SCAN_B), :] = h_b
        return h_f, h_b

    h_f, h_b = lax.fori_loop(0, t_len // SCAN_UNROLL, step, (hc[0], hc[1]))
    hc[0] = h_f
    hc[1] = h_b

    for d, o_ref in ((0, hf_ref), (1, hb_ref)):
        for hd in range(N_HEADS):
            for b in range(SCAN_B):
                o_ref[b, :, hd * HEAD:(hd + 1) * HEAD] = (
                    h_s[d, hd, pl.ds(b, t_len, stride=SCAN_B), :].astype(BF16))

    @pl.when(j == n_t - 1)
    def _():
        for d in range(2):
            for s in range(N_HEADS):
                last_ref[0, d, :, s * HEAD:(s + 1) * HEAD] = hc[d, s]


def _rglru_scan(xc, h0, wg, ba, bx, lam):
    seq_len, n_b, _ = xc.shape
    n_g = n_b // SCAN_B
    n_t = seq_len // SCAN_T
    blk = (SCAN_B, SCAN_T, D)
    blk_in = (SCAN_T, SCAN_B, D)
    return pl.pallas_call(
        _scan_body,
        grid=(n_g, n_t),
        in_specs=[pl.BlockSpec(blk_in, lambda g, j: (j, g, 0)),
                  pl.BlockSpec(blk_in, lambda g, j: (n_t - 1 - j, g, 0)),
                  pl.BlockSpec((1, 2, SCAN_B, D), lambda g, j: (g, 0, 0, 0)),
                  _const_spec((2, N_HEADS, HEAD, 2 * HEAD)),
                  _const_spec((2, D)),
                  _const_spec((2, D)),
                  _const_spec((2, D))],
        out_specs=[pl.BlockSpec(blk, lambda g, j: (g, j, 0)),
                   pl.BlockSpec(blk, lambda g, j: (g, n_t - 1 - j, 0)),
                   pl.BlockSpec((1, 2, SCAN_B, D), lambda g, j: (g, 0, 0, 0))],
        out_shape=[jax.ShapeDtypeStruct((n_b, seq_len, D), BF16),
                   jax.ShapeDtypeStruct((n_b, seq_len, D), BF16),
                   jax.ShapeDtypeStruct((n_g, 2, SCAN_B, D), F32)],
        scratch_shapes=[pltpu.VMEM((2, N_HEADS, SCAN_T * SCAN_B, HEAD), F32),
                        pltpu.VMEM((2, N_HEADS, SCAN_T * SCAN_B, HEAD), F32),
                        pltpu.VMEM((2, N_HEADS, SCAN_T * SCAN_B, HEAD), F32),
                        pltpu.VMEM((2, N_HEADS, SCAN_B, HEAD), F32)],
        compiler_params=_params(("parallel", "arbitrary")),
        name="rglru_scan",
    )(xc, xc, h0, wg, ba, bx, lam)


def _pack_halves(v):
    half = v.shape[1] // 2
    lo = lax.bitcast_convert_type(v[:, :half].astype(BF16).astype(F32), jnp.uint32)
    hi = lax.bitcast_convert_type(v[:, half:].astype(BF16).astype(F32), jnp.uint32)
    return lax.bitcast_convert_type((lo >> 16) | (hi & jnp.uint32(0xFFFF0000)), jnp.int32)


def _unpack_halves(p):
    u = lax.bitcast_convert_type(p, jnp.uint32)
    lo = lax.bitcast_convert_type(u << 16, F32)
    hi = lax.bitcast_convert_type(u & jnp.uint32(0xFFFF0000), F32)
    return lo, hi


def _mix_body(x_ref, mod_ref, hf_ref, hb_ref, g1_ref, g2_ref, win_ref, cw_ref,
              wco_ref, wro_ref, wo_ref, wr_ref, br_ref, *rest, row_len, n_cast):
    cast_in = rest[:n_cast]
    x1_ref, h2_ref, route_ref, wts_ref, cnt_ref = rest[n_cast:n_cast + 5]
    cast_out = rest[n_cast + 5:2 * n_cast + 5]
    seen = rest[2 * n_cast + 5]
    for src, dst in zip(cast_in, cast_out):
        dst[...] = src[...].astype(BF16)
    x = x_ref[...]
    tm = x.shape[0]
    h = _norm_mod(x, g1_ref[...], mod_ref[0, 1:2, :], mod_ref[0, 0:1, :]).astype(BF16)

    def proj(k):
        return _dot(h, win_ref[:, k * D:(k + 1) * D])

    cv = proj(1) * proj(2)
    pos = lax.broadcasted_iota(jnp.int32, (tm, 1), 0) % row_len
    conv = cv * cw_ref[1:2, :]
    conv = conv + jnp.where(pos >= 1, pltpu.roll(cv, 1, 0), 0.0) * cw_ref[0:1, :]
    conv = conv + jnp.where(pos <= row_len - 2, pltpu.roll(cv, tm - 1, 0), 0.0) * cw_ref[2:3, :]
    y_a = _dot((proj(0) * conv).astype(BF16), wco_ref[...])
    merged = _sigmoid(proj(5)) * y_a

    hs = hf_ref[...].astype(F32) + hb_ref[...].astype(F32)
    y_b = _dot((hs * jax.nn.gelu(proj(4))).astype(BF16), wro_ref[...])
    merged = merged + _sigmoid(proj(6)) * y_b

    mix = _dot(merged.astype(BF16), wo_ref[...])
    x1 = x + mod_ref[0, 2:3, :] * mix
    x1_ref[...] = x1
    h2 = _norm_mod(x1, g2_ref[...], mod_ref[0, 4:5, :], mod_ref[0, 3:4, :])
    h2_ref[...] = _pack_halves(h2)

    h2_hi = h2.astype(BF16)
    h2_lo = (h2 - h2_hi.astype(F32)).astype(BF16)
    big = _dot(h2_hi, wr_ref[...])
    logits = (big[:, :ROUTE_PAD] + big[:, ROUTE_PAD:]
              + _dot(h2_lo, wr_ref[:, :ROUTE_PAD]) + br_ref[...])
    lt = logits.T
    row = lax.broadcasted_iota(jnp.int32, (EPG, tm), 0)
    lg = lt[0:EPG]
    mg = jnp.max(lg, axis=0, keepdims=True)
    p_grp = 1.0 / jnp.sum(jnp.exp(lg - mg), axis=0, keepdims=True)
    grp = jnp.min(jnp.where(lg == mg, row, EPG), axis=0, keepdims=True)
    le = lt[EPG * N_GROUPS:EPG * (N_GROUPS + 1)]
    for g in range(N_GROUPS - 2, -1, -1):
        le = jnp.where(grp == g, lt[EPG * (g + 1):EPG * (g + 2)], le)
    me = jnp.max(le, axis=0, keepdims=True)
    ee = jnp.exp(le - me)
    pe = ee / jnp.sum(ee, axis=0, keepdims=True)
    p1 = jnp.max(pe, axis=0, keepdims=True)
    i1 = jnp.min(jnp.where(pe == p1, row, EPG), axis=0, keepdims=True)
    pe2 = jnp.where(row == i1, -1.0, pe)
    p2 = jnp.max(pe2, axis=0, keepdims=True)
    i2 = jnp.min(jnp.where(pe2 == p2, row, EPG), axis=0, keepdims=True)
    den = p1 + p2
    e1 = grp * EPG + i1
    e2 = grp * EPG + i2

    @pl.when(pl.program_id(0) == 0)
    def _():
        seen[...] = jnp.zeros_like(seen)

    erow = lax.broadcasted_iota(jnp.int32, (N_EXPERTS, tm), 0)
    hit1 = erow == e1
    hit2 = erow == e2
    both = jnp.where(jnp.logical_or(hit1, hit2), 1.0, 0.0)
    tri = jnp.where(lax.broadcasted_iota(jnp.int32, (tm, tm), 0)
                    <= lax.broadcasted_iota(jnp.int32, (tm, tm), 1), 1.0, 0.0).astype(BF16)
    before = _dot(both.astype(BF16), tri) - both + seen[...]
    r1 = jnp.sum(jnp.where(hit1, before, 0.0), axis=0, keepdims=True).astype(jnp.int32)
    r2 = jnp.sum(jnp.where(hit2, before, 0.0), axis=0, keepdims=True).astype(jnp.int32)
    total = seen[...] + jnp.sum(both, axis=1, keepdims=True)
    seen[...] = total
    cnt_ref[...] = total[:, :V7X_LANES].astype(jnp.int32)

    route_ref[...] = jnp.where(row == 0, e1, jnp.where(row == 1, e2, jnp.where(
        row == 2, r1, jnp.where(row == 3, r2, 0))))
    w8 = jnp.where(row == 0, p_grp * p1 / den, jnp.where(row == 1, p_grp * p2 / den, 0.0))
    wts_ref[...] = jnp.concatenate([w8, jnp.zeros((V7X_LANES - EPG, tm), F32)], axis=0).T


def _mixer(x, mod, tiles_per_mod, hf, hb, g1, g2, w_in, cw, wco, wro, wo, wr, br, row_len, cast_ws):
    n = x.shape[0]
    tm = MIX_TM
    assert tm % row_len == 0 and n % tm == 0
    steps = n // tm
    assert N_EXPERTS % steps == 0
    epb = N_EXPERTS // steps
    cast_specs = [pl.BlockSpec((epb,) + w.shape[1:], lambda i: (i, 0, 0)) for w in cast_ws]
    cast_shapes = [jax.ShapeDtypeStruct(w.shape, BF16) for w in cast_ws]
    mod_map = lambda i: (i // tiles_per_mod, 0, 0)
    tok = lambda i: (i, 0)
    col = lambda i: (0, i)
    return pl.pallas_call(
        functools.partial(_mix_body, row_len=row_len, n_cast=len(cast_ws)),
        grid=(steps,),
        in_specs=[pl.BlockSpec((tm, D), tok),
                  pl.BlockSpec((1, 6, D), mod_map),
                  pl.BlockSpec((tm, D), tok),
                  pl.BlockSpec((tm, D), tok),
                  _const_spec((1, D)),
                  _const_spec((1, D)),
                  _const_spec(w_in.shape),
                  _const_spec((3, D)),
                  _const_spec((D, D)),
                  _const_spec((D, D)),
                  _const_spec((D, D)),
                  _const_spec((D, 2 * ROUTE_PAD)),
                  _const_spec((1, ROUTE_PAD))] + cast_specs,
        out_specs=[pl.BlockSpec((tm, D), tok),
                   pl.BlockSpec((tm, D // 2), tok),
                   pl.BlockSpec((EPG, tm), col),
                   pl.BlockSpec((tm, V7X_LANES), tok),
                   pl.BlockSpec((N_EXPERTS, V7X_LANES), lambda i: (0, 0))] + cast_specs,
        out_shape=[jax.ShapeDtypeStruct((n, D), F32),
                   jax.ShapeDtypeStruct((n, D // 2), jnp.int32),
                   jax.ShapeDtypeStruct((EPG, n), jnp.int32),
                   jax.ShapeDtypeStruct((n, V7X_LANES), F32),
                   jax.ShapeDtypeStruct((N_EXPERTS, V7X_LANES), jnp.int32)] + cast_shapes,
        scratch_shapes=[pltpu.VMEM((N_EXPERTS, tm), F32)],
        compiler_params=_params(("arbitrary",)),
        name="mixer",
    )(x, mod, hf, hb, g1, g2, w_in, cw, wco, wro, wo, wr, br, *cast_ws)


def _sc_mesh():
    return plsc.VectorSubcoreMesh(core_axis_name="c", subcore_axis_name="s",
                                  num_cores=V7X_SC_CORES, num_subcores=V7X_SC_SUBCORES)


def _sc_worker_id():
    return lax.axis_index("s") * V7X_SC_CORES + lax.axis_index("c")


def _sc_dispatch(rows, dest, n_slots):
    n, width = rows.shape
    per_w = n // V7X_SC_WORKERS
    n_ch = per_w // SC_WINDOW
    assert n_ch * SC_WINDOW * V7X_SC_WORKERS == n
    idx = dest.reshape(2, V7X_SC_WORKERS, n_ch, SC_WINDOW).transpose(1, 0, 2, 3)

    def body(x_hbm, d_hbm, o_hbm, idx_v, rows_v):
        wid = _sc_worker_id()
        pltpu.sync_copy(d_hbm.at[wid], idx_v)
        for j in range(n_ch):
            pltpu.sync_copy(x_hbm.at[pl.ds(wid * per_w + j * SC_WINDOW, SC_WINDOW)], rows_v)
            pltpu.sync_copy(rows_v, o_hbm.at[idx_v.at[0, j]])
            pltpu.sync_copy(rows_v, o_hbm.at[idx_v.at[1, j]])

    return pl.kernel(
        body,
        out_type=jax.ShapeDtypeStruct((n_slots, width), jnp.int32),
        mesh=_sc_mesh(),
        scratch_types=[pltpu.VMEM((2, n_ch, SC_WINDOW), jnp.int32),
                       pltpu.VMEM((SC_WINDOW, width), jnp.int32)],
        name="sc_dispatch",
    )(rows, idx)


def _sc_collect(rows, dest):
    n = dest.shape[1]
    width = rows.shape[1]
    per_w = n // V7X_SC_WORKERS
    n_ch = per_w // SC_WINDOW
    assert n_ch * SC_WINDOW * V7X_SC_WORKERS == n
    idx = dest.reshape(2, V7X_SC_WORKERS, n_ch, SC_WINDOW).transpose(1, 0, 2, 3)

    def body(y_hbm, d_hbm, o_hbm, idx_v, rows_v):
        wid = _sc_worker_id()
        pltpu.sync_copy(d_hbm.at[wid], idx_v)
        for k in range(2):
            for j in range(n_ch):
                pltpu.sync_copy(y_hbm.at[idx_v.at[k, j]], rows_v)
                pltpu.sync_copy(rows_v, o_hbm.at[pl.ds(k * n + wid * per_w + j * SC_WINDOW, SC_WINDOW)])

    return pl.kernel(
        body,
        out_type=jax.ShapeDtypeStruct((2 * n, width), jnp.int32),
        mesh=_sc_mesh(),
        scratch_types=[pltpu.VMEM((2, n_ch, SC_WINDOW), jnp.int32),
                       pltpu.VMEM((SC_WINDOW, width), jnp.int32)],
        name="sc_collect",
    )(rows, idx)


def _expert_body(tr_ref, re_ref, nr_ref, nu_ref, xs_ref, w1_hbm, w3_hbm, w2_hbm, o_ref,
                 w1s, w3s, w2s, sem):
    i = pl.program_id(0)
    run = tr_ref[i]
    first = jnp.logical_or(i == 0, run != tr_ref[jnp.maximum(i - 1, 0)])
    slot = run % 2

    def weight_copies(r, dst_slot):
        e = re_ref[r]
        return (pltpu.make_async_copy(w1_hbm.at[e], w1s.at[dst_slot], sem.at[0, dst_slot]),
                pltpu.make_async_copy(w3_hbm.at[e], w3s.at[dst_slot], sem.at[1, dst_slot]),
                pltpu.make_async_copy(w2_hbm.at[e], w2s.at[dst_slot], sem.at[2, dst_slot]))

    @pl.when(i == 0)
    def _():
        for cp in weight_copies(0, 0):
            cp.start()

    @pl.when(first)
    def _():
        for cp in weight_copies(run, slot):
            cp.wait()

        @pl.when(run + 1 < nr_ref[0])
        def _():
            for cp in weight_copies(run + 1, 1 - slot):
                cp.start()

    @pl.when(i < nu_ref[0])
    def _():
        lo, hi = _unpack_halves(xs_ref[...])
        lo = lo.astype(BF16)
        hi = hi.astype(BF16)
        half = D // 2
        a = _dot(lo, w1s[slot, 0:half, :]) + _dot(hi, w1s[slot, half:D, :])
        b = _dot(lo, w3s[slot, 0:half, :]) + _dot(hi, w3s[slot, half:D, :])
        z = (a * _sigmoid(a)) * b
        o_ref[...] = _pack_halves(_dot(z.astype(BF16), w2s[slot]))

    @pl.when(i >= nu_ref[0])
    def _():
        o_ref[...] = jnp.zeros_like(o_ref)


def _experts(tile_run, run_e, n_runs, n_used, xs, w1, w3, w2, tme):
    n_slots = xs.shape[0]
    grid_spec = pltpu.PrefetchScalarGridSpec(
        num_scalar_prefetch=4,
        grid=(n_slots // tme,),
        in_specs=[pl.BlockSpec((tme, D // 2), lambda i, *_: (i, 0)),
                  pl.BlockSpec(memory_space=pl.ANY),
                  pl.BlockSpec(memory_space=pl.ANY),
                  pl.BlockSpec(memory_space=pl.ANY)],
        out_specs=pl.BlockSpec((tme, D // 2), lambda i, *_: (i, 0)),
        scratch_shapes=[pltpu.VMEM((2, D, D_EXPERT), BF16),
                        pltpu.VMEM((2, D, D_EXPERT), BF16),
                        pltpu.VMEM((2, D_EXPERT, D), BF16),
                        pltpu.SemaphoreType.DMA((3, 2))],
    )
    return pl.pallas_call(
        _expert_body,
        grid_spec=grid_spec,
        out_shape=jax.ShapeDtypeStruct((n_slots, D // 2), jnp.int32),
        compiler_params=_params(("arbitrary",)),
        name="experts",
    )(tile_run, run_e, n_runs, n_used, xs, w1, w3, w2)


def _final_body(x1_ref, mod_ref, y0_ref, y1_ref, wt_ref, gf_ref, o_ref):
    w0 = wt_ref[:, 0:1]
    w1 = wt_ref[:, 1:2]
    lo0, hi0 = _unpack_halves(y0_ref[...])
    lo1, hi1 = _unpack_halves(y1_ref[...])
    moe = jnp.concatenate([w0 * lo0 + w1 * lo1, w0 * hi0 + w1 * hi1], axis=1)
    x2 = x1_ref[...] + mod_ref[0, 5:6, :] * moe
    ms = jnp.mean(x2 * x2, axis=-1, keepdims=True)
    o_ref[...] = x2 * lax.rsqrt(ms + EPS) * gf_ref[...]


def _final(x1, mod, tiles_per_mod, yg, wts, g_final):
    n = x1.shape[0]
    tm = FIN_TM
    nt = n // tm
    return pl.pallas_call(
        _final_body,
        grid=(nt,),
        in_specs=[pl.BlockSpec((tm, D), lambda i: (i, 0)),
                  pl.BlockSpec((1, 6, D), lambda i: (i // tiles_per_mod, 0, 0)),
                  pl.BlockSpec((tm, D // 2), lambda i: (i, 0)),
                  pl.BlockSpec((tm, D // 2), lambda i: (i + nt, 0)),
                  pl.BlockSpec((tm, V7X_LANES), lambda i: (i, 0)),
                  _const_spec((1, D))],
        out_specs=pl.BlockSpec((tm, D), lambda i: (i, 0)),
        out_shape=jax.ShapeDtypeStruct((n, D), F32),
        compiler_params=_params(("parallel",)),
        name="final",
    )(x1, mod, yg, yg, wts, g_final)


def _slot_plan(route, cnt, n, tme):
    counts = cnt[:, 0]
    padded = ((counts + tme - 1) // tme) * tme
    pend = jnp.cumsum(padded)
    pstart = pend - padded
    onehot = route[0:2, :, None] == jnp.arange(N_EXPERTS, dtype=jnp.int32)[None, None, :]
    dest = jnp.sum(jnp.where(onehot, pstart[None, None, :], 0), axis=-1) + route[2:4]
    n_slots = ((2 * n + N_EXPERTS * (tme - 1)) // tme) * tme
    tile_start = jnp.arange(n_slots // tme, dtype=jnp.int32) * tme
    tile_e = jnp.sum((tile_start[:, None] >= pend[None, :]).astype(jnp.int32), axis=1)
    tile_e = jnp.minimum(tile_e, N_EXPERTS - 1)
    n_used = (pend[-1] // tme).astype(jnp.int32).reshape(1)
    used = counts > 0
    run_of_e = jnp.cumsum(used.astype(jnp.int32)) - 1
    eids = jnp.arange(N_EXPERTS, dtype=jnp.int32)
    run_e = jnp.sum(jnp.where(used[None, :] & (run_of_e[None, :] == eids[:, None]), eids[None, :], 0), axis=1)
    tile_run = jnp.sum(jnp.where(tile_e[:, None] == eids[None, :], run_of_e[None, :], 0), axis=1)
    n_runs = jnp.sum(used.astype(jnp.int32)).reshape(1)
    plan = (tile_run.astype(jnp.int32), run_e.astype(jnp.int32), n_runs, n_used)
    return dest.astype(jnp.int32), plan, n_slots


def _group(x, mod, mod_per_seq, h0, p, row_len, tme, cast_ws):
    n_b, seq_len, _ = x.shape
    n = n_b * seq_len
    xt = x.reshape(n, D)
    mod_seq = mod if mod_per_seq else jnp.broadcast_to(mod, (n_b, 6, D))
    xc = _xr_conv(x, mod_seq, p["g1"], p["w_xr"], p["rnn_conv_w"], p["rnn_conv_b"])
    hf, hb, last = _rglru_scan(xc, h0, p["wg"], p["ba"], p["bx"], p["lam"])
    tiles_per_mod = (seq_len // MIX_TM) if mod_per_seq else (n // MIX_TM)
    outs = _mixer(xt, mod, tiles_per_mod, hf.reshape(n, D), hb.reshape(n, D),
                                     p["g1"], p["g2"], p["w_in"], p["conv_w"], p["wco"], p["wro"],
                                     p["wo"], p["wr"], p["br"], row_len, cast_ws)
    x1, h2, route, wts, cnt = outs[:5]
    casts = outs[5:]
    dest, plan, n_slots = _slot_plan(route, cnt, n, tme)
    xs = _sc_dispatch(h2, dest, n_slots)

    def finish(w1b, w3b, w2b):
        ys = _experts(*plan, xs, w1b, w3b, w2b, tme)
        yg = _sc_collect(ys, dest)
        tiles_per_mod_f = (seq_len // FIN_TM) if mod_per_seq else (n // FIN_TM)
        y = _final(x1, mod, tiles_per_mod_f, yg, wts, p["g_final"])
        return y.reshape(n_b, seq_len, D)

    return finish, last, casts


def kernel(x_prompt, x_sample, state_rnn, c, c_ctx, w_ada, b_ada, g_norm1, g_norm2, w_in, conv_w, w_conv_out, rnn_conv_w, rnn_conv_b, w_gate_a, b_gate_a, w_gate_x, b_gate_x, lam, w_rnn_out, w_o, w_router_group, b_router_group, w_router_expert, b_router_expert, w1, w3, w2, g_final):
    assert w_ada.shape[0] == 1, "single layer"
    n_pb, n_sb = x_prompt.shape[0], x_sample.shape[0]

    cond = jnp.zeros((16, D), F32).at[0].set(c_ctx).at[1:1 + n_sb].set(c)
    mod = _ada(cond, w_ada[0], b_ada[0]).reshape(16, 6, D)

    w_in_b = w_in[0].astype(BF16)
    wr = jnp.zeros((D, ROUTE_PAD), F32)
    wr = wr.at[:, :N_GROUPS].set(w_router_group[0]).at[:, EPG:EPG + N_EXPERTS].set(w_router_expert[0])
    br = jnp.zeros((1, ROUTE_PAD), F32).at[0, N_GROUPS:EPG].set(NEG_BIG)
    br = br.at[0, :N_GROUPS].set(b_router_group[0]).at[0, EPG:EPG + N_EXPERTS].set(b_router_expert[0])
    wr_hi = wr.astype(BF16)
    p = dict(
        g1=g_norm1, g2=g_norm2, w_in=w_in_b, w_xr=w_in_b[:, 3 * D:4 * D],
        conv_w=conv_w[0], rnn_conv_w=rnn_conv_w[0], rnn_conv_b=rnn_conv_b,
        wg=(0.5 * jnp.concatenate([w_gate_a[0], w_gate_x[0]], axis=-1)).astype(BF16),
        ba=b_gate_a[0], bx=b_gate_x[0], lam=lam[0],
        wco=w_conv_out[0].astype(BF16), wro=w_rnn_out[0].astype(BF16), wo=w_o[0].astype(BF16),
        wr=jnp.concatenate([wr_hi, (wr - wr_hi.astype(F32)).astype(BF16)], axis=1), br=br,
        g_final=g_final.reshape(1, D),
    )

    h0_p = jnp.zeros((n_pb // SCAN_B, 2, SCAN_B, D), F32)
    finish_p, last, (w2b,) = _group(x_prompt, mod[0:1], False, h0_p, p, x_prompt.shape[1], 256,
                                    [w2[0]])
    state_new = last.transpose(0, 2, 1, 3).reshape(n_pb, 1, 2, D)

    h0_s = state_rnn[:, 0].reshape(n_sb // SCAN_B, SCAN_B, 2, D).transpose(0, 2, 1, 3)
    finish_s, _, (w1b, w3b) = _group(x_sample, mod[1:1 + n_sb], True, h0_s, p, GRID_W, 512,
                                     [w1[0], w3[0]])
    return (finish_p(w1b, w3b, w2b), finish_s(w1b, w3b, w2b), state_new)
```

```python
import functools

import jax
import jax.numpy as jnp
from jax import lax
from jax.experimental import pallas as pl
from jax.experimental.pallas import tpu as pltpu
from jax.experimental.pallas import tpu_sc as plsc

D = 1024
N_HEADS = 8
HEAD = D // N_HEADS
GRID_W = 64
RG_C = 8.0
N_GROUPS = 4
EPG = 8
N_EXPERTS = N_GROUPS * EPG
D_EXPERT = 512
EPS = 1e-6
F32 = jnp.float32
BF16 = jnp.bfloat16

V7X_LANES = 128
V7X_SUBLANES = 8
V7X_VMEM_LIMIT_BYTES = 56 * 1024 * 1024
V7X_SC_CORES = 2
V7X_SC_SUBCORES = 16
V7X_SC_WORKERS = V7X_SC_CORES * V7X_SC_SUBCORES
SC_WINDOW = 128

XR_T = 256
XR_SUB = 32
XR_LEFT = 2
SCAN_T = 128
SCAN_B = V7X_SUBLANES
SCAN_UNROLL = 8
LOG2_E = 1.4426950408889634
TINY = 1e-30
MIX_TM = 512
FIN_TM = 1024
ROUTE_PAD = 128
NEG_BIG = -1e30


def _sigmoid(x):
    return 0.5 * jnp.tanh(0.5 * x) + 0.5


def _norm_mod(x, g, scale, shift):
    ms = jnp.mean(x * x, axis=-1, keepdims=True)
    return (x * lax.rsqrt(ms + EPS)) * (g * (1.0 + scale)) + shift


def _dot(a, b):
    return jnp.dot(a, b, preferred_element_type=F32)


def _params(sem, vmem=V7X_VMEM_LIMIT_BYTES):
    return pltpu.CompilerParams(dimension_semantics=sem, vmem_limit_bytes=vmem)


def _const_spec(shape):
    zeros = (0,) * len(shape)
    return pl.BlockSpec(shape, lambda *_: zeros, pipeline_mode=pl.Buffered(1))


def _ada_body(c_ref, w_ref, b_ref, o_ref):
    c = c_ref[...]
    s = (c * _sigmoid(c)).astype(BF16)
    o_ref[...] = _dot(s, w_ref[...].astype(BF16)) + b_ref[...]


def _ada(cond, w_ada, b_ada):
    rows = cond.shape[0]
    n_out = w_ada.shape[1]
    return pl.pallas_call(
        _ada_body,
        grid=(n_out // D,),
        in_specs=[pl.BlockSpec((rows, D), lambda i: (0, 0)),
                  pl.BlockSpec((D, D), lambda i: (0, i)),
                  pl.BlockSpec((1, D), lambda i: (0, i))],
        out_specs=pl.BlockSpec((rows, D), lambda i: (0, i)),
        out_shape=jax.ShapeDtypeStruct((rows, n_out), F32),
        compiler_params=_params(("parallel",)),
        name="ada",
    )(cond, w_ada, b_ada.reshape(1, n_out))


def _xr_body(x_ref, xn_ref, mod_ref, g_ref, w_ref, cw_ref, cb_ref, o_ref, xt):
    j = pl.program_id(1)
    n_t = pl.num_programs(1)
    t_len = x_ref.shape[1]
    nb = SCAN_B
    body0 = XR_LEFT * nb

    @pl.when(j == 0)
    def _():
        xt[:, 0:body0, :] = jnp.zeros((N_HEADS, body0, HEAD), F32)

    @pl.when(j > 0)
    def _():
        xt[:, 0:body0, :] = xt[:, t_len * nb:t_len * nb + body0, :]

    g = g_ref[...]
    for b in range(nb):
        h = _norm_mod(x_ref[b], g, mod_ref[b, 1:2, :], mod_ref[b, 0:1, :]).astype(BF16)
        r = _dot(h, w_ref[...])
        for s in range(N_HEADS):
            xt[s, pl.ds(body0 + b, t_len, stride=nb), :] = r[:, s * HEAD:(s + 1) * HEAD]

    hn = _norm_mod(xn_ref[:, 0, :], g, mod_ref[:, 1, :], mod_ref[:, 0, :]).astype(BF16)
    rn = jnp.where(j < n_t - 1, _dot(hn, w_ref[...]), 0.0)
    tail = body0 + t_len * nb
    for s in range(N_HEADS):
        xt[s, tail:tail + nb, :] = rn[:, s * HEAD:(s + 1) * HEAD]

    sub = XR_SUB
    for s in range(N_HEADS):
        sl = slice(s * HEAD, (s + 1) * HEAD)
        for t0 in range(0, t_len, sub):
            y = cb_ref[:, sl]
            for k in range(4):
                r0 = (t0 + k) * nb
                y = y + xt[s, r0:r0 + sub * nb, :] * cw_ref[k:k + 1, sl]
            o_ref[t0:t0 + sub, :, sl] = y.reshape(sub, nb, HEAD)


def _xr_conv(x, mod_seq, g1, w_xr, cw, cb):
    n_b, seq_len, _ = x.shape
    t_len = min(XR_T, seq_len)
    assert n_b % SCAN_B == 0 and seq_len % t_len == 0 and t_len % XR_SUB == 0
    n_t = seq_len // t_len
    nxt = t_len // V7X_SUBLANES
    last_blk = seq_len // V7X_SUBLANES - 1
    return pl.pallas_call(
        _xr_body,
        grid=(n_b // SCAN_B, n_t),
        in_specs=[pl.BlockSpec((SCAN_B, t_len, D), lambda g, j: (g, j, 0)),
                  pl.BlockSpec((SCAN_B, V7X_SUBLANES, D),
                               lambda g, j: (g, jnp.minimum((j + 1) * nxt, last_blk), 0)),
                  pl.BlockSpec((SCAN_B, 6, D), lambda g, j: (g, 0, 0)),
                  _const_spec((1, D)),
                  _const_spec((D, D)),
                  _const_spec((4, D)),
                  _const_spec((1, D))],
        out_specs=pl.BlockSpec((t_len, SCAN_B, D), lambda g, j: (j, g, 0)),
        out_shape=jax.ShapeDtypeStruct((seq_len, n_b, D), F32),
        scratch_shapes=[pltpu.VMEM((N_HEADS, (t_len + XR_LEFT + 1) * SCAN_B, HEAD), F32)],
        compiler_params=_params(("parallel", "arbitrary")),
        name="xr_conv",
    )(x, x, mod_seq, g1, w_xr, cw, cb)


def _scan_body(xf_ref, xb_ref, h0_ref, wg_ref, ba_ref, bx_ref, lam_ref,
               hf_ref, hb_ref, last_ref, a_s, u_s, h_s, hc):
    j = pl.program_id(1)
    n_t = pl.num_programs(1)
    t_len = xf_ref.shape[0]
    rows = SCAN_B * t_len

    @pl.when(j == 0)
    def _():
        for d in range(2):
            for s in range(N_HEADS):
                hc[d, s] = h0_ref[0, d, :, s * HEAD:(s + 1) * HEAD]

    for d, x_ref in ((0, xf_ref), (1, xb_ref)):
        z = -lam_ref[d:d + 1, :]
        sp = jnp.maximum(z, 0.0) + jnp.log(1.0 + jnp.exp(-jnp.abs(z)))
        c2 = (-0.5 * RG_C * LOG2_E) * sp
        for hd in range(N_HEADS):
            sl = slice(hd * HEAD, (hd + 1) * HEAD)
            xh = x_ref[:, :, sl].reshape(rows, HEAD)
            g = _dot(xh.astype(BF16), wg_ref[d, hd])
            t_r = jnp.tanh(g[:, :HEAD] + 0.5 * ba_ref[d:d + 1, sl])
            t_i = jnp.tanh(g[:, HEAD:] + 0.5 * bx_ref[d:d + 1, sl])
            a = jnp.exp2(c2[:, sl] * t_r + c2[:, sl])
            q = 0.25 - 0.25 * (a * a)
            half_mult = q * lax.rsqrt(jnp.maximum(q, TINY))
            u = half_mult * ((t_i + 1.0) * xh)
            a_s[d, hd] = a
            u_s[d, hd] = u

    def step(tb, carry):
        h_f, h_b = carry
        for k in range(SCAN_UNROLL):
            t = tb * SCAN_UNROLL + k
            rf = pl.multiple_of(t * SCAN_B, SCAN_B)
            rb = pl.multiple_of((t_len - 1 - t) * SCAN_B, SCAN_B)
            h_f = a_s[0, :, pl.ds(rf, SCAN_B), :] * h_f + u_s[0, :, pl.ds(rf, SCAN_B), :]
            h_b = a_s[1, :, pl.ds(rb, SCAN_B), :] * h_b + u_s[1, :, pl.ds(rb, SCAN_B), :]
            h_s[0, :, pl.ds(rf, SCAN_B), :] = h_f
            h_s[1, :, pl.ds(rb, SCAN_B), :] = h_b
        return h_f, h_b

    h_f, h_b = lax.fori_loop(0, t_len // SCAN_UNROLL, step, (hc[0], hc[1]))
    hc[0] = h_f
    hc[1] = h_b

    for d, o_ref in ((0, hf_ref), (1, hb_ref)):
        for hd in range(N_HEADS):
            for b in range(SCAN_B):
                o_ref[b, :, hd * HEAD:(hd + 1) * HEAD] = (
                    h_s[d, hd, pl.ds(b, t_len, stride=SCAN_B), :].astype(BF16))

    @pl.when(j == n_t - 1)
    def _():
        for d in range(2):
            for s in range(N_HEADS):
                last_ref[0, d, :, s * HEAD:(s + 1) * HEAD] = hc[d, s]


def _rglru_scan(xc, h0, wg, ba, bx, lam):
    seq_len, n_b, _ = xc.shape
    n_g = n_b // SCAN_B
    n_t = seq_len // SCAN_T
    blk = (SCAN_B, SCAN_T, D)
    blk_in = (SCAN_T, SCAN_B, D)
    return pl.pallas_call(
        _scan_body,
        grid=(n_g, n_t),
        in_specs=[pl.BlockSpec(blk_in, lambda g, j: (j, g, 0)),
                  pl.BlockSpec(blk_in, lambda g, j: (n_t - 1 - j, g, 0)),
                  pl.BlockSpec((1, 2, SCAN_B, D), lambda g, j: (g, 0, 0, 0)),
                  _const_spec((2, N_HEADS, HEAD, 2 * HEAD)),
                  _const_spec((2, D)),
                  _const_spec((2, D)),
                  _const_spec((2, D))],
        out_specs=[pl.BlockSpec(blk, lambda g, j: (g, j, 0)),
                   pl.BlockSpec(blk, lambda g, j: (g, n_t - 1 - j, 0)),
                   pl.BlockSpec((1, 2, SCAN_B, D), lambda g, j: (g, 0, 0, 0))],
        out_shape=[jax.ShapeDtypeStruct((n_b, seq_len, D), BF16),
                   jax.ShapeDtypeStruct((n_b, seq_len, D), BF16),
                   jax.ShapeDtypeStruct((n_g, 2, SCAN_B, D), F32)],
        scratch_shapes=[pltpu.VMEM((2, N_HEADS, SCAN_T * SCAN_B, HEAD), F32),
                        pltpu.VMEM((2, N_HEADS, SCAN_T * SCAN_B, HEAD), F32),
                        pltpu.VMEM((2, N_HEADS, SCAN_T * SCAN_B, HEAD), F32),
                        pltpu.VMEM((2, N_HEADS, SCAN_B, HEAD), F32)],
        compiler_params=_params(("parallel", "arbitrary")),
        name="rglru_scan",
    )(xc, xc, h0, wg, ba, bx, lam)


def _pack_halves(v):
    half = v.shape[1] // 2
    lo = lax.bitcast_convert_type(v[:, :half].astype(BF16).astype(F32), jnp.uint32)
    hi = lax.bitcast_convert_type(v[:, half:].astype(BF16).astype(F32), jnp.uint32)
    return lax.bitcast_convert_type((lo >> 16) | (hi & jnp.uint32(0xFFFF0000)), jnp.int32)


def _unpack_halves(p):
    u = lax.bitcast_convert_type(p, jnp.uint32)
    lo = lax.bitcast_convert_type(u << 16, F32)
    hi = lax.bitcast_convert_type(u & jnp.uint32(0xFFFF0000), F32)
    return lo, hi


def _mix_body(x_ref, mod_ref, hf_ref, hb_ref, g1_ref, g2_ref, win_ref, cw_ref,
              wco_ref, wro_ref, wo_ref, wr_ref, br_ref, *rest, row_len, n_cast):
    cast_in = rest[:n_cast]
    x1_ref, h2_ref, route_ref, wts_ref, cnt_ref = rest[n_cast:n_cast + 5]
    cast_out = rest[n_cast + 5:2 * n_cast + 5]
    seen = rest[2 * n_cast + 5]
    for src, dst in zip(cast_in, cast_out):
        dst[...] = src[...].astype(BF16)
    x = x_ref[...]
    tm = x.shape[0]
    h = _norm_mod(x, g1_ref[...], mod_ref[0, 1:2, :], mod_ref[0, 0:1, :]).astype(BF16)

    def proj(k):
        return _dot(h, win_ref[:, k * D:(k + 1) * D])

    cv = proj(1) * proj(2)
    pos = lax.broadcasted_iota(jnp.int32, (tm, 1), 0) % row_len
    conv = cv * cw_ref[1:2, :]
    conv = conv + jnp.where(pos >= 1, pltpu.roll(cv, 1, 0), 0.0) * cw_ref[0:1, :]
    conv = conv + jnp.where(pos <= row_len - 2, pltpu.roll(cv, tm - 1, 0), 0.0) * cw_ref[2:3, :]
    y_a = _dot((proj(0) * conv).astype(BF16), wco_ref[...])
    merged = _sigmoid(proj(5)) * y_a

    hs = hf_ref[...].astype(F32) + hb_ref[...].astype(F32)
    y_b = _dot((hs * jax.nn.gelu(proj(4))).astype(BF16), wro_ref[...])
    merged = merged + _sigmoid(proj(6)) * y_b

    mix = _dot(merged.astype(BF16), wo_ref[...])
    x1 = x + mod_ref[0, 2:3, :] * mix
    x1_ref[...] = x1
    h2 = _norm_mod(x1, g2_ref[...], mod_ref[0, 4:5, :], mod_ref[0, 3:4, :])
    h2_ref[...] = _pack_halves(h2)

    h2_hi = h2.astype(BF16)
    h2_lo = (h2 - h2_hi.astype(F32)).astype(BF16)
    big = _dot(h2_hi, wr_ref[...])
    logits = (big[:, :ROUTE_PAD] + big[:, ROUTE_PAD:]
              + _dot(h2_lo, wr_ref[:, :ROUTE_PAD]) + br_ref[...])
    lt = logits.T
    row = lax.broadcasted_iota(jnp.int32, (EPG, tm), 0)
    lg = lt[0:EPG]
    mg = jnp.max(lg, axis=0, keepdims=True)
    p_grp = 1.0 / jnp.sum(jnp.exp(lg - mg), axis=0, keepdims=True)
    grp = jnp.min(jnp.where(lg == mg, row, EPG), axis=0, keepdims=True)
    le = lt[EPG * N_GROUPS:EPG * (N_GROUPS + 1)]
    for g in range(N_GROUPS - 2, -1, -1):
        le = jnp.where(grp == g, lt[EPG * (g + 1):EPG * (g + 2)], le)
    me = jnp.max(le, axis=0, keepdims=True)
    ee = jnp.exp(le - me)
    pe = ee / jnp.sum(ee, axis=0, keepdims=True)
    p1 = jnp.max(pe, axis=0, keepdims=True)
    i1 = jnp.min(jnp.where(pe == p1, row, EPG), axis=0, keepdims=True)
    pe2 = jnp.where(row == i1, -1.0, pe)
    p2 = jnp.max(pe2, axis=0, keepdims=True)
    i2 = jnp.min(jnp.where(pe2 == p2, row, EPG), axis=0, keepdims=True)
    den = p1 + p2
    e1 = grp * EPG + i1
    e2 = grp * EPG + i2

    @pl.when(pl.program_id(0) == 0)
    def _():
        seen[...] = jnp.zeros_like(seen)

    erow = lax.broadcasted_iota(jnp.int32, (N_EXPERTS, tm), 0)
    hit1 = erow == e1
    hit2 = erow == e2
    both = jnp.where(jnp.logical_or(hit1, hit2), 1.0, 0.0)
    tri = jnp.where(lax.broadcasted_iota(jnp.int32, (tm, tm), 0)
                    <= lax.broadcasted_iota(jnp.int32, (tm, tm), 1), 1.0, 0.0).astype(BF16)
    before = _dot(both.astype(BF16), tri) - both + seen[...]
    r1 = jnp.sum(jnp.where(hit1, before, 0.0), axis=0, keepdims=True).astype(jnp.int32)
    r2 = jnp.sum(jnp.where(hit2, before, 0.0), axis=0, keepdims=True).astype(jnp.int32)
    total = seen[...] + jnp.sum(both, axis=1, keepdims=True)
    seen[...] = total
    cnt_ref[...] = total[:, :V7X_LANES].astype(jnp.int32)

    route_ref[...] = jnp.where(row == 0, e1, jnp.where(row == 1, e2, jnp.where(
        row == 2, r1, jnp.where(row == 3, r2, 0))))
    w8 = jnp.where(row == 0, p_grp * p1 / den, jnp.where(row == 1, p_grp * p2 / den, 0.0))
    wts_ref[...] = jnp.concatenate([w8, jnp.zeros((V7X_LANES - EPG, tm), F32)], axis=0).T


def _mixer(x, mod, tiles_per_mod, hf, hb, g1, g2, w_in, cw, wco, wro, wo, wr, br, row_len, cast_ws):
    n = x.shape[0]
    tm = MIX_TM
    assert tm % row_len == 0 and n % tm == 0
    steps = n // tm
    assert N_EXPERTS % steps == 0
    epb = N_EXPERTS // steps
    cast_specs = [pl.BlockSpec((epb,) + w.shape[1:], lambda i: (i, 0, 0)) for w in cast_ws]
    cast_shapes = [jax.ShapeDtypeStruct(w.shape, BF16) for w in cast_ws]
    mod_map = lambda i: (i // tiles_per_mod, 0, 0)
    tok = lambda i: (i, 0)
    col = lambda i: (0, i)
    return pl.pallas_call(
        functools.partial(_mix_body, row_len=row_len, n_cast=len(cast_ws)),
        grid=(steps,),
        in_specs=[pl.BlockSpec((tm, D), tok),
                  pl.BlockSpec((1, 6, D), mod_map),
                  pl.BlockSpec((tm, D), tok),
                  pl.BlockSpec((tm, D), tok),
                  _const_spec((1, D)),
                  _const_spec((1, D)),
                  _const_spec(w_in.shape),
                  _const_spec((3, D)),
                  _const_spec((D, D)),
                  _const_spec((D, D)),
                  _const_spec((D, D)),
                  _const_spec((D, 2 * ROUTE_PAD)),
                  _const_spec((1, ROUTE_PAD))] + cast_specs,
        out_specs=[pl.BlockSpec((tm, D), tok),
                   pl.BlockSpec((tm, D // 2), tok),
                   pl.BlockSpec((EPG, tm), col),
                   pl.BlockSpec((tm, V7X_LANES), tok),
                   pl.BlockSpec((N_EXPERTS, V7X_LANES), lambda i: (0, 0))] + cast_specs,
        out_shape=[jax.ShapeDtypeStruct((n, D), F32),
                   jax.ShapeDtypeStruct((n, D // 2), jnp.int32),
                   jax.ShapeDtypeStruct((EPG, n), jnp.int32),
                   jax.ShapeDtypeStruct((n, V7X_LANES), F32),
                   jax.ShapeDtypeStruct((N_EXPERTS, V7X_LANES), jnp.int32)] + cast_shapes,
        scratch_shapes=[pltpu.VMEM((N_EXPERTS, tm), F32)],
        compiler_params=_params(("arbitrary",)),
        name="mixer",
    )(x, mod, hf, hb, g1, g2, w_in, cw, wco, wro, wo, wr, br, *cast_ws)


def _sc_mesh():
    return plsc.VectorSubcoreMesh(core_axis_name="c", subcore_axis_name="s",
                                  num_cores=V7X_SC_CORES, num_subcores=V7X_SC_SUBCORES)


def _sc_worker_id():
    return lax.axis_index("s") * V7X_SC_CORES + lax.axis_index("c")


def _sc_dispatch(rows, dest, n_slots):
    n, width = rows.shape
    per_w = n // V7X_SC_WORKERS
    n_ch = per_w // SC_WINDOW
    assert n_ch * SC_WINDOW * V7X_SC_WORKERS == n
    idx = dest.reshape(2, V7X_SC_WORKERS, n_ch, SC_WINDOW).transpose(1, 0, 2, 3)

    def body(x_hbm, d_hbm, o_hbm, idx_v, rows_v):
        wid = _sc_worker_id()
        pltpu.sync_copy(d_hbm.at[wid], idx_v)
        for j in range(n_ch):
            pltpu.sync_copy(x_hbm.at[pl.ds(wid * per_w + j * SC_WINDOW, SC_WINDOW)], rows_v)
            pltpu.sync_copy(rows_v, o_hbm.at[idx_v.at[0, j]])
            pltpu.sync_copy(rows_v, o_hbm.at[idx_v.at[1, j]])

    return pl.kernel(
        body,
        out_type=jax.ShapeDtypeStruct((n_slots, width), jnp.int32),
        mesh=_sc_mesh(),
        scratch_types=[pltpu.VMEM((2, n_ch, SC_WINDOW), jnp.int32),
                       pltpu.VMEM((SC_WINDOW, width), jnp.int32)],
        name="sc_dispatch",
    )(rows, idx)


def _sc_collect(rows, dest):
    n = dest.shape[1]
    width = rows.shape[1]
    per_w = n // V7X_SC_WORKERS
    n_ch = per_w // SC_WINDOW
    assert n_ch * SC_WINDOW * V7X_SC_WORKERS == n
    idx = dest.reshape(2, V7X_SC_WORKERS, n_ch, SC_WINDOW).transpose(1, 0, 2, 3)

    def body(y_hbm, d_hbm, o_hbm, idx_v, rows_v):
        wid = _sc_worker_id()
        pltpu.sync_copy(d_hbm.at[wid], idx_v)
        for k in range(2):
            for j in range(n_ch):
                pltpu.sync_copy(y_hbm.at[idx_v.at[k, j]], rows_v)
                pltpu.sync_copy(rows_v, o_hbm.at[pl.ds(k * n + wid * per_w + j * SC_WINDOW, SC_WINDOW)])

    return pl.kernel(
        body,
        out_type=jax.ShapeDtypeStruct((2 * n, width), jnp.int32),
        mesh=_sc_mesh(),
        scratch_types=[pltpu.VMEM((2, n_ch, SC_WINDOW), jnp.int32),
                       pltpu.VMEM((SC_WINDOW, width), jnp.int32)],
        name="sc_collect",
    )(rows, idx)


def _expert_body(tr_ref, re_ref, nr_ref, nu_ref, xs_ref, w1_hbm, w3_hbm, w2_hbm, o_ref,
                 w1s, w3s, w2s, sem):
    i = pl.program_id(0)
    run = tr_ref[i]
    first = jnp.logical_or(i == 0, run != tr_ref[jnp.maximum(i - 1, 0)])
    slot = run % 2

    def weight_copies(r, dst_slot):
        e = re_ref[r]
        return (pltpu.make_async_copy(w1_hbm.at[e], w1s.at[dst_slot], sem.at[0, dst_slot]),
                pltpu.make_async_copy(w3_hbm.at[e], w3s.at[dst_slot], sem.at[1, dst_slot]),
                pltpu.make_async_copy(w2_hbm.at[e], w2s.at[dst_slot], sem.at[2, dst_slot]))

    @pl.when(i == 0)
    def _():
        for cp in weight_copies(0, 0):
            cp.start()

    @pl.when(first)
    def _():
        for cp in weight_copies(run, slot):
            cp.wait()

        @pl.when(run + 1 < nr_ref[0])
        def _():
            for cp in weight_copies(run + 1, 1 - slot):
                cp.start()

    @pl.when(i < nu_ref[0])
    def _():
        lo, hi = _unpack_halves(xs_ref[...])
        lo = lo.astype(BF16)
        hi = hi.astype(BF16)
        half = D // 2
        a = _dot(lo, w1s[slot, 0:half, :]) + _dot(hi, w1s[slot, half:D, :])
        b = _dot(lo, w3s[slot, 0:half, :]) + _dot(hi, w3s[slot, half:D, :])
        z = (a * _sigmoid(a)) * b
        o_ref[...] = _pack_halves(_dot(z.astype(BF16), w2s[slot]))

    @pl.when(i >= nu_ref[0])
    def _():
        o_ref[...] = jnp.zeros_like(o_ref)


def _experts(tile_run, run_e, n_runs, n_used, xs, w1, w3, w2, tme):
    n_slots = xs.shape[0]
    grid_spec = pltpu.PrefetchScalarGridSpec(
        num_scalar_prefetch=4,
        grid=(n_slots // tme,),
        in_specs=[pl.BlockSpec((tme, D // 2), lambda i, *_: (i, 0)),
                  pl.BlockSpec(memory_space=pl.ANY),
                  pl.BlockSpec(memory_space=pl.ANY),
                  pl.BlockSpec(memory_space=pl.ANY)],
        out_specs=pl.BlockSpec((tme, D // 2), lambda i, *_: (i, 0)),
        scratch_shapes=[pltpu.VMEM((2, D, D_EXPERT), BF16),
                        pltpu.VMEM((2, D, D_EXPERT), BF16),
                        pltpu.VMEM((2, D_EXPERT, D), BF16),
                        pltpu.SemaphoreType.DMA((3, 2))],
    )
    return pl.pallas_call(
        _expert_body,
        grid_spec=grid_spec,
        out_shape=jax.ShapeDtypeStruct((n_slots, D // 2), jnp.int32),
        compiler_params=_params(("arbitrary",)),
        name="experts",
    )(tile_run, run_e, n_runs, n_used, xs, w1, w3, w2)


def _final_body(x1_ref, mod_ref, y0_ref, y1_ref, wt_ref, gf_ref, o_ref):
    w0 = wt_ref[:, 0:1]
    w1 = wt_ref[:, 1:2]
    lo0, hi0 = _unpack_halves(y0_ref[...])
    lo1, hi1 = _unpack_halves(y1_ref[...])
    moe = jnp.concatenate([w0 * lo0 + w1 * lo1, w0 * hi0 + w1 * hi1], axis=1)
    x2 = x1_ref[...] + mod_ref[0, 5:6, :] * moe
    ms = jnp.mean(x2 * x2, axis=-1, keepdims=True)
    o_ref[...] = x2 * lax.rsqrt(ms + EPS) * gf_ref[...]


def _final(x1, mod, tiles_per_mod, yg, wts, g_final):
    n = x1.shape[0]
    tm = FIN_TM
    nt = n // tm
    return pl.pallas_call(
        _final_body,
        grid=(nt,),
        in_specs=[pl.BlockSpec((tm, D), lambda i: (i, 0)),
                  pl.BlockSpec((1, 6, D), lambda i: (i // tiles_per_mod, 0, 0)),
                  pl.BlockSpec((tm, D // 2), lambda i: (i, 0)),
                  pl.BlockSpec((tm, D // 2), lambda i: (i + nt, 0)),
                  pl.BlockSpec((tm, V7X_LANES), lambda i: (i, 0)),
                  _const_spec((1, D))],
        out_specs=pl.BlockSpec((tm, D), lambda i: (i, 0)),
        out_shape=jax.ShapeDtypeStruct((n, D), F32),
        compiler_params=_params(("parallel",)),
        name="final",
    )(x1, mod, yg, yg, wts, g_final)


def _slot_plan(route, cnt, n, tme):
    counts = cnt[:, 0]
    padded = ((counts + tme - 1) // tme) * tme
    pend = jnp.cumsum(padded)
    pstart = pend - padded
    onehot = route[0:2, :, None] == jnp.arange(N_EXPERTS, dtype=jnp.int32)[None, None, :]
    dest = jnp.sum(jnp.where(onehot, pstart[None, None, :], 0), axis=-1) + route[2:4]
    n_slots = ((2 * n + N_EXPERTS * (tme - 1)) // tme) * tme
    tile_start = jnp.arange(n_slots // tme, dtype=jnp.int32) * tme
    tile_e = jnp.sum((tile_start[:, None] >= pend[None, :]).astype(jnp.int32), axis=1)
    tile_e = jnp.minimum(tile_e, N_EXPERTS - 1)
    n_used = (pend[-1] // tme).astype(jnp.int32).reshape(1)
    used = counts > 0
    run_of_e = jnp.cumsum(used.astype(jnp.int32)) - 1
    eids = jnp.arange(N_EXPERTS, dtype=jnp.int32)
    run_e = jnp.sum(jnp.where(used[None, :] & (run_of_e[None, :] == eids[:, None]), eids[None, :], 0), axis=1)
    tile_run = jnp.sum(jnp.where(tile_e[:, None] == eids[None, :], run_of_e[None, :], 0), axis=1)
    n_runs = jnp.sum(used.astype(jnp.int32)).reshape(1)
    plan = (tile_run.astype(jnp.int32), run_e.astype(jnp.int32), n_runs, n_used)
    return dest.astype(jnp.int32), plan, n_slots


def _group(x, mod, mod_per_seq, h0, p, row_len, tme, cast_ws):
    n_b, seq_len, _ = x.shape
    n = n_b * seq_len
    xt = x.reshape(n, D)
    mod_seq = mod if mod_per_seq else jnp.broadcast_to(mod, (n_b, 6, D))
    xc = _xr_conv(x, mod_seq, p["g1"], p["w_xr"], p["rnn_conv_w"], p["rnn_conv_b"])
    hf, hb, last = _rglru_scan(xc, h0, p["wg"], p["ba"], p["bx"], p["lam"])
    tiles_per_mod = (seq_len // MIX_TM) if mod_per_seq else (n // MIX_TM)
    outs = _mixer(xt, mod, tiles_per_mod, hf.reshape(n, D), hb.reshape(n, D),
                                     p["g1"], p["g2"], p["w_in"], p["conv_w"], p["wco"], p["wro"],
                                     p["wo"], p["wr"], p["br"], row_len, cast_ws)
    x1, h2, route, wts, cnt = outs[:5]
    casts = outs[5:]
    dest, plan, n_slots = _slot_plan(route, cnt, n, tme)
    xs = _sc_dispatch(h2, dest, n_slots)

    def finish(w1b, w3b, w2b):
        ys = _experts(*plan, xs, w1b, w3b, w2b, tme)
        yg = _sc_collect(ys, dest)
        tiles_per_mod_f = (seq_len // FIN_TM) if mod_per_seq else (n // FIN_TM)
        y = _final(x1, mod, tiles_per_mod_f, yg, wts, p["g_final"])
        return y.reshape(n_b, seq_len, D)

    return finish, last, casts


def kernel(x_prompt, x_sample, state_rnn, c, c_ctx, w_ada, b_ada, g_norm1, g_norm2, w_in, conv_w, w_conv_out, rnn_conv_w, rnn_conv_b, w_gate_a, b_gate_a, w_gate_x, b_gate_x, lam, w_rnn_out, w_o, w_router_group, b_router_group, w_router_expert, b_router_expert, w1, w3, w2, g_final):
    assert w_ada.shape[0] == 1, "single layer"
    n_pb, n_sb = x_prompt.shape[0], x_sample.shape[0]

    cond = jnp.zeros((16, D), F32).at[0].set(c_ctx).at[1:1 + n_sb].set(c)
    mod = _ada(cond, w_ada[0], b_ada[0]).reshape(16, 6, D)

    w_in_b = w_in[0].astype(BF16)
    wr = jnp.zeros((D, ROUTE_PAD), F32)
    wr = wr.at[:, :N_GROUPS].set(w_router_group[0]).at[:, EPG:EPG + N_EXPERTS].set(w_router_expert[0])
    br = jnp.zeros((1, ROUTE_PAD), F32).at[0, N_GROUPS:EPG].set(NEG_BIG)
    br = br.at[0, :N_GROUPS].set(b_router_group[0]).at[0, EPG:EPG + N_EXPERTS].set(b_router_expert[0])
    wr_hi = wr.astype(BF16)
    p = dict(
        g1=g_norm1, g2=g_norm2, w_in=w_in_b, w_xr=w_in_b[:, 3 * D:4 * D],
        conv_w=conv_w[0], rnn_conv_w=rnn_conv_w[0], rnn_conv_b=rnn_conv_b,
        wg=(0.5 * jnp.concatenate([w_gate_a[0], w_gate_x[0]], axis=-1)).astype(BF16),
        ba=b_gate_a[0], bx=b_gate_x[0], lam=lam[0],
        wco=w_conv_out[0].astype(BF16), wro=w_rnn_out[0].astype(BF16), wo=w_o[0].astype(BF16),
        wr=jnp.concatenate([wr_hi, (wr - wr_hi.astype(F32)).astype(BF16)], axis=1), br=br,
        g_final=g_final.reshape(1, D),
    )

    h0_p = jnp.zeros((n_pb // SCAN_B, 2, SCAN_B, D), F32)
    finish_p, last, (w2b,) = _group(x_prompt, mod[0:1], False, h0_p, p, x_prompt.shape[1], 256,
                                    [w2[0]])
    state_new = last.transpose(0, 2, 1, 3).reshape(n_pb, 1, 2, D)

    h0_s = state_rnn[:, 0].reshape(n_sb // SCAN_B, SCAN_B, 2, D).transpose(0, 2, 1, 3)
    finish_s, _, (w1b, w3b) = _group(x_sample, mod[1:1 + n_sb], True, h0_s, p, GRID_W, 256,
                                     [w1[0], w3[0]])
    return (finish_p(w1b, w3b, w2b), finish_s(w1b, w3b, w2b), state_new)
```

```python
import functools

import jax
import jax.numpy as jnp
from jax import lax
from jax.experimental import pallas as pl
from jax.experimental.pallas import tpu as pltpu
from jax.experimental.pallas import tpu_sc as plsc

D = 1024
N_HEADS = 8
HEAD = D // N_HEADS
GRID_W = 64
RG_C = 8.0
N_GROUPS = 4
EPG = 8
N_EXPERTS = N_GROUPS * EPG
D_EXPERT = 512
EPS = 1e-6
F32 = jnp.float32
BF16 = jnp.bfloat16

V7X_LANES = 128
V7X_SUBLANES = 8
V7X_VMEM_LIMIT_BYTES = 56 * 1024 * 1024
V7X_SC_CORES = 2
V7X_SC_SUBCORES = 16
V7X_SC_WORKERS = V7X_SC_CORES * V7X_SC_SUBCORES
SC_WINDOW = 64

XR_T = 256
XR_SUB = 32
XR_LEFT = 2
SCAN_T = 128
SCAN_B = V7X_SUBLANES
SCAN_UNROLL = 8
LOG2_E = 1.4426950408889634
TINY = 1e-30
MIX_TM = 512
FIN_TM = 1024
ROUTE_PAD = 128
NEG_BIG = -1e30


def _sigmoid(x):
    return 0.5 * jnp.tanh(0.5 * x) + 0.5


def _norm_mod(x, g, scale, shift):
    ms = jnp.mean(x * x, axis=-1, keepdims=True)
    return (x * lax.rsqrt(ms + EPS)) * (g * (1.0 + scale)) + shift


def _dot(a, b):
    return jnp.dot(a, b, preferred_element_type=F32)


def _params(sem, vmem=V7X_VMEM_LIMIT_BYTES):
    return pltpu.CompilerParams(dimension_semantics=sem, vmem_limit_bytes=vmem)


def _const_spec(shape):
    zeros = (0,) * len(shape)
    return pl.BlockSpec(shape, lambda *_: zeros, pipeline_mode=pl.Buffered(1))


def _ada_body(c_ref, w_ref, b_ref, o_ref):
    c = c_ref[...]
    s = (c * _sigmoid(c)).astype(BF16)
    o_ref[...] = _dot(s, w_ref[...].astype(BF16)) + b_ref[...]


def _ada(cond, w_ada, b_ada):
    rows = cond.shape[0]
    n_out = w_ada.shape[1]
    return pl.pallas_call(
        _ada_body,
        grid=(n_out // D,),
        in_specs=[pl.BlockSpec((rows, D), lambda i: (0, 0)),
                  pl.BlockSpec((D, D), lambda i: (0, i)),
                  pl.BlockSpec((1, D), lambda i: (0, i))],
        out_specs=pl.BlockSpec((rows, D), lambda i: (0, i)),
        out_shape=jax.ShapeDtypeStruct((rows, n_out), F32),
        compiler_params=_params(("parallel",)),
        name="ada",
    )(cond, w_ada, b_ada.reshape(1, n_out))


def _xr_body(x_ref, xn_ref, mod_ref, g_ref, w_ref, cw_ref, cb_ref, o_ref, xt):
    j = pl.program_id(1)
    n_t = pl.num_programs(1)
    t_len = x_ref.shape[1]
    nb = SCAN_B
    body0 = XR_LEFT * nb

    @pl.when(j == 0)
    def _():
        xt[:, 0:body0, :] = jnp.zeros((N_HEADS, body0, HEAD), F32)

    @pl.when(j > 0)
    def _():
        xt[:, 0:body0, :] = xt[:, t_len * nb:t_len * nb + body0, :]

    g = g_ref[...]
    for b in range(nb):
        h = _norm_mod(x_ref[b], g, mod_ref[b, 1:2, :], mod_ref[b, 0:1, :]).astype(BF16)
        r = _dot(h, w_ref[...])
        for s in range(N_HEADS):
            xt[s, pl.ds(body0 + b, t_len, stride=nb), :] = r[:, s * HEAD:(s + 1) * HEAD]

    hn = _norm_mod(xn_ref[:, 0, :], g, mod_ref[:, 1, :], mod_ref[:, 0, :]).astype(BF16)
    rn = jnp.where(j < n_t - 1, _dot(hn, w_ref[...]), 0.0)
    tail = body0 + t_len * nb
    for s in range(N_HEADS):
        xt[s, tail:tail + nb, :] = rn[:, s * HEAD:(s + 1) * HEAD]

    sub = XR_SUB
    for s in range(N_HEADS):
        sl = slice(s * HEAD, (s + 1) * HEAD)
        for t0 in range(0, t_len, sub):
            y = cb_ref[:, sl]
            for k in range(4):
                r0 = (t0 + k) * nb
                y = y + xt[s, r0:r0 + sub * nb, :] * cw_ref[k:k + 1, sl]
            o_ref[t0:t0 + sub, :, sl] = y.reshape(sub, nb, HEAD)


def _xr_conv(x, mod_seq, g1, w_xr, cw, cb):
    n_b, seq_len, _ = x.shape
    t_len = min(XR_T, seq_len)
    assert n_b % SCAN_B == 0 and seq_len % t_len == 0 and t_len % XR_SUB == 0
    n_t = seq_len // t_len
    nxt = t_len // V7X_SUBLANES
    last_blk = seq_len // V7X_SUBLANES - 1
    return pl.pallas_call(
        _xr_body,
        grid=(n_b // SCAN_B, n_t),
        in_specs=[pl.BlockSpec((SCAN_B, t_len, D), lambda g, j: (g, j, 0)),
                  pl.BlockSpec((SCAN_B, V7X_SUBLANES, D),
                               lambda g, j: (g, jnp.minimum((j + 1) * nxt, last_blk), 0)),
                  pl.BlockSpec((SCAN_B, 6, D), lambda g, j: (g, 0, 0)),
                  _const_spec((1, D)),
                  _const_spec((D, D)),
                  _const_spec((4, D)),
                  _const_spec((1, D))],
        out_specs=pl.BlockSpec((t_len, SCAN_B, D), lambda g, j: (j, g, 0)),
        out_shape=jax.ShapeDtypeStruct((seq_len, n_b, D), F32),
        scratch_shapes=[pltpu.VMEM((N_HEADS, (t_len + XR_LEFT + 1) * SCAN_B, HEAD), F32)],
        compiler_params=_params(("parallel", "arbitrary")),
        name="xr_conv",
    )(x, x, mod_seq, g1, w_xr, cw, cb)


def _scan_body(xf_ref, xb_ref, h0_ref, wg_ref, ba_ref, bx_ref, lam_ref,
               hf_ref, hb_ref, last_ref, a_s, u_s, h_s, hc):
    j = pl.program_id(1)
    n_t = pl.num_programs(1)
    t_len = xf_ref.shape[0]
    rows = SCAN_B * t_len

    @pl.when(j == 0)
    def _():
        for d in range(2):
            for s in range(N_HEADS):
                hc[d, s] = h0_ref[0, d, :, s * HEAD:(s + 1) * HEAD]

    for d, x_ref in ((0, xf_ref), (1, xb_ref)):
        z = -lam_ref[d:d + 1, :]
        sp = jnp.maximum(z, 0.0) + jnp.log(1.0 + jnp.exp(-jnp.abs(z)))
        c2 = (-0.5 * RG_C * LOG2_E) * sp
        for hd in range(N_HEADS):
            sl = slice(hd * HEAD, (hd + 1) * HEAD)
            xh = x_ref[:, :, sl].reshape(rows, HEAD)
            g = _dot(xh.astype(BF16), wg_ref[d, hd])
            t_r = jnp.tanh(g[:, :HEAD] + 0.5 * ba_ref[d:d + 1, sl])
            t_i = jnp.tanh(g[:, HEAD:] + 0.5 * bx_ref[d:d + 1, sl])
            a = jnp.exp2(c2[:, sl] * t_r + c2[:, sl])
            q = 0.25 - 0.25 * (a * a)
            half_mult = q * lax.rsqrt(jnp.maximum(q, TINY))
            u = half_mult * ((t_i + 1.0) * xh)
            a_s[d, hd] = a
            u_s[d, hd] = u

    def step(tb, carry):
        h_f, h_b = carry
        for k in range(SCAN_UNROLL):
            t = tb * SCAN_UNROLL + k
            rf = pl.multiple_of(t * SCAN_B, SCAN_B)
            rb = pl.multiple_of((t_len - 1 - t) * SCAN_B, SCAN_B)
            h_f = a_s[0, :, pl.ds(rf, SCAN_B), :] * h_f + u_s[0, :, pl.ds(rf, SCAN_B), :]
            h_b = a_s[1, :, pl.ds(rb, SCAN_B), :] * h_b + u_s[1, :, pl.ds(rb, SCAN_B), :]
            h_s[0, :, pl.ds(rf, SCAN_B), :] = h_f
            h_s[1, :, pl.ds(rb, SCAN_B), :] = h_b
        return h_f, h_b

    h_f, h_b = lax.fori_loop(0, t_len // SCAN_UNROLL, step, (hc[0], hc[1]))
    hc[0] = h_f
    hc[1] = h_b

    for d, o_ref in ((0, hf_ref), (1, hb_ref)):
        for hd in range(N_HEADS):
            for b in range(SCAN_B):
                o_ref[b, :, hd * HEAD:(hd + 1) * HEAD] = (
                    h_s[d, hd, pl.ds(b, t_len, stride=SCAN_B), :].astype(BF16))

    @pl.when(j == n_t - 1)
    def _():
        for d in range(2):
            for s in range(N_HEADS):
                last_ref[0, d, :, s * HEAD:(s + 1) * HEAD] = hc[d, s]


def _rglru_scan(xc, h0, wg, ba, bx, lam):
    seq_len, n_b, _ = xc.shape
    n_g = n_b // SCAN_B
    n_t = seq_len // SCAN_T
    blk = (SCAN_B, SCAN_T, D)
    blk_in = (SCAN_T, SCAN_B, D)
    return pl.pallas_call(
        _scan_body,
        grid=(n_g, n_t),
        in_specs=[pl.BlockSpec(blk_in, lambda g, j: (j, g, 0)),
                  pl.BlockSpec(blk_in, lambda g, j: (n_t - 1 - j, g, 0)),
                  pl.BlockSpec((1, 2, SCAN_B, D), lambda g, j: (g, 0, 0, 0)),
                  _const_spec((2, N_HEADS, HEAD, 2 * HEAD)),
                  _const_spec((2, D)),
                  _const_spec((2, D)),
                  _const_spec((2, D))],
        out_specs=[pl.BlockSpec(blk, lambda g, j: (g, j, 0)),
                   pl.BlockSpec(blk, lambda g, j: (g, n_t - 1 - j, 0)),
                   pl.BlockSpec((1, 2, SCAN_B, D), lambda g, j: (g, 0, 0, 0))],
        out_shape=[jax.ShapeDtypeStruct((n_b, seq_len, D), BF16),
                   jax.ShapeDtypeStruct((n_b, seq_len, D), BF16),
                   jax.ShapeDtypeStruct((n_g, 2, SCAN_B, D), F32)],
        scratch_shapes=[pltpu.VMEM((2, N_HEADS, SCAN_T * SCAN_B, HEAD), F32),
                        pltpu.VMEM((2, N_HEADS, SCAN_T * SCAN_B, HEAD), F32),
                        pltpu.VMEM((2, N_HEADS, SCAN_T * SCAN_B, HEAD), F32),
                        pltpu.VMEM((2, N_HEADS, SCAN_B, HEAD), F32)],
        compiler_params=_params(("parallel", "arbitrary")),
        name="rglru_scan",
    )(xc, xc, h0, wg, ba, bx, lam)


def _pack_halves(v):
    half = v.shape[1] // 2
    lo = lax.bitcast_convert_type(v[:, :half].astype(BF16).astype(F32), jnp.uint32)
    hi = lax.bitcast_convert_type(v[:, half:].astype(BF16).astype(F32), jnp.uint32)
    return lax.bitcast_convert_type((lo >> 16) | (hi & jnp.uint32(0xFFFF0000)), jnp.int32)


def _unpack_halves(p):
    u = lax.bitcast_convert_type(p, jnp.uint32)
    lo = lax.bitcast_convert_type(u << 16, F32)
    hi = lax.bitcast_convert_type(u & jnp.uint32(0xFFFF0000), F32)
    return lo, hi


def _mix_body(x_ref, mod_ref, hf_ref, hb_ref, g1_ref, g2_ref, win_ref, cw_ref,
              wco_ref, wro_ref, wo_ref, wr_ref, br_ref, *rest, row_len, n_cast):
    cast_in = rest[:n_cast]
    x1_ref, h2_ref, route_ref, wts_ref, cnt_ref = rest[n_cast:n_cast + 5]
    cast_out = rest[n_cast + 5:2 * n_cast + 5]
    seen = rest[2 * n_cast + 5]
    for src, dst in zip(cast_in, cast_out):
        dst[...] = src[...].astype(BF16)
    x = x_ref[...]
    tm = x.shape[0]
    h = _norm_mod(x, g1_ref[...], mod_ref[0, 1:2, :], mod_ref[0, 0:1, :]).astype(BF16)

    def proj(k):
        return _dot(h, win_ref[:, k * D:(k + 1) * D])

    cv = proj(1) * proj(2)
    pos = lax.broadcasted_iota(jnp.int32, (tm, 1), 0) % row_len
    conv = cv * cw_ref[1:2, :]
    conv = conv + jnp.where(pos >= 1, pltpu.roll(cv, 1, 0), 0.0) * cw_ref[0:1, :]
    conv = conv + jnp.where(pos <= row_len - 2, pltpu.roll(cv, tm - 1, 0), 0.0) * cw_ref[2:3, :]
    y_a = _dot((proj(0) * conv).astype(BF16), wco_ref[...])
    merged = _sigmoid(proj(5)) * y_a

    hs = hf_ref[...].astype(F32) + hb_ref[...].astype(F32)
    y_b = _dot((hs * jax.nn.gelu(proj(4))).astype(BF16), wro_ref[...])
    merged = merged + _sigmoid(proj(6)) * y_b

    mix = _dot(merged.astype(BF16), wo_ref[...])
    x1 = x + mod_ref[0, 2:3, :] * mix
    x1_ref[...] = x1
    h2 = _norm_mod(x1, g2_ref[...], mod_ref[0, 4:5, :], mod_ref[0, 3:4, :])
    h2_ref[...] = _pack_halves(h2)

    h2_hi = h2.astype(BF16)
    h2_lo = (h2 - h2_hi.astype(F32)).astype(BF16)
    big = _dot(h2_hi, wr_ref[...])
    logits = (big[:, :ROUTE_PAD] + big[:, ROUTE_PAD:]
              + _dot(h2_lo, wr_ref[:, :ROUTE_PAD]) + br_ref[...])
    lt = logits.T
    row = lax.broadcasted_iota(jnp.int32, (EPG, tm), 0)
    lg = lt[0:EPG]
    mg = jnp.max(lg, axis=0, keepdims=True)
    p_grp = 1.0 / jnp.sum(jnp.exp(lg - mg), axis=0, keepdims=True)
    grp = jnp.min(jnp.where(lg == mg, row, EPG), axis=0, keepdims=True)
    le = lt[EPG * N_GROUPS:EPG * (N_GROUPS + 1)]
    for g in range(N_GROUPS - 2, -1, -1):
        le = jnp.where(grp == g, lt[EPG * (g + 1):EPG * (g + 2)], le)
    me = jnp.max(le, axis=0, keepdims=True)
    ee = jnp.exp(le - me)
    pe = ee / jnp.sum(ee, axis=0, keepdims=True)
    p1 = jnp.max(pe, axis=0, keepdims=True)
    i1 = jnp.min(jnp.where(pe == p1, row, EPG), axis=0, keepdims=True)
    pe2 = jnp.where(row == i1, -1.0, pe)
    p2 = jnp.max(pe2, axis=0, keepdims=True)
    i2 = jnp.min(jnp.where(pe2 == p2, row, EPG), axis=0, keepdims=True)
    den = p1 + p2
    e1 = grp * EPG + i1
    e2 = grp * EPG + i2

    @pl.when(pl.program_id(0) == 0)
    def _():
        seen[...] = jnp.zeros_like(seen)

    erow = lax.broadcasted_iota(jnp.int32, (N_EXPERTS, tm), 0)
    hit1 = erow == e1
    hit2 = erow == e2
    both = jnp.where(jnp.logical_or(hit1, hit2), 1.0, 0.0)
    tri = jnp.where(lax.broadcasted_iota(jnp.int32, (tm, tm), 0)
                    <= lax.broadcasted_iota(jnp.int32, (tm, tm), 1), 1.0, 0.0).astype(BF16)
    before = _dot(both.astype(BF16), tri) - both + seen[...]
    r1 = jnp.sum(jnp.where(hit1, before, 0.0), axis=0, keepdims=True).astype(jnp.int32)
    r2 = jnp.sum(jnp.where(hit2, before, 0.0), axis=0, keepdims=True).astype(jnp.int32)
    total = seen[...] + jnp.sum(both, axis=1, keepdims=True)
    seen[...] = total
    cnt_ref[...] = total[:, :V7X_LANES].astype(jnp.int32)

    route_ref[...] = jnp.where(row == 0, e1, jnp.where(row == 1, e2, jnp.where(
        row == 2, r1, jnp.where(row == 3, r2, 0))))
    w8 = jnp.where(row == 0, p_grp * p1 / den, jnp.where(row == 1, p_grp * p2 / den, 0.0))
    wts_ref[...] = jnp.concatenate([w8, jnp.zeros((V7X_LANES - EPG, tm), F32)], axis=0).T


def _mixer(x, mod, tiles_per_mod, hf, hb, g1, g2, w_in, cw, wco, wro, wo, wr, br, row_len, cast_ws):
    n = x.shape[0]
    tm = MIX_TM
    assert tm % row_len == 0 and n % tm == 0
    steps = n // tm
    assert N_EXPERTS % steps == 0
    epb = N_EXPERTS // steps
    cast_specs = [pl.BlockSpec((epb,) + w.shape[1:], lambda i: (i, 0, 0)) for w in cast_ws]
    cast_shapes = [jax.ShapeDtypeStruct(w.shape, BF16) for w in cast_ws]
    mod_map = lambda i: (i // tiles_per_mod, 0, 0)
    tok = lambda i: (i, 0)
    col = lambda i: (0, i)
    return pl.pallas_call(
        functools.partial(_mix_body, row_len=row_len, n_cast=len(cast_ws)),
        grid=(steps,),
        in_specs=[pl.BlockSpec((tm, D), tok),
                  pl.BlockSpec((1, 6, D), mod_map),
                  pl.BlockSpec((tm, D), tok),
                  pl.BlockSpec((tm, D), tok),
                  _const_spec((1, D)),
                  _const_spec((1, D)),
                  _const_spec(w_in.shape),
                  _const_spec((3, D)),
                  _const_spec((D, D)),
                  _const_spec((D, D)),
                  _const_spec((D, D)),
                  _const_spec((D, 2 * ROUTE_PAD)),
                  _const_spec((1, ROUTE_PAD))] + cast_specs,
        out_specs=[pl.BlockSpec((tm, D), tok),
                   pl.BlockSpec((tm, D // 2), tok),
                   pl.BlockSpec((EPG, tm), col),
                   pl.BlockSpec((tm, V7X_LANES), tok),
                   pl.BlockSpec((N_EXPERTS, V7X_LANES), lambda i: (0, 0))] + cast_specs,
        out_shape=[jax.ShapeDtypeStruct((n, D), F32),
                   jax.ShapeDtypeStruct((n, D // 2), jnp.int32),
                   jax.ShapeDtypeStruct((EPG, n), jnp.int32),
                   jax.ShapeDtypeStruct((n, V7X_LANES), F32),
                   jax.ShapeDtypeStruct((N_EXPERTS, V7X_LANES), jnp.int32)] + cast_shapes,
        scratch_shapes=[pltpu.VMEM((N_EXPERTS, tm), F32)],
        compiler_params=_params(("arbitrary",)),
        name="mixer",
    )(x, mod, hf, hb, g1, g2, w_in, cw, wco, wro, wo, wr, br, *cast_ws)


def _sc_mesh():
    return plsc.VectorSubcoreMesh(core_axis_name="c", subcore_axis_name="s",
                                  num_cores=V7X_SC_CORES, num_subcores=V7X_SC_SUBCORES)


def _sc_worker_id():
    return lax.axis_index("s") * V7X_SC_CORES + lax.axis_index("c")


def _sc_dispatch(rows, dest, n_slots):
    n, width = rows.shape
    per_w = n // V7X_SC_WORKERS
    n_ch = per_w // SC_WINDOW
    assert n_ch * SC_WINDOW * V7X_SC_WORKERS == n
    idx = dest.reshape(2, V7X_SC_WORKERS, n_ch, SC_WINDOW).transpose(1, 0, 2, 3)

    def body(x_hbm, d_hbm, o_hbm, idx_v, buf, ld_sem, st_sem):
        wid = _sc_worker_id()
        pltpu.sync_copy(d_hbm.at[wid], idx_v)

        def load(j):
            src = x_hbm.at[pl.ds(wid * per_w + j * SC_WINDOW, SC_WINDOW)]
            return pltpu.async_copy(src, buf.at[j % 2], ld_sem.at[j % 2])

        def scatter(j):
            return [pltpu.async_copy(buf.at[j % 2], o_hbm.at[idx_v.at[k, j]], st_sem.at[j % 2])
                    for k in range(2)]

        loads = {0: load(0)}
        stores = {}
        for j in range(n_ch):
            loads[j].wait()
            if j >= 1:
                for cp in stores[j - 1]:
                    cp.wait()
            if j + 1 < n_ch:
                loads[j + 1] = load(j + 1)
            stores[j] = scatter(j)
        for cp in stores[n_ch - 1]:
            cp.wait()

    return pl.kernel(
        body,
        out_type=jax.ShapeDtypeStruct((n_slots, width), jnp.int32),
        mesh=_sc_mesh(),
        scratch_types=[pltpu.VMEM((2, n_ch, SC_WINDOW), jnp.int32),
                       pltpu.VMEM((2, SC_WINDOW, width), jnp.int32),
                       pltpu.SemaphoreType.DMA((2,)),
                       pltpu.SemaphoreType.DMA((2,))],
        name="sc_dispatch",
    )(rows, idx)


def _sc_collect(rows, dest):
    n = dest.shape[1]
    width = rows.shape[1]
    per_w = n // V7X_SC_WORKERS
    n_ch = per_w // SC_WINDOW
    assert n_ch * SC_WINDOW * V7X_SC_WORKERS == n
    idx = dest.reshape(2, V7X_SC_WORKERS, n_ch, SC_WINDOW).transpose(1, 0, 2, 3)
    windows = [(k, j) for k in range(2) for j in range(n_ch)]

    def body(y_hbm, d_hbm, o_hbm, idx_v, buf, ld_sem, st_sem):
        wid = _sc_worker_id()
        pltpu.sync_copy(d_hbm.at[wid], idx_v)

        def gather(c):
            k, j = windows[c]
            return pltpu.async_copy(y_hbm.at[idx_v.at[k, j]], buf.at[c % 2], ld_sem.at[c % 2])

        def store(c):
            k, j = windows[c]
            dst = o_hbm.at[pl.ds(k * n + wid * per_w + j * SC_WINDOW, SC_WINDOW)]
            return pltpu.async_copy(buf.at[c % 2], dst, st_sem.at[c % 2])

        loads = {0: gather(0)}
        stores = {}
        for c in range(len(windows)):
            loads[c].wait()
            if c >= 1:
                stores[c - 1].wait()
            if c + 1 < len(windows):
                loads[c + 1] = gather(c + 1)
            stores[c] = store(c)
        stores[len(windows) - 1].wait()

    return pl.kernel(
        body,
        out_type=jax.ShapeDtypeStruct((2 * n, width), jnp.int32),
        mesh=_sc_mesh(),
        scratch_types=[pltpu.VMEM((2, n_ch, SC_WINDOW), jnp.int32),
                       pltpu.VMEM((2, SC_WINDOW, width), jnp.int32),
                       pltpu.SemaphoreType.DMA((2,)),
                       pltpu.SemaphoreType.DMA((2,))],
        name="sc_collect",
    )(rows, idx)


def _expert_body(tr_ref, re_ref, nr_ref, nu_ref, xs_ref, w1_hbm, w3_hbm, w2_hbm, o_ref,
                 w1s, w3s, w2s, sem):
    i = pl.program_id(0)
    run = tr_ref[i]
    first = jnp.logical_or(i == 0, run != tr_ref[jnp.maximum(i - 1, 0)])
    slot = run % 2

    def weight_copies(r, dst_slot):
        e = re_ref[r]
        return (pltpu.make_async_copy(w1_hbm.at[e], w1s.at[dst_slot], sem.at[0, dst_slot]),
                pltpu.make_async_copy(w3_hbm.at[e], w3s.at[dst_slot], sem.at[1, dst_slot]),
                pltpu.make_async_copy(w2_hbm.at[e], w2s.at[dst_slot], sem.at[2, dst_slot]))

    @pl.when(i == 0)
    def _():
        for cp in weight_copies(0, 0):
            cp.start()

    @pl.when(first)
    def _():
        for cp in weight_copies(run, slot):
            cp.wait()

        @pl.when(run + 1 < nr_ref[0])
        def _():
            for cp in weight_copies(run + 1, 1 - slot):
                cp.start()

    @pl.when(i < nu_ref[0])
    def _():
        lo, hi = _unpack_halves(xs_ref[...])
        lo = lo.astype(BF16)
        hi = hi.astype(BF16)
        half = D // 2
        a = _dot(lo, w1s[slot, 0:half, :]) + _dot(hi, w1s[slot, half:D, :])
        b = _dot(lo, w3s[slot, 0:half, :]) + _dot(hi, w3s[slot, half:D, :])
        z = (a * _sigmoid(a)) * b
        o_ref[...] = _pack_halves(_dot(z.astype(BF16), w2s[slot]))

    @pl.when(i >= nu_ref[0])
    def _():
        o_ref[...] = jnp.zeros_like(o_ref)


def _experts(tile_run, run_e, n_runs, n_used, xs, w1, w3, w2, tme):
    n_slots = xs.shape[0]
    grid_spec = pltpu.PrefetchScalarGridSpec(
        num_scalar_prefetch=4,
        grid=(n_slots // tme,),
        in_specs=[pl.BlockSpec((tme, D // 2), lambda i, *_: (i, 0)),
                  pl.BlockSpec(memory_space=pl.ANY),
                  pl.BlockSpec(memory_space=pl.ANY),
                  pl.BlockSpec(memory_space=pl.ANY)],
        out_specs=pl.BlockSpec((tme, D // 2), lambda i, *_: (i, 0)),
        scratch_shapes=[pltpu.VMEM((2, D, D_EXPERT), BF16),
                        pltpu.VMEM((2, D, D_EXPERT), BF16),
                        pltpu.VMEM((2, D_EXPERT, D), BF16),
                        pltpu.SemaphoreType.DMA((3, 2))],
    )
    return pl.pallas_call(
        _expert_body,
        grid_spec=grid_spec,
        out_shape=jax.ShapeDtypeStruct((n_slots, D // 2), jnp.int32),
        compiler_params=_params(("arbitrary",)),
        name="experts",
    )(tile_run, run_e, n_runs, n_used, xs, w1, w3, w2)


def _final_body(x1_ref, mod_ref, y0_ref, y1_ref, wt_ref, gf_ref, o_ref):
    w0 = wt_ref[:, 0:1]
    w1 = wt_ref[:, 1:2]
    lo0, hi0 = _unpack_halves(y0_ref[...])
    lo1, hi1 = _unpack_halves(y1_ref[...])
    moe = jnp.concatenate([w0 * lo0 + w1 * lo1, w0 * hi0 + w1 * hi1], axis=1)
    x2 = x1_ref[...] + mod_ref[0, 5:6, :] * moe
    ms = jnp.mean(x2 * x2, axis=-1, keepdims=True)
    o_ref[...] = x2 * lax.rsqrt(ms + EPS) * gf_ref[...]


def _final(x1, mod, tiles_per_mod, yg, wts, g_final):
    n = x1.shape[0]
    tm = FIN_TM
    nt = n // tm
    return pl.pallas_call(
        _final_body,
        grid=(nt,),
        in_specs=[pl.BlockSpec((tm, D), lambda i: (i, 0)),
                  pl.BlockSpec((1, 6, D), lambda i: (i // tiles_per_mod, 0, 0)),
                  pl.BlockSpec((tm, D // 2), lambda i: (i, 0)),
                  pl.BlockSpec((tm, D // 2), lambda i: (i + nt, 0)),
                  pl.BlockSpec((tm, V7X_LANES), lambda i: (i, 0)),
                  _const_spec((1, D))],
        out_specs=pl.BlockSpec((tm, D), lambda i: (i, 0)),
        out_shape=jax.ShapeDtypeStruct((n, D), F32),
        compiler_params=_params(("parallel",)),
        name="final",
    )(x1, mod, yg, yg, wts, g_final)


def _slot_plan(route, cnt, n, tme):
    counts = cnt[:, 0]
    padded = ((counts + tme - 1) // tme) * tme
    pend = jnp.cumsum(padded)
    pstart = pend - padded
    onehot = route[0:2, :, None] == jnp.arange(N_EXPERTS, dtype=jnp.int32)[None, None, :]
    dest = jnp.sum(jnp.where(onehot, pstart[None, None, :], 0), axis=-1) + route[2:4]
    n_slots = ((2 * n + N_EXPERTS * (tme - 1)) // tme) * tme
    tile_start = jnp.arange(n_slots // tme, dtype=jnp.int32) * tme
    tile_e = jnp.sum((tile_start[:, None] >= pend[None, :]).astype(jnp.int32), axis=1)
    tile_e = jnp.minimum(tile_e, N_EXPERTS - 1)
    n_used = (pend[-1] // tme).astype(jnp.int32).reshape(1)
    used = counts > 0
    run_of_e = jnp.cumsum(used.astype(jnp.int32)) - 1
    eids = jnp.arange(N_EXPERTS, dtype=jnp.int32)
    run_e = jnp.sum(jnp.where(used[None, :] & (run_of_e[None, :] == eids[:, None]), eids[None, :], 0), axis=1)
    tile_run = jnp.sum(jnp.where(tile_e[:, None] == eids[None, :], run_of_e[None, :], 0), axis=1)
    n_runs = jnp.sum(used.astype(jnp.int32)).reshape(1)
    plan = (tile_run.astype(jnp.int32), run_e.astype(jnp.int32), n_runs, n_used)
    return dest.astype(jnp.int32), plan, n_slots


def _group(x, mod, mod_per_seq, h0, p, row_len, tme, cast_ws):
    n_b, seq_len, _ = x.shape
    n = n_b * seq_len
    xt = x.reshape(n, D)
    mod_seq = mod if mod_per_seq else jnp.broadcast_to(mod, (n_b, 6, D))
    xc = _xr_conv(x, mod_seq, p["g1"], p["w_xr"], p["rnn_conv_w"], p["rnn_conv_b"])
    hf, hb, last = _rglru_scan(xc, h0, p["wg"], p["ba"], p["bx"], p["lam"])
    tiles_per_mod = (seq_len // MIX_TM) if mod_per_seq else (n // MIX_TM)
    outs = _mixer(xt, mod, tiles_per_mod, hf.reshape(n, D), hb.reshape(n, D),
                                     p["g1"], p["g2"], p["w_in"], p["conv_w"], p["wco"], p["wro"],
                                     p["wo"], p["wr"], p["br"], row_len, cast_ws)
    x1, h2, route, wts, cnt = outs[:5]
    casts = outs[5:]
    dest, plan, n_slots = _slot_plan(route, cnt, n, tme)
    xs = _sc_dispatch(h2, dest, n_slots)

    def finish(w1b, w3b, w2b):
        ys = _experts(*plan, xs, w1b, w3b, w2b, tme)
        yg = _sc_collect(ys, dest)
        tiles_per_mod_f = (seq_len // FIN_TM) if mod_per_seq else (n // FIN_TM)
        y = _final(x1, mod, tiles_per_mod_f, yg, wts, p["g_final"])
        return y.reshape(n_b, seq_len, D)

    return finish, last, casts


def kernel(x_prompt, x_sample, state_rnn, c, c_ctx, w_ada, b_ada, g_norm1, g_norm2, w_in, conv_w, w_conv_out, rnn_conv_w, rnn_conv_b, w_gate_a, b_gate_a, w_gate_x, b_gate_x, lam, w_rnn_out, w_o, w_router_group, b_router_group, w_router_expert, b_router_expert, w1, w3, w2, g_final):
    assert w_ada.shape[0] == 1, "single layer"
    n_pb, n_sb = x_prompt.shape[0], x_sample.shape[0]

    cond = jnp.zeros((16, D), F32).at[0].set(c_ctx).at[1:1 + n_sb].set(c)
    mod = _ada(cond, w_ada[0], b_ada[0]).reshape(16, 6, D)

    w_in_b = w_in[0].astype(BF16)
    wr = jnp.zeros((D, ROUTE_PAD), F32)
    wr = wr.at[:, :N_GROUPS].set(w_router_group[0]).at[:, EPG:EPG + N_EXPERTS].set(w_router_expert[0])
    br = jnp.zeros((1, ROUTE_PAD), F32).at[0, N_GROUPS:EPG].set(NEG_BIG)
    br = br.at[0, :N_GROUPS].set(b_router_group[0]).at[0, EPG:EPG + N_EXPERTS].set(b_router_expert[0])
    wr_hi = wr.astype(BF16)
    p = dict(
        g1=g_norm1, g2=g_norm2, w_in=w_in_b, w_xr=w_in_b[:, 3 * D:4 * D],
        conv_w=conv_w[0], rnn_conv_w=rnn_conv_w[0], rnn_conv_b=rnn_conv_b,
        wg=(0.5 * jnp.concatenate([w_gate_a[0], w_gate_x[0]], axis=-1)).astype(BF16),
        ba=b_gate_a[0], bx=b_gate_x[0], lam=lam[0],
        wco=w_conv_out[0].astype(BF16), wro=w_rnn_out[0].astype(BF16), wo=w_o[0].astype(BF16),
        wr=jnp.concatenate([wr_hi, (wr - wr_hi.astype(F32)).astype(BF16)], axis=1), br=br,
        g_final=g_final.reshape(1, D),
    )

    h0_p = jnp.zeros((n_pb // SCAN_B, 2, SCAN_B, D), F32)
    finish_p, last, (w2b,) = _group(x_prompt, mod[0:1], False, h0_p, p, x_prompt.shape[1], 256,
                                    [w2[0]])
    state_new = last.transpose(0, 2, 1, 3).reshape(n_pb, 1, 2, D)

    h0_s = state_rnn[:, 0].reshape(n_sb // SCAN_B, SCAN_B, 2, D).transpose(0, 2, 1, 3)
    finish_s, _, (w1b, w3b) = _group(x_sample, mod[1:1 + n_sb], True, h0_s, p, GRID_W, 512,
                                     [w1[0], w3[0]])
    return (finish_p(w1b, w3b, w2b), finish_s(w1b, w3b, w2b), state_new)
```

```python
import functools

import jax
import jax.numpy as jnp
from jax import lax
from jax.experimental import pallas as pl
from jax.experimental.pallas import tpu as pltpu
from jax.experimental.pallas import tpu_sc as plsc

D = 1024
N_HEADS = 8
HEAD = D // N_HEADS
GRID_W = 64
RG_C = 8.0
N_GROUPS = 4
EPG = 8
N_EXPERTS = N_GROUPS * EPG
D_EXPERT = 512
EPS = 1e-6
F32 = jnp.float32
BF16 = jnp.bfloat16

V7X_LANES = 128
V7X_SUBLANES = 8
V7X_VMEM_LIMIT_BYTES = 56 * 1024 * 1024
V7X_SC_CORES = 2
V7X_SC_SUBCORES = 16
V7X_SC_WORKERS = V7X_SC_CORES * V7X_SC_SUBCORES
SC_WINDOW = 64

XR_T = 256
XR_SUB = 32
XR_LEFT = 2
SCAN_T = 128
SCAN_B = V7X_SUBLANES
SCAN_UNROLL = 8
LOG2_E = 1.4426950408889634
TINY = 1e-30
MIX_TM = 512
FIN_TM = 1024
ROUTE_PAD = 128
NEG_BIG = -1e30


def _sigmoid(x):
    return 0.5 * jnp.tanh(0.5 * x) + 0.5


def _norm_mod(x, g, scale, shift):
    ms = jnp.mean(x * x, axis=-1, keepdims=True)
    return (x * lax.rsqrt(ms + EPS)) * (g * (1.0 + scale)) + shift


def _dot(a, b):
    return jnp.dot(a, b, preferred_element_type=F32)


def _params(sem, vmem=V7X_VMEM_LIMIT_BYTES):
    return pltpu.CompilerParams(dimension_semantics=sem, vmem_limit_bytes=vmem)


def _const_spec(shape):
    zeros = (0,) * len(shape)
    return pl.BlockSpec(shape, lambda *_: zeros, pipeline_mode=pl.Buffered(1))


def _ada_body(c_ref, w_ref, b_ref, o_ref):
    c = c_ref[...]
    s = (c * _sigmoid(c)).astype(BF16)
    o_ref[...] = _dot(s, w_ref[...].astype(BF16)) + b_ref[...]


def _ada(cond, w_ada, b_ada):
    rows = cond.shape[0]
    n_out = w_ada.shape[1]
    return pl.pallas_call(
        _ada_body,
        grid=(n_out // D,),
        in_specs=[pl.BlockSpec((rows, D), lambda i: (0, 0)),
                  pl.BlockSpec((D, D), lambda i: (0, i)),
                  pl.BlockSpec((1, D), lambda i: (0, i))],
        out_specs=pl.BlockSpec((rows, D), lambda i: (0, i)),
        out_shape=jax.ShapeDtypeStruct((rows, n_out), F32),
        compiler_params=_params(("parallel",)),
        name="ada",
    )(cond, w_ada, b_ada.reshape(1, n_out))


def _xr_body(x_ref, xn_ref, mod_ref, g_ref, w_ref, cw_ref, cb_ref, o_ref, xt):
    j = pl.program_id(1)
    n_t = pl.num_programs(1)
    t_len = x_ref.shape[1]
    nb = SCAN_B
    body0 = XR_LEFT * nb

    @pl.when(j == 0)
    def _():
        xt[:, 0:body0, :] = jnp.zeros((N_HEADS, body0, HEAD), F32)

    @pl.when(j > 0)
    def _():
        xt[:, 0:body0, :] = xt[:, t_len * nb:t_len * nb + body0, :]

    g = g_ref[...]
    for b in range(nb):
        h = _norm_mod(x_ref[b], g, mod_ref[b, 1:2, :], mod_ref[b, 0:1, :]).astype(BF16)
        r = _dot(h, w_ref[...])
        for s in range(N_HEADS):
            xt[s, pl.ds(body0 + b, t_len, stride=nb), :] = r[:, s * HEAD:(s + 1) * HEAD]

    hn = _norm_mod(xn_ref[:, 0, :], g, mod_ref[:, 1, :], mod_ref[:, 0, :]).astype(BF16)
    rn = jnp.where(j < n_t - 1, _dot(hn, w_ref[...]), 0.0)
    tail = body0 + t_len * nb
    for s in range(N_HEADS):
        xt[s, tail:tail + nb, :] = rn[:, s * HEAD:(s + 1) * HEAD]

    sub = XR_SUB
    for s in range(N_HEADS):
        sl = slice(s * HEAD, (s + 1) * HEAD)
        for t0 in range(0, t_len, sub):
            y = cb_ref[:, sl]
            for k in range(4):
                r0 = (t0 + k) * nb
                y = y + xt[s, r0:r0 + sub * nb, :] * cw_ref[k:k + 1, sl]
            o_ref[t0:t0 + sub, :, sl] = y.reshape(sub, nb, HEAD)


def _xr_conv(x, mod_seq, g1, w_xr, cw, cb):
    n_b, seq_len, _ = x.shape
    t_len = min(XR_T, seq_len)
    assert n_b % SCAN_B == 0 and seq_len % t_len == 0 and t_len % XR_SUB == 0
    n_t = seq_len // t_len
    nxt = t_len // V7X_SUBLANES
    last_blk = seq_len // V7X_SUBLANES - 1
    return pl.pallas_call(
        _xr_body,
        grid=(n_b // SCAN_B, n_t),
        in_specs=[pl.BlockSpec((SCAN_B, t_len, D), lambda g, j: (g, j, 0)),
                  pl.BlockSpec((SCAN_B, V7X_SUBLANES, D),
                               lambda g, j: (g, jnp.minimum((j + 1) * nxt, last_blk), 0)),
                  pl.BlockSpec((SCAN_B, 6, D), lambda g, j: (g, 0, 0)),
                  _const_spec((1, D)),
                  _const_spec((D, D)),
                  _const_spec((4, D)),
                  _const_spec((1, D))],
        out_specs=pl.BlockSpec((t_len, SCAN_B, D), lambda g, j: (j, g, 0)),
        out_shape=jax.ShapeDtypeStruct((seq_len, n_b, D), F32),
        scratch_shapes=[pltpu.VMEM((N_HEADS, (t_len + XR_LEFT + 1) * SCAN_B, HEAD), F32)],
        compiler_params=_params(("parallel", "arbitrary")),
        name="xr_conv",
    )(x, x, mod_seq, g1, w_xr, cw, cb)


def _scan_body(xf_ref, xb_ref, h0_ref, wg_ref, ba_ref, bx_ref, lam_ref,
               hf_ref, hb_ref, last_ref, a_s, u_s, h_s, hc):
    j = pl.program_id(1)
    n_t = pl.num_programs(1)
    t_len = xf_ref.shape[0]
    rows = SCAN_B * t_len

    @pl.when(j == 0)
    def _():
        for d in range(2):
            for s in range(N_HEADS):
                hc[d, s] = h0_ref[0, d, :, s * HEAD:(s + 1) * HEAD]

    for d, x_ref in ((0, xf_ref), (1, xb_ref)):
        z = -lam_ref[d:d + 1, :]
        sp = jnp.maximum(z, 0.0) + jnp.log(1.0 + jnp.exp(-jnp.abs(z)))
        c2 = (-0.5 * RG_C * LOG2_E) * sp
        for hd in range(N_HEADS):
            sl = slice(hd * HEAD, (hd + 1) * HEAD)
            xh = x_ref[:, :, sl].reshape(rows, HEAD)
            g = _dot(xh.astype(BF16), wg_ref[d, hd])
            t_r = jnp.tanh(g[:, :HEAD] + 0.5 * ba_ref[d:d + 1, sl])
            t_i = jnp.tanh(g[:, HEAD:] + 0.5 * bx_ref[d:d + 1, sl])
            a = jnp.exp2(c2[:, sl] * t_r + c2[:, sl])
            q = 0.25 - 0.25 * (a * a)
            half_mult = q * lax.rsqrt(jnp.maximum(q, TINY))
            u = half_mult * ((t_i + 1.0) * xh)
            a_s[d, hd] = a
            u_s[d, hd] = u

    def step(tb, carry):
        h_f, h_b = carry
        for k in range(SCAN_UNROLL):
            t = tb * SCAN_UNROLL + k
            rf = pl.multiple_of(t * SCAN_B, SCAN_B)
            rb = pl.multiple_of((t_len - 1 - t) * SCAN_B, SCAN_B)
            h_f = a_s[0, :, pl.ds(rf, SCAN_B), :] * h_f + u_s[0, :, pl.ds(rf, SCAN_B), :]
            h_b = a_s[1, :, pl.ds(rb, SCAN_B), :] * h_b + u_s[1, :, pl.ds(rb, SCAN_B), :]
            h_s[0, :, pl.ds(rf, SCAN_B), :] = h_f
            h_s[1, :, pl.ds(rb, SCAN_B), :] = h_b
        return h_f, h_b

    h_f, h_b = lax.fori_loop(0, t_len // SCAN_UNROLL, step, (hc[0], hc[1]))
    hc[0] = h_f
    hc[1] = h_b

    for d, o_ref in ((0, hf_ref), (1, hb_ref)):
        for hd in range(N_HEADS):
            for b in range(SCAN_B):
                o_ref[b, :, hd * HEAD:(hd + 1) * HEAD] = (
                    h_s[d, hd, pl.ds(b, t_len, stride=SCAN_B), :].astype(BF16))

    @pl.when(j == n_t - 1)
    def _():
        for d in range(2):
            for s in range(N_HEADS):
                last_ref[0, d, :, s * HEAD:(s + 1) * HEAD] = hc[d, s]


def _rglru_scan(xc, h0, wg, ba, bx, lam):
    seq_len, n_b, _ = xc.shape
    n_g = n_b // SCAN_B
    n_t = seq_len // SCAN_T
    blk = (SCAN_B, SCAN_T, D)
    blk_in = (SCAN_T, SCAN_B, D)
    return pl.pallas_call(
        _scan_body,
        grid=(n_g, n_t),
        in_specs=[pl.BlockSpec(blk_in, lambda g, j: (j, g, 0)),
                  pl.BlockSpec(blk_in, lambda g, j: (n_t - 1 - j, g, 0)),
                  pl.BlockSpec((1, 2, SCAN_B, D), lambda g, j: (g, 0, 0, 0)),
                  _const_spec((2, N_HEADS, HEAD, 2 * HEAD)),
                  _const_spec((2, D)),
                  _const_spec((2, D)),
                  _const_spec((2, D))],
        out_specs=[pl.BlockSpec(blk, lambda g, j: (g, j, 0)),
                   pl.BlockSpec(blk, lambda g, j: (g, n_t - 1 - j, 0)),
                   pl.BlockSpec((1, 2, SCAN_B, D), lambda g, j: (g, 0, 0, 0))],
        out_shape=[jax.ShapeDtypeStruct((n_b, seq_len, D), BF16),
                   jax.ShapeDtypeStruct((n_b, seq_len, D), BF16),
                   jax.ShapeDtypeStruct((n_g, 2, SCAN_B, D), F32)],
        scratch_shapes=[pltpu.VMEM((2, N_HEADS, SCAN_T * SCAN_B, HEAD), F32),
                        pltpu.VMEM((2, N_HEADS, SCAN_T * SCAN_B, HEAD), F32),
                        pltpu.VMEM((2, N_HEADS, SCAN_T * SCAN_B, HEAD), F32),
                        pltpu.VMEM((2, N_HEADS, SCAN_B, HEAD), F32)],
        compiler_params=_params(("parallel", "arbitrary")),
        name="rglru_scan",
    )(xc, xc, h0, wg, ba, bx, lam)


def _pack_halves(v):
    half = v.shape[1] // 2
    lo = lax.bitcast_convert_type(v[:, :half].astype(BF16).astype(F32), jnp.uint32)
    hi = lax.bitcast_convert_type(v[:, half:].astype(BF16).astype(F32), jnp.uint32)
    return lax.bitcast_convert_type((lo >> 16) | (hi & jnp.uint32(0xFFFF0000)), jnp.int32)


def _unpack_halves(p):
    u = lax.bitcast_convert_type(p, jnp.uint32)
    lo = lax.bitcast_convert_type(u << 16, F32)
    hi = lax.bitcast_convert_type(u & jnp.uint32(0xFFFF0000), F32)
    return lo, hi


def _mix_body(x_ref, mod_ref, hf_ref, hb_ref, g1_ref, g2_ref, win_ref, cw_ref,
              wco_ref, wro_ref, wo_ref, wr_ref, br_ref, *rest, row_len, n_cast):
    cast_in = rest[:n_cast]
    x1_ref, h2_ref, route_ref, wts_ref, cnt_ref = rest[n_cast:n_cast + 5]
    cast_out = rest[n_cast + 5:2 * n_cast + 5]
    seen = rest[2 * n_cast + 5]
    for src, dst in zip(cast_in, cast_out):
        dst[...] = src[...].astype(BF16)
    x = x_ref[...]
    tm = x.shape[0]
    h = _norm_mod(x, g1_ref[...], mod_ref[0, 1:2, :], mod_ref[0, 0:1, :]).astype(BF16)

    def proj(k):
        return _dot(h, win_ref[:, k * D:(k + 1) * D])

    cv = proj(1) * proj(2)
    pos = lax.broadcasted_iota(jnp.int32, (tm, 1), 0) % row_len
    conv = cv * cw_ref[1:2, :]
    conv = conv + jnp.where(pos >= 1, pltpu.roll(cv, 1, 0), 0.0) * cw_ref[0:1, :]
    conv = conv + jnp.where(pos <= row_len - 2, pltpu.roll(cv, tm - 1, 0), 0.0) * cw_ref[2:3, :]
    y_a = _dot((proj(0) * conv).astype(BF16), wco_ref[...])
    merged = _sigmoid(proj(5)) * y_a

    hs = hf_ref[...].astype(F32) + hb_ref[...].astype(F32)
    y_b = _dot((hs * jax.nn.gelu(proj(4))).astype(BF16), wro_ref[...])
    merged = merged + _sigmoid(proj(6)) * y_b

    mix = _dot(merged.astype(BF16), wo_ref[...])
    x1 = x + mod_ref[0, 2:3, :] * mix
    x1_ref[...] = x1
    h2 = _norm_mod(x1, g2_ref[...], mod_ref[0, 4:5, :], mod_ref[0, 3:4, :])
    h2_ref[...] = _pack_halves(h2)

    h2_hi = h2.astype(BF16)
    h2_lo = (h2 - h2_hi.astype(F32)).astype(BF16)
    big = _dot(h2_hi, wr_ref[...])
    logits = (big[:, :ROUTE_PAD] + big[:, ROUTE_PAD:]
              + _dot(h2_lo, wr_ref[:, :ROUTE_PAD]) + br_ref[...])
    lt = logits.T
    row = lax.broadcasted_iota(jnp.int32, (EPG, tm), 0)
    lg = lt[0:EPG]
    mg = jnp.max(lg, axis=0, keepdims=True)
    p_grp = 1.0 / jnp.sum(jnp.exp(lg - mg), axis=0, keepdims=True)
    grp = jnp.min(jnp.where(lg == mg, row, EPG), axis=0, keepdims=True)
    le = lt[EPG * N_GROUPS:EPG * (N_GROUPS + 1)]
    for g in range(N_GROUPS - 2, -1, -1):
        le = jnp.where(grp == g, lt[EPG * (g + 1):EPG * (g + 2)], le)
    me = jnp.max(le, axis=0, keepdims=True)
    ee = jnp.exp(le - me)
    pe = ee / jnp.sum(ee, axis=0, keepdims=True)
    p1 = jnp.max(pe, axis=0, keepdims=True)
    i1 = jnp.min(jnp.where(pe == p1, row, EPG), axis=0, keepdims=True)
    pe2 = jnp.where(row == i1, -1.0, pe)
    p2 = jnp.max(pe2, axis=0, keepdims=True)
    i2 = jnp.min(jnp.where(pe2 == p2, row, EPG), axis=0, keepdims=True)
    den = p1 + p2
    e1 = grp * EPG + i1
    e2 = grp * EPG + i2

    @pl.when(pl.program_id(0) == 0)
    def _():
        seen[...] = jnp.zeros_like(seen)

    erow = lax.broadcasted_iota(jnp.int32, (N_EXPERTS, tm), 0)
    hit1 = erow == e1
    hit2 = erow == e2
    both = jnp.where(jnp.logical_or(hit1, hit2), 1.0, 0.0)
    tri = jnp.where(lax.broadcasted_iota(jnp.int32, (tm, tm), 0)
                    <= lax.broadcasted_iota(jnp.int32, (tm, tm), 1), 1.0, 0.0).astype(BF16)
    before = _dot(both.astype(BF16), tri) - both + seen[...]
    r1 = jnp.sum(jnp.where(hit1, before, 0.0), axis=0, keepdims=True).astype(jnp.int32)
    r2 = jnp.sum(jnp.where(hit2, before, 0.0), axis=0, keepdims=True).astype(jnp.int32)
    total = seen[...] + jnp.sum(both, axis=1, keepdims=True)
    seen[...] = total
    cnt_ref[...] = total[:, :V7X_LANES].astype(jnp.int32)

    route_ref[...] = jnp.where(row == 0, e1, jnp.where(row == 1, e2, jnp.where(
        row == 2, r1, jnp.where(row == 3, r2, 0))))
    w8 = jnp.where(row == 0, p_grp * p1 / den, jnp.where(row == 1, p_grp * p2 / den, 0.0))
    wts_ref[...] = jnp.concatenate([w8, jnp.zeros((V7X_LANES - EPG, tm), F32)], axis=0).T


def _mixer(x, mod, tiles_per_mod, hf, hb, g1, g2, w_in, cw, wco, wro, wo, wr, br, row_len, cast_ws):
    n = x.shape[0]
    tm = MIX_TM
    assert tm % row_len == 0 and n % tm == 0
    steps = n // tm
    assert N_EXPERTS % steps == 0
    epb = N_EXPERTS // steps
    cast_specs = [pl.BlockSpec((epb,) + w.shape[1:], lambda i: (i, 0, 0)) for w in cast_ws]
    cast_shapes = [jax.ShapeDtypeStruct(w.shape, BF16) for w in cast_ws]
    mod_map = lambda i: (i // tiles_per_mod, 0, 0)
    tok = lambda i: (i, 0)
    col = lambda i: (0, i)
    return pl.pallas_call(
        functools.partial(_mix_body, row_len=row_len, n_cast=len(cast_ws)),
        grid=(steps,),
        in_specs=[pl.BlockSpec((tm, D), tok),
                  pl.BlockSpec((1, 6, D), mod_map),
                  pl.BlockSpec((tm, D), tok),
                  pl.BlockSpec((tm, D), tok),
                  _const_spec((1, D)),
                  _const_spec((1, D)),
                  _const_spec(w_in.shape),
                  _const_spec((3, D)),
                  _const_spec((D, D)),
                  _const_spec((D, D)),
                  _const_spec((D, D)),
                  _const_spec((D, 2 * ROUTE_PAD)),
                  _const_spec((1, ROUTE_PAD))] + cast_specs,
        out_specs=[pl.BlockSpec((tm, D), tok),
                   pl.BlockSpec((tm, D // 2), tok),
                   pl.BlockSpec((EPG, tm), col),
                   pl.BlockSpec((tm, V7X_LANES), tok),
                   pl.BlockSpec((N_EXPERTS, V7X_LANES), lambda i: (0, 0))] + cast_specs,
        out_shape=[jax.ShapeDtypeStruct((n, D), F32),
                   jax.ShapeDtypeStruct((n, D // 2), jnp.int32),
                   jax.ShapeDtypeStruct((EPG, n), jnp.int32),
                   jax.ShapeDtypeStruct((n, V7X_LANES), F32),
                   jax.ShapeDtypeStruct((N_EXPERTS, V7X_LANES), jnp.int32)] + cast_shapes,
        scratch_shapes=[pltpu.VMEM((N_EXPERTS, tm), F32)],
        compiler_params=_params(("arbitrary",)),
        name="mixer",
    )(x, mod, hf, hb, g1, g2, w_in, cw, wco, wro, wo, wr, br, *cast_ws)


def _sc_mesh():
    return plsc.VectorSubcoreMesh(core_axis_name="c", subcore_axis_name="s",
                                  num_cores=V7X_SC_CORES, num_subcores=V7X_SC_SUBCORES)


def _sc_worker_id():
    return lax.axis_index("s") * V7X_SC_CORES + lax.axis_index("c")


def _sc_dispatch(rows, dest, n_slots):
    n, width = rows.shape
    per_w = n // V7X_SC_WORKERS
    n_ch = per_w // SC_WINDOW
    assert n_ch * SC_WINDOW * V7X_SC_WORKERS == n
    idx = dest.reshape(2, V7X_SC_WORKERS, n_ch, SC_WINDOW).transpose(1, 0, 2, 3)

    def body(x_hbm, d_hbm, o_hbm, idx_v, buf, ld_sem, st_sem):
        wid = _sc_worker_id()
        pltpu.sync_copy(d_hbm.at[wid], idx_v)

        def load(j):
            src = x_hbm.at[pl.ds(wid * per_w + j * SC_WINDOW, SC_WINDOW)]
            return pltpu.async_copy(src, buf.at[j % 2], ld_sem.at[j % 2])

        def scatter(j):
            return [pltpu.async_copy(buf.at[j % 2], o_hbm.at[idx_v.at[k, j]], st_sem.at[j % 2])
                    for k in range(2)]

        loads = {0: load(0)}
        stores = {}
        for j in range(n_ch):
            loads[j].wait()
            if j >= 1:
                for cp in stores[j - 1]:
                    cp.wait()
            if j + 1 < n_ch:
                loads[j + 1] = load(j + 1)
            stores[j] = scatter(j)
        for cp in stores[n_ch - 1]:
            cp.wait()

    return pl.kernel(
        body,
        out_type=jax.ShapeDtypeStruct((n_slots, width), jnp.int32),
        mesh=_sc_mesh(),
        scratch_types=[pltpu.VMEM((2, n_ch, SC_WINDOW), jnp.int32),
                       pltpu.VMEM((2, SC_WINDOW, width), jnp.int32),
                       pltpu.SemaphoreType.DMA((2,)),
                       pltpu.SemaphoreType.DMA((2,))],
        name="sc_dispatch",
    )(rows, idx)


def _sc_collect(rows, dest):
    n = dest.shape[1]
    width = rows.shape[1]
    per_w = n // V7X_SC_WORKERS
    n_ch = per_w // SC_WINDOW
    assert n_ch * SC_WINDOW * V7X_SC_WORKERS == n
    idx = dest.reshape(2, V7X_SC_WORKERS, n_ch, SC_WINDOW).transpose(1, 0, 2, 3)
    windows = [(k, j) for k in range(2) for j in range(n_ch)]

    def body(y_hbm, d_hbm, o_hbm, idx_v, buf, ld_sem, st_sem):
        wid = _sc_worker_id()
        pltpu.sync_copy(d_hbm.at[wid], idx_v)

        def gather(c):
            k, j = windows[c]
            return pltpu.async_copy(y_hbm.at[idx_v.at[k, j]], buf.at[c % 2], ld_sem.at[c % 2])

        def store(c):
            k, j = windows[c]
            dst = o_hbm.at[pl.ds(k * n + wid * per_w + j * SC_WINDOW, SC_WINDOW)]
            return pltpu.async_copy(buf.at[c % 2], dst, st_sem.at[c % 2])

        loads = {0: gather(0)}
        stores = {}
        for c in range(len(windows)):
            loads[c].wait()
            if c >= 1:
                stores[c - 1].wait()
            if c + 1 < len(windows):
                loads[c + 1] = gather(c + 1)
            stores[c] = store(c)
        stores[len(windows) - 1].wait()

    return pl.kernel(
        body,
        out_type=jax.ShapeDtypeStruct((2 * n, width), jnp.int32),
        mesh=_sc_mesh(),
        scratch_types=[pltpu.VMEM((2, n_ch, SC_WINDOW), jnp.int32),
                       pltpu.VMEM((2, SC_WINDOW, width), jnp.int32),
                       pltpu.SemaphoreType.DMA((2,)),
                       pltpu.SemaphoreType.DMA((2,))],
        name="sc_collect",
    )(rows, idx)


def _expert_body(tr_ref, re_ref, nr_ref, nu_ref, xs_ref, w1_hbm, w3_hbm, w2_hbm, o_ref,
                 w1s, w3s, w2s, sem):
    i = pl.program_id(0)
    run = tr_ref[i]
    first = jnp.logical_or(i == 0, run != tr_ref[jnp.maximum(i - 1, 0)])
    slot = run % 2

    def weight_copies(r, dst_slot):
        e = re_ref[r]
        return (pltpu.make_async_copy(w1_hbm.at[e], w1s.at[dst_slot], sem.at[0, dst_slot]),
                pltpu.make_async_copy(w3_hbm.at[e], w3s.at[dst_slot], sem.at[1, dst_slot]),
                pltpu.make_async_copy(w2_hbm.at[e], w2s.at[dst_slot], sem.at[2, dst_slot]))

    @pl.when(i == 0)
    def _():
        for cp in weight_copies(0, 0):
            cp.start()

    @pl.when(first)
    def _():
        for cp in weight_copies(run, slot):
            cp.wait()

        @pl.when(run + 1 < nr_ref[0])
        def _():
            for cp in weight_copies(run + 1, 1 - slot):
                cp.start()

    @pl.when(i < nu_ref[0])
    def _():
        lo, hi = _unpack_halves(xs_ref[...])
        lo = lo.astype(BF16)
        hi = hi.astype(BF16)
        half = D // 2
        a = _dot(lo, w1s[slot, 0:half, :]) + _dot(hi, w1s[slot, half:D, :])
        b = _dot(lo, w3s[slot, 0:half, :]) + _dot(hi, w3s[slot, half:D, :])
        z = (a * _sigmoid(a)) * b
        o_ref[...] = _pack_halves(_dot(z.astype(BF16), w2s[slot]))

    @pl.when(i >= nu_ref[0])
    def _():
        o_ref[...] = jnp.zeros_like(o_ref)


def _experts(tile_run, run_e, n_runs, n_used, xs, w1, w3, w2, tme):
    n_slots = xs.shape[0]
    grid_spec = pltpu.PrefetchScalarGridSpec(
        num_scalar_prefetch=4,
        grid=(n_slots // tme,),
        in_specs=[pl.BlockSpec((tme, D // 2), lambda i, *_: (i, 0)),
                  pl.BlockSpec(memory_space=pl.ANY),
                  pl.BlockSpec(memory_space=pl.ANY),
                  pl.BlockSpec(memory_space=pl.ANY)],
        out_specs=pl.BlockSpec((tme, D // 2), lambda i, *_: (i, 0)),
        scratch_shapes=[pltpu.VMEM((2, D, D_EXPERT), BF16),
                        pltpu.VMEM((2, D, D_EXPERT), BF16),
                        pltpu.VMEM((2, D_EXPERT, D), BF16),
                        pltpu.SemaphoreType.DMA((3, 2))],
    )
    return pl.pallas_call(
        _expert_body,
        grid_spec=grid_spec,
        out_shape=jax.ShapeDtypeStruct((n_slots, D // 2), jnp.int32),
        compiler_params=_params(("arbitrary",)),
        name="experts",
    )(tile_run, run_e, n_runs, n_used, xs, w1, w3, w2)


def _final_body(x1_ref, mod_ref, y0_ref, y1_ref, wt_ref, gf_ref, o_ref):
    w0 = wt_ref[:, 0:1]
    w1 = wt_ref[:, 1:2]
    lo0, hi0 = _unpack_halves(y0_ref[...])
    lo1, hi1 = _unpack_halves(y1_ref[...])
    moe = jnp.concatenate([w0 * lo0 + w1 * lo1, w0 * hi0 + w1 * hi1], axis=1)
    x2 = x1_ref[...] + mod_ref[0, 5:6, :] * moe
    ms = jnp.mean(x2 * x2, axis=-1, keepdims=True)
    o_ref[...] = x2 * lax.rsqrt(ms + EPS) * gf_ref[...]


def _final(x1, mod, tiles_per_mod, yg, wts, g_final):
    n = x1.shape[0]
    tm = FIN_TM
    nt = n // tm
    return pl.pallas_call(
        _final_body,
        grid=(nt,),
        in_specs=[pl.BlockSpec((tm, D), lambda i: (i, 0)),
                  pl.BlockSpec((1, 6, D), lambda i: (i // tiles_per_mod, 0, 0)),
                  pl.BlockSpec((tm, D // 2), lambda i: (i, 0)),
                  pl.BlockSpec((tm, D // 2), lambda i: (i + nt, 0)),
                  pl.BlockSpec((tm, V7X_LANES), lambda i: (i, 0)),
                  _const_spec((1, D))],
        out_specs=pl.BlockSpec((tm, D), lambda i: (i, 0)),
        out_shape=jax.ShapeDtypeStruct((n, D), F32),
        compiler_params=_params(("parallel",)),
        name="final",
    )(x1, mod, yg, yg, wts, g_final)


def _slot_plan(route, cnt, n, tme):
    counts = cnt[:, 0]
    padded = ((counts + tme - 1) // tme) * tme
    pend = jnp.cumsum(padded)
    pstart = pend - padded
    onehot = route[0:2, :, None] == jnp.arange(N_EXPERTS, dtype=jnp.int32)[None, None, :]
    dest = jnp.sum(jnp.where(onehot, pstart[None, None, :], 0), axis=-1) + route[2:4]
    n_slots = ((2 * n + N_EXPERTS * (tme - 1)) // tme) * tme
    tile_start = jnp.arange(n_slots // tme, dtype=jnp.int32) * tme
    tile_e = jnp.sum((tile_start[:, None] >= pend[None, :]).astype(jnp.int32), axis=1)
    tile_e = jnp.minimum(tile_e, N_EXPERTS - 1)
    n_used = (pend[-1] // tme).astype(jnp.int32).reshape(1)
    used = counts > 0
    run_of_e = jnp.cumsum(used.astype(jnp.int32)) - 1
    eids = jnp.arange(N_EXPERTS, dtype=jnp.int32)
    run_e = jnp.sum(jnp.where(used[None, :] & (run_of_e[None, :] == eids[:, None]), eids[None, :], 0), axis=1)
    tile_run = jnp.sum(jnp.where(tile_e[:, None] == eids[None, :], run_of_e[None, :], 0), axis=1)
    n_runs = jnp.sum(used.astype(jnp.int32)).reshape(1)
    plan = (tile_run.astype(jnp.int32), run_e.astype(jnp.int32), n_runs, n_used)
    return dest.astype(jnp.int32), plan, n_slots


def _group(x, mod, mod_per_seq, h0, p, row_len, tme, cast_ws):
    n_b, seq_len, _ = x.shape
    n = n_b * seq_len
    xt = x.reshape(n, D)
    mod_seq = mod if mod_per_seq else jnp.broadcast_to(mod, (n_b, 6, D))
    xc = _xr_conv(x, mod_seq, p["g1"], p["w_xr"], p["rnn_conv_w"], p["rnn_conv_b"])
    hf, hb, last = _rglru_scan(xc, h0, p["wg"], p["ba"], p["bx"], p["lam"])
    tiles_per_mod = (seq_len // MIX_TM) if mod_per_seq else (n // MIX_TM)
    outs = _mixer(xt, mod, tiles_per_mod, hf.reshape(n, D), hb.reshape(n, D),
                                     p["g1"], p["g2"], p["w_in"], p["conv_w"], p["wco"], p["wro"],
                                     p["wo"], p["wr"], p["br"], row_len, cast_ws)
    x1, h2, route, wts, cnt = outs[:5]
    casts = outs[5:]
    dest, plan, n_slots = _slot_plan(route, cnt, n, tme)
    xs = _sc_dispatch(h2, dest, n_slots)

    def finish(w1b, w3b, w2b):
        ys = _experts(*plan, xs, w1b, w3b, w2b, tme)
        yg = _sc_collect(ys, dest)
        tiles_per_mod_f = (seq_len // FIN_TM) if mod_per_seq else (n // FIN_TM)
        y = _final(x1, mod, tiles_per_mod_f, yg, wts, p["g_final"])
        return y.reshape(n_b, seq_len, D)

    return finish, last, casts


def kernel(x_prompt, x_sample, state_rnn, c, c_ctx, w_ada, b_ada, g_norm1, g_norm2, w_in, conv_w, w_conv_out, rnn_conv_w, rnn_conv_b, w_gate_a, b_gate_a, w_gate_x, b_gate_x, lam, w_rnn_out, w_o, w_router_group, b_router_group, w_router_expert, b_router_expert, w1, w3, w2, g_final):
    assert w_ada.shape[0] == 1, "single layer"
    n_pb, n_sb = x_prompt.shape[0], x_sample.shape[0]

    cond = jnp.zeros((16, D), F32).at[0].set(c_ctx).at[1:1 + n_sb].set(c)
    mod = _ada(cond, w_ada[0], b_ada[0]).reshape(16, 6, D)

    w_in_b = w_in[0].astype(BF16)
    wr = jnp.zeros((D, ROUTE_PAD), F32)
    wr = wr.at[:, :N_GROUPS].set(w_router_group[0]).at[:, EPG:EPG + N_EXPERTS].set(w_router_expert[0])
    br = jnp.zeros((1, ROUTE_PAD), F32).at[0, N_GROUPS:EPG].set(NEG_BIG)
    br = br.at[0, :N_GROUPS].set(b_router_group[0]).at[0, EPG:EPG + N_EXPERTS].set(b_router_expert[0])
    wr_hi = wr.astype(BF16)
    p = dict(
        g1=g_norm1, g2=g_norm2, w_in=w_in_b, w_xr=w_in_b[:, 3 * D:4 * D],
        conv_w=conv_w[0], rnn_conv_w=rnn_conv_w[0], rnn_conv_b=rnn_conv_b,
        wg=(0.5 * jnp.concatenate([w_gate_a[0], w_gate_x[0]], axis=-1)).astype(BF16),
        ba=b_gate_a[0], bx=b_gate_x[0], lam=lam[0],
        wco=w_conv_out[0].astype(BF16), wro=w_rnn_out[0].astype(BF16), wo=w_o[0].astype(BF16),
        wr=jnp.concatenate([wr_hi, (wr - wr_hi.astype(F32)).astype(BF16)], axis=1), br=br,
        g_final=g_final.reshape(1, D),
    )

    h0_p = jnp.zeros((n_pb // SCAN_B, 2, SCAN_B, D), F32)
    finish_p, last, (w2b,) = _group(x_prompt, mod[0:1], False, h0_p, p, x_prompt.shape[1], 512,
                                    [w2[0]])
    state_new = last.transpose(0, 2, 1, 3).reshape(n_pb, 1, 2, D)

    h0_s = state_rnn[:, 0].reshape(n_sb // SCAN_B, SCAN_B, 2, D).transpose(0, 2, 1, 3)
    finish_s, _, (w1b, w3b) = _group(x_sample, mod[1:1 + n_sb], True, h0_s, p, GRID_W, 512,
                                     [w1[0], w3[0]])
    return (finish_p(w1b, w3b, w2b), finish_s(w1b, w3b, w2b), state_new)
```

```python
import functools

import jax
import jax.numpy as jnp
from jax import lax
from jax.experimental import pallas as pl
from jax.experimental.pallas import tpu as pltpu
from jax.experimental.pallas import tpu_sc as plsc

D = 1024
N_HEADS = 8
HEAD = D // N_HEADS
GRID_W = 64
RG_C = 8.0
N_GROUPS = 4
EPG = 8
N_EXPERTS = N_GROUPS * EPG
D_EXPERT = 512
EPS = 1e-6
F32 = jnp.float32
BF16 = jnp.bfloat16

V7X_LANES = 128
V7X_SUBLANES = 8
V7X_VMEM_LIMIT_BYTES = 56 * 1024 * 1024
V7X_SC_CORES = 2
V7X_SC_SUBCORES = 16
V7X_SC_WORKERS = V7X_SC_CORES * V7X_SC_SUBCORES
SC_WINDOW = 64

XR_T = 256
XR_SUB = 32
XR_LEFT = 2
SCAN_T = 128
SCAN_B = V7X_SUBLANES
SCAN_UNROLL = 8
LOG2_E = 1.4426950408889634
TINY = 1e-30
MIX_TM = 512
FIN_TM = 1024
EXPERT_ROW_BUFS = 3
ROUTE_PAD = 128
NEG_BIG = -1e30


def _sigmoid(x):
    return 0.5 * jnp.tanh(0.5 * x) + 0.5


def _norm_mod(x, g, scale, shift):
    ms = jnp.mean(x * x, axis=-1, keepdims=True)
    return (x * lax.rsqrt(ms + EPS)) * (g * (1.0 + scale)) + shift


def _dot(a, b):
    return jnp.dot(a, b, preferred_element_type=F32)


def _params(sem, vmem=V7X_VMEM_LIMIT_BYTES):
    return pltpu.CompilerParams(dimension_semantics=sem, vmem_limit_bytes=vmem)


def _const_spec(shape):
    zeros = (0,) * len(shape)
    return pl.BlockSpec(shape, lambda *_: zeros, pipeline_mode=pl.Buffered(1))


def _ada_body(c_ref, w_ref, b_ref, o_ref):
    c = c_ref[...]
    s = (c * _sigmoid(c)).astype(BF16)
    o_ref[...] = _dot(s, w_ref[...].astype(BF16)) + b_ref[...]


def _ada(cond, w_ada, b_ada):
    rows = cond.shape[0]
    n_out = w_ada.shape[1]
    return pl.pallas_call(
        _ada_body,
        grid=(n_out // D,),
        in_specs=[pl.BlockSpec((rows, D), lambda i: (0, 0)),
                  pl.BlockSpec((D, D), lambda i: (0, i)),
                  pl.BlockSpec((1, D), lambda i: (0, i))],
        out_specs=pl.BlockSpec((rows, D), lambda i: (0, i)),
        out_shape=jax.ShapeDtypeStruct((rows, n_out), F32),
        compiler_params=_params(("parallel",)),
        name="ada",
    )(cond, w_ada, b_ada.reshape(1, n_out))


def _xr_body(x_ref, xn_ref, mod_ref, g_ref, w_ref, cw_ref, cb_ref, o_ref, xt):
    j = pl.program_id(1)
    n_t = pl.num_programs(1)
    t_len = x_ref.shape[1]
    nb = SCAN_B
    body0 = XR_LEFT * nb

    @pl.when(j == 0)
    def _():
        xt[:, 0:body0, :] = jnp.zeros((N_HEADS, body0, HEAD), F32)

    @pl.when(j > 0)
    def _():
        xt[:, 0:body0, :] = xt[:, t_len * nb:t_len * nb + body0, :]

    g = g_ref[...]
    for b in range(nb):
        h = _norm_mod(x_ref[b], g, mod_ref[b, 1:2, :], mod_ref[b, 0:1, :]).astype(BF16)
        r = _dot(h, w_ref[...])
        for s in range(N_HEADS):
            xt[s, pl.ds(body0 + b, t_len, stride=nb), :] = r[:, s * HEAD:(s + 1) * HEAD]

    hn = _norm_mod(xn_ref[:, 0, :], g, mod_ref[:, 1, :], mod_ref[:, 0, :]).astype(BF16)
    rn = jnp.where(j < n_t - 1, _dot(hn, w_ref[...]), 0.0)
    tail = body0 + t_len * nb
    for s in range(N_HEADS):
        xt[s, tail:tail + nb, :] = rn[:, s * HEAD:(s + 1) * HEAD]

    sub = XR_SUB
    for s in range(N_HEADS):
        sl = slice(s * HEAD, (s + 1) * HEAD)
        for t0 in range(0, t_len, sub):
            y = cb_ref[:, sl]
            for k in range(4):
                r0 = (t0 + k) * nb
                y = y + xt[s, r0:r0 + sub * nb, :] * cw_ref[k:k + 1, sl]
            o_ref[t0:t0 + sub, :, sl] = y.reshape(sub, nb, HEAD)


def _xr_conv(x, mod_seq, g1, w_xr, cw, cb):
    n_b, seq_len, _ = x.shape
    t_len = min(XR_T, seq_len)
    assert n_b % SCAN_B == 0 and seq_len % t_len == 0 and t_len % XR_SUB == 0
    n_t = seq_len // t_len
    nxt = t_len // V7X_SUBLANES
    last_blk = seq_len // V7X_SUBLANES - 1
    return pl.pallas_call(
        _xr_body,
        grid=(n_b // SCAN_B, n_t),
        in_specs=[pl.BlockSpec((SCAN_B, t_len, D), lambda g, j: (g, j, 0)),
                  pl.BlockSpec((SCAN_B, V7X_SUBLANES, D),
                               lambda g, j: (g, jnp.minimum((j + 1) * nxt, last_blk), 0)),
                  pl.BlockSpec((SCAN_B, 6, D), lambda g, j: (g, 0, 0)),
                  _const_spec((1, D)),
                  _const_spec((D, D)),
                  _const_spec((4, D)),
                  _const_spec((1, D))],
        out_specs=pl.BlockSpec((t_len, SCAN_B, D), lambda g, j: (j, g, 0)),
        out_shape=jax.ShapeDtypeStruct((seq_len, n_b, D), F32),
        scratch_shapes=[pltpu.VMEM((N_HEADS, (t_len + XR_LEFT + 1) * SCAN_B, HEAD), F32)],
        compiler_params=_params(("parallel", "arbitrary")),
        name="xr_conv",
    )(x, x, mod_seq, g1, w_xr, cw, cb)


def _scan_body(xf_ref, xb_ref, h0_ref, wg_ref, ba_ref, bx_ref, lam_ref,
               hf_ref, hb_ref, last_ref, a_s, u_s, h_s, hc):
    j = pl.program_id(1)
    n_t = pl.num_programs(1)
    t_len = xf_ref.shape[0]
    rows = SCAN_B * t_len

    @pl.when(j == 0)
    def _():
        for d in range(2):
            for s in range(N_HEADS):
                hc[d, s] = h0_ref[0, d, :, s * HEAD:(s + 1) * HEAD]

    for d, x_ref in ((0, xf_ref), (1, xb_ref)):
        z = -lam_ref[d:d + 1, :]
        sp = jnp.maximum(z, 0.0) + jnp.log(1.0 + jnp.exp(-jnp.abs(z)))
        c2 = (-0.5 * RG_C * LOG2_E) * sp
        for hd in range(N_HEADS):
            sl = slice(hd * HEAD, (hd + 1) * HEAD)
            xh = x_ref[:, :, sl].reshape(rows, HEAD)
            g = _dot(xh.astype(BF16), wg_ref[d, hd])
            t_r = jnp.tanh(g[:, :HEAD] + 0.5 * ba_ref[d:d + 1, sl])
            t_i = jnp.tanh(g[:, HEAD:] + 0.5 * bx_ref[d:d + 1, sl])
            a = jnp.exp2(c2[:, sl] * t_r + c2[:, sl])
            q = 0.25 - 0.25 * (a * a)
            half_mult = q * lax.rsqrt(jnp.maximum(q, TINY))
            u = half_mult * ((t_i + 1.0) * xh)
            a_s[d, hd] = a
            u_s[d, hd] = u

    def step(tb, carry):
        h_f, h_b = carry
        for k in range(SCAN_UNROLL):
            t = tb * SCAN_UNROLL + k
            rf = pl.multiple_of(t * SCAN_B, SCAN_B)
            rb = pl.multiple_of((t_len - 1 - t) * SCAN_B, SCAN_B)
            h_f = a_s[0, :, pl.ds(rf, SCAN_B), :] * h_f + u_s[0, :, pl.ds(rf, SCAN_B), :]
            h_b = a_s[1, :, pl.ds(rb, SCAN_B), :] * h_b + u_s[1, :, pl.ds(rb, SCAN_B), :]
            h_s[0, :, pl.ds(rf, SCAN_B), :] = h_f
            h_s[1, :, pl.ds(rb, SCAN_B), :] = h_b
        return h_f, h_b

    h_f, h_b = lax.fori_loop(0, t_len // SCAN_UNROLL, step, (hc[0], hc[1]))
    hc[0] = h_f
    hc[1] = h_b

    for d, o_ref in ((0, hf_ref), (1, hb_ref)):
        for hd in range(N_HEADS):
            for b in range(SCAN_B):
                o_ref[b, :, hd * HEAD:(hd + 1) * HEAD] = (
                    h_s[d, hd, pl.ds(b, t_len, stride=SCAN_B), :].astype(BF16))

    @pl.when(j == n_t - 1)
    def _():
        for d in range(2):
            for s in range(N_HEADS):
                last_ref[0, d, :, s * HEAD:(s + 1) * HEAD] = hc[d, s]


def _rglru_scan(xc, h0, wg, ba, bx, lam):
    seq_len, n_b, _ = xc.shape
    n_g = n_b // SCAN_B
    n_t = seq_len // SCAN_T
    blk = (SCAN_B, SCAN_T, D)
    blk_in = (SCAN_T, SCAN_B, D)
    return pl.pallas_call(
        _scan_body,
        grid=(n_g, n_t),
        in_specs=[pl.BlockSpec(blk_in, lambda g, j: (j, g, 0)),
                  pl.BlockSpec(blk_in, lambda g, j: (n_t - 1 - j, g, 0)),
                  pl.BlockSpec((1, 2, SCAN_B, D), lambda g, j: (g, 0, 0, 0)),
                  _const_spec((2, N_HEADS, HEAD, 2 * HEAD)),
                  _const_spec((2, D)),
                  _const_spec((2, D)),
                  _const_spec((2, D))],
        out_specs=[pl.BlockSpec(blk, lambda g, j: (g, j, 0)),
                   pl.BlockSpec(blk, lambda g, j: (g, n_t - 1 - j, 0)),
                   pl.BlockSpec((1, 2, SCAN_B, D), lambda g, j: (g, 0, 0, 0))],
        out_shape=[jax.ShapeDtypeStruct((n_b, seq_len, D), BF16),
                   jax.ShapeDtypeStruct((n_b, seq_len, D), BF16),
                   jax.ShapeDtypeStruct((n_g, 2, SCAN_B, D), F32)],
        scratch_shapes=[pltpu.VMEM((2, N_HEADS, SCAN_T * SCAN_B, HEAD), F32),
                        pltpu.VMEM((2, N_HEADS, SCAN_T * SCAN_B, HEAD), F32),
                        pltpu.VMEM((2, N_HEADS, SCAN_T * SCAN_B, HEAD), F32),
                        pltpu.VMEM((2, N_HEADS, SCAN_B, HEAD), F32)],
        compiler_params=_params(("parallel", "arbitrary")),
        name="rglru_scan",
    )(xc, xc, h0, wg, ba, bx, lam)


def _pack_halves(v):
    half = v.shape[1] // 2
    lo = lax.bitcast_convert_type(v[:, :half].astype(BF16).astype(F32), jnp.uint32)
    hi = lax.bitcast_convert_type(v[:, half:].astype(BF16).astype(F32), jnp.uint32)
    return lax.bitcast_convert_type((lo >> 16) | (hi & jnp.uint32(0xFFFF0000)), jnp.int32)


def _unpack_halves(p):
    u = lax.bitcast_convert_type(p, jnp.uint32)
    lo = lax.bitcast_convert_type(u << 16, F32)
    hi = lax.bitcast_convert_type(u & jnp.uint32(0xFFFF0000), F32)
    return lo, hi


def _mix_body(x_ref, mod_ref, hf_ref, hb_ref, g1_ref, g2_ref, win_ref, cw_ref,
              wco_ref, wro_ref, wo_ref, wr_ref, br_ref, *rest, row_len, n_cast):
    cast_in = rest[:n_cast]
    x1_ref, h2_ref, route_ref, wts_ref, cnt_ref = rest[n_cast:n_cast + 5]
    cast_out = rest[n_cast + 5:2 * n_cast + 5]
    seen = rest[2 * n_cast + 5]
    x = x_ref[...]
    tm = x.shape[0]
    h = _norm_mod(x, g1_ref[...], mod_ref[0, 1:2, :], mod_ref[0, 0:1, :]).astype(BF16)

    def proj(k):
        return _dot(h, win_ref[:, k * D:(k + 1) * D])

    cv = proj(1) * proj(2)
    pos = lax.broadcasted_iota(jnp.int32, (tm, 1), 0) % row_len
    conv = cv * cw_ref[1:2, :]
    conv = conv + jnp.where(pos >= 1, pltpu.roll(cv, 1, 0), 0.0) * cw_ref[0:1, :]
    conv = conv + jnp.where(pos <= row_len - 2, pltpu.roll(cv, tm - 1, 0), 0.0) * cw_ref[2:3, :]
    y_a = _dot((proj(0) * conv).astype(BF16), wco_ref[...])
    merged = _sigmoid(proj(5)) * y_a

    hs = hf_ref[...].astype(F32) + hb_ref[...].astype(F32)
    y_b = _dot((hs * jax.nn.gelu(proj(4))).astype(BF16), wro_ref[...])
    merged = merged + _sigmoid(proj(6)) * y_b

    mix = _dot(merged.astype(BF16), wo_ref[...])
    x1 = x + mod_ref[0, 2:3, :] * mix
    x1_ref[...] = x1
    for src, dst in zip(cast_in, cast_out):
        dst[...] = src[...].astype(BF16)
    h2 = _norm_mod(x1, g2_ref[...], mod_ref[0, 4:5, :], mod_ref[0, 3:4, :])
    h2_ref[...] = _pack_halves(h2)

    h2_hi = h2.astype(BF16)
    h2_lo = (h2 - h2_hi.astype(F32)).astype(BF16)
    big = _dot(h2_hi, wr_ref[...])
    logits = (big[:, :ROUTE_PAD] + big[:, ROUTE_PAD:]
              + _dot(h2_lo, wr_ref[:, :ROUTE_PAD]) + br_ref[...])
    lt = logits.T
    row = lax.broadcasted_iota(jnp.int32, (EPG, tm), 0)
    lg = lt[0:EPG]
    mg = jnp.max(lg, axis=0, keepdims=True)
    p_grp = 1.0 / jnp.sum(jnp.exp(lg - mg), axis=0, keepdims=True)
    grp = jnp.min(jnp.where(lg == mg, row, EPG), axis=0, keepdims=True)
    le = lt[EPG * N_GROUPS:EPG * (N_GROUPS + 1)]
    for g in range(N_GROUPS - 2, -1, -1):
        le = jnp.where(grp == g, lt[EPG * (g + 1):EPG * (g + 2)], le)
    me = jnp.max(le, axis=0, keepdims=True)
    ee = jnp.exp(le - me)
    pe = ee / jnp.sum(ee, axis=0, keepdims=True)
    p1 = jnp.max(pe, axis=0, keepdims=True)
    i1 = jnp.min(jnp.where(pe == p1, row, EPG), axis=0, keepdims=True)
    pe2 = jnp.where(row == i1, -1.0, pe)
    p2 = jnp.max(pe2, axis=0, keepdims=True)
    i2 = jnp.min(jnp.where(pe2 == p2, row, EPG), axis=0, keepdims=True)
    den = p1 + p2
    e1 = grp * EPG + i1
    e2 = grp * EPG + i2

    @pl.when(pl.program_id(0) == 0)
    def _():
        seen[...] = jnp.zeros_like(seen)

    erow = lax.broadcasted_iota(jnp.int32, (N_EXPERTS, tm), 0)
    hit1 = erow == e1
    hit2 = erow == e2
    both = jnp.where(jnp.logical_or(hit1, hit2), 1.0, 0.0)
    tri = jnp.where(lax.broadcasted_iota(jnp.int32, (tm, tm), 0)
                    <= lax.broadcasted_iota(jnp.int32, (tm, tm), 1), 1.0, 0.0).astype(BF16)
    before = _dot(both.astype(BF16), tri) - both + seen[...]
    r1 = jnp.sum(jnp.where(hit1, before, 0.0), axis=0, keepdims=True).astype(jnp.int32)
    r2 = jnp.sum(jnp.where(hit2, before, 0.0), axis=0, keepdims=True).astype(jnp.int32)
    total = seen[...] + jnp.sum(both, axis=1, keepdims=True)
    seen[...] = total
    cnt_ref[...] = total[:, :V7X_LANES].astype(jnp.int32)

    route_ref[...] = jnp.where(row == 0, e1, jnp.where(row == 1, e2, jnp.where(
        row == 2, r1, jnp.where(row == 3, r2, 0))))
    w8 = jnp.where(row == 0, p_grp * p1 / den, jnp.where(row == 1, p_grp * p2 / den, 0.0))
    wts_ref[...] = jnp.concatenate([w8, jnp.zeros((V7X_LANES - EPG, tm), F32)], axis=0).T


def _mixer(x, mod, tiles_per_mod, hf, hb, g1, g2, w_in, cw, wco, wro, wo, wr, br, row_len, cast_ws):
    n = x.shape[0]
    tm = MIX_TM
    assert tm % row_len == 0 and n % tm == 0
    steps = n // tm
    assert N_EXPERTS % steps == 0
    epb = N_EXPERTS // steps
    cast_specs = [pl.BlockSpec((epb,) + w.shape[1:], lambda i: (i, 0, 0)) for w in cast_ws]
    cast_shapes = [jax.ShapeDtypeStruct(w.shape, BF16) for w in cast_ws]
    mod_map = lambda i: (i // tiles_per_mod, 0, 0)
    tok = lambda i: (i, 0)
    col = lambda i: (0, i)
    return pl.pallas_call(
        functools.partial(_mix_body, row_len=row_len, n_cast=len(cast_ws)),
        grid=(steps,),
        in_specs=[pl.BlockSpec((tm, D), tok),
                  pl.BlockSpec((1, 6, D), mod_map),
                  pl.BlockSpec((tm, D), tok),
                  pl.BlockSpec((tm, D), tok),
                  _const_spec((1, D)),
                  _const_spec((1, D)),
                  _const_spec(w_in.shape),
                  _const_spec((3, D)),
                  _const_spec((D, D)),
                  _const_spec((D, D)),
                  _const_spec((D, D)),
                  _const_spec((D, 2 * ROUTE_PAD)),
                  _const_spec((1, ROUTE_PAD))] + cast_specs,
        out_specs=[pl.BlockSpec((tm, D), tok),
                   pl.BlockSpec((tm, D // 2), tok),
                   pl.BlockSpec((EPG, tm), col),
                   pl.BlockSpec((tm, V7X_LANES), tok),
                   pl.BlockSpec((N_EXPERTS, V7X_LANES), lambda i: (0, 0))] + cast_specs,
        out_shape=[jax.ShapeDtypeStruct((n, D), F32),
                   jax.ShapeDtypeStruct((n, D // 2), jnp.int32),
                   jax.ShapeDtypeStruct((EPG, n), jnp.int32),
                   jax.ShapeDtypeStruct((n, V7X_LANES), F32),
                   jax.ShapeDtypeStruct((N_EXPERTS, V7X_LANES), jnp.int32)] + cast_shapes,
        scratch_shapes=[pltpu.VMEM((N_EXPERTS, tm), F32)],
        compiler_params=_params(("arbitrary",)),
        name="mixer",
    )(x, mod, hf, hb, g1, g2, w_in, cw, wco, wro, wo, wr, br, *cast_ws)


def _sc_mesh():
    return plsc.VectorSubcoreMesh(core_axis_name="c", subcore_axis_name="s",
                                  num_cores=V7X_SC_CORES, num_subcores=V7X_SC_SUBCORES)


def _sc_worker_id():
    return lax.axis_index("s") * V7X_SC_CORES + lax.axis_index("c")


def _sc_dispatch(rows, dest, n_slots):
    n, width = rows.shape
    per_w = n // V7X_SC_WORKERS
    n_ch = per_w // SC_WINDOW
    assert n_ch * SC_WINDOW * V7X_SC_WORKERS == n
    idx = dest.reshape(2, V7X_SC_WORKERS, n_ch, SC_WINDOW).transpose(1, 0, 2, 3)

    def body(x_hbm, d_hbm, o_hbm, idx_v, buf, ld_sem, st_sem):
        wid = _sc_worker_id()
        pltpu.sync_copy(d_hbm.at[wid], idx_v)

        def load(j):
            src = x_hbm.at[pl.ds(wid * per_w + j * SC_WINDOW, SC_WINDOW)]
            return pltpu.async_copy(src, buf.at[j % 2], ld_sem.at[j % 2])

        def scatter(j):
            return [pltpu.async_copy(buf.at[j % 2], o_hbm.at[idx_v.at[k, j]], st_sem.at[j % 2])
                    for k in range(2)]

        loads = {0: load(0)}
        stores = {}
        for j in range(n_ch):
            loads[j].wait()
            if j >= 1:
                for cp in stores[j - 1]:
                    cp.wait()
            if j + 1 < n_ch:
                loads[j + 1] = load(j + 1)
            stores[j] = scatter(j)
        for cp in stores[n_ch - 1]:
            cp.wait()

    return pl.kernel(
        body,
        out_type=jax.ShapeDtypeStruct((n_slots, width), jnp.int32),
        mesh=_sc_mesh(),
        scratch_types=[pltpu.VMEM((2, n_ch, SC_WINDOW), jnp.int32),
                       pltpu.VMEM((2, SC_WINDOW, width), jnp.int32),
                       pltpu.SemaphoreType.DMA((2,)),
                       pltpu.SemaphoreType.DMA((2,))],
        name="sc_dispatch",
    )(rows, idx)


def _sc_collect(rows, dest):
    n = dest.shape[1]
    width = rows.shape[1]
    per_w = n // V7X_SC_WORKERS
    n_ch = per_w // SC_WINDOW
    assert n_ch * SC_WINDOW * V7X_SC_WORKERS == n
    idx = dest.reshape(2, V7X_SC_WORKERS, n_ch, SC_WINDOW).transpose(1, 0, 2, 3)
    windows = [(k, j) for k in range(2) for j in range(n_ch)]

    def body(y_hbm, d_hbm, o_hbm, idx_v, buf, ld_sem, st_sem):
        wid = _sc_worker_id()
        pltpu.sync_copy(d_hbm.at[wid], idx_v)

        def gather(c):
            k, j = windows[c]
            return pltpu.async_copy(y_hbm.at[idx_v.at[k, j]], buf.at[c % 2], ld_sem.at[c % 2])

        def store(c):
            k, j = windows[c]
            dst = o_hbm.at[pl.ds(k * n + wid * per_w + j * SC_WINDOW, SC_WINDOW)]
            return pltpu.async_copy(buf.at[c % 2], dst, st_sem.at[c % 2])

        loads = {0: gather(0)}
        stores = {}
        for c in range(len(windows)):
            loads[c].wait()
            if c >= 1:
                stores[c - 1].wait()
            if c + 1 < len(windows):
                loads[c + 1] = gather(c + 1)
            stores[c] = store(c)
        stores[len(windows) - 1].wait()

    return pl.kernel(
        body,
        out_type=jax.ShapeDtypeStruct((2 * n, width), jnp.int32),
        mesh=_sc_mesh(),
        scratch_types=[pltpu.VMEM((2, n_ch, SC_WINDOW), jnp.int32),
                       pltpu.VMEM((2, SC_WINDOW, width), jnp.int32),
                       pltpu.SemaphoreType.DMA((2,)),
                       pltpu.SemaphoreType.DMA((2,))],
        name="sc_collect",
    )(rows, idx)


def _expert_body(tr_ref, re_ref, nr_ref, nu_ref, xs_hbm, w1_hbm, w3_hbm, w2_hbm, o_ref,
                 xbuf, w1s, w3s, w2s, xsem, sem):
    i = pl.program_id(0)
    n_used = nu_ref[0]
    tme = xbuf.shape[1]
    run = tr_ref[i]
    first = jnp.logical_or(i == 0, run != tr_ref[jnp.maximum(i - 1, 0)])
    slot = run % 2

    def row_copy(t):
        s = t % EXPERT_ROW_BUFS
        start = t * tme if isinstance(t, int) else pl.multiple_of(t * tme, tme)
        src = xs_hbm.at[pl.ds(start, tme)]
        return pltpu.make_async_copy(src, xbuf.at[s], xsem.at[s])

    @pl.when(i == 0)
    def _():
        for t in range(EXPERT_ROW_BUFS - 1):
            @pl.when(t < n_used)
            def _():
                row_copy(t).start()

    def weight_copies(r, dst_slot):
        e = re_ref[r]
        return (pltpu.make_async_copy(w1_hbm.at[e], w1s.at[dst_slot], sem.at[0, dst_slot]),
                pltpu.make_async_copy(w3_hbm.at[e], w3s.at[dst_slot], sem.at[1, dst_slot]),
                pltpu.make_async_copy(w2_hbm.at[e], w2s.at[dst_slot], sem.at[2, dst_slot]))

    @pl.when(i == 0)
    def _():
        for cp in weight_copies(0, 0):
            cp.start()

    @pl.when(first)
    def _():
        for cp in weight_copies(run, slot):
            cp.wait()

        @pl.when(run + 1 < nr_ref[0])
        def _():
            for cp in weight_copies(run + 1, 1 - slot):
                cp.start()

    @pl.when(i < n_used)
    def _():
        @pl.when(i + (EXPERT_ROW_BUFS - 1) < n_used)
        def _():
            row_copy(i + (EXPERT_ROW_BUFS - 1)).start()

        row_copy(i).wait()
        lo, hi = _unpack_halves(xbuf[i % EXPERT_ROW_BUFS])
        lo = lo.astype(BF16)
        hi = hi.astype(BF16)
        half = D // 2
        a = _dot(lo, w1s[slot, 0:half, :]) + _dot(hi, w1s[slot, half:D, :])
        b = _dot(lo, w3s[slot, 0:half, :]) + _dot(hi, w3s[slot, half:D, :])
        z = (a * _sigmoid(a)) * b
        o_ref[...] = _pack_halves(_dot(z.astype(BF16), w2s[slot]))

    @pl.when(i >= n_used)
    def _():
        o_ref[...] = jnp.zeros_like(o_ref)


def _experts(tile_run, run_e, n_runs, n_used, xs, w1, w3, w2, tme):
    n_slots = xs.shape[0]
    grid_spec = pltpu.PrefetchScalarGridSpec(
        num_scalar_prefetch=4,
        grid=(n_slots // tme,),
        in_specs=[pl.BlockSpec(memory_space=pl.ANY),
                  pl.BlockSpec(memory_space=pl.ANY),
                  pl.BlockSpec(memory_space=pl.ANY),
                  pl.BlockSpec(memory_space=pl.ANY)],
        out_specs=pl.BlockSpec((tme, D // 2), lambda i, *_: (i, 0)),
        scratch_shapes=[pltpu.VMEM((EXPERT_ROW_BUFS, tme, D // 2), jnp.int32),
                        pltpu.VMEM((2, D, D_EXPERT), BF16),
                        pltpu.VMEM((2, D, D_EXPERT), BF16),
                        pltpu.VMEM((2, D_EXPERT, D), BF16),
                        pltpu.SemaphoreType.DMA((EXPERT_ROW_BUFS,)),
                        pltpu.SemaphoreType.DMA((3, 2))],
    )
    return pl.pallas_call(
        _expert_body,
        grid_spec=grid_spec,
        out_shape=jax.ShapeDtypeStruct((n_slots, D // 2), jnp.int32),
        compiler_params=_params(("arbitrary",)),
        name="experts",
    )(tile_run, run_e, n_runs, n_used, xs, w1, w3, w2)


def _final_body(x1_ref, mod_ref, y0_ref, y1_ref, wt_ref, gf_ref, o_ref):
    w0 = wt_ref[:, 0:1]
    w1 = wt_ref[:, 1:2]
    lo0, hi0 = _unpack_halves(y0_ref[...])
    lo1, hi1 = _unpack_halves(y1_ref[...])
    moe = jnp.concatenate([w0 * lo0 + w1 * lo1, w0 * hi0 + w1 * hi1], axis=1)
    x2 = x1_ref[...] + mod_ref[0, 5:6, :] * moe
    ms = jnp.mean(x2 * x2, axis=-1, keepdims=True)
    o_ref[...] = x2 * lax.rsqrt(ms + EPS) * gf_ref[...]


def _final(x1, mod, tiles_per_mod, yg, wts, g_final):
    n = x1.shape[0]
    tm = FIN_TM
    nt = n // tm
    return pl.pallas_call(
        _final_body,
        grid=(nt,),
        in_specs=[pl.BlockSpec((tm, D), lambda i: (i, 0)),
                  pl.BlockSpec((1, 6, D), lambda i: (i // tiles_per_mod, 0, 0)),
                  pl.BlockSpec((tm, D // 2), lambda i: (i, 0)),
                  pl.BlockSpec((tm, D // 2), lambda i: (i + nt, 0)),
                  pl.BlockSpec((tm, V7X_LANES), lambda i: (i, 0)),
                  _const_spec((1, D))],
        out_specs=pl.BlockSpec((tm, D), lambda i: (i, 0)),
        out_shape=jax.ShapeDtypeStruct((n, D), F32),
        compiler_params=_params(("parallel",)),
        name="final",
    )(x1, mod, yg, yg, wts, g_final)


def _slot_plan(route, cnt, n, tme):
    counts = cnt[:, 0]
    padded = ((counts + tme - 1) // tme) * tme
    pend = jnp.cumsum(padded)
    pstart = pend - padded
    onehot = route[0:2, :, None] == jnp.arange(N_EXPERTS, dtype=jnp.int32)[None, None, :]
    dest = jnp.sum(jnp.where(onehot, pstart[None, None, :], 0), axis=-1) + route[2:4]
    n_slots = ((2 * n + N_EXPERTS * (tme - 1)) // tme) * tme
    tile_start = jnp.arange(n_slots // tme, dtype=jnp.int32) * tme
    tile_e = jnp.sum((tile_start[:, None] >= pend[None, :]).astype(jnp.int32), axis=1)
    tile_e = jnp.minimum(tile_e, N_EXPERTS - 1)
    n_used = (pend[-1] // tme).astype(jnp.int32).reshape(1)
    used = counts > 0
    run_of_e = jnp.cumsum(used.astype(jnp.int32)) - 1
    eids = jnp.arange(N_EXPERTS, dtype=jnp.int32)
    run_e = jnp.sum(jnp.where(used[None, :] & (run_of_e[None, :] == eids[:, None]), eids[None, :], 0), axis=1)
    tile_run = jnp.sum(jnp.where(tile_e[:, None] == eids[None, :], run_of_e[None, :], 0), axis=1)
    n_runs = jnp.sum(used.astype(jnp.int32)).reshape(1)
    plan = (tile_run.astype(jnp.int32), run_e.astype(jnp.int32), n_runs, n_used)
    return dest.astype(jnp.int32), plan, n_slots


def _group(x, mod, mod_per_seq, h0, p, row_len, tme, cast_ws):
    n_b, seq_len, _ = x.shape
    n = n_b * seq_len
    xt = x.reshape(n, D)
    mod_seq = mod if mod_per_seq else jnp.broadcast_to(mod, (n_b, 6, D))
    xc = _xr_conv(x, mod_seq, p["g1"], p["w_xr"], p["rnn_conv_w"], p["rnn_conv_b"])
    hf, hb, last = _rglru_scan(xc, h0, p["wg"], p["ba"], p["bx"], p["lam"])
    tiles_per_mod = (seq_len // MIX_TM) if mod_per_seq else (n // MIX_TM)
    outs = _mixer(xt, mod, tiles_per_mod, hf.reshape(n, D), hb.reshape(n, D),
                                     p["g1"], p["g2"], p["w_in"], p["conv_w"], p["wco"], p["wro"],
                                     p["wo"], p["wr"], p["br"], row_len, cast_ws)
    x1, h2, route, wts, cnt = outs[:5]
    casts = outs[5:]
    dest, plan, n_slots = _slot_plan(route, cnt, n, tme)
    xs = _sc_dispatch(h2, dest, n_slots)

    def finish(w1b, w3b, w2b):
        ys = _experts(*plan, xs, w1b, w3b, w2b, tme)
        yg = _sc_collect(ys, dest)
        tiles_per_mod_f = (seq_len // FIN_TM) if mod_per_seq else (n // FIN_TM)
        y = _final(x1, mod, tiles_per_mod_f, yg, wts, p["g_final"])
        return y.reshape(n_b, seq_len, D)

    return finish, last, casts


def kernel(x_prompt, x_sample, state_rnn, c, c_ctx, w_ada, b_ada, g_norm1, g_norm2, w_in, conv_w, w_conv_out, rnn_conv_w, rnn_conv_b, w_gate_a, b_gate_a, w_gate_x, b_gate_x, lam, w_rnn_out, w_o, w_router_group, b_router_group, w_router_expert, b_router_expert, w1, w3, w2, g_final):
    assert w_ada.shape[0] == 1, "single layer"
    n_pb, n_sb = x_prompt.shape[0], x_sample.shape[0]

    cond = jnp.zeros((16, D), F32).at[0].set(c_ctx).at[1:1 + n_sb].set(c)
    mod = _ada(cond, w_ada[0], b_ada[0]).reshape(16, 6, D)

    w_in_b = w_in[0].astype(BF16)
    wr = jnp.zeros((D, ROUTE_PAD), F32)
    wr = wr.at[:, :N_GROUPS].set(w_router_group[0]).at[:, EPG:EPG + N_EXPERTS].set(w_router_expert[0])
    br = jnp.zeros((1, ROUTE_PAD), F32).at[0, N_GROUPS:EPG].set(NEG_BIG)
    br = br.at[0, :N_GROUPS].set(b_router_group[0]).at[0, EPG:EPG + N_EXPERTS].set(b_router_expert[0])
    wr_hi = wr.astype(BF16)
    p = dict(
        g1=g_norm1, g2=g_norm2, w_in=w_in_b, w_xr=w_in_b[:, 3 * D:4 * D],
        conv_w=conv_w[0], rnn_conv_w=rnn_conv_w[0], rnn_conv_b=rnn_conv_b,
        wg=(0.5 * jnp.concatenate([w_gate_a[0], w_gate_x[0]], axis=-1)).astype(BF16),
        ba=b_gate_a[0], bx=b_gate_x[0], lam=lam[0],
        wco=w_conv_out[0].astype(BF16), wro=w_rnn_out[0].astype(BF16), wo=w_o[0].astype(BF16),
        wr=jnp.concatenate([wr_hi, (wr - wr_hi.astype(F32)).astype(BF16)], axis=1), br=br,
        g_final=g_final.reshape(1, D),
    )

    h0_p = jnp.zeros((n_pb // SCAN_B, 2, SCAN_B, D), F32)
    finish_p, last, (w2b,) = _group(x_prompt, mod[0:1], False, h0_p, p, x_prompt.shape[1], 512,
                                    [w2[0]])
    state_new = last.transpose(0, 2, 1, 3).reshape(n_pb, 1, 2, D)

    h0_s = state_rnn[:, 0].reshape(n_sb // SCAN_B, SCAN_B, 2, D).transpose(0, 2, 1, 3)
    finish_s, _, (w1b, w3b) = _group(x_sample, mod[1:1 + n_sb], True, h0_s, p, GRID_W, 512,
                                     [w1[0], w3[0]])
    return (finish_p(w1b, w3b, w2b), finish_s(w1b, w3b, w2b), state_new)
```

```python
import functools

import jax
import jax.numpy as jnp
from jax import lax
from jax.experimental import pallas as pl
from jax.experimental.pallas import tpu as pltpu
from jax.experimental.pallas import tpu_sc as plsc

D = 1024
N_HEADS = 8
HEAD = D // N_HEADS
GRID_W = 64
RG_C = 8.0
N_GROUPS = 4
EPG = 8
N_EXPERTS = N_GROUPS * EPG
D_EXPERT = 512
EPS = 1e-6
F32 = jnp.float32
BF16 = jnp.bfloat16

V7X_LANES = 128
V7X_SUBLANES = 8
V7X_VMEM_LIMIT_BYTES = 56 * 1024 * 1024
V7X_SC_CORES = 2
V7X_SC_SUBCORES = 16
V7X_SC_WORKERS = V7X_SC_CORES * V7X_SC_SUBCORES
SC_WINDOW = 64

XR_T = 256
XR_SUB = 32
XR_LEFT = 2
SCAN_T = 128
SCAN_B = V7X_SUBLANES
SCAN_UNROLL = 8
LOG2_E = 1.4426950408889634
TINY = 1e-30
MIX_TM = 512
FIN_TM = 1024
EXPERT_ROW_BUFS = 3
ROUTE_PAD = 128
NEG_BIG = -1e30


def _sigmoid(x):
    return 0.5 * jnp.tanh(0.5 * x) + 0.5


def _norm_mod(x, g, scale, shift):
    ms = jnp.mean(x * x, axis=-1, keepdims=True)
    return (x * lax.rsqrt(ms + EPS)) * (g * (1.0 + scale)) + shift


def _dot(a, b):
    return jnp.dot(a, b, preferred_element_type=F32)


def _params(sem, vmem=V7X_VMEM_LIMIT_BYTES):
    return pltpu.CompilerParams(dimension_semantics=sem, vmem_limit_bytes=vmem)


def _const_spec(shape):
    zeros = (0,) * len(shape)
    return pl.BlockSpec(shape, lambda *_: zeros, pipeline_mode=pl.Buffered(1))


def _ada_body(c_ref, w_ref, b_ref, o_ref):
    c = c_ref[...]
    s = (c * _sigmoid(c)).astype(BF16)
    o_ref[...] = _dot(s, w_ref[...].astype(BF16)) + b_ref[...]


def _ada(cond, w_ada, b_ada):
    rows = cond.shape[0]
    n_out = w_ada.shape[1]
    return pl.pallas_call(
        _ada_body,
        grid=(n_out // D,),
        in_specs=[pl.BlockSpec((rows, D), lambda i: (0, 0)),
                  pl.BlockSpec((D, D), lambda i: (0, i)),
                  pl.BlockSpec((1, D), lambda i: (0, i))],
        out_specs=pl.BlockSpec((rows, D), lambda i: (0, i)),
        out_shape=jax.ShapeDtypeStruct((rows, n_out), F32),
        compiler_params=_params(("parallel",)),
        name="ada",
    )(cond, w_ada, b_ada.reshape(1, n_out))


def _xr_body(x_ref, xn_ref, mod_ref, g_ref, w_ref, cw_ref, cb_ref, o_ref, xt):
    j = pl.program_id(1)
    n_t = pl.num_programs(1)
    t_len = x_ref.shape[1]
    nb = SCAN_B
    body0 = XR_LEFT * nb

    @pl.when(j == 0)
    def _():
        xt[:, 0:body0, :] = jnp.zeros((N_HEADS, body0, HEAD), F32)

    @pl.when(j > 0)
    def _():
        xt[:, 0:body0, :] = xt[:, t_len * nb:t_len * nb + body0, :]

    g = g_ref[...]
    for b in range(nb):
        h = _norm_mod(x_ref[b], g, mod_ref[b, 1:2, :], mod_ref[b, 0:1, :]).astype(BF16)
        r = _dot(h, w_ref[...])
        for s in range(N_HEADS):
            xt[s, pl.ds(body0 + b, t_len, stride=nb), :] = r[:, s * HEAD:(s + 1) * HEAD]

    hn = _norm_mod(xn_ref[:, 0, :], g, mod_ref[:, 1, :], mod_ref[:, 0, :]).astype(BF16)
    rn = jnp.where(j < n_t - 1, _dot(hn, w_ref[...]), 0.0)
    tail = body0 + t_len * nb
    for s in range(N_HEADS):
        xt[s, tail:tail + nb, :] = rn[:, s * HEAD:(s + 1) * HEAD]

    sub = XR_SUB
    for s in range(N_HEADS):
        sl = slice(s * HEAD, (s + 1) * HEAD)
        for t0 in range(0, t_len, sub):
            y = cb_ref[:, sl]
            for k in range(4):
                r0 = (t0 + k) * nb
                y = y + xt[s, r0:r0 + sub * nb, :] * cw_ref[k:k + 1, sl]
            o_ref[t0:t0 + sub, :, sl] = y.reshape(sub, nb, HEAD)


def _xr_conv(x, mod_seq, g1, w_xr, cw, cb):
    n_b, seq_len, _ = x.shape
    t_len = min(XR_T, seq_len)
    assert n_b % SCAN_B == 0 and seq_len % t_len == 0 and t_len % XR_SUB == 0
    n_t = seq_len // t_len
    nxt = t_len // V7X_SUBLANES
    last_blk = seq_len // V7X_SUBLANES - 1
    return pl.pallas_call(
        _xr_body,
        grid=(n_b // SCAN_B, n_t),
        in_specs=[pl.BlockSpec((SCAN_B, t_len, D), lambda g, j: (g, j, 0)),
                  pl.BlockSpec((SCAN_B, V7X_SUBLANES, D),
                               lambda g, j: (g, jnp.minimum((j + 1) * nxt, last_blk), 0)),
                  pl.BlockSpec((SCAN_B, 6, D), lambda g, j: (g, 0, 0)),
                  _const_spec((1, D)),
                  _const_spec((D, D)),
                  _const_spec((4, D)),
                  _const_spec((1, D))],
        out_specs=pl.BlockSpec((t_len, SCAN_B, D), lambda g, j: (j, g, 0)),
        out_shape=jax.ShapeDtypeStruct((seq_len, n_b, D), F32),
        scratch_shapes=[pltpu.VMEM((N_HEADS, (t_len + XR_LEFT + 1) * SCAN_B, HEAD), F32)],
        compiler_params=_params(("parallel", "arbitrary")),
        name="xr_conv",
    )(x, x, mod_seq, g1, w_xr, cw, cb)


def _scan_body(xf_ref, xb_ref, h0_ref, wg_ref, ba_ref, bx_ref, lam_ref,
               hf_ref, hb_ref, last_ref, a_s, u_s, h_s, hc):
    j = pl.program_id(1)
    n_t = pl.num_programs(1)
    t_len = xf_ref.shape[0]
    rows = SCAN_B * t_len

    @pl.when(j == 0)
    def _():
        for d in range(2):
            for s in range(N_HEADS):
                hc[d, s] = h0_ref[0, d, :, s * HEAD:(s + 1) * HEAD]

    for d, x_ref in ((0, xf_ref), (1, xb_ref)):
        z = -lam_ref[d:d + 1, :]
        sp = jnp.maximum(z, 0.0) + jnp.log(1.0 + jnp.exp(-jnp.abs(z)))
        c2 = (-0.5 * RG_C * LOG2_E) * sp
        for hd in range(N_HEADS):
            sl = slice(hd * HEAD, (hd + 1) * HEAD)
            xh = x_ref[:, :, sl].reshape(rows, HEAD)
            g = _dot(xh.astype(BF16), wg_ref[d, hd])
            t_r = jnp.tanh(g[:, :HEAD] + 0.5 * ba_ref[d:d + 1, sl])
            t_i = jnp.tanh(g[:, HEAD:] + 0.5 * bx_ref[d:d + 1, sl])
            a = jnp.exp2(c2[:, sl] * t_r + c2[:, sl])
            q = 0.25 - 0.25 * (a * a)
            half_mult = q * lax.rsqrt(jnp.maximum(q, TINY))
            u = half_mult * ((t_i + 1.0) * xh)
            a_s[d, hd] = a
            u_s[d, hd] = u

    def step(tb, carry):
        h_f, h_b = carry
        for k in range(SCAN_UNROLL):
            t = tb * SCAN_UNROLL + k
            rf = pl.multiple_of(t * SCAN_B, SCAN_B)
            rb = pl.multiple_of((t_len - 1 - t) * SCAN_B, SCAN_B)
            h_f = a_s[0, :, pl.ds(rf, SCAN_B), :] * h_f + u_s[0, :, pl.ds(rf, SCAN_B), :]
            h_b = a_s[1, :, pl.ds(rb, SCAN_B), :] * h_b + u_s[1, :, pl.ds(rb, SCAN_B), :]
            h_s[0, :, pl.ds(rf, SCAN_B), :] = h_f
            h_s[1, :, pl.ds(rb, SCAN_B), :] = h_b
        return h_f, h_b

    h_f, h_b = lax.fori_loop(0, t_len // SCAN_UNROLL, step, (hc[0], hc[1]))
    hc[0] = h_f
    hc[1] = h_b

    for d, o_ref in ((0, hf_ref), (1, hb_ref)):
        for hd in range(N_HEADS):
            for b in range(SCAN_B):
                o_ref[b, :, hd * HEAD:(hd + 1) * HEAD] = (
                    h_s[d, hd, pl.ds(b, t_len, stride=SCAN_B), :].astype(BF16))

    @pl.when(j == n_t - 1)
    def _():
        for d in range(2):
            for s in range(N_HEADS):
                last_ref[0, d, :, s * HEAD:(s + 1) * HEAD] = hc[d, s]


def _rglru_scan(xc, h0, wg, ba, bx, lam):
    seq_len, n_b, _ = xc.shape
    n_g = n_b // SCAN_B
    n_t = seq_len // SCAN_T
    blk = (SCAN_B, SCAN_T, D)
    blk_in = (SCAN_T, SCAN_B, D)
    return pl.pallas_call(
        _scan_body,
        grid=(n_g, n_t),
        in_specs=[pl.BlockSpec(blk_in, lambda g, j: (j, g, 0)),
                  pl.BlockSpec(blk_in, lambda g, j: (n_t - 1 - j, g, 0)),
                  pl.BlockSpec((1, 2, SCAN_B, D), lambda g, j: (g, 0, 0, 0)),
                  _const_spec((2, N_HEADS, HEAD, 2 * HEAD)),
                  _const_spec((2, D)),
                  _const_spec((2, D)),
                  _const_spec((2, D))],
        out_specs=[pl.BlockSpec(blk, lambda g, j: (g, j, 0)),
                   pl.BlockSpec(blk, lambda g, j: (g, n_t - 1 - j, 0)),
                   pl.BlockSpec((1, 2, SCAN_B, D), lambda g, j: (g, 0, 0, 0))],
        out_shape=[jax.ShapeDtypeStruct((n_b, seq_len, D), BF16),
                   jax.ShapeDtypeStruct((n_b, seq_len, D), BF16),
                   jax.ShapeDtypeStruct((n_g, 2, SCAN_B, D), F32)],
        scratch_shapes=[pltpu.VMEM((2, N_HEADS, SCAN_T * SCAN_B, HEAD), F32),
                        pltpu.VMEM((2, N_HEADS, SCAN_T * SCAN_B, HEAD), F32),
                        pltpu.VMEM((2, N_HEADS, SCAN_T * SCAN_B, HEAD), F32),
                        pltpu.VMEM((2, N_HEADS, SCAN_B, HEAD), F32)],
        compiler_params=_params(("parallel", "arbitrary")),
        name="rglru_scan",
    )(xc, xc, h0, wg, ba, bx, lam)


def _pack_halves(v):
    half = v.shape[1] // 2
    lo = lax.bitcast_convert_type(v[:, :half].astype(BF16).astype(F32), jnp.uint32)
    hi = lax.bitcast_convert_type(v[:, half:].astype(BF16).astype(F32), jnp.uint32)
    return lax.bitcast_convert_type((lo >> 16) | (hi & jnp.uint32(0xFFFF0000)), jnp.int32)


def _unpack_halves(p):
    u = lax.bitcast_convert_type(p, jnp.uint32)
    lo = lax.bitcast_convert_type(u << 16, F32)
    hi = lax.bitcast_convert_type(u & jnp.uint32(0xFFFF0000), F32)
    return lo, hi


def _mix_body(x_ref, mod_ref, hf_ref, hb_ref, g1_ref, g2_ref, win_ref, cw_ref,
              wco_ref, wro_ref, wo_ref, wr_ref, br_ref, *rest, row_len, n_cast):
    cast_in = rest[:n_cast]
    x1_ref, h2_ref, route_ref, wts_ref, cnt_ref = rest[n_cast:n_cast + 5]
    cast_out = rest[n_cast + 5:2 * n_cast + 5]
    seen = rest[2 * n_cast + 5]
    x = x_ref[...]
    tm = x.shape[0]
    h = _norm_mod(x, g1_ref[...], mod_ref[0, 1:2, :], mod_ref[0, 0:1, :]).astype(BF16)

    def proj(k):
        return _dot(h, win_ref[:, k * D:(k + 1) * D])

    cv = proj(1) * proj(2)
    pos = lax.broadcasted_iota(jnp.int32, (tm, 1), 0) % row_len
    conv = cv * cw_ref[1:2, :]
    conv = conv + jnp.where(pos >= 1, pltpu.roll(cv, 1, 0), 0.0) * cw_ref[0:1, :]
    conv = conv + jnp.where(pos <= row_len - 2, pltpu.roll(cv, tm - 1, 0), 0.0) * cw_ref[2:3, :]
    y_a = _dot((proj(0) * conv).astype(BF16), wco_ref[...])
    merged = _sigmoid(proj(5)) * y_a

    hs = hf_ref[...].astype(F32) + hb_ref[...].astype(F32)
    y_b = _dot((hs * jax.nn.gelu(proj(4))).astype(BF16), wro_ref[...])
    merged = merged + _sigmoid(proj(6)) * y_b

    mix = _dot(merged.astype(BF16), wo_ref[...])
    x1 = x + mod_ref[0, 2:3, :] * mix
    x1_ref[...] = x1
    for src, dst in zip(cast_in, cast_out):
        dst[...] = src[...].astype(BF16)
    h2 = _norm_mod(x1, g2_ref[...], mod_ref[0, 4:5, :], mod_ref[0, 3:4, :])
    h2_ref[...] = _pack_halves(h2)

    h2_hi = h2.astype(BF16)
    h2_lo = (h2 - h2_hi.astype(F32)).astype(BF16)
    big = _dot(h2_hi, wr_ref[...])
    logits = (big[:, :ROUTE_PAD] + big[:, ROUTE_PAD:]
              + _dot(h2_lo, wr_ref[:, :ROUTE_PAD]) + br_ref[...])
    lt = logits.T
    row = lax.broadcasted_iota(jnp.int32, (EPG, tm), 0)
    lg = lt[0:EPG]
    mg = jnp.max(lg, axis=0, keepdims=True)
    p_grp = 1.0 / jnp.sum(jnp.exp(lg - mg), axis=0, keepdims=True)
    grp = jnp.min(jnp.where(lg == mg, row, EPG), axis=0, keepdims=True)
    le = lt[EPG * N_GROUPS:EPG * (N_GROUPS + 1)]
    for g in range(N_GROUPS - 2, -1, -1):
        le = jnp.where(grp == g, lt[EPG * (g + 1):EPG * (g + 2)], le)
    me = jnp.max(le, axis=0, keepdims=True)
    ee = jnp.exp(le - me)
    pe = ee / jnp.sum(ee, axis=0, keepdims=True)
    p1 = jnp.max(pe, axis=0, keepdims=True)
    i1 = jnp.min(jnp.where(pe == p1, row, EPG), axis=0, keepdims=True)
    pe2 = jnp.where(row == i1, -1.0, pe)
    p2 = jnp.max(pe2, axis=0, keepdims=True)
    i2 = jnp.min(jnp.where(pe2 == p2, row, EPG), axis=0, keepdims=True)
    den = p1 + p2
    e1 = grp * EPG + i1
    e2 = grp * EPG + i2

    @pl.when(pl.program_id(0) == 0)
    def _():
        seen[...] = jnp.zeros_like(seen)

    erow = lax.broadcasted_iota(jnp.int32, (N_EXPERTS, tm), 0)
    hit1 = erow == e1
    hit2 = erow == e2
    both = jnp.where(jnp.logical_or(hit1, hit2), 1.0, 0.0)
    tri = jnp.where(lax.broadcasted_iota(jnp.int32, (tm, tm), 0)
                    <= lax.broadcasted_iota(jnp.int32, (tm, tm), 1), 1.0, 0.0).astype(BF16)
    before = _dot(both.astype(BF16), tri) - both + seen[...]
    r1 = jnp.sum(jnp.where(hit1, before, 0.0), axis=0, keepdims=True).astype(jnp.int32)
    r2 = jnp.sum(jnp.where(hit2, before, 0.0), axis=0, keepdims=True).astype(jnp.int32)
    total = seen[...] + jnp.sum(both, axis=1, keepdims=True)
    seen[...] = total
    cnt_ref[...] = total[:, :V7X_LANES].astype(jnp.int32)

    route_ref[...] = jnp.where(row == 0, e1, jnp.where(row == 1, e2, jnp.where(
        row == 2, r1, jnp.where(row == 3, r2, 0))))
    w8 = jnp.where(row == 0, p_grp * p1 / den, jnp.where(row == 1, p_grp * p2 / den, 0.0))
    wts_ref[...] = jnp.concatenate([w8, jnp.zeros((V7X_LANES - EPG, tm), F32)], axis=0).T


def _mixer(x, mod, tiles_per_mod, hf, hb, g1, g2, w_in, cw, wco, wro, wo, wr, br, row_len, cast_ws):
    n = x.shape[0]
    tm = MIX_TM
    assert tm % row_len == 0 and n % tm == 0
    steps = n // tm
    assert N_EXPERTS % steps == 0
    epb = N_EXPERTS // steps
    cast_specs = [pl.BlockSpec((epb,) + w.shape[1:], lambda i: (i, 0, 0)) for w in cast_ws]
    cast_shapes = [jax.ShapeDtypeStruct(w.shape, BF16) for w in cast_ws]
    mod_map = lambda i: (i // tiles_per_mod, 0, 0)
    tok = lambda i: (i, 0)
    col = lambda i: (0, i)
    return pl.pallas_call(
        functools.partial(_mix_body, row_len=row_len, n_cast=len(cast_ws)),
        grid=(steps,),
        in_specs=[pl.BlockSpec((tm, D), tok),
                  pl.BlockSpec((1, 6, D), mod_map),
                  pl.BlockSpec((tm, D), tok),
                  pl.BlockSpec((tm, D), tok),
                  _const_spec((1, D)),
                  _const_spec((1, D)),
                  _const_spec(w_in.shape),
                  _const_spec((3, D)),
                  _const_spec((D, D)),
                  _const_spec((D, D)),
                  _const_spec((D, D)),
                  _const_spec((D, 2 * ROUTE_PAD)),
                  _const_spec((1, ROUTE_PAD))] + cast_specs,
        out_specs=[pl.BlockSpec((tm, D), tok),
                   pl.BlockSpec((tm, D // 2), tok),
                   pl.BlockSpec((EPG, tm), col),
                   pl.BlockSpec((tm, V7X_LANES), tok),
                   pl.BlockSpec((N_EXPERTS, V7X_LANES), lambda i: (0, 0))] + cast_specs,
        out_shape=[jax.ShapeDtypeStruct((n, D), F32),
                   jax.ShapeDtypeStruct((n, D // 2), jnp.int32),
                   jax.ShapeDtypeStruct((EPG, n), jnp.int32),
                   jax.ShapeDtypeStruct((n, V7X_LANES), F32),
                   jax.ShapeDtypeStruct((N_EXPERTS, V7X_LANES), jnp.int32)] + cast_shapes,
        scratch_shapes=[pltpu.VMEM((N_EXPERTS, tm), F32)],
        compiler_params=_params(("arbitrary",)),
        name="mixer",
    )(x, mod, hf, hb, g1, g2, w_in, cw, wco, wro, wo, wr, br, *cast_ws)


def _sc_mesh():
    return plsc.VectorSubcoreMesh(core_axis_name="c", subcore_axis_name="s",
                                  num_cores=V7X_SC_CORES, num_subcores=V7X_SC_SUBCORES)


def _sc_worker_id():
    return lax.axis_index("s") * V7X_SC_CORES + lax.axis_index("c")


def _sc_dispatch(rows, dest, n_slots):
    n, width = rows.shape
    per_w = n // V7X_SC_WORKERS
    n_ch = per_w // SC_WINDOW
    assert n_ch * SC_WINDOW * V7X_SC_WORKERS == n
    idx = dest.reshape(2, V7X_SC_WORKERS, n_ch, SC_WINDOW).transpose(1, 0, 2, 3)

    def body(x_hbm, d_hbm, o_hbm, idx_v, buf, ld_sem, st_sem):
        wid = _sc_worker_id()
        pltpu.sync_copy(d_hbm.at[wid], idx_v)

        def load(j):
            src = x_hbm.at[pl.ds(wid * per_w + j * SC_WINDOW, SC_WINDOW)]
            return pltpu.async_copy(src, buf.at[j % 2], ld_sem.at[j % 2])

        def scatter(j):
            return [pltpu.async_copy(buf.at[j % 2], o_hbm.at[idx_v.at[k, j]], st_sem.at[j % 2])
                    for k in range(2)]

        loads = {0: load(0)}
        stores = {}
        for j in range(n_ch):
            loads[j].wait()
            if j >= 1:
                for cp in stores[j - 1]:
                    cp.wait()
            if j + 1 < n_ch:
                loads[j + 1] = load(j + 1)
            stores[j] = scatter(j)
        for cp in stores[n_ch - 1]:
            cp.wait()

    return pl.kernel(
        body,
        out_type=jax.ShapeDtypeStruct((n_slots, width), jnp.int32),
        mesh=_sc_mesh(),
        scratch_types=[pltpu.VMEM((2, n_ch, SC_WINDOW), jnp.int32),
                       pltpu.VMEM((2, SC_WINDOW, width), jnp.int32),
                       pltpu.SemaphoreType.DMA((2,)),
                       pltpu.SemaphoreType.DMA((2,))],
        name="sc_dispatch",
    )(rows, idx)


def _sc_collect(rows, dest):
    n = dest.shape[1]
    width = rows.shape[1]
    per_w = n // V7X_SC_WORKERS
    n_ch = per_w // SC_WINDOW
    assert n_ch * SC_WINDOW * V7X_SC_WORKERS == n
    idx = dest.reshape(2, V7X_SC_WORKERS, n_ch, SC_WINDOW).transpose(1, 0, 2, 3)
    windows = [(k, j) for k in range(2) for j in range(n_ch)]

    def body(y_hbm, d_hbm, o_hbm, idx_v, buf, ld_sem, st_sem):
        wid = _sc_worker_id()
        pltpu.sync_copy(d_hbm.at[wid], idx_v)

        def gather(c):
            k, j = windows[c]
            return pltpu.async_copy(y_hbm.at[idx_v.at[k, j]], buf.at[c % 2], ld_sem.at[c % 2])

        def store(c):
            k, j = windows[c]
            dst = o_hbm.at[pl.ds(k * n + wid * per_w + j * SC_WINDOW, SC_WINDOW)]
            return pltpu.async_copy(buf.at[c % 2], dst, st_sem.at[c % 2])

        loads = {0: gather(0)}
        stores = {}
        for c in range(len(windows)):
            loads[c].wait()
            if c >= 1:
                stores[c - 1].wait()
            if c + 1 < len(windows):
                loads[c + 1] = gather(c + 1)
            stores[c] = store(c)
        stores[len(windows) - 1].wait()

    return pl.kernel(
        body,
        out_type=jax.ShapeDtypeStruct((2 * n, width), jnp.int32),
        mesh=_sc_mesh(),
        scratch_types=[pltpu.VMEM((2, n_ch, SC_WINDOW), jnp.int32),
                       pltpu.VMEM((2, SC_WINDOW, width), jnp.int32),
                       pltpu.SemaphoreType.DMA((2,)),
                       pltpu.SemaphoreType.DMA((2,))],
        name="sc_collect",
    )(rows, idx)


def _expert_body(tr_ref, re_ref, nr_ref, nu_ref, xs_hbm, w1_hbm, w3_hbm, w2_hbm, o_ref,
                 xbuf, w1s, w3s, w2s, xsem, sem):
    i = pl.program_id(0)
    n_used = nu_ref[0]
    tme = xbuf.shape[1]
    run = tr_ref[i]
    first = jnp.logical_or(i == 0, run != tr_ref[jnp.maximum(i - 1, 0)])
    slot = run % 2

    def row_copy(t):
        s = t % EXPERT_ROW_BUFS
        start = t * tme if isinstance(t, int) else pl.multiple_of(t * tme, tme)
        src = xs_hbm.at[pl.ds(start, tme)]
        return pltpu.make_async_copy(src, xbuf.at[s], xsem.at[s])

    @pl.when(i == 0)
    def _():
        for t in range(EXPERT_ROW_BUFS - 1):
            @pl.when(t < n_used)
            def _():
                row_copy(t).start()

    def weight_copies(r, dst_slot):
        e = re_ref[r]
        return (pltpu.make_async_copy(w1_hbm.at[e], w1s.at[dst_slot], sem.at[0, dst_slot]),
                pltpu.make_async_copy(w3_hbm.at[e], w3s.at[dst_slot], sem.at[1, dst_slot]),
                pltpu.make_async_copy(w2_hbm.at[e], w2s.at[dst_slot], sem.at[2, dst_slot]))

    @pl.when(i == 0)
    def _():
        for cp in weight_copies(0, 0):
            cp.start()

    @pl.when(first)
    def _():
        for cp in weight_copies(run, slot):
            cp.wait()

        @pl.when(run + 1 < nr_ref[0])
        def _():
            for cp in weight_copies(run + 1, 1 - slot):
                cp.start()

    @pl.when(i < n_used)
    def _():
        @pl.when(i + (EXPERT_ROW_BUFS - 1) < n_used)
        def _():
            row_copy(i + (EXPERT_ROW_BUFS - 1)).start()

        row_copy(i).wait()
        lo, hi = _unpack_halves(xbuf[i % EXPERT_ROW_BUFS])
        lo = lo.astype(BF16)
        hi = hi.astype(BF16)
        half = D // 2
        a = _dot(lo, w1s[slot, 0:half, :]) + _dot(hi, w1s[slot, half:D, :])
        b = _dot(lo, w3s[slot, 0:half, :]) + _dot(hi, w3s[slot, half:D, :])
        z = (a * _sigmoid(a)) * b
        o_ref[...] = _pack_halves(_dot(z.astype(BF16), w2s[slot]))


def _experts(tile_run, run_e, n_runs, n_used, xs, w1, w3, w2, tme):
    n_slots = xs.shape[0]
    grid_spec = pltpu.PrefetchScalarGridSpec(
        num_scalar_prefetch=4,
        grid=(n_slots // tme,),
        in_specs=[pl.BlockSpec(memory_space=pl.ANY),
                  pl.BlockSpec(memory_space=pl.ANY),
                  pl.BlockSpec(memory_space=pl.ANY),
                  pl.BlockSpec(memory_space=pl.ANY)],
        out_specs=pl.BlockSpec((tme, D // 2),
                               lambda i, tr, re, nr, nu: (jnp.minimum(i, nu[0] - 1), 0)),
        scratch_shapes=[pltpu.VMEM((EXPERT_ROW_BUFS, tme, D // 2), jnp.int32),
                        pltpu.VMEM((2, D, D_EXPERT), BF16),
                        pltpu.VMEM((2, D, D_EXPERT), BF16),
                        pltpu.VMEM((2, D_EXPERT, D), BF16),
                        pltpu.SemaphoreType.DMA((EXPERT_ROW_BUFS,)),
                        pltpu.SemaphoreType.DMA((3, 2))],
    )
    return pl.pallas_call(
        _expert_body,
        grid_spec=grid_spec,
        out_shape=jax.ShapeDtypeStruct((n_slots, D // 2), jnp.int32),
        compiler_params=_params(("arbitrary",)),
        name="experts",
    )(tile_run, run_e, n_runs, n_used, xs, w1, w3, w2)


def _final_body(x1_ref, mod_ref, y0_ref, y1_ref, wt_ref, gf_ref, o_ref):
    w0 = wt_ref[:, 0:1]
    w1 = wt_ref[:, 1:2]
    lo0, hi0 = _unpack_halves(y0_ref[...])
    lo1, hi1 = _unpack_halves(y1_ref[...])
    moe = jnp.concatenate([w0 * lo0 + w1 * lo1, w0 * hi0 + w1 * hi1], axis=1)
    x2 = x1_ref[...] + mod_ref[0, 5:6, :] * moe
    ms = jnp.mean(x2 * x2, axis=-1, keepdims=True)
    o_ref[...] = x2 * lax.rsqrt(ms + EPS) * gf_ref[...]


def _final(x1, mod, tiles_per_mod, yg, wts, g_final):
    n = x1.shape[0]
    tm = FIN_TM
    nt = n // tm
    return pl.pallas_call(
        _final_body,
        grid=(nt,),
        in_specs=[pl.BlockSpec((tm, D), lambda i: (i, 0)),
                  pl.BlockSpec((1, 6, D), lambda i: (i // tiles_per_mod, 0, 0)),
                  pl.BlockSpec((tm, D // 2), lambda i: (i, 0)),
                  pl.BlockSpec((tm, D // 2), lambda i: (i + nt, 0)),
                  pl.BlockSpec((tm, V7X_LANES), lambda i: (i, 0)),
                  _const_spec((1, D))],
        out_specs=pl.BlockSpec((tm, D), lambda i: (i, 0)),
        out_shape=jax.ShapeDtypeStruct((n, D), F32),
        compiler_params=_params(("parallel",)),
        name="final",
    )(x1, mod, yg, yg, wts, g_final)


def _slot_plan(route, cnt, n, tme):
    counts = cnt[:, 0]
    padded = ((counts + tme - 1) // tme) * tme
    pend = jnp.cumsum(padded)
    pstart = pend - padded
    onehot = route[0:2, :, None] == jnp.arange(N_EXPERTS, dtype=jnp.int32)[None, None, :]
    dest = jnp.sum(jnp.where(onehot, pstart[None, None, :], 0), axis=-1) + route[2:4]
    n_slots = ((2 * n + N_EXPERTS * (tme - 1)) // tme) * tme
    tile_start = jnp.arange(n_slots // tme, dtype=jnp.int32) * tme
    tile_e = jnp.sum((tile_start[:, None] >= pend[None, :]).astype(jnp.int32), axis=1)
    tile_e = jnp.minimum(tile_e, N_EXPERTS - 1)
    n_used = (pend[-1] // tme).astype(jnp.int32).reshape(1)
    used = counts > 0
    run_of_e = jnp.cumsum(used.astype(jnp.int32)) - 1
    eids = jnp.arange(N_EXPERTS, dtype=jnp.int32)
    run_e = jnp.sum(jnp.where(used[None, :] & (run_of_e[None, :] == eids[:, None]), eids[None, :], 0), axis=1)
    tile_run = jnp.sum(jnp.where(tile_e[:, None] == eids[None, :], run_of_e[None, :], 0), axis=1)
    n_runs = jnp.sum(used.astype(jnp.int32)).reshape(1)
    plan = (tile_run.astype(jnp.int32), run_e.astype(jnp.int32), n_runs, n_used)
    return dest.astype(jnp.int32), plan, n_slots


def _group(x, mod, mod_per_seq, h0, p, row_len, tme, cast_ws):
    n_b, seq_len, _ = x.shape
    n = n_b * seq_len
    xt = x.reshape(n, D)
    mod_seq = mod if mod_per_seq else jnp.broadcast_to(mod, (n_b, 6, D))
    xc = _xr_conv(x, mod_seq, p["g1"], p["w_xr"], p["rnn_conv_w"], p["rnn_conv_b"])
    hf, hb, last = _rglru_scan(xc, h0, p["wg"], p["ba"], p["bx"], p["lam"])
    tiles_per_mod = (seq_len // MIX_TM) if mod_per_seq else (n // MIX_TM)
    outs = _mixer(xt, mod, tiles_per_mod, hf.reshape(n, D), hb.reshape(n, D),
                                     p["g1"], p["g2"], p["w_in"], p["conv_w"], p["wco"], p["wro"],
                                     p["wo"], p["wr"], p["br"], row_len, cast_ws)
    x1, h2, route, wts, cnt = outs[:5]
    casts = outs[5:]
    dest, plan, n_slots = _slot_plan(route, cnt, n, tme)
    xs = _sc_dispatch(h2, dest, n_slots)

    def finish(w1b, w3b, w2b):
        ys = _experts(*plan, xs, w1b, w3b, w2b, tme)
        yg = _sc_collect(ys, dest)
        tiles_per_mod_f = (seq_len // FIN_TM) if mod_per_seq else (n // FIN_TM)
        y = _final(x1, mod, tiles_per_mod_f, yg, wts, p["g_final"])
        return y.reshape(n_b, seq_len, D)

    return finish, last, casts


def kernel(x_prompt, x_sample, state_rnn, c, c_ctx, w_ada, b_ada, g_norm1, g_norm2, w_in, conv_w, w_conv_out, rnn_conv_w, rnn_conv_b, w_gate_a, b_gate_a, w_gate_x, b_gate_x, lam, w_rnn_out, w_o, w_router_group, b_router_group, w_router_expert, b_router_expert, w1, w3, w2, g_final):
    assert w_ada.shape[0] == 1, "single layer"
    n_pb, n_sb = x_prompt.shape[0], x_sample.shape[0]

    cond = jnp.zeros((16, D), F32).at[0].set(c_ctx).at[1:1 + n_sb].set(c)
    mod = _ada(cond, w_ada[0], b_ada[0]).reshape(16, 6, D)

    w_in_b = w_in[0].astype(BF16)
    wr = jnp.zeros((D, ROUTE_PAD), F32)
    wr = wr.at[:, :N_GROUPS].set(w_router_group[0]).at[:, EPG:EPG + N_EXPERTS].set(w_router_expert[0])
    br = jnp.zeros((1, ROUTE_PAD), F32).at[0, N_GROUPS:EPG].set(NEG_BIG)
    br = br.at[0, :N_GROUPS].set(b_router_group[0]).at[0, EPG:EPG + N_EXPERTS].set(b_router_expert[0])
    wr_hi = wr.astype(BF16)
    p = dict(
        g1=g_norm1, g2=g_norm2, w_in=w_in_b, w_xr=w_in_b[:, 3 * D:4 * D],
        conv_w=conv_w[0], rnn_conv_w=rnn_conv_w[0], rnn_conv_b=rnn_conv_b,
        wg=(0.5 * jnp.concatenate([w_gate_a[0], w_gate_x[0]], axis=-1)).astype(BF16),
        ba=b_gate_a[0], bx=b_gate_x[0], lam=lam[0],
        wco=w_conv_out[0].astype(BF16), wro=w_rnn_out[0].astype(BF16), wo=w_o[0].astype(BF16),
        wr=jnp.concatenate([wr_hi, (wr - wr_hi.astype(F32)).astype(BF16)], axis=1), br=br,
        g_final=g_final.reshape(1, D),
    )

    h0_p = jnp.zeros((n_pb // SCAN_B, 2, SCAN_B, D), F32)
    finish_p, last, (w2b,) = _group(x_prompt, mod[0:1], False, h0_p, p, x_prompt.shape[1], 512,
                                    [w2[0]])
    state_new = last.transpose(0, 2, 1, 3).reshape(n_pb, 1, 2, D)

    h0_s = state_rnn[:, 0].reshape(n_sb // SCAN_B, SCAN_B, 2, D).transpose(0, 2, 1, 3)
    finish_s, _, (w1b, w3b) = _group(x_sample, mod[1:1 + n_sb], True, h0_s, p, GRID_W, 512,
                                     [w1[0], w3[0]])
    return (finish_p(w1b, w3b, w2b), finish_s(w1b, w3b, w2b), state_new)
```

```python
import functools

import jax
import jax.numpy as jnp
from jax import lax
from jax.experimental import pallas as pl
from jax.experimental.pallas import tpu as pltpu
from jax.experimental.pallas import tpu_sc as plsc

D = 1024
N_HEADS = 8
HEAD = D // N_HEADS
GRID_W = 64
RG_C = 8.0
N_GROUPS = 4
EPG = 8
N_EXPERTS = N_GROUPS * EPG
D_EXPERT = 512
EPS = 1e-6
F32 = jnp.float32
BF16 = jnp.bfloat16

V7X_LANES = 128
V7X_SUBLANES = 8
V7X_VMEM_LIMIT_BYTES = 56 * 1024 * 1024
V7X_SC_CORES = 2
V7X_SC_SUBCORES = 16
V7X_SC_WORKERS = V7X_SC_CORES * V7X_SC_SUBCORES
SC_WINDOW = 64

XR_T = 256
XR_SUB = 32
XR_LEFT = 2
XR_COL = 3
SCAN_T = 128
SCAN_B = V7X_SUBLANES
SCAN_UNROLL = 8
LOG2_E = 1.4426950408889634
TINY = 1e-30
MIX_TM = 512
FIN_TM = 1024
EXPERT_ROW_BUFS = 3
ROUTE_PAD = 128
NEG_BIG = -1e30


def _sigmoid(x):
    return 0.5 * jnp.tanh(0.5 * x) + 0.5


def _norm_mod(x, g, scale, shift):
    ms = jnp.mean(x * x, axis=-1, keepdims=True)
    return (x * lax.rsqrt(ms + EPS)) * (g * (1.0 + scale)) + shift


def _dot(a, b):
    return jnp.dot(a, b, preferred_element_type=F32)


def _params(sem, vmem=V7X_VMEM_LIMIT_BYTES):
    return pltpu.CompilerParams(dimension_semantics=sem, vmem_limit_bytes=vmem)


def _const_spec(shape):
    zeros = (0,) * len(shape)
    return pl.BlockSpec(shape, lambda *_: zeros, pipeline_mode=pl.Buffered(1))


def _ada_body(c_ref, w_ref, b_ref, o_ref):
    c = c_ref[...]
    s = (c * _sigmoid(c)).astype(BF16)
    o_ref[...] = _dot(s, w_ref[...].astype(BF16)) + b_ref[...]


def _ada(cond, w_ada, b_ada):
    rows = cond.shape[0]
    n_out = w_ada.shape[1]
    return pl.pallas_call(
        _ada_body,
        grid=(n_out // D,),
        in_specs=[pl.BlockSpec((rows, D), lambda i: (0, 0)),
                  pl.BlockSpec((D, D), lambda i: (0, i)),
                  pl.BlockSpec((1, D), lambda i: (0, i))],
        out_specs=pl.BlockSpec((rows, D), lambda i: (0, i)),
        out_shape=jax.ShapeDtypeStruct((rows, n_out), F32),
        compiler_params=_params(("parallel",)),
        name="ada",
    )(cond, w_ada, b_ada.reshape(1, n_out))


def _xr_body(x_ref, xn_ref, mod_ref, g_ref, w_ref, cw_ref, cb_ref, o_ref, xt):
    j = pl.program_id(1)
    n_t = pl.num_programs(1)
    t_len = x_ref.shape[1]
    nb = SCAN_B
    body0 = XR_LEFT * nb

    @pl.when(j == 0)
    def _():
        xt[:, 0:body0, :] = jnp.zeros((N_HEADS, body0, HEAD), F32)

    @pl.when(j > 0)
    def _():
        xt[:, 0:body0, :] = xt[:, t_len * nb:t_len * nb + body0, :]

    g = g_ref[...]
    for b in range(nb):
        h = _norm_mod(x_ref[b], g, mod_ref[b, 1:2, :], mod_ref[b, 0:1, :]).astype(BF16)
        r = _dot(h, w_ref[...])
        for s in range(N_HEADS):
            xt[s, pl.ds(body0 + b, t_len, stride=nb), :] = r[:, s * HEAD:(s + 1) * HEAD]

    hn = _norm_mod(xn_ref[:, 0, :], g, mod_ref[:, 1, :], mod_ref[:, 0, :]).astype(BF16)
    rn = jnp.where(j < n_t - 1, _dot(hn, w_ref[...]), 0.0)
    tail = body0 + t_len * nb
    for s in range(N_HEADS):
        xt[s, tail:tail + nb, :] = rn[:, s * HEAD:(s + 1) * HEAD]

    sub = XR_SUB
    for s in range(N_HEADS):
        sl = slice(s * HEAD, (s + 1) * HEAD)
        for t0 in range(0, t_len, sub):
            y = cb_ref[:, sl]
            for k in range(4):
                r0 = (t0 + k) * nb
                y = y + xt[s, r0:r0 + sub * nb, :] * cw_ref[k:k + 1, sl]
            o_ref[t0:t0 + sub, :, sl] = y.reshape(sub, nb, HEAD)


def _xr_conv(x, mod_seq, g1, w_in, cw, cb):
    n_b, seq_len, _ = x.shape
    t_len = min(XR_T, seq_len)
    assert n_b % SCAN_B == 0 and seq_len % t_len == 0 and t_len % XR_SUB == 0
    n_t = seq_len // t_len
    nxt = t_len // V7X_SUBLANES
    last_blk = seq_len // V7X_SUBLANES - 1
    return pl.pallas_call(
        _xr_body,
        grid=(n_b // SCAN_B, n_t),
        in_specs=[pl.BlockSpec((SCAN_B, t_len, D), lambda g, j: (g, j, 0)),
                  pl.BlockSpec((SCAN_B, V7X_SUBLANES, D),
                               lambda g, j: (g, jnp.minimum((j + 1) * nxt, last_blk), 0)),
                  pl.BlockSpec((SCAN_B, 6, D), lambda g, j: (g, 0, 0)),
                  _const_spec((1, D)),
                  pl.BlockSpec((D, D), lambda g, j: (0, XR_COL), pipeline_mode=pl.Buffered(1)),
                  _const_spec((4, D)),
                  _const_spec((1, D))],
        out_specs=pl.BlockSpec((t_len, SCAN_B, D), lambda g, j: (j, g, 0)),
        out_shape=jax.ShapeDtypeStruct((seq_len, n_b, D), F32),
        scratch_shapes=[pltpu.VMEM((N_HEADS, (t_len + XR_LEFT + 1) * SCAN_B, HEAD), F32)],
        compiler_params=_params(("parallel", "arbitrary")),
        name="xr_conv",
    )(x, x, mod_seq, g1, w_in, cw, cb)


def _scan_body(xf_ref, xb_ref, h0_ref, wg_ref, ba_ref, bx_ref, lam_ref,
               hf_ref, hb_ref, last_ref, a_s, u_s, h_s, hc):
    j = pl.program_id(1)
    n_t = pl.num_programs(1)
    t_len = xf_ref.shape[0]
    rows = SCAN_B * t_len

    @pl.when(j == 0)
    def _():
        for d in range(2):
            for s in range(N_HEADS):
                hc[d, s] = h0_ref[0, d, :, s * HEAD:(s + 1) * HEAD]

    for d, x_ref in ((0, xf_ref), (1, xb_ref)):
        z = -lam_ref[d:d + 1, :]
        sp = jnp.maximum(z, 0.0) + jnp.log(1.0 + jnp.exp(-jnp.abs(z)))
        c2 = (-0.5 * RG_C * LOG2_E) * sp
        for hd in range(N_HEADS):
            sl = slice(hd * HEAD, (hd + 1) * HEAD)
            xh = x_ref[:, :, sl].reshape(rows, HEAD)
            g = _dot(xh.astype(BF16), wg_ref[d, hd])
            t_r = jnp.tanh(g[:, :HEAD] + 0.5 * ba_ref[d:d + 1, sl])
            t_i = jnp.tanh(g[:, HEAD:] + 0.5 * bx_ref[d:d + 1, sl])
            a = jnp.exp2(c2[:, sl] * t_r + c2[:, sl])
            q = 0.25 - 0.25 * (a * a)
            half_mult = q * lax.rsqrt(jnp.maximum(q, TINY))
            u = half_mult * ((t_i + 1.0) * xh)
            a_s[d, hd] = a
            u_s[d, hd] = u

    def step(tb, carry):
        h_f, h_b = carry
        for k in range(SCAN_UNROLL):
            t = tb * SCAN_UNROLL + k
            rf = pl.multiple_of(t * SCAN_B, SCAN_B)
            rb = pl.multiple_of((t_len - 1 - t) * SCAN_B, SCAN_B)
            h_f = a_s[0, :, pl.ds(rf, SCAN_B), :] * h_f + u_s[0, :, pl.ds(rf, SCAN_B), :]
            h_b = a_s[1, :, pl.ds(rb, SCAN_B), :] * h_b + u_s[1, :, pl.ds(rb, SCAN_B), :]
            h_s[0, :, pl.ds(rf, SCAN_B), :] = h_f
            h_s[1, :, pl.ds(rb, SCAN_B), :] = h_b
        return h_f, h_b

    h_f, h_b = lax.fori_loop(0, t_len // SCAN_UNROLL, step, (hc[0], hc[1]))
    hc[0] = h_f
    hc[1] = h_b

    for d, o_ref in ((0, hf_ref), (1, hb_ref)):
        for hd in range(N_HEADS):
            for b in range(SCAN_B):
                o_ref[b, :, hd * HEAD:(hd + 1) * HEAD] = (
                    h_s[d, hd, pl.ds(b, t_len, stride=SCAN_B), :].astype(BF16))

    @pl.when(j == n_t - 1)
    def _():
        for d in range(2):
            for s in range(N_HEADS):
                last_ref[0, d, :, s * HEAD:(s + 1) * HEAD] = hc[d, s]


def _rglru_scan(xc, h0, wg, ba, bx, lam):
    seq_len, n_b, _ = xc.shape
    n_g = n_b // SCAN_B
    n_t = seq_len // SCAN_T
    blk = (SCAN_B, SCAN_T, D)
    blk_in = (SCAN_T, SCAN_B, D)
    return pl.pallas_call(
        _scan_body,
        grid=(n_g, n_t),
        in_specs=[pl.BlockSpec(blk_in, lambda g, j: (j, g, 0)),
                  pl.BlockSpec(blk_in, lambda g, j: (n_t - 1 - j, g, 0)),
                  pl.BlockSpec((1, 2, SCAN_B, D), lambda g, j: (g, 0, 0, 0)),
                  _const_spec((2, N_HEADS, HEAD, 2 * HEAD)),
                  _const_spec((2, D)),
                  _const_spec((2, D)),
                  _const_spec((2, D))],
        out_specs=[pl.BlockSpec(blk, lambda g, j: (g, j, 0)),
                   pl.BlockSpec(blk, lambda g, j: (g, n_t - 1 - j, 0)),
                   pl.BlockSpec((1, 2, SCAN_B, D), lambda g, j: (g, 0, 0, 0))],
        out_shape=[jax.ShapeDtypeStruct((n_b, seq_len, D), BF16),
                   jax.ShapeDtypeStruct((n_b, seq_len, D), BF16),
                   jax.ShapeDtypeStruct((n_g, 2, SCAN_B, D), F32)],
        scratch_shapes=[pltpu.VMEM((2, N_HEADS, SCAN_T * SCAN_B, HEAD), F32),
                        pltpu.VMEM((2, N_HEADS, SCAN_T * SCAN_B, HEAD), F32),
                        pltpu.VMEM((2, N_HEADS, SCAN_T * SCAN_B, HEAD), F32),
                        pltpu.VMEM((2, N_HEADS, SCAN_B, HEAD), F32)],
        compiler_params=_params(("parallel", "arbitrary")),
        name="rglru_scan",
    )(xc, xc, h0, wg, ba, bx, lam)


def _pack_halves(v):
    half = v.shape[1] // 2
    lo = lax.bitcast_convert_type(v[:, :half].astype(BF16).astype(F32), jnp.uint32)
    hi = lax.bitcast_convert_type(v[:, half:].astype(BF16).astype(F32), jnp.uint32)
    return lax.bitcast_convert_type((lo >> 16) | (hi & jnp.uint32(0xFFFF0000)), jnp.int32)


def _unpack_halves(p):
    u = lax.bitcast_convert_type(p, jnp.uint32)
    lo = lax.bitcast_convert_type(u << 16, F32)
    hi = lax.bitcast_convert_type(u & jnp.uint32(0xFFFF0000), F32)
    return lo, hi


def _mix_body(x_ref, mod_ref, hf_ref, hb_ref, g1_ref, g2_ref, win_ref, cw_ref,
              wco_ref, wro_ref, wo_ref, wr_ref, br_ref, *rest, row_len, n_cast):
    cast_in = rest[:n_cast]
    x1_ref, h2_ref, route_ref, wts_ref, cnt_ref = rest[n_cast:n_cast + 5]
    cast_out = rest[n_cast + 5:2 * n_cast + 5]
    seen = rest[2 * n_cast + 5]
    x = x_ref[...]
    tm = x.shape[0]
    h = _norm_mod(x, g1_ref[...], mod_ref[0, 1:2, :], mod_ref[0, 0:1, :]).astype(BF16)

    def proj(k):
        return _dot(h, win_ref[:, k * D:(k + 1) * D])

    cv = proj(1) * proj(2)
    pos = lax.broadcasted_iota(jnp.int32, (tm, 1), 0) % row_len
    conv = cv * cw_ref[1:2, :]
    conv = conv + jnp.where(pos >= 1, pltpu.roll(cv, 1, 0), 0.0) * cw_ref[0:1, :]
    conv = conv + jnp.where(pos <= row_len - 2, pltpu.roll(cv, tm - 1, 0), 0.0) * cw_ref[2:3, :]
    y_a = _dot((proj(0) * conv).astype(BF16), wco_ref[...])
    merged = _sigmoid(proj(5)) * y_a

    hs = hf_ref[...].astype(F32) + hb_ref[...].astype(F32)
    y_b = _dot((hs * jax.nn.gelu(proj(4))).astype(BF16), wro_ref[...])
    merged = merged + _sigmoid(proj(6)) * y_b

    mix = _dot(merged.astype(BF16), wo_ref[...])
    x1 = x + mod_ref[0, 2:3, :] * mix
    x1_ref[...] = x1.astype(BF16)
    for src, dst in zip(cast_in, cast_out):
        dst[...] = src[...].astype(BF16)
    h2 = _norm_mod(x1, g2_ref[...], mod_ref[0, 4:5, :], mod_ref[0, 3:4, :])
    h2_ref[...] = _pack_halves(h2)

    h2_hi = h2.astype(BF16)
    h2_lo = (h2 - h2_hi.astype(F32)).astype(BF16)
    big = _dot(h2_hi, wr_ref[...])
    logits = (big[:, :ROUTE_PAD] + big[:, ROUTE_PAD:]
              + _dot(h2_lo, wr_ref[:, :ROUTE_PAD]) + br_ref[...])
    lt = logits.T
    row = lax.broadcasted_iota(jnp.int32, (EPG, tm), 0)
    lg = lt[0:EPG]
    mg = jnp.max(lg, axis=0, keepdims=True)
    p_grp = 1.0 / jnp.sum(jnp.exp(lg - mg), axis=0, keepdims=True)
    grp = jnp.min(jnp.where(lg == mg, row, EPG), axis=0, keepdims=True)
    le = lt[EPG * N_GROUPS:EPG * (N_GROUPS + 1)]
    for g in range(N_GROUPS - 2, -1, -1):
        le = jnp.where(grp == g, lt[EPG * (g + 1):EPG * (g + 2)], le)
    me = jnp.max(le, axis=0, keepdims=True)
    ee = jnp.exp(le - me)
    pe = ee / jnp.sum(ee, axis=0, keepdims=True)
    p1 = jnp.max(pe, axis=0, keepdims=True)
    i1 = jnp.min(jnp.where(pe == p1, row, EPG), axis=0, keepdims=True)
    pe2 = jnp.where(row == i1, -1.0, pe)
    p2 = jnp.max(pe2, axis=0, keepdims=True)
    i2 = jnp.min(jnp.where(pe2 == p2, row, EPG), axis=0, keepdims=True)
    den = p1 + p2
    e1 = grp * EPG + i1
    e2 = grp * EPG + i2

    @pl.when(pl.program_id(0) == 0)
    def _():
        seen[...] = jnp.zeros_like(seen)

    erow = lax.broadcasted_iota(jnp.int32, (N_EXPERTS, tm), 0)
    hit1 = erow == e1
    hit2 = erow == e2
    both = jnp.where(jnp.logical_or(hit1, hit2), 1.0, 0.0)
    tri = jnp.where(lax.broadcasted_iota(jnp.int32, (tm, tm), 0)
                    <= lax.broadcasted_iota(jnp.int32, (tm, tm), 1), 1.0, 0.0).astype(BF16)
    before = _dot(both.astype(BF16), tri) - both + seen[...]
    r1 = jnp.sum(jnp.where(hit1, before, 0.0), axis=0, keepdims=True).astype(jnp.int32)
    r2 = jnp.sum(jnp.where(hit2, before, 0.0), axis=0, keepdims=True).astype(jnp.int32)
    total = seen[...] + jnp.sum(both, axis=1, keepdims=True)
    seen[...] = total
    cnt_ref[...] = total[:, :V7X_LANES].astype(jnp.int32)

    route_ref[...] = jnp.where(row == 0, e1, jnp.where(row == 1, e2, jnp.where(
        row == 2, r1, jnp.where(row == 3, r2, 0))))
    w8 = jnp.where(row == 0, p_grp * p1 / den, jnp.where(row == 1, p_grp * p2 / den, 0.0))
    wts_ref[...] = jnp.concatenate([w8, jnp.zeros((V7X_LANES - EPG, tm), F32)], axis=0).T


def _mixer(x, mod, tiles_per_mod, hf, hb, g1, g2, w_in, cw, wco, wro, wo, wr, br, row_len, cast_ws):
    n = x.shape[0]
    tm = MIX_TM
    assert tm % row_len == 0 and n % tm == 0
    steps = n // tm
    assert N_EXPERTS % steps == 0
    epb = N_EXPERTS // steps
    cast_specs = [pl.BlockSpec((epb,) + w.shape[1:], lambda i: (i, 0, 0)) for w in cast_ws]
    cast_shapes = [jax.ShapeDtypeStruct(w.shape, BF16) for w in cast_ws]
    mod_map = lambda i: (i // tiles_per_mod, 0, 0)
    tok = lambda i: (i, 0)
    col = lambda i: (0, i)
    return pl.pallas_call(
        functools.partial(_mix_body, row_len=row_len, n_cast=len(cast_ws)),
        grid=(steps,),
        in_specs=[pl.BlockSpec((tm, D), tok),
                  pl.BlockSpec((1, 6, D), mod_map),
                  pl.BlockSpec((tm, D), tok),
                  pl.BlockSpec((tm, D), tok),
                  _const_spec((1, D)),
                  _const_spec((1, D)),
                  _const_spec(w_in.shape),
                  _const_spec((3, D)),
                  _const_spec((D, D)),
                  _const_spec((D, D)),
                  _const_spec((D, D)),
                  _const_spec((D, 2 * ROUTE_PAD)),
                  _const_spec((1, ROUTE_PAD))] + cast_specs,
        out_specs=[pl.BlockSpec((tm, D), tok),
                   pl.BlockSpec((tm, D // 2), tok),
                   pl.BlockSpec((EPG, tm), col),
                   pl.BlockSpec((tm, V7X_LANES), tok),
                   pl.BlockSpec((N_EXPERTS, V7X_LANES), lambda i: (0, 0))] + cast_specs,
        out_shape=[jax.ShapeDtypeStruct((n, D), BF16),
                   jax.ShapeDtypeStruct((n, D // 2), jnp.int32),
                   jax.ShapeDtypeStruct((EPG, n), jnp.int32),
                   jax.ShapeDtypeStruct((n, V7X_LANES), F32),
                   jax.ShapeDtypeStruct((N_EXPERTS, V7X_LANES), jnp.int32)] + cast_shapes,
        scratch_shapes=[pltpu.VMEM((N_EXPERTS, tm), F32)],
        compiler_params=_params(("arbitrary",)),
        name="mixer",
    )(x, mod, hf, hb, g1, g2, w_in, cw, wco, wro, wo, wr, br, *cast_ws)


def _sc_mesh():
    return plsc.VectorSubcoreMesh(core_axis_name="c", subcore_axis_name="s",
                                  num_cores=V7X_SC_CORES, num_subcores=V7X_SC_SUBCORES)


def _sc_worker_id():
    return lax.axis_index("s") * V7X_SC_CORES + lax.axis_index("c")


def _sc_dispatch(rows, dest, n_slots):
    n, width = rows.shape
    per_w = n // V7X_SC_WORKERS
    n_ch = per_w // SC_WINDOW
    assert n_ch * SC_WINDOW * V7X_SC_WORKERS == n
    idx = dest.reshape(2, V7X_SC_WORKERS, n_ch, SC_WINDOW).transpose(1, 0, 2, 3)

    def body(x_hbm, d_hbm, o_hbm, idx_v, buf, ld_sem, st_sem):
        wid = _sc_worker_id()
        pltpu.sync_copy(d_hbm.at[wid], idx_v)

        def load(j):
            src = x_hbm.at[pl.ds(wid * per_w + j * SC_WINDOW, SC_WINDOW)]
            return pltpu.async_copy(src, buf.at[j % 2], ld_sem.at[j % 2])

        def scatter(j):
            return [pltpu.async_copy(buf.at[j % 2], o_hbm.at[idx_v.at[k, j]], st_sem.at[j % 2])
                    for k in range(2)]

        loads = {0: load(0)}
        stores = {}
        for j in range(n_ch):
            loads[j].wait()
            if j >= 1:
                for cp in stores[j - 1]:
                    cp.wait()
            if j + 1 < n_ch:
                loads[j + 1] = load(j + 1)
            stores[j] = scatter(j)
        for cp in stores[n_ch - 1]:
            cp.wait()

    return pl.kernel(
        body,
        out_type=jax.ShapeDtypeStruct((n_slots, width), jnp.int32),
        mesh=_sc_mesh(),
        scratch_types=[pltpu.VMEM((2, n_ch, SC_WINDOW), jnp.int32),
                       pltpu.VMEM((2, SC_WINDOW, width), jnp.int32),
                       pltpu.SemaphoreType.DMA((2,)),
                       pltpu.SemaphoreType.DMA((2,))],
        name="sc_dispatch",
    )(rows, idx)


def _sc_collect(rows, dest):
    n = dest.shape[1]
    width = rows.shape[1]
    per_w = n // V7X_SC_WORKERS
    n_ch = per_w // SC_WINDOW
    assert n_ch * SC_WINDOW * V7X_SC_WORKERS == n
    idx = dest.reshape(2, V7X_SC_WORKERS, n_ch, SC_WINDOW).transpose(1, 0, 2, 3)
    windows = [(k, j) for k in range(2) for j in range(n_ch)]

    def body(y_hbm, d_hbm, o_hbm, idx_v, buf, ld_sem, st_sem):
        wid = _sc_worker_id()
        pltpu.sync_copy(d_hbm.at[wid], idx_v)

        def gather(c):
            k, j = windows[c]
            return pltpu.async_copy(y_hbm.at[idx_v.at[k, j]], buf.at[c % 2], ld_sem.at[c % 2])

        def store(c):
            k, j = windows[c]
            dst = o_hbm.at[pl.ds(k * n + wid * per_w + j * SC_WINDOW, SC_WINDOW)]
            return pltpu.async_copy(buf.at[c % 2], dst, st_sem.at[c % 2])

        loads = {0: gather(0)}
        stores = {}
        for c in range(len(windows)):
            loads[c].wait()
            if c >= 1:
                stores[c - 1].wait()
            if c + 1 < len(windows):
                loads[c + 1] = gather(c + 1)
            stores[c] = store(c)
        stores[len(windows) - 1].wait()

    return pl.kernel(
        body,
        out_type=jax.ShapeDtypeStruct((2 * n, width), jnp.int32),
        mesh=_sc_mesh(),
        scratch_types=[pltpu.VMEM((2, n_ch, SC_WINDOW), jnp.int32),
                       pltpu.VMEM((2, SC_WINDOW, width), jnp.int32),
                       pltpu.SemaphoreType.DMA((2,)),
                       pltpu.SemaphoreType.DMA((2,))],
        name="sc_collect",
    )(rows, idx)


def _expert_body(tr_ref, re_ref, nr_ref, nu_ref, xs_hbm, w1_hbm, w3_hbm, w2_hbm, o_ref,
                 xbuf, w1s, w3s, w2s, xsem, sem):
    i = pl.program_id(0)
    n_used = nu_ref[0]
    tme = xbuf.shape[1]
    run = tr_ref[i]
    first = jnp.logical_or(i == 0, run != tr_ref[jnp.maximum(i - 1, 0)])
    slot = run % 2

    def row_copy(t):
        s = t % EXPERT_ROW_BUFS
        start = t * tme if isinstance(t, int) else pl.multiple_of(t * tme, tme)
        src = xs_hbm.at[pl.ds(start, tme)]
        return pltpu.make_async_copy(src, xbuf.at[s], xsem.at[s])

    @pl.when(i == 0)
    def _():
        for t in range(EXPERT_ROW_BUFS - 1):
            @pl.when(t < n_used)
            def _():
                row_copy(t).start()

    def weight_copies(r, dst_slot):
        e = re_ref[r]
        return (pltpu.make_async_copy(w1_hbm.at[e], w1s.at[dst_slot], sem.at[0, dst_slot]),
                pltpu.make_async_copy(w3_hbm.at[e], w3s.at[dst_slot], sem.at[1, dst_slot]),
                pltpu.make_async_copy(w2_hbm.at[e], w2s.at[dst_slot], sem.at[2, dst_slot]))

    @pl.when(i == 0)
    def _():
        for cp in weight_copies(0, 0):
            cp.start()

    @pl.when(first)
    def _():
        for cp in weight_copies(run, slot):
            cp.wait()

        @pl.when(run + 1 < nr_ref[0])
        def _():
            for cp in weight_copies(run + 1, 1 - slot):
                cp.start()

    @pl.when(i < n_used)
    def _():
        @pl.when(i + (EXPERT_ROW_BUFS - 1) < n_used)
        def _():
            row_copy(i + (EXPERT_ROW_BUFS - 1)).start()

        row_copy(i).wait()
        lo, hi = _unpack_halves(xbuf[i % EXPERT_ROW_BUFS])
        lo = lo.astype(BF16)
        hi = hi.astype(BF16)
        half = D // 2
        a = _dot(lo, w1s[slot, 0:half, :]) + _dot(hi, w1s[slot, half:D, :])
        b = _dot(lo, w3s[slot, 0:half, :]) + _dot(hi, w3s[slot, half:D, :])
        z = (a * _sigmoid(a)) * b
        o_ref[...] = _pack_halves(_dot(z.astype(BF16), w2s[slot]))


def _experts(tile_run, run_e, n_runs, n_used, xs, w1, w3, w2, tme):
    n_slots = xs.shape[0]
    grid_spec = pltpu.PrefetchScalarGridSpec(
        num_scalar_prefetch=4,
        grid=(n_slots // tme,),
        in_specs=[pl.BlockSpec(memory_space=pl.ANY),
                  pl.BlockSpec(memory_space=pl.ANY),
                  pl.BlockSpec(memory_space=pl.ANY),
                  pl.BlockSpec(memory_space=pl.ANY)],
        out_specs=pl.BlockSpec((tme, D // 2),
                               lambda i, tr, re, nr, nu: (jnp.minimum(i, nu[0] - 1), 0)),
        scratch_shapes=[pltpu.VMEM((EXPERT_ROW_BUFS, tme, D // 2), jnp.int32),
                        pltpu.VMEM((2, D, D_EXPERT), BF16),
                        pltpu.VMEM((2, D, D_EXPERT), BF16),
                        pltpu.VMEM((2, D_EXPERT, D), BF16),
                        pltpu.SemaphoreType.DMA((EXPERT_ROW_BUFS,)),
                        pltpu.SemaphoreType.DMA((3, 2))],
    )
    return pl.pallas_call(
        _expert_body,
        grid_spec=grid_spec,
        out_shape=jax.ShapeDtypeStruct((n_slots, D // 2), jnp.int32),
        compiler_params=_params(("arbitrary",)),
        name="experts",
    )(tile_run, run_e, n_runs, n_used, xs, w1, w3, w2)


def _final_body(x1_ref, mod_ref, y0_ref, y1_ref, wt_ref, gf_ref, o_ref):
    w0 = wt_ref[:, 0:1]
    w1 = wt_ref[:, 1:2]
    lo0, hi0 = _unpack_halves(y0_ref[...])
    lo1, hi1 = _unpack_halves(y1_ref[...])
    moe = jnp.concatenate([w0 * lo0 + w1 * lo1, w0 * hi0 + w1 * hi1], axis=1)
    x2 = x1_ref[...].astype(F32) + mod_ref[0, 5:6, :] * moe
    ms = jnp.mean(x2 * x2, axis=-1, keepdims=True)
    o_ref[...] = x2 * lax.rsqrt(ms + EPS) * gf_ref[...]


def _final(x1, mod, tiles_per_mod, yg, wts, g_final):
    n = x1.shape[0]
    tm = FIN_TM
    nt = n // tm
    return pl.pallas_call(
        _final_body,
        grid=(nt,),
        in_specs=[pl.BlockSpec((tm, D), lambda i: (i, 0)),
                  pl.BlockSpec((1, 6, D), lambda i: (i // tiles_per_mod, 0, 0)),
                  pl.BlockSpec((tm, D // 2), lambda i: (i, 0)),
                  pl.BlockSpec((tm, D // 2), lambda i: (i + nt, 0)),
                  pl.BlockSpec((tm, V7X_LANES), lambda i: (i, 0)),
                  _const_spec((1, D))],
        out_specs=pl.BlockSpec((tm, D), lambda i: (i, 0)),
        out_shape=jax.ShapeDtypeStruct((n, D), F32),
        compiler_params=_params(("parallel",)),
        name="final",
    )(x1, mod, yg, yg, wts, g_final)


def _slot_plan(route, cnt, n, tme):
    counts = cnt[:, 0]
    padded = ((counts + tme - 1) // tme) * tme
    pend = jnp.cumsum(padded)
    pstart = pend - padded
    onehot = route[0:2, :, None] == jnp.arange(N_EXPERTS, dtype=jnp.int32)[None, None, :]
    dest = jnp.sum(jnp.where(onehot, pstart[None, None, :], 0), axis=-1) + route[2:4]
    n_slots = ((2 * n + N_EXPERTS * (tme - 1)) // tme) * tme
    tile_start = jnp.arange(n_slots // tme, dtype=jnp.int32) * tme
    tile_e = jnp.sum((tile_start[:, None] >= pend[None, :]).astype(jnp.int32), axis=1)
    tile_e = jnp.minimum(tile_e, N_EXPERTS - 1)
    n_used = (pend[-1] // tme).astype(jnp.int32).reshape(1)
    used = counts > 0
    run_of_e = jnp.cumsum(used.astype(jnp.int32)) - 1
    eids = jnp.arange(N_EXPERTS, dtype=jnp.int32)
    run_e = jnp.sum(jnp.where(used[None, :] & (run_of_e[None, :] == eids[:, None]), eids[None, :], 0), axis=1)
    tile_run = jnp.sum(jnp.where(tile_e[:, None] == eids[None, :], run_of_e[None, :], 0), axis=1)
    n_runs = jnp.sum(used.astype(jnp.int32)).reshape(1)
    plan = (tile_run.astype(jnp.int32), run_e.astype(jnp.int32), n_runs, n_used)
    return dest.astype(jnp.int32), plan, n_slots


def _group(x, mod, mod_per_seq, h0, p, row_len, tme, cast_ws):
    n_b, seq_len, _ = x.shape
    n = n_b * seq_len
    xt = x.reshape(n, D)
    mod_seq = mod if mod_per_seq else jnp.broadcast_to(mod, (n_b, 6, D))
    xc = _xr_conv(x, mod_seq, p["g1"], p["w_in"], p["rnn_conv_w"], p["rnn_conv_b"])
    hf, hb, last = _rglru_scan(xc, h0, p["wg"], p["ba"], p["bx"], p["lam"])
    tiles_per_mod = (seq_len // MIX_TM) if mod_per_seq else (n // MIX_TM)
    outs = _mixer(xt, mod, tiles_per_mod, hf.reshape(n, D), hb.reshape(n, D),
                                     p["g1"], p["g2"], p["w_in"], p["conv_w"], p["wco"], p["wro"],
                                     p["wo"], p["wr"], p["br"], row_len, cast_ws)
    x1, h2, route, wts, cnt = outs[:5]
    casts = outs[5:]
    dest, plan, n_slots = _slot_plan(route, cnt, n, tme)
    xs = _sc_dispatch(h2, dest, n_slots)

    def finish(w1b, w3b, w2b):
        ys = _experts(*plan, xs, w1b, w3b, w2b, tme)
        yg = _sc_collect(ys, dest)
        tiles_per_mod_f = (seq_len // FIN_TM) if mod_per_seq else (n // FIN_TM)
        y = _final(x1, mod, tiles_per_mod_f, yg, wts, p["g_final"])
        return y.reshape(n_b, seq_len, D)

    return finish, last, casts


def kernel(x_prompt, x_sample, state_rnn, c, c_ctx, w_ada, b_ada, g_norm1, g_norm2, w_in, conv_w, w_conv_out, rnn_conv_w, rnn_conv_b, w_gate_a, b_gate_a, w_gate_x, b_gate_x, lam, w_rnn_out, w_o, w_router_group, b_router_group, w_router_expert, b_router_expert, w1, w3, w2, g_final):
    assert w_ada.shape[0] == 1, "single layer"
    n_pb, n_sb = x_prompt.shape[0], x_sample.shape[0]

    cond = jnp.concatenate([c_ctx[None, :], c, jnp.zeros((16 - 1 - n_sb, D), F32)], axis=0)
    mod = _ada(cond, w_ada[0], b_ada[0]).reshape(16, 6, D)

    w_in_b = w_in[0].astype(BF16)
    pad_w = jnp.zeros((D, EPG - N_GROUPS), F32)
    wr = jnp.concatenate([w_router_group[0], pad_w, w_router_expert[0],
                          jnp.zeros((D, ROUTE_PAD - EPG - N_EXPERTS), F32)], axis=1)
    br = jnp.concatenate([b_router_group[0], jnp.full((EPG - N_GROUPS,), NEG_BIG, F32),
                          b_router_expert[0],
                          jnp.zeros((ROUTE_PAD - EPG - N_EXPERTS,), F32)]).reshape(1, ROUTE_PAD)
    wr_hi = wr.astype(BF16)
    p = dict(
        g1=g_norm1, g2=g_norm2, w_in=w_in_b,
        conv_w=conv_w[0], rnn_conv_w=rnn_conv_w[0], rnn_conv_b=rnn_conv_b,
        wg=(0.5 * jnp.concatenate([w_gate_a[0], w_gate_x[0]], axis=-1)).astype(BF16),
        ba=b_gate_a[0], bx=b_gate_x[0], lam=lam[0],
        wco=w_conv_out[0].astype(BF16), wro=w_rnn_out[0].astype(BF16), wo=w_o[0].astype(BF16),
        wr=jnp.concatenate([wr_hi, (wr - wr_hi.astype(F32)).astype(BF16)], axis=1), br=br,
        g_final=g_final.reshape(1, D),
    )

    h0_p = jnp.zeros((n_pb // SCAN_B, 2, SCAN_B, D), F32)
    finish_p, last, (w2b,) = _group(x_prompt, mod[0:1], False, h0_p, p, x_prompt.shape[1], 512,
                                    [w2[0]])
    state_new = last.transpose(0, 2, 1, 3).reshape(n_pb, 1, 2, D)

    h0_s = state_rnn[:, 0].reshape(n_sb // SCAN_B, SCAN_B, 2, D).transpose(0, 2, 1, 3)
    finish_s, _, (w1b, w3b) = _group(x_sample, mod[1:1 + n_sb], True, h0_s, p, GRID_W, 512,
                                     [w1[0], w3[0]])
    return (finish_p(w1b, w3b, w2b), finish_s(w1b, w3b, w2b), state_new)
```

```python
import functools

import jax
import jax.numpy as jnp
from jax import lax
from jax.experimental import pallas as pl
from jax.experimental.pallas import tpu as pltpu
from jax.experimental.pallas import tpu_sc as plsc

D = 1024
N_HEADS = 8
HEAD = D // N_HEADS
GRID_W = 64
RG_C = 8.0
N_GROUPS = 4
EPG = 8
N_EXPERTS = N_GROUPS * EPG
D_EXPERT = 512
EPS = 1e-6
F32 = jnp.float32
BF16 = jnp.bfloat16

V7X_LANES = 128
V7X_SUBLANES = 8
V7X_VMEM_LIMIT_BYTES = 56 * 1024 * 1024
V7X_SC_CORES = 2
V7X_SC_SUBCORES = 16
V7X_SC_WORKERS = V7X_SC_CORES * V7X_SC_SUBCORES
SC_WINDOW = 64

XR_T = 256
XR_SUB = 32
XR_LEFT = 2
XR_COL = 3
SCAN_T = 128
SCAN_B = V7X_SUBLANES
SCAN_UNROLL = 8
LOG2_E = 1.4426950408889634
TINY = 1e-30
MIX_TM = 512
FIN_TM = 1024
EXPERT_ROW_BUFS = 3
EXPERT_W_BUFS = 3
ROUTE_PAD = 128
NEG_BIG = -1e30


def _sigmoid(x):
    return 0.5 * jnp.tanh(0.5 * x) + 0.5


def _norm_mod(x, g, scale, shift):
    ms = jnp.mean(x * x, axis=-1, keepdims=True)
    return (x * lax.rsqrt(ms + EPS)) * (g * (1.0 + scale)) + shift


def _dot(a, b):
    return jnp.dot(a, b, preferred_element_type=F32)


def _params(sem, vmem=V7X_VMEM_LIMIT_BYTES):
    return pltpu.CompilerParams(dimension_semantics=sem, vmem_limit_bytes=vmem)


def _const_spec(shape):
    zeros = (0,) * len(shape)
    return pl.BlockSpec(shape, lambda *_: zeros, pipeline_mode=pl.Buffered(1))


def _ada_body(c_ref, w_ref, b_ref, o_ref):
    c = c_ref[...]
    s = (c * _sigmoid(c)).astype(BF16)
    o_ref[...] = _dot(s, w_ref[...].astype(BF16)) + b_ref[...]


def _ada(cond, w_ada, b_ada):
    rows = cond.shape[0]
    n_out = w_ada.shape[1]
    return pl.pallas_call(
        _ada_body,
        grid=(n_out // D,),
        in_specs=[pl.BlockSpec((rows, D), lambda i: (0, 0)),
                  pl.BlockSpec((D, D), lambda i: (0, i)),
                  pl.BlockSpec((1, D), lambda i: (0, i))],
        out_specs=pl.BlockSpec((rows, D), lambda i: (0, i)),
        out_shape=jax.ShapeDtypeStruct((rows, n_out), F32),
        compiler_params=_params(("parallel",)),
        name="ada",
    )(cond, w_ada, b_ada.reshape(1, n_out))


def _xr_body(x_ref, xn_ref, mod_ref, g_ref, w_ref, cw_ref, cb_ref, o_ref, xt):
    j = pl.program_id(1)
    n_t = pl.num_programs(1)
    t_len = x_ref.shape[1]
    nb = SCAN_B
    body0 = XR_LEFT * nb

    @pl.when(j == 0)
    def _():
        xt[:, 0:body0, :] = jnp.zeros((N_HEADS, body0, HEAD), F32)

    @pl.when(j > 0)
    def _():
        xt[:, 0:body0, :] = xt[:, t_len * nb:t_len * nb + body0, :]

    g = g_ref[...]
    for b in range(nb):
        h = _norm_mod(x_ref[b], g, mod_ref[b, 1:2, :], mod_ref[b, 0:1, :]).astype(BF16)
        r = _dot(h, w_ref[...])
        for s in range(N_HEADS):
            xt[s, pl.ds(body0 + b, t_len, stride=nb), :] = r[:, s * HEAD:(s + 1) * HEAD]

    hn = _norm_mod(xn_ref[:, 0, :], g, mod_ref[:, 1, :], mod_ref[:, 0, :]).astype(BF16)
    rn = jnp.where(j < n_t - 1, _dot(hn, w_ref[...]), 0.0)
    tail = body0 + t_len * nb
    for s in range(N_HEADS):
        xt[s, tail:tail + nb, :] = rn[:, s * HEAD:(s + 1) * HEAD]

    sub = XR_SUB
    for s in range(N_HEADS):
        sl = slice(s * HEAD, (s + 1) * HEAD)
        for t0 in range(0, t_len, sub):
            y = cb_ref[:, sl]
            for k in range(4):
                r0 = (t0 + k) * nb
                y = y + xt[s, r0:r0 + sub * nb, :] * cw_ref[k:k + 1, sl]
            o_ref[t0:t0 + sub, :, sl] = y.reshape(sub, nb, HEAD)


def _xr_conv(x, mod_seq, g1, w_in, cw, cb):
    n_b, seq_len, _ = x.shape
    t_len = min(XR_T, seq_len)
    assert n_b % SCAN_B == 0 and seq_len % t_len == 0 and t_len % XR_SUB == 0
    n_t = seq_len // t_len
    nxt = t_len // V7X_SUBLANES
    last_blk = seq_len // V7X_SUBLANES - 1
    return pl.pallas_call(
        _xr_body,
        grid=(n_b // SCAN_B, n_t),
        in_specs=[pl.BlockSpec((SCAN_B, t_len, D), lambda g, j: (g, j, 0)),
                  pl.BlockSpec((SCAN_B, V7X_SUBLANES, D),
                               lambda g, j: (g, jnp.minimum((j + 1) * nxt, last_blk), 0)),
                  pl.BlockSpec((SCAN_B, 6, D), lambda g, j: (g, 0, 0)),
                  _const_spec((1, D)),
                  pl.BlockSpec((D, D), lambda g, j: (0, XR_COL), pipeline_mode=pl.Buffered(1)),
                  _const_spec((4, D)),
                  _const_spec((1, D))],
        out_specs=pl.BlockSpec((t_len, SCAN_B, D), lambda g, j: (j, g, 0)),
        out_shape=jax.ShapeDtypeStruct((seq_len, n_b, D), F32),
        scratch_shapes=[pltpu.VMEM((N_HEADS, (t_len + XR_LEFT + 1) * SCAN_B, HEAD), F32)],
        compiler_params=_params(("parallel", "arbitrary")),
        name="xr_conv",
    )(x, x, mod_seq, g1, w_in, cw, cb)


def _scan_body(xf_ref, xb_ref, h0_ref, wg_ref, ba_ref, bx_ref, lam_ref,
               hf_ref, hb_ref, last_ref, a_s, u_s, h_s, hc):
    j = pl.program_id(1)
    n_t = pl.num_programs(1)
    t_len = xf_ref.shape[0]
    rows = SCAN_B * t_len

    @pl.when(j == 0)
    def _():
        for d in range(2):
            for s in range(N_HEADS):
                hc[d, s] = h0_ref[0, d, :, s * HEAD:(s + 1) * HEAD]

    for d, x_ref in ((0, xf_ref), (1, xb_ref)):
        z = -lam_ref[d:d + 1, :]
        sp = jnp.maximum(z, 0.0) + jnp.log(1.0 + jnp.exp(-jnp.abs(z)))
        c2 = (-0.5 * RG_C * LOG2_E) * sp
        for hd in range(N_HEADS):
            sl = slice(hd * HEAD, (hd + 1) * HEAD)
            xh = x_ref[:, :, sl].reshape(rows, HEAD)
            g = _dot(xh.astype(BF16), wg_ref[d, hd])
            t_r = jnp.tanh(g[:, :HEAD] + 0.5 * ba_ref[d:d + 1, sl])
            t_i = jnp.tanh(g[:, HEAD:] + 0.5 * bx_ref[d:d + 1, sl])
            a = jnp.exp2(c2[:, sl] * t_r + c2[:, sl])
            q = 0.25 - 0.25 * (a * a)
            half_mult = q * lax.rsqrt(jnp.maximum(q, TINY))
            u = half_mult * ((t_i + 1.0) * xh)
            a_s[d, hd] = a
            u_s[d, hd] = u

    def step(tb, carry):
        h_f, h_b = carry
        for k in range(SCAN_UNROLL):
            t = tb * SCAN_UNROLL + k
            rf = pl.multiple_of(t * SCAN_B, SCAN_B)
            rb = pl.multiple_of((t_len - 1 - t) * SCAN_B, SCAN_B)
            h_f = a_s[0, :, pl.ds(rf, SCAN_B), :] * h_f + u_s[0, :, pl.ds(rf, SCAN_B), :]
            h_b = a_s[1, :, pl.ds(rb, SCAN_B), :] * h_b + u_s[1, :, pl.ds(rb, SCAN_B), :]
            h_s[0, :, pl.ds(rf, SCAN_B), :] = h_f
            h_s[1, :, pl.ds(rb, SCAN_B), :] = h_b
        return h_f, h_b

    h_f, h_b = lax.fori_loop(0, t_len // SCAN_UNROLL, step, (hc[0], hc[1]))
    hc[0] = h_f
    hc[1] = h_b

    for d, o_ref in ((0, hf_ref), (1, hb_ref)):
        for hd in range(N_HEADS):
            for b in range(SCAN_B):
                o_ref[b, :, hd * HEAD:(hd + 1) * HEAD] = (
                    h_s[d, hd, pl.ds(b, t_len, stride=SCAN_B), :].astype(BF16))

    @pl.when(j == n_t - 1)
    def _():
        for d in range(2):
            for s in range(N_HEADS):
                last_ref[0, d, :, s * HEAD:(s + 1) * HEAD] = hc[d, s]


def _rglru_scan(xc, h0, wg, ba, bx, lam):
    seq_len, n_b, _ = xc.shape
    n_g = n_b // SCAN_B
    n_t = seq_len // SCAN_T
    blk = (SCAN_B, SCAN_T, D)
    blk_in = (SCAN_T, SCAN_B, D)
    return pl.pallas_call(
        _scan_body,
        grid=(n_g, n_t),
        in_specs=[pl.BlockSpec(blk_in, lambda g, j: (j, g, 0)),
                  pl.BlockSpec(blk_in, lambda g, j: (n_t - 1 - j, g, 0)),
                  pl.BlockSpec((1, 2, SCAN_B, D), lambda g, j: (g, 0, 0, 0)),
                  _const_spec((2, N_HEADS, HEAD, 2 * HEAD)),
                  _const_spec((2, D)),
                  _const_spec((2, D)),
                  _const_spec((2, D))],
        out_specs=[pl.BlockSpec(blk, lambda g, j: (g, j, 0)),
                   pl.BlockSpec(blk, lambda g, j: (g, n_t - 1 - j, 0)),
                   pl.BlockSpec((1, 2, SCAN_B, D), lambda g, j: (g, 0, 0, 0))],
        out_shape=[jax.ShapeDtypeStruct((n_b, seq_len, D), BF16),
                   jax.ShapeDtypeStruct((n_b, seq_len, D), BF16),
                   jax.ShapeDtypeStruct((n_g, 2, SCAN_B, D), F32)],
        scratch_shapes=[pltpu.VMEM((2, N_HEADS, SCAN_T * SCAN_B, HEAD), F32),
                        pltpu.VMEM((2, N_HEADS, SCAN_T * SCAN_B, HEAD), F32),
                        pltpu.VMEM((2, N_HEADS, SCAN_T * SCAN_B, HEAD), F32),
                        pltpu.VMEM((2, N_HEADS, SCAN_B, HEAD), F32)],
        compiler_params=_params(("parallel", "arbitrary")),
        name="rglru_scan",
    )(xc, xc, h0, wg, ba, bx, lam)


def _pack_halves(v):
    half = v.shape[1] // 2
    lo = lax.bitcast_convert_type(v[:, :half].astype(BF16).astype(F32), jnp.uint32)
    hi = lax.bitcast_convert_type(v[:, half:].astype(BF16).astype(F32), jnp.uint32)
    return lax.bitcast_convert_type((lo >> 16) | (hi & jnp.uint32(0xFFFF0000)), jnp.int32)


def _unpack_halves(p):
    u = lax.bitcast_convert_type(p, jnp.uint32)
    lo = lax.bitcast_convert_type(u << 16, F32)
    hi = lax.bitcast_convert_type(u & jnp.uint32(0xFFFF0000), F32)
    return lo, hi


def _mix_body(x_ref, mod_ref, hf_ref, hb_ref, g1_ref, g2_ref, win_ref, cw_ref,
              wco_ref, wro_ref, wo_ref, wr_ref, br_ref, *rest, row_len, n_cast):
    cast_in = rest[:n_cast]
    x1_ref, h2_ref, route_ref, wts_ref, cnt_ref = rest[n_cast:n_cast + 5]
    cast_out = rest[n_cast + 5:2 * n_cast + 5]
    seen = rest[2 * n_cast + 5]
    x = x_ref[...]
    tm = x.shape[0]
    h = _norm_mod(x, g1_ref[...], mod_ref[0, 1:2, :], mod_ref[0, 0:1, :]).astype(BF16)

    def proj(k):
        return _dot(h, win_ref[:, k * D:(k + 1) * D])

    cv = proj(1) * proj(2)
    pos = lax.broadcasted_iota(jnp.int32, (tm, 1), 0) % row_len
    conv = cv * cw_ref[1:2, :]
    conv = conv + jnp.where(pos >= 1, pltpu.roll(cv, 1, 0), 0.0) * cw_ref[0:1, :]
    conv = conv + jnp.where(pos <= row_len - 2, pltpu.roll(cv, tm - 1, 0), 0.0) * cw_ref[2:3, :]
    y_a = _dot((proj(0) * conv).astype(BF16), wco_ref[...])
    merged = _sigmoid(proj(5)) * y_a

    hs = hf_ref[...].astype(F32) + hb_ref[...].astype(F32)
    y_b = _dot((hs * jax.nn.gelu(proj(4))).astype(BF16), wro_ref[...])
    merged = merged + _sigmoid(proj(6)) * y_b

    mix = _dot(merged.astype(BF16), wo_ref[...])
    x1 = x + mod_ref[0, 2:3, :] * mix
    x1_ref[...] = x1.astype(BF16)
    for src, dst in zip(cast_in, cast_out):
        dst[...] = src[...].astype(BF16)
    h2 = _norm_mod(x1, g2_ref[...], mod_ref[0, 4:5, :], mod_ref[0, 3:4, :])
    h2_ref[...] = _pack_halves(h2)

    h2_hi = h2.astype(BF16)
    h2_lo = (h2 - h2_hi.astype(F32)).astype(BF16)
    big = _dot(h2_hi, wr_ref[...])
    logits = (big[:, :ROUTE_PAD] + big[:, ROUTE_PAD:]
              + _dot(h2_lo, wr_ref[:, :ROUTE_PAD]) + br_ref[...])
    lt = logits.T
    row = lax.broadcasted_iota(jnp.int32, (EPG, tm), 0)
    lg = lt[0:EPG]
    mg = jnp.max(lg, axis=0, keepdims=True)
    p_grp = 1.0 / jnp.sum(jnp.exp(lg - mg), axis=0, keepdims=True)
    grp = jnp.min(jnp.where(lg == mg, row, EPG), axis=0, keepdims=True)
    le = lt[EPG * N_GROUPS:EPG * (N_GROUPS + 1)]
    for g in range(N_GROUPS - 2, -1, -1):
        le = jnp.where(grp == g, lt[EPG * (g + 1):EPG * (g + 2)], le)
    me = jnp.max(le, axis=0, keepdims=True)
    ee = jnp.exp(le - me)
    pe = ee / jnp.sum(ee, axis=0, keepdims=True)
    p1 = jnp.max(pe, axis=0, keepdims=True)
    i1 = jnp.min(jnp.where(pe == p1, row, EPG), axis=0, keepdims=True)
    pe2 = jnp.where(row == i1, -1.0, pe)
    p2 = jnp.max(pe2, axis=0, keepdims=True)
    i2 = jnp.min(jnp.where(pe2 == p2, row, EPG), axis=0, keepdims=True)
    den = p1 + p2
    e1 = grp * EPG + i1
    e2 = grp * EPG + i2

    @pl.when(pl.program_id(0) == 0)
    def _():
        seen[...] = jnp.zeros_like(seen)

    erow = lax.broadcasted_iota(jnp.int32, (N_EXPERTS, tm), 0)
    hit1 = erow == e1
    hit2 = erow == e2
    both = jnp.where(jnp.logical_or(hit1, hit2), 1.0, 0.0)
    tri = jnp.where(lax.broadcasted_iota(jnp.int32, (tm, tm), 0)
                    <= lax.broadcasted_iota(jnp.int32, (tm, tm), 1), 1.0, 0.0).astype(BF16)
    before = _dot(both.astype(BF16), tri) - both + seen[...]
    r1 = jnp.sum(jnp.where(hit1, before, 0.0), axis=0, keepdims=True).astype(jnp.int32)
    r2 = jnp.sum(jnp.where(hit2, before, 0.0), axis=0, keepdims=True).astype(jnp.int32)
    total = seen[...] + jnp.sum(both, axis=1, keepdims=True)
    seen[...] = total
    cnt_ref[...] = total[:, :V7X_LANES].astype(jnp.int32)

    route_ref[...] = jnp.where(row == 0, e1, jnp.where(row == 1, e2, jnp.where(
        row == 2, r1, jnp.where(row == 3, r2, 0))))
    w8 = jnp.where(row == 0, p_grp * p1 / den, jnp.where(row == 1, p_grp * p2 / den, 0.0))
    wts_ref[...] = jnp.concatenate([w8, jnp.zeros((V7X_LANES - EPG, tm), F32)], axis=0).T


def _mixer(x, mod, tiles_per_mod, hf, hb, g1, g2, w_in, cw, wco, wro, wo, wr, br, row_len, cast_ws):
    n = x.shape[0]
    tm = MIX_TM
    assert tm % row_len == 0 and n % tm == 0
    steps = n // tm
    assert N_EXPERTS % steps == 0
    epb = N_EXPERTS // steps
    cast_specs = [pl.BlockSpec((epb,) + w.shape[1:], lambda i: (i, 0, 0)) for w in cast_ws]
    cast_shapes = [jax.ShapeDtypeStruct(w.shape, BF16) for w in cast_ws]
    mod_map = lambda i: (i // tiles_per_mod, 0, 0)
    tok = lambda i: (i, 0)
    col = lambda i: (0, i)
    return pl.pallas_call(
        functools.partial(_mix_body, row_len=row_len, n_cast=len(cast_ws)),
        grid=(steps,),
        in_specs=[pl.BlockSpec((tm, D), tok),
                  pl.BlockSpec((1, 6, D), mod_map),
                  pl.BlockSpec((tm, D), tok),
                  pl.BlockSpec((tm, D), tok),
                  _const_spec((1, D)),
                  _const_spec((1, D)),
                  _const_spec(w_in.shape),
                  _const_spec((3, D)),
                  _const_spec((D, D)),
                  _const_spec((D, D)),
                  _const_spec((D, D)),
                  _const_spec((D, 2 * ROUTE_PAD)),
                  _const_spec((1, ROUTE_PAD))] + cast_specs,
        out_specs=[pl.BlockSpec((tm, D), tok),
                   pl.BlockSpec((tm, D // 2), tok),
                   pl.BlockSpec((EPG, tm), col),
                   pl.BlockSpec((tm, V7X_LANES), tok),
                   pl.BlockSpec((N_EXPERTS, V7X_LANES), lambda i: (0, 0))] + cast_specs,
        out_shape=[jax.ShapeDtypeStruct((n, D), BF16),
                   jax.ShapeDtypeStruct((n, D // 2), jnp.int32),
                   jax.ShapeDtypeStruct((EPG, n), jnp.int32),
                   jax.ShapeDtypeStruct((n, V7X_LANES), F32),
                   jax.ShapeDtypeStruct((N_EXPERTS, V7X_LANES), jnp.int32)] + cast_shapes,
        scratch_shapes=[pltpu.VMEM((N_EXPERTS, tm), F32)],
        compiler_params=_params(("arbitrary",)),
        name="mixer",
    )(x, mod, hf, hb, g1, g2, w_in, cw, wco, wro, wo, wr, br, *cast_ws)


def _sc_mesh():
    return plsc.VectorSubcoreMesh(core_axis_name="c", subcore_axis_name="s",
                                  num_cores=V7X_SC_CORES, num_subcores=V7X_SC_SUBCORES)


def _sc_worker_id():
    return lax.axis_index("s") * V7X_SC_CORES + lax.axis_index("c")


def _sc_dispatch(rows, dest, n_slots):
    n, width = rows.shape
    per_w = n // V7X_SC_WORKERS
    n_ch = per_w // SC_WINDOW
    assert n_ch * SC_WINDOW * V7X_SC_WORKERS == n
    idx = dest.reshape(2, V7X_SC_WORKERS, n_ch, SC_WINDOW).transpose(1, 0, 2, 3)

    def body(x_hbm, d_hbm, o_hbm, idx_v, buf, ld_sem, st_sem):
        wid = _sc_worker_id()
        pltpu.sync_copy(d_hbm.at[wid], idx_v)

        def load(j):
            src = x_hbm.at[pl.ds(wid * per_w + j * SC_WINDOW, SC_WINDOW)]
            return pltpu.async_copy(src, buf.at[j % 2], ld_sem.at[j % 2])

        def scatter(j):
            return [pltpu.async_copy(buf.at[j % 2], o_hbm.at[idx_v.at[k, j]], st_sem.at[j % 2])
                    for k in range(2)]

        loads = {0: load(0)}
        stores = {}
        for j in range(n_ch):
            loads[j].wait()
            if j >= 1:
                for cp in stores[j - 1]:
                    cp.wait()
            if j + 1 < n_ch:
                loads[j + 1] = load(j + 1)
            stores[j] = scatter(j)
        for cp in stores[n_ch - 1]:
            cp.wait()

    return pl.kernel(
        body,
        out_type=jax.ShapeDtypeStruct((n_slots, width), jnp.int32),
        mesh=_sc_mesh(),
        scratch_types=[pltpu.VMEM((2, n_ch, SC_WINDOW), jnp.int32),
                       pltpu.VMEM((2, SC_WINDOW, width), jnp.int32),
                       pltpu.SemaphoreType.DMA((2,)),
                       pltpu.SemaphoreType.DMA((2,))],
        name="sc_dispatch",
    )(rows, idx)


def _sc_collect(rows, dest):
    n = dest.shape[1]
    width = rows.shape[1]
    per_w = n // V7X_SC_WORKERS
    n_ch = per_w // SC_WINDOW
    assert n_ch * SC_WINDOW * V7X_SC_WORKERS == n
    idx = dest.reshape(2, V7X_SC_WORKERS, n_ch, SC_WINDOW).transpose(1, 0, 2, 3)
    windows = [(k, j) for k in range(2) for j in range(n_ch)]

    def body(y_hbm, d_hbm, o_hbm, idx_v, buf, ld_sem, st_sem):
        wid = _sc_worker_id()
        pltpu.sync_copy(d_hbm.at[wid], idx_v)

        def gather(c):
            k, j = windows[c]
            return pltpu.async_copy(y_hbm.at[idx_v.at[k, j]], buf.at[c % 2], ld_sem.at[c % 2])

        def store(c):
            k, j = windows[c]
            dst = o_hbm.at[pl.ds(k * n + wid * per_w + j * SC_WINDOW, SC_WINDOW)]
            return pltpu.async_copy(buf.at[c % 2], dst, st_sem.at[c % 2])

        loads = {0: gather(0)}
        stores = {}
        for c in range(len(windows)):
            loads[c].wait()
            if c >= 1:
                stores[c - 1].wait()
            if c + 1 < len(windows):
                loads[c + 1] = gather(c + 1)
            stores[c] = store(c)
        stores[len(windows) - 1].wait()

    return pl.kernel(
        body,
        out_type=jax.ShapeDtypeStruct((2 * n, width), jnp.int32),
        mesh=_sc_mesh(),
        scratch_types=[pltpu.VMEM((2, n_ch, SC_WINDOW), jnp.int32),
                       pltpu.VMEM((2, SC_WINDOW, width), jnp.int32),
                       pltpu.SemaphoreType.DMA((2,)),
                       pltpu.SemaphoreType.DMA((2,))],
        name="sc_collect",
    )(rows, idx)


def _expert_body(tr_ref, re_ref, nr_ref, nu_ref, xs_hbm, w1_hbm, w3_hbm, w2_hbm, o_ref,
                 xbuf, w1s, w3s, w2s, xsem, sem):
    i = pl.program_id(0)
    n_used = nu_ref[0]
    tme = xbuf.shape[1]
    run = tr_ref[i]
    first = jnp.logical_or(i == 0, run != tr_ref[jnp.maximum(i - 1, 0)])
    slot = run % EXPERT_W_BUFS

    def row_copy(t):
        s = t % EXPERT_ROW_BUFS
        start = t * tme if isinstance(t, int) else pl.multiple_of(t * tme, tme)
        src = xs_hbm.at[pl.ds(start, tme)]
        return pltpu.make_async_copy(src, xbuf.at[s], xsem.at[s])

    def weight_copies(r):
        e = re_ref[r]
        s = r % EXPERT_W_BUFS
        return (pltpu.make_async_copy(w1_hbm.at[e], w1s.at[s], sem.at[0, s]),
                pltpu.make_async_copy(w3_hbm.at[e], w3s.at[s], sem.at[1, s]),
                pltpu.make_async_copy(w2_hbm.at[e], w2s.at[s], sem.at[2, s]))

    @pl.when(i == 0)
    def _():
        for t in range(EXPERT_ROW_BUFS - 1):
            @pl.when(t < n_used)
            def _():
                row_copy(t).start()
        for r in range(EXPERT_W_BUFS - 1):
            @pl.when(r < nr_ref[0])
            def _():
                for cp in weight_copies(r):
                    cp.start()

    @pl.when(first)
    def _():
        @pl.when(run + (EXPERT_W_BUFS - 1) < nr_ref[0])
        def _():
            for cp in weight_copies(run + (EXPERT_W_BUFS - 1)):
                cp.start()

        for cp in weight_copies(run):
            cp.wait()

    @pl.when(i < n_used)
    def _():
        @pl.when(i + (EXPERT_ROW_BUFS - 1) < n_used)
        def _():
            row_copy(i + (EXPERT_ROW_BUFS - 1)).start()

        row_copy(i).wait()
        lo, hi = _unpack_halves(xbuf[i % EXPERT_ROW_BUFS])
        lo = lo.astype(BF16)
        hi = hi.astype(BF16)
        half = D // 2
        a = _dot(lo, w1s[slot, 0:half, :]) + _dot(hi, w1s[slot, half:D, :])
        b = _dot(lo, w3s[slot, 0:half, :]) + _dot(hi, w3s[slot, half:D, :])
        z = (a * _sigmoid(a)) * b
        o_ref[...] = _pack_halves(_dot(z.astype(BF16), w2s[slot]))


def _experts(tile_run, run_e, n_runs, n_used, xs, w1, w3, w2, tme):
    n_slots = xs.shape[0]
    grid_spec = pltpu.PrefetchScalarGridSpec(
        num_scalar_prefetch=4,
        grid=(n_slots // tme,),
        in_specs=[pl.BlockSpec(memory_space=pl.ANY),
                  pl.BlockSpec(memory_space=pl.ANY),
                  pl.BlockSpec(memory_space=pl.ANY),
                  pl.BlockSpec(memory_space=pl.ANY)],
        out_specs=pl.BlockSpec((tme, D // 2),
                               lambda i, tr, re, nr, nu: (jnp.minimum(i, nu[0] - 1), 0)),
        scratch_shapes=[pltpu.VMEM((EXPERT_ROW_BUFS, tme, D // 2), jnp.int32),
                        pltpu.VMEM((EXPERT_W_BUFS, D, D_EXPERT), BF16),
                        pltpu.VMEM((EXPERT_W_BUFS, D, D_EXPERT), BF16),
                        pltpu.VMEM((EXPERT_W_BUFS, D_EXPERT, D), BF16),
                        pltpu.SemaphoreType.DMA((EXPERT_ROW_BUFS,)),
                        pltpu.SemaphoreType.DMA((3, EXPERT_W_BUFS))],
    )
    return pl.pallas_call(
        _expert_body,
        grid_spec=grid_spec,
        out_shape=jax.ShapeDtypeStruct((n_slots, D // 2), jnp.int32),
        compiler_params=_params(("arbitrary",)),
        name="experts",
    )(tile_run, run_e, n_runs, n_used, xs, w1, w3, w2)


def _final_body(x1_ref, mod_ref, y0_ref, y1_ref, wt_ref, gf_ref, o_ref):
    w0 = wt_ref[:, 0:1]
    w1 = wt_ref[:, 1:2]
    lo0, hi0 = _unpack_halves(y0_ref[...])
    lo1, hi1 = _unpack_halves(y1_ref[...])
    moe = jnp.concatenate([w0 * lo0 + w1 * lo1, w0 * hi0 + w1 * hi1], axis=1)
    x2 = x1_ref[...].astype(F32) + mod_ref[0, 5:6, :] * moe
    ms = jnp.mean(x2 * x2, axis=-1, keepdims=True)
    o_ref[...] = x2 * lax.rsqrt(ms + EPS) * gf_ref[...]


def _final(x1, mod, tiles_per_mod, yg, wts, g_final):
    n = x1.shape[0]
    tm = FIN_TM
    nt = n // tm
    return pl.pallas_call(
        _final_body,
        grid=(nt,),
        in_specs=[pl.BlockSpec((tm, D), lambda i: (i, 0)),
                  pl.BlockSpec((1, 6, D), lambda i: (i // tiles_per_mod, 0, 0)),
                  pl.BlockSpec((tm, D // 2), lambda i: (i, 0)),
                  pl.BlockSpec((tm, D // 2), lambda i: (i + nt, 0)),
                  pl.BlockSpec((tm, V7X_LANES), lambda i: (i, 0)),
                  _const_spec((1, D))],
        out_specs=pl.BlockSpec((tm, D), lambda i: (i, 0)),
        out_shape=jax.ShapeDtypeStruct((n, D), F32),
        compiler_params=_params(("parallel",)),
        name="final",
    )(x1, mod, yg, yg, wts, g_final)


def _slot_plan(route, cnt, n, tme):
    counts = cnt[:, 0]
    padded = ((counts + tme - 1) // tme) * tme
    pend = jnp.cumsum(padded)
    pstart = pend - padded
    onehot = route[0:2, :, None] == jnp.arange(N_EXPERTS, dtype=jnp.int32)[None, None, :]
    dest = jnp.sum(jnp.where(onehot, pstart[None, None, :], 0), axis=-1) + route[2:4]
    n_slots = ((2 * n + N_EXPERTS * (tme - 1)) // tme) * tme
    tile_start = jnp.arange(n_slots // tme, dtype=jnp.int32) * tme
    tile_e = jnp.sum((tile_start[:, None] >= pend[None, :]).astype(jnp.int32), axis=1)
    tile_e = jnp.minimum(tile_e, N_EXPERTS - 1)
    n_used = (pend[-1] // tme).astype(jnp.int32).reshape(1)
    used = counts > 0
    run_of_e = jnp.cumsum(used.astype(jnp.int32)) - 1
    eids = jnp.arange(N_EXPERTS, dtype=jnp.int32)
    run_e = jnp.sum(jnp.where(used[None, :] & (run_of_e[None, :] == eids[:, None]), eids[None, :], 0), axis=1)
    tile_run = jnp.sum(jnp.where(tile_e[:, None] == eids[None, :], run_of_e[None, :], 0), axis=1)
    n_runs = jnp.sum(used.astype(jnp.int32)).reshape(1)
    plan = (tile_run.astype(jnp.int32), run_e.astype(jnp.int32), n_runs, n_used)
    return dest.astype(jnp.int32), plan, n_slots


def _group(x, mod, mod_per_seq, h0, p, row_len, tme, cast_ws):
    n_b, seq_len, _ = x.shape
    n = n_b * seq_len
    xt = x.reshape(n, D)
    mod_seq = mod if mod_per_seq else jnp.broadcast_to(mod, (n_b, 6, D))
    xc = _xr_conv(x, mod_seq, p["g1"], p["w_in"], p["rnn_conv_w"], p["rnn_conv_b"])
    hf, hb, last = _rglru_scan(xc, h0, p["wg"], p["ba"], p["bx"], p["lam"])
    tiles_per_mod = (seq_len // MIX_TM) if mod_per_seq else (n // MIX_TM)
    outs = _mixer(xt, mod, tiles_per_mod, hf.reshape(n, D), hb.reshape(n, D),
                                     p["g1"], p["g2"], p["w_in"], p["conv_w"], p["wco"], p["wro"],
                                     p["wo"], p["wr"], p["br"], row_len, cast_ws)
    x1, h2, route, wts, cnt = outs[:5]
    casts = outs[5:]
    dest, plan, n_slots = _slot_plan(route, cnt, n, tme)
    xs = _sc_dispatch(h2, dest, n_slots)

    def finish(w1b, w3b, w2b):
        ys = _experts(*plan, xs, w1b, w3b, w2b, tme)
        yg = _sc_collect(ys, dest)
        tiles_per_mod_f = (seq_len // FIN_TM) if mod_per_seq else (n // FIN_TM)
        y = _final(x1, mod, tiles_per_mod_f, yg, wts, p["g_final"])
        return y.reshape(n_b, seq_len, D)

    return finish, last, casts


def kernel(x_prompt, x_sample, state_rnn, c, c_ctx, w_ada, b_ada, g_norm1, g_norm2, w_in, conv_w, w_conv_out, rnn_conv_w, rnn_conv_b, w_gate_a, b_gate_a, w_gate_x, b_gate_x, lam, w_rnn_out, w_o, w_router_group, b_router_group, w_router_expert, b_router_expert, w1, w3, w2, g_final):
    assert w_ada.shape[0] == 1, "single layer"
    n_pb, n_sb = x_prompt.shape[0], x_sample.shape[0]

    cond = jnp.concatenate([c_ctx[None, :], c, jnp.zeros((16 - 1 - n_sb, D), F32)], axis=0)
    mod = _ada(cond, w_ada[0], b_ada[0]).reshape(16, 6, D)

    w_in_b = w_in[0].astype(BF16)
    pad_w = jnp.zeros((D, EPG - N_GROUPS), F32)
    wr = jnp.concatenate([w_router_group[0], pad_w, w_router_expert[0],
                          jnp.zeros((D, ROUTE_PAD - EPG - N_EXPERTS), F32)], axis=1)
    br = jnp.concatenate([b_router_group[0], jnp.full((EPG - N_GROUPS,), NEG_BIG, F32),
                          b_router_expert[0],
                          jnp.zeros((ROUTE_PAD - EPG - N_EXPERTS,), F32)]).reshape(1, ROUTE_PAD)
    wr_hi = wr.astype(BF16)
    p = dict(
        g1=g_norm1, g2=g_norm2, w_in=w_in_b,
        conv_w=conv_w[0], rnn_conv_w=rnn_conv_w[0], rnn_conv_b=rnn_conv_b,
        wg=(0.5 * jnp.concatenate([w_gate_a[0], w_gate_x[0]], axis=-1)).astype(BF16),
        ba=b_gate_a[0], bx=b_gate_x[0], lam=lam[0],
        wco=w_conv_out[0].astype(BF16), wro=w_rnn_out[0].astype(BF16), wo=w_o[0].astype(BF16),
        wr=jnp.concatenate([wr_hi, (wr - wr_hi.astype(F32)).astype(BF16)], axis=1), br=br,
        g_final=g_final.reshape(1, D),
    )

    h0_p = jnp.zeros((n_pb // SCAN_B, 2, SCAN_B, D), F32)
    finish_p, last, (w2b,) = _group(x_prompt, mod[0:1], False, h0_p, p, x_prompt.shape[1], 512,
                                    [w2[0]])
    state_new = last.transpose(0, 2, 1, 3).reshape(n_pb, 1, 2, D)

    h0_s = state_rnn[:, 0].reshape(n_sb // SCAN_B, SCAN_B, 2, D).transpose(0, 2, 1, 3)
    finish_s, _, (w1b, w3b) = _group(x_sample, mod[1:1 + n_sb], True, h0_s, p, GRID_W, 512,
                                     [w1[0], w3[0]])
    return (finish_p(w1b, w3b, w2b), finish_s(w1b, w3b, w2b), state_new)
```

```python
import functools

import jax
import jax.numpy as jnp
from jax import lax
from jax.experimental import pallas as pl
from jax.experimental.pallas import tpu as pltpu
from jax.experimental.pallas import tpu_sc as plsc

D = 1024
N_HEADS = 8
HEAD = D // N_HEADS
GRID_W = 64
RG_C = 8.0
N_GROUPS = 4
EPG = 8
N_EXPERTS = N_GROUPS * EPG
D_EXPERT = 512
EPS = 1e-6
F32 = jnp.float32
BF16 = jnp.bfloat16

V7X_LANES = 128
V7X_SUBLANES = 8
V7X_VMEM_LIMIT_BYTES = 56 * 1024 * 1024
V7X_SC_CORES = 2
V7X_SC_SUBCORES = 16
V7X_SC_WORKERS = V7X_SC_CORES * V7X_SC_SUBCORES
SC_WINDOW = 64

XR_T = 256
XR_SUB = 32
XR_LEFT = 2
XR_COL = 3
XR_BUFS = 3
SCAN_T = 128
SCAN_B = V7X_SUBLANES
SCAN_UNROLL = 8
LOG2_E = 1.4426950408889634
TINY = 1e-30
MIX_TM = 512
FIN_TM = 1024
EXPERT_ROW_BUFS = 3
EXPERT_W_BUFS = 3
ROUTE_PAD = 128
NEG_BIG = -1e30


def _sigmoid(x):
    return 0.5 * jnp.tanh(0.5 * x) + 0.5


def _norm_mod(x, g, scale, shift):
    ms = jnp.mean(x * x, axis=-1, keepdims=True)
    return (x * lax.rsqrt(ms + EPS)) * (g * (1.0 + scale)) + shift


def _dot(a, b):
    return jnp.dot(a, b, preferred_element_type=F32)


def _params(sem, vmem=V7X_VMEM_LIMIT_BYTES):
    return pltpu.CompilerParams(dimension_semantics=sem, vmem_limit_bytes=vmem)


def _const_spec(shape):
    zeros = (0,) * len(shape)
    return pl.BlockSpec(shape, lambda *_: zeros, pipeline_mode=pl.Buffered(1))


def _ada_body(c_ref, w_ref, b_ref, o_ref):
    c = c_ref[...]
    s = (c * _sigmoid(c)).astype(BF16)
    o_ref[...] = _dot(s, w_ref[...].astype(BF16)) + b_ref[...]


def _ada(cond, w_ada, b_ada):
    rows = cond.shape[0]
    n_out = w_ada.shape[1]
    return pl.pallas_call(
        _ada_body,
        grid=(n_out // D,),
        in_specs=[pl.BlockSpec((rows, D), lambda i: (0, 0)),
                  pl.BlockSpec((D, D), lambda i: (0, i)),
                  pl.BlockSpec((1, D), lambda i: (0, i))],
        out_specs=pl.BlockSpec((rows, D), lambda i: (0, i)),
        out_shape=jax.ShapeDtypeStruct((rows, n_out), F32),
        compiler_params=_params(("parallel",)),
        name="ada",
    )(cond, w_ada, b_ada.reshape(1, n_out))


def _xr_body(x_hbm, xn_ref, mod_ref, g_ref, w_ref, cw_ref, cb_ref, o_ref, xring, xt, xsem, *,
             n_t, n_steps):
    j = pl.program_id(1)
    t_len = xring.shape[2]
    nb = SCAN_B
    body0 = XR_LEFT * nb

    step = pl.program_id(0) * n_t + j

    def x_copy(s):
        slot = s % XR_BUFS
        src = x_hbm.at[pl.ds((s // n_t) * nb, nb), pl.ds((s % n_t) * t_len, t_len)]
        return pltpu.make_async_copy(src, xring.at[slot], xsem.at[slot])

    @pl.when(step == 0)
    def _():
        for s in range(XR_BUFS - 1):
            @pl.when(s < n_steps)
            def _():
                x_copy(s).start()

    @pl.when(step + (XR_BUFS - 1) < n_steps)
    def _():
        x_copy(step + (XR_BUFS - 1)).start()

    x_copy(step).wait()
    x_ref = xring.at[step % XR_BUFS]

    @pl.when(j == 0)
    def _():
        xt[:, 0:body0, :] = jnp.zeros((N_HEADS, body0, HEAD), F32)

    @pl.when(j > 0)
    def _():
        xt[:, 0:body0, :] = xt[:, t_len * nb:t_len * nb + body0, :]

    g = g_ref[...]
    for b in range(nb):
        h = _norm_mod(x_ref[b], g, mod_ref[b, 1:2, :], mod_ref[b, 0:1, :]).astype(BF16)
        r = _dot(h, w_ref[...])
        for s in range(N_HEADS):
            xt[s, pl.ds(body0 + b, t_len, stride=nb), :] = r[:, s * HEAD:(s + 1) * HEAD]

    hn = _norm_mod(xn_ref[:, 0, :], g, mod_ref[:, 1, :], mod_ref[:, 0, :]).astype(BF16)
    rn = jnp.where(j < n_t - 1, _dot(hn, w_ref[...]), 0.0)
    tail = body0 + t_len * nb
    for s in range(N_HEADS):
        xt[s, tail:tail + nb, :] = rn[:, s * HEAD:(s + 1) * HEAD]

    sub = XR_SUB
    for s in range(N_HEADS):
        sl = slice(s * HEAD, (s + 1) * HEAD)
        for t0 in range(0, t_len, sub):
            y = cb_ref[:, sl]
            for k in range(4):
                r0 = (t0 + k) * nb
                y = y + xt[s, r0:r0 + sub * nb, :] * cw_ref[k:k + 1, sl]
            o_ref[t0:t0 + sub, :, sl] = y.reshape(sub, nb, HEAD)


def _xr_conv(x, mod_seq, g1, w_in, cw, cb):
    n_b, seq_len, _ = x.shape
    t_len = min(XR_T, seq_len)
    assert n_b % SCAN_B == 0 and seq_len % t_len == 0 and t_len % XR_SUB == 0
    n_t = seq_len // t_len
    nxt = t_len // V7X_SUBLANES
    last_blk = seq_len // V7X_SUBLANES - 1
    n_g = n_b // SCAN_B
    return pl.pallas_call(
        functools.partial(_xr_body, n_t=n_t, n_steps=n_g * n_t),
        grid=(n_g, n_t),
        in_specs=[pl.BlockSpec(memory_space=pl.ANY),
                  pl.BlockSpec((SCAN_B, V7X_SUBLANES, D),
                               lambda g, j: (g, jnp.minimum((j + 1) * nxt, last_blk), 0)),
                  pl.BlockSpec((SCAN_B, 6, D), lambda g, j: (g, 0, 0)),
                  _const_spec((1, D)),
                  pl.BlockSpec((D, D), lambda g, j: (0, XR_COL), pipeline_mode=pl.Buffered(1)),
                  _const_spec((4, D)),
                  _const_spec((1, D))],
        out_specs=pl.BlockSpec((t_len, SCAN_B, D), lambda g, j: (j, g, 0)),
        out_shape=jax.ShapeDtypeStruct((seq_len, n_b, D), F32),
        scratch_shapes=[pltpu.VMEM((XR_BUFS, SCAN_B, t_len, D), F32),
                        pltpu.VMEM((N_HEADS, (t_len + XR_LEFT + 1) * SCAN_B, HEAD), F32),
                        pltpu.SemaphoreType.DMA((XR_BUFS,))],
        compiler_params=_params(("arbitrary", "arbitrary")),
        name="xr_conv",
    )(x, x, mod_seq, g1, w_in, cw, cb)


def _scan_body(xf_ref, xb_ref, h0_ref, wg_ref, ba_ref, bx_ref, lam_ref,
               hf_ref, hb_ref, last_ref, a_s, u_s, h_s, hc):
    j = pl.program_id(1)
    n_t = pl.num_programs(1)
    t_len = xf_ref.shape[0]
    rows = SCAN_B * t_len

    @pl.when(j == 0)
    def _():
        for d in range(2):
            for s in range(N_HEADS):
                hc[d, s] = h0_ref[0, d, :, s * HEAD:(s + 1) * HEAD]

    for d, x_ref in ((0, xf_ref), (1, xb_ref)):
        z = -lam_ref[d:d + 1, :]
        sp = jnp.maximum(z, 0.0) + jnp.log(1.0 + jnp.exp(-jnp.abs(z)))
        c2 = (-0.5 * RG_C * LOG2_E) * sp
        for hd in range(N_HEADS):
            sl = slice(hd * HEAD, (hd + 1) * HEAD)
            xh = x_ref[:, :, sl].reshape(rows, HEAD)
            g = _dot(xh.astype(BF16), wg_ref[d, hd])
            t_r = jnp.tanh(g[:, :HEAD] + 0.5 * ba_ref[d:d + 1, sl])
            t_i = jnp.tanh(g[:, HEAD:] + 0.5 * bx_ref[d:d + 1, sl])
            a = jnp.exp2(c2[:, sl] * t_r + c2[:, sl])
            q = 0.25 - 0.25 * (a * a)
            half_mult = q * lax.rsqrt(jnp.maximum(q, TINY))
            u = half_mult * ((t_i + 1.0) * xh)
            a_s[d, hd] = a
            u_s[d, hd] = u

    def step(tb, carry):
        h_f, h_b = carry
        for k in range(SCAN_UNROLL):
            t = tb * SCAN_UNROLL + k
            rf = pl.multiple_of(t * SCAN_B, SCAN_B)
            rb = pl.multiple_of((t_len - 1 - t) * SCAN_B, SCAN_B)
            h_f = a_s[0, :, pl.ds(rf, SCAN_B), :] * h_f + u_s[0, :, pl.ds(rf, SCAN_B), :]
            h_b = a_s[1, :, pl.ds(rb, SCAN_B), :] * h_b + u_s[1, :, pl.ds(rb, SCAN_B), :]
            h_s[0, :, pl.ds(rf, SCAN_B), :] = h_f
            h_s[1, :, pl.ds(rb, SCAN_B), :] = h_b
        return h_f, h_b

    h_f, h_b = lax.fori_loop(0, t_len // SCAN_UNROLL, step, (hc[0], hc[1]))
    hc[0] = h_f
    hc[1] = h_b

    for d, o_ref in ((0, hf_ref), (1, hb_ref)):
        for hd in range(N_HEADS):
            for b in range(SCAN_B):
                o_ref[b, :, hd * HEAD:(hd + 1) * HEAD] = (
                    h_s[d, hd, pl.ds(b, t_len, stride=SCAN_B), :].astype(BF16))

    @pl.when(j == n_t - 1)
    def _():
        for d in range(2):
            for s in range(N_HEADS):
                last_ref[0, d, :, s * HEAD:(s + 1) * HEAD] = hc[d, s]


def _rglru_scan(xc, h0, wg, ba, bx, lam):
    seq_len, n_b, _ = xc.shape
    n_g = n_b // SCAN_B
    n_t = seq_len // SCAN_T
    blk = (SCAN_B, SCAN_T, D)
    blk_in = (SCAN_T, SCAN_B, D)
    return pl.pallas_call(
        _scan_body,
        grid=(n_g, n_t),
        in_specs=[pl.BlockSpec(blk_in, lambda g, j: (j, g, 0)),
                  pl.BlockSpec(blk_in, lambda g, j: (n_t - 1 - j, g, 0)),
                  pl.BlockSpec((1, 2, SCAN_B, D), lambda g, j: (g, 0, 0, 0)),
                  _const_spec((2, N_HEADS, HEAD, 2 * HEAD)),
                  _const_spec((2, D)),
                  _const_spec((2, D)),
                  _const_spec((2, D))],
        out_specs=[pl.BlockSpec(blk, lambda g, j: (g, j, 0)),
                   pl.BlockSpec(blk, lambda g, j: (g, n_t - 1 - j, 0)),
                   pl.BlockSpec((1, 2, SCAN_B, D), lambda g, j: (g, 0, 0, 0))],
        out_shape=[jax.ShapeDtypeStruct((n_b, seq_len, D), BF16),
                   jax.ShapeDtypeStruct((n_b, seq_len, D), BF16),
                   jax.ShapeDtypeStruct((n_g, 2, SCAN_B, D), F32)],
        scratch_shapes=[pltpu.VMEM((2, N_HEADS, SCAN_T * SCAN_B, HEAD), F32),
                        pltpu.VMEM((2, N_HEADS, SCAN_T * SCAN_B, HEAD), F32),
                        pltpu.VMEM((2, N_HEADS, SCAN_T * SCAN_B, HEAD), F32),
                        pltpu.VMEM((2, N_HEADS, SCAN_B, HEAD), F32)],
        compiler_params=_params(("parallel", "arbitrary")),
        name="rglru_scan",
    )(xc, xc, h0, wg, ba, bx, lam)


def _pack_halves(v):
    half = v.shape[1] // 2
    lo = lax.bitcast_convert_type(v[:, :half].astype(BF16).astype(F32), jnp.uint32)
    hi = lax.bitcast_convert_type(v[:, half:].astype(BF16).astype(F32), jnp.uint32)
    return lax.bitcast_convert_type((lo >> 16) | (hi & jnp.uint32(0xFFFF0000)), jnp.int32)


def _unpack_halves(p):
    u = lax.bitcast_convert_type(p, jnp.uint32)
    lo = lax.bitcast_convert_type(u << 16, F32)
    hi = lax.bitcast_convert_type(u & jnp.uint32(0xFFFF0000), F32)
    return lo, hi


def _mix_body(x_ref, mod_ref, hf_ref, hb_ref, g1_ref, g2_ref, win_ref, cw_ref,
              wco_ref, wro_ref, wo_ref, wr_ref, br_ref, *rest, row_len, n_cast):
    cast_in = rest[:n_cast]
    x1_ref, h2_ref, route_ref, wts_ref, cnt_ref = rest[n_cast:n_cast + 5]
    cast_out = rest[n_cast + 5:2 * n_cast + 5]
    seen = rest[2 * n_cast + 5]
    x = x_ref[...]
    tm = x.shape[0]
    h = _norm_mod(x, g1_ref[...], mod_ref[0, 1:2, :], mod_ref[0, 0:1, :]).astype(BF16)

    def proj(k):
        return _dot(h, win_ref[:, k * D:(k + 1) * D])

    cv = proj(1) * proj(2)
    pos = lax.broadcasted_iota(jnp.int32, (tm, 1), 0) % row_len
    conv = cv * cw_ref[1:2, :]
    conv = conv + jnp.where(pos >= 1, pltpu.roll(cv, 1, 0), 0.0) * cw_ref[0:1, :]
    conv = conv + jnp.where(pos <= row_len - 2, pltpu.roll(cv, tm - 1, 0), 0.0) * cw_ref[2:3, :]
    y_a = _dot((proj(0) * conv).astype(BF16), wco_ref[...])
    merged = _sigmoid(proj(5)) * y_a

    hs = hf_ref[...].astype(F32) + hb_ref[...].astype(F32)
    y_b = _dot((hs * jax.nn.gelu(proj(4))).astype(BF16), wro_ref[...])
    merged = merged + _sigmoid(proj(6)) * y_b

    mix = _dot(merged.astype(BF16), wo_ref[...])
    x1 = x + mod_ref[0, 2:3, :] * mix
    x1_ref[...] = x1.astype(BF16)
    for src, dst in zip(cast_in, cast_out):
        dst[...] = src[...].astype(BF16)
    h2 = _norm_mod(x1, g2_ref[...], mod_ref[0, 4:5, :], mod_ref[0, 3:4, :])
    h2_ref[...] = _pack_halves(h2)

    h2_hi = h2.astype(BF16)
    h2_lo = (h2 - h2_hi.astype(F32)).astype(BF16)
    big = _dot(h2_hi, wr_ref[...])
    logits = (big[:, :ROUTE_PAD] + big[:, ROUTE_PAD:]
              + _dot(h2_lo, wr_ref[:, :ROUTE_PAD]) + br_ref[...])
    lt = logits.T
    row = lax.broadcasted_iota(jnp.int32, (EPG, tm), 0)
    lg = lt[0:EPG]
    mg = jnp.max(lg, axis=0, keepdims=True)
    p_grp = 1.0 / jnp.sum(jnp.exp(lg - mg), axis=0, keepdims=True)
    grp = jnp.min(jnp.where(lg == mg, row, EPG), axis=0, keepdims=True)
    le = lt[EPG * N_GROUPS:EPG * (N_GROUPS + 1)]
    for g in range(N_GROUPS - 2, -1, -1):
        le = jnp.where(grp == g, lt[EPG * (g + 1):EPG * (g + 2)], le)
    me = jnp.max(le, axis=0, keepdims=True)
    ee = jnp.exp(le - me)
    pe = ee / jnp.sum(ee, axis=0, keepdims=True)
    p1 = jnp.max(pe, axis=0, keepdims=True)
    i1 = jnp.min(jnp.where(pe == p1, row, EPG), axis=0, keepdims=True)
    pe2 = jnp.where(row == i1, -1.0, pe)
    p2 = jnp.max(pe2, axis=0, keepdims=True)
    i2 = jnp.min(jnp.where(pe2 == p2, row, EPG), axis=0, keepdims=True)
    den = p1 + p2
    e1 = grp * EPG + i1
    e2 = grp * EPG + i2

    @pl.when(pl.program_id(0) == 0)
    def _():
        seen[...] = jnp.zeros_like(seen)

    erow = lax.broadcasted_iota(jnp.int32, (N_EXPERTS, tm), 0)
    hit1 = erow == e1
    hit2 = erow == e2
    both = jnp.where(jnp.logical_or(hit1, hit2), 1.0, 0.0)
    tri = jnp.where(lax.broadcasted_iota(jnp.int32, (tm, tm), 0)
                    <= lax.broadcasted_iota(jnp.int32, (tm, tm), 1), 1.0, 0.0).astype(BF16)
    before = _dot(both.astype(BF16), tri) - both + seen[...]
    r1 = jnp.sum(jnp.where(hit1, before, 0.0), axis=0, keepdims=True).astype(jnp.int32)
    r2 = jnp.sum(jnp.where(hit2, before, 0.0), axis=0, keepdims=True).astype(jnp.int32)
    total = seen[...] + jnp.sum(both, axis=1, keepdims=True)
    seen[...] = total
    cnt_ref[...] = total[:, :V7X_LANES].astype(jnp.int32)

    route_ref[...] = jnp.where(row == 0, e1, jnp.where(row == 1, e2, jnp.where(
        row == 2, r1, jnp.where(row == 3, r2, 0))))
    w8 = jnp.where(row == 0, p_grp * p1 / den, jnp.where(row == 1, p_grp * p2 / den, 0.0))
    wts_ref[...] = jnp.concatenate([w8, jnp.zeros((V7X_LANES - EPG, tm), F32)], axis=0).T


def _mixer(x, mod, tiles_per_mod, hf, hb, g1, g2, w_in, cw, wco, wro, wo, wr, br, row_len, cast_ws):
    n = x.shape[0]
    tm = MIX_TM
    assert tm % row_len == 0 and n % tm == 0
    steps = n // tm
    assert N_EXPERTS % steps == 0
    epb = N_EXPERTS // steps
    cast_specs = [pl.BlockSpec((epb,) + w.shape[1:], lambda i: (i, 0, 0)) for w in cast_ws]
    cast_shapes = [jax.ShapeDtypeStruct(w.shape, BF16) for w in cast_ws]
    mod_map = lambda i: (i // tiles_per_mod, 0, 0)
    tok = lambda i: (i, 0)
    col = lambda i: (0, i)
    return pl.pallas_call(
        functools.partial(_mix_body, row_len=row_len, n_cast=len(cast_ws)),
        grid=(steps,),
        in_specs=[pl.BlockSpec((tm, D), tok),
                  pl.BlockSpec((1, 6, D), mod_map),
                  pl.BlockSpec((tm, D), tok),
                  pl.BlockSpec((tm, D), tok),
                  _const_spec((1, D)),
                  _const_spec((1, D)),
                  _const_spec(w_in.shape),
                  _const_spec((3, D)),
                  _const_spec((D, D)),
                  _const_spec((D, D)),
                  _const_spec((D, D)),
                  _const_spec((D, 2 * ROUTE_PAD)),
                  _const_spec((1, ROUTE_PAD))] + cast_specs,
        out_specs=[pl.BlockSpec((tm, D), tok),
                   pl.BlockSpec((tm, D // 2), tok),
                   pl.BlockSpec((EPG, tm), col),
                   pl.BlockSpec((tm, V7X_LANES), tok),
                   pl.BlockSpec((N_EXPERTS, V7X_LANES), lambda i: (0, 0))] + cast_specs,
        out_shape=[jax.ShapeDtypeStruct((n, D), BF16),
                   jax.ShapeDtypeStruct((n, D // 2), jnp.int32),
                   jax.ShapeDtypeStruct((EPG, n), jnp.int32),
                   jax.ShapeDtypeStruct((n, V7X_LANES), F32),
                   jax.ShapeDtypeStruct((N_EXPERTS, V7X_LANES), jnp.int32)] + cast_shapes,
        scratch_shapes=[pltpu.VMEM((N_EXPERTS, tm), F32)],
        compiler_params=_params(("arbitrary",)),
        name="mixer",
    )(x, mod, hf, hb, g1, g2, w_in, cw, wco, wro, wo, wr, br, *cast_ws)


def _sc_mesh():
    return plsc.VectorSubcoreMesh(core_axis_name="c", subcore_axis_name="s",
                                  num_cores=V7X_SC_CORES, num_subcores=V7X_SC_SUBCORES)


def _sc_worker_id():
    return lax.axis_index("s") * V7X_SC_CORES + lax.axis_index("c")


def _sc_dispatch(rows, dest, n_slots):
    n, width = rows.shape
    per_w = n // V7X_SC_WORKERS
    n_ch = per_w // SC_WINDOW
    assert n_ch * SC_WINDOW * V7X_SC_WORKERS == n
    idx = dest.reshape(2, V7X_SC_WORKERS, n_ch, SC_WINDOW).transpose(1, 0, 2, 3)

    def body(x_hbm, d_hbm, o_hbm, idx_v, buf, ld_sem, st_sem):
        wid = _sc_worker_id()
        pltpu.sync_copy(d_hbm.at[wid], idx_v)

        def load(j):
            src = x_hbm.at[pl.ds(wid * per_w + j * SC_WINDOW, SC_WINDOW)]
            return pltpu.async_copy(src, buf.at[j % 2], ld_sem.at[j % 2])

        def scatter(j):
            return [pltpu.async_copy(buf.at[j % 2], o_hbm.at[idx_v.at[k, j]], st_sem.at[j % 2])
                    for k in range(2)]

        loads = {0: load(0)}
        stores = {}
        for j in range(n_ch):
            loads[j].wait()
            if j >= 1:
                for cp in stores[j - 1]:
                    cp.wait()
            if j + 1 < n_ch:
                loads[j + 1] = load(j + 1)
            stores[j] = scatter(j)
        for cp in stores[n_ch - 1]:
            cp.wait()

    return pl.kernel(
        body,
        out_type=jax.ShapeDtypeStruct((n_slots, width), jnp.int32),
        mesh=_sc_mesh(),
        scratch_types=[pltpu.VMEM((2, n_ch, SC_WINDOW), jnp.int32),
                       pltpu.VMEM((2, SC_WINDOW, width), jnp.int32),
                       pltpu.SemaphoreType.DMA((2,)),
                       pltpu.SemaphoreType.DMA((2,))],
        name="sc_dispatch",
    )(rows, idx)


def _sc_collect(rows, dest):
    n = dest.shape[1]
    width = rows.shape[1]
    per_w = n // V7X_SC_WORKERS
    n_ch = per_w // SC_WINDOW
    assert n_ch * SC_WINDOW * V7X_SC_WORKERS == n
    idx = dest.reshape(2, V7X_SC_WORKERS, n_ch, SC_WINDOW).transpose(1, 0, 2, 3)
    windows = [(k, j) for k in range(2) for j in range(n_ch)]

    def body(y_hbm, d_hbm, o_hbm, idx_v, buf, ld_sem, st_sem):
        wid = _sc_worker_id()
        pltpu.sync_copy(d_hbm.at[wid], idx_v)

        def gather(c):
            k, j = windows[c]
            return pltpu.async_copy(y_hbm.at[idx_v.at[k, j]], buf.at[c % 2], ld_sem.at[c % 2])

        def store(c):
            k, j = windows[c]
            dst = o_hbm.at[pl.ds(k * n + wid * per_w + j * SC_WINDOW, SC_WINDOW)]
            return pltpu.async_copy(buf.at[c % 2], dst, st_sem.at[c % 2])

        loads = {0: gather(0)}
        stores = {}
        for c in range(len(windows)):
            loads[c].wait()
            if c >= 1:
                stores[c - 1].wait()
            if c + 1 < len(windows):
                loads[c + 1] = gather(c + 1)
            stores[c] = store(c)
        stores[len(windows) - 1].wait()

    return pl.kernel(
        body,
        out_type=jax.ShapeDtypeStruct((2 * n, width), jnp.int32),
        mesh=_sc_mesh(),
        scratch_types=[pltpu.VMEM((2, n_ch, SC_WINDOW), jnp.int32),
                       pltpu.VMEM((2, SC_WINDOW, width), jnp.int32),
                       pltpu.SemaphoreType.DMA((2,)),
                       pltpu.SemaphoreType.DMA((2,))],
        name="sc_collect",
    )(rows, idx)


def _expert_body(tr_ref, re_ref, nr_ref, nu_ref, xs_hbm, w1_hbm, w3_hbm, w2_hbm, o_ref,
                 xbuf, w1s, w3s, w2s, xsem, sem):
    i = pl.program_id(0)
    n_used = nu_ref[0]
    tme = xbuf.shape[1]
    run = tr_ref[i]
    first = jnp.logical_or(i == 0, run != tr_ref[jnp.maximum(i - 1, 0)])
    slot = run % EXPERT_W_BUFS

    def row_copy(t):
        s = t % EXPERT_ROW_BUFS
        start = t * tme if isinstance(t, int) else pl.multiple_of(t * tme, tme)
        src = xs_hbm.at[pl.ds(start, tme)]
        return pltpu.make_async_copy(src, xbuf.at[s], xsem.at[s])

    def weight_copies(r):
        e = re_ref[r]
        s = r % EXPERT_W_BUFS
        return (pltpu.make_async_copy(w1_hbm.at[e], w1s.at[s], sem.at[0, s]),
                pltpu.make_async_copy(w3_hbm.at[e], w3s.at[s], sem.at[1, s]),
                pltpu.make_async_copy(w2_hbm.at[e], w2s.at[s], sem.at[2, s]))

    @pl.when(i == 0)
    def _():
        for t in range(EXPERT_ROW_BUFS - 1):
            @pl.when(t < n_used)
            def _():
                row_copy(t).start()
        for r in range(EXPERT_W_BUFS - 1):
            @pl.when(r < nr_ref[0])
            def _():
                for cp in weight_copies(r):
                    cp.start()

    @pl.when(first)
    def _():
        @pl.when(run + (EXPERT_W_BUFS - 1) < nr_ref[0])
        def _():
            for cp in weight_copies(run + (EXPERT_W_BUFS - 1)):
                cp.start()

        for cp in weight_copies(run):
            cp.wait()

    @pl.when(i < n_used)
    def _():
        @pl.when(i + (EXPERT_ROW_BUFS - 1) < n_used)
        def _():
            row_copy(i + (EXPERT_ROW_BUFS - 1)).start()

        row_copy(i).wait()
        lo, hi = _unpack_halves(xbuf[i % EXPERT_ROW_BUFS])
        lo = lo.astype(BF16)
        hi = hi.astype(BF16)
        half = D // 2
        a = _dot(lo, w1s[slot, 0:half, :]) + _dot(hi, w1s[slot, half:D, :])
        b = _dot(lo, w3s[slot, 0:half, :]) + _dot(hi, w3s[slot, half:D, :])
        z = (a * _sigmoid(a)) * b
        o_ref[...] = _pack_halves(_dot(z.astype(BF16), w2s[slot]))


def _experts(tile_run, run_e, n_runs, n_used, xs, w1, w3, w2, tme):
    n_slots = xs.shape[0]
    grid_spec = pltpu.PrefetchScalarGridSpec(
        num_scalar_prefetch=4,
        grid=(n_slots // tme,),
        in_specs=[pl.BlockSpec(memory_space=pl.ANY),
                  pl.BlockSpec(memory_space=pl.ANY),
                  pl.BlockSpec(memory_space=pl.ANY),
                  pl.BlockSpec(memory_space=pl.ANY)],
        out_specs=pl.BlockSpec((tme, D // 2),
                               lambda i, tr, re, nr, nu: (jnp.minimum(i, nu[0] - 1), 0)),
        scratch_shapes=[pltpu.VMEM((EXPERT_ROW_BUFS, tme, D // 2), jnp.int32),
                        pltpu.VMEM((EXPERT_W_BUFS, D, D_EXPERT), BF16),
                        pltpu.VMEM((EXPERT_W_BUFS, D, D_EXPERT), BF16),
                        pltpu.VMEM((EXPERT_W_BUFS, D_EXPERT, D), BF16),
                        pltpu.SemaphoreType.DMA((EXPERT_ROW_BUFS,)),
                        pltpu.SemaphoreType.DMA((3, EXPERT_W_BUFS))],
    )
    return pl.pallas_call(
        _expert_body,
        grid_spec=grid_spec,
        out_shape=jax.ShapeDtypeStruct((n_slots, D // 2), jnp.int32),
        compiler_params=_params(("arbitrary",)),
        name="experts",
    )(tile_run, run_e, n_runs, n_used, xs, w1, w3, w2)


def _final_body(x1_ref, mod_ref, y0_ref, y1_ref, wt_ref, gf_ref, o_ref):
    w0 = wt_ref[:, 0:1]
    w1 = wt_ref[:, 1:2]
    lo0, hi0 = _unpack_halves(y0_ref[...])
    lo1, hi1 = _unpack_halves(y1_ref[...])
    moe = jnp.concatenate([w0 * lo0 + w1 * lo1, w0 * hi0 + w1 * hi1], axis=1)
    x2 = x1_ref[...].astype(F32) + mod_ref[0, 5:6, :] * moe
    ms = jnp.mean(x2 * x2, axis=-1, keepdims=True)
    o_ref[...] = x2 * lax.rsqrt(ms + EPS) * gf_ref[...]


def _final(x1, mod, tiles_per_mod, yg, wts, g_final):
    n = x1.shape[0]
    tm = FIN_TM
    nt = n // tm
    return pl.pallas_call(
        _final_body,
        grid=(nt,),
        in_specs=[pl.BlockSpec((tm, D), lambda i: (i, 0)),
                  pl.BlockSpec((1, 6, D), lambda i: (i // tiles_per_mod, 0, 0)),
                  pl.BlockSpec((tm, D // 2), lambda i: (i, 0)),
                  pl.BlockSpec((tm, D // 2), lambda i: (i + nt, 0)),
                  pl.BlockSpec((tm, V7X_LANES), lambda i: (i, 0)),
                  _const_spec((1, D))],
        out_specs=pl.BlockSpec((tm, D), lambda i: (i, 0)),
        out_shape=jax.ShapeDtypeStruct((n, D), F32),
        compiler_params=_params(("parallel",)),
        name="final",
    )(x1, mod, yg, yg, wts, g_final)


def _slot_plan(route, cnt, n, tme):
    counts = cnt[:, 0]
    padded = ((counts + tme - 1) // tme) * tme
    pend = jnp.cumsum(padded)
    pstart = pend - padded
    onehot = route[0:2, :, None] == jnp.arange(N_EXPERTS, dtype=jnp.int32)[None, None, :]
    dest = jnp.sum(jnp.where(onehot, pstart[None, None, :], 0), axis=-1) + route[2:4]
    n_slots = ((2 * n + N_EXPERTS * (tme - 1)) // tme) * tme
    tile_start = jnp.arange(n_slots // tme, dtype=jnp.int32) * tme
    tile_e = jnp.sum((tile_start[:, None] >= pend[None, :]).astype(jnp.int32), axis=1)
    tile_e = jnp.minimum(tile_e, N_EXPERTS - 1)
    n_used = (pend[-1] // tme).astype(jnp.int32).reshape(1)
    used = counts > 0
    run_of_e = jnp.cumsum(used.astype(jnp.int32)) - 1
    eids = jnp.arange(N_EXPERTS, dtype=jnp.int32)
    run_e = jnp.sum(jnp.where(used[None, :] & (run_of_e[None, :] == eids[:, None]), eids[None, :], 0), axis=1)
    tile_run = jnp.sum(jnp.where(tile_e[:, None] == eids[None, :], run_of_e[None, :], 0), axis=1)
    n_runs = jnp.sum(used.astype(jnp.int32)).reshape(1)
    plan = (tile_run.astype(jnp.int32), run_e.astype(jnp.int32), n_runs, n_used)
    return dest.astype(jnp.int32), plan, n_slots


def _group(x, mod, mod_per_seq, h0, p, row_len, tme, cast_ws):
    n_b, seq_len, _ = x.shape
    n = n_b * seq_len
    xt = x.reshape(n, D)
    mod_seq = mod if mod_per_seq else jnp.broadcast_to(mod, (n_b, 6, D))
    xc = _xr_conv(x, mod_seq, p["g1"], p["w_in"], p["rnn_conv_w"], p["rnn_conv_b"])
    hf, hb, last = _rglru_scan(xc, h0, p["wg"], p["ba"], p["bx"], p["lam"])
    tiles_per_mod = (seq_len // MIX_TM) if mod_per_seq else (n // MIX_TM)
    outs = _mixer(xt, mod, tiles_per_mod, hf.reshape(n, D), hb.reshape(n, D),
                                     p["g1"], p["g2"], p["w_in"], p["conv_w"], p["wco"], p["wro"],
                                     p["wo"], p["wr"], p["br"], row_len, cast_ws)
    x1, h2, route, wts, cnt = outs[:5]
    casts = outs[5:]
    dest, plan, n_slots = _slot_plan(route, cnt, n, tme)
    xs = _sc_dispatch(h2, dest, n_slots)

    def finish(w1b, w3b, w2b):
        ys = _experts(*plan, xs, w1b, w3b, w2b, tme)
        yg = _sc_collect(ys, dest)
        tiles_per_mod_f = (seq_len // FIN_TM) if mod_per_seq else (n // FIN_TM)
        y = _final(x1, mod, tiles_per_mod_f, yg, wts, p["g_final"])
        return y.reshape(n_b, seq_len, D)

    return finish, last, casts


def kernel(x_prompt, x_sample, state_rnn, c, c_ctx, w_ada, b_ada, g_norm1, g_norm2, w_in, conv_w, w_conv_out, rnn_conv_w, rnn_conv_b, w_gate_a, b_gate_a, w_gate_x, b_gate_x, lam, w_rnn_out, w_o, w_router_group, b_router_group, w_router_expert, b_router_expert, w1, w3, w2, g_final):
    assert w_ada.shape[0] == 1, "single layer"
    n_pb, n_sb = x_prompt.shape[0], x_sample.shape[0]

    cond = jnp.concatenate([c_ctx[None, :], c, jnp.zeros((16 - 1 - n_sb, D), F32)], axis=0)
    mod = _ada(cond, w_ada[0], b_ada[0]).reshape(16, 6, D)

    w_in_b = w_in[0].astype(BF16)
    pad_w = jnp.zeros((D, EPG - N_GROUPS), F32)
    wr = jnp.concatenate([w_router_group[0], pad_w, w_router_expert[0],
                          jnp.zeros((D, ROUTE_PAD - EPG - N_EXPERTS), F32)], axis=1)
    br = jnp.concatenate([b_router_group[0], jnp.full((EPG - N_GROUPS,), NEG_BIG, F32),
                          b_router_expert[0],
                          jnp.zeros((ROUTE_PAD - EPG - N_EXPERTS,), F32)]).reshape(1, ROUTE_PAD)
    wr_hi = wr.astype(BF16)
    p = dict(
        g1=g_norm1, g2=g_norm2, w_in=w_in_b,
        conv_w=conv_w[0], rnn_conv_w=rnn_conv_w[0], rnn_conv_b=rnn_conv_b,
        wg=(0.5 * jnp.concatenate([w_gate_a[0], w_gate_x[0]], axis=-1)).astype(BF16),
        ba=b_gate_a[0], bx=b_gate_x[0], lam=lam[0],
        wco=w_conv_out[0].astype(BF16), wro=w_rnn_out[0].astype(BF16), wo=w_o[0].astype(BF16),
        wr=jnp.concatenate([wr_hi, (wr - wr_hi.astype(F32)).astype(BF16)], axis=1), br=br,
        g_final=g_final.reshape(1, D),
    )

    h0_p = jnp.zeros((n_pb // SCAN_B, 2, SCAN_B, D), F32)
    finish_p, last, (w2b,) = _group(x_prompt, mod[0:1], False, h0_p, p, x_prompt.shape[1], 512,
                                    [w2[0]])
    state_new = last.transpose(0, 2, 1, 3).reshape(n_pb, 1, 2, D)

    h0_s = state_rnn[:, 0].reshape(n_sb // SCAN_B, SCAN_B, 2, D).transpose(0, 2, 1, 3)
    finish_s, _, (w1b, w3b) = _group(x_sample, mod[1:1 + n_sb], True, h0_s, p, GRID_W, 512,
                                     [w1[0], w3[0]])
    return (finish_p(w1b, w3b, w2b), finish_s(w1b, w3b, w2b), state_new)
```

```python
import functools

import jax
import jax.numpy as jnp
from jax import lax
from jax.experimental import pallas as pl
from jax.experimental.pallas import tpu as pltpu
from jax.experimental.pallas import tpu_sc as plsc

D = 1024
N_HEADS = 8
HEAD = D // N_HEADS
GRID_W = 64
RG_C = 8.0
N_GROUPS = 4
EPG = 8
N_EXPERTS = N_GROUPS * EPG
D_EXPERT = 512
EPS = 1e-6
F32 = jnp.float32
BF16 = jnp.bfloat16

V7X_LANES = 128
V7X_SUBLANES = 8
V7X_VMEM_LIMIT_BYTES = 56 * 1024 * 1024
V7X_SC_CORES = 2
V7X_SC_SUBCORES = 16
V7X_SC_WORKERS = V7X_SC_CORES * V7X_SC_SUBCORES
SC_WINDOW = 64

XR_T = 256
XR_SUB = 32
XR_LEFT = 2
XR_COL = 3
SCAN_T = 128
SCAN_B = V7X_SUBLANES
SCAN_UNROLL = 8
LOG2_E = 1.4426950408889634
TINY = 1e-30
MIX_TM = 512
FIN_TM = 1024
EXPERT_ROW_BUFS = 3
EXPERT_W_BUFS = 3
ROUTE_PAD = 128
NEG_BIG = -1e30


def _sigmoid(x):
    return 0.5 * jnp.tanh(0.5 * x) + 0.5


def _norm_mod(x, g, scale, shift):
    ms = jnp.mean(x * x, axis=-1, keepdims=True)
    return (x * lax.rsqrt(ms + EPS)) * (g * (1.0 + scale)) + shift


def _dot(a, b):
    return jnp.dot(a, b, preferred_element_type=F32)


def _params(sem, vmem=V7X_VMEM_LIMIT_BYTES):
    return pltpu.CompilerParams(dimension_semantics=sem, vmem_limit_bytes=vmem)


def _const_spec(shape):
    zeros = (0,) * len(shape)
    return pl.BlockSpec(shape, lambda *_: zeros, pipeline_mode=pl.Buffered(1))


def _ada_body(c_ref, w_ref, b_ref, o_ref):
    c = c_ref[...]
    s = (c * _sigmoid(c)).astype(BF16)
    o_ref[...] = _dot(s, w_ref[...].astype(BF16)) + b_ref[...]


def _ada(cond, w_ada, b_ada):
    rows = cond.shape[0]
    n_out = w_ada.shape[1]
    return pl.pallas_call(
        _ada_body,
        grid=(n_out // D,),
        in_specs=[pl.BlockSpec((rows, D), lambda i: (0, 0)),
                  pl.BlockSpec((D, D), lambda i: (0, i)),
                  pl.BlockSpec((1, D), lambda i: (0, i))],
        out_specs=pl.BlockSpec((rows, D), lambda i: (0, i)),
        out_shape=jax.ShapeDtypeStruct((rows, n_out), F32),
        compiler_params=_params(("parallel",)),
        name="ada",
    )(cond, w_ada, b_ada.reshape(1, n_out))


def _xr_body(x_ref, xn_ref, mod_ref, g_ref, w_ref, cw_ref, cb_ref, o_ref, xt):
    j = pl.program_id(1)
    n_t = pl.num_programs(1)
    t_len = x_ref.shape[1]
    nb = SCAN_B
    body0 = XR_LEFT * nb

    @pl.when(j == 0)
    def _():
        xt[:, 0:body0, :] = jnp.zeros((N_HEADS, body0, HEAD), F32)

    @pl.when(j > 0)
    def _():
        xt[:, 0:body0, :] = xt[:, t_len * nb:t_len * nb + body0, :]

    g = g_ref[...]
    for b in range(nb):
        h = _norm_mod(x_ref[b], g, mod_ref[b, 1:2, :], mod_ref[b, 0:1, :]).astype(BF16)
        r = _dot(h, w_ref[...])
        for s in range(N_HEADS):
            xt[s, pl.ds(body0 + b, t_len, stride=nb), :] = r[:, s * HEAD:(s + 1) * HEAD]

    hn = _norm_mod(xn_ref[:, 0, :], g, mod_ref[:, 1, :], mod_ref[:, 0, :]).astype(BF16)
    rn = jnp.where(j < n_t - 1, _dot(hn, w_ref[...]), 0.0)
    tail = body0 + t_len * nb
    for s in range(N_HEADS):
        xt[s, tail:tail + nb, :] = rn[:, s * HEAD:(s + 1) * HEAD]

    sub = XR_SUB

    def conv(s, t0):
        sl = slice(s * HEAD, (s + 1) * HEAD)
        y = cb_ref[:, sl]
        for k in range(4):
            r0 = (t0 + k) * nb
            y = y + xt[s, r0:r0 + sub * nb, :] * cw_ref[k:k + 1, sl]
        return y

    half_slabs = N_HEADS // 2
    for s in range(half_slabs):
        for t0 in range(0, t_len, sub):
            y = jnp.concatenate([conv(s, t0), conv(s + half_slabs, t0)], axis=1)
            o_ref[t0:t0 + sub, :, s * HEAD:(s + 1) * HEAD] = _pack_halves(y).reshape(sub, nb, HEAD)


def _xr_conv(x, mod_seq, g1, w_in, cw, cb):
    n_b, seq_len, _ = x.shape
    t_len = min(XR_T, seq_len)
    assert n_b % SCAN_B == 0 and seq_len % t_len == 0 and t_len % XR_SUB == 0
    n_t = seq_len // t_len
    nxt = t_len // V7X_SUBLANES
    last_blk = seq_len // V7X_SUBLANES - 1
    return pl.pallas_call(
        _xr_body,
        grid=(n_b // SCAN_B, n_t),
        in_specs=[pl.BlockSpec((SCAN_B, t_len, D), lambda g, j: (g, j, 0)),
                  pl.BlockSpec((SCAN_B, V7X_SUBLANES, D),
                               lambda g, j: (g, jnp.minimum((j + 1) * nxt, last_blk), 0)),
                  pl.BlockSpec((SCAN_B, 6, D), lambda g, j: (g, 0, 0)),
                  _const_spec((1, D)),
                  pl.BlockSpec((D, D), lambda g, j: (0, XR_COL), pipeline_mode=pl.Buffered(1)),
                  _const_spec((4, D)),
                  _const_spec((1, D))],
        out_specs=pl.BlockSpec((t_len, SCAN_B, D // 2), lambda g, j: (j, g, 0)),
        out_shape=jax.ShapeDtypeStruct((seq_len, n_b, D // 2), jnp.int32),
        scratch_shapes=[pltpu.VMEM((N_HEADS, (t_len + XR_LEFT + 1) * SCAN_B, HEAD), F32)],
        compiler_params=_params(("parallel", "arbitrary")),
        name="xr_conv",
    )(x, x, mod_seq, g1, w_in, cw, cb)


def _scan_body(xf_ref, xb_ref, h0_ref, wg_ref, ba_ref, bx_ref, lam_ref,
               hf_ref, hb_ref, last_ref, a_s, u_s, h_s, hc):
    j = pl.program_id(1)
    n_t = pl.num_programs(1)
    t_len = xf_ref.shape[0]
    rows = SCAN_B * t_len

    @pl.when(j == 0)
    def _():
        for d in range(2):
            for s in range(N_HEADS):
                hc[d, s] = h0_ref[0, d, :, s * HEAD:(s + 1) * HEAD]

    for d, x_ref in ((0, xf_ref), (1, xb_ref)):
        z = -lam_ref[d:d + 1, :]
        sp = jnp.maximum(z, 0.0) + jnp.log(1.0 + jnp.exp(-jnp.abs(z)))
        c2 = (-0.5 * RG_C * LOG2_E) * sp
        for hd in range(N_HEADS):
            sl = slice(hd * HEAD, (hd + 1) * HEAD)
            word = lax.bitcast_convert_type(
                x_ref[:, :, (hd % (N_HEADS // 2)) * HEAD:(hd % (N_HEADS // 2) + 1) * HEAD], jnp.uint32)
            bits = (word << 16) if hd < N_HEADS // 2 else (word & jnp.uint32(0xFFFF0000))
            xh = lax.bitcast_convert_type(bits, F32).reshape(rows, HEAD)
            g = _dot(xh.astype(BF16), wg_ref[d, hd])
            t_r = jnp.tanh(g[:, :HEAD] + 0.5 * ba_ref[d:d + 1, sl])
            t_i = jnp.tanh(g[:, HEAD:] + 0.5 * bx_ref[d:d + 1, sl])
            a = jnp.exp2(c2[:, sl] * t_r + c2[:, sl])
            q = 0.25 - 0.25 * (a * a)
            half_mult = q * lax.rsqrt(jnp.maximum(q, TINY))
            u = half_mult * ((t_i + 1.0) * xh)
            a_s[d, hd] = a
            u_s[d, hd] = u

    def step(tb, carry):
        h_f, h_b = carry
        for k in range(SCAN_UNROLL):
            t = tb * SCAN_UNROLL + k
            rf = pl.multiple_of(t * SCAN_B, SCAN_B)
            rb = pl.multiple_of((t_len - 1 - t) * SCAN_B, SCAN_B)
            h_f = a_s[0, :, pl.ds(rf, SCAN_B), :] * h_f + u_s[0, :, pl.ds(rf, SCAN_B), :]
            h_b = a_s[1, :, pl.ds(rb, SCAN_B), :] * h_b + u_s[1, :, pl.ds(rb, SCAN_B), :]
            h_s[0, :, pl.ds(rf, SCAN_B), :] = h_f
            h_s[1, :, pl.ds(rb, SCAN_B), :] = h_b
        return h_f, h_b

    h_f, h_b = lax.fori_loop(0, t_len // SCAN_UNROLL, step, (hc[0], hc[1]))
    hc[0] = h_f
    hc[1] = h_b

    for d, o_ref in ((0, hf_ref), (1, hb_ref)):
        for hd in range(N_HEADS):
            for b in range(SCAN_B):
                o_ref[b, :, hd * HEAD:(hd + 1) * HEAD] = (
                    h_s[d, hd, pl.ds(b, t_len, stride=SCAN_B), :].astype(BF16))

    @pl.when(j == n_t - 1)
    def _():
        for d in range(2):
            for s in range(N_HEADS):
                last_ref[0, d, :, s * HEAD:(s + 1) * HEAD] = hc[d, s]


def _rglru_scan(xc, h0, wg, ba, bx, lam):
    seq_len, n_b, _ = xc.shape
    n_g = n_b // SCAN_B
    n_t = seq_len // SCAN_T
    blk = (SCAN_B, SCAN_T, D)
    blk_in = (SCAN_T, SCAN_B, D // 2)
    return pl.pallas_call(
        _scan_body,
        grid=(n_g, n_t),
        in_specs=[pl.BlockSpec(blk_in, lambda g, j: (j, g, 0)),
                  pl.BlockSpec(blk_in, lambda g, j: (n_t - 1 - j, g, 0)),
                  pl.BlockSpec((1, 2, SCAN_B, D), lambda g, j: (g, 0, 0, 0)),
                  _const_spec((2, N_HEADS, HEAD, 2 * HEAD)),
                  _const_spec((2, D)),
                  _const_spec((2, D)),
                  _const_spec((2, D))],
        out_specs=[pl.BlockSpec(blk, lambda g, j: (g, j, 0)),
                   pl.BlockSpec(blk, lambda g, j: (g, n_t - 1 - j, 0)),
                   pl.BlockSpec((1, 2, SCAN_B, D), lambda g, j: (g, 0, 0, 0))],
        out_shape=[jax.ShapeDtypeStruct((n_b, seq_len, D), BF16),
                   jax.ShapeDtypeStruct((n_b, seq_len, D), BF16),
                   jax.ShapeDtypeStruct((n_g, 2, SCAN_B, D), F32)],
        scratch_shapes=[pltpu.VMEM((2, N_HEADS, SCAN_T * SCAN_B, HEAD), F32),
                        pltpu.VMEM((2, N_HEADS, SCAN_T * SCAN_B, HEAD), F32),
                        pltpu.VMEM((2, N_HEADS, SCAN_T * SCAN_B, HEAD), F32),
                        pltpu.VMEM((2, N_HEADS, SCAN_B, HEAD), F32)],
        compiler_params=_params(("parallel", "arbitrary")),
        name="rglru_scan",
    )(xc, xc, h0, wg, ba, bx, lam)


def _pack_halves(v):
    half = v.shape[1] // 2
    lo = lax.bitcast_convert_type(v[:, :half].astype(BF16).astype(F32), jnp.uint32)
    hi = lax.bitcast_convert_type(v[:, half:].astype(BF16).astype(F32), jnp.uint32)
    return lax.bitcast_convert_type((lo >> 16) | (hi & jnp.uint32(0xFFFF0000)), jnp.int32)


def _unpack_halves(p):
    u = lax.bitcast_convert_type(p, jnp.uint32)
    lo = lax.bitcast_convert_type(u << 16, F32)
    hi = lax.bitcast_convert_type(u & jnp.uint32(0xFFFF0000), F32)
    return lo, hi


def _mix_body(x_ref, mod_ref, hf_ref, hb_ref, g1_ref, g2_ref, win_ref, cw_ref,
              wco_ref, wro_ref, wo_ref, wr_ref, br_ref, *rest, row_len, n_cast):
    cast_in = rest[:n_cast]
    x1_ref, h2_ref, route_ref, wts_ref, cnt_ref = rest[n_cast:n_cast + 5]
    cast_out = rest[n_cast + 5:2 * n_cast + 5]
    seen = rest[2 * n_cast + 5]
    x = x_ref[...]
    tm = x.shape[0]
    h = _norm_mod(x, g1_ref[...], mod_ref[0, 1:2, :], mod_ref[0, 0:1, :]).astype(BF16)

    def proj(k):
        return _dot(h, win_ref[:, k * D:(k + 1) * D])

    cv = proj(1) * proj(2)
    pos = lax.broadcasted_iota(jnp.int32, (tm, 1), 0) % row_len
    conv = cv * cw_ref[1:2, :]
    conv = conv + jnp.where(pos >= 1, pltpu.roll(cv, 1, 0), 0.0) * cw_ref[0:1, :]
    conv = conv + jnp.where(pos <= row_len - 2, pltpu.roll(cv, tm - 1, 0), 0.0) * cw_ref[2:3, :]
    y_a = _dot((proj(0) * conv).astype(BF16), wco_ref[...])
    merged = _sigmoid(proj(5)) * y_a

    hs = hf_ref[...].astype(F32) + hb_ref[...].astype(F32)
    y_b = _dot((hs * jax.nn.gelu(proj(4))).astype(BF16), wro_ref[...])
    merged = merged + _sigmoid(proj(6)) * y_b

    mix = _dot(merged.astype(BF16), wo_ref[...])
    x1 = x + mod_ref[0, 2:3, :] * mix
    x1_ref[...] = x1.astype(BF16)
    for src, dst in zip(cast_in, cast_out):
        dst[...] = src[...].astype(BF16)
    h2 = _norm_mod(x1, g2_ref[...], mod_ref[0, 4:5, :], mod_ref[0, 3:4, :])
    h2_ref[...] = _pack_halves(h2)

    h2_hi = h2.astype(BF16)
    h2_lo = (h2 - h2_hi.astype(F32)).astype(BF16)
    big = _dot(h2_hi, wr_ref[...])
    logits = (big[:, :ROUTE_PAD] + big[:, ROUTE_PAD:]
              + _dot(h2_lo, wr_ref[:, :ROUTE_PAD]) + br_ref[...])
    lt = logits.T
    row = lax.broadcasted_iota(jnp.int32, (EPG, tm), 0)
    lg = lt[0:EPG]
    mg = jnp.max(lg, axis=0, keepdims=True)
    p_grp = 1.0 / jnp.sum(jnp.exp(lg - mg), axis=0, keepdims=True)
    grp = jnp.min(jnp.where(lg == mg, row, EPG), axis=0, keepdims=True)
    le = lt[EPG * N_GROUPS:EPG * (N_GROUPS + 1)]
    for g in range(N_GROUPS - 2, -1, -1):
        le = jnp.where(grp == g, lt[EPG * (g + 1):EPG * (g + 2)], le)
    me = jnp.max(le, axis=0, keepdims=True)
    ee = jnp.exp(le - me)
    pe = ee / jnp.sum(ee, axis=0, keepdims=True)
    p1 = jnp.max(pe, axis=0, keepdims=True)
    i1 = jnp.min(jnp.where(pe == p1, row, EPG), axis=0, keepdims=True)
    pe2 = jnp.where(row == i1, -1.0, pe)
    p2 = jnp.max(pe2, axis=0, keepdims=True)
    i2 = jnp.min(jnp.where(pe2 == p2, row, EPG), axis=0, keepdims=True)
    den = p1 + p2
    e1 = grp * EPG + i1
    e2 = grp * EPG + i2

    @pl.when(pl.program_id(0) == 0)
    def _():
        seen[...] = jnp.zeros_like(seen)

    erow = lax.broadcasted_iota(jnp.int32, (N_EXPERTS, tm), 0)
    hit1 = erow == e1
    hit2 = erow == e2
    both = jnp.where(jnp.logical_or(hit1, hit2), 1.0, 0.0)
    tri = jnp.where(lax.broadcasted_iota(jnp.int32, (tm, tm), 0)
                    <= lax.broadcasted_iota(jnp.int32, (tm, tm), 1), 1.0, 0.0).astype(BF16)
    before = _dot(both.astype(BF16), tri) - both + seen[...]
    r1 = jnp.sum(jnp.where(hit1, before, 0.0), axis=0, keepdims=True).astype(jnp.int32)
    r2 = jnp.sum(jnp.where(hit2, before, 0.0), axis=0, keepdims=True).astype(jnp.int32)
    total = seen[...] + jnp.sum(both, axis=1, keepdims=True)
    seen[...] = total
    cnt_ref[...] = total[:, :V7X_LANES].astype(jnp.int32)

    route_ref[...] = jnp.where(row == 0, e1, jnp.where(row == 1, e2, jnp.where(
        row == 2, r1, jnp.where(row == 3, r2, 0))))
    w8 = jnp.where(row == 0, p_grp * p1 / den, jnp.where(row == 1, p_grp * p2 / den, 0.0))
    wts_ref[...] = jnp.concatenate([w8, jnp.zeros((V7X_LANES - EPG, tm), F32)], axis=0).T


def _mixer(x, mod, tiles_per_mod, hf, hb, g1, g2, w_in, cw, wco, wro, wo, wr, br, row_len, cast_ws):
    n = x.shape[0]
    tm = MIX_TM
    assert tm % row_len == 0 and n % tm == 0
    steps = n // tm
    assert N_EXPERTS % steps == 0
    epb = N_EXPERTS // steps
    cast_specs = [pl.BlockSpec((epb,) + w.shape[1:], lambda i: (i, 0, 0)) for w in cast_ws]
    cast_shapes = [jax.ShapeDtypeStruct(w.shape, BF16) for w in cast_ws]
    mod_map = lambda i: (i // tiles_per_mod, 0, 0)
    tok = lambda i: (i, 0)
    col = lambda i: (0, i)
    return pl.pallas_call(
        functools.partial(_mix_body, row_len=row_len, n_cast=len(cast_ws)),
        grid=(steps,),
        in_specs=[pl.BlockSpec((tm, D), tok),
                  pl.BlockSpec((1, 6, D), mod_map),
                  pl.BlockSpec((tm, D), tok),
                  pl.BlockSpec((tm, D), tok),
                  _const_spec((1, D)),
                  _const_spec((1, D)),
                  _const_spec(w_in.shape),
                  _const_spec((3, D)),
                  _const_spec((D, D)),
                  _const_spec((D, D)),
                  _const_spec((D, D)),
                  _const_spec((D, 2 * ROUTE_PAD)),
                  _const_spec((1, ROUTE_PAD))] + cast_specs,
        out_specs=[pl.BlockSpec((tm, D), tok),
                   pl.BlockSpec((tm, D // 2), tok),
                   pl.BlockSpec((EPG, tm), col),
                   pl.BlockSpec((tm, V7X_LANES), tok),
                   pl.BlockSpec((N_EXPERTS, V7X_LANES), lambda i: (0, 0))] + cast_specs,
        out_shape=[jax.ShapeDtypeStruct((n, D), BF16),
                   jax.ShapeDtypeStruct((n, D // 2), jnp.int32),
                   jax.ShapeDtypeStruct((EPG, n), jnp.int32),
                   jax.ShapeDtypeStruct((n, V7X_LANES), F32),
                   jax.ShapeDtypeStruct((N_EXPERTS, V7X_LANES), jnp.int32)] + cast_shapes,
        scratch_shapes=[pltpu.VMEM((N_EXPERTS, tm), F32)],
        compiler_params=_params(("arbitrary",)),
        name="mixer",
    )(x, mod, hf, hb, g1, g2, w_in, cw, wco, wro, wo, wr, br, *cast_ws)


def _sc_mesh():
    return plsc.VectorSubcoreMesh(core_axis_name="c", subcore_axis_name="s",
                                  num_cores=V7X_SC_CORES, num_subcores=V7X_SC_SUBCORES)


def _sc_worker_id():
    return lax.axis_index("s") * V7X_SC_CORES + lax.axis_index("c")


def _sc_dispatch(rows, dest, n_slots):
    n, width = rows.shape
    per_w = n // V7X_SC_WORKERS
    n_ch = per_w // SC_WINDOW
    assert n_ch * SC_WINDOW * V7X_SC_WORKERS == n
    idx = dest.reshape(2, V7X_SC_WORKERS, n_ch, SC_WINDOW).transpose(1, 0, 2, 3)

    def body(x_hbm, d_hbm, o_hbm, idx_v, buf, ld_sem, st_sem):
        wid = _sc_worker_id()
        pltpu.sync_copy(d_hbm.at[wid], idx_v)

        def load(j):
            src = x_hbm.at[pl.ds(wid * per_w + j * SC_WINDOW, SC_WINDOW)]
            return pltpu.async_copy(src, buf.at[j % 2], ld_sem.at[j % 2])

        def scatter(j):
            return [pltpu.async_copy(buf.at[j % 2], o_hbm.at[idx_v.at[k, j]], st_sem.at[j % 2])
                    for k in range(2)]

        loads = {0: load(0)}
        stores = {}
        for j in range(n_ch):
            loads[j].wait()
            if j >= 1:
                for cp in stores[j - 1]:
                    cp.wait()
            if j + 1 < n_ch:
                loads[j + 1] = load(j + 1)
            stores[j] = scatter(j)
        for cp in stores[n_ch - 1]:
            cp.wait()

    return pl.kernel(
        body,
        out_type=jax.ShapeDtypeStruct((n_slots, width), jnp.int32),
        mesh=_sc_mesh(),
        scratch_types=[pltpu.VMEM((2, n_ch, SC_WINDOW), jnp.int32),
                       pltpu.VMEM((2, SC_WINDOW, width), jnp.int32),
                       pltpu.SemaphoreType.DMA((2,)),
                       pltpu.SemaphoreType.DMA((2,))],
        name="sc_dispatch",
    )(rows, idx)


def _sc_collect(rows, dest):
    n = dest.shape[1]
    width = rows.shape[1]
    per_w = n // V7X_SC_WORKERS
    n_ch = per_w // SC_WINDOW
    assert n_ch * SC_WINDOW * V7X_SC_WORKERS == n
    idx = dest.reshape(2, V7X_SC_WORKERS, n_ch, SC_WINDOW).transpose(1, 0, 2, 3)
    windows = [(k, j) for k in range(2) for j in range(n_ch)]

    def body(y_hbm, d_hbm, o_hbm, idx_v, buf, ld_sem, st_sem):
        wid = _sc_worker_id()
        pltpu.sync_copy(d_hbm.at[wid], idx_v)

        def gather(c):
            k, j = windows[c]
            return pltpu.async_copy(y_hbm.at[idx_v.at[k, j]], buf.at[c % 2], ld_sem.at[c % 2])

        def store(c):
            k, j = windows[c]
            dst = o_hbm.at[pl.ds(k * n + wid * per_w + j * SC_WINDOW, SC_WINDOW)]
            return pltpu.async_copy(buf.at[c % 2], dst, st_sem.at[c % 2])

        loads = {0: gather(0)}
        stores = {}
        for c in range(len(windows)):
            loads[c].wait()
            if c >= 1:
                stores[c - 1].wait()
            if c + 1 < len(windows):
                loads[c + 1] = gather(c + 1)
            stores[c] = store(c)
        stores[len(windows) - 1].wait()

    return pl.kernel(
        body,
        out_type=jax.ShapeDtypeStruct((2 * n, width), jnp.int32),
        mesh=_sc_mesh(),
        scratch_types=[pltpu.VMEM((2, n_ch, SC_WINDOW), jnp.int32),
                       pltpu.VMEM((2, SC_WINDOW, width), jnp.int32),
                       pltpu.SemaphoreType.DMA((2,)),
                       pltpu.SemaphoreType.DMA((2,))],
        name="sc_collect",
    )(rows, idx)


def _expert_body(tr_ref, re_ref, nr_ref, nu_ref, xs_hbm, w1_hbm, w3_hbm, w2_hbm, o_ref,
                 xbuf, w1s, w3s, w2s, xsem, sem):
    i = pl.program_id(0)
    n_used = nu_ref[0]
    tme = xbuf.shape[1]
    run = tr_ref[i]
    first = jnp.logical_or(i == 0, run != tr_ref[jnp.maximum(i - 1, 0)])
    slot = run % EXPERT_W_BUFS

    def row_copy(t):
        s = t % EXPERT_ROW_BUFS
        start = t * tme if isinstance(t, int) else pl.multiple_of(t * tme, tme)
        src = xs_hbm.at[pl.ds(start, tme)]
        return pltpu.make_async_copy(src, xbuf.at[s], xsem.at[s])

    def weight_copies(r):
        e = re_ref[r]
        s = r % EXPERT_W_BUFS
        return (pltpu.make_async_copy(w1_hbm.at[e], w1s.at[s], sem.at[0, s]),
                pltpu.make_async_copy(w3_hbm.at[e], w3s.at[s], sem.at[1, s]),
                pltpu.make_async_copy(w2_hbm.at[e], w2s.at[s], sem.at[2, s]))

    @pl.when(i == 0)
    def _():
        for t in range(EXPERT_ROW_BUFS - 1):
            @pl.when(t < n_used)
            def _():
                row_copy(t).start()
        for r in range(EXPERT_W_BUFS - 1):
            @pl.when(r < nr_ref[0])
            def _():
                for cp in weight_copies(r):
                    cp.start()

    @pl.when(first)
    def _():
        @pl.when(run + (EXPERT_W_BUFS - 1) < nr_ref[0])
        def _():
            for cp in weight_copies(run + (EXPERT_W_BUFS - 1)):
                cp.start()

        for cp in weight_copies(run):
            cp.wait()

    @pl.when(i < n_used)
    def _():
        @pl.when(i + (EXPERT_ROW_BUFS - 1) < n_used)
        def _():
            row_copy(i + (EXPERT_ROW_BUFS - 1)).start()

        row_copy(i).wait()
        lo, hi = _unpack_halves(xbuf[i % EXPERT_ROW_BUFS])
        lo = lo.astype(BF16)
        hi = hi.astype(BF16)
        half = D // 2
        a = _dot(lo, w1s[slot, 0:half, :]) + _dot(hi, w1s[slot, half:D, :])
        b = _dot(lo, w3s[slot, 0:half, :]) + _dot(hi, w3s[slot, half:D, :])
        z = (a * _sigmoid(a)) * b
        o_ref[...] = _pack_halves(_dot(z.astype(BF16), w2s[slot]))


def _experts(tile_run, run_e, n_runs, n_used, xs, w1, w3, w2, tme):
    n_slots = xs.shape[0]
    grid_spec = pltpu.PrefetchScalarGridSpec(
        num_scalar_prefetch=4,
        grid=(n_slots // tme,),
        in_specs=[pl.BlockSpec(memory_space=pl.ANY),
                  pl.BlockSpec(memory_space=pl.ANY),
                  pl.BlockSpec(memory_space=pl.ANY),
                  pl.BlockSpec(memory_space=pl.ANY)],
        out_specs=pl.BlockSpec((tme, D // 2),
                               lambda i, tr, re, nr, nu: (jnp.minimum(i, nu[0] - 1), 0)),
        scratch_shapes=[pltpu.VMEM((EXPERT_ROW_BUFS, tme, D // 2), jnp.int32),
                        pltpu.VMEM((EXPERT_W_BUFS, D, D_EXPERT), BF16),
                        pltpu.VMEM((EXPERT_W_BUFS, D, D_EXPERT), BF16),
                        pltpu.VMEM((EXPERT_W_BUFS, D_EXPERT, D), BF16),
                        pltpu.SemaphoreType.DMA((EXPERT_ROW_BUFS,)),
                        pltpu.SemaphoreType.DMA((3, EXPERT_W_BUFS))],
    )
    return pl.pallas_call(
        _expert_body,
        grid_spec=grid_spec,
        out_shape=jax.ShapeDtypeStruct((n_slots, D // 2), jnp.int32),
        compiler_params=_params(("arbitrary",)),
        name="experts",
    )(tile_run, run_e, n_runs, n_used, xs, w1, w3, w2)


def _final_body(x1_ref, mod_ref, y0_ref, y1_ref, wt_ref, gf_ref, o_ref):
    w0 = wt_ref[:, 0:1]
    w1 = wt_ref[:, 1:2]
    lo0, hi0 = _unpack_halves(y0_ref[...])
    lo1, hi1 = _unpack_halves(y1_ref[...])
    moe = jnp.concatenate([w0 * lo0 + w1 * lo1, w0 * hi0 + w1 * hi1], axis=1)
    x2 = x1_ref[...].astype(F32) + mod_ref[0, 5:6, :] * moe
    ms = jnp.mean(x2 * x2, axis=-1, keepdims=True)
    o_ref[...] = x2 * lax.rsqrt(ms + EPS) * gf_ref[...]


def _final(x1, mod, tiles_per_mod, yg, wts, g_final):
    n = x1.shape[0]
    tm = FIN_TM
    nt = n // tm
    return pl.pallas_call(
        _final_body,
        grid=(nt,),
        in_specs=[pl.BlockSpec((tm, D), lambda i: (i, 0)),
                  pl.BlockSpec((1, 6, D), lambda i: (i // tiles_per_mod, 0, 0)),
                  pl.BlockSpec((tm, D // 2), lambda i: (i, 0)),
                  pl.BlockSpec((tm, D // 2), lambda i: (i + nt, 0)),
                  pl.BlockSpec((tm, V7X_LANES), lambda i: (i, 0)),
                  _const_spec((1, D))],
        out_specs=pl.BlockSpec((tm, D), lambda i: (i, 0)),
        out_shape=jax.ShapeDtypeStruct((n, D), F32),
        compiler_params=_params(("parallel",)),
        name="final",
    )(x1, mod, yg, yg, wts, g_final)


def _slot_plan(route, cnt, n, tme):
    counts = cnt[:, 0]
    padded = ((counts + tme - 1) // tme) * tme
    pend = jnp.cumsum(padded)
    pstart = pend - padded
    onehot = route[0:2, :, None] == jnp.arange(N_EXPERTS, dtype=jnp.int32)[None, None, :]
    dest = jnp.sum(jnp.where(onehot, pstart[None, None, :], 0), axis=-1) + route[2:4]
    n_slots = ((2 * n + N_EXPERTS * (tme - 1)) // tme) * tme
    tile_start = jnp.arange(n_slots // tme, dtype=jnp.int32) * tme
    tile_e = jnp.sum((tile_start[:, None] >= pend[None, :]).astype(jnp.int32), axis=1)
    tile_e = jnp.minimum(tile_e, N_EXPERTS - 1)
    n_used = (pend[-1] // tme).astype(jnp.int32).reshape(1)
    used = counts > 0
    run_of_e = jnp.cumsum(used.astype(jnp.int32)) - 1
    eids = jnp.arange(N_EXPERTS, dtype=jnp.int32)
    run_e = jnp.sum(jnp.where(used[None, :] & (run_of_e[None, :] == eids[:, None]), eids[None, :], 0), axis=1)
    tile_run = jnp.sum(jnp.where(tile_e[:, None] == eids[None, :], run_of_e[None, :], 0), axis=1)
    n_runs = jnp.sum(used.astype(jnp.int32)).reshape(1)
    plan = (tile_run.astype(jnp.int32), run_e.astype(jnp.int32), n_runs, n_used)
    return dest.astype(jnp.int32), plan, n_slots


def _group(x, mod, mod_per_seq, h0, p, row_len, tme, cast_ws):
    n_b, seq_len, _ = x.shape
    n = n_b * seq_len
    xt = x.reshape(n, D)
    mod_seq = mod if mod_per_seq else jnp.broadcast_to(mod, (n_b, 6, D))
    xc = _xr_conv(x, mod_seq, p["g1"], p["w_in"], p["rnn_conv_w"], p["rnn_conv_b"])
    hf, hb, last = _rglru_scan(xc, h0, p["wg"], p["ba"], p["bx"], p["lam"])
    tiles_per_mod = (seq_len // MIX_TM) if mod_per_seq else (n // MIX_TM)
    outs = _mixer(xt, mod, tiles_per_mod, hf.reshape(n, D), hb.reshape(n, D),
                                     p["g1"], p["g2"], p["w_in"], p["conv_w"], p["wco"], p["wro"],
                                     p["wo"], p["wr"], p["br"], row_len, cast_ws)
    x1, h2, route, wts, cnt = outs[:5]
    casts = outs[5:]
    dest, plan, n_slots = _slot_plan(route, cnt, n, tme)
    xs = _sc_dispatch(h2, dest, n_slots)

    def finish(w1b, w3b, w2b):
        ys = _experts(*plan, xs, w1b, w3b, w2b, tme)
        yg = _sc_collect(ys, dest)
        tiles_per_mod_f = (seq_len // FIN_TM) if mod_per_seq else (n // FIN_TM)
        y = _final(x1, mod, tiles_per_mod_f, yg, wts, p["g_final"])
        return y.reshape(n_b, seq_len, D)

    return finish, last, casts


def kernel(x_prompt, x_sample, state_rnn, c, c_ctx, w_ada, b_ada, g_norm1, g_norm2, w_in, conv_w, w_conv_out, rnn_conv_w, rnn_conv_b, w_gate_a, b_gate_a, w_gate_x, b_gate_x, lam, w_rnn_out, w_o, w_router_group, b_router_group, w_router_expert, b_router_expert, w1, w3, w2, g_final):
    assert w_ada.shape[0] == 1, "single layer"
    n_pb, n_sb = x_prompt.shape[0], x_sample.shape[0]

    cond = jnp.concatenate([c_ctx[None, :], c, jnp.zeros((16 - 1 - n_sb, D), F32)], axis=0)
    mod = _ada(cond, w_ada[0], b_ada[0]).reshape(16, 6, D)

    w_in_b = w_in[0].astype(BF16)
    pad_w = jnp.zeros((D, EPG - N_GROUPS), F32)
    wr = jnp.concatenate([w_router_group[0], pad_w, w_router_expert[0],
                          jnp.zeros((D, ROUTE_PAD - EPG - N_EXPERTS), F32)], axis=1)
    br = jnp.concatenate([b_router_group[0], jnp.full((EPG - N_GROUPS,), NEG_BIG, F32),
                          b_router_expert[0],
                          jnp.zeros((ROUTE_PAD - EPG - N_EXPERTS,), F32)]).reshape(1, ROUTE_PAD)
    wr_hi = wr.astype(BF16)
    p = dict(
        g1=g_norm1, g2=g_norm2, w_in=w_in_b,
        conv_w=conv_w[0], rnn_conv_w=rnn_conv_w[0], rnn_conv_b=rnn_conv_b,
        wg=(0.5 * jnp.concatenate([w_gate_a[0], w_gate_x[0]], axis=-1)).astype(BF16),
        ba=b_gate_a[0], bx=b_gate_x[0], lam=lam[0],
        wco=w_conv_out[0].astype(BF16), wro=w_rnn_out[0].astype(BF16), wo=w_o[0].astype(BF16),
        wr=jnp.concatenate([wr_hi, (wr - wr_hi.astype(F32)).astype(BF16)], axis=1), br=br,
        g_final=g_final.reshape(1, D),
    )

    h0_p = jnp.zeros((n_pb // SCAN_B, 2, SCAN_B, D), F32)
    finish_p, last, (w2b,) = _group(x_prompt, mod[0:1], False, h0_p, p, x_prompt.shape[1], 512,
                                    [w2[0]])
    state_new = last.transpose(0, 2, 1, 3).reshape(n_pb, 1, 2, D)

    h0_s = state_rnn[:, 0].reshape(n_sb // SCAN_B, SCAN_B, 2, D).transpose(0, 2, 1, 3)
    finish_s, _, (w1b, w3b) = _group(x_sample, mod[1:1 + n_sb], True, h0_s, p, GRID_W, 512,
                                     [w1[0], w3[0]])
    return (finish_p(w1b, w3b, w2b), finish_s(w1b, w3b, w2b), state_new)
```

```python
import functools

import jax
import jax.numpy as jnp
from jax import lax
from jax.experimental import pallas as pl
from jax.experimental.pallas import tpu as pltpu
from jax.experimental.pallas import tpu_sc as plsc

D = 1024
N_HEADS = 8
HEAD = D // N_HEADS
GRID_W = 64
RG_C = 8.0
N_GROUPS = 4
EPG = 8
N_EXPERTS = N_GROUPS * EPG
D_EXPERT = 512
EPS = 1e-6
F32 = jnp.float32
BF16 = jnp.bfloat16

V7X_LANES = 128
V7X_SUBLANES = 8
V7X_VMEM_LIMIT_BYTES = 56 * 1024 * 1024
V7X_SC_CORES = 2
V7X_SC_SUBCORES = 16
V7X_SC_WORKERS = V7X_SC_CORES * V7X_SC_SUBCORES
SC_WINDOW = 64

XR_T = 256
XR_SUB = 32
XR_LEFT = 2
XR_COL = 3
SCAN_T = 128
SCAN_B = V7X_SUBLANES
SCAN_UNROLL = 8
LOG2_E = 1.4426950408889634
TINY = 1e-30
MIX_TM = 512
FIN_TM = 1024
EXPERT_ROW_BUFS = 3
EXPERT_W_BUFS = 3
ROUTE_PAD = 128
NEG_BIG = -1e30


def _sigmoid(x):
    return 0.5 * jnp.tanh(0.5 * x) + 0.5


def _norm_mod(x, g, scale, shift):
    ms = jnp.mean(x * x, axis=-1, keepdims=True)
    return (x * lax.rsqrt(ms + EPS)) * (g * (1.0 + scale)) + shift


def _dot(a, b):
    return jnp.dot(a, b, preferred_element_type=F32)


def _params(sem, vmem=V7X_VMEM_LIMIT_BYTES):
    return pltpu.CompilerParams(dimension_semantics=sem, vmem_limit_bytes=vmem)


def _const_spec(shape):
    zeros = (0,) * len(shape)
    return pl.BlockSpec(shape, lambda *_: zeros, pipeline_mode=pl.Buffered(1))


def _ada_body(c_ref, w_ref, b_ref, o_ref):
    c = c_ref[...]
    s = (c * _sigmoid(c)).astype(BF16)
    o_ref[...] = _dot(s, w_ref[...].astype(BF16)) + b_ref[...]


def _ada(cond, w_ada, b_ada):
    rows = cond.shape[0]
    n_out = w_ada.shape[1]
    return pl.pallas_call(
        _ada_body,
        grid=(n_out // D,),
        in_specs=[pl.BlockSpec((rows, D), lambda i: (0, 0)),
                  pl.BlockSpec((D, D), lambda i: (0, i)),
                  pl.BlockSpec((1, D), lambda i: (0, i))],
        out_specs=pl.BlockSpec((rows, D), lambda i: (0, i)),
        out_shape=jax.ShapeDtypeStruct((rows, n_out), F32),
        compiler_params=_params(("parallel",)),
        name="ada",
    )(cond, w_ada, b_ada.reshape(1, n_out))


def _xr_body(x_ref, xn_ref, mod_ref, g_ref, w_ref, cw_ref, cb_ref, o_ref, xt):
    j = pl.program_id(1)
    n_t = pl.num_programs(1)
    t_len = x_ref.shape[1]
    nb = SCAN_B
    body0 = XR_LEFT * nb

    @pl.when(j == 0)
    def _():
        xt[:, 0:body0, :] = jnp.zeros((N_HEADS, body0, HEAD), F32)

    @pl.when(j > 0)
    def _():
        xt[:, 0:body0, :] = xt[:, t_len * nb:t_len * nb + body0, :]

    g = g_ref[...]
    for b in range(nb):
        h = _norm_mod(x_ref[b], g, mod_ref[b, 1:2, :], mod_ref[b, 0:1, :]).astype(BF16)
        r = _dot(h, w_ref[...])
        for s in range(N_HEADS):
            xt[s, pl.ds(body0 + b, t_len, stride=nb), :] = r[:, s * HEAD:(s + 1) * HEAD]

    hn = _norm_mod(xn_ref[:, 0, :], g, mod_ref[:, 1, :], mod_ref[:, 0, :]).astype(BF16)
    rn = jnp.where(j < n_t - 1, _dot(hn, w_ref[...]), 0.0)
    tail = body0 + t_len * nb
    for s in range(N_HEADS):
        xt[s, tail:tail + nb, :] = rn[:, s * HEAD:(s + 1) * HEAD]

    sub = XR_SUB
    for s in range(N_HEADS):
        sl = slice(s * HEAD, (s + 1) * HEAD)
        for t0 in range(0, t_len, sub):
            y = cb_ref[:, sl]
            for k in range(4):
                r0 = (t0 + k) * nb
                y = y + xt[s, r0:r0 + sub * nb, :] * cw_ref[k:k + 1, sl]
            o_ref[t0:t0 + sub, :, sl] = y.reshape(sub, nb, HEAD)


def _xr_conv(x, mod_seq, g1, w_in, cw, cb):
    n_b, seq_len, _ = x.shape
    t_len = min(XR_T, seq_len)
    assert n_b % SCAN_B == 0 and seq_len % t_len == 0 and t_len % XR_SUB == 0
    n_t = seq_len // t_len
    nxt = t_len // V7X_SUBLANES
    last_blk = seq_len // V7X_SUBLANES - 1
    return pl.pallas_call(
        _xr_body,
        grid=(n_b // SCAN_B, n_t),
        in_specs=[pl.BlockSpec((SCAN_B, t_len, D), lambda g, j: (g, j, 0)),
                  pl.BlockSpec((SCAN_B, V7X_SUBLANES, D),
                               lambda g, j: (g, jnp.minimum((j + 1) * nxt, last_blk), 0)),
                  pl.BlockSpec((SCAN_B, 6, D), lambda g, j: (g, 0, 0)),
                  _const_spec((1, D)),
                  pl.BlockSpec((D, D), lambda g, j: (0, XR_COL), pipeline_mode=pl.Buffered(1)),
                  _const_spec((4, D)),
                  _const_spec((1, D))],
        out_specs=pl.BlockSpec((t_len, SCAN_B, D), lambda g, j: (j, g, 0)),
        out_shape=jax.ShapeDtypeStruct((seq_len, n_b, D), F32),
        scratch_shapes=[pltpu.VMEM((N_HEADS, (t_len + XR_LEFT + 1) * SCAN_B, HEAD), F32)],
        compiler_params=_params(("parallel", "arbitrary")),
        name="xr_conv",
    )(x, x, mod_seq, g1, w_in, cw, cb)


def _scan_body(xf_ref, xb_ref, h0_ref, wg_ref, ba_ref, bx_ref, lam_ref,
               hf_ref, hb_ref, last_ref, a_s, u_s, h_s, hc):
    j = pl.program_id(1)
    n_t = pl.num_programs(1)
    t_len = xf_ref.shape[0]
    rows = SCAN_B * t_len

    @pl.when(j == 0)
    def _():
        for d in range(2):
            for s in range(N_HEADS):
                hc[d, s] = h0_ref[0, d, :, s * HEAD:(s + 1) * HEAD]

    for d, x_ref in ((0, xf_ref), (1, xb_ref)):
        z = -lam_ref[d:d + 1, :]
        sp = jnp.maximum(z, 0.0) + jnp.log(1.0 + jnp.exp(-jnp.abs(z)))
        c2 = (-0.5 * RG_C * LOG2_E) * sp
        for hd in range(N_HEADS):
            sl = slice(hd * HEAD, (hd + 1) * HEAD)
            xh = x_ref[:, :, sl].reshape(rows, HEAD)
            g = _dot(xh.astype(BF16), wg_ref[d, hd])
            t_r = jnp.tanh(g[:, :HEAD] + 0.5 * ba_ref[d:d + 1, sl])
            t_i = jnp.tanh(g[:, HEAD:] + 0.5 * bx_ref[d:d + 1, sl])
            a = jnp.exp2(c2[:, sl] * t_r + c2[:, sl])
            q = 0.25 - 0.25 * (a * a)
            half_mult = q * lax.rsqrt(jnp.maximum(q, TINY))
            u = half_mult * ((t_i + 1.0) * xh)
            a_s[d, hd] = a
            u_s[d, hd] = u

    def step(tb, carry):
        h_f, h_b = carry
        for k in range(SCAN_UNROLL):
            t = tb * SCAN_UNROLL + k
            rf = pl.multiple_of(t * SCAN_B, SCAN_B)
            rb = pl.multiple_of((t_len - 1 - t) * SCAN_B, SCAN_B)
            h_f = a_s[0, :, pl.ds(rf, SCAN_B), :] * h_f + u_s[0, :, pl.ds(rf, SCAN_B), :]
            h_b = a_s[1, :, pl.ds(rb, SCAN_B), :] * h_b + u_s[1, :, pl.ds(rb, SCAN_B), :]
            h_s[0, :, pl.ds(rf, SCAN_B), :] = h_f
            h_s[1, :, pl.ds(rb, SCAN_B), :] = h_b
        return h_f, h_b

    h_f, h_b = lax.fori_loop(0, t_len // SCAN_UNROLL, step, (hc[0], hc[1]))
    hc[0] = h_f
    hc[1] = h_b

    for d, o_ref in ((0, hf_ref), (1, hb_ref)):
        for hd in range(N_HEADS):
            for b in range(SCAN_B):
                o_ref[b, :, hd * HEAD:(hd + 1) * HEAD] = (
                    h_s[d, hd, pl.ds(b, t_len, stride=SCAN_B), :].astype(BF16))

    @pl.when(j == n_t - 1)
    def _():
        for d in range(2):
            for s in range(N_HEADS):
                last_ref[0, d, :, s * HEAD:(s + 1) * HEAD] = hc[d, s]


def _rglru_scan(xc, h0, wg, ba, bx, lam):
    seq_len, n_b, _ = xc.shape
    n_g = n_b // SCAN_B
    n_t = seq_len // SCAN_T
    blk = (SCAN_B, SCAN_T, D)
    blk_in = (SCAN_T, SCAN_B, D)
    return pl.pallas_call(
        _scan_body,
        grid=(n_g, n_t),
        in_specs=[pl.BlockSpec(blk_in, lambda g, j: (j, g, 0)),
                  pl.BlockSpec(blk_in, lambda g, j: (n_t - 1 - j, g, 0)),
                  pl.BlockSpec((1, 2, SCAN_B, D), lambda g, j: (g, 0, 0, 0)),
                  _const_spec((2, N_HEADS, HEAD, 2 * HEAD)),
                  _const_spec((2, D)),
                  _const_spec((2, D)),
                  _const_spec((2, D))],
        out_specs=[pl.BlockSpec(blk, lambda g, j: (g, j, 0)),
                   pl.BlockSpec(blk, lambda g, j: (g, n_t - 1 - j, 0)),
                   pl.BlockSpec((1, 2, SCAN_B, D), lambda g, j: (g, 0, 0, 0))],
        out_shape=[jax.ShapeDtypeStruct((n_b, seq_len, D), BF16),
                   jax.ShapeDtypeStruct((n_b, seq_len, D), BF16),
                   jax.ShapeDtypeStruct((n_g, 2, SCAN_B, D), F32)],
        scratch_shapes=[pltpu.VMEM((2, N_HEADS, SCAN_T * SCAN_B, HEAD), F32),
                        pltpu.VMEM((2, N_HEADS, SCAN_T * SCAN_B, HEAD), F32),
                        pltpu.VMEM((2, N_HEADS, SCAN_T * SCAN_B, HEAD), F32),
                        pltpu.VMEM((2, N_HEADS, SCAN_B, HEAD), F32)],
        compiler_params=_params(("parallel", "arbitrary")),
        name="rglru_scan",
    )(xc, xc, h0, wg, ba, bx, lam)


def _pack_halves(v):
    half = v.shape[1] // 2
    lo = lax.bitcast_convert_type(v[:, :half].astype(BF16).astype(F32), jnp.uint32)
    hi = lax.bitcast_convert_type(v[:, half:].astype(BF16).astype(F32), jnp.uint32)
    return lax.bitcast_convert_type((lo >> 16) | (hi & jnp.uint32(0xFFFF0000)), jnp.int32)


def _unpack_halves(p):
    u = lax.bitcast_convert_type(p, jnp.uint32)
    lo = lax.bitcast_convert_type(u << 16, F32)
    hi = lax.bitcast_convert_type(u & jnp.uint32(0xFFFF0000), F32)
    return lo, hi


def _mix_body(x_ref, mod_ref, hf_ref, hb_ref, g1_ref, g2_ref, win_ref, cw_ref,
              wco_ref, wro_ref, wo_ref, wr_ref, br_ref, *rest, row_len, n_cast):
    cast_in = rest[:n_cast]
    x1_ref, h2_ref, route_ref, wts_ref, cnt_ref = rest[n_cast:n_cast + 5]
    cast_out = rest[n_cast + 5:2 * n_cast + 5]
    seen = rest[2 * n_cast + 5]
    x = x_ref[...]
    tm = x.shape[0]
    h = _norm_mod(x, g1_ref[...], mod_ref[0, 1:2, :], mod_ref[0, 0:1, :]).astype(BF16)

    def proj(k):
        return _dot(h, win_ref[:, k * D:(k + 1) * D])

    cv = proj(1) * proj(2)
    pos = lax.broadcasted_iota(jnp.int32, (tm, 1), 0) % row_len
    conv = cv * cw_ref[1:2, :]
    conv = conv + jnp.where(pos >= 1, pltpu.roll(cv, 1, 0), 0.0) * cw_ref[0:1, :]
    conv = conv + jnp.where(pos <= row_len - 2, pltpu.roll(cv, tm - 1, 0), 0.0) * cw_ref[2:3, :]
    y_a = _dot((proj(0) * conv).astype(BF16), wco_ref[...])
    merged = _sigmoid(proj(5)) * y_a

    hs = hf_ref[...].astype(F32) + hb_ref[...].astype(F32)
    y_b = _dot((hs * jax.nn.gelu(proj(4))).astype(BF16), wro_ref[...])
    merged = merged + _sigmoid(proj(6)) * y_b

    mix = _dot(merged.astype(BF16), wo_ref[...])
    x1 = x + mod_ref[0, 2:3, :] * mix
    x1_ref[...] = x1.astype(BF16)
    for src, dst in zip(cast_in, cast_out):
        dst[...] = src[...].astype(BF16)
    h2 = _norm_mod(x1, g2_ref[...], mod_ref[0, 4:5, :], mod_ref[0, 3:4, :])
    h2_ref[...] = _pack_halves(h2)

    h2_hi = h2.astype(BF16)
    h2_lo = (h2 - h2_hi.astype(F32)).astype(BF16)
    big = _dot(h2_hi, wr_ref[...])
    logits = (big[:, :ROUTE_PAD] + big[:, ROUTE_PAD:]
              + _dot(h2_lo, wr_ref[:, :ROUTE_PAD]) + br_ref[...])
    lt = logits.T
    row = lax.broadcasted_iota(jnp.int32, (EPG, tm), 0)
    lg = lt[0:EPG]
    mg = jnp.max(lg, axis=0, keepdims=True)
    p_grp = 1.0 / jnp.sum(jnp.exp(lg - mg), axis=0, keepdims=True)
    grp = jnp.min(jnp.where(lg == mg, row, EPG), axis=0, keepdims=True)
    le = lt[EPG * N_GROUPS:EPG * (N_GROUPS + 1)]
    for g in range(N_GROUPS - 2, -1, -1):
        le = jnp.where(grp == g, lt[EPG * (g + 1):EPG * (g + 2)], le)
    me = jnp.max(le, axis=0, keepdims=True)
    ee = jnp.exp(le - me)
    pe = ee / jnp.sum(ee, axis=0, keepdims=True)
    p1 = jnp.max(pe, axis=0, keepdims=True)
    i1 = jnp.min(jnp.where(pe == p1, row, EPG), axis=0, keepdims=True)
    pe2 = jnp.where(row == i1, -1.0, pe)
    p2 = jnp.max(pe2, axis=0, keepdims=True)
    i2 = jnp.min(jnp.where(pe2 == p2, row, EPG), axis=0, keepdims=True)
    den = p1 + p2
    e1 = grp * EPG + i1
    e2 = grp * EPG + i2

    @pl.when(pl.program_id(0) == 0)
    def _():
        seen[...] = jnp.zeros_like(seen)

    erow = lax.broadcasted_iota(jnp.int32, (N_EXPERTS, tm), 0)
    hit1 = erow == e1
    hit2 = erow == e2
    both = jnp.where(jnp.logical_or(hit1, hit2), 1.0, 0.0)
    tri = jnp.where(lax.broadcasted_iota(jnp.int32, (tm, tm), 0)
                    <= lax.broadcasted_iota(jnp.int32, (tm, tm), 1), 1.0, 0.0).astype(BF16)
    before = _dot(both.astype(BF16), tri) - both + seen[...]
    r1 = jnp.sum(jnp.where(hit1, before, 0.0), axis=0, keepdims=True).astype(jnp.int32)
    r2 = jnp.sum(jnp.where(hit2, before, 0.0), axis=0, keepdims=True).astype(jnp.int32)
    total = seen[...] + jnp.sum(both, axis=1, keepdims=True)
    seen[...] = total
    cnt_ref[...] = total[:, :V7X_LANES].astype(jnp.int32)

    route_ref[...] = jnp.where(row == 0, e1, jnp.where(row == 1, e2, jnp.where(
        row == 2, r1, jnp.where(row == 3, r2, 0))))
    w8 = jnp.where(row == 0, p_grp * p1 / den, jnp.where(row == 1, p_grp * p2 / den, 0.0))
    wts_ref[...] = jnp.concatenate([w8, jnp.zeros((V7X_LANES - EPG, tm), F32)], axis=0).T


def _mixer(x, mod, tiles_per_mod, hf, hb, g1, g2, w_in, cw, wco, wro, wo, wr, br, row_len, cast_ws):
    n = x.shape[0]
    tm = MIX_TM
    assert tm % row_len == 0 and n % tm == 0
    steps = n // tm
    assert N_EXPERTS % steps == 0
    epb = N_EXPERTS // steps
    cast_specs = [pl.BlockSpec((epb,) + w.shape[1:], lambda i: (i, 0, 0)) for w in cast_ws]
    cast_shapes = [jax.ShapeDtypeStruct(w.shape, BF16) for w in cast_ws]
    mod_map = lambda i: (i // tiles_per_mod, 0, 0)
    tok = lambda i: (i, 0)
    col = lambda i: (0, i)
    return pl.pallas_call(
        functools.partial(_mix_body, row_len=row_len, n_cast=len(cast_ws)),
        grid=(steps,),
        in_specs=[pl.BlockSpec((tm, D), tok),
                  pl.BlockSpec((1, 6, D), mod_map),
                  pl.BlockSpec((tm, D), tok),
                  pl.BlockSpec((tm, D), tok),
                  _const_spec((1, D)),
                  _const_spec((1, D)),
                  _const_spec(w_in.shape),
                  _const_spec((3, D)),
                  _const_spec((D, D)),
                  _const_spec((D, D)),
                  _const_spec((D, D)),
                  _const_spec((D, 2 * ROUTE_PAD)),
                  _const_spec((1, ROUTE_PAD))] + cast_specs,
        out_specs=[pl.BlockSpec((tm, D), tok),
                   pl.BlockSpec((tm, D // 2), tok),
                   pl.BlockSpec((EPG, tm), col),
                   pl.BlockSpec((tm, V7X_LANES), tok),
                   pl.BlockSpec((N_EXPERTS, V7X_LANES), lambda i: (0, 0))] + cast_specs,
        out_shape=[jax.ShapeDtypeStruct((n, D), BF16),
                   jax.ShapeDtypeStruct((n, D // 2), jnp.int32),
                   jax.ShapeDtypeStruct((EPG, n), jnp.int32),
                   jax.ShapeDtypeStruct((n, V7X_LANES), F32),
                   jax.ShapeDtypeStruct((N_EXPERTS, V7X_LANES), jnp.int32)] + cast_shapes,
        scratch_shapes=[pltpu.VMEM((N_EXPERTS, tm), F32)],
        compiler_params=_params(("arbitrary",)),
        name="mixer",
    )(x, mod, hf, hb, g1, g2, w_in, cw, wco, wro, wo, wr, br, *cast_ws)


def _sc_mesh():
    return plsc.VectorSubcoreMesh(core_axis_name="c", subcore_axis_name="s",
                                  num_cores=V7X_SC_CORES, num_subcores=V7X_SC_SUBCORES)


def _sc_worker_id():
    return lax.axis_index("s") * V7X_SC_CORES + lax.axis_index("c")


def _sc_dispatch(rows, dest, n_slots):
    n, width = rows.shape
    per_w = n // V7X_SC_WORKERS
    n_ch = per_w // SC_WINDOW
    assert n_ch * SC_WINDOW * V7X_SC_WORKERS == n
    idx = dest.reshape(2, V7X_SC_WORKERS, n_ch, SC_WINDOW).transpose(1, 0, 2, 3)

    def body(x_hbm, d_hbm, o_hbm, idx_v, buf, ld_sem, st_sem):
        wid = _sc_worker_id()
        pltpu.sync_copy(d_hbm.at[wid], idx_v)

        def load(j):
            src = x_hbm.at[pl.ds(wid * per_w + j * SC_WINDOW, SC_WINDOW)]
            return pltpu.async_copy(src, buf.at[j % 2], ld_sem.at[j % 2])

        def scatter(j):
            return [pltpu.async_copy(buf.at[j % 2], o_hbm.at[idx_v.at[k, j]], st_sem.at[j % 2])
                    for k in range(2)]

        loads = {0: load(0)}
        stores = {}
        for j in range(n_ch):
            loads[j].wait()
            if j >= 1:
                for cp in stores[j - 1]:
                    cp.wait()
            if j + 1 < n_ch:
                loads[j + 1] = load(j + 1)
            stores[j] = scatter(j)
        for cp in stores[n_ch - 1]:
            cp.wait()

    return pl.kernel(
        body,
        out_type=jax.ShapeDtypeStruct((n_slots, width), jnp.int32),
        mesh=_sc_mesh(),
        scratch_types=[pltpu.VMEM((2, n_ch, SC_WINDOW), jnp.int32),
                       pltpu.VMEM((2, SC_WINDOW, width), jnp.int32),
                       pltpu.SemaphoreType.DMA((2,)),
                       pltpu.SemaphoreType.DMA((2,))],
        name="sc_dispatch",
    )(rows, idx)


def _sc_collect(rows, dest):
    n = dest.shape[1]
    width = rows.shape[1]
    per_w = n // V7X_SC_WORKERS
    n_ch = per_w // SC_WINDOW
    assert n_ch * SC_WINDOW * V7X_SC_WORKERS == n
    idx = dest.reshape(2, V7X_SC_WORKERS, n_ch, SC_WINDOW).transpose(1, 0, 2, 3)
    windows = [(k, j) for k in range(2) for j in range(n_ch)]

    def body(y_hbm, d_hbm, o_hbm, idx_v, buf, ld_sem, st_sem):
        wid = _sc_worker_id()
        pltpu.sync_copy(d_hbm.at[wid], idx_v)

        def gather(c):
            k, j = windows[c]
            return pltpu.async_copy(y_hbm.at[idx_v.at[k, j]], buf.at[c % 2], ld_sem.at[c % 2])

        def store(c):
            k, j = windows[c]
            dst = o_hbm.at[pl.ds(k * n + wid * per_w + j * SC_WINDOW, SC_WINDOW)]
            return pltpu.async_copy(buf.at[c % 2], dst, st_sem.at[c % 2])

        loads = {0: gather(0)}
        stores = {}
        for c in range(len(windows)):
            loads[c].wait()
            if c >= 1:
                stores[c - 1].wait()
            if c + 1 < len(windows):
                loads[c + 1] = gather(c + 1)
            stores[c] = store(c)
        stores[len(windows) - 1].wait()

    return pl.kernel(
        body,
        out_type=jax.ShapeDtypeStruct((2 * n, width), jnp.int32),
        mesh=_sc_mesh(),
        scratch_types=[pltpu.VMEM((2, n_ch, SC_WINDOW), jnp.int32),
                       pltpu.VMEM((2, SC_WINDOW, width), jnp.int32),
                       pltpu.SemaphoreType.DMA((2,)),
                       pltpu.SemaphoreType.DMA((2,))],
        name="sc_collect",
    )(rows, idx)


def _expert_body(tr_ref, re_ref, nr_ref, nu_ref, xs_hbm, w1_hbm, w3_hbm, w2_hbm, o_ref,
                 xbuf, w1s, w3s, w2s, xsem, sem):
    i = pl.program_id(0)
    n_used = nu_ref[0]
    tme = xbuf.shape[1]
    run = tr_ref[i]
    first = jnp.logical_or(i == 0, run != tr_ref[jnp.maximum(i - 1, 0)])
    slot = run % EXPERT_W_BUFS

    def row_copy(t):
        s = t % EXPERT_ROW_BUFS
        start = t * tme if isinstance(t, int) else pl.multiple_of(t * tme, tme)
        src = xs_hbm.at[pl.ds(start, tme)]
        return pltpu.make_async_copy(src, xbuf.at[s], xsem.at[s])

    def weight_copies(r):
        e = re_ref[r]
        s = r % EXPERT_W_BUFS
        return (pltpu.make_async_copy(w1_hbm.at[e], w1s.at[s], sem.at[0, s]),
                pltpu.make_async_copy(w3_hbm.at[e], w3s.at[s], sem.at[1, s]),
                pltpu.make_async_copy(w2_hbm.at[e], w2s.at[s], sem.at[2, s]))

    @pl.when(i == 0)
    def _():
        for t in range(EXPERT_ROW_BUFS - 1):
            @pl.when(t < n_used)
            def _():
                row_copy(t).start()
        for r in range(EXPERT_W_BUFS - 1):
            @pl.when(r < nr_ref[0])
            def _():
                for cp in weight_copies(r):
                    cp.start()

    @pl.when(jnp.logical_and(first, i < n_used))
    def _():
        @pl.when(run + (EXPERT_W_BUFS - 1) < nr_ref[0])
        def _():
            for cp in weight_copies(run + (EXPERT_W_BUFS - 1)):
                cp.start()

        for cp in weight_copies(run):
            cp.wait()

    @pl.when(i < n_used)
    def _():
        @pl.when(i + (EXPERT_ROW_BUFS - 1) < n_used)
        def _():
            row_copy(i + (EXPERT_ROW_BUFS - 1)).start()

        row_copy(i).wait()
        lo, hi = _unpack_halves(xbuf[i % EXPERT_ROW_BUFS])
        lo = lo.astype(BF16)
        hi = hi.astype(BF16)
        half = D // 2
        a = _dot(lo, w1s[slot, 0:half, :]) + _dot(hi, w1s[slot, half:D, :])
        b = _dot(lo, w3s[slot, 0:half, :]) + _dot(hi, w3s[slot, half:D, :])
        z = (a * _sigmoid(a)) * b
        o_ref[...] = _pack_halves(_dot(z.astype(BF16), w2s[slot]))


def _experts(tile_run, run_e, n_runs, n_used, xs, w1, w3, w2, tme):
    n_slots = xs.shape[0]
    grid_spec = pltpu.PrefetchScalarGridSpec(
        num_scalar_prefetch=4,
        grid=(n_slots // tme,),
        in_specs=[pl.BlockSpec(memory_space=pl.ANY),
                  pl.BlockSpec(memory_space=pl.ANY),
                  pl.BlockSpec(memory_space=pl.ANY),
                  pl.BlockSpec(memory_space=pl.ANY)],
        out_specs=pl.BlockSpec((tme, D // 2),
                               lambda i, tr, re, nr, nu: (jnp.clip(i, 0, jnp.maximum(nu[0] - 1, 0)), 0)),
        scratch_shapes=[pltpu.VMEM((EXPERT_ROW_BUFS, tme, D // 2), jnp.int32),
                        pltpu.VMEM((EXPERT_W_BUFS, D, D_EXPERT), BF16),
                        pltpu.VMEM((EXPERT_W_BUFS, D, D_EXPERT), BF16),
                        pltpu.VMEM((EXPERT_W_BUFS, D_EXPERT, D), BF16),
                        pltpu.SemaphoreType.DMA((EXPERT_ROW_BUFS,)),
                        pltpu.SemaphoreType.DMA((3, EXPERT_W_BUFS))],
    )
    return pl.pallas_call(
        _expert_body,
        grid_spec=grid_spec,
        out_shape=jax.ShapeDtypeStruct((n_slots, D // 2), jnp.int32),
        compiler_params=_params(("arbitrary",)),
        name="experts",
    )(tile_run, run_e, n_runs, n_used, xs, w1, w3, w2)


def _final_body(x1_ref, mod_ref, y0_ref, y1_ref, wt_ref, gf_ref, o_ref):
    w0 = wt_ref[:, 0:1]
    w1 = wt_ref[:, 1:2]
    lo0, hi0 = _unpack_halves(y0_ref[...])
    lo1, hi1 = _unpack_halves(y1_ref[...])
    moe = jnp.concatenate([w0 * lo0 + w1 * lo1, w0 * hi0 + w1 * hi1], axis=1)
    x2 = x1_ref[...].astype(F32) + mod_ref[0, 5:6, :] * moe
    ms = jnp.mean(x2 * x2, axis=-1, keepdims=True)
    o_ref[...] = x2 * lax.rsqrt(ms + EPS) * gf_ref[...]


def _final(x1, mod, tiles_per_mod, yg, wts, g_final):
    n = x1.shape[0]
    tm = FIN_TM
    nt = n // tm
    return pl.pallas_call(
        _final_body,
        grid=(nt,),
        in_specs=[pl.BlockSpec((tm, D), lambda i: (i, 0)),
                  pl.BlockSpec((1, 6, D), lambda i: (i // tiles_per_mod, 0, 0)),
                  pl.BlockSpec((tm, D // 2), lambda i: (i, 0)),
                  pl.BlockSpec((tm, D // 2), lambda i: (i + nt, 0)),
                  pl.BlockSpec((tm, V7X_LANES), lambda i: (i, 0)),
                  _const_spec((1, D))],
        out_specs=pl.BlockSpec((tm, D), lambda i: (i, 0)),
        out_shape=jax.ShapeDtypeStruct((n, D), F32),
        compiler_params=_params(("parallel",)),
        name="final",
    )(x1, mod, yg, yg, wts, g_final)


def _slot_plan(route, cnt, n, tme):
    counts = cnt[:, 0]
    padded = ((counts + tme - 1) // tme) * tme
    pend = jnp.cumsum(padded)
    pstart = pend - padded
    onehot = route[0:2, :, None] == jnp.arange(N_EXPERTS, dtype=jnp.int32)[None, None, :]
    dest = jnp.sum(jnp.where(onehot, pstart[None, None, :], 0), axis=-1) + route[2:4]
    n_slots = ((2 * n + N_EXPERTS * (tme - 1)) // tme) * tme
    dest = jnp.clip(dest, 0, n_slots - 1)
    tile_start = jnp.arange(n_slots // tme, dtype=jnp.int32) * tme
    tile_e = jnp.sum((tile_start[:, None] >= pend[None, :]).astype(jnp.int32), axis=1)
    tile_e = jnp.minimum(tile_e, N_EXPERTS - 1)
    n_used = (pend[-1] // tme).astype(jnp.int32).reshape(1)
    used = counts > 0
    run_of_e = jnp.cumsum(used.astype(jnp.int32)) - 1
    eids = jnp.arange(N_EXPERTS, dtype=jnp.int32)
    run_e = jnp.sum(jnp.where(used[None, :] & (run_of_e[None, :] == eids[:, None]), eids[None, :], 0), axis=1)
    tile_run = jnp.sum(jnp.where(tile_e[:, None] == eids[None, :], run_of_e[None, :], 0), axis=1)
    n_runs = jnp.sum(used.astype(jnp.int32)).reshape(1)
    plan = (tile_run.astype(jnp.int32), run_e.astype(jnp.int32), n_runs, n_used)
    return dest.astype(jnp.int32), plan, n_slots


def _group(x, mod, mod_per_seq, h0, p, row_len, tme, cast_ws):
    n_b, seq_len, _ = x.shape
    n = n_b * seq_len
    xt = x.reshape(n, D)
    mod_seq = mod if mod_per_seq else jnp.broadcast_to(mod, (n_b, 6, D))
    xc = _xr_conv(x, mod_seq, p["g1"], p["w_in"], p["rnn_conv_w"], p["rnn_conv_b"])
    hf, hb, last = _rglru_scan(xc, h0, p["wg"], p["ba"], p["bx"], p["lam"])
    tiles_per_mod = (seq_len // MIX_TM) if mod_per_seq else (n // MIX_TM)
    outs = _mixer(xt, mod, tiles_per_mod, hf.reshape(n, D), hb.reshape(n, D),
                                     p["g1"], p["g2"], p["w_in"], p["conv_w"], p["wco"], p["wro"],
                                     p["wo"], p["wr"], p["br"], row_len, cast_ws)
    x1, h2, route, wts, cnt = outs[:5]
    casts = outs[5:]
    dest, plan, n_slots = _slot_plan(route, cnt, n, tme)
    xs = _sc_dispatch(h2, dest, n_slots)

    def finish(w1b, w3b, w2b):
        ys = _experts(*plan, xs, w1b, w3b, w2b, tme)
        yg = _sc_collect(ys, dest)
        tiles_per_mod_f = (seq_len // FIN_TM) if mod_per_seq else (n // FIN_TM)
        y = _final(x1, mod, tiles_per_mod_f, yg, wts, p["g_final"])
        return y.reshape(n_b, seq_len, D)

    return finish, last, casts


def kernel(x_prompt, x_sample, state_rnn, c, c_ctx, w_ada, b_ada, g_norm1, g_norm2, w_in, conv_w, w_conv_out, rnn_conv_w, rnn_conv_b, w_gate_a, b_gate_a, w_gate_x, b_gate_x, lam, w_rnn_out, w_o, w_router_group, b_router_group, w_router_expert, b_router_expert, w1, w3, w2, g_final):
    assert w_ada.shape[0] == 1, "single layer"
    n_pb, n_sb = x_prompt.shape[0], x_sample.shape[0]

    cond = jnp.concatenate([c_ctx[None, :], c, jnp.zeros((16 - 1 - n_sb, D), F32)], axis=0)
    mod = _ada(cond, w_ada[0], b_ada[0]).reshape(16, 6, D)

    w_in_b = w_in[0].astype(BF16)
    pad_w = jnp.zeros((D, EPG - N_GROUPS), F32)
    wr = jnp.concatenate([w_router_group[0], pad_w, w_router_expert[0],
                          jnp.zeros((D, ROUTE_PAD - EPG - N_EXPERTS), F32)], axis=1)
    br = jnp.concatenate([b_router_group[0], jnp.full((EPG - N_GROUPS,), NEG_BIG, F32),
                          b_router_expert[0],
                          jnp.zeros((ROUTE_PAD - EPG - N_EXPERTS,), F32)]).reshape(1, ROUTE_PAD)
    wr_hi = wr.astype(BF16)
    p = dict(
        g1=g_norm1, g2=g_norm2, w_in=w_in_b,
        conv_w=conv_w[0], rnn_conv_w=rnn_conv_w[0], rnn_conv_b=rnn_conv_b,
        wg=(0.5 * jnp.concatenate([w_gate_a[0], w_gate_x[0]], axis=-1)).astype(BF16),
        ba=b_gate_a[0], bx=b_gate_x[0], lam=lam[0],
        wco=w_conv_out[0].astype(BF16), wro=w_rnn_out[0].astype(BF16), wo=w_o[0].astype(BF16),
        wr=jnp.concatenate([wr_hi, (wr - wr_hi.astype(F32)).astype(BF16)], axis=1), br=br,
        g_final=g_final.reshape(1, D),
    )

    h0_p = jnp.zeros((n_pb // SCAN_B, 2, SCAN_B, D), F32)
    finish_p, last, (w2b,) = _group(x_prompt, mod[0:1], False, h0_p, p, x_prompt.shape[1], 512,
                                    [w2[0]])
    state_new = last.transpose(0, 2, 1, 3).reshape(n_pb, 1, 2, D)

    h0_s = state_rnn[:, 0].reshape(n_sb // SCAN_B, SCAN_B, 2, D).transpose(0, 2, 1, 3)
    finish_s, _, (w1b, w3b) = _group(x_sample, mod[1:1 + n_sb], True, h0_s, p, GRID_W, 512,
                                     [w1[0], w3[0]])
    return (finish_p(w1b, w3b, w2b), finish_s(w1b, w3b, w2b), state_new)
```

```python
import functools

import jax
import jax.numpy as jnp
from jax import lax
from jax.experimental import pallas as pl
from jax.experimental.pallas import tpu as pltpu
from jax.experimental.pallas import tpu_sc as plsc

D = 1024
N_HEADS = 8
HEAD = D // N_HEADS
GRID_W = 64
RG_C = 8.0
N_GROUPS = 4
EPG = 8
N_EXPERTS = N_GROUPS * EPG
D_EXPERT = 512
EPS = 1e-6
F32 = jnp.float32
BF16 = jnp.bfloat16

V7X_LANES = 128
V7X_SUBLANES = 8
V7X_VMEM_LIMIT_BYTES = 56 * 1024 * 1024
V7X_SC_CORES = 2
V7X_SC_SUBCORES = 16
V7X_SC_WORKERS = V7X_SC_CORES * V7X_SC_SUBCORES
SC_WINDOW = 64

XR_T = 256
XR_SUB = 32
XR_MB = 4
XR_LEFT = 2
XR_COL = 3
SCAN_T = 128
SCAN_B = V7X_SUBLANES
SCAN_UNROLL = 8
LOG2_E = 1.4426950408889634
TINY = 1e-30
MIX_TM = 512
FIN_TM = 1024
EXPERT_ROW_BUFS = 3
EXPERT_W_BUFS = 3
ROUTE_PAD = 128
NEG_BIG = -1e30


def _sigmoid(x):
    return 0.5 * jnp.tanh(0.5 * x) + 0.5


def _norm_mod(x, g, scale, shift):
    ms = jnp.mean(x * x, axis=-1, keepdims=True)
    return (x * lax.rsqrt(ms + EPS)) * (g * (1.0 + scale)) + shift


def _dot(a, b):
    return jnp.dot(a, b, preferred_element_type=F32)


def _params(sem, vmem=V7X_VMEM_LIMIT_BYTES):
    return pltpu.CompilerParams(dimension_semantics=sem, vmem_limit_bytes=vmem)


def _const_spec(shape):
    zeros = (0,) * len(shape)
    return pl.BlockSpec(shape, lambda *_: zeros, pipeline_mode=pl.Buffered(1))


def _ada_body(c_ref, w_ref, b_ref, o_ref):
    c = c_ref[...]
    s = (c * _sigmoid(c)).astype(BF16)
    o_ref[...] = _dot(s, w_ref[...].astype(BF16)) + b_ref[...]


def _ada(cond, w_ada, b_ada):
    rows = cond.shape[0]
    n_out = w_ada.shape[1]
    return pl.pallas_call(
        _ada_body,
        grid=(n_out // D,),
        in_specs=[pl.BlockSpec((rows, D), lambda i: (0, 0)),
                  pl.BlockSpec((D, D), lambda i: (0, i)),
                  pl.BlockSpec((1, D), lambda i: (0, i))],
        out_specs=pl.BlockSpec((rows, D), lambda i: (0, i)),
        out_shape=jax.ShapeDtypeStruct((rows, n_out), F32),
        compiler_params=_params(("parallel",)),
        name="ada",
    )(cond, w_ada, b_ada.reshape(1, n_out))


def _xr_body(x_ref, xn_ref, mod_ref, g_ref, w_ref, cw_ref, cb_ref, o_ref, xt):
    j = pl.program_id(1)
    n_t = pl.num_programs(1)
    t_len = x_ref.shape[1]
    nb = SCAN_B
    body0 = XR_LEFT * nb

    @pl.when(j == 0)
    def _():
        xt[:, 0:body0, :] = jnp.zeros((N_HEADS, body0, HEAD), F32)

    @pl.when(j > 0)
    def _():
        xt[:, 0:body0, :] = xt[:, t_len * nb:t_len * nb + body0, :]

    g = g_ref[...]
    for b0 in range(0, nb, XR_MB):
        h = jnp.concatenate(
            [_norm_mod(x_ref[b], g, mod_ref[b, 1:2, :], mod_ref[b, 0:1, :]).astype(BF16)
             for b in range(b0, b0 + XR_MB)], axis=0)
        r = _dot(h, w_ref[...])
        for i in range(XR_MB):
            for s in range(N_HEADS):
                xt[s, pl.ds(body0 + b0 + i, t_len, stride=nb), :] = (
                    r[i * t_len:(i + 1) * t_len, s * HEAD:(s + 1) * HEAD])

    hn = _norm_mod(xn_ref[:, 0, :], g, mod_ref[:, 1, :], mod_ref[:, 0, :]).astype(BF16)
    rn = jnp.where(j < n_t - 1, _dot(hn, w_ref[...]), 0.0)
    tail = body0 + t_len * nb
    for s in range(N_HEADS):
        xt[s, tail:tail + nb, :] = rn[:, s * HEAD:(s + 1) * HEAD]

    sub = XR_SUB
    for s in range(N_HEADS):
        sl = slice(s * HEAD, (s + 1) * HEAD)
        for t0 in range(0, t_len, sub):
            y = cb_ref[:, sl]
            for k in range(4):
                r0 = (t0 + k) * nb
                y = y + xt[s, r0:r0 + sub * nb, :] * cw_ref[k:k + 1, sl]
            o_ref[t0:t0 + sub, :, sl] = y.reshape(sub, nb, HEAD)


def _xr_conv(x, mod_seq, g1, w_in, cw, cb):
    n_b, seq_len, _ = x.shape
    t_len = min(XR_T, seq_len)
    assert n_b % SCAN_B == 0 and seq_len % t_len == 0 and t_len % XR_SUB == 0
    n_t = seq_len // t_len
    nxt = t_len // V7X_SUBLANES
    last_blk = seq_len // V7X_SUBLANES - 1
    return pl.pallas_call(
        _xr_body,
        grid=(n_b // SCAN_B, n_t),
        in_specs=[pl.BlockSpec((SCAN_B, t_len, D), lambda g, j: (g, j, 0)),
                  pl.BlockSpec((SCAN_B, V7X_SUBLANES, D),
                               lambda g, j: (g, jnp.minimum((j + 1) * nxt, last_blk), 0)),
                  pl.BlockSpec((SCAN_B, 6, D), lambda g, j: (g, 0, 0)),
                  _const_spec((1, D)),
                  pl.BlockSpec((D, D), lambda g, j: (0, XR_COL), pipeline_mode=pl.Buffered(1)),
                  _const_spec((4, D)),
                  _const_spec((1, D))],
        out_specs=pl.BlockSpec((t_len, SCAN_B, D), lambda g, j: (j, g, 0)),
        out_shape=jax.ShapeDtypeStruct((seq_len, n_b, D), F32),
        scratch_shapes=[pltpu.VMEM((N_HEADS, (t_len + XR_LEFT + 1) * SCAN_B, HEAD), F32)],
        compiler_params=_params(("parallel", "arbitrary")),
        name="xr_conv",
    )(x, x, mod_seq, g1, w_in, cw, cb)


def _scan_body(xf_ref, xb_ref, h0_ref, wg_ref, ba_ref, bx_ref, lam_ref,
               hf_ref, hb_ref, last_ref, a_s, u_s, h_s, hc):
    j = pl.program_id(1)
    n_t = pl.num_programs(1)
    t_len = xf_ref.shape[0]
    rows = SCAN_B * t_len

    @pl.when(j == 0)
    def _():
        for d in range(2):
            for s in range(N_HEADS):
                hc[d, s] = h0_ref[0, d, :, s * HEAD:(s + 1) * HEAD]

    for d, x_ref in ((0, xf_ref), (1, xb_ref)):
        z = -lam_ref[d:d + 1, :]
        sp = jnp.maximum(z, 0.0) + jnp.log(1.0 + jnp.exp(-jnp.abs(z)))
        c2 = (-0.5 * RG_C * LOG2_E) * sp
        for hd in range(N_HEADS):
            sl = slice(hd * HEAD, (hd + 1) * HEAD)
            xh = x_ref[:, :, sl].reshape(rows, HEAD)
            g = _dot(xh.astype(BF16), wg_ref[d, hd])
            t_r = jnp.tanh(g[:, :HEAD] + 0.5 * ba_ref[d:d + 1, sl])
            t_i = jnp.tanh(g[:, HEAD:] + 0.5 * bx_ref[d:d + 1, sl])
            a = jnp.exp2(c2[:, sl] * t_r + c2[:, sl])
            q = 0.25 - 0.25 * (a * a)
            half_mult = q * lax.rsqrt(jnp.maximum(q, TINY))
            u = half_mult * ((t_i + 1.0) * xh)
            a_s[d, hd] = a
            u_s[d, hd] = u

    def step(tb, carry):
        h_f, h_b = carry
        for k in range(SCAN_UNROLL):
            t = tb * SCAN_UNROLL + k
            rf = pl.multiple_of(t * SCAN_B, SCAN_B)
            rb = pl.multiple_of((t_len - 1 - t) * SCAN_B, SCAN_B)
            h_f = a_s[0, :, pl.ds(rf, SCAN_B), :] * h_f + u_s[0, :, pl.ds(rf, SCAN_B), :]
            h_b = a_s[1, :, pl.ds(rb, SCAN_B), :] * h_b + u_s[1, :, pl.ds(rb, SCAN_B), :]
            h_s[0, :, pl.ds(rf, SCAN_B), :] = h_f
            h_s[1, :, pl.ds(rb, SCAN_B), :] = h_b
        return h_f, h_b

    h_f, h_b = lax.fori_loop(0, t_len // SCAN_UNROLL, step, (hc[0], hc[1]))
    hc[0] = h_f
    hc[1] = h_b

    for d, o_ref in ((0, hf_ref), (1, hb_ref)):
        for hd in range(N_HEADS):
            for b in range(SCAN_B):
                o_ref[b, :, hd * HEAD:(hd + 1) * HEAD] = (
                    h_s[d, hd, pl.ds(b, t_len, stride=SCAN_B), :].astype(BF16))

    @pl.when(j == n_t - 1)
    def _():
        for d in range(2):
            for s in range(N_HEADS):
                last_ref[0, d, :, s * HEAD:(s + 1) * HEAD] = hc[d, s]


def _rglru_scan(xc, h0, wg, ba, bx, lam):
    seq_len, n_b, _ = xc.shape
    n_g = n_b // SCAN_B
    n_t = seq_len // SCAN_T
    blk = (SCAN_B, SCAN_T, D)
    blk_in = (SCAN_T, SCAN_B, D)
    return pl.pallas_call(
        _scan_body,
        grid=(n_g, n_t),
        in_specs=[pl.BlockSpec(blk_in, lambda g, j: (j, g, 0)),
                  pl.BlockSpec(blk_in, lambda g, j: (n_t - 1 - j, g, 0)),
                  pl.BlockSpec((1, 2, SCAN_B, D), lambda g, j: (g, 0, 0, 0)),
                  _const_spec((2, N_HEADS, HEAD, 2 * HEAD)),
                  _const_spec((2, D)),
                  _const_spec((2, D)),
                  _const_spec((2, D))],
        out_specs=[pl.BlockSpec(blk, lambda g, j: (g, j, 0)),
                   pl.BlockSpec(blk, lambda g, j: (g, n_t - 1 - j, 0)),
                   pl.BlockSpec((1, 2, SCAN_B, D), lambda g, j: (g, 0, 0, 0))],
        out_shape=[jax.ShapeDtypeStruct((n_b, seq_len, D), BF16),
                   jax.ShapeDtypeStruct((n_b, seq_len, D), BF16),
                   jax.ShapeDtypeStruct((n_g, 2, SCAN_B, D), F32)],
        scratch_shapes=[pltpu.VMEM((2, N_HEADS, SCAN_T * SCAN_B, HEAD), F32),
                        pltpu.VMEM((2, N_HEADS, SCAN_T * SCAN_B, HEAD), F32),
                        pltpu.VMEM((2, N_HEADS, SCAN_T * SCAN_B, HEAD), F32),
                        pltpu.VMEM((2, N_HEADS, SCAN_B, HEAD), F32)],
        compiler_params=_params(("parallel", "arbitrary")),
        name="rglru_scan",
    )(xc, xc, h0, wg, ba, bx, lam)


def _pack_halves(v):
    half = v.shape[1] // 2
    lo = lax.bitcast_convert_type(v[:, :half].astype(BF16).astype(F32), jnp.uint32)
    hi = lax.bitcast_convert_type(v[:, half:].astype(BF16).astype(F32), jnp.uint32)
    return lax.bitcast_convert_type((lo >> 16) | (hi & jnp.uint32(0xFFFF0000)), jnp.int32)


def _unpack_halves(p):
    u = lax.bitcast_convert_type(p, jnp.uint32)
    lo = lax.bitcast_convert_type(u << 16, F32)
    hi = lax.bitcast_convert_type(u & jnp.uint32(0xFFFF0000), F32)
    return lo, hi


def _mix_body(x_ref, mod_ref, hf_ref, hb_ref, g1_ref, g2_ref, win_ref, cw_ref,
              wco_ref, wro_ref, wo_ref, wr_ref, br_ref, *rest, row_len, n_cast):
    cast_in = rest[:n_cast]
    x1_ref, h2_ref, route_ref, wts_ref, cnt_ref = rest[n_cast:n_cast + 5]
    cast_out = rest[n_cast + 5:2 * n_cast + 5]
    seen = rest[2 * n_cast + 5]
    x = x_ref[...]
    tm = x.shape[0]
    h = _norm_mod(x, g1_ref[...], mod_ref[0, 1:2, :], mod_ref[0, 0:1, :]).astype(BF16)

    def proj(k):
        return _dot(h, win_ref[:, k * D:(k + 1) * D])

    cv = proj(1) * proj(2)
    pos = lax.broadcasted_iota(jnp.int32, (tm, 1), 0) % row_len
    conv = cv * cw_ref[1:2, :]
    conv = conv + jnp.where(pos >= 1, pltpu.roll(cv, 1, 0), 0.0) * cw_ref[0:1, :]
    conv = conv + jnp.where(pos <= row_len - 2, pltpu.roll(cv, tm - 1, 0), 0.0) * cw_ref[2:3, :]
    y_a = _dot((proj(0) * conv).astype(BF16), wco_ref[...])
    merged = _sigmoid(proj(5)) * y_a

    hs = hf_ref[...].astype(F32) + hb_ref[...].astype(F32)
    y_b = _dot((hs * jax.nn.gelu(proj(4))).astype(BF16), wro_ref[...])
    merged = merged + _sigmoid(proj(6)) * y_b

    mix = _dot(merged.astype(BF16), wo_ref[...])
    x1 = x + mod_ref[0, 2:3, :] * mix
    x1_ref[...] = x1.astype(BF16)
    for src, dst in zip(cast_in, cast_out):
        dst[...] = src[...].astype(BF16)
    h2 = _norm_mod(x1, g2_ref[...], mod_ref[0, 4:5, :], mod_ref[0, 3:4, :])
    h2_ref[...] = _pack_halves(h2)

    h2_hi = h2.astype(BF16)
    h2_lo = (h2 - h2_hi.astype(F32)).astype(BF16)
    big = _dot(h2_hi, wr_ref[...])
    logits = (big[:, :ROUTE_PAD] + big[:, ROUTE_PAD:]
              + _dot(h2_lo, wr_ref[:, :ROUTE_PAD]) + br_ref[...])
    lt = logits.T
    row = lax.broadcasted_iota(jnp.int32, (EPG, tm), 0)
    lg = lt[0:EPG]
    mg = jnp.max(lg, axis=0, keepdims=True)
    p_grp = 1.0 / jnp.sum(jnp.exp(lg - mg), axis=0, keepdims=True)
    grp = jnp.min(jnp.where(lg == mg, row, EPG), axis=0, keepdims=True)
    le = lt[EPG * N_GROUPS:EPG * (N_GROUPS + 1)]
    for g in range(N_GROUPS - 2, -1, -1):
        le = jnp.where(grp == g, lt[EPG * (g + 1):EPG * (g + 2)], le)
    me = jnp.max(le, axis=0, keepdims=True)
    ee = jnp.exp(le - me)
    pe = ee / jnp.sum(ee, axis=0, keepdims=True)
    p1 = jnp.max(pe, axis=0, keepdims=True)
    i1 = jnp.min(jnp.where(pe == p1, row, EPG), axis=0, keepdims=True)
    pe2 = jnp.where(row == i1, -1.0, pe)
    p2 = jnp.max(pe2, axis=0, keepdims=True)
    i2 = jnp.min(jnp.where(pe2 == p2, row, EPG), axis=0, keepdims=True)
    den = p1 + p2
    e1 = grp * EPG + i1
    e2 = grp * EPG + i2

    @pl.when(pl.program_id(0) == 0)
    def _():
        seen[...] = jnp.zeros_like(seen)

    erow = lax.broadcasted_iota(jnp.int32, (N_EXPERTS, tm), 0)
    hit1 = erow == e1
    hit2 = erow == e2
    both = jnp.where(jnp.logical_or(hit1, hit2), 1.0, 0.0)
    tri = jnp.where(lax.broadcasted_iota(jnp.int32, (tm, tm), 0)
                    <= lax.broadcasted_iota(jnp.int32, (tm, tm), 1), 1.0, 0.0).astype(BF16)
    before = _dot(both.astype(BF16), tri) - both + seen[...]
    r1 = jnp.sum(jnp.where(hit1, before, 0.0), axis=0, keepdims=True).astype(jnp.int32)
    r2 = jnp.sum(jnp.where(hit2, before, 0.0), axis=0, keepdims=True).astype(jnp.int32)
    total = seen[...] + jnp.sum(both, axis=1, keepdims=True)
    seen[...] = total
    cnt_ref[...] = total[:, :V7X_LANES].astype(jnp.int32)

    route_ref[...] = jnp.where(row == 0, e1, jnp.where(row == 1, e2, jnp.where(
        row == 2, r1, jnp.where(row == 3, r2, 0))))
    w8 = jnp.where(row == 0, p_grp * p1 / den, jnp.where(row == 1, p_grp * p2 / den, 0.0))
    wts_ref[...] = jnp.concatenate([w8, jnp.zeros((V7X_LANES - EPG, tm), F32)], axis=0).T


def _mixer(x, mod, tiles_per_mod, hf, hb, g1, g2, w_in, cw, wco, wro, wo, wr, br, row_len, cast_ws):
    n = x.shape[0]
    tm = MIX_TM
    assert tm % row_len == 0 and n % tm == 0
    steps = n // tm
    assert N_EXPERTS % steps == 0
    epb = N_EXPERTS // steps
    cast_specs = [pl.BlockSpec((epb,) + w.shape[1:], lambda i: (i, 0, 0)) for w in cast_ws]
    cast_shapes = [jax.ShapeDtypeStruct(w.shape, BF16) for w in cast_ws]
    mod_map = lambda i: (i // tiles_per_mod, 0, 0)
    tok = lambda i: (i, 0)
    col = lambda i: (0, i)
    return pl.pallas_call(
        functools.partial(_mix_body, row_len=row_len, n_cast=len(cast_ws)),
        grid=(steps,),
        in_specs=[pl.BlockSpec((tm, D), tok),
                  pl.BlockSpec((1, 6, D), mod_map),
                  pl.BlockSpec((tm, D), tok),
                  pl.BlockSpec((tm, D), tok),
                  _const_spec((1, D)),
                  _const_spec((1, D)),
                  _const_spec(w_in.shape),
                  _const_spec((3, D)),
                  _const_spec((D, D)),
                  _const_spec((D, D)),
                  _const_spec((D, D)),
                  _const_spec((D, 2 * ROUTE_PAD)),
                  _const_spec((1, ROUTE_PAD))] + cast_specs,
        out_specs=[pl.BlockSpec((tm, D), tok),
                   pl.BlockSpec((tm, D // 2), tok),
                   pl.BlockSpec((EPG, tm), col),
                   pl.BlockSpec((tm, V7X_LANES), tok),
                   pl.BlockSpec((N_EXPERTS, V7X_LANES), lambda i: (0, 0))] + cast_specs,
        out_shape=[jax.ShapeDtypeStruct((n, D), BF16),
                   jax.ShapeDtypeStruct((n, D // 2), jnp.int32),
                   jax.ShapeDtypeStruct((EPG, n), jnp.int32),
                   jax.ShapeDtypeStruct((n, V7X_LANES), F32),
                   jax.ShapeDtypeStruct((N_EXPERTS, V7X_LANES), jnp.int32)] + cast_shapes,
        scratch_shapes=[pltpu.VMEM((N_EXPERTS, tm), F32)],
        compiler_params=_params(("arbitrary",)),
        name="mixer",
    )(x, mod, hf, hb, g1, g2, w_in, cw, wco, wro, wo, wr, br, *cast_ws)


def _sc_mesh():
    return plsc.VectorSubcoreMesh(core_axis_name="c", subcore_axis_name="s",
                                  num_cores=V7X_SC_CORES, num_subcores=V7X_SC_SUBCORES)


def _sc_worker_id():
    return lax.axis_index("s") * V7X_SC_CORES + lax.axis_index("c")


def _sc_dispatch(rows, dest, n_slots):
    n, width = rows.shape
    per_w = n // V7X_SC_WORKERS
    n_ch = per_w // SC_WINDOW
    assert n_ch * SC_WINDOW * V7X_SC_WORKERS == n
    idx = dest.reshape(2, V7X_SC_WORKERS, n_ch, SC_WINDOW).transpose(1, 0, 2, 3)

    def body(x_hbm, d_hbm, o_hbm, idx_v, buf, ld_sem, st_sem):
        wid = _sc_worker_id()
        pltpu.sync_copy(d_hbm.at[wid], idx_v)

        def load(j):
            src = x_hbm.at[pl.ds(wid * per_w + j * SC_WINDOW, SC_WINDOW)]
            return pltpu.async_copy(src, buf.at[j % 2], ld_sem.at[j % 2])

        def scatter(j):
            return [pltpu.async_copy(buf.at[j % 2], o_hbm.at[idx_v.at[k, j]], st_sem.at[j % 2])
                    for k in range(2)]

        loads = {0: load(0)}
        stores = {}
        for j in range(n_ch):
            loads[j].wait()
            if j >= 1:
                for cp in stores[j - 1]:
                    cp.wait()
            if j + 1 < n_ch:
                loads[j + 1] = load(j + 1)
            stores[j] = scatter(j)
        for cp in stores[n_ch - 1]:
            cp.wait()

    return pl.kernel(
        body,
        out_type=jax.ShapeDtypeStruct((n_slots, width), jnp.int32),
        mesh=_sc_mesh(),
        scratch_types=[pltpu.VMEM((2, n_ch, SC_WINDOW), jnp.int32),
                       pltpu.VMEM((2, SC_WINDOW, width), jnp.int32),
                       pltpu.SemaphoreType.DMA((2,)),
                       pltpu.SemaphoreType.DMA((2,))],
        name="sc_dispatch",
    )(rows, idx)


def _sc_collect(rows, dest):
    n = dest.shape[1]
    width = rows.shape[1]
    per_w = n // V7X_SC_WORKERS
    n_ch = per_w // SC_WINDOW
    assert n_ch * SC_WINDOW * V7X_SC_WORKERS == n
    idx = dest.reshape(2, V7X_SC_WORKERS, n_ch, SC_WINDOW).transpose(1, 0, 2, 3)
    windows = [(k, j) for k in range(2) for j in range(n_ch)]

    def body(y_hbm, d_hbm, o_hbm, idx_v, buf, ld_sem, st_sem):
        wid = _sc_worker_id()
        pltpu.sync_copy(d_hbm.at[wid], idx_v)

        def gather(c):
            k, j = windows[c]
            return pltpu.async_copy(y_hbm.at[idx_v.at[k, j]], buf.at[c % 2], ld_sem.at[c % 2])

        def store(c):
            k, j = windows[c]
            dst = o_hbm.at[pl.ds(k * n + wid * per_w + j * SC_WINDOW, SC_WINDOW)]
            return pltpu.async_copy(buf.at[c % 2], dst, st_sem.at[c % 2])

        loads = {0: gather(0)}
        stores = {}
        for c in range(len(windows)):
            loads[c].wait()
            if c >= 1:
                stores[c - 1].wait()
            if c + 1 < len(windows):
                loads[c + 1] = gather(c + 1)
            stores[c] = store(c)
        stores[len(windows) - 1].wait()

    return pl.kernel(
        body,
        out_type=jax.ShapeDtypeStruct((2 * n, width), jnp.int32),
        mesh=_sc_mesh(),
        scratch_types=[pltpu.VMEM((2, n_ch, SC_WINDOW), jnp.int32),
                       pltpu.VMEM((2, SC_WINDOW, width), jnp.int32),
                       pltpu.SemaphoreType.DMA((2,)),
                       pltpu.SemaphoreType.DMA((2,))],
        name="sc_collect",
    )(rows, idx)


def _expert_body(tr_ref, re_ref, nr_ref, nu_ref, xs_hbm, w1_hbm, w3_hbm, w2_hbm, o_ref,
                 xbuf, w1s, w3s, w2s, xsem, sem):
    i = pl.program_id(0)
    n_used = nu_ref[0]
    tme = xbuf.shape[1]
    run = tr_ref[i]
    first = jnp.logical_or(i == 0, run != tr_ref[jnp.maximum(i - 1, 0)])
    slot = run % EXPERT_W_BUFS

    def row_copy(t):
        s = t % EXPERT_ROW_BUFS
        start = t * tme if isinstance(t, int) else pl.multiple_of(t * tme, tme)
        src = xs_hbm.at[pl.ds(start, tme)]
        return pltpu.make_async_copy(src, xbuf.at[s], xsem.at[s])

    def weight_copies(r):
        e = re_ref[r]
        s = r % EXPERT_W_BUFS
        return (pltpu.make_async_copy(w1_hbm.at[e], w1s.at[s], sem.at[0, s]),
                pltpu.make_async_copy(w3_hbm.at[e], w3s.at[s], sem.at[1, s]),
                pltpu.make_async_copy(w2_hbm.at[e], w2s.at[s], sem.at[2, s]))

    @pl.when(i == 0)
    def _():
        for t in range(EXPERT_ROW_BUFS - 1):
            @pl.when(t < n_used)
            def _():
                row_copy(t).start()
        for r in range(EXPERT_W_BUFS - 1):
            @pl.when(r < nr_ref[0])
            def _():
                for cp in weight_copies(r):
                    cp.start()

    @pl.when(jnp.logical_and(first, i < n_used))
    def _():
        @pl.when(run + (EXPERT_W_BUFS - 1) < nr_ref[0])
        def _():
            for cp in weight_copies(run + (EXPERT_W_BUFS - 1)):
                cp.start()

        for cp in weight_copies(run):
            cp.wait()

    @pl.when(i < n_used)
    def _():
        @pl.when(i + (EXPERT_ROW_BUFS - 1) < n_used)
        def _():
            row_copy(i + (EXPERT_ROW_BUFS - 1)).start()

        row_copy(i).wait()
        lo, hi = _unpack_halves(xbuf[i % EXPERT_ROW_BUFS])
        lo = lo.astype(BF16)
        hi = hi.astype(BF16)
        half = D // 2
        a = _dot(lo, w1s[slot, 0:half, :]) + _dot(hi, w1s[slot, half:D, :])
        b = _dot(lo, w3s[slot, 0:half, :]) + _dot(hi, w3s[slot, half:D, :])
        z = (a * _sigmoid(a)) * b
        o_ref[...] = _pack_halves(_dot(z.astype(BF16), w2s[slot]))


def _experts(tile_run, run_e, n_runs, n_used, xs, w1, w3, w2, tme):
    n_slots = xs.shape[0]
    grid_spec = pltpu.PrefetchScalarGridSpec(
        num_scalar_prefetch=4,
        grid=(n_slots // tme,),
        in_specs=[pl.BlockSpec(memory_space=pl.ANY),
                  pl.BlockSpec(memory_space=pl.ANY),
                  pl.BlockSpec(memory_space=pl.ANY),
                  pl.BlockSpec(memory_space=pl.ANY)],
        out_specs=pl.BlockSpec((tme, D // 2),
                               lambda i, tr, re, nr, nu: (jnp.clip(i, 0, jnp.maximum(nu[0] - 1, 0)), 0)),
        scratch_shapes=[pltpu.VMEM((EXPERT_ROW_BUFS, tme, D // 2), jnp.int32),
                        pltpu.VMEM((EXPERT_W_BUFS, D, D_EXPERT), BF16),
                        pltpu.VMEM((EXPERT_W_BUFS, D, D_EXPERT), BF16),
                        pltpu.VMEM((EXPERT_W_BUFS, D_EXPERT, D), BF16),
                        pltpu.SemaphoreType.DMA((EXPERT_ROW_BUFS,)),
                        pltpu.SemaphoreType.DMA((3, EXPERT_W_BUFS))],
    )
    return pl.pallas_call(
        _expert_body,
        grid_spec=grid_spec,
        out_shape=jax.ShapeDtypeStruct((n_slots, D // 2), jnp.int32),
        compiler_params=_params(("arbitrary",)),
        name="experts",
    )(tile_run, run_e, n_runs, n_used, xs, w1, w3, w2)


def _final_body(x1_ref, mod_ref, y0_ref, y1_ref, wt_ref, gf_ref, o_ref):
    w0 = wt_ref[:, 0:1]
    w1 = wt_ref[:, 1:2]
    lo0, hi0 = _unpack_halves(y0_ref[...])
    lo1, hi1 = _unpack_halves(y1_ref[...])
    moe = jnp.concatenate([w0 * lo0 + w1 * lo1, w0 * hi0 + w1 * hi1], axis=1)
    x2 = x1_ref[...].astype(F32) + mod_ref[0, 5:6, :] * moe
    ms = jnp.mean(x2 * x2, axis=-1, keepdims=True)
    o_ref[...] = x2 * lax.rsqrt(ms + EPS) * gf_ref[...]


def _final(x1, mod, tiles_per_mod, yg, wts, g_final):
    n = x1.shape[0]
    tm = FIN_TM
    nt = n // tm
    return pl.pallas_call(
        _final_body,
        grid=(nt,),
        in_specs=[pl.BlockSpec((tm, D), lambda i: (i, 0)),
                  pl.BlockSpec((1, 6, D), lambda i: (i // tiles_per_mod, 0, 0)),
                  pl.BlockSpec((tm, D // 2), lambda i: (i, 0)),
                  pl.BlockSpec((tm, D // 2), lambda i: (i + nt, 0)),
                  pl.BlockSpec((tm, V7X_LANES), lambda i: (i, 0)),
                  _const_spec((1, D))],
        out_specs=pl.BlockSpec((tm, D), lambda i: (i, 0)),
        out_shape=jax.ShapeDtypeStruct((n, D), F32),
        compiler_params=_params(("parallel",)),
        name="final",
    )(x1, mod, yg, yg, wts, g_final)


def _slot_plan(route, cnt, n, tme):
    counts = cnt[:, 0]
    padded = ((counts + tme - 1) // tme) * tme
    pend = jnp.cumsum(padded)
    pstart = pend - padded
    onehot = route[0:2, :, None] == jnp.arange(N_EXPERTS, dtype=jnp.int32)[None, None, :]
    dest = jnp.sum(jnp.where(onehot, pstart[None, None, :], 0), axis=-1) + route[2:4]
    n_slots = ((2 * n + N_EXPERTS * (tme - 1)) // tme) * tme
    dest = jnp.clip(dest, 0, n_slots - 1)
    tile_start = jnp.arange(n_slots // tme, dtype=jnp.int32) * tme
    tile_e = jnp.sum((tile_start[:, None] >= pend[None, :]).astype(jnp.int32), axis=1)
    tile_e = jnp.minimum(tile_e, N_EXPERTS - 1)
    n_used = (pend[-1] // tme).astype(jnp.int32).reshape(1)
    used = counts > 0
    run_of_e = jnp.cumsum(used.astype(jnp.int32)) - 1
    eids = jnp.arange(N_EXPERTS, dtype=jnp.int32)
    run_e = jnp.sum(jnp.where(used[None, :] & (run_of_e[None, :] == eids[:, None]), eids[None, :], 0), axis=1)
    tile_run = jnp.sum(jnp.where(tile_e[:, None] == eids[None, :], run_of_e[None, :], 0), axis=1)
    n_runs = jnp.sum(used.astype(jnp.int32)).reshape(1)
    plan = (tile_run.astype(jnp.int32), run_e.astype(jnp.int32), n_runs, n_used)
    return dest.astype(jnp.int32), plan, n_slots


def _group(x, mod, mod_per_seq, h0, p, row_len, tme, cast_ws):
    n_b, seq_len, _ = x.shape
    n = n_b * seq_len
    xt = x.reshape(n, D)
    mod_seq = mod if mod_per_seq else jnp.broadcast_to(mod, (n_b, 6, D))
    xc = _xr_conv(x, mod_seq, p["g1"], p["w_in"], p["rnn_conv_w"], p["rnn_conv_b"])
    hf, hb, last = _rglru_scan(xc, h0, p["wg"], p["ba"], p["bx"], p["lam"])
    tiles_per_mod = (seq_len // MIX_TM) if mod_per_seq else (n // MIX_TM)
    outs = _mixer(xt, mod, tiles_per_mod, hf.reshape(n, D), hb.reshape(n, D),
                                     p["g1"], p["g2"], p["w_in"], p["conv_w"], p["wco"], p["wro"],
                                     p["wo"], p["wr"], p["br"], row_len, cast_ws)
    x1, h2, route, wts, cnt = outs[:5]
    casts = outs[5:]
    dest, plan, n_slots = _slot_plan(route, cnt, n, tme)
    xs = _sc_dispatch(h2, dest, n_slots)

    def finish(w1b, w3b, w2b):
        ys = _experts(*plan, xs, w1b, w3b, w2b, tme)
        yg = _sc_collect(ys, dest)
        tiles_per_mod_f = (seq_len // FIN_TM) if mod_per_seq else (n // FIN_TM)
        y = _final(x1, mod, tiles_per_mod_f, yg, wts, p["g_final"])
        return y.reshape(n_b, seq_len, D)

    return finish, last, casts


def kernel(x_prompt, x_sample, state_rnn, c, c_ctx, w_ada, b_ada, g_norm1, g_norm2, w_in, conv_w, w_conv_out, rnn_conv_w, rnn_conv_b, w_gate_a, b_gate_a, w_gate_x, b_gate_x, lam, w_rnn_out, w_o, w_router_group, b_router_group, w_router_expert, b_router_expert, w1, w3, w2, g_final):
    assert w_ada.shape[0] == 1, "single layer"
    n_pb, n_sb = x_prompt.shape[0], x_sample.shape[0]

    cond = jnp.concatenate([c_ctx[None, :], c, jnp.zeros((16 - 1 - n_sb, D), F32)], axis=0)
    mod = _ada(cond, w_ada[0], b_ada[0]).reshape(16, 6, D)

    w_in_b = w_in[0].astype(BF16)
    pad_w = jnp.zeros((D, EPG - N_GROUPS), F32)
    wr = jnp.concatenate([w_router_group[0], pad_w, w_router_expert[0],
                          jnp.zeros((D, ROUTE_PAD - EPG - N_EXPERTS), F32)], axis=1)
    br = jnp.concatenate([b_router_group[0], jnp.full((EPG - N_GROUPS,), NEG_BIG, F32),
                          b_router_expert[0],
                          jnp.zeros((ROUTE_PAD - EPG - N_EXPERTS,), F32)]).reshape(1, ROUTE_PAD)
    wr_hi = wr.astype(BF16)
    p = dict(
        g1=g_norm1, g2=g_norm2, w_in=w_in_b,
        conv_w=conv_w[0], rnn_conv_w=rnn_conv_w[0], rnn_conv_b=rnn_conv_b,
        wg=(0.5 * jnp.concatenate([w_gate_a[0], w_gate_x[0]], axis=-1)).astype(BF16),
        ba=b_gate_a[0], bx=b_gate_x[0], lam=lam[0],
        wco=w_conv_out[0].astype(BF16), wro=w_rnn_out[0].astype(BF16), wo=w_o[0].astype(BF16),
        wr=jnp.concatenate([wr_hi, (wr - wr_hi.astype(F32)).astype(BF16)], axis=1), br=br,
        g_final=g_final.reshape(1, D),
    )

    h0_p = jnp.zeros((n_pb // SCAN_B, 2, SCAN_B, D), F32)
    finish_p, last, (w2b,) = _group(x_prompt, mod[0:1], False, h0_p, p, x_prompt.shape[1], 512,
                                    [w2[0]])
    state_new = last.transpose(0, 2, 1, 3).reshape(n_pb, 1, 2, D)

    h0_s = state_rnn[:, 0].reshape(n_sb // SCAN_B, SCAN_B, 2, D).transpose(0, 2, 1, 3)
    finish_s, _, (w1b, w3b) = _group(x_sample, mod[1:1 + n_sb], True, h0_s, p, GRID_W, 512,
                                     [w1[0], w3[0]])
    return (finish_p(w1b, w3b, w2b), finish_s(w1b, w3b, w2b), state_new)
```

```python
import functools

import jax
import jax.numpy as jnp
from jax import lax
from jax.experimental import pallas as pl
from jax.experimental.pallas import tpu as pltpu
from jax.experimental.pallas import tpu_sc as plsc

D = 1024
N_HEADS = 8
HEAD = D // N_HEADS
GRID_W = 64
RG_C = 8.0
N_GROUPS = 4
EPG = 8
N_EXPERTS = N_GROUPS * EPG
D_EXPERT = 512
EPS = 1e-6
F32 = jnp.float32
BF16 = jnp.bfloat16

V7X_LANES = 128
V7X_SUBLANES = 8
V7X_VMEM_LIMIT_BYTES = 56 * 1024 * 1024
V7X_SC_CORES = 2
V7X_SC_SUBCORES = 16
V7X_SC_WORKERS = V7X_SC_CORES * V7X_SC_SUBCORES
SC_WINDOW = 64

XR_T = 256
XR_SUB = 32
XR_MB = 4
XR_LEFT = 2
XR_COL = 3
SCAN_T = 128
SCAN_B = V7X_SUBLANES
LOG2_E = 1.4426950408889634
TINY = 1e-30
MIX_TM = 512
FIN_TM = 1024
EXPERT_ROW_BUFS = 3
EXPERT_W_BUFS = 3
ROUTE_PAD = 128
NEG_BIG = -1e30


def _sigmoid(x):
    return 0.5 * jnp.tanh(0.5 * x) + 0.5


def _norm_mod(x, g, scale, shift):
    ms = jnp.mean(x * x, axis=-1, keepdims=True)
    return (x * lax.rsqrt(ms + EPS)) * (g * (1.0 + scale)) + shift


def _dot(a, b):
    return jnp.dot(a, b, preferred_element_type=F32)


def _params(sem, vmem=V7X_VMEM_LIMIT_BYTES):
    return pltpu.CompilerParams(dimension_semantics=sem, vmem_limit_bytes=vmem)


def _const_spec(shape):
    zeros = (0,) * len(shape)
    return pl.BlockSpec(shape, lambda *_: zeros, pipeline_mode=pl.Buffered(1))


def _ada_body(c_ref, w_ref, b_ref, o_ref):
    c = c_ref[...]
    s = (c * _sigmoid(c)).astype(BF16)
    o_ref[...] = _dot(s, w_ref[...].astype(BF16)) + b_ref[...]


def _ada(cond, w_ada, b_ada):
    rows = cond.shape[0]
    n_out = w_ada.shape[1]
    return pl.pallas_call(
        _ada_body,
        grid=(n_out // D,),
        in_specs=[pl.BlockSpec((rows, D), lambda i: (0, 0)),
                  pl.BlockSpec((D, D), lambda i: (0, i)),
                  pl.BlockSpec((1, D), lambda i: (0, i))],
        out_specs=pl.BlockSpec((rows, D), lambda i: (0, i)),
        out_shape=jax.ShapeDtypeStruct((rows, n_out), F32),
        compiler_params=_params(("parallel",)),
        name="ada",
    )(cond, w_ada, b_ada.reshape(1, n_out))


def _xr_body(x_ref, xn_ref, mod_ref, g_ref, w_ref, cw_ref, cb_ref, o_ref, xt):
    j = pl.program_id(1)
    n_t = pl.num_programs(1)
    t_len = x_ref.shape[1]
    nb = SCAN_B
    body0 = XR_LEFT * nb

    @pl.when(j == 0)
    def _():
        xt[:, 0:body0, :] = jnp.zeros((N_HEADS, body0, HEAD), F32)

    @pl.when(j > 0)
    def _():
        xt[:, 0:body0, :] = xt[:, t_len * nb:t_len * nb + body0, :]

    g = g_ref[...]
    for b0 in range(0, nb, XR_MB):
        h = jnp.concatenate(
            [_norm_mod(x_ref[b], g, mod_ref[b, 1:2, :], mod_ref[b, 0:1, :]).astype(BF16)
             for b in range(b0, b0 + XR_MB)], axis=0)
        r = _dot(h, w_ref[...])
        for i in range(XR_MB):
            for s in range(N_HEADS):
                xt[s, pl.ds(body0 + b0 + i, t_len, stride=nb), :] = (
                    r[i * t_len:(i + 1) * t_len, s * HEAD:(s + 1) * HEAD])

    hn = _norm_mod(xn_ref[:, 0, :], g, mod_ref[:, 1, :], mod_ref[:, 0, :]).astype(BF16)
    rn = jnp.where(j < n_t - 1, _dot(hn, w_ref[...]), 0.0)
    tail = body0 + t_len * nb
    for s in range(N_HEADS):
        xt[s, tail:tail + nb, :] = rn[:, s * HEAD:(s + 1) * HEAD]

    sub = XR_SUB
    for s in range(N_HEADS):
        sl = slice(s * HEAD, (s + 1) * HEAD)
        for t0 in range(0, t_len, sub):
            y = cb_ref[:, sl]
            for k in range(4):
                r0 = (t0 + k) * nb
                y = y + xt[s, r0:r0 + sub * nb, :] * cw_ref[k:k + 1, sl]
            o_ref[t0:t0 + sub, :, sl] = y.reshape(sub, nb, HEAD)


def _xr_conv(x, mod_seq, g1, w_in, cw, cb):
    n_b, seq_len, _ = x.shape
    t_len = min(XR_T, seq_len)
    assert n_b % SCAN_B == 0 and seq_len % t_len == 0 and t_len % XR_SUB == 0
    n_t = seq_len // t_len
    nxt = t_len // V7X_SUBLANES
    last_blk = seq_len // V7X_SUBLANES - 1
    return pl.pallas_call(
        _xr_body,
        grid=(n_b // SCAN_B, n_t),
        in_specs=[pl.BlockSpec((SCAN_B, t_len, D), lambda g, j: (g, j, 0)),
                  pl.BlockSpec((SCAN_B, V7X_SUBLANES, D),
                               lambda g, j: (g, jnp.minimum((j + 1) * nxt, last_blk), 0)),
                  pl.BlockSpec((SCAN_B, 6, D), lambda g, j: (g, 0, 0)),
                  _const_spec((1, D)),
                  pl.BlockSpec((D, D), lambda g, j: (0, XR_COL), pipeline_mode=pl.Buffered(1)),
                  _const_spec((4, D)),
                  _const_spec((1, D))],
        out_specs=pl.BlockSpec((t_len, SCAN_B, D), lambda g, j: (j, g, 0)),
        out_shape=jax.ShapeDtypeStruct((seq_len, n_b, D), F32),
        scratch_shapes=[pltpu.VMEM((N_HEADS, (t_len + XR_LEFT + 1) * SCAN_B, HEAD), F32)],
        compiler_params=_params(("parallel", "arbitrary")),
        name="xr_conv",
    )(x, x, mod_seq, g1, w_in, cw, cb)


def _scan_body(xf_ref, xb_ref, h0_ref, wg_ref, ba_ref, bx_ref, lam_ref,
               hf_ref, hb_ref, last_ref, a_s, u_s, h_s, hc):
    j = pl.program_id(1)
    n_t = pl.num_programs(1)
    t_len = xf_ref.shape[0]
    rows = SCAN_B * t_len

    @pl.when(j == 0)
    def _():
        for d in range(2):
            for s in range(N_HEADS):
                hc[d, s] = h0_ref[0, d, :, s * HEAD:(s + 1) * HEAD]

    c2s = []
    for d in range(2):
        z = -lam_ref[d:d + 1, :]
        sp = jnp.maximum(z, 0.0) + jnp.log(1.0 + jnp.exp(-jnp.abs(z)))
        c2s.append((-0.5 * RG_C * LOG2_E) * sp)

    for hd in range(N_HEADS):
        sl = slice(hd * HEAD, (hd + 1) * HEAD)
        for d, x_ref in ((0, xf_ref), (1, xb_ref)):
            xh = x_ref[:, :, sl].reshape(rows, HEAD)
            g = _dot(xh.astype(BF16), wg_ref[d, hd])
            t_r = jnp.tanh(g[:, :HEAD] + 0.5 * ba_ref[d:d + 1, sl])
            t_i = jnp.tanh(g[:, HEAD:] + 0.5 * bx_ref[d:d + 1, sl])
            a = jnp.exp2(c2s[d][:, sl] * t_r + c2s[d][:, sl])
            q = 0.25 - 0.25 * (a * a)
            half_mult = q * lax.rsqrt(jnp.maximum(q, TINY))
            u = half_mult * ((t_i + 1.0) * xh)
            a_s[d, hd] = a
            u_s[d, hd] = u
        h_f = hc[0, hd]
        h_b = hc[1, hd]
        for t in range(t_len):
            rf = t * SCAN_B
            rb = (t_len - 1 - t) * SCAN_B
            h_f = a_s[0, hd, rf:rf + SCAN_B, :] * h_f + u_s[0, hd, rf:rf + SCAN_B, :]
            h_b = a_s[1, hd, rb:rb + SCAN_B, :] * h_b + u_s[1, hd, rb:rb + SCAN_B, :]
            h_s[0, hd, rf:rf + SCAN_B, :] = h_f
            h_s[1, hd, rb:rb + SCAN_B, :] = h_b
        hc[0, hd] = h_f
        hc[1, hd] = h_b

    for d, o_ref in ((0, hf_ref), (1, hb_ref)):
        for hd in range(N_HEADS):
            for b in range(SCAN_B):
                o_ref[b, :, hd * HEAD:(hd + 1) * HEAD] = (
                    h_s[d, hd, pl.ds(b, t_len, stride=SCAN_B), :].astype(BF16))

    @pl.when(j == n_t - 1)
    def _():
        for d in range(2):
            for s in range(N_HEADS):
                last_ref[0, d, :, s * HEAD:(s + 1) * HEAD] = hc[d, s]


def _rglru_scan(xc, h0, wg, ba, bx, lam):
    seq_len, n_b, _ = xc.shape
    n_g = n_b // SCAN_B
    n_t = seq_len // SCAN_T
    blk = (SCAN_B, SCAN_T, D)
    blk_in = (SCAN_T, SCAN_B, D)
    return pl.pallas_call(
        _scan_body,
        grid=(n_g, n_t),
        in_specs=[pl.BlockSpec(blk_in, lambda g, j: (j, g, 0)),
                  pl.BlockSpec(blk_in, lambda g, j: (n_t - 1 - j, g, 0)),
                  pl.BlockSpec((1, 2, SCAN_B, D), lambda g, j: (g, 0, 0, 0)),
                  _const_spec((2, N_HEADS, HEAD, 2 * HEAD)),
                  _const_spec((2, D)),
                  _const_spec((2, D)),
                  _const_spec((2, D))],
        out_specs=[pl.BlockSpec(blk, lambda g, j: (g, j, 0)),
                   pl.BlockSpec(blk, lambda g, j: (g, n_t - 1 - j, 0)),
                   pl.BlockSpec((1, 2, SCAN_B, D), lambda g, j: (g, 0, 0, 0))],
        out_shape=[jax.ShapeDtypeStruct((n_b, seq_len, D), BF16),
                   jax.ShapeDtypeStruct((n_b, seq_len, D), BF16),
                   jax.ShapeDtypeStruct((n_g, 2, SCAN_B, D), F32)],
        scratch_shapes=[pltpu.VMEM((2, N_HEADS, SCAN_T * SCAN_B, HEAD), F32),
                        pltpu.VMEM((2, N_HEADS, SCAN_T * SCAN_B, HEAD), F32),
                        pltpu.VMEM((2, N_HEADS, SCAN_T * SCAN_B, HEAD), F32),
                        pltpu.VMEM((2, N_HEADS, SCAN_B, HEAD), F32)],
        compiler_params=_params(("parallel", "arbitrary")),
        name="rglru_scan",
    )(xc, xc, h0, wg, ba, bx, lam)


def _pack_halves(v):
    half = v.shape[1] // 2
    lo = lax.bitcast_convert_type(v[:, :half].astype(BF16).astype(F32), jnp.uint32)
    hi = lax.bitcast_convert_type(v[:, half:].astype(BF16).astype(F32), jnp.uint32)
    return lax.bitcast_convert_type((lo >> 16) | (hi & jnp.uint32(0xFFFF0000)), jnp.int32)


def _unpack_halves(p):
    u = lax.bitcast_convert_type(p, jnp.uint32)
    lo = lax.bitcast_convert_type(u << 16, F32)
    hi = lax.bitcast_convert_type(u & jnp.uint32(0xFFFF0000), F32)
    return lo, hi


def _mix_body(x_ref, mod_ref, hf_ref, hb_ref, g1_ref, g2_ref, win_ref, cw_ref,
              wco_ref, wro_ref, wo_ref, wr_ref, br_ref, *rest, row_len, n_cast):
    cast_in = rest[:n_cast]
    x1_ref, h2_ref, route_ref, wts_ref, cnt_ref = rest[n_cast:n_cast + 5]
    cast_out = rest[n_cast + 5:2 * n_cast + 5]
    seen = rest[2 * n_cast + 5]
    x = x_ref[...]
    tm = x.shape[0]
    h = _norm_mod(x, g1_ref[...], mod_ref[0, 1:2, :], mod_ref[0, 0:1, :]).astype(BF16)

    def proj(k):
        return _dot(h, win_ref[:, k * D:(k + 1) * D])

    cv = proj(1) * proj(2)
    pos = lax.broadcasted_iota(jnp.int32, (tm, 1), 0) % row_len
    conv = cv * cw_ref[1:2, :]
    conv = conv + jnp.where(pos >= 1, pltpu.roll(cv, 1, 0), 0.0) * cw_ref[0:1, :]
    conv = conv + jnp.where(pos <= row_len - 2, pltpu.roll(cv, tm - 1, 0), 0.0) * cw_ref[2:3, :]
    y_a = _dot((proj(0) * conv).astype(BF16), wco_ref[...])
    merged = _sigmoid(proj(5)) * y_a

    hs = hf_ref[...].astype(F32) + hb_ref[...].astype(F32)
    y_b = _dot((hs * jax.nn.gelu(proj(4))).astype(BF16), wro_ref[...])
    merged = merged + _sigmoid(proj(6)) * y_b

    mix = _dot(merged.astype(BF16), wo_ref[...])
    x1 = x + mod_ref[0, 2:3, :] * mix
    x1_ref[...] = x1.astype(BF16)
    for src, dst in zip(cast_in, cast_out):
        dst[...] = src[...].astype(BF16)
    h2 = _norm_mod(x1, g2_ref[...], mod_ref[0, 4:5, :], mod_ref[0, 3:4, :])
    h2_ref[...] = _pack_halves(h2)

    h2_hi = h2.astype(BF16)
    h2_lo = (h2 - h2_hi.astype(F32)).astype(BF16)
    big = _dot(h2_hi, wr_ref[...])
    logits = (big[:, :ROUTE_PAD] + big[:, ROUTE_PAD:]
              + _dot(h2_lo, wr_ref[:, :ROUTE_PAD]) + br_ref[...])
    lt = logits.T
    row = lax.broadcasted_iota(jnp.int32, (EPG, tm), 0)
    lg = lt[0:EPG]
    mg = jnp.max(lg, axis=0, keepdims=True)
    p_grp = 1.0 / jnp.sum(jnp.exp(lg - mg), axis=0, keepdims=True)
    grp = jnp.min(jnp.where(lg == mg, row, EPG), axis=0, keepdims=True)
    le = lt[EPG * N_GROUPS:EPG * (N_GROUPS + 1)]
    for g in range(N_GROUPS - 2, -1, -1):
        le = jnp.where(grp == g, lt[EPG * (g + 1):EPG * (g + 2)], le)
    me = jnp.max(le, axis=0, keepdims=True)
    ee = jnp.exp(le - me)
    pe = ee / jnp.sum(ee, axis=0, keepdims=True)
    p1 = jnp.max(pe, axis=0, keepdims=True)
    i1 = jnp.min(jnp.where(pe == p1, row, EPG), axis=0, keepdims=True)
    pe2 = jnp.where(row == i1, -1.0, pe)
    p2 = jnp.max(pe2, axis=0, keepdims=True)
    i2 = jnp.min(jnp.where(pe2 == p2, row, EPG), axis=0, keepdims=True)
    den = p1 + p2
    e1 = grp * EPG + i1
    e2 = grp * EPG + i2

    @pl.when(pl.program_id(0) == 0)
    def _():
        seen[...] = jnp.zeros_like(seen)

    erow = lax.broadcasted_iota(jnp.int32, (N_EXPERTS, tm), 0)
    hit1 = erow == e1
    hit2 = erow == e2
    both = jnp.where(jnp.logical_or(hit1, hit2), 1.0, 0.0)
    tri = jnp.where(lax.broadcasted_iota(jnp.int32, (tm, tm), 0)
                    <= lax.broadcasted_iota(jnp.int32, (tm, tm), 1), 1.0, 0.0).astype(BF16)
    before = _dot(both.astype(BF16), tri) - both + seen[...]
    r1 = jnp.sum(jnp.where(hit1, before, 0.0), axis=0, keepdims=True).astype(jnp.int32)
    r2 = jnp.sum(jnp.where(hit2, before, 0.0), axis=0, keepdims=True).astype(jnp.int32)
    total = seen[...] + jnp.sum(both, axis=1, keepdims=True)
    seen[...] = total
    cnt_ref[...] = total[:, :V7X_LANES].astype(jnp.int32)

    route_ref[...] = jnp.where(row == 0, e1, jnp.where(row == 1, e2, jnp.where(
        row == 2, r1, jnp.where(row == 3, r2, 0))))
    w8 = jnp.where(row == 0, p_grp * p1 / den, jnp.where(row == 1, p_grp * p2 / den, 0.0))
    wts_ref[...] = jnp.concatenate([w8, jnp.zeros((V7X_LANES - EPG, tm), F32)], axis=0).T


def _mixer(x, mod, tiles_per_mod, hf, hb, g1, g2, w_in, cw, wco, wro, wo, wr, br, row_len, cast_ws):
    n = x.shape[0]
    tm = MIX_TM
    assert tm % row_len == 0 and n % tm == 0
    steps = n // tm
    assert N_EXPERTS % steps == 0
    epb = N_EXPERTS // steps
    cast_specs = [pl.BlockSpec((epb,) + w.shape[1:], lambda i: (i, 0, 0)) for w in cast_ws]
    cast_shapes = [jax.ShapeDtypeStruct(w.shape, BF16) for w in cast_ws]
    mod_map = lambda i: (i // tiles_per_mod, 0, 0)
    tok = lambda i: (i, 0)
    col = lambda i: (0, i)
    return pl.pallas_call(
        functools.partial(_mix_body, row_len=row_len, n_cast=len(cast_ws)),
        grid=(steps,),
        in_specs=[pl.BlockSpec((tm, D), tok),
                  pl.BlockSpec((1, 6, D), mod_map),
                  pl.BlockSpec((tm, D), tok),
                  pl.BlockSpec((tm, D), tok),
                  _const_spec((1, D)),
                  _const_spec((1, D)),
                  _const_spec(w_in.shape),
                  _const_spec((3, D)),
                  _const_spec((D, D)),
                  _const_spec((D, D)),
                  _const_spec((D, D)),
                  _const_spec((D, 2 * ROUTE_PAD)),
                  _const_spec((1, ROUTE_PAD))] + cast_specs,
        out_specs=[pl.BlockSpec((tm, D), tok),
                   pl.BlockSpec((tm, D // 2), tok),
                   pl.BlockSpec((EPG, tm), col),
                   pl.BlockSpec((tm, V7X_LANES), tok),
                   pl.BlockSpec((N_EXPERTS, V7X_LANES), lambda i: (0, 0))] + cast_specs,
        out_shape=[jax.ShapeDtypeStruct((n, D), BF16),
                   jax.ShapeDtypeStruct((n, D // 2), jnp.int32),
                   jax.ShapeDtypeStruct((EPG, n), jnp.int32),
                   jax.ShapeDtypeStruct((n, V7X_LANES), F32),
                   jax.ShapeDtypeStruct((N_EXPERTS, V7X_LANES), jnp.int32)] + cast_shapes,
        scratch_shapes=[pltpu.VMEM((N_EXPERTS, tm), F32)],
        compiler_params=_params(("arbitrary",)),
        name="mixer",
    )(x, mod, hf, hb, g1, g2, w_in, cw, wco, wro, wo, wr, br, *cast_ws)


def _sc_mesh():
    return plsc.VectorSubcoreMesh(core_axis_name="c", subcore_axis_name="s",
                                  num_cores=V7X_SC_CORES, num_subcores=V7X_SC_SUBCORES)


def _sc_worker_id():
    return lax.axis_index("s") * V7X_SC_CORES + lax.axis_index("c")


def _sc_dispatch(rows, dest, n_slots):
    n, width = rows.shape
    per_w = n // V7X_SC_WORKERS
    n_ch = per_w // SC_WINDOW
    assert n_ch * SC_WINDOW * V7X_SC_WORKERS == n
    idx = dest.reshape(2, V7X_SC_WORKERS, n_ch, SC_WINDOW).transpose(1, 0, 2, 3)

    def body(x_hbm, d_hbm, o_hbm, idx_v, buf, ld_sem, st_sem):
        wid = _sc_worker_id()
        pltpu.sync_copy(d_hbm.at[wid], idx_v)

        def load(j):
            src = x_hbm.at[pl.ds(wid * per_w + j * SC_WINDOW, SC_WINDOW)]
            return pltpu.async_copy(src, buf.at[j % 2], ld_sem.at[j % 2])

        def scatter(j):
            return [pltpu.async_copy(buf.at[j % 2], o_hbm.at[idx_v.at[k, j]], st_sem.at[j % 2])
                    for k in range(2)]

        loads = {0: load(0)}
        stores = {}
        for j in range(n_ch):
            loads[j].wait()
            if j >= 1:
                for cp in stores[j - 1]:
                    cp.wait()
            if j + 1 < n_ch:
                loads[j + 1] = load(j + 1)
            stores[j] = scatter(j)
        for cp in stores[n_ch - 1]:
            cp.wait()

    return pl.kernel(
        body,
        out_type=jax.ShapeDtypeStruct((n_slots, width), jnp.int32),
        mesh=_sc_mesh(),
        scratch_types=[pltpu.VMEM((2, n_ch, SC_WINDOW), jnp.int32),
                       pltpu.VMEM((2, SC_WINDOW, width), jnp.int32),
                       pltpu.SemaphoreType.DMA((2,)),
                       pltpu.SemaphoreType.DMA((2,))],
        name="sc_dispatch",
    )(rows, idx)


def _sc_collect(rows, dest):
    n = dest.shape[1]
    width = rows.shape[1]
    per_w = n // V7X_SC_WORKERS
    n_ch = per_w // SC_WINDOW
    assert n_ch * SC_WINDOW * V7X_SC_WORKERS == n
    idx = dest.reshape(2, V7X_SC_WORKERS, n_ch, SC_WINDOW).transpose(1, 0, 2, 3)
    windows = [(k, j) for k in range(2) for j in range(n_ch)]

    def body(y_hbm, d_hbm, o_hbm, idx_v, buf, ld_sem, st_sem):
        wid = _sc_worker_id()
        pltpu.sync_copy(d_hbm.at[wid], idx_v)

        def gather(c):
            k, j = windows[c]
            return pltpu.async_copy(y_hbm.at[idx_v.at[k, j]], buf.at[c % 2], ld_sem.at[c % 2])

        def store(c):
            k, j = windows[c]
            dst = o_hbm.at[pl.ds(k * n + wid * per_w + j * SC_WINDOW, SC_WINDOW)]
            return pltpu.async_copy(buf.at[c % 2], dst, st_sem.at[c % 2])

        loads = {0: gather(0)}
        stores = {}
        for c in range(len(windows)):
            loads[c].wait()
            if c >= 1:
                stores[c - 1].wait()
            if c + 1 < len(windows):
                loads[c + 1] = gather(c + 1)
            stores[c] = store(c)
        stores[len(windows) - 1].wait()

    return pl.kernel(
        body,
        out_type=jax.ShapeDtypeStruct((2 * n, width), jnp.int32),
        mesh=_sc_mesh(),
        scratch_types=[pltpu.VMEM((2, n_ch, SC_WINDOW), jnp.int32),
                       pltpu.VMEM((2, SC_WINDOW, width), jnp.int32),
                       pltpu.SemaphoreType.DMA((2,)),
                       pltpu.SemaphoreType.DMA((2,))],
        name="sc_collect",
    )(rows, idx)


def _expert_body(tr_ref, re_ref, nr_ref, nu_ref, xs_hbm, w1_hbm, w3_hbm, w2_hbm, o_ref,
                 xbuf, w1s, w3s, w2s, xsem, sem):
    i = pl.program_id(0)
    n_used = nu_ref[0]
    tme = xbuf.shape[1]
    run = tr_ref[i]
    first = jnp.logical_or(i == 0, run != tr_ref[jnp.maximum(i - 1, 0)])
    slot = run % EXPERT_W_BUFS

    def row_copy(t):
        s = t % EXPERT_ROW_BUFS
        start = t * tme if isinstance(t, int) else pl.multiple_of(t * tme, tme)
        src = xs_hbm.at[pl.ds(start, tme)]
        return pltpu.make_async_copy(src, xbuf.at[s], xsem.at[s])

    def weight_copies(r):
        e = re_ref[r]
        s = r % EXPERT_W_BUFS
        return (pltpu.make_async_copy(w1_hbm.at[e], w1s.at[s], sem.at[0, s]),
                pltpu.make_async_copy(w3_hbm.at[e], w3s.at[s], sem.at[1, s]),
                pltpu.make_async_copy(w2_hbm.at[e], w2s.at[s], sem.at[2, s]))

    @pl.when(i == 0)
    def _():
        for t in range(EXPERT_ROW_BUFS - 1):
            @pl.when(t < n_used)
            def _():
                row_copy(t).start()
        for r in range(EXPERT_W_BUFS - 1):
            @pl.when(r < nr_ref[0])
            def _():
                for cp in weight_copies(r):
                    cp.start()

    @pl.when(jnp.logical_and(first, i < n_used))
    def _():
        @pl.when(run + (EXPERT_W_BUFS - 1) < nr_ref[0])
        def _():
            for cp in weight_copies(run + (EXPERT_W_BUFS - 1)):
                cp.start()

        for cp in weight_copies(run):
            cp.wait()

    @pl.when(i < n_used)
    def _():
        @pl.when(i + (EXPERT_ROW_BUFS - 1) < n_used)
        def _():
            row_copy(i + (EXPERT_ROW_BUFS - 1)).start()

        row_copy(i).wait()
        lo, hi = _unpack_halves(xbuf[i % EXPERT_ROW_BUFS])
        lo = lo.astype(BF16)
        hi = hi.astype(BF16)
        half = D // 2
        a = _dot(lo, w1s[slot, 0:half, :]) + _dot(hi, w1s[slot, half:D, :])
        b = _dot(lo, w3s[slot, 0:half, :]) + _dot(hi, w3s[slot, half:D, :])
        z = (a * _sigmoid(a)) * b
        o_ref[...] = _pack_halves(_dot(z.astype(BF16), w2s[slot]))


def _experts(tile_run, run_e, n_runs, n_used, xs, w1, w3, w2, tme):
    n_slots = xs.shape[0]
    grid_spec = pltpu.PrefetchScalarGridSpec(
        num_scalar_prefetch=4,
        grid=(n_slots // tme,),
        in_specs=[pl.BlockSpec(memory_space=pl.ANY),
                  pl.BlockSpec(memory_space=pl.ANY),
                  pl.BlockSpec(memory_space=pl.ANY),
                  pl.BlockSpec(memory_space=pl.ANY)],
        out_specs=pl.BlockSpec((tme, D // 2),
                               lambda i, tr, re, nr, nu: (jnp.clip(i, 0, jnp.maximum(nu[0] - 1, 0)), 0)),
        scratch_shapes=[pltpu.VMEM((EXPERT_ROW_BUFS, tme, D // 2), jnp.int32),
                        pltpu.VMEM((EXPERT_W_BUFS, D, D_EXPERT), BF16),
                        pltpu.VMEM((EXPERT_W_BUFS, D, D_EXPERT), BF16),
                        pltpu.VMEM((EXPERT_W_BUFS, D_EXPERT, D), BF16),
                        pltpu.SemaphoreType.DMA((EXPERT_ROW_BUFS,)),
                        pltpu.SemaphoreType.DMA((3, EXPERT_W_BUFS))],
    )
    return pl.pallas_call(
        _expert_body,
        grid_spec=grid_spec,
        out_shape=jax.ShapeDtypeStruct((n_slots, D // 2), jnp.int32),
        compiler_params=_params(("arbitrary",)),
        name="experts",
    )(tile_run, run_e, n_runs, n_used, xs, w1, w3, w2)


def _final_body(x1_ref, mod_ref, y0_ref, y1_ref, wt_ref, gf_ref, o_ref):
    w0 = wt_ref[:, 0:1]
    w1 = wt_ref[:, 1:2]
    lo0, hi0 = _unpack_halves(y0_ref[...])
    lo1, hi1 = _unpack_halves(y1_ref[...])
    moe = jnp.concatenate([w0 * lo0 + w1 * lo1, w0 * hi0 + w1 * hi1], axis=1)
    x2 = x1_ref[...].astype(F32) + mod_ref[0, 5:6, :] * moe
    ms = jnp.mean(x2 * x2, axis=-1, keepdims=True)
    o_ref[...] = x2 * lax.rsqrt(ms + EPS) * gf_ref[...]


def _final(x1, mod, tiles_per_mod, yg, wts, g_final):
    n = x1.shape[0]
    tm = FIN_TM
    nt = n // tm
    return pl.pallas_call(
        _final_body,
        grid=(nt,),
        in_specs=[pl.BlockSpec((tm, D), lambda i: (i, 0)),
                  pl.BlockSpec((1, 6, D), lambda i: (i // tiles_per_mod, 0, 0)),
                  pl.BlockSpec((tm, D // 2), lambda i: (i, 0)),
                  pl.BlockSpec((tm, D // 2), lambda i: (i + nt, 0)),
                  pl.BlockSpec((tm, V7X_LANES), lambda i: (i, 0)),
                  _const_spec((1, D))],
        out_specs=pl.BlockSpec((tm, D), lambda i: (i, 0)),
        out_shape=jax.ShapeDtypeStruct((n, D), F32),
        compiler_params=_params(("parallel",)),
        name="final",
    )(x1, mod, yg, yg, wts, g_final)


def _slot_plan(route, cnt, n, tme):
    counts = cnt[:, 0]
    padded = ((counts + tme - 1) // tme) * tme
    pend = jnp.cumsum(padded)
    pstart = pend - padded
    onehot = route[0:2, :, None] == jnp.arange(N_EXPERTS, dtype=jnp.int32)[None, None, :]
    dest = jnp.sum(jnp.where(onehot, pstart[None, None, :], 0), axis=-1) + route[2:4]
    n_slots = ((2 * n + N_EXPERTS * (tme - 1)) // tme) * tme
    dest = jnp.clip(dest, 0, n_slots - 1)
    tile_start = jnp.arange(n_slots // tme, dtype=jnp.int32) * tme
    tile_e = jnp.sum((tile_start[:, None] >= pend[None, :]).astype(jnp.int32), axis=1)
    tile_e = jnp.minimum(tile_e, N_EXPERTS - 1)
    n_used = (pend[-1] // tme).astype(jnp.int32).reshape(1)
    used = counts > 0
    run_of_e = jnp.cumsum(used.astype(jnp.int32)) - 1
    eids = jnp.arange(N_EXPERTS, dtype=jnp.int32)
    run_e = jnp.sum(jnp.where(used[None, :] & (run_of_e[None, :] == eids[:, None]), eids[None, :], 0), axis=1)
    tile_run = jnp.sum(jnp.where(tile_e[:, None] == eids[None, :], run_of_e[None, :], 0), axis=1)
    n_runs = jnp.sum(used.astype(jnp.int32)).reshape(1)
    plan = (tile_run.astype(jnp.int32), run_e.astype(jnp.int32), n_runs, n_used)
    return dest.astype(jnp.int32), plan, n_slots


def _group(x, mod, mod_per_seq, h0, p, row_len, tme, cast_ws):
    n_b, seq_len, _ = x.shape
    n = n_b * seq_len
    xt = x.reshape(n, D)
    mod_seq = mod if mod_per_seq else jnp.broadcast_to(mod, (n_b, 6, D))
    xc = _xr_conv(x, mod_seq, p["g1"], p["w_in"], p["rnn_conv_w"], p["rnn_conv_b"])
    hf, hb, last = _rglru_scan(xc, h0, p["wg"], p["ba"], p["bx"], p["lam"])
    tiles_per_mod = (seq_len // MIX_TM) if mod_per_seq else (n // MIX_TM)
    outs = _mixer(xt, mod, tiles_per_mod, hf.reshape(n, D), hb.reshape(n, D),
                                     p["g1"], p["g2"], p["w_in"], p["conv_w"], p["wco"], p["wro"],
                                     p["wo"], p["wr"], p["br"], row_len, cast_ws)
    x1, h2, route, wts, cnt = outs[:5]
    casts = outs[5:]
    dest, plan, n_slots = _slot_plan(route, cnt, n, tme)
    xs = _sc_dispatch(h2, dest, n_slots)

    def finish(w1b, w3b, w2b):
        ys = _experts(*plan, xs, w1b, w3b, w2b, tme)
        yg = _sc_collect(ys, dest)
        tiles_per_mod_f = (seq_len // FIN_TM) if mod_per_seq else (n // FIN_TM)
        y = _final(x1, mod, tiles_per_mod_f, yg, wts, p["g_final"])
        return y.reshape(n_b, seq_len, D)

    return finish, last, casts


def kernel(x_prompt, x_sample, state_rnn, c, c_ctx, w_ada, b_ada, g_norm1, g_norm2, w_in, conv_w, w_conv_out, rnn_conv_w, rnn_conv_b, w_gate_a, b_gate_a, w_gate_x, b_gate_x, lam, w_rnn_out, w_o, w_router_group, b_router_group, w_router_expert, b_router_expert, w1, w3, w2, g_final):
    assert w_ada.shape[0] == 1, "single layer"
    n_pb, n_sb = x_prompt.shape[0], x_sample.shape[0]

    cond = jnp.concatenate([c_ctx[None, :], c, jnp.zeros((16 - 1 - n_sb, D), F32)], axis=0)
    mod = _ada(cond, w_ada[0], b_ada[0]).reshape(16, 6, D)

    w_in_b = w_in[0].astype(BF16)
    pad_w = jnp.zeros((D, EPG - N_GROUPS), F32)
    wr = jnp.concatenate([w_router_group[0], pad_w, w_router_expert[0],
                          jnp.zeros((D, ROUTE_PAD - EPG - N_EXPERTS), F32)], axis=1)
    br = jnp.concatenate([b_router_group[0], jnp.full((EPG - N_GROUPS,), NEG_BIG, F32),
                          b_router_expert[0],
                          jnp.zeros((ROUTE_PAD - EPG - N_EXPERTS,), F32)]).reshape(1, ROUTE_PAD)
    wr_hi = wr.astype(BF16)
    p = dict(
        g1=g_norm1, g2=g_norm2, w_in=w_in_b,
        conv_w=conv_w[0], rnn_conv_w=rnn_conv_w[0], rnn_conv_b=rnn_conv_b,
        wg=(0.5 * jnp.concatenate([w_gate_a[0], w_gate_x[0]], axis=-1)).astype(BF16),
        ba=b_gate_a[0], bx=b_gate_x[0], lam=lam[0],
        wco=w_conv_out[0].astype(BF16), wro=w_rnn_out[0].astype(BF16), wo=w_o[0].astype(BF16),
        wr=jnp.concatenate([wr_hi, (wr - wr_hi.astype(F32)).astype(BF16)], axis=1), br=br,
        g_final=g_final.reshape(1, D),
    )

    h0_p = jnp.zeros((n_pb // SCAN_B, 2, SCAN_B, D), F32)
    finish_p, last, (w2b,) = _group(x_prompt, mod[0:1], False, h0_p, p, x_prompt.shape[1], 512,
                                    [w2[0]])
    state_new = last.transpose(0, 2, 1, 3).reshape(n_pb, 1, 2, D)

    h0_s = state_rnn[:, 0].reshape(n_sb // SCAN_B, SCAN_B, 2, D).transpose(0, 2, 1, 3)
    finish_s, _, (w1b, w3b) = _group(x_sample, mod[1:1 + n_sb], True, h0_s, p, GRID_W, 512,
                                     [w1[0], w3[0]])
    return (finish_p(w1b, w3b, w2b), finish_s(w1b, w3b, w2b), state_new)
```

```python
import functools

import jax
import jax.numpy as jnp
from jax import lax
from jax.experimental import pallas as pl
from jax.experimental.pallas import tpu as pltpu
from jax.experimental.pallas import tpu_sc as plsc

D = 1024
N_HEADS = 8
HEAD = D // N_HEADS
GRID_W = 64
RG_C = 8.0
N_GROUPS = 4
EPG = 8
N_EXPERTS = N_GROUPS * EPG
D_EXPERT = 512
EPS = 1e-6
F32 = jnp.float32
BF16 = jnp.bfloat16

V7X_LANES = 128
V7X_SUBLANES = 8
V7X_VMEM_LIMIT_BYTES = 56 * 1024 * 1024
V7X_SC_CORES = 2
V7X_SC_SUBCORES = 16
V7X_SC_WORKERS = V7X_SC_CORES * V7X_SC_SUBCORES
SC_WINDOW = 64
SC_BUFS = 3

XR_T = 256
XR_SUB = 32
XR_MB = 4
XR_LEFT = 2
XR_COL = 3
SCAN_T = 128
SCAN_B = V7X_SUBLANES
LOG2_E = 1.4426950408889634
TINY = 1e-30
MIX_TM = 512
FIN_TM = 1024
EXPERT_ROW_BUFS = 3
EXPERT_W_BUFS = 3
ROUTE_PAD = 128
NEG_BIG = -1e30


def _sigmoid(x):
    return 0.5 * jnp.tanh(0.5 * x) + 0.5


def _norm_mod(x, g, scale, shift):
    ms = jnp.mean(x * x, axis=-1, keepdims=True)
    return (x * lax.rsqrt(ms + EPS)) * (g * (1.0 + scale)) + shift


def _dot(a, b):
    return jnp.dot(a, b, preferred_element_type=F32)


def _params(sem, vmem=V7X_VMEM_LIMIT_BYTES):
    return pltpu.CompilerParams(dimension_semantics=sem, vmem_limit_bytes=vmem)


def _const_spec(shape):
    zeros = (0,) * len(shape)
    return pl.BlockSpec(shape, lambda *_: zeros, pipeline_mode=pl.Buffered(1))


def _ada_body(c_ref, w_ref, b_ref, o_ref):
    c = c_ref[...]
    s = (c * _sigmoid(c)).astype(BF16)
    o_ref[...] = _dot(s, w_ref[...].astype(BF16)) + b_ref[...]


def _ada(cond, w_ada, b_ada):
    rows = cond.shape[0]
    n_out = w_ada.shape[1]
    return pl.pallas_call(
        _ada_body,
        grid=(n_out // D,),
        in_specs=[pl.BlockSpec((rows, D), lambda i: (0, 0)),
                  pl.BlockSpec((D, D), lambda i: (0, i)),
                  pl.BlockSpec((1, D), lambda i: (0, i))],
        out_specs=pl.BlockSpec((rows, D), lambda i: (0, i)),
        out_shape=jax.ShapeDtypeStruct((rows, n_out), F32),
        compiler_params=_params(("parallel",)),
        name="ada",
    )(cond, w_ada, b_ada.reshape(1, n_out))


def _xr_body(x_ref, xn_ref, mod_ref, g_ref, w_ref, cw_ref, cb_ref, o_ref, xt):
    j = pl.program_id(1)
    n_t = pl.num_programs(1)
    t_len = x_ref.shape[1]
    nb = SCAN_B
    body0 = XR_LEFT * nb

    @pl.when(j == 0)
    def _():
        xt[:, 0:body0, :] = jnp.zeros((N_HEADS, body0, HEAD), F32)

    @pl.when(j > 0)
    def _():
        xt[:, 0:body0, :] = xt[:, t_len * nb:t_len * nb + body0, :]

    g = g_ref[...]
    for b0 in range(0, nb, XR_MB):
        h = jnp.concatenate(
            [_norm_mod(x_ref[b], g, mod_ref[b, 1:2, :], mod_ref[b, 0:1, :]).astype(BF16)
             for b in range(b0, b0 + XR_MB)], axis=0)
        r = _dot(h, w_ref[...])
        for i in range(XR_MB):
            for s in range(N_HEADS):
                xt[s, pl.ds(body0 + b0 + i, t_len, stride=nb), :] = (
                    r[i * t_len:(i + 1) * t_len, s * HEAD:(s + 1) * HEAD])

    hn = _norm_mod(xn_ref[:, 0, :], g, mod_ref[:, 1, :], mod_ref[:, 0, :]).astype(BF16)
    rn = jnp.where(j < n_t - 1, _dot(hn, w_ref[...]), 0.0)
    tail = body0 + t_len * nb
    for s in range(N_HEADS):
        xt[s, tail:tail + nb, :] = rn[:, s * HEAD:(s + 1) * HEAD]

    sub = XR_SUB
    for s in range(N_HEADS):
        sl = slice(s * HEAD, (s + 1) * HEAD)
        for t0 in range(0, t_len, sub):
            y = cb_ref[:, sl]
            for k in range(4):
                r0 = (t0 + k) * nb
                y = y + xt[s, r0:r0 + sub * nb, :] * cw_ref[k:k + 1, sl]
            o_ref[t0:t0 + sub, :, sl] = y.reshape(sub, nb, HEAD)


def _xr_conv(x, mod_seq, g1, w_in, cw, cb):
    n_b, seq_len, _ = x.shape
    t_len = min(XR_T, seq_len)
    assert n_b % SCAN_B == 0 and seq_len % t_len == 0 and t_len % XR_SUB == 0
    n_t = seq_len // t_len
    nxt = t_len // V7X_SUBLANES
    last_blk = seq_len // V7X_SUBLANES - 1
    return pl.pallas_call(
        _xr_body,
        grid=(n_b // SCAN_B, n_t),
        in_specs=[pl.BlockSpec((SCAN_B, t_len, D), lambda g, j: (g, j, 0)),
                  pl.BlockSpec((SCAN_B, V7X_SUBLANES, D),
                               lambda g, j: (g, jnp.minimum((j + 1) * nxt, last_blk), 0)),
                  pl.BlockSpec((SCAN_B, 6, D), lambda g, j: (g, 0, 0)),
                  _const_spec((1, D)),
                  pl.BlockSpec((D, D), lambda g, j: (0, XR_COL), pipeline_mode=pl.Buffered(1)),
                  _const_spec((4, D)),
                  _const_spec((1, D))],
        out_specs=pl.BlockSpec((t_len, SCAN_B, D), lambda g, j: (j, g, 0)),
        out_shape=jax.ShapeDtypeStruct((seq_len, n_b, D), F32),
        scratch_shapes=[pltpu.VMEM((N_HEADS, (t_len + XR_LEFT + 1) * SCAN_B, HEAD), F32)],
        compiler_params=_params(("parallel", "arbitrary")),
        name="xr_conv",
    )(x, x, mod_seq, g1, w_in, cw, cb)


def _scan_body(xf_ref, xb_ref, h0_ref, wg_ref, ba_ref, bx_ref, lam_ref,
               hf_ref, hb_ref, last_ref, a_s, u_s, h_s, hc):
    j = pl.program_id(1)
    n_t = pl.num_programs(1)
    t_len = xf_ref.shape[0]
    rows = SCAN_B * t_len

    @pl.when(j == 0)
    def _():
        for d in range(2):
            for s in range(N_HEADS):
                hc[d, s] = h0_ref[0, d, :, s * HEAD:(s + 1) * HEAD]

    c2s = []
    for d in range(2):
        z = -lam_ref[d:d + 1, :]
        sp = jnp.maximum(z, 0.0) + jnp.log(1.0 + jnp.exp(-jnp.abs(z)))
        c2s.append((-0.5 * RG_C * LOG2_E) * sp)

    for hd in range(N_HEADS):
        sl = slice(hd * HEAD, (hd + 1) * HEAD)
        for d, x_ref in ((0, xf_ref), (1, xb_ref)):
            xh = x_ref[:, :, sl].reshape(rows, HEAD)
            g = _dot(xh.astype(BF16), wg_ref[d, hd])
            t_r = jnp.tanh(g[:, :HEAD] + 0.5 * ba_ref[d:d + 1, sl])
            t_i = jnp.tanh(g[:, HEAD:] + 0.5 * bx_ref[d:d + 1, sl])
            a = jnp.exp2(c2s[d][:, sl] * t_r + c2s[d][:, sl])
            q = 0.25 - 0.25 * (a * a)
            half_mult = q * lax.rsqrt(jnp.maximum(q, TINY))
            u = half_mult * ((t_i + 1.0) * xh)
            a_s[d, hd] = a
            u_s[d, hd] = u
        h_f = hc[0, hd]
        h_b = hc[1, hd]
        for t in range(t_len):
            rf = t * SCAN_B
            rb = (t_len - 1 - t) * SCAN_B
            h_f = a_s[0, hd, rf:rf + SCAN_B, :] * h_f + u_s[0, hd, rf:rf + SCAN_B, :]
            h_b = a_s[1, hd, rb:rb + SCAN_B, :] * h_b + u_s[1, hd, rb:rb + SCAN_B, :]
            h_s[0, hd, rf:rf + SCAN_B, :] = h_f
            h_s[1, hd, rb:rb + SCAN_B, :] = h_b
        hc[0, hd] = h_f
        hc[1, hd] = h_b

    for d, o_ref in ((0, hf_ref), (1, hb_ref)):
        for hd in range(N_HEADS):
            for b in range(SCAN_B):
                o_ref[b, :, hd * HEAD:(hd + 1) * HEAD] = (
                    h_s[d, hd, pl.ds(b, t_len, stride=SCAN_B), :].astype(BF16))

    @pl.when(j == n_t - 1)
    def _():
        for d in range(2):
            for s in range(N_HEADS):
                last_ref[0, d, :, s * HEAD:(s + 1) * HEAD] = hc[d, s]


def _rglru_scan(xc, h0, wg, ba, bx, lam):
    seq_len, n_b, _ = xc.shape
    n_g = n_b // SCAN_B
    n_t = seq_len // SCAN_T
    blk = (SCAN_B, SCAN_T, D)
    blk_in = (SCAN_T, SCAN_B, D)
    return pl.pallas_call(
        _scan_body,
        grid=(n_g, n_t),
        in_specs=[pl.BlockSpec(blk_in, lambda g, j: (j, g, 0)),
                  pl.BlockSpec(blk_in, lambda g, j: (n_t - 1 - j, g, 0)),
                  pl.BlockSpec((1, 2, SCAN_B, D), lambda g, j: (g, 0, 0, 0)),
                  _const_spec((2, N_HEADS, HEAD, 2 * HEAD)),
                  _const_spec((2, D)),
                  _const_spec((2, D)),
                  _const_spec((2, D))],
        out_specs=[pl.BlockSpec(blk, lambda g, j: (g, j, 0)),
                   pl.BlockSpec(blk, lambda g, j: (g, n_t - 1 - j, 0)),
                   pl.BlockSpec((1, 2, SCAN_B, D), lambda g, j: (g, 0, 0, 0))],
        out_shape=[jax.ShapeDtypeStruct((n_b, seq_len, D), BF16),
                   jax.ShapeDtypeStruct((n_b, seq_len, D), BF16),
                   jax.ShapeDtypeStruct((n_g, 2, SCAN_B, D), F32)],
        scratch_shapes=[pltpu.VMEM((2, N_HEADS, SCAN_T * SCAN_B, HEAD), F32),
                        pltpu.VMEM((2, N_HEADS, SCAN_T * SCAN_B, HEAD), F32),
                        pltpu.VMEM((2, N_HEADS, SCAN_T * SCAN_B, HEAD), F32),
                        pltpu.VMEM((2, N_HEADS, SCAN_B, HEAD), F32)],
        compiler_params=_params(("parallel", "arbitrary")),
        name="rglru_scan",
    )(xc, xc, h0, wg, ba, bx, lam)


def _pack_halves(v):
    half = v.shape[1] // 2
    lo = lax.bitcast_convert_type(v[:, :half].astype(BF16).astype(F32), jnp.uint32)
    hi = lax.bitcast_convert_type(v[:, half:].astype(BF16).astype(F32), jnp.uint32)
    return lax.bitcast_convert_type((lo >> 16) | (hi & jnp.uint32(0xFFFF0000)), jnp.int32)


def _unpack_halves(p):
    u = lax.bitcast_convert_type(p, jnp.uint32)
    lo = lax.bitcast_convert_type(u << 16, F32)
    hi = lax.bitcast_convert_type(u & jnp.uint32(0xFFFF0000), F32)
    return lo, hi


def _mix_body(x_ref, mod_ref, hf_ref, hb_ref, g1_ref, g2_ref, win_ref, cw_ref,
              wco_ref, wro_ref, wo_ref, wr_ref, br_ref, *rest, row_len, n_cast):
    cast_in = rest[:n_cast]
    x1_ref, h2_ref, route_ref, wts_ref, cnt_ref = rest[n_cast:n_cast + 5]
    cast_out = rest[n_cast + 5:2 * n_cast + 5]
    seen = rest[2 * n_cast + 5]
    x = x_ref[...]
    tm = x.shape[0]
    h = _norm_mod(x, g1_ref[...], mod_ref[0, 1:2, :], mod_ref[0, 0:1, :]).astype(BF16)

    def proj(k):
        return _dot(h, win_ref[:, k * D:(k + 1) * D])

    cv = proj(1) * proj(2)
    pos = lax.broadcasted_iota(jnp.int32, (tm, 1), 0) % row_len
    conv = cv * cw_ref[1:2, :]
    conv = conv + jnp.where(pos >= 1, pltpu.roll(cv, 1, 0), 0.0) * cw_ref[0:1, :]
    conv = conv + jnp.where(pos <= row_len - 2, pltpu.roll(cv, tm - 1, 0), 0.0) * cw_ref[2:3, :]
    y_a = _dot((proj(0) * conv).astype(BF16), wco_ref[...])
    merged = _sigmoid(proj(5)) * y_a

    hs = hf_ref[...].astype(F32) + hb_ref[...].astype(F32)
    y_b = _dot((hs * jax.nn.gelu(proj(4))).astype(BF16), wro_ref[...])
    merged = merged + _sigmoid(proj(6)) * y_b

    mix = _dot(merged.astype(BF16), wo_ref[...])
    x1 = x + mod_ref[0, 2:3, :] * mix
    x1_ref[...] = x1.astype(BF16)
    for src, dst in zip(cast_in, cast_out):
        dst[...] = src[...].astype(BF16)
    h2 = _norm_mod(x1, g2_ref[...], mod_ref[0, 4:5, :], mod_ref[0, 3:4, :])
    h2_ref[...] = _pack_halves(h2)

    h2_hi = h2.astype(BF16)
    h2_lo = (h2 - h2_hi.astype(F32)).astype(BF16)
    big = _dot(h2_hi, wr_ref[...])
    logits = (big[:, :ROUTE_PAD] + big[:, ROUTE_PAD:]
              + _dot(h2_lo, wr_ref[:, :ROUTE_PAD]) + br_ref[...])
    lt = logits.T
    row = lax.broadcasted_iota(jnp.int32, (EPG, tm), 0)
    lg = lt[0:EPG]
    mg = jnp.max(lg, axis=0, keepdims=True)
    p_grp = 1.0 / jnp.sum(jnp.exp(lg - mg), axis=0, keepdims=True)
    grp = jnp.min(jnp.where(lg == mg, row, EPG), axis=0, keepdims=True)
    le = lt[EPG * N_GROUPS:EPG * (N_GROUPS + 1)]
    for g in range(N_GROUPS - 2, -1, -1):
        le = jnp.where(grp == g, lt[EPG * (g + 1):EPG * (g + 2)], le)
    me = jnp.max(le, axis=0, keepdims=True)
    ee = jnp.exp(le - me)
    pe = ee / jnp.sum(ee, axis=0, keepdims=True)
    p1 = jnp.max(pe, axis=0, keepdims=True)
    i1 = jnp.min(jnp.where(pe == p1, row, EPG), axis=0, keepdims=True)
    pe2 = jnp.where(row == i1, -1.0, pe)
    p2 = jnp.max(pe2, axis=0, keepdims=True)
    i2 = jnp.min(jnp.where(pe2 == p2, row, EPG), axis=0, keepdims=True)
    den = p1 + p2
    e1 = grp * EPG + i1
    e2 = grp * EPG + i2

    @pl.when(pl.program_id(0) == 0)
    def _():
        seen[...] = jnp.zeros_like(seen)

    erow = lax.broadcasted_iota(jnp.int32, (N_EXPERTS, tm), 0)
    hit1 = erow == e1
    hit2 = erow == e2
    both = jnp.where(jnp.logical_or(hit1, hit2), 1.0, 0.0)
    tri = jnp.where(lax.broadcasted_iota(jnp.int32, (tm, tm), 0)
                    <= lax.broadcasted_iota(jnp.int32, (tm, tm), 1), 1.0, 0.0).astype(BF16)
    before = _dot(both.astype(BF16), tri) - both + seen[...]
    r1 = jnp.sum(jnp.where(hit1, before, 0.0), axis=0, keepdims=True).astype(jnp.int32)
    r2 = jnp.sum(jnp.where(hit2, before, 0.0), axis=0, keepdims=True).astype(jnp.int32)
    total = seen[...] + jnp.sum(both, axis=1, keepdims=True)
    seen[...] = total
    cnt_ref[...] = total[:, :V7X_LANES].astype(jnp.int32)

    route_ref[...] = jnp.where(row == 0, e1, jnp.where(row == 1, e2, jnp.where(
        row == 2, r1, jnp.where(row == 3, r2, 0))))
    w8 = jnp.where(row == 0, p_grp * p1 / den, jnp.where(row == 1, p_grp * p2 / den, 0.0))
    wts_ref[...] = jnp.concatenate([w8, jnp.zeros((V7X_LANES - EPG, tm), F32)], axis=0).T


def _mixer(x, mod, tiles_per_mod, hf, hb, g1, g2, w_in, cw, wco, wro, wo, wr, br, row_len, cast_ws):
    n = x.shape[0]
    tm = MIX_TM
    assert tm % row_len == 0 and n % tm == 0
    steps = n // tm
    assert N_EXPERTS % steps == 0
    epb = N_EXPERTS // steps
    cast_specs = [pl.BlockSpec((epb,) + w.shape[1:], lambda i: (i, 0, 0)) for w in cast_ws]
    cast_shapes = [jax.ShapeDtypeStruct(w.shape, BF16) for w in cast_ws]
    mod_map = lambda i: (i // tiles_per_mod, 0, 0)
    tok = lambda i: (i, 0)
    col = lambda i: (0, i)
    return pl.pallas_call(
        functools.partial(_mix_body, row_len=row_len, n_cast=len(cast_ws)),
        grid=(steps,),
        in_specs=[pl.BlockSpec((tm, D), tok),
                  pl.BlockSpec((1, 6, D), mod_map),
                  pl.BlockSpec((tm, D), tok),
                  pl.BlockSpec((tm, D), tok),
                  _const_spec((1, D)),
                  _const_spec((1, D)),
                  _const_spec(w_in.shape),
                  _const_spec((3, D)),
                  _const_spec((D, D)),
                  _const_spec((D, D)),
                  _const_spec((D, D)),
                  _const_spec((D, 2 * ROUTE_PAD)),
                  _const_spec((1, ROUTE_PAD))] + cast_specs,
        out_specs=[pl.BlockSpec((tm, D), tok),
                   pl.BlockSpec((tm, D // 2), tok),
                   pl.BlockSpec((EPG, tm), col),
                   pl.BlockSpec((tm, V7X_LANES), tok),
                   pl.BlockSpec((N_EXPERTS, V7X_LANES), lambda i: (0, 0))] + cast_specs,
        out_shape=[jax.ShapeDtypeStruct((n, D), BF16),
                   jax.ShapeDtypeStruct((n, D // 2), jnp.int32),
                   jax.ShapeDtypeStruct((EPG, n), jnp.int32),
                   jax.ShapeDtypeStruct((n, V7X_LANES), F32),
                   jax.ShapeDtypeStruct((N_EXPERTS, V7X_LANES), jnp.int32)] + cast_shapes,
        scratch_shapes=[pltpu.VMEM((N_EXPERTS, tm), F32)],
        compiler_params=_params(("arbitrary",)),
        name="mixer",
    )(x, mod, hf, hb, g1, g2, w_in, cw, wco, wro, wo, wr, br, *cast_ws)


def _sc_mesh():
    return plsc.VectorSubcoreMesh(core_axis_name="c", subcore_axis_name="s",
                                  num_cores=V7X_SC_CORES, num_subcores=V7X_SC_SUBCORES)


def _sc_worker_id():
    return lax.axis_index("s") * V7X_SC_CORES + lax.axis_index("c")


def _sc_dispatch(rows, dest, n_slots):
    n, width = rows.shape
    per_w = n // V7X_SC_WORKERS
    n_ch = per_w // SC_WINDOW
    assert n_ch * SC_WINDOW * V7X_SC_WORKERS == n
    idx = dest.reshape(2, V7X_SC_WORKERS, n_ch, SC_WINDOW).transpose(1, 0, 2, 3)

    def body(x_hbm, d_hbm, o_hbm, idx_v, buf, ld_sem, st_sem):
        wid = _sc_worker_id()
        pltpu.sync_copy(d_hbm.at[wid], idx_v)

        def load(j):
            src = x_hbm.at[pl.ds(wid * per_w + j * SC_WINDOW, SC_WINDOW)]
            return pltpu.async_copy(src, buf.at[j % SC_BUFS], ld_sem.at[j % SC_BUFS])

        def scatter(j):
            return [pltpu.async_copy(buf.at[j % SC_BUFS], o_hbm.at[idx_v.at[k, j]], st_sem.at[j % SC_BUFS])
                    for k in range(2)]

        loads = {j: load(j) for j in range(min(SC_BUFS - 1, n_ch))}
        stores, waited = {}, set()
        for j in range(n_ch):
            loads[j].wait()
            nxt = j + SC_BUFS - 1
            if nxt < n_ch:
                if nxt - SC_BUFS >= 0:
                    for cp in stores[nxt - SC_BUFS]:
                        cp.wait()
                    waited.add(nxt - SC_BUFS)
                loads[nxt] = load(nxt)
            stores[j] = scatter(j)
        for j in range(n_ch):
            if j not in waited:
                for cp in stores[j]:
                    cp.wait()

    return pl.kernel(
        body,
        out_type=jax.ShapeDtypeStruct((n_slots, width), jnp.int32),
        mesh=_sc_mesh(),
        scratch_types=[pltpu.VMEM((2, n_ch, SC_WINDOW), jnp.int32),
                       pltpu.VMEM((SC_BUFS, SC_WINDOW, width), jnp.int32),
                       pltpu.SemaphoreType.DMA((SC_BUFS,)),
                       pltpu.SemaphoreType.DMA((SC_BUFS,))],
        name="sc_dispatch",
    )(rows, idx)


def _sc_collect(rows, dest):
    n = dest.shape[1]
    width = rows.shape[1]
    per_w = n // V7X_SC_WORKERS
    n_ch = per_w // SC_WINDOW
    assert n_ch * SC_WINDOW * V7X_SC_WORKERS == n
    idx = dest.reshape(2, V7X_SC_WORKERS, n_ch, SC_WINDOW).transpose(1, 0, 2, 3)
    windows = [(k, j) for k in range(2) for j in range(n_ch)]

    def body(y_hbm, d_hbm, o_hbm, idx_v, buf, ld_sem, st_sem):
        wid = _sc_worker_id()
        pltpu.sync_copy(d_hbm.at[wid], idx_v)

        def gather(c):
            k, j = windows[c]
            return pltpu.async_copy(y_hbm.at[idx_v.at[k, j]], buf.at[c % SC_BUFS], ld_sem.at[c % SC_BUFS])

        def store(c):
            k, j = windows[c]
            dst = o_hbm.at[pl.ds(k * n + wid * per_w + j * SC_WINDOW, SC_WINDOW)]
            return pltpu.async_copy(buf.at[c % SC_BUFS], dst, st_sem.at[c % SC_BUFS])

        n_win = len(windows)
        loads = {c: gather(c) for c in range(min(SC_BUFS - 1, n_win))}
        stores, waited = {}, set()
        for c in range(n_win):
            loads[c].wait()
            nxt = c + SC_BUFS - 1
            if nxt < n_win:
                if nxt - SC_BUFS >= 0:
                    stores[nxt - SC_BUFS].wait()
                    waited.add(nxt - SC_BUFS)
                loads[nxt] = gather(nxt)
            stores[c] = store(c)
        for c in range(n_win):
            if c not in waited:
                stores[c].wait()

    return pl.kernel(
        body,
        out_type=jax.ShapeDtypeStruct((2 * n, width), jnp.int32),
        mesh=_sc_mesh(),
        scratch_types=[pltpu.VMEM((2, n_ch, SC_WINDOW), jnp.int32),
                       pltpu.VMEM((SC_BUFS, SC_WINDOW, width), jnp.int32),
                       pltpu.SemaphoreType.DMA((SC_BUFS,)),
                       pltpu.SemaphoreType.DMA((SC_BUFS,))],
        name="sc_collect",
    )(rows, idx)


def _expert_body(tr_ref, re_ref, nr_ref, nu_ref, xs_hbm, w1_hbm, w3_hbm, w2_hbm, o_ref,
                 xbuf, w1s, w3s, w2s, xsem, sem):
    i = pl.program_id(0)
    n_used = nu_ref[0]
    tme = xbuf.shape[1]
    run = tr_ref[i]
    first = jnp.logical_or(i == 0, run != tr_ref[jnp.maximum(i - 1, 0)])
    slot = run % EXPERT_W_BUFS

    def row_copy(t):
        s = t % EXPERT_ROW_BUFS
        start = t * tme if isinstance(t, int) else pl.multiple_of(t * tme, tme)
        src = xs_hbm.at[pl.ds(start, tme)]
        return pltpu.make_async_copy(src, xbuf.at[s], xsem.at[s])

    def weight_copies(r):
        e = re_ref[r]
        s = r % EXPERT_W_BUFS
        return (pltpu.make_async_copy(w1_hbm.at[e], w1s.at[s], sem.at[0, s]),
                pltpu.make_async_copy(w3_hbm.at[e], w3s.at[s], sem.at[1, s]),
                pltpu.make_async_copy(w2_hbm.at[e], w2s.at[s], sem.at[2, s]))

    @pl.when(i == 0)
    def _():
        for t in range(EXPERT_ROW_BUFS - 1):
            @pl.when(t < n_used)
            def _():
                row_copy(t).start()
        for r in range(EXPERT_W_BUFS - 1):
            @pl.when(r < nr_ref[0])
            def _():
                for cp in weight_copies(r):
                    cp.start()

    @pl.when(jnp.logical_and(first, i < n_used))
    def _():
        @pl.when(run + (EXPERT_W_BUFS - 1) < nr_ref[0])
        def _():
            for cp in weight_copies(run + (EXPERT_W_BUFS - 1)):
                cp.start()

        for cp in weight_copies(run):
            cp.wait()

    @pl.when(i < n_used)
    def _():
        @pl.when(i + (EXPERT_ROW_BUFS - 1) < n_used)
        def _():
            row_copy(i + (EXPERT_ROW_BUFS - 1)).start()

        row_copy(i).wait()
        lo, hi = _unpack_halves(xbuf[i % EXPERT_ROW_BUFS])
        lo = lo.astype(BF16)
        hi = hi.astype(BF16)
        half = D // 2
        a = _dot(lo, w1s[slot, 0:half, :]) + _dot(hi, w1s[slot, half:D, :])
        b = _dot(lo, w3s[slot, 0:half, :]) + _dot(hi, w3s[slot, half:D, :])
        z = (a * _sigmoid(a)) * b
        o_ref[...] = _pack_halves(_dot(z.astype(BF16), w2s[slot]))


def _experts(tile_run, run_e, n_runs, n_used, xs, w1, w3, w2, tme):
    n_slots = xs.shape[0]
    grid_spec = pltpu.PrefetchScalarGridSpec(
        num_scalar_prefetch=4,
        grid=(n_slots // tme,),
        in_specs=[pl.BlockSpec(memory_space=pl.ANY),
                  pl.BlockSpec(memory_space=pl.ANY),
                  pl.BlockSpec(memory_space=pl.ANY),
                  pl.BlockSpec(memory_space=pl.ANY)],
        out_specs=pl.BlockSpec((tme, D // 2),
                               lambda i, tr, re, nr, nu: (jnp.clip(i, 0, jnp.maximum(nu[0] - 1, 0)), 0)),
        scratch_shapes=[pltpu.VMEM((EXPERT_ROW_BUFS, tme, D // 2), jnp.int32),
                        pltpu.VMEM((EXPERT_W_BUFS, D, D_EXPERT), BF16),
                        pltpu.VMEM((EXPERT_W_BUFS, D, D_EXPERT), BF16),
                        pltpu.VMEM((EXPERT_W_BUFS, D_EXPERT, D), BF16),
                        pltpu.SemaphoreType.DMA((EXPERT_ROW_BUFS,)),
                        pltpu.SemaphoreType.DMA((3, EXPERT_W_BUFS))],
    )
    return pl.pallas_call(
        _expert_body,
        grid_spec=grid_spec,
        out_shape=jax.ShapeDtypeStruct((n_slots, D // 2), jnp.int32),
        compiler_params=_params(("arbitrary",)),
        name="experts",
    )(tile_run, run_e, n_runs, n_used, xs, w1, w3, w2)


def _final_body(x1_ref, mod_ref, y0_ref, y1_ref, wt_ref, gf_ref, o_ref):
    w0 = wt_ref[:, 0:1]
    w1 = wt_ref[:, 1:2]
    lo0, hi0 = _unpack_halves(y0_ref[...])
    lo1, hi1 = _unpack_halves(y1_ref[...])
    moe = jnp.concatenate([w0 * lo0 + w1 * lo1, w0 * hi0 + w1 * hi1], axis=1)
    x2 = x1_ref[...].astype(F32) + mod_ref[0, 5:6, :] * moe
    ms = jnp.mean(x2 * x2, axis=-1, keepdims=True)
    o_ref[...] = x2 * lax.rsqrt(ms + EPS) * gf_ref[...]


def _final(x1, mod, tiles_per_mod, yg, wts, g_final):
    n = x1.shape[0]
    tm = FIN_TM
    nt = n // tm
    return pl.pallas_call(
        _final_body,
        grid=(nt,),
        in_specs=[pl.BlockSpec((tm, D), lambda i: (i, 0)),
                  pl.BlockSpec((1, 6, D), lambda i: (i // tiles_per_mod, 0, 0)),
                  pl.BlockSpec((tm, D // 2), lambda i: (i, 0)),
                  pl.BlockSpec((tm, D // 2), lambda i: (i + nt, 0)),
                  pl.BlockSpec((tm, V7X_LANES), lambda i: (i, 0)),
                  _const_spec((1, D))],
        out_specs=pl.BlockSpec((tm, D), lambda i: (i, 0)),
        out_shape=jax.ShapeDtypeStruct((n, D), F32),
        compiler_params=_params(("parallel",)),
        name="final",
    )(x1, mod, yg, yg, wts, g_final)


def _slot_plan(route, cnt, n, tme):
    counts = cnt[:, 0]
    padded = ((counts + tme - 1) // tme) * tme
    pend = jnp.cumsum(padded)
    pstart = pend - padded
    onehot = route[0:2, :, None] == jnp.arange(N_EXPERTS, dtype=jnp.int32)[None, None, :]
    dest = jnp.sum(jnp.where(onehot, pstart[None, None, :], 0), axis=-1) + route[2:4]
    n_slots = ((2 * n + N_EXPERTS * (tme - 1)) // tme) * tme
    dest = jnp.clip(dest, 0, n_slots - 1)
    tile_start = jnp.arange(n_slots // tme, dtype=jnp.int32) * tme
    tile_e = jnp.sum((tile_start[:, None] >= pend[None, :]).astype(jnp.int32), axis=1)
    tile_e = jnp.minimum(tile_e, N_EXPERTS - 1)
    n_used = (pend[-1] // tme).astype(jnp.int32).reshape(1)
    used = counts > 0
    run_of_e = jnp.cumsum(used.astype(jnp.int32)) - 1
    eids = jnp.arange(N_EXPERTS, dtype=jnp.int32)
    run_e = jnp.sum(jnp.where(used[None, :] & (run_of_e[None, :] == eids[:, None]), eids[None, :], 0), axis=1)
    tile_run = jnp.sum(jnp.where(tile_e[:, None] == eids[None, :], run_of_e[None, :], 0), axis=1)
    n_runs = jnp.sum(used.astype(jnp.int32)).reshape(1)
    plan = (tile_run.astype(jnp.int32), run_e.astype(jnp.int32), n_runs, n_used)
    return dest.astype(jnp.int32), plan, n_slots


def _group(x, mod, mod_per_seq, h0, p, row_len, tme, cast_ws):
    n_b, seq_len, _ = x.shape
    n = n_b * seq_len
    xt = x.reshape(n, D)
    mod_seq = mod if mod_per_seq else jnp.broadcast_to(mod, (n_b, 6, D))
    xc = _xr_conv(x, mod_seq, p["g1"], p["w_in"], p["rnn_conv_w"], p["rnn_conv_b"])
    hf, hb, last = _rglru_scan(xc, h0, p["wg"], p["ba"], p["bx"], p["lam"])
    tiles_per_mod = (seq_len // MIX_TM) if mod_per_seq else (n // MIX_TM)
    outs = _mixer(xt, mod, tiles_per_mod, hf.reshape(n, D), hb.reshape(n, D),
                                     p["g1"], p["g2"], p["w_in"], p["conv_w"], p["wco"], p["wro"],
                                     p["wo"], p["wr"], p["br"], row_len, cast_ws)
    x1, h2, route, wts, cnt = outs[:5]
    casts = outs[5:]
    dest, plan, n_slots = _slot_plan(route, cnt, n, tme)
    xs = _sc_dispatch(h2, dest, n_slots)

    def finish(w1b, w3b, w2b):
        ys = _experts(*plan, xs, w1b, w3b, w2b, tme)
        yg = _sc_collect(ys, dest)
        tiles_per_mod_f = (seq_len // FIN_TM) if mod_per_seq else (n // FIN_TM)
        y = _final(x1, mod, tiles_per_mod_f, yg, wts, p["g_final"])
        return y.reshape(n_b, seq_len, D)

    return finish, last, casts


def kernel(x_prompt, x_sample, state_rnn, c, c_ctx, w_ada, b_ada, g_norm1, g_norm2, w_in, conv_w, w_conv_out, rnn_conv_w, rnn_conv_b, w_gate_a, b_gate_a, w_gate_x, b_gate_x, lam, w_rnn_out, w_o, w_router_group, b_router_group, w_router_expert, b_router_expert, w1, w3, w2, g_final):
    assert w_ada.shape[0] == 1, "single layer"
    n_pb, n_sb = x_prompt.shape[0], x_sample.shape[0]

    cond = jnp.concatenate([c_ctx[None, :], c, jnp.zeros((16 - 1 - n_sb, D), F32)], axis=0)
    mod = _ada(cond, w_ada[0], b_ada[0]).reshape(16, 6, D)

    w_in_b = w_in[0].astype(BF16)
    pad_w = jnp.zeros((D, EPG - N_GROUPS), F32)
    wr = jnp.concatenate([w_router_group[0], pad_w, w_router_expert[0],
                          jnp.zeros((D, ROUTE_PAD - EPG - N_EXPERTS), F32)], axis=1)
    br = jnp.concatenate([b_router_group[0], jnp.full((EPG - N_GROUPS,), NEG_BIG, F32),
                          b_router_expert[0],
                          jnp.zeros((ROUTE_PAD - EPG - N_EXPERTS,), F32)]).reshape(1, ROUTE_PAD)
    wr_hi = wr.astype(BF16)
    p = dict(
        g1=g_norm1, g2=g_norm2, w_in=w_in_b,
        conv_w=conv_w[0], rnn_conv_w=rnn_conv_w[0], rnn_conv_b=rnn_conv_b,
        wg=(0.5 * jnp.concatenate([w_gate_a[0], w_gate_x[0]], axis=-1)).astype(BF16),
        ba=b_gate_a[0], bx=b_gate_x[0], lam=lam[0],
        wco=w_conv_out[0].astype(BF16), wro=w_rnn_out[0].astype(BF16), wo=w_o[0].astype(BF16),
        wr=jnp.concatenate([wr_hi, (wr - wr_hi.astype(F32)).astype(BF16)], axis=1), br=br,
        g_final=g_final.reshape(1, D),
    )

    h0_p = jnp.zeros((n_pb // SCAN_B, 2, SCAN_B, D), F32)
    finish_p, last, (w2b,) = _group(x_prompt, mod[0:1], False, h0_p, p, x_prompt.shape[1], 512,
                                    [w2[0]])
    state_new = last.transpose(0, 2, 1, 3).reshape(n_pb, 1, 2, D)

    h0_s = state_rnn[:, 0].reshape(n_sb // SCAN_B, SCAN_B, 2, D).transpose(0, 2, 1, 3)
    finish_s, _, (w1b, w3b) = _group(x_sample, mod[1:1 + n_sb], True, h0_s, p, GRID_W, 512,
                                     [w1[0], w3[0]])
    return (finish_p(w1b, w3b, w2b), finish_s(w1b, w3b, w2b), state_new)
```

```python
import functools

import jax
import jax.numpy as jnp
from jax import lax
from jax.experimental import pallas as pl
from jax.experimental.pallas import tpu as pltpu
from jax.experimental.pallas import tpu_sc as plsc

D = 1024
N_HEADS = 8
HEAD = D // N_HEADS
GRID_W = 64
RG_C = 8.0
N_GROUPS = 4
EPG = 8
N_EXPERTS = N_GROUPS * EPG
D_EXPERT = 512
EPS = 1e-6
F32 = jnp.float32
BF16 = jnp.bfloat16

V7X_LANES = 128
V7X_SUBLANES = 8
V7X_VMEM_LIMIT_BYTES = 56 * 1024 * 1024
V7X_SC_CORES = 2
V7X_SC_SUBCORES = 16
V7X_SC_WORKERS = V7X_SC_CORES * V7X_SC_SUBCORES
SC_WINDOW = 32
SC_BUFS = 6

XR_T = 256
XR_SUB = 32
XR_MB = 4
XR_LEFT = 2
XR_COL = 3
SCAN_T = 128
SCAN_B = V7X_SUBLANES
LOG2_E = 1.4426950408889634
TINY = 1e-30
MIX_TM = 512
FIN_TM = 1024
EXPERT_ROW_BUFS = 3
EXPERT_W_BUFS = 3
ROUTE_PAD = 128
NEG_BIG = -1e30


def _sigmoid(x):
    return 0.5 * jnp.tanh(0.5 * x) + 0.5


def _norm_mod(x, g, scale, shift):
    ms = jnp.mean(x * x, axis=-1, keepdims=True)
    return (x * lax.rsqrt(ms + EPS)) * (g * (1.0 + scale)) + shift


def _dot(a, b):
    return jnp.dot(a, b, preferred_element_type=F32)


def _params(sem, vmem=V7X_VMEM_LIMIT_BYTES):
    return pltpu.CompilerParams(dimension_semantics=sem, vmem_limit_bytes=vmem)


def _const_spec(shape):
    zeros = (0,) * len(shape)
    return pl.BlockSpec(shape, lambda *_: zeros, pipeline_mode=pl.Buffered(1))


def _ada_body(c_ref, w_ref, b_ref, o_ref):
    c = c_ref[...]
    s = (c * _sigmoid(c)).astype(BF16)
    o_ref[...] = _dot(s, w_ref[...].astype(BF16)) + b_ref[...]


def _ada(cond, w_ada, b_ada):
    rows = cond.shape[0]
    n_out = w_ada.shape[1]
    return pl.pallas_call(
        _ada_body,
        grid=(n_out // D,),
        in_specs=[pl.BlockSpec((rows, D), lambda i: (0, 0)),
                  pl.BlockSpec((D, D), lambda i: (0, i)),
                  pl.BlockSpec((1, D), lambda i: (0, i))],
        out_specs=pl.BlockSpec((rows, D), lambda i: (0, i)),
        out_shape=jax.ShapeDtypeStruct((rows, n_out), F32),
        compiler_params=_params(("parallel",)),
        name="ada",
    )(cond, w_ada, b_ada.reshape(1, n_out))


def _xr_body(x_ref, xn_ref, mod_ref, g_ref, w_ref, cw_ref, cb_ref, o_ref, xt):
    j = pl.program_id(1)
    n_t = pl.num_programs(1)
    t_len = x_ref.shape[1]
    nb = SCAN_B
    body0 = XR_LEFT * nb

    @pl.when(j == 0)
    def _():
        xt[:, 0:body0, :] = jnp.zeros((N_HEADS, body0, HEAD), F32)

    @pl.when(j > 0)
    def _():
        xt[:, 0:body0, :] = xt[:, t_len * nb:t_len * nb + body0, :]

    g = g_ref[...]
    for b0 in range(0, nb, XR_MB):
        h = jnp.concatenate(
            [_norm_mod(x_ref[b], g, mod_ref[b, 1:2, :], mod_ref[b, 0:1, :]).astype(BF16)
             for b in range(b0, b0 + XR_MB)], axis=0)
        r = _dot(h, w_ref[...])
        for i in range(XR_MB):
            for s in range(N_HEADS):
                xt[s, pl.ds(body0 + b0 + i, t_len, stride=nb), :] = (
                    r[i * t_len:(i + 1) * t_len, s * HEAD:(s + 1) * HEAD])

    hn = _norm_mod(xn_ref[:, 0, :], g, mod_ref[:, 1, :], mod_ref[:, 0, :]).astype(BF16)
    rn = jnp.where(j < n_t - 1, _dot(hn, w_ref[...]), 0.0)
    tail = body0 + t_len * nb
    for s in range(N_HEADS):
        xt[s, tail:tail + nb, :] = rn[:, s * HEAD:(s + 1) * HEAD]

    sub = XR_SUB
    for s in range(N_HEADS):
        sl = slice(s * HEAD, (s + 1) * HEAD)
        for t0 in range(0, t_len, sub):
            y = cb_ref[:, sl]
            for k in range(4):
                r0 = (t0 + k) * nb
                y = y + xt[s, r0:r0 + sub * nb, :] * cw_ref[k:k + 1, sl]
            o_ref[t0:t0 + sub, :, sl] = y.reshape(sub, nb, HEAD)


def _xr_conv(x, mod_seq, g1, w_in, cw, cb):
    n_b, seq_len, _ = x.shape
    t_len = min(XR_T, seq_len)
    assert n_b % SCAN_B == 0 and seq_len % t_len == 0 and t_len % XR_SUB == 0
    n_t = seq_len // t_len
    nxt = t_len // V7X_SUBLANES
    last_blk = seq_len // V7X_SUBLANES - 1
    return pl.pallas_call(
        _xr_body,
        grid=(n_b // SCAN_B, n_t),
        in_specs=[pl.BlockSpec((SCAN_B, t_len, D), lambda g, j: (g, j, 0)),
                  pl.BlockSpec((SCAN_B, V7X_SUBLANES, D),
                               lambda g, j: (g, jnp.minimum((j + 1) * nxt, last_blk), 0)),
                  pl.BlockSpec((SCAN_B, 6, D), lambda g, j: (g, 0, 0)),
                  _const_spec((1, D)),
                  pl.BlockSpec((D, D), lambda g, j: (0, XR_COL), pipeline_mode=pl.Buffered(1)),
                  _const_spec((4, D)),
                  _const_spec((1, D))],
        out_specs=pl.BlockSpec((t_len, SCAN_B, D), lambda g, j: (j, g, 0)),
        out_shape=jax.ShapeDtypeStruct((seq_len, n_b, D), F32),
        scratch_shapes=[pltpu.VMEM((N_HEADS, (t_len + XR_LEFT + 1) * SCAN_B, HEAD), F32)],
        compiler_params=_params(("parallel", "arbitrary")),
        name="xr_conv",
    )(x, x, mod_seq, g1, w_in, cw, cb)


def _scan_body(xf_ref, xb_ref, h0_ref, wg_ref, ba_ref, bx_ref, lam_ref,
               hf_ref, hb_ref, last_ref, a_s, u_s, h_s, hc):
    j = pl.program_id(1)
    n_t = pl.num_programs(1)
    t_len = xf_ref.shape[0]
    rows = SCAN_B * t_len

    @pl.when(j == 0)
    def _():
        for d in range(2):
            for s in range(N_HEADS):
                hc[d, s] = h0_ref[0, d, :, s * HEAD:(s + 1) * HEAD]

    c2s = []
    for d in range(2):
        z = -lam_ref[d:d + 1, :]
        sp = jnp.maximum(z, 0.0) + jnp.log(1.0 + jnp.exp(-jnp.abs(z)))
        c2s.append((-0.5 * RG_C * LOG2_E) * sp)

    for hd in range(N_HEADS):
        sl = slice(hd * HEAD, (hd + 1) * HEAD)
        for d, x_ref in ((0, xf_ref), (1, xb_ref)):
            xh = x_ref[:, :, sl].reshape(rows, HEAD)
            g = _dot(xh.astype(BF16), wg_ref[d, hd])
            t_r = jnp.tanh(g[:, :HEAD] + 0.5 * ba_ref[d:d + 1, sl])
            t_i = jnp.tanh(g[:, HEAD:] + 0.5 * bx_ref[d:d + 1, sl])
            a = jnp.exp2(c2s[d][:, sl] * t_r + c2s[d][:, sl])
            q = 0.25 - 0.25 * (a * a)
            half_mult = q * lax.rsqrt(jnp.maximum(q, TINY))
            u = half_mult * ((t_i + 1.0) * xh)
            a_s[d, hd] = a
            u_s[d, hd] = u
        h_f = hc[0, hd]
        h_b = hc[1, hd]
        for t in range(t_len):
            rf = t * SCAN_B
            rb = (t_len - 1 - t) * SCAN_B
            h_f = a_s[0, hd, rf:rf + SCAN_B, :] * h_f + u_s[0, hd, rf:rf + SCAN_B, :]
            h_b = a_s[1, hd, rb:rb + SCAN_B, :] * h_b + u_s[1, hd, rb:rb + SCAN_B, :]
            h_s[0, hd, rf:rf + SCAN_B, :] = h_f
            h_s[1, hd, rb:rb + SCAN_B, :] = h_b
        hc[0, hd] = h_f
        hc[1, hd] = h_b

    for d, o_ref in ((0, hf_ref), (1, hb_ref)):
        for hd in range(N_HEADS):
            for b in range(SCAN_B):
                o_ref[b, :, hd * HEAD:(hd + 1) * HEAD] = (
                    h_s[d, hd, pl.ds(b, t_len, stride=SCAN_B), :].astype(BF16))

    @pl.when(j == n_t - 1)
    def _():
        for d in range(2):
            for s in range(N_HEADS):
                last_ref[0, d, :, s * HEAD:(s + 1) * HEAD] = hc[d, s]


def _rglru_scan(xc, h0, wg, ba, bx, lam):
    seq_len, n_b, _ = xc.shape
    n_g = n_b // SCAN_B
    n_t = seq_len // SCAN_T
    blk = (SCAN_B, SCAN_T, D)
    blk_in = (SCAN_T, SCAN_B, D)
    return pl.pallas_call(
        _scan_body,
        grid=(n_g, n_t),
        in_specs=[pl.BlockSpec(blk_in, lambda g, j: (j, g, 0)),
                  pl.BlockSpec(blk_in, lambda g, j: (n_t - 1 - j, g, 0)),
                  pl.BlockSpec((1, 2, SCAN_B, D), lambda g, j: (g, 0, 0, 0)),
                  _const_spec((2, N_HEADS, HEAD, 2 * HEAD)),
                  _const_spec((2, D)),
                  _const_spec((2, D)),
                  _const_spec((2, D))],
        out_specs=[pl.BlockSpec(blk, lambda g, j: (g, j, 0)),
                   pl.BlockSpec(blk, lambda g, j: (g, n_t - 1 - j, 0)),
                   pl.BlockSpec((1, 2, SCAN_B, D), lambda g, j: (g, 0, 0, 0))],
        out_shape=[jax.ShapeDtypeStruct((n_b, seq_len, D), BF16),
                   jax.ShapeDtypeStruct((n_b, seq_len, D), BF16),
                   jax.ShapeDtypeStruct((n_g, 2, SCAN_B, D), F32)],
        scratch_shapes=[pltpu.VMEM((2, N_HEADS, SCAN_T * SCAN_B, HEAD), F32),
                        pltpu.VMEM((2, N_HEADS, SCAN_T * SCAN_B, HEAD), F32),
                        pltpu.VMEM((2, N_HEADS, SCAN_T * SCAN_B, HEAD), F32),
                        pltpu.VMEM((2, N_HEADS, SCAN_B, HEAD), F32)],
        compiler_params=_params(("parallel", "arbitrary")),
        name="rglru_scan",
    )(xc, xc, h0, wg, ba, bx, lam)


def _pack_halves(v):
    half = v.shape[1] // 2
    lo = lax.bitcast_convert_type(v[:, :half].astype(BF16).astype(F32), jnp.uint32)
    hi = lax.bitcast_convert_type(v[:, half:].astype(BF16).astype(F32), jnp.uint32)
    return lax.bitcast_convert_type((lo >> 16) | (hi & jnp.uint32(0xFFFF0000)), jnp.int32)


def _unpack_halves(p):
    u = lax.bitcast_convert_type(p, jnp.uint32)
    lo = lax.bitcast_convert_type(u << 16, F32)
    hi = lax.bitcast_convert_type(u & jnp.uint32(0xFFFF0000), F32)
    return lo, hi


def _mix_body(x_ref, mod_ref, hf_ref, hb_ref, g1_ref, g2_ref, win_ref, cw_ref,
              wco_ref, wro_ref, wo_ref, wr_ref, br_ref, *rest, row_len, n_cast):
    cast_in = rest[:n_cast]
    x1_ref, h2_ref, route_ref, wts_ref, cnt_ref = rest[n_cast:n_cast + 5]
    cast_out = rest[n_cast + 5:2 * n_cast + 5]
    seen = rest[2 * n_cast + 5]
    x = x_ref[...]
    tm = x.shape[0]
    h = _norm_mod(x, g1_ref[...], mod_ref[0, 1:2, :], mod_ref[0, 0:1, :]).astype(BF16)

    def proj(k):
        return _dot(h, win_ref[:, k * D:(k + 1) * D])

    cv = proj(1) * proj(2)
    pos = lax.broadcasted_iota(jnp.int32, (tm, 1), 0) % row_len
    conv = cv * cw_ref[1:2, :]
    conv = conv + jnp.where(pos >= 1, pltpu.roll(cv, 1, 0), 0.0) * cw_ref[0:1, :]
    conv = conv + jnp.where(pos <= row_len - 2, pltpu.roll(cv, tm - 1, 0), 0.0) * cw_ref[2:3, :]
    y_a = _dot((proj(0) * conv).astype(BF16), wco_ref[...])
    merged = _sigmoid(proj(5)) * y_a

    hs = hf_ref[...].astype(F32) + hb_ref[...].astype(F32)
    y_b = _dot((hs * jax.nn.gelu(proj(4))).astype(BF16), wro_ref[...])
    merged = merged + _sigmoid(proj(6)) * y_b

    mix = _dot(merged.astype(BF16), wo_ref[...])
    x1 = x + mod_ref[0, 2:3, :] * mix
    x1_ref[...] = x1.astype(BF16)
    for src, dst in zip(cast_in, cast_out):
        dst[...] = src[...].astype(BF16)
    h2 = _norm_mod(x1, g2_ref[...], mod_ref[0, 4:5, :], mod_ref[0, 3:4, :])
    h2_ref[...] = _pack_halves(h2)

    h2_hi = h2.astype(BF16)
    h2_lo = (h2 - h2_hi.astype(F32)).astype(BF16)
    big = _dot(h2_hi, wr_ref[...])
    logits = (big[:, :ROUTE_PAD] + big[:, ROUTE_PAD:]
              + _dot(h2_lo, wr_ref[:, :ROUTE_PAD]) + br_ref[...])
    lt = logits.T
    row = lax.broadcasted_iota(jnp.int32, (EPG, tm), 0)
    lg = lt[0:EPG]
    mg = jnp.max(lg, axis=0, keepdims=True)
    p_grp = 1.0 / jnp.sum(jnp.exp(lg - mg), axis=0, keepdims=True)
    grp = jnp.min(jnp.where(lg == mg, row, EPG), axis=0, keepdims=True)
    le = lt[EPG * N_GROUPS:EPG * (N_GROUPS + 1)]
    for g in range(N_GROUPS - 2, -1, -1):
        le = jnp.where(grp == g, lt[EPG * (g + 1):EPG * (g + 2)], le)
    me = jnp.max(le, axis=0, keepdims=True)
    ee = jnp.exp(le - me)
    pe = ee / jnp.sum(ee, axis=0, keepdims=True)
    p1 = jnp.max(pe, axis=0, keepdims=True)
    i1 = jnp.min(jnp.where(pe == p1, row, EPG), axis=0, keepdims=True)
    pe2 = jnp.where(row == i1, -1.0, pe)
    p2 = jnp.max(pe2, axis=0, keepdims=True)
    i2 = jnp.min(jnp.where(pe2 == p2, row, EPG), axis=0, keepdims=True)
    den = p1 + p2
    e1 = grp * EPG + i1
    e2 = grp * EPG + i2

    @pl.when(pl.program_id(0) == 0)
    def _():
        seen[...] = jnp.zeros_like(seen)

    erow = lax.broadcasted_iota(jnp.int32, (N_EXPERTS, tm), 0)
    hit1 = erow == e1
    hit2 = erow == e2
    both = jnp.where(jnp.logical_or(hit1, hit2), 1.0, 0.0)
    tri = jnp.where(lax.broadcasted_iota(jnp.int32, (tm, tm), 0)
                    <= lax.broadcasted_iota(jnp.int32, (tm, tm), 1), 1.0, 0.0).astype(BF16)
    before = _dot(both.astype(BF16), tri) - both + seen[...]
    r1 = jnp.sum(jnp.where(hit1, before, 0.0), axis=0, keepdims=True).astype(jnp.int32)
    r2 = jnp.sum(jnp.where(hit2, before, 0.0), axis=0, keepdims=True).astype(jnp.int32)
    total = seen[...] + jnp.sum(both, axis=1, keepdims=True)
    seen[...] = total
    cnt_ref[...] = total[:, :V7X_LANES].astype(jnp.int32)

    route_ref[...] = jnp.where(row == 0, e1, jnp.where(row == 1, e2, jnp.where(
        row == 2, r1, jnp.where(row == 3, r2, 0))))
    w8 = jnp.where(row == 0, p_grp * p1 / den, jnp.where(row == 1, p_grp * p2 / den, 0.0))
    wts_ref[...] = jnp.concatenate([w8, jnp.zeros((V7X_LANES - EPG, tm), F32)], axis=0).T


def _mixer(x, mod, tiles_per_mod, hf, hb, g1, g2, w_in, cw, wco, wro, wo, wr, br, row_len, cast_ws):
    n = x.shape[0]
    tm = MIX_TM
    assert tm % row_len == 0 and n % tm == 0
    steps = n // tm
    assert N_EXPERTS % steps == 0
    epb = N_EXPERTS // steps
    cast_specs = [pl.BlockSpec((epb,) + w.shape[1:], lambda i: (i, 0, 0)) for w in cast_ws]
    cast_shapes = [jax.ShapeDtypeStruct(w.shape, BF16) for w in cast_ws]
    mod_map = lambda i: (i // tiles_per_mod, 0, 0)
    tok = lambda i: (i, 0)
    col = lambda i: (0, i)
    return pl.pallas_call(
        functools.partial(_mix_body, row_len=row_len, n_cast=len(cast_ws)),
        grid=(steps,),
        in_specs=[pl.BlockSpec((tm, D), tok),
                  pl.BlockSpec((1, 6, D), mod_map),
                  pl.BlockSpec((tm, D), tok),
                  pl.BlockSpec((tm, D), tok),
                  _const_spec((1, D)),
                  _const_spec((1, D)),
                  _const_spec(w_in.shape),
                  _const_spec((3, D)),
                  _const_spec((D, D)),
                  _const_spec((D, D)),
                  _const_spec((D, D)),
                  _const_spec((D, 2 * ROUTE_PAD)),
                  _const_spec((1, ROUTE_PAD))] + cast_specs,
        out_specs=[pl.BlockSpec((tm, D), tok),
                   pl.BlockSpec((tm, D // 2), tok),
                   pl.BlockSpec((EPG, tm), col),
                   pl.BlockSpec((tm, V7X_LANES), tok),
                   pl.BlockSpec((N_EXPERTS, V7X_LANES), lambda i: (0, 0))] + cast_specs,
        out_shape=[jax.ShapeDtypeStruct((n, D), BF16),
                   jax.ShapeDtypeStruct((n, D // 2), jnp.int32),
                   jax.ShapeDtypeStruct((EPG, n), jnp.int32),
                   jax.ShapeDtypeStruct((n, V7X_LANES), F32),
                   jax.ShapeDtypeStruct((N_EXPERTS, V7X_LANES), jnp.int32)] + cast_shapes,
        scratch_shapes=[pltpu.VMEM((N_EXPERTS, tm), F32)],
        compiler_params=_params(("arbitrary",)),
        name="mixer",
    )(x, mod, hf, hb, g1, g2, w_in, cw, wco, wro, wo, wr, br, *cast_ws)


def _sc_mesh():
    return plsc.VectorSubcoreMesh(core_axis_name="c", subcore_axis_name="s",
                                  num_cores=V7X_SC_CORES, num_subcores=V7X_SC_SUBCORES)


def _sc_worker_id():
    return lax.axis_index("s") * V7X_SC_CORES + lax.axis_index("c")


def _sc_dispatch(rows, dest, n_slots):
    n, width = rows.shape
    per_w = n // V7X_SC_WORKERS
    n_ch = per_w // SC_WINDOW
    assert n_ch * SC_WINDOW * V7X_SC_WORKERS == n
    idx = dest.reshape(2, V7X_SC_WORKERS, n_ch, SC_WINDOW).transpose(1, 0, 2, 3)

    def body(x_hbm, d_hbm, o_hbm, idx_v, buf, ld_sem, st_sem):
        wid = _sc_worker_id()
        pltpu.sync_copy(d_hbm.at[wid], idx_v)

        def load(j):
            src = x_hbm.at[pl.ds(wid * per_w + j * SC_WINDOW, SC_WINDOW)]
            return pltpu.async_copy(src, buf.at[j % SC_BUFS], ld_sem.at[j % SC_BUFS])

        def scatter(j):
            return [pltpu.async_copy(buf.at[j % SC_BUFS], o_hbm.at[idx_v.at[k, j]], st_sem.at[j % SC_BUFS])
                    for k in range(2)]

        loads = {j: load(j) for j in range(min(SC_BUFS - 1, n_ch))}
        stores, waited = {}, set()
        for j in range(n_ch):
            loads[j].wait()
            nxt = j + SC_BUFS - 1
            if nxt < n_ch:
                if nxt - SC_BUFS >= 0:
                    for cp in stores[nxt - SC_BUFS]:
                        cp.wait()
                    waited.add(nxt - SC_BUFS)
                loads[nxt] = load(nxt)
            stores[j] = scatter(j)
        for j in range(n_ch):
            if j not in waited:
                for cp in stores[j]:
                    cp.wait()

    return pl.kernel(
        body,
        out_type=jax.ShapeDtypeStruct((n_slots, width), jnp.int32),
        mesh=_sc_mesh(),
        scratch_types=[pltpu.VMEM((2, n_ch, SC_WINDOW), jnp.int32),
                       pltpu.VMEM((SC_BUFS, SC_WINDOW, width), jnp.int32),
                       pltpu.SemaphoreType.DMA((SC_BUFS,)),
                       pltpu.SemaphoreType.DMA((SC_BUFS,))],
        name="sc_dispatch",
    )(rows, idx)


def _sc_collect(rows, dest):
    n = dest.shape[1]
    width = rows.shape[1]
    per_w = n // V7X_SC_WORKERS
    n_ch = per_w // SC_WINDOW
    assert n_ch * SC_WINDOW * V7X_SC_WORKERS == n
    idx = dest.reshape(2, V7X_SC_WORKERS, n_ch, SC_WINDOW).transpose(1, 0, 2, 3)
    windows = [(k, j) for k in range(2) for j in range(n_ch)]

    def body(y_hbm, d_hbm, o_hbm, idx_v, buf, ld_sem, st_sem):
        wid = _sc_worker_id()
        pltpu.sync_copy(d_hbm.at[wid], idx_v)

        def gather(c):
            k, j = windows[c]
            return pltpu.async_copy(y_hbm.at[idx_v.at[k, j]], buf.at[c % SC_BUFS], ld_sem.at[c % SC_BUFS])

        def store(c):
            k, j = windows[c]
            dst = o_hbm.at[pl.ds(k * n + wid * per_w + j * SC_WINDOW, SC_WINDOW)]
            return pltpu.async_copy(buf.at[c % SC_BUFS], dst, st_sem.at[c % SC_BUFS])

        n_win = len(windows)
        loads = {c: gather(c) for c in range(min(SC_BUFS - 1, n_win))}
        stores, waited = {}, set()
        for c in range(n_win):
            loads[c].wait()
            nxt = c + SC_BUFS - 1
            if nxt < n_win:
                if nxt - SC_BUFS >= 0:
                    stores[nxt - SC_BUFS].wait()
                    waited.add(nxt - SC_BUFS)
                loads[nxt] = gather(nxt)
            stores[c] = store(c)
        for c in range(n_win):
            if c not in waited:
                stores[c].wait()

    return pl.kernel(
        body,
        out_type=jax.ShapeDtypeStruct((2 * n, width), jnp.int32),
        mesh=_sc_mesh(),
        scratch_types=[pltpu.VMEM((2, n_ch, SC_WINDOW), jnp.int32),
                       pltpu.VMEM((SC_BUFS, SC_WINDOW, width), jnp.int32),
                       pltpu.SemaphoreType.DMA((SC_BUFS,)),
                       pltpu.SemaphoreType.DMA((SC_BUFS,))],
        name="sc_collect",
    )(rows, idx)


def _expert_body(tr_ref, re_ref, nr_ref, nu_ref, xs_hbm, w1_hbm, w3_hbm, w2_hbm, o_ref,
                 xbuf, w1s, w3s, w2s, xsem, sem):
    i = pl.program_id(0)
    n_used = nu_ref[0]
    tme = xbuf.shape[1]
    run = tr_ref[i]
    first = jnp.logical_or(i == 0, run != tr_ref[jnp.maximum(i - 1, 0)])
    slot = run % EXPERT_W_BUFS

    def row_copy(t):
        s = t % EXPERT_ROW_BUFS
        start = t * tme if isinstance(t, int) else pl.multiple_of(t * tme, tme)
        src = xs_hbm.at[pl.ds(start, tme)]
        return pltpu.make_async_copy(src, xbuf.at[s], xsem.at[s])

    def weight_copies(r):
        e = re_ref[r]
        s = r % EXPERT_W_BUFS
        return (pltpu.make_async_copy(w1_hbm.at[e], w1s.at[s], sem.at[0, s]),
                pltpu.make_async_copy(w3_hbm.at[e], w3s.at[s], sem.at[1, s]),
                pltpu.make_async_copy(w2_hbm.at[e], w2s.at[s], sem.at[2, s]))

    @pl.when(i == 0)
    def _():
        for t in range(EXPERT_ROW_BUFS - 1):
            @pl.when(t < n_used)
            def _():
                row_copy(t).start()
        for r in range(EXPERT_W_BUFS - 1):
            @pl.when(r < nr_ref[0])
            def _():
                for cp in weight_copies(r):
                    cp.start()

    @pl.when(jnp.logical_and(first, i < n_used))
    def _():
        @pl.when(run + (EXPERT_W_BUFS - 1) < nr_ref[0])
        def _():
            for cp in weight_copies(run + (EXPERT_W_BUFS - 1)):
                cp.start()

        for cp in weight_copies(run):
            cp.wait()

    @pl.when(i < n_used)
    def _():
        @pl.when(i + (EXPERT_ROW_BUFS - 1) < n_used)
        def _():
            row_copy(i + (EXPERT_ROW_BUFS - 1)).start()

        row_copy(i).wait()
        lo, hi = _unpack_halves(xbuf[i % EXPERT_ROW_BUFS])
        lo = lo.astype(BF16)
        hi = hi.astype(BF16)
        half = D // 2
        a = _dot(lo, w1s[slot, 0:half, :]) + _dot(hi, w1s[slot, half:D, :])
        b = _dot(lo, w3s[slot, 0:half, :]) + _dot(hi, w3s[slot, half:D, :])
        z = (a * _sigmoid(a)) * b
        o_ref[...] = _pack_halves(_dot(z.astype(BF16), w2s[slot]))


def _experts(tile_run, run_e, n_runs, n_used, xs, w1, w3, w2, tme):
    n_slots = xs.shape[0]
    grid_spec = pltpu.PrefetchScalarGridSpec(
        num_scalar_prefetch=4,
        grid=(n_slots // tme,),
        in_specs=[pl.BlockSpec(memory_space=pl.ANY),
                  pl.BlockSpec(memory_space=pl.ANY),
                  pl.BlockSpec(memory_space=pl.ANY),
                  pl.BlockSpec(memory_space=pl.ANY)],
        out_specs=pl.BlockSpec((tme, D // 2),
                               lambda i, tr, re, nr, nu: (jnp.clip(i, 0, jnp.maximum(nu[0] - 1, 0)), 0)),
        scratch_shapes=[pltpu.VMEM((EXPERT_ROW_BUFS, tme, D // 2), jnp.int32),
                        pltpu.VMEM((EXPERT_W_BUFS, D, D_EXPERT), BF16),
                        pltpu.VMEM((EXPERT_W_BUFS, D, D_EXPERT), BF16),
                        pltpu.VMEM((EXPERT_W_BUFS, D_EXPERT, D), BF16),
                        pltpu.SemaphoreType.DMA((EXPERT_ROW_BUFS,)),
                        pltpu.SemaphoreType.DMA((3, EXPERT_W_BUFS))],
    )
    return pl.pallas_call(
        _expert_body,
        grid_spec=grid_spec,
        out_shape=jax.ShapeDtypeStruct((n_slots, D // 2), jnp.int32),
        compiler_params=_params(("arbitrary",)),
        name="experts",
    )(tile_run, run_e, n_runs, n_used, xs, w1, w3, w2)


def _final_body(x1_ref, mod_ref, y0_ref, y1_ref, wt_ref, gf_ref, o_ref):
    w0 = wt_ref[:, 0:1]
    w1 = wt_ref[:, 1:2]
    lo0, hi0 = _unpack_halves(y0_ref[...])
    lo1, hi1 = _unpack_halves(y1_ref[...])
    moe = jnp.concatenate([w0 * lo0 + w1 * lo1, w0 * hi0 + w1 * hi1], axis=1)
    x2 = x1_ref[...].astype(F32) + mod_ref[0, 5:6, :] * moe
    ms = jnp.mean(x2 * x2, axis=-1, keepdims=True)
    o_ref[...] = x2 * lax.rsqrt(ms + EPS) * gf_ref[...]


def _final(x1, mod, tiles_per_mod, yg, wts, g_final):
    n = x1.shape[0]
    tm = FIN_TM
    nt = n // tm
    return pl.pallas_call(
        _final_body,
        grid=(nt,),
        in_specs=[pl.BlockSpec((tm, D), lambda i: (i, 0)),
                  pl.BlockSpec((1, 6, D), lambda i: (i // tiles_per_mod, 0, 0)),
                  pl.BlockSpec((tm, D // 2), lambda i: (i, 0)),
                  pl.BlockSpec((tm, D // 2), lambda i: (i + nt, 0)),
                  pl.BlockSpec((tm, V7X_LANES), lambda i: (i, 0)),
                  _const_spec((1, D))],
        out_specs=pl.BlockSpec((tm, D), lambda i: (i, 0)),
        out_shape=jax.ShapeDtypeStruct((n, D), F32),
        compiler_params=_params(("parallel",)),
        name="final",
    )(x1, mod, yg, yg, wts, g_final)


def _slot_plan(route, cnt, n, tme):
    counts = cnt[:, 0]
    padded = ((counts + tme - 1) // tme) * tme
    pend = jnp.cumsum(padded)
    pstart = pend - padded
    onehot = route[0:2, :, None] == jnp.arange(N_EXPERTS, dtype=jnp.int32)[None, None, :]
    dest = jnp.sum(jnp.where(onehot, pstart[None, None, :], 0), axis=-1) + route[2:4]
    n_slots = ((2 * n + N_EXPERTS * (tme - 1)) // tme) * tme
    dest = jnp.clip(dest, 0, n_slots - 1)
    tile_start = jnp.arange(n_slots // tme, dtype=jnp.int32) * tme
    tile_e = jnp.sum((tile_start[:, None] >= pend[None, :]).astype(jnp.int32), axis=1)
    tile_e = jnp.minimum(tile_e, N_EXPERTS - 1)
    n_used = (pend[-1] // tme).astype(jnp.int32).reshape(1)
    used = counts > 0
    run_of_e = jnp.cumsum(used.astype(jnp.int32)) - 1
    eids = jnp.arange(N_EXPERTS, dtype=jnp.int32)
    run_e = jnp.sum(jnp.where(used[None, :] & (run_of_e[None, :] == eids[:, None]), eids[None, :], 0), axis=1)
    tile_run = jnp.sum(jnp.where(tile_e[:, None] == eids[None, :], run_of_e[None, :], 0), axis=1)
    n_runs = jnp.sum(used.astype(jnp.int32)).reshape(1)
    plan = (tile_run.astype(jnp.int32), run_e.astype(jnp.int32), n_runs, n_used)
    return dest.astype(jnp.int32), plan, n_slots


def _group(x, mod, mod_per_seq, h0, p, row_len, tme, cast_ws):
    n_b, seq_len, _ = x.shape
    n = n_b * seq_len
    xt = x.reshape(n, D)
    mod_seq = mod if mod_per_seq else jnp.broadcast_to(mod, (n_b, 6, D))
    xc = _xr_conv(x, mod_seq, p["g1"], p["w_in"], p["rnn_conv_w"], p["rnn_conv_b"])
    hf, hb, last = _rglru_scan(xc, h0, p["wg"], p["ba"], p["bx"], p["lam"])
    tiles_per_mod = (seq_len // MIX_TM) if mod_per_seq else (n // MIX_TM)
    outs = _mixer(xt, mod, tiles_per_mod, hf.reshape(n, D), hb.reshape(n, D),
                                     p["g1"], p["g2"], p["w_in"], p["conv_w"], p["wco"], p["wro"],
                                     p["wo"], p["wr"], p["br"], row_len, cast_ws)
    x1, h2, route, wts, cnt = outs[:5]
    casts = outs[5:]
    dest, plan, n_slots = _slot_plan(route, cnt, n, tme)
    xs = _sc_dispatch(h2, dest, n_slots)

    def finish(w1b, w3b, w2b):
        ys = _experts(*plan, xs, w1b, w3b, w2b, tme)
        yg = _sc_collect(ys, dest)
        tiles_per_mod_f = (seq_len // FIN_TM) if mod_per_seq else (n // FIN_TM)
        y = _final(x1, mod, tiles_per_mod_f, yg, wts, p["g_final"])
        return y.reshape(n_b, seq_len, D)

    return finish, last, casts


def kernel(x_prompt, x_sample, state_rnn, c, c_ctx, w_ada, b_ada, g_norm1, g_norm2, w_in, conv_w, w_conv_out, rnn_conv_w, rnn_conv_b, w_gate_a, b_gate_a, w_gate_x, b_gate_x, lam, w_rnn_out, w_o, w_router_group, b_router_group, w_router_expert, b_router_expert, w1, w3, w2, g_final):
    assert w_ada.shape[0] == 1, "single layer"
    n_pb, n_sb = x_prompt.shape[0], x_sample.shape[0]

    cond = jnp.concatenate([c_ctx[None, :], c, jnp.zeros((16 - 1 - n_sb, D), F32)], axis=0)
    mod = _ada(cond, w_ada[0], b_ada[0]).reshape(16, 6, D)

    w_in_b = w_in[0].astype(BF16)
    pad_w = jnp.zeros((D, EPG - N_GROUPS), F32)
    wr = jnp.concatenate([w_router_group[0], pad_w, w_router_expert[0],
                          jnp.zeros((D, ROUTE_PAD - EPG - N_EXPERTS), F32)], axis=1)
    br = jnp.concatenate([b_router_group[0], jnp.full((EPG - N_GROUPS,), NEG_BIG, F32),
                          b_router_expert[0],
                          jnp.zeros((ROUTE_PAD - EPG - N_EXPERTS,), F32)]).reshape(1, ROUTE_PAD)
    wr_hi = wr.astype(BF16)
    p = dict(
        g1=g_norm1, g2=g_norm2, w_in=w_in_b,
        conv_w=conv_w[0], rnn_conv_w=rnn_conv_w[0], rnn_conv_b=rnn_conv_b,
        wg=(0.5 * jnp.concatenate([w_gate_a[0], w_gate_x[0]], axis=-1)).astype(BF16),
        ba=b_gate_a[0], bx=b_gate_x[0], lam=lam[0],
        wco=w_conv_out[0].astype(BF16), wro=w_rnn_out[0].astype(BF16), wo=w_o[0].astype(BF16),
        wr=jnp.concatenate([wr_hi, (wr - wr_hi.astype(F32)).astype(BF16)], axis=1), br=br,
        g_final=g_final.reshape(1, D),
    )

    h0_p = jnp.zeros((n_pb // SCAN_B, 2, SCAN_B, D), F32)
    finish_p, last, (w2b,) = _group(x_prompt, mod[0:1], False, h0_p, p, x_prompt.shape[1], 512,
                                    [w2[0]])
    state_new = last.transpose(0, 2, 1, 3).reshape(n_pb, 1, 2, D)

    h0_s = state_rnn[:, 0].reshape(n_sb // SCAN_B, SCAN_B, 2, D).transpose(0, 2, 1, 3)
    finish_s, _, (w1b, w3b) = _group(x_sample, mod[1:1 + n_sb], True, h0_s, p, GRID_W, 512,
                                     [w1[0], w3[0]])
    return (finish_p(w1b, w3b, w2b), finish_s(w1b, w3b, w2b), state_new)
```

```python
import functools

import jax
import jax.numpy as jnp
from jax import lax
from jax.experimental import pallas as pl
from jax.experimental.pallas import tpu as pltpu
from jax.experimental.pallas import tpu_sc as plsc

D = 1024
N_HEADS = 8
HEAD = D // N_HEADS
GRID_W = 64
RG_C = 8.0
N_GROUPS = 4
EPG = 8
N_EXPERTS = N_GROUPS * EPG
D_EXPERT = 512
EPS = 1e-6
F32 = jnp.float32
BF16 = jnp.bfloat16

V7X_LANES = 128
V7X_SUBLANES = 8
V7X_VMEM_LIMIT_BYTES = 56 * 1024 * 1024
V7X_SC_CORES = 2
V7X_SC_SUBCORES = 16
V7X_SC_WORKERS = V7X_SC_CORES * V7X_SC_SUBCORES
SC_WINDOW = 64
SC_BUFS = 3

XR_T = 256
XR_SUB = 32
XR_MB = 4
XR_LEFT = 2
XR_COL = 3
SCAN_T = 128
SCAN_B = V7X_SUBLANES
LOG2_E = 1.4426950408889634
TINY = 1e-30
MIX_TM = 512
FIN_TM = 1024
EXPERT_ROW_BUFS = 3
EXPERT_W_BUFS = 3
PLAN_TILE_RUN, PLAN_RUN_EXPERT, PLAN_N_RUNS, PLAN_N_USED = 0, 1, 2, 3
ROUTE_PAD = 128
NEG_BIG = -1e30


def _sigmoid(x):
    return 0.5 * jnp.tanh(0.5 * x) + 0.5


def _norm_mod(x, g, scale, shift):
    ms = jnp.mean(x * x, axis=-1, keepdims=True)
    return (x * lax.rsqrt(ms + EPS)) * (g * (1.0 + scale)) + shift


def _dot(a, b):
    return jnp.dot(a, b, preferred_element_type=F32)


def _params(sem, vmem=V7X_VMEM_LIMIT_BYTES):
    return pltpu.CompilerParams(dimension_semantics=sem, vmem_limit_bytes=vmem)


def _const_spec(shape):
    zeros = (0,) * len(shape)
    return pl.BlockSpec(shape, lambda *_: zeros, pipeline_mode=pl.Buffered(1))


def _ada_body(c_ref, w_ref, b_ref, o_ref):
    c = c_ref[...]
    s = (c * _sigmoid(c)).astype(BF16)
    o_ref[...] = _dot(s, w_ref[...].astype(BF16)) + b_ref[...]


def _ada(cond, w_ada, b_ada):
    rows = cond.shape[0]
    n_out = w_ada.shape[1]
    return pl.pallas_call(
        _ada_body,
        grid=(n_out // D,),
        in_specs=[pl.BlockSpec((rows, D), lambda i: (0, 0)),
                  pl.BlockSpec((D, D), lambda i: (0, i)),
                  pl.BlockSpec((1, D), lambda i: (0, i))],
        out_specs=pl.BlockSpec((rows, D), lambda i: (0, i)),
        out_shape=jax.ShapeDtypeStruct((rows, n_out), F32),
        compiler_params=_params(("parallel",)),
        name="ada",
    )(cond, w_ada, b_ada.reshape(1, n_out))


def _xr_body(x_ref, xn_ref, mod_ref, g_ref, w_ref, cw_ref, cb_ref, o_ref, xt):
    j = pl.program_id(1)
    n_t = pl.num_programs(1)
    t_len = x_ref.shape[1]
    nb = SCAN_B
    body0 = XR_LEFT * nb

    @pl.when(j == 0)
    def _():
        xt[:, 0:body0, :] = jnp.zeros((N_HEADS, body0, HEAD), F32)

    @pl.when(j > 0)
    def _():
        xt[:, 0:body0, :] = xt[:, t_len * nb:t_len * nb + body0, :]

    g = g_ref[...]
    for b0 in range(0, nb, XR_MB):
        h = jnp.concatenate(
            [_norm_mod(x_ref[b], g, mod_ref[b, 1:2, :], mod_ref[b, 0:1, :]).astype(BF16)
             for b in range(b0, b0 + XR_MB)], axis=0)
        r = _dot(h, w_ref[...])
        for i in range(XR_MB):
            for s in range(N_HEADS):
                xt[s, pl.ds(body0 + b0 + i, t_len, stride=nb), :] = (
                    r[i * t_len:(i + 1) * t_len, s * HEAD:(s + 1) * HEAD])

    hn = _norm_mod(xn_ref[:, 0, :], g, mod_ref[:, 1, :], mod_ref[:, 0, :]).astype(BF16)
    rn = jnp.where(j < n_t - 1, _dot(hn, w_ref[...]), 0.0)
    tail = body0 + t_len * nb
    for s in range(N_HEADS):
        xt[s, tail:tail + nb, :] = rn[:, s * HEAD:(s + 1) * HEAD]

    sub = XR_SUB
    for s in range(N_HEADS):
        sl = slice(s * HEAD, (s + 1) * HEAD)
        for t0 in range(0, t_len, sub):
            y = cb_ref[:, sl]
            for k in range(4):
                r0 = (t0 + k) * nb
                y = y + xt[s, r0:r0 + sub * nb, :] * cw_ref[k:k + 1, sl]
            o_ref[t0:t0 + sub, :, sl] = y.reshape(sub, nb, HEAD)


def _xr_conv(x, mod_seq, g1, w_in, cw, cb):
    n_b, seq_len, _ = x.shape
    t_len = min(XR_T, seq_len)
    assert n_b % SCAN_B == 0 and seq_len % t_len == 0 and t_len % XR_SUB == 0
    n_t = seq_len // t_len
    nxt = t_len // V7X_SUBLANES
    last_blk = seq_len // V7X_SUBLANES - 1
    return pl.pallas_call(
        _xr_body,
        grid=(n_b // SCAN_B, n_t),
        in_specs=[pl.BlockSpec((SCAN_B, t_len, D), lambda g, j: (g, j, 0)),
                  pl.BlockSpec((SCAN_B, V7X_SUBLANES, D),
                               lambda g, j: (g, jnp.minimum((j + 1) * nxt, last_blk), 0)),
                  pl.BlockSpec((SCAN_B, 6, D), lambda g, j: (g, 0, 0)),
                  _const_spec((1, D)),
                  pl.BlockSpec((D, D), lambda g, j: (0, XR_COL), pipeline_mode=pl.Buffered(1)),
                  _const_spec((4, D)),
                  _const_spec((1, D))],
        out_specs=pl.BlockSpec((t_len, SCAN_B, D), lambda g, j: (j, g, 0)),
        out_shape=jax.ShapeDtypeStruct((seq_len, n_b, D), F32),
        scratch_shapes=[pltpu.VMEM((N_HEADS, (t_len + XR_LEFT + 1) * SCAN_B, HEAD), F32)],
        compiler_params=_params(("parallel", "arbitrary")),
        name="xr_conv",
    )(x, x, mod_seq, g1, w_in, cw, cb)


def _scan_body(xf_ref, xb_ref, h0_ref, wg_ref, ba_ref, bx_ref, lam_ref,
               hf_ref, hb_ref, last_ref, a_s, u_s, h_s, hc):
    j = pl.program_id(1)
    n_t = pl.num_programs(1)
    t_len = xf_ref.shape[0]
    rows = SCAN_B * t_len

    @pl.when(j == 0)
    def _():
        for d in range(2):
            for s in range(N_HEADS):
                hc[d, s] = h0_ref[0, d, :, s * HEAD:(s + 1) * HEAD]

    c2s = []
    for d in range(2):
        z = -lam_ref[d:d + 1, :]
        sp = jnp.maximum(z, 0.0) + jnp.log(1.0 + jnp.exp(-jnp.abs(z)))
        c2s.append((-0.5 * RG_C * LOG2_E) * sp)

    for hd in range(N_HEADS):
        sl = slice(hd * HEAD, (hd + 1) * HEAD)
        for d, x_ref in ((0, xf_ref), (1, xb_ref)):
            xh = x_ref[:, :, sl].reshape(rows, HEAD)
            g = _dot(xh.astype(BF16), wg_ref[d, hd])
            t_r = jnp.tanh(g[:, :HEAD] + 0.5 * ba_ref[d:d + 1, sl])
            t_i = jnp.tanh(g[:, HEAD:] + 0.5 * bx_ref[d:d + 1, sl])
            a = jnp.exp2(c2s[d][:, sl] * t_r + c2s[d][:, sl])
            q = 0.25 - 0.25 * (a * a)
            half_mult = q * lax.rsqrt(jnp.maximum(q, TINY))
            u = half_mult * ((t_i + 1.0) * xh)
            a_s[d, hd] = a
            u_s[d, hd] = u
        h_f = hc[0, hd]
        h_b = hc[1, hd]
        for t in range(t_len):
            rf = t * SCAN_B
            rb = (t_len - 1 - t) * SCAN_B
            h_f = a_s[0, hd, rf:rf + SCAN_B, :] * h_f + u_s[0, hd, rf:rf + SCAN_B, :]
            h_b = a_s[1, hd, rb:rb + SCAN_B, :] * h_b + u_s[1, hd, rb:rb + SCAN_B, :]
            h_s[0, hd, rf:rf + SCAN_B, :] = h_f
            h_s[1, hd, rb:rb + SCAN_B, :] = h_b
        hc[0, hd] = h_f
        hc[1, hd] = h_b

    for d, o_ref in ((0, hf_ref), (1, hb_ref)):
        for hd in range(N_HEADS):
            for b in range(SCAN_B):
                o_ref[b, :, hd * HEAD:(hd + 1) * HEAD] = (
                    h_s[d, hd, pl.ds(b, t_len, stride=SCAN_B), :].astype(BF16))

    @pl.when(j == n_t - 1)
    def _():
        for d in range(2):
            for s in range(N_HEADS):
                last_ref[0, d, :, s * HEAD:(s + 1) * HEAD] = hc[d, s]


def _rglru_scan(xc, h0, wg, ba, bx, lam):
    seq_len, n_b, _ = xc.shape
    n_g = n_b // SCAN_B
    n_t = seq_len // SCAN_T
    blk = (SCAN_B, SCAN_T, D)
    blk_in = (SCAN_T, SCAN_B, D)
    return pl.pallas_call(
        _scan_body,
        grid=(n_g, n_t),
        in_specs=[pl.BlockSpec(blk_in, lambda g, j: (j, g, 0)),
                  pl.BlockSpec(blk_in, lambda g, j: (n_t - 1 - j, g, 0)),
                  pl.BlockSpec((1, 2, SCAN_B, D), lambda g, j: (g, 0, 0, 0)),
                  _const_spec((2, N_HEADS, HEAD, 2 * HEAD)),
                  _const_spec((2, D)),
                  _const_spec((2, D)),
                  _const_spec((2, D))],
        out_specs=[pl.BlockSpec(blk, lambda g, j: (g, j, 0)),
                   pl.BlockSpec(blk, lambda g, j: (g, n_t - 1 - j, 0)),
                   pl.BlockSpec((1, 2, SCAN_B, D), lambda g, j: (g, 0, 0, 0))],
        out_shape=[jax.ShapeDtypeStruct((n_b, seq_len, D), BF16),
                   jax.ShapeDtypeStruct((n_b, seq_len, D), BF16),
                   jax.ShapeDtypeStruct((n_g, 2, SCAN_B, D), F32)],
        scratch_shapes=[pltpu.VMEM((2, N_HEADS, SCAN_T * SCAN_B, HEAD), F32),
                        pltpu.VMEM((2, N_HEADS, SCAN_T * SCAN_B, HEAD), F32),
                        pltpu.VMEM((2, N_HEADS, SCAN_T * SCAN_B, HEAD), F32),
                        pltpu.VMEM((2, N_HEADS, SCAN_B, HEAD), F32)],
        compiler_params=_params(("parallel", "arbitrary")),
        name="rglru_scan",
    )(xc, xc, h0, wg, ba, bx, lam)


def _pack_halves(v):
    half = v.shape[1] // 2
    lo = lax.bitcast_convert_type(v[:, :half].astype(BF16).astype(F32), jnp.uint32)
    hi = lax.bitcast_convert_type(v[:, half:].astype(BF16).astype(F32), jnp.uint32)
    return lax.bitcast_convert_type((lo >> 16) | (hi & jnp.uint32(0xFFFF0000)), jnp.int32)


def _unpack_halves(p):
    u = lax.bitcast_convert_type(p, jnp.uint32)
    lo = lax.bitcast_convert_type(u << 16, F32)
    hi = lax.bitcast_convert_type(u & jnp.uint32(0xFFFF0000), F32)
    return lo, hi


def _mix_body(x_ref, mod_ref, hf_ref, hb_ref, g1_ref, g2_ref, win_ref, cw_ref,
              wco_ref, wro_ref, wo_ref, wr_ref, br_ref, *rest, row_len, n_cast):
    cast_in = rest[:n_cast]
    x1_ref, h2_ref, route_ref, wts_ref, cnt_ref = rest[n_cast:n_cast + 5]
    cast_out = rest[n_cast + 5:2 * n_cast + 5]
    seen = rest[2 * n_cast + 5]
    x = x_ref[...]
    tm = x.shape[0]
    h = _norm_mod(x, g1_ref[...], mod_ref[0, 1:2, :], mod_ref[0, 0:1, :]).astype(BF16)

    def proj(k):
        return _dot(h, win_ref[:, k * D:(k + 1) * D])

    cv = proj(1) * proj(2)
    pos = lax.broadcasted_iota(jnp.int32, (tm, 1), 0) % row_len
    conv = cv * cw_ref[1:2, :]
    conv = conv + jnp.where(pos >= 1, pltpu.roll(cv, 1, 0), 0.0) * cw_ref[0:1, :]
    conv = conv + jnp.where(pos <= row_len - 2, pltpu.roll(cv, tm - 1, 0), 0.0) * cw_ref[2:3, :]
    y_a = _dot((proj(0) * conv).astype(BF16), wco_ref[...])
    merged = _sigmoid(proj(5)) * y_a

    hs = hf_ref[...].astype(F32) + hb_ref[...].astype(F32)
    y_b = _dot((hs * jax.nn.gelu(proj(4))).astype(BF16), wro_ref[...])
    merged = merged + _sigmoid(proj(6)) * y_b

    mix = _dot(merged.astype(BF16), wo_ref[...])
    x1 = x + mod_ref[0, 2:3, :] * mix
    x1_ref[...] = x1.astype(BF16)
    for src, dst in zip(cast_in, cast_out):
        dst[...] = src[...].astype(BF16)
    h2 = _norm_mod(x1, g2_ref[...], mod_ref[0, 4:5, :], mod_ref[0, 3:4, :])
    h2_ref[...] = _pack_halves(h2)

    h2_hi = h2.astype(BF16)
    h2_lo = (h2 - h2_hi.astype(F32)).astype(BF16)
    big = _dot(h2_hi, wr_ref[...])
    logits = (big[:, :ROUTE_PAD] + big[:, ROUTE_PAD:]
              + _dot(h2_lo, wr_ref[:, :ROUTE_PAD]) + br_ref[...])
    lt = logits.T
    row = lax.broadcasted_iota(jnp.int32, (EPG, tm), 0)
    lg = lt[0:EPG]
    mg = jnp.max(lg, axis=0, keepdims=True)
    p_grp = 1.0 / jnp.sum(jnp.exp(lg - mg), axis=0, keepdims=True)
    grp = jnp.min(jnp.where(lg == mg, row, EPG), axis=0, keepdims=True)
    le = lt[EPG * N_GROUPS:EPG * (N_GROUPS + 1)]
    for g in range(N_GROUPS - 2, -1, -1):
        le = jnp.where(grp == g, lt[EPG * (g + 1):EPG * (g + 2)], le)
    me = jnp.max(le, axis=0, keepdims=True)
    ee = jnp.exp(le - me)
    pe = ee / jnp.sum(ee, axis=0, keepdims=True)
    p1 = jnp.max(pe, axis=0, keepdims=True)
    i1 = jnp.min(jnp.where(pe == p1, row, EPG), axis=0, keepdims=True)
    pe2 = jnp.where(row == i1, -1.0, pe)
    p2 = jnp.max(pe2, axis=0, keepdims=True)
    i2 = jnp.min(jnp.where(pe2 == p2, row, EPG), axis=0, keepdims=True)
    den = p1 + p2
    e1 = grp * EPG + i1
    e2 = grp * EPG + i2

    @pl.when(pl.program_id(0) == 0)
    def _():
        seen[...] = jnp.zeros_like(seen)

    erow = lax.broadcasted_iota(jnp.int32, (N_EXPERTS, tm), 0)
    hit1 = erow == e1
    hit2 = erow == e2
    both = jnp.where(jnp.logical_or(hit1, hit2), 1.0, 0.0)
    tri = jnp.where(lax.broadcasted_iota(jnp.int32, (tm, tm), 0)
                    <= lax.broadcasted_iota(jnp.int32, (tm, tm), 1), 1.0, 0.0).astype(BF16)
    before = _dot(both.astype(BF16), tri) - both + seen[...]
    r1 = jnp.sum(jnp.where(hit1, before, 0.0), axis=0, keepdims=True).astype(jnp.int32)
    r2 = jnp.sum(jnp.where(hit2, before, 0.0), axis=0, keepdims=True).astype(jnp.int32)
    total = seen[...] + jnp.sum(both, axis=1, keepdims=True)
    seen[...] = total
    cnt_ref[...] = total[:, :V7X_LANES].astype(jnp.int32)

    route_ref[...] = jnp.where(row == 0, e1, jnp.where(row == 1, e2, jnp.where(
        row == 2, r1, jnp.where(row == 3, r2, 0))))
    w8 = jnp.where(row == 0, p_grp * p1 / den, jnp.where(row == 1, p_grp * p2 / den, 0.0))
    wts_ref[...] = jnp.concatenate([w8, jnp.zeros((V7X_LANES - EPG, tm), F32)], axis=0).T


def _mixer(x, mod, tiles_per_mod, hf, hb, g1, g2, w_in, cw, wco, wro, wo, wr, br, row_len, cast_ws):
    n = x.shape[0]
    tm = MIX_TM
    assert tm % row_len == 0 and n % tm == 0
    steps = n // tm
    assert N_EXPERTS % steps == 0
    epb = N_EXPERTS // steps
    cast_specs = [pl.BlockSpec((epb,) + w.shape[1:], lambda i: (i, 0, 0)) for w in cast_ws]
    cast_shapes = [jax.ShapeDtypeStruct(w.shape, BF16) for w in cast_ws]
    mod_map = lambda i: (i // tiles_per_mod, 0, 0)
    tok = lambda i: (i, 0)
    col = lambda i: (0, i)
    return pl.pallas_call(
        functools.partial(_mix_body, row_len=row_len, n_cast=len(cast_ws)),
        grid=(steps,),
        in_specs=[pl.BlockSpec((tm, D), tok),
                  pl.BlockSpec((1, 6, D), mod_map),
                  pl.BlockSpec((tm, D), tok),
                  pl.BlockSpec((tm, D), tok),
                  _const_spec((1, D)),
                  _const_spec((1, D)),
                  _const_spec(w_in.shape),
                  _const_spec((3, D)),
                  _const_spec((D, D)),
                  _const_spec((D, D)),
                  _const_spec((D, D)),
                  _const_spec((D, 2 * ROUTE_PAD)),
                  _const_spec((1, ROUTE_PAD))] + cast_specs,
        out_specs=[pl.BlockSpec((tm, D), tok),
                   pl.BlockSpec((tm, D // 2), tok),
                   pl.BlockSpec((EPG, tm), col),
                   pl.BlockSpec((tm, V7X_LANES), tok),
                   pl.BlockSpec((N_EXPERTS, V7X_LANES), lambda i: (0, 0))] + cast_specs,
        out_shape=[jax.ShapeDtypeStruct((n, D), BF16),
                   jax.ShapeDtypeStruct((n, D // 2), jnp.int32),
                   jax.ShapeDtypeStruct((EPG, n), jnp.int32),
                   jax.ShapeDtypeStruct((n, V7X_LANES), F32),
                   jax.ShapeDtypeStruct((N_EXPERTS, V7X_LANES), jnp.int32)] + cast_shapes,
        scratch_shapes=[pltpu.VMEM((N_EXPERTS, tm), F32)],
        compiler_params=_params(("arbitrary",)),
        name="mixer",
    )(x, mod, hf, hb, g1, g2, w_in, cw, wco, wro, wo, wr, br, *cast_ws)


def _sc_mesh():
    return plsc.VectorSubcoreMesh(core_axis_name="c", subcore_axis_name="s",
                                  num_cores=V7X_SC_CORES, num_subcores=V7X_SC_SUBCORES)


def _sc_worker_id():
    return lax.axis_index("s") * V7X_SC_CORES + lax.axis_index("c")


def _sc_dispatch(rows, dest, n_slots):
    n, width = rows.shape
    per_w = n // V7X_SC_WORKERS
    n_ch = per_w // SC_WINDOW
    assert n_ch * SC_WINDOW * V7X_SC_WORKERS == n
    idx = dest.reshape(2, V7X_SC_WORKERS, n_ch, SC_WINDOW).transpose(1, 0, 2, 3)

    def body(x_hbm, d_hbm, o_hbm, idx_v, buf, ld_sem, st_sem):
        wid = _sc_worker_id()
        pltpu.sync_copy(d_hbm.at[wid], idx_v)

        def load(j):
            src = x_hbm.at[pl.ds(wid * per_w + j * SC_WINDOW, SC_WINDOW)]
            return pltpu.async_copy(src, buf.at[j % SC_BUFS], ld_sem.at[j % SC_BUFS])

        def scatter(j):
            return [pltpu.async_copy(buf.at[j % SC_BUFS], o_hbm.at[idx_v.at[k, j]], st_sem.at[j % SC_BUFS])
                    for k in range(2)]

        loads = {j: load(j) for j in range(min(SC_BUFS - 1, n_ch))}
        stores, waited = {}, set()
        for j in range(n_ch):
            loads[j].wait()
            nxt = j + SC_BUFS - 1
            if nxt < n_ch:
                if nxt - SC_BUFS >= 0:
                    for cp in stores[nxt - SC_BUFS]:
                        cp.wait()
                    waited.add(nxt - SC_BUFS)
                loads[nxt] = load(nxt)
            stores[j] = scatter(j)
        for j in range(n_ch):
            if j not in waited:
                for cp in stores[j]:
                    cp.wait()

    return pl.kernel(
        body,
        out_type=jax.ShapeDtypeStruct((n_slots, width), jnp.int32),
        mesh=_sc_mesh(),
        scratch_types=[pltpu.VMEM((2, n_ch, SC_WINDOW), jnp.int32),
                       pltpu.VMEM((SC_BUFS, SC_WINDOW, width), jnp.int32),
                       pltpu.SemaphoreType.DMA((SC_BUFS,)),
                       pltpu.SemaphoreType.DMA((SC_BUFS,))],
        name="sc_dispatch",
    )(rows, idx)


def _sc_collect(rows, dest):
    n = dest.shape[1]
    width = rows.shape[1]
    per_w = n // V7X_SC_WORKERS
    n_ch = per_w // SC_WINDOW
    assert n_ch * SC_WINDOW * V7X_SC_WORKERS == n
    idx = dest.reshape(2, V7X_SC_WORKERS, n_ch, SC_WINDOW).transpose(1, 0, 2, 3)
    windows = [(k, j) for k in range(2) for j in range(n_ch)]

    def body(y_hbm, d_hbm, o_hbm, idx_v, buf, ld_sem, st_sem):
        wid = _sc_worker_id()
        pltpu.sync_copy(d_hbm.at[wid], idx_v)

        def gather(c):
            k, j = windows[c]
            return pltpu.async_copy(y_hbm.at[idx_v.at[k, j]], buf.at[c % SC_BUFS], ld_sem.at[c % SC_BUFS])

        def store(c):
            k, j = windows[c]
            dst = o_hbm.at[pl.ds(k * n + wid * per_w + j * SC_WINDOW, SC_WINDOW)]
            return pltpu.async_copy(buf.at[c % SC_BUFS], dst, st_sem.at[c % SC_BUFS])

        n_win = len(windows)
        loads = {c: gather(c) for c in range(min(SC_BUFS - 1, n_win))}
        stores, waited = {}, set()
        for c in range(n_win):
            loads[c].wait()
            nxt = c + SC_BUFS - 1
            if nxt < n_win:
                if nxt - SC_BUFS >= 0:
                    stores[nxt - SC_BUFS].wait()
                    waited.add(nxt - SC_BUFS)
                loads[nxt] = gather(nxt)
            stores[c] = store(c)
        for c in range(n_win):
            if c not in waited:
                stores[c].wait()

    return pl.kernel(
        body,
        out_type=jax.ShapeDtypeStruct((2 * n, width), jnp.int32),
        mesh=_sc_mesh(),
        scratch_types=[pltpu.VMEM((2, n_ch, SC_WINDOW), jnp.int32),
                       pltpu.VMEM((SC_BUFS, SC_WINDOW, width), jnp.int32),
                       pltpu.SemaphoreType.DMA((SC_BUFS,)),
                       pltpu.SemaphoreType.DMA((SC_BUFS,))],
        name="sc_collect",
    )(rows, idx)


def _expert_body(plan_ref, xs_hbm, w1_hbm, w3_hbm, w2_hbm, o_ref,
                 xbuf, w1s, w3s, w2s, xsem, sem):
    i = pl.program_id(0)
    n_used = plan_ref[PLAN_N_USED * V7X_LANES]
    n_runs = plan_ref[PLAN_N_RUNS * V7X_LANES]
    tme = xbuf.shape[1]
    run = plan_ref[PLAN_TILE_RUN * V7X_LANES + i]
    prev_run = plan_ref[PLAN_TILE_RUN * V7X_LANES + jnp.maximum(i - 1, 0)]
    first = jnp.logical_or(i == 0, run != prev_run)
    slot = run % EXPERT_W_BUFS

    def row_copy(t):
        s = t % EXPERT_ROW_BUFS
        start = t * tme if isinstance(t, int) else pl.multiple_of(t * tme, tme)
        src = xs_hbm.at[pl.ds(start, tme)]
        return pltpu.make_async_copy(src, xbuf.at[s], xsem.at[s])

    def weight_copies(r):
        e = plan_ref[PLAN_RUN_EXPERT * V7X_LANES + r]
        s = r % EXPERT_W_BUFS
        return (pltpu.make_async_copy(w1_hbm.at[e], w1s.at[s], sem.at[0, s]),
                pltpu.make_async_copy(w3_hbm.at[e], w3s.at[s], sem.at[1, s]),
                pltpu.make_async_copy(w2_hbm.at[e], w2s.at[s], sem.at[2, s]))

    @pl.when(i == 0)
    def _():
        for t in range(EXPERT_ROW_BUFS - 1):
            @pl.when(t < n_used)
            def _():
                row_copy(t).start()
        for r in range(EXPERT_W_BUFS - 1):
            @pl.when(r < n_runs)
            def _():
                for cp in weight_copies(r):
                    cp.start()

    @pl.when(jnp.logical_and(first, i < n_used))
    def _():
        @pl.when(run + (EXPERT_W_BUFS - 1) < n_runs)
        def _():
            for cp in weight_copies(run + (EXPERT_W_BUFS - 1)):
                cp.start()

        for cp in weight_copies(run):
            cp.wait()

    @pl.when(i < n_used)
    def _():
        @pl.when(i + (EXPERT_ROW_BUFS - 1) < n_used)
        def _():
            row_copy(i + (EXPERT_ROW_BUFS - 1)).start()

        row_copy(i).wait()
        lo, hi = _unpack_halves(xbuf[i % EXPERT_ROW_BUFS])
        lo = lo.astype(BF16)
        hi = hi.astype(BF16)
        half = D // 2
        a = _dot(lo, w1s[slot, 0:half, :]) + _dot(hi, w1s[slot, half:D, :])
        b = _dot(lo, w3s[slot, 0:half, :]) + _dot(hi, w3s[slot, half:D, :])
        z = (a * _sigmoid(a)) * b
        o_ref[...] = _pack_halves(_dot(z.astype(BF16), w2s[slot]))


def _experts(plan, xs, w1, w3, w2, tme):
    n_slots = xs.shape[0]
    grid_spec = pltpu.PrefetchScalarGridSpec(
        num_scalar_prefetch=1,
        grid=(n_slots // tme,),
        in_specs=[pl.BlockSpec(memory_space=pl.ANY),
                  pl.BlockSpec(memory_space=pl.ANY),
                  pl.BlockSpec(memory_space=pl.ANY),
                  pl.BlockSpec(memory_space=pl.ANY)],
        out_specs=pl.BlockSpec((tme, D // 2),
                               lambda i, plan: (jnp.clip(
                                   i, 0, jnp.maximum(plan[PLAN_N_USED * V7X_LANES] - 1, 0)), 0)),
        scratch_shapes=[pltpu.VMEM((EXPERT_ROW_BUFS, tme, D // 2), jnp.int32),
                        pltpu.VMEM((EXPERT_W_BUFS, D, D_EXPERT), BF16),
                        pltpu.VMEM((EXPERT_W_BUFS, D, D_EXPERT), BF16),
                        pltpu.VMEM((EXPERT_W_BUFS, D_EXPERT, D), BF16),
                        pltpu.SemaphoreType.DMA((EXPERT_ROW_BUFS,)),
                        pltpu.SemaphoreType.DMA((3, EXPERT_W_BUFS))],
    )
    return pl.pallas_call(
        _expert_body,
        grid_spec=grid_spec,
        out_shape=jax.ShapeDtypeStruct((n_slots, D // 2), jnp.int32),
        compiler_params=_params(("arbitrary",)),
        name="experts",
    )(plan, xs, w1, w3, w2)


def _final_body(x1_ref, mod_ref, y0_ref, y1_ref, wt_ref, gf_ref, o_ref):
    w0 = wt_ref[:, 0:1]
    w1 = wt_ref[:, 1:2]
    lo0, hi0 = _unpack_halves(y0_ref[...])
    lo1, hi1 = _unpack_halves(y1_ref[...])
    moe = jnp.concatenate([w0 * lo0 + w1 * lo1, w0 * hi0 + w1 * hi1], axis=1)
    x2 = x1_ref[...].astype(F32) + mod_ref[0, 5:6, :] * moe
    ms = jnp.mean(x2 * x2, axis=-1, keepdims=True)
    o_ref[...] = x2 * lax.rsqrt(ms + EPS) * gf_ref[...]


def _final(x1, mod, tiles_per_mod, yg, wts, g_final):
    n = x1.shape[0]
    tm = FIN_TM
    nt = n // tm
    return pl.pallas_call(
        _final_body,
        grid=(nt,),
        in_specs=[pl.BlockSpec((tm, D), lambda i: (i, 0)),
                  pl.BlockSpec((1, 6, D), lambda i: (i // tiles_per_mod, 0, 0)),
                  pl.BlockSpec((tm, D // 2), lambda i: (i, 0)),
                  pl.BlockSpec((tm, D // 2), lambda i: (i + nt, 0)),
                  pl.BlockSpec((tm, V7X_LANES), lambda i: (i, 0)),
                  _const_spec((1, D))],
        out_specs=pl.BlockSpec((tm, D), lambda i: (i, 0)),
        out_shape=jax.ShapeDtypeStruct((n, D), F32),
        compiler_params=_params(("parallel",)),
        name="final",
    )(x1, mod, yg, yg, wts, g_final)


def _plan_body(route_ref, cnt_ref, dest_ref, meta_ref, *, tme, n_slots):
    shift = tme.bit_length() - 1
    c = cnt_ref[...]
    erow = lax.broadcasted_iota(jnp.int32, c.shape, 0)

    def running_sum(v):
        for sh in (1, 2, 4, 8, 16):
            v = v + jnp.where(erow >= sh, pltpu.roll(v, sh, 0), 0)
        return v

    padded = lax.shift_left(lax.shift_right_logical(c + (tme - 1), shift), shift)
    pend = running_sum(padded)
    pstart = pend - padded
    run_of = running_sum((c > 0).astype(jnp.int32)) - 1

    acc = route_ref[2:4, :]
    ids = route_ref[0:2, :]
    for e in range(N_EXPERTS):
        acc = acc + jnp.where(ids == e, pstart[e:e + 1, 0:1], 0)
    dest_ref[0:2, :] = jnp.clip(acc, 0, n_slots - 1)
    dest_ref[2:EPG, :] = jnp.zeros((EPG - 2, acc.shape[1]), jnp.int32)

    lane = lax.broadcasted_iota(jnp.int32, (1, V7X_LANES), 1)
    tile_e = jnp.zeros((1, V7X_LANES), jnp.int32)
    for e in range(N_EXPERTS):
        tile_e = tile_e + (lane * tme >= pend[e:e + 1, 0:1]).astype(jnp.int32)
    tile_e = jnp.minimum(tile_e, N_EXPERTS - 1)
    tile_run = jnp.zeros((1, V7X_LANES), jnp.int32)
    run_e = jnp.zeros((1, V7X_LANES), jnp.int32)
    for e in range(N_EXPERTS):
        tile_run = tile_run + jnp.where(tile_e == e, run_of[e:e + 1, 0:1], 0)
        is_run = jnp.logical_and(lane == run_of[e:e + 1, 0:1], c[e:e + 1, 0:1] > 0)
        run_e = run_e + jnp.where(is_run, e, 0)
    n_runs = run_of[N_EXPERTS - 1:N_EXPERTS, :] + 1
    n_used = lax.shift_right_logical(pend[N_EXPERTS - 1:N_EXPERTS, :], shift)
    mrow = lax.broadcasted_iota(jnp.int32, (EPG, V7X_LANES), 0)
    meta_ref[...] = jnp.where(mrow == PLAN_TILE_RUN, tile_run, jnp.where(
        mrow == PLAN_RUN_EXPERT, run_e, jnp.where(
            mrow == PLAN_N_RUNS, n_runs, jnp.where(mrow == PLAN_N_USED, n_used, 0))))


def _slot_plan(route, cnt, n, tme):
    n_slots = ((2 * n + N_EXPERTS * (tme - 1)) // tme) * tme
    assert tme & (tme - 1) == 0 and n_slots // tme <= V7X_LANES
    dest, meta = pl.pallas_call(
        functools.partial(_plan_body, tme=tme, n_slots=n_slots),
        out_shape=[jax.ShapeDtypeStruct((EPG, n), jnp.int32),
                   jax.ShapeDtypeStruct((EPG, V7X_LANES), jnp.int32)],
        name="plan",
    )(route, cnt)
    return dest[0:2], meta.reshape(-1), n_slots


def _group(x, mod, mod_per_seq, h0, p, row_len, tme, cast_ws):
    n_b, seq_len, _ = x.shape
    n = n_b * seq_len
    xt = x.reshape(n, D)
    mod_seq = mod if mod_per_seq else jnp.broadcast_to(mod, (n_b, 6, D))
    xc = _xr_conv(x, mod_seq, p["g1"], p["w_in"], p["rnn_conv_w"], p["rnn_conv_b"])
    hf, hb, last = _rglru_scan(xc, h0, p["wg"], p["ba"], p["bx"], p["lam"])
    tiles_per_mod = (seq_len // MIX_TM) if mod_per_seq else (n // MIX_TM)
    outs = _mixer(xt, mod, tiles_per_mod, hf.reshape(n, D), hb.reshape(n, D),
                                     p["g1"], p["g2"], p["w_in"], p["conv_w"], p["wco"], p["wro"],
                                     p["wo"], p["wr"], p["br"], row_len, cast_ws)
    x1, h2, route, wts, cnt = outs[:5]
    casts = outs[5:]
    dest, plan, n_slots = _slot_plan(route, cnt, n, tme)
    xs = _sc_dispatch(h2, dest, n_slots)

    def finish(w1b, w3b, w2b):
        ys = _experts(plan, xs, w1b, w3b, w2b, tme)
        yg = _sc_collect(ys, dest)
        tiles_per_mod_f = (seq_len // FIN_TM) if mod_per_seq else (n // FIN_TM)
        y = _final(x1, mod, tiles_per_mod_f, yg, wts, p["g_final"])
        return y.reshape(n_b, seq_len, D)

    return finish, last, casts


def kernel(x_prompt, x_sample, state_rnn, c, c_ctx, w_ada, b_ada, g_norm1, g_norm2, w_in, conv_w, w_conv_out, rnn_conv_w, rnn_conv_b, w_gate_a, b_gate_a, w_gate_x, b_gate_x, lam, w_rnn_out, w_o, w_router_group, b_router_group, w_router_expert, b_router_expert, w1, w3, w2, g_final):
    assert w_ada.shape[0] == 1, "single layer"
    n_pb, n_sb = x_prompt.shape[0], x_sample.shape[0]

    cond = jnp.concatenate([c_ctx[None, :], c, jnp.zeros((16 - 1 - n_sb, D), F32)], axis=0)
    mod = _ada(cond, w_ada[0], b_ada[0]).reshape(16, 6, D)

    w_in_b = w_in[0].astype(BF16)
    pad_w = jnp.zeros((D, EPG - N_GROUPS), F32)
    wr = jnp.concatenate([w_router_group[0], pad_w, w_router_expert[0],
                          jnp.zeros((D, ROUTE_PAD - EPG - N_EXPERTS), F32)], axis=1)
    br = jnp.concatenate([b_router_group[0], jnp.full((EPG - N_GROUPS,), NEG_BIG, F32),
                          b_router_expert[0],
                          jnp.zeros((ROUTE_PAD - EPG - N_EXPERTS,), F32)]).reshape(1, ROUTE_PAD)
    wr_hi = wr.astype(BF16)
    p = dict(
        g1=g_norm1, g2=g_norm2, w_in=w_in_b,
        conv_w=conv_w[0], rnn_conv_w=rnn_conv_w[0], rnn_conv_b=rnn_conv_b,
        wg=(0.5 * jnp.concatenate([w_gate_a[0], w_gate_x[0]], axis=-1)).astype(BF16),
        ba=b_gate_a[0], bx=b_gate_x[0], lam=lam[0],
        wco=w_conv_out[0].astype(BF16), wro=w_rnn_out[0].astype(BF16), wo=w_o[0].astype(BF16),
        wr=jnp.concatenate([wr_hi, (wr - wr_hi.astype(F32)).astype(BF16)], axis=1), br=br,
        g_final=g_final.reshape(1, D),
    )

    h0_p = jnp.zeros((n_pb // SCAN_B, 2, SCAN_B, D), F32)
    finish_p, last, (w2b,) = _group(x_prompt, mod[0:1], False, h0_p, p, x_prompt.shape[1], 512,
                                    [w2[0]])
    state_new = last.transpose(0, 2, 1, 3).reshape(n_pb, 1, 2, D)

    h0_s = state_rnn[:, 0].reshape(n_sb // SCAN_B, SCAN_B, 2, D).transpose(0, 2, 1, 3)
    finish_s, _, (w1b, w3b) = _group(x_sample, mod[1:1 + n_sb], True, h0_s, p, GRID_W, 512,
                                     [w1[0], w3[0]])
    return (finish_p(w1b, w3b, w2b), finish_s(w1b, w3b, w2b), state_new)
```

```python
import functools

import jax
import jax.numpy as jnp
from jax import lax
from jax.experimental import pallas as pl
from jax.experimental.pallas import tpu as pltpu
from jax.experimental.pallas import tpu_sc as plsc

D = 1024
N_HEADS = 8
HEAD = D // N_HEADS
GRID_W = 64
RG_C = 8.0
N_GROUPS = 4
EPG = 8
N_EXPERTS = N_GROUPS * EPG
D_EXPERT = 512
EPS = 1e-6
F32 = jnp.float32
BF16 = jnp.bfloat16

V7X_LANES = 128
V7X_SUBLANES = 8
V7X_VMEM_LIMIT_BYTES = 56 * 1024 * 1024
V7X_SC_CORES = 2
V7X_SC_SUBCORES = 16
V7X_SC_WORKERS = V7X_SC_CORES * V7X_SC_SUBCORES
SC_WINDOW = 64
SC_BUFS = 3

XR_T = 256
XR_SUB = 32
XR_MB = 4
XR_LEFT = 2
XR_COL = 3
SCAN_T = 128
SCAN_B = V7X_SUBLANES
LOG2_E = 1.4426950408889634
TINY = 1e-30
MIX_TM = 512
FIN_TM = 2048
EXPERT_ROW_BUFS = 3
EXPERT_W_BUFS = 3
PLAN_TILE_RUN, PLAN_RUN_EXPERT, PLAN_N_RUNS, PLAN_N_USED = 0, 1, 2, 3
ROUTE_PAD = 128
NEG_BIG = -1e30


def _sigmoid(x):
    return 0.5 * jnp.tanh(0.5 * x) + 0.5


def _norm_mod(x, g, scale, shift):
    ms = jnp.mean(x * x, axis=-1, keepdims=True)
    return (x * lax.rsqrt(ms + EPS)) * (g * (1.0 + scale)) + shift


def _dot(a, b):
    return jnp.dot(a, b, preferred_element_type=F32)


def _params(sem, vmem=V7X_VMEM_LIMIT_BYTES):
    return pltpu.CompilerParams(dimension_semantics=sem, vmem_limit_bytes=vmem)


def _const_spec(shape):
    zeros = (0,) * len(shape)
    return pl.BlockSpec(shape, lambda *_: zeros, pipeline_mode=pl.Buffered(1))


def _ada_body(c_ref, w_ref, b_ref, o_ref):
    c = c_ref[...]
    s = (c * _sigmoid(c)).astype(BF16)
    o_ref[...] = _dot(s, w_ref[...].astype(BF16)) + b_ref[...]


def _ada(cond, w_ada, b_ada):
    rows = cond.shape[0]
    n_out = w_ada.shape[1]
    return pl.pallas_call(
        _ada_body,
        grid=(n_out // D,),
        in_specs=[pl.BlockSpec((rows, D), lambda i: (0, 0)),
                  pl.BlockSpec((D, D), lambda i: (0, i)),
                  pl.BlockSpec((1, D), lambda i: (0, i))],
        out_specs=pl.BlockSpec((rows, D), lambda i: (0, i)),
        out_shape=jax.ShapeDtypeStruct((rows, n_out), F32),
        compiler_params=_params(("parallel",)),
        name="ada",
    )(cond, w_ada, b_ada.reshape(1, n_out))


def _xr_body(x_ref, xn_ref, mod_ref, g_ref, w_ref, cw_ref, cb_ref, o_ref, xt):
    j = pl.program_id(1)
    n_t = pl.num_programs(1)
    t_len = x_ref.shape[1]
    nb = SCAN_B
    body0 = XR_LEFT * nb

    @pl.when(j == 0)
    def _():
        xt[:, 0:body0, :] = jnp.zeros((N_HEADS, body0, HEAD), F32)

    @pl.when(j > 0)
    def _():
        xt[:, 0:body0, :] = xt[:, t_len * nb:t_len * nb + body0, :]

    g = g_ref[...]
    for b0 in range(0, nb, XR_MB):
        h = jnp.concatenate(
            [_norm_mod(x_ref[b], g, mod_ref[b, 1:2, :], mod_ref[b, 0:1, :]).astype(BF16)
             for b in range(b0, b0 + XR_MB)], axis=0)
        r = _dot(h, w_ref[...])
        for i in range(XR_MB):
            for s in range(N_HEADS):
                xt[s, pl.ds(body0 + b0 + i, t_len, stride=nb), :] = (
                    r[i * t_len:(i + 1) * t_len, s * HEAD:(s + 1) * HEAD])

    hn = _norm_mod(xn_ref[:, 0, :], g, mod_ref[:, 1, :], mod_ref[:, 0, :]).astype(BF16)
    rn = jnp.where(j < n_t - 1, _dot(hn, w_ref[...]), 0.0)
    tail = body0 + t_len * nb
    for s in range(N_HEADS):
        xt[s, tail:tail + nb, :] = rn[:, s * HEAD:(s + 1) * HEAD]

    sub = XR_SUB
    for s in range(N_HEADS):
        sl = slice(s * HEAD, (s + 1) * HEAD)
        for t0 in range(0, t_len, sub):
            y = cb_ref[:, sl]
            for k in range(4):
                r0 = (t0 + k) * nb
                y = y + xt[s, r0:r0 + sub * nb, :] * cw_ref[k:k + 1, sl]
            o_ref[t0:t0 + sub, :, sl] = y.reshape(sub, nb, HEAD)


def _xr_conv(x, mod_seq, g1, w_in, cw, cb):
    n_b, seq_len, _ = x.shape
    t_len = min(XR_T, seq_len)
    assert n_b % SCAN_B == 0 and seq_len % t_len == 0 and t_len % XR_SUB == 0
    n_t = seq_len // t_len
    nxt = t_len // V7X_SUBLANES
    last_blk = seq_len // V7X_SUBLANES - 1
    return pl.pallas_call(
        _xr_body,
        grid=(n_b // SCAN_B, n_t),
        in_specs=[pl.BlockSpec((SCAN_B, t_len, D), lambda g, j: (g, j, 0)),
                  pl.BlockSpec((SCAN_B, V7X_SUBLANES, D),
                               lambda g, j: (g, jnp.minimum((j + 1) * nxt, last_blk), 0)),
                  pl.BlockSpec((SCAN_B, 6, D), lambda g, j: (g, 0, 0)),
                  _const_spec((1, D)),
                  pl.BlockSpec((D, D), lambda g, j: (0, XR_COL), pipeline_mode=pl.Buffered(1)),
                  _const_spec((4, D)),
                  _const_spec((1, D))],
        out_specs=pl.BlockSpec((t_len, SCAN_B, D), lambda g, j: (j, g, 0)),
        out_shape=jax.ShapeDtypeStruct((seq_len, n_b, D), F32),
        scratch_shapes=[pltpu.VMEM((N_HEADS, (t_len + XR_LEFT + 1) * SCAN_B, HEAD), F32)],
        compiler_params=_params(("parallel", "arbitrary")),
        name="xr_conv",
    )(x, x, mod_seq, g1, w_in, cw, cb)


def _scan_body(xf_ref, xb_ref, h0_ref, wg_ref, ba_ref, bx_ref, lam_ref,
               hf_ref, hb_ref, last_ref, a_s, u_s, h_s, hc):
    j = pl.program_id(1)
    n_t = pl.num_programs(1)
    t_len = xf_ref.shape[0]
    rows = SCAN_B * t_len

    @pl.when(j == 0)
    def _():
        for d in range(2):
            for s in range(N_HEADS):
                hc[d, s] = h0_ref[0, d, :, s * HEAD:(s + 1) * HEAD]

    c2s = []
    for d in range(2):
        z = -lam_ref[d:d + 1, :]
        sp = jnp.maximum(z, 0.0) + jnp.log(1.0 + jnp.exp(-jnp.abs(z)))
        c2s.append((-0.5 * RG_C * LOG2_E) * sp)

    for hd in range(N_HEADS):
        sl = slice(hd * HEAD, (hd + 1) * HEAD)
        for d, x_ref in ((0, xf_ref), (1, xb_ref)):
            xh = x_ref[:, :, sl].reshape(rows, HEAD)
            g = _dot(xh.astype(BF16), wg_ref[d, hd])
            t_r = jnp.tanh(g[:, :HEAD] + 0.5 * ba_ref[d:d + 1, sl])
            t_i = jnp.tanh(g[:, HEAD:] + 0.5 * bx_ref[d:d + 1, sl])
            a = jnp.exp2(c2s[d][:, sl] * t_r + c2s[d][:, sl])
            q = 0.25 - 0.25 * (a * a)
            half_mult = q * lax.rsqrt(jnp.maximum(q, TINY))
            u = half_mult * ((t_i + 1.0) * xh)
            a_s[d, hd] = a
            u_s[d, hd] = u
        h_f = hc[0, hd]
        h_b = hc[1, hd]
        for t in range(t_len):
            rf = t * SCAN_B
            rb = (t_len - 1 - t) * SCAN_B
            h_f = a_s[0, hd, rf:rf + SCAN_B, :] * h_f + u_s[0, hd, rf:rf + SCAN_B, :]
            h_b = a_s[1, hd, rb:rb + SCAN_B, :] * h_b + u_s[1, hd, rb:rb + SCAN_B, :]
            h_s[0, hd, rf:rf + SCAN_B, :] = h_f
            h_s[1, hd, rb:rb + SCAN_B, :] = h_b
        hc[0, hd] = h_f
        hc[1, hd] = h_b

    for d, o_ref in ((0, hf_ref), (1, hb_ref)):
        for hd in range(N_HEADS):
            for b in range(SCAN_B):
                o_ref[b, :, hd * HEAD:(hd + 1) * HEAD] = (
                    h_s[d, hd, pl.ds(b, t_len, stride=SCAN_B), :].astype(BF16))

    @pl.when(j == n_t - 1)
    def _():
        for d in range(2):
            for s in range(N_HEADS):
                last_ref[0, d, :, s * HEAD:(s + 1) * HEAD] = hc[d, s]


def _rglru_scan(xc, h0, wg, ba, bx, lam):
    seq_len, n_b, _ = xc.shape
    n_g = n_b // SCAN_B
    n_t = seq_len // SCAN_T
    blk = (SCAN_B, SCAN_T, D)
    blk_in = (SCAN_T, SCAN_B, D)
    return pl.pallas_call(
        _scan_body,
        grid=(n_g, n_t),
        in_specs=[pl.BlockSpec(blk_in, lambda g, j: (j, g, 0)),
                  pl.BlockSpec(blk_in, lambda g, j: (n_t - 1 - j, g, 0)),
                  pl.BlockSpec((1, 2, SCAN_B, D), lambda g, j: (g, 0, 0, 0)),
                  _const_spec((2, N_HEADS, HEAD, 2 * HEAD)),
                  _const_spec((2, D)),
                  _const_spec((2, D)),
                  _const_spec((2, D))],
        out_specs=[pl.BlockSpec(blk, lambda g, j: (g, j, 0)),
                   pl.BlockSpec(blk, lambda g, j: (g, n_t - 1 - j, 0)),
                   pl.BlockSpec((1, 2, SCAN_B, D), lambda g, j: (g, 0, 0, 0))],
        out_shape=[jax.ShapeDtypeStruct((n_b, seq_len, D), BF16),
                   jax.ShapeDtypeStruct((n_b, seq_len, D), BF16),
                   jax.ShapeDtypeStruct((n_g, 2, SCAN_B, D), F32)],
        scratch_shapes=[pltpu.VMEM((2, N_HEADS, SCAN_T * SCAN_B, HEAD), F32),
                        pltpu.VMEM((2, N_HEADS, SCAN_T * SCAN_B, HEAD), F32),
                        pltpu.VMEM((2, N_HEADS, SCAN_T * SCAN_B, HEAD), F32),
                        pltpu.VMEM((2, N_HEADS, SCAN_B, HEAD), F32)],
        compiler_params=_params(("parallel", "arbitrary")),
        name="rglru_scan",
    )(xc, xc, h0, wg, ba, bx, lam)


def _pack_halves(v):
    half = v.shape[1] // 2
    lo = lax.bitcast_convert_type(v[:, :half].astype(BF16).astype(F32), jnp.uint32)
    hi = lax.bitcast_convert_type(v[:, half:].astype(BF16).astype(F32), jnp.uint32)
    return lax.bitcast_convert_type((lo >> 16) | (hi & jnp.uint32(0xFFFF0000)), jnp.int32)


def _unpack_halves(p):
    u = lax.bitcast_convert_type(p, jnp.uint32)
    lo = lax.bitcast_convert_type(u << 16, F32)
    hi = lax.bitcast_convert_type(u & jnp.uint32(0xFFFF0000), F32)
    return lo, hi


def _mix_body(x_ref, mod_ref, hf_ref, hb_ref, g1_ref, g2_ref, win_ref, cw_ref,
              wco_ref, wro_ref, wo_ref, wr_ref, br_ref, *rest, row_len, n_cast):
    cast_in = rest[:n_cast]
    x1_ref, h2_ref, route_ref, wts_ref, cnt_ref = rest[n_cast:n_cast + 5]
    cast_out = rest[n_cast + 5:2 * n_cast + 5]
    seen = rest[2 * n_cast + 5]
    x = x_ref[...]
    tm = x.shape[0]
    h = _norm_mod(x, g1_ref[...], mod_ref[0, 1:2, :], mod_ref[0, 0:1, :]).astype(BF16)

    def proj(k):
        return _dot(h, win_ref[:, k * D:(k + 1) * D])

    cv = proj(1) * proj(2)
    pos = lax.broadcasted_iota(jnp.int32, (tm, 1), 0) % row_len
    conv = cv * cw_ref[1:2, :]
    conv = conv + jnp.where(pos >= 1, pltpu.roll(cv, 1, 0), 0.0) * cw_ref[0:1, :]
    conv = conv + jnp.where(pos <= row_len - 2, pltpu.roll(cv, tm - 1, 0), 0.0) * cw_ref[2:3, :]
    y_a = _dot((proj(0) * conv).astype(BF16), wco_ref[...])
    merged = _sigmoid(proj(5)) * y_a

    hs = hf_ref[...].astype(F32) + hb_ref[...].astype(F32)
    y_b = _dot((hs * jax.nn.gelu(proj(4))).astype(BF16), wro_ref[...])
    merged = merged + _sigmoid(proj(6)) * y_b

    mix = _dot(merged.astype(BF16), wo_ref[...])
    x1 = x + mod_ref[0, 2:3, :] * mix
    x1_ref[...] = x1.astype(BF16)
    for src, dst in zip(cast_in, cast_out):
        dst[...] = src[...].astype(BF16)
    h2 = _norm_mod(x1, g2_ref[...], mod_ref[0, 4:5, :], mod_ref[0, 3:4, :])
    h2_ref[...] = _pack_halves(h2)

    h2_hi = h2.astype(BF16)
    h2_lo = (h2 - h2_hi.astype(F32)).astype(BF16)
    big = _dot(h2_hi, wr_ref[...])
    logits = (big[:, :ROUTE_PAD] + big[:, ROUTE_PAD:]
              + _dot(h2_lo, wr_ref[:, :ROUTE_PAD]) + br_ref[...])
    lt = logits.T
    row = lax.broadcasted_iota(jnp.int32, (EPG, tm), 0)
    lg = lt[0:EPG]
    mg = jnp.max(lg, axis=0, keepdims=True)
    p_grp = 1.0 / jnp.sum(jnp.exp(lg - mg), axis=0, keepdims=True)
    grp = jnp.min(jnp.where(lg == mg, row, EPG), axis=0, keepdims=True)
    le = lt[EPG * N_GROUPS:EPG * (N_GROUPS + 1)]
    for g in range(N_GROUPS - 2, -1, -1):
        le = jnp.where(grp == g, lt[EPG * (g + 1):EPG * (g + 2)], le)
    me = jnp.max(le, axis=0, keepdims=True)
    ee = jnp.exp(le - me)
    pe = ee / jnp.sum(ee, axis=0, keepdims=True)
    p1 = jnp.max(pe, axis=0, keepdims=True)
    i1 = jnp.min(jnp.where(pe == p1, row, EPG), axis=0, keepdims=True)
    pe2 = jnp.where(row == i1, -1.0, pe)
    p2 = jnp.max(pe2, axis=0, keepdims=True)
    i2 = jnp.min(jnp.where(pe2 == p2, row, EPG), axis=0, keepdims=True)
    den = p1 + p2
    e1 = grp * EPG + i1
    e2 = grp * EPG + i2

    @pl.when(pl.program_id(0) == 0)
    def _():
        seen[...] = jnp.zeros_like(seen)

    erow = lax.broadcasted_iota(jnp.int32, (N_EXPERTS, tm), 0)
    hit1 = erow == e1
    hit2 = erow == e2
    both = jnp.where(jnp.logical_or(hit1, hit2), 1.0, 0.0)
    tri = jnp.where(lax.broadcasted_iota(jnp.int32, (tm, tm), 0)
                    <= lax.broadcasted_iota(jnp.int32, (tm, tm), 1), 1.0, 0.0).astype(BF16)
    before = _dot(both.astype(BF16), tri) - both + seen[...]
    r1 = jnp.sum(jnp.where(hit1, before, 0.0), axis=0, keepdims=True).astype(jnp.int32)
    r2 = jnp.sum(jnp.where(hit2, before, 0.0), axis=0, keepdims=True).astype(jnp.int32)
    total = seen[...] + jnp.sum(both, axis=1, keepdims=True)
    seen[...] = total
    cnt_ref[...] = total[:, :V7X_LANES].astype(jnp.int32)

    route_ref[...] = jnp.where(row == 0, e1, jnp.where(row == 1, e2, jnp.where(
        row == 2, r1, jnp.where(row == 3, r2, 0))))
    w8 = jnp.where(row == 0, p_grp * p1 / den, jnp.where(row == 1, p_grp * p2 / den, 0.0))
    wts_ref[...] = jnp.concatenate([w8, jnp.zeros((V7X_LANES - EPG, tm), F32)], axis=0).T


def _mixer(x, mod, tiles_per_mod, hf, hb, g1, g2, w_in, cw, wco, wro, wo, wr, br, row_len, cast_ws):
    n = x.shape[0]
    tm = MIX_TM
    assert tm % row_len == 0 and n % tm == 0
    steps = n // tm
    assert N_EXPERTS % steps == 0
    epb = N_EXPERTS // steps
    cast_specs = [pl.BlockSpec((epb,) + w.shape[1:], lambda i: (i, 0, 0)) for w in cast_ws]
    cast_shapes = [jax.ShapeDtypeStruct(w.shape, BF16) for w in cast_ws]
    mod_map = lambda i: (i // tiles_per_mod, 0, 0)
    tok = lambda i: (i, 0)
    col = lambda i: (0, i)
    return pl.pallas_call(
        functools.partial(_mix_body, row_len=row_len, n_cast=len(cast_ws)),
        grid=(steps,),
        in_specs=[pl.BlockSpec((tm, D), tok),
                  pl.BlockSpec((1, 6, D), mod_map),
                  pl.BlockSpec((tm, D), tok),
                  pl.BlockSpec((tm, D), tok),
                  _const_spec((1, D)),
                  _const_spec((1, D)),
                  _const_spec(w_in.shape),
                  _const_spec((3, D)),
                  _const_spec((D, D)),
                  _const_spec((D, D)),
                  _const_spec((D, D)),
                  _const_spec((D, 2 * ROUTE_PAD)),
                  _const_spec((1, ROUTE_PAD))] + cast_specs,
        out_specs=[pl.BlockSpec((tm, D), tok),
                   pl.BlockSpec((tm, D // 2), tok),
                   pl.BlockSpec((EPG, tm), col),
                   pl.BlockSpec((tm, V7X_LANES), tok),
                   pl.BlockSpec((N_EXPERTS, V7X_LANES), lambda i: (0, 0))] + cast_specs,
        out_shape=[jax.ShapeDtypeStruct((n, D), BF16),
                   jax.ShapeDtypeStruct((n, D // 2), jnp.int32),
                   jax.ShapeDtypeStruct((EPG, n), jnp.int32),
                   jax.ShapeDtypeStruct((n, V7X_LANES), F32),
                   jax.ShapeDtypeStruct((N_EXPERTS, V7X_LANES), jnp.int32)] + cast_shapes,
        scratch_shapes=[pltpu.VMEM((N_EXPERTS, tm), F32)],
        compiler_params=_params(("arbitrary",)),
        name="mixer",
    )(x, mod, hf, hb, g1, g2, w_in, cw, wco, wro, wo, wr, br, *cast_ws)


def _sc_mesh():
    return plsc.VectorSubcoreMesh(core_axis_name="c", subcore_axis_name="s",
                                  num_cores=V7X_SC_CORES, num_subcores=V7X_SC_SUBCORES)


def _sc_worker_id():
    return lax.axis_index("s") * V7X_SC_CORES + lax.axis_index("c")


def _sc_dispatch(rows, dest, n_slots):
    n, width = rows.shape
    per_w = n // V7X_SC_WORKERS
    n_ch = per_w // SC_WINDOW
    assert n_ch * SC_WINDOW * V7X_SC_WORKERS == n
    idx = dest.reshape(2, V7X_SC_WORKERS, n_ch, SC_WINDOW).transpose(1, 0, 2, 3)

    def body(x_hbm, d_hbm, o_hbm, idx_v, buf, ld_sem, st_sem):
        wid = _sc_worker_id()
        pltpu.sync_copy(d_hbm.at[wid], idx_v)

        def load(j):
            src = x_hbm.at[pl.ds(wid * per_w + j * SC_WINDOW, SC_WINDOW)]
            return pltpu.async_copy(src, buf.at[j % SC_BUFS], ld_sem.at[j % SC_BUFS])

        def scatter(j):
            return [pltpu.async_copy(buf.at[j % SC_BUFS], o_hbm.at[idx_v.at[k, j]], st_sem.at[j % SC_BUFS])
                    for k in range(2)]

        loads = {j: load(j) for j in range(min(SC_BUFS - 1, n_ch))}
        stores, waited = {}, set()
        for j in range(n_ch):
            loads[j].wait()
            nxt = j + SC_BUFS - 1
            if nxt < n_ch:
                if nxt - SC_BUFS >= 0:
                    for cp in stores[nxt - SC_BUFS]:
                        cp.wait()
                    waited.add(nxt - SC_BUFS)
                loads[nxt] = load(nxt)
            stores[j] = scatter(j)
        for j in range(n_ch):
            if j not in waited:
                for cp in stores[j]:
                    cp.wait()

    return pl.kernel(
        body,
        out_type=jax.ShapeDtypeStruct((n_slots, width), jnp.int32),
        mesh=_sc_mesh(),
        scratch_types=[pltpu.VMEM((2, n_ch, SC_WINDOW), jnp.int32),
                       pltpu.VMEM((SC_BUFS, SC_WINDOW, width), jnp.int32),
                       pltpu.SemaphoreType.DMA((SC_BUFS,)),
                       pltpu.SemaphoreType.DMA((SC_BUFS,))],
        name="sc_dispatch",
    )(rows, idx)


def _sc_collect(rows, dest):
    n = dest.shape[1]
    width = rows.shape[1]
    per_w = n // V7X_SC_WORKERS
    n_ch = per_w // SC_WINDOW
    assert n_ch * SC_WINDOW * V7X_SC_WORKERS == n
    idx = dest.reshape(2, V7X_SC_WORKERS, n_ch, SC_WINDOW).transpose(1, 0, 2, 3)
    windows = [(k, j) for k in range(2) for j in range(n_ch)]

    def body(y_hbm, d_hbm, o_hbm, idx_v, buf, ld_sem, st_sem):
        wid = _sc_worker_id()
        pltpu.sync_copy(d_hbm.at[wid], idx_v)

        def gather(c):
            k, j = windows[c]
            return pltpu.async_copy(y_hbm.at[idx_v.at[k, j]], buf.at[c % SC_BUFS], ld_sem.at[c % SC_BUFS])

        def store(c):
            k, j = windows[c]
            dst = o_hbm.at[pl.ds(k * n + wid * per_w + j * SC_WINDOW, SC_WINDOW)]
            return pltpu.async_copy(buf.at[c % SC_BUFS], dst, st_sem.at[c % SC_BUFS])

        n_win = len(windows)
        loads = {c: gather(c) for c in range(min(SC_BUFS - 1, n_win))}
        stores, waited = {}, set()
        for c in range(n_win):
            loads[c].wait()
            nxt = c + SC_BUFS - 1
            if nxt < n_win:
                if nxt - SC_BUFS >= 0:
                    stores[nxt - SC_BUFS].wait()
                    waited.add(nxt - SC_BUFS)
                loads[nxt] = gather(nxt)
            stores[c] = store(c)
        for c in range(n_win):
            if c not in waited:
                stores[c].wait()

    return pl.kernel(
        body,
        out_type=jax.ShapeDtypeStruct((2 * n, width), jnp.int32),
        mesh=_sc_mesh(),
        scratch_types=[pltpu.VMEM((2, n_ch, SC_WINDOW), jnp.int32),
                       pltpu.VMEM((SC_BUFS, SC_WINDOW, width), jnp.int32),
                       pltpu.SemaphoreType.DMA((SC_BUFS,)),
                       pltpu.SemaphoreType.DMA((SC_BUFS,))],
        name="sc_collect",
    )(rows, idx)


def _expert_body(plan_ref, xs_hbm, w1_hbm, w3_hbm, w2_hbm, o_ref,
                 xbuf, w1s, w3s, w2s, xsem, sem):
    i = pl.program_id(0)
    n_used = plan_ref[PLAN_N_USED * V7X_LANES]
    n_runs = plan_ref[PLAN_N_RUNS * V7X_LANES]
    tme = xbuf.shape[1]
    run = plan_ref[PLAN_TILE_RUN * V7X_LANES + i]
    prev_run = plan_ref[PLAN_TILE_RUN * V7X_LANES + jnp.maximum(i - 1, 0)]
    first = jnp.logical_or(i == 0, run != prev_run)
    slot = run % EXPERT_W_BUFS

    def row_copy(t):
        s = t % EXPERT_ROW_BUFS
        start = t * tme if isinstance(t, int) else pl.multiple_of(t * tme, tme)
        src = xs_hbm.at[pl.ds(start, tme)]
        return pltpu.make_async_copy(src, xbuf.at[s], xsem.at[s])

    def weight_copies(r):
        e = plan_ref[PLAN_RUN_EXPERT * V7X_LANES + r]
        s = r % EXPERT_W_BUFS
        return (pltpu.make_async_copy(w1_hbm.at[e], w1s.at[s], sem.at[0, s]),
                pltpu.make_async_copy(w3_hbm.at[e], w3s.at[s], sem.at[1, s]),
                pltpu.make_async_copy(w2_hbm.at[e], w2s.at[s], sem.at[2, s]))

    @pl.when(i == 0)
    def _():
        for t in range(EXPERT_ROW_BUFS - 1):
            @pl.when(t < n_used)
            def _():
                row_copy(t).start()
        for r in range(EXPERT_W_BUFS - 1):
            @pl.when(r < n_runs)
            def _():
                for cp in weight_copies(r):
                    cp.start()

    @pl.when(jnp.logical_and(first, i < n_used))
    def _():
        @pl.when(run + (EXPERT_W_BUFS - 1) < n_runs)
        def _():
            for cp in weight_copies(run + (EXPERT_W_BUFS - 1)):
                cp.start()

        for cp in weight_copies(run):
            cp.wait()

    @pl.when(i < n_used)
    def _():
        @pl.when(i + (EXPERT_ROW_BUFS - 1) < n_used)
        def _():
            row_copy(i + (EXPERT_ROW_BUFS - 1)).start()

        row_copy(i).wait()
        lo, hi = _unpack_halves(xbuf[i % EXPERT_ROW_BUFS])
        lo = lo.astype(BF16)
        hi = hi.astype(BF16)
        half = D // 2
        a = _dot(lo, w1s[slot, 0:half, :]) + _dot(hi, w1s[slot, half:D, :])
        b = _dot(lo, w3s[slot, 0:half, :]) + _dot(hi, w3s[slot, half:D, :])
        z = (a * _sigmoid(a)) * b
        o_ref[...] = _pack_halves(_dot(z.astype(BF16), w2s[slot]))


def _experts(plan, xs, w1, w3, w2, tme):
    n_slots = xs.shape[0]
    grid_spec = pltpu.PrefetchScalarGridSpec(
        num_scalar_prefetch=1,
        grid=(n_slots // tme,),
        in_specs=[pl.BlockSpec(memory_space=pl.ANY),
                  pl.BlockSpec(memory_space=pl.ANY),
                  pl.BlockSpec(memory_space=pl.ANY),
                  pl.BlockSpec(memory_space=pl.ANY)],
        out_specs=pl.BlockSpec((tme, D // 2),
                               lambda i, plan: (jnp.clip(
                                   i, 0, jnp.maximum(plan[PLAN_N_USED * V7X_LANES] - 1, 0)), 0)),
        scratch_shapes=[pltpu.VMEM((EXPERT_ROW_BUFS, tme, D // 2), jnp.int32),
                        pltpu.VMEM((EXPERT_W_BUFS, D, D_EXPERT), BF16),
                        pltpu.VMEM((EXPERT_W_BUFS, D, D_EXPERT), BF16),
                        pltpu.VMEM((EXPERT_W_BUFS, D_EXPERT, D), BF16),
                        pltpu.SemaphoreType.DMA((EXPERT_ROW_BUFS,)),
                        pltpu.SemaphoreType.DMA((3, EXPERT_W_BUFS))],
    )
    return pl.pallas_call(
        _expert_body,
        grid_spec=grid_spec,
        out_shape=jax.ShapeDtypeStruct((n_slots, D // 2), jnp.int32),
        compiler_params=_params(("arbitrary",)),
        name="experts",
    )(plan, xs, w1, w3, w2)


def _final_body(x1_ref, mod_ref, y0_ref, y1_ref, wt_ref, gf_ref, o_ref):
    w0 = wt_ref[:, 0:1]
    w1 = wt_ref[:, 1:2]
    lo0, hi0 = _unpack_halves(y0_ref[...])
    lo1, hi1 = _unpack_halves(y1_ref[...])
    moe = jnp.concatenate([w0 * lo0 + w1 * lo1, w0 * hi0 + w1 * hi1], axis=1)
    x2 = x1_ref[...].astype(F32) + mod_ref[0, 5:6, :] * moe
    ms = jnp.mean(x2 * x2, axis=-1, keepdims=True)
    o_ref[...] = x2 * lax.rsqrt(ms + EPS) * gf_ref[...]


def _final(x1, mod, tiles_per_mod, yg, wts, g_final):
    n = x1.shape[0]
    tm = FIN_TM
    nt = n // tm
    return pl.pallas_call(
        _final_body,
        grid=(nt,),
        in_specs=[pl.BlockSpec((tm, D), lambda i: (i, 0)),
                  pl.BlockSpec((1, 6, D), lambda i: (i // tiles_per_mod, 0, 0)),
                  pl.BlockSpec((tm, D // 2), lambda i: (i, 0)),
                  pl.BlockSpec((tm, D // 2), lambda i: (i + nt, 0)),
                  pl.BlockSpec((tm, V7X_LANES), lambda i: (i, 0)),
                  _const_spec((1, D))],
        out_specs=pl.BlockSpec((tm, D), lambda i: (i, 0)),
        out_shape=jax.ShapeDtypeStruct((n, D), F32),
        compiler_params=_params(("parallel",)),
        name="final",
    )(x1, mod, yg, yg, wts, g_final)


def _plan_body(route_ref, cnt_ref, dest_ref, meta_ref, *, tme, n_slots):
    shift = tme.bit_length() - 1
    c = cnt_ref[...]
    erow = lax.broadcasted_iota(jnp.int32, c.shape, 0)

    def running_sum(v):
        for sh in (1, 2, 4, 8, 16):
            v = v + jnp.where(erow >= sh, pltpu.roll(v, sh, 0), 0)
        return v

    padded = lax.shift_left(lax.shift_right_logical(c + (tme - 1), shift), shift)
    pend = running_sum(padded)
    pstart = pend - padded
    run_of = running_sum((c > 0).astype(jnp.int32)) - 1

    acc = route_ref[2:4, :]
    ids = route_ref[0:2, :]
    for e in range(N_EXPERTS):
        acc = acc + jnp.where(ids == e, pstart[e:e + 1, 0:1], 0)
    dest_ref[0:2, :] = jnp.clip(acc, 0, n_slots - 1)
    dest_ref[2:EPG, :] = jnp.zeros((EPG - 2, acc.shape[1]), jnp.int32)

    lane = lax.broadcasted_iota(jnp.int32, (1, V7X_LANES), 1)
    tile_e = jnp.zeros((1, V7X_LANES), jnp.int32)
    for e in range(N_EXPERTS):
        tile_e = tile_e + (lane * tme >= pend[e:e + 1, 0:1]).astype(jnp.int32)
    tile_e = jnp.minimum(tile_e, N_EXPERTS - 1)
    tile_run = jnp.zeros((1, V7X_LANES), jnp.int32)
    run_e = jnp.zeros((1, V7X_LANES), jnp.int32)
    for e in range(N_EXPERTS):
        tile_run = tile_run + jnp.where(tile_e == e, run_of[e:e + 1, 0:1], 0)
        is_run = jnp.logical_and(lane == run_of[e:e + 1, 0:1], c[e:e + 1, 0:1] > 0)
        run_e = run_e + jnp.where(is_run, e, 0)
    n_runs = run_of[N_EXPERTS - 1:N_EXPERTS, :] + 1
    n_used = lax.shift_right_logical(pend[N_EXPERTS - 1:N_EXPERTS, :], shift)
    mrow = lax.broadcasted_iota(jnp.int32, (EPG, V7X_LANES), 0)
    meta_ref[...] = jnp.where(mrow == PLAN_TILE_RUN, tile_run, jnp.where(
        mrow == PLAN_RUN_EXPERT, run_e, jnp.where(
            mrow == PLAN_N_RUNS, n_runs, jnp.where(mrow == PLAN_N_USED, n_used, 0))))


def _slot_plan(route, cnt, n, tme):
    n_slots = ((2 * n + N_EXPERTS * (tme - 1)) // tme) * tme
    assert tme & (tme - 1) == 0 and n_slots // tme <= V7X_LANES
    dest, meta = pl.pallas_call(
        functools.partial(_plan_body, tme=tme, n_slots=n_slots),
        out_shape=[jax.ShapeDtypeStruct((EPG, n), jnp.int32),
                   jax.ShapeDtypeStruct((EPG, V7X_LANES), jnp.int32)],
        name="plan",
    )(route, cnt)
    return dest[0:2], meta.reshape(-1), n_slots


def _group(x, mod, mod_per_seq, h0, p, row_len, tme, cast_ws):
    n_b, seq_len, _ = x.shape
    n = n_b * seq_len
    xt = x.reshape(n, D)
    mod_seq = mod if mod_per_seq else jnp.broadcast_to(mod, (n_b, 6, D))
    xc = _xr_conv(x, mod_seq, p["g1"], p["w_in"], p["rnn_conv_w"], p["rnn_conv_b"])
    hf, hb, last = _rglru_scan(xc, h0, p["wg"], p["ba"], p["bx"], p["lam"])
    tiles_per_mod = (seq_len // MIX_TM) if mod_per_seq else (n // MIX_TM)
    outs = _mixer(xt, mod, tiles_per_mod, hf.reshape(n, D), hb.reshape(n, D),
                                     p["g1"], p["g2"], p["w_in"], p["conv_w"], p["wco"], p["wro"],
                                     p["wo"], p["wr"], p["br"], row_len, cast_ws)
    x1, h2, route, wts, cnt = outs[:5]
    casts = outs[5:]
    dest, plan, n_slots = _slot_plan(route, cnt, n, tme)
    xs = _sc_dispatch(h2, dest, n_slots)

    def finish(w1b, w3b, w2b):
        ys = _experts(plan, xs, w1b, w3b, w2b, tme)
        yg = _sc_collect(ys, dest)
        tiles_per_mod_f = (seq_len // FIN_TM) if mod_per_seq else (n // FIN_TM)
        y = _final(x1, mod, tiles_per_mod_f, yg, wts, p["g_final"])
        return y.reshape(n_b, seq_len, D)

    return finish, last, casts


def kernel(x_prompt, x_sample, state_rnn, c, c_ctx, w_ada, b_ada, g_norm1, g_norm2, w_in, conv_w, w_conv_out, rnn_conv_w, rnn_conv_b, w_gate_a, b_gate_a, w_gate_x, b_gate_x, lam, w_rnn_out, w_o, w_router_group, b_router_group, w_router_expert, b_router_expert, w1, w3, w2, g_final):
    assert w_ada.shape[0] == 1, "single layer"
    n_pb, n_sb = x_prompt.shape[0], x_sample.shape[0]

    cond = jnp.concatenate([c_ctx[None, :], c, jnp.zeros((16 - 1 - n_sb, D), F32)], axis=0)
    mod = _ada(cond, w_ada[0], b_ada[0]).reshape(16, 6, D)

    w_in_b = w_in[0].astype(BF16)
    pad_w = jnp.zeros((D, EPG - N_GROUPS), F32)
    wr = jnp.concatenate([w_router_group[0], pad_w, w_router_expert[0],
                          jnp.zeros((D, ROUTE_PAD - EPG - N_EXPERTS), F32)], axis=1)
    br = jnp.concatenate([b_router_group[0], jnp.full((EPG - N_GROUPS,), NEG_BIG, F32),
                          b_router_expert[0],
                          jnp.zeros((ROUTE_PAD - EPG - N_EXPERTS,), F32)]).reshape(1, ROUTE_PAD)
    wr_hi = wr.astype(BF16)
    p = dict(
        g1=g_norm1, g2=g_norm2, w_in=w_in_b,
        conv_w=conv_w[0], rnn_conv_w=rnn_conv_w[0], rnn_conv_b=rnn_conv_b,
        wg=(0.5 * jnp.concatenate([w_gate_a[0], w_gate_x[0]], axis=-1)).astype(BF16),
        ba=b_gate_a[0], bx=b_gate_x[0], lam=lam[0],
        wco=w_conv_out[0].astype(BF16), wro=w_rnn_out[0].astype(BF16), wo=w_o[0].astype(BF16),
        wr=jnp.concatenate([wr_hi, (wr - wr_hi.astype(F32)).astype(BF16)], axis=1), br=br,
        g_final=g_final.reshape(1, D),
    )

    h0_p = jnp.zeros((n_pb // SCAN_B, 2, SCAN_B, D), F32)
    finish_p, last, (w2b,) = _group(x_prompt, mod[0:1], False, h0_p, p, x_prompt.shape[1], 512,
                                    [w2[0]])
    state_new = last.transpose(0, 2, 1, 3).reshape(n_pb, 1, 2, D)

    h0_s = state_rnn[:, 0].reshape(n_sb // SCAN_B, SCAN_B, 2, D).transpose(0, 2, 1, 3)
    finish_s, _, (w1b, w3b) = _group(x_sample, mod[1:1 + n_sb], True, h0_s, p, GRID_W, 512,
                                     [w1[0], w3[0]])
    return (finish_p(w1b, w3b, w2b), finish_s(w1b, w3b, w2b), state_new)
```

```python
import functools

import jax
import jax.numpy as jnp
from jax import lax
from jax.experimental import pallas as pl
from jax.experimental.pallas import tpu as pltpu
from jax.experimental.pallas import tpu_sc as plsc

D = 1024
N_HEADS = 8
HEAD = D // N_HEADS
GRID_W = 64
RG_C = 8.0
N_GROUPS = 4
EPG = 8
N_EXPERTS = N_GROUPS * EPG
D_EXPERT = 512
EPS = 1e-6
F32 = jnp.float32
BF16 = jnp.bfloat16

V7X_LANES = 128
V7X_SUBLANES = 8
V7X_VMEM_LIMIT_BYTES = 56 * 1024 * 1024
V7X_SC_CORES = 2
V7X_SC_SUBCORES = 16
V7X_SC_WORKERS = V7X_SC_CORES * V7X_SC_SUBCORES
SC_WINDOW = 64
SC_BUFS = 3

XR_T = 256
XR_SUB = 32
XR_MB = 4
XR_LEFT = 2
XR_COL = 3
SCAN_T = 128
SCAN_B = V7X_SUBLANES
LOG2_E = 1.4426950408889634
TINY = 1e-30
MIX_TM = 512
FIN_TM = 1024
EXPERT_ROW_BUFS = 4
EXPERT_W_BUFS = 3
PLAN_TILE_RUN, PLAN_RUN_EXPERT, PLAN_N_RUNS, PLAN_N_USED = 0, 1, 2, 3
ROUTE_PAD = 128
NEG_BIG = -1e30


def _sigmoid(x):
    return 0.5 * jnp.tanh(0.5 * x) + 0.5


def _norm_mod(x, g, scale, shift):
    ms = jnp.mean(x * x, axis=-1, keepdims=True)
    return (x * lax.rsqrt(ms + EPS)) * (g * (1.0 + scale)) + shift


def _dot(a, b):
    return jnp.dot(a, b, preferred_element_type=F32)


def _params(sem, vmem=V7X_VMEM_LIMIT_BYTES):
    return pltpu.CompilerParams(dimension_semantics=sem, vmem_limit_bytes=vmem)


def _const_spec(shape):
    zeros = (0,) * len(shape)
    return pl.BlockSpec(shape, lambda *_: zeros, pipeline_mode=pl.Buffered(1))


def _ada_body(c_ref, w_ref, b_ref, o_ref):
    c = c_ref[...]
    s = (c * _sigmoid(c)).astype(BF16)
    o_ref[...] = _dot(s, w_ref[...].astype(BF16)) + b_ref[...]


def _ada(cond, w_ada, b_ada):
    rows = cond.shape[0]
    n_out = w_ada.shape[1]
    return pl.pallas_call(
        _ada_body,
        grid=(n_out // D,),
        in_specs=[pl.BlockSpec((rows, D), lambda i: (0, 0)),
                  pl.BlockSpec((D, D), lambda i: (0, i)),
                  pl.BlockSpec((1, D), lambda i: (0, i))],
        out_specs=pl.BlockSpec((rows, D), lambda i: (0, i)),
        out_shape=jax.ShapeDtypeStruct((rows, n_out), F32),
        compiler_params=_params(("parallel",)),
        name="ada",
    )(cond, w_ada, b_ada.reshape(1, n_out))


def _xr_body(x_ref, xn_ref, mod_ref, g_ref, w_ref, cw_ref, cb_ref, o_ref, xt):
    j = pl.program_id(1)
    n_t = pl.num_programs(1)
    t_len = x_ref.shape[1]
    nb = SCAN_B
    body0 = XR_LEFT * nb

    @pl.when(j == 0)
    def _():
        xt[:, 0:body0, :] = jnp.zeros((N_HEADS, body0, HEAD), F32)

    @pl.when(j > 0)
    def _():
        xt[:, 0:body0, :] = xt[:, t_len * nb:t_len * nb + body0, :]

    g = g_ref[...]
    for b0 in range(0, nb, XR_MB):
        h = jnp.concatenate(
            [_norm_mod(x_ref[b], g, mod_ref[b, 1:2, :], mod_ref[b, 0:1, :]).astype(BF16)
             for b in range(b0, b0 + XR_MB)], axis=0)
        r = _dot(h, w_ref[...])
        for i in range(XR_MB):
            for s in range(N_HEADS):
                xt[s, pl.ds(body0 + b0 + i, t_len, stride=nb), :] = (
                    r[i * t_len:(i + 1) * t_len, s * HEAD:(s + 1) * HEAD])

    hn = _norm_mod(xn_ref[:, 0, :], g, mod_ref[:, 1, :], mod_ref[:, 0, :]).astype(BF16)
    rn = jnp.where(j < n_t - 1, _dot(hn, w_ref[...]), 0.0)
    tail = body0 + t_len * nb
    for s in range(N_HEADS):
        xt[s, tail:tail + nb, :] = rn[:, s * HEAD:(s + 1) * HEAD]

    sub = XR_SUB
    for s in range(N_HEADS):
        sl = slice(s * HEAD, (s + 1) * HEAD)
        for t0 in range(0, t_len, sub):
            y = cb_ref[:, sl]
            for k in range(4):
                r0 = (t0 + k) * nb
                y = y + xt[s, r0:r0 + sub * nb, :] * cw_ref[k:k + 1, sl]
            o_ref[t0:t0 + sub, :, sl] = y.reshape(sub, nb, HEAD)


def _xr_conv(x, mod_seq, g1, w_in, cw, cb):
    n_b, seq_len, _ = x.shape
    t_len = min(XR_T, seq_len)
    assert n_b % SCAN_B == 0 and seq_len % t_len == 0 and t_len % XR_SUB == 0
    n_t = seq_len // t_len
    nxt = t_len // V7X_SUBLANES
    last_blk = seq_len // V7X_SUBLANES - 1
    return pl.pallas_call(
        _xr_body,
        grid=(n_b // SCAN_B, n_t),
        in_specs=[pl.BlockSpec((SCAN_B, t_len, D), lambda g, j: (g, j, 0)),
                  pl.BlockSpec((SCAN_B, V7X_SUBLANES, D),
                               lambda g, j: (g, jnp.minimum((j + 1) * nxt, last_blk), 0)),
                  pl.BlockSpec((SCAN_B, 6, D), lambda g, j: (g, 0, 0)),
                  _const_spec((1, D)),
                  pl.BlockSpec((D, D), lambda g, j: (0, XR_COL), pipeline_mode=pl.Buffered(1)),
                  _const_spec((4, D)),
                  _const_spec((1, D))],
        out_specs=pl.BlockSpec((t_len, SCAN_B, D), lambda g, j: (j, g, 0)),
        out_shape=jax.ShapeDtypeStruct((seq_len, n_b, D), F32),
        scratch_shapes=[pltpu.VMEM((N_HEADS, (t_len + XR_LEFT + 1) * SCAN_B, HEAD), F32)],
        compiler_params=_params(("parallel", "arbitrary")),
        name="xr_conv",
    )(x, x, mod_seq, g1, w_in, cw, cb)


def _scan_body(xf_ref, xb_ref, h0_ref, wg_ref, ba_ref, bx_ref, lam_ref,
               hf_ref, hb_ref, last_ref, a_s, u_s, h_s, hc):
    j = pl.program_id(1)
    n_t = pl.num_programs(1)
    t_len = xf_ref.shape[0]
    rows = SCAN_B * t_len

    @pl.when(j == 0)
    def _():
        for d in range(2):
            for s in range(N_HEADS):
                hc[d, s] = h0_ref[0, d, :, s * HEAD:(s + 1) * HEAD]

    c2s = []
    for d in range(2):
        z = -lam_ref[d:d + 1, :]
        sp = jnp.maximum(z, 0.0) + jnp.log(1.0 + jnp.exp(-jnp.abs(z)))
        c2s.append((-0.5 * RG_C * LOG2_E) * sp)

    for hd in range(N_HEADS):
        sl = slice(hd * HEAD, (hd + 1) * HEAD)
        for d, x_ref in ((0, xf_ref), (1, xb_ref)):
            xh = x_ref[:, :, sl].reshape(rows, HEAD)
            g = _dot(xh.astype(BF16), wg_ref[d, hd])
            t_r = jnp.tanh(g[:, :HEAD] + 0.5 * ba_ref[d:d + 1, sl])
            t_i = jnp.tanh(g[:, HEAD:] + 0.5 * bx_ref[d:d + 1, sl])
            a = jnp.exp2(c2s[d][:, sl] * t_r + c2s[d][:, sl])
            q = 0.25 - 0.25 * (a * a)
            half_mult = q * lax.rsqrt(jnp.maximum(q, TINY))
            u = half_mult * ((t_i + 1.0) * xh)
            a_s[d, hd] = a
            u_s[d, hd] = u
        h_f = hc[0, hd]
        h_b = hc[1, hd]
        for t in range(t_len):
            rf = t * SCAN_B
            rb = (t_len - 1 - t) * SCAN_B
            h_f = a_s[0, hd, rf:rf + SCAN_B, :] * h_f + u_s[0, hd, rf:rf + SCAN_B, :]
            h_b = a_s[1, hd, rb:rb + SCAN_B, :] * h_b + u_s[1, hd, rb:rb + SCAN_B, :]
            h_s[0, hd, rf:rf + SCAN_B, :] = h_f
            h_s[1, hd, rb:rb + SCAN_B, :] = h_b
        hc[0, hd] = h_f
        hc[1, hd] = h_b

    for d, o_ref in ((0, hf_ref), (1, hb_ref)):
        for hd in range(N_HEADS):
            for b in range(SCAN_B):
                o_ref[b, :, hd * HEAD:(hd + 1) * HEAD] = (
                    h_s[d, hd, pl.ds(b, t_len, stride=SCAN_B), :].astype(BF16))

    @pl.when(j == n_t - 1)
    def _():
        for d in range(2):
            for s in range(N_HEADS):
                last_ref[0, d, :, s * HEAD:(s + 1) * HEAD] = hc[d, s]


def _rglru_scan(xc, h0, wg, ba, bx, lam):
    seq_len, n_b, _ = xc.shape
    n_g = n_b // SCAN_B
    n_t = seq_len // SCAN_T
    blk = (SCAN_B, SCAN_T, D)
    blk_in = (SCAN_T, SCAN_B, D)
    return pl.pallas_call(
        _scan_body,
        grid=(n_g, n_t),
        in_specs=[pl.BlockSpec(blk_in, lambda g, j: (j, g, 0)),
                  pl.BlockSpec(blk_in, lambda g, j: (n_t - 1 - j, g, 0)),
                  pl.BlockSpec((1, 2, SCAN_B, D), lambda g, j: (g, 0, 0, 0)),
                  _const_spec((2, N_HEADS, HEAD, 2 * HEAD)),
                  _const_spec((2, D)),
                  _const_spec((2, D)),
                  _const_spec((2, D))],
        out_specs=[pl.BlockSpec(blk, lambda g, j: (g, j, 0)),
                   pl.BlockSpec(blk, lambda g, j: (g, n_t - 1 - j, 0)),
                   pl.BlockSpec((1, 2, SCAN_B, D), lambda g, j: (g, 0, 0, 0))],
        out_shape=[jax.ShapeDtypeStruct((n_b, seq_len, D), BF16),
                   jax.ShapeDtypeStruct((n_b, seq_len, D), BF16),
                   jax.ShapeDtypeStruct((n_g, 2, SCAN_B, D), F32)],
        scratch_shapes=[pltpu.VMEM((2, N_HEADS, SCAN_T * SCAN_B, HEAD), F32),
                        pltpu.VMEM((2, N_HEADS, SCAN_T * SCAN_B, HEAD), F32),
                        pltpu.VMEM((2, N_HEADS, SCAN_T * SCAN_B, HEAD), F32),
                        pltpu.VMEM((2, N_HEADS, SCAN_B, HEAD), F32)],
        compiler_params=_params(("parallel", "arbitrary")),
        name="rglru_scan",
    )(xc, xc, h0, wg, ba, bx, lam)


def _pack_halves(v):
    half = v.shape[1] // 2
    lo = lax.bitcast_convert_type(v[:, :half].astype(BF16).astype(F32), jnp.uint32)
    hi = lax.bitcast_convert_type(v[:, half:].astype(BF16).astype(F32), jnp.uint32)
    return lax.bitcast_convert_type((lo >> 16) | (hi & jnp.uint32(0xFFFF0000)), jnp.int32)


def _unpack_halves(p):
    u = lax.bitcast_convert_type(p, jnp.uint32)
    lo = lax.bitcast_convert_type(u << 16, F32)
    hi = lax.bitcast_convert_type(u & jnp.uint32(0xFFFF0000), F32)
    return lo, hi


def _mix_body(x_ref, mod_ref, hf_ref, hb_ref, g1_ref, g2_ref, win_ref, cw_ref,
              wco_ref, wro_ref, wo_ref, wr_ref, br_ref, *rest, row_len, n_cast):
    cast_in = rest[:n_cast]
    x1_ref, h2_ref, route_ref, wts_ref, cnt_ref = rest[n_cast:n_cast + 5]
    cast_out = rest[n_cast + 5:2 * n_cast + 5]
    seen = rest[2 * n_cast + 5]
    x = x_ref[...]
    tm = x.shape[0]
    h = _norm_mod(x, g1_ref[...], mod_ref[0, 1:2, :], mod_ref[0, 0:1, :]).astype(BF16)

    def proj(k):
        return _dot(h, win_ref[:, k * D:(k + 1) * D])

    cv = proj(1) * proj(2)
    pos = lax.broadcasted_iota(jnp.int32, (tm, 1), 0) % row_len
    conv = cv * cw_ref[1:2, :]
    conv = conv + jnp.where(pos >= 1, pltpu.roll(cv, 1, 0), 0.0) * cw_ref[0:1, :]
    conv = conv + jnp.where(pos <= row_len - 2, pltpu.roll(cv, tm - 1, 0), 0.0) * cw_ref[2:3, :]
    y_a = _dot((proj(0) * conv).astype(BF16), wco_ref[...])
    merged = _sigmoid(proj(5)) * y_a

    hs = hf_ref[...].astype(F32) + hb_ref[...].astype(F32)
    y_b = _dot((hs * jax.nn.gelu(proj(4))).astype(BF16), wro_ref[...])
    merged = merged + _sigmoid(proj(6)) * y_b

    mix = _dot(merged.astype(BF16), wo_ref[...])
    x1 = x + mod_ref[0, 2:3, :] * mix
    x1_ref[...] = x1.astype(BF16)
    for src, dst in zip(cast_in, cast_out):
        dst[...] = src[...].astype(BF16)
    h2 = _norm_mod(x1, g2_ref[...], mod_ref[0, 4:5, :], mod_ref[0, 3:4, :])
    h2_ref[...] = _pack_halves(h2)

    h2_hi = h2.astype(BF16)
    h2_lo = (h2 - h2_hi.astype(F32)).astype(BF16)
    big = _dot(h2_hi, wr_ref[...])
    logits = (big[:, :ROUTE_PAD] + big[:, ROUTE_PAD:]
              + _dot(h2_lo, wr_ref[:, :ROUTE_PAD]) + br_ref[...])
    lt = logits.T
    row = lax.broadcasted_iota(jnp.int32, (EPG, tm), 0)
    lg = lt[0:EPG]
    mg = jnp.max(lg, axis=0, keepdims=True)
    p_grp = 1.0 / jnp.sum(jnp.exp(lg - mg), axis=0, keepdims=True)
    grp = jnp.min(jnp.where(lg == mg, row, EPG), axis=0, keepdims=True)
    le = lt[EPG * N_GROUPS:EPG * (N_GROUPS + 1)]
    for g in range(N_GROUPS - 2, -1, -1):
        le = jnp.where(grp == g, lt[EPG * (g + 1):EPG * (g + 2)], le)
    me = jnp.max(le, axis=0, keepdims=True)
    ee = jnp.exp(le - me)
    pe = ee / jnp.sum(ee, axis=0, keepdims=True)
    p1 = jnp.max(pe, axis=0, keepdims=True)
    i1 = jnp.min(jnp.where(pe == p1, row, EPG), axis=0, keepdims=True)
    pe2 = jnp.where(row == i1, -1.0, pe)
    p2 = jnp.max(pe2, axis=0, keepdims=True)
    i2 = jnp.min(jnp.where(pe2 == p2, row, EPG), axis=0, keepdims=True)
    den = p1 + p2
    e1 = grp * EPG + i1
    e2 = grp * EPG + i2

    @pl.when(pl.program_id(0) == 0)
    def _():
        seen[...] = jnp.zeros_like(seen)

    erow = lax.broadcasted_iota(jnp.int32, (N_EXPERTS, tm), 0)
    hit1 = erow == e1
    hit2 = erow == e2
    both = jnp.where(jnp.logical_or(hit1, hit2), 1.0, 0.0)
    tri = jnp.where(lax.broadcasted_iota(jnp.int32, (tm, tm), 0)
                    <= lax.broadcasted_iota(jnp.int32, (tm, tm), 1), 1.0, 0.0).astype(BF16)
    before = _dot(both.astype(BF16), tri) - both + seen[...]
    r1 = jnp.sum(jnp.where(hit1, before, 0.0), axis=0, keepdims=True).astype(jnp.int32)
    r2 = jnp.sum(jnp.where(hit2, before, 0.0), axis=0, keepdims=True).astype(jnp.int32)
    total = seen[...] + jnp.sum(both, axis=1, keepdims=True)
    seen[...] = total
    cnt_ref[...] = total[:, :V7X_LANES].astype(jnp.int32)

    route_ref[...] = jnp.where(row == 0, e1, jnp.where(row == 1, e2, jnp.where(
        row == 2, r1, jnp.where(row == 3, r2, 0))))
    w8 = jnp.where(row == 0, p_grp * p1 / den, jnp.where(row == 1, p_grp * p2 / den, 0.0))
    wts_ref[...] = jnp.concatenate([w8, jnp.zeros((V7X_LANES - EPG, tm), F32)], axis=0).T


def _mixer(x, mod, tiles_per_mod, hf, hb, g1, g2, w_in, cw, wco, wro, wo, wr, br, row_len, cast_ws):
    n = x.shape[0]
    tm = MIX_TM
    assert tm % row_len == 0 and n % tm == 0
    steps = n // tm
    assert N_EXPERTS % steps == 0
    epb = N_EXPERTS // steps
    cast_specs = [pl.BlockSpec((epb,) + w.shape[1:], lambda i: (i, 0, 0)) for w in cast_ws]
    cast_shapes = [jax.ShapeDtypeStruct(w.shape, BF16) for w in cast_ws]
    mod_map = lambda i: (i // tiles_per_mod, 0, 0)
    tok = lambda i: (i, 0)
    col = lambda i: (0, i)
    return pl.pallas_call(
        functools.partial(_mix_body, row_len=row_len, n_cast=len(cast_ws)),
        grid=(steps,),
        in_specs=[pl.BlockSpec((tm, D), tok),
                  pl.BlockSpec((1, 6, D), mod_map),
                  pl.BlockSpec((tm, D), tok),
                  pl.BlockSpec((tm, D), tok),
                  _const_spec((1, D)),
                  _const_spec((1, D)),
                  _const_spec(w_in.shape),
                  _const_spec((3, D)),
                  _const_spec((D, D)),
                  _const_spec((D, D)),
                  _const_spec((D, D)),
                  _const_spec((D, 2 * ROUTE_PAD)),
                  _const_spec((1, ROUTE_PAD))] + cast_specs,
        out_specs=[pl.BlockSpec((tm, D), tok),
                   pl.BlockSpec((tm, D // 2), tok),
                   pl.BlockSpec((EPG, tm), col),
                   pl.BlockSpec((tm, V7X_LANES), tok),
                   pl.BlockSpec((N_EXPERTS, V7X_LANES), lambda i: (0, 0))] + cast_specs,
        out_shape=[jax.ShapeDtypeStruct((n, D), BF16),
                   jax.ShapeDtypeStruct((n, D // 2), jnp.int32),
                   jax.ShapeDtypeStruct((EPG, n), jnp.int32),
                   jax.ShapeDtypeStruct((n, V7X_LANES), F32),
                   jax.ShapeDtypeStruct((N_EXPERTS, V7X_LANES), jnp.int32)] + cast_shapes,
        scratch_shapes=[pltpu.VMEM((N_EXPERTS, tm), F32)],
        compiler_params=_params(("arbitrary",)),
        name="mixer",
    )(x, mod, hf, hb, g1, g2, w_in, cw, wco, wro, wo, wr, br, *cast_ws)


def _sc_mesh():
    return plsc.VectorSubcoreMesh(core_axis_name="c", subcore_axis_name="s",
                                  num_cores=V7X_SC_CORES, num_subcores=V7X_SC_SUBCORES)


def _sc_worker_id():
    return lax.axis_index("s") * V7X_SC_CORES + lax.axis_index("c")


def _sc_dispatch(rows, dest, n_slots):
    n, width = rows.shape
    per_w = n // V7X_SC_WORKERS
    n_ch = per_w // SC_WINDOW
    assert n_ch * SC_WINDOW * V7X_SC_WORKERS == n
    idx = dest.reshape(2, V7X_SC_WORKERS, n_ch, SC_WINDOW).transpose(1, 0, 2, 3)

    def body(x_hbm, d_hbm, o_hbm, idx_v, buf, ld_sem, st_sem):
        wid = _sc_worker_id()
        pltpu.sync_copy(d_hbm.at[wid], idx_v)

        def load(j):
            src = x_hbm.at[pl.ds(wid * per_w + j * SC_WINDOW, SC_WINDOW)]
            return pltpu.async_copy(src, buf.at[j % SC_BUFS], ld_sem.at[j % SC_BUFS])

        def scatter(j):
            return [pltpu.async_copy(buf.at[j % SC_BUFS], o_hbm.at[idx_v.at[k, j]], st_sem.at[j % SC_BUFS])
                    for k in range(2)]

        loads = {j: load(j) for j in range(min(SC_BUFS - 1, n_ch))}
        stores, waited = {}, set()
        for j in range(n_ch):
            loads[j].wait()
            nxt = j + SC_BUFS - 1
            if nxt < n_ch:
                if nxt - SC_BUFS >= 0:
                    for cp in stores[nxt - SC_BUFS]:
                        cp.wait()
                    waited.add(nxt - SC_BUFS)
                loads[nxt] = load(nxt)
            stores[j] = scatter(j)
        for j in range(n_ch):
            if j not in waited:
                for cp in stores[j]:
                    cp.wait()

    return pl.kernel(
        body,
        out_type=jax.ShapeDtypeStruct((n_slots, width), jnp.int32),
        mesh=_sc_mesh(),
        scratch_types=[pltpu.VMEM((2, n_ch, SC_WINDOW), jnp.int32),
                       pltpu.VMEM((SC_BUFS, SC_WINDOW, width), jnp.int32),
                       pltpu.SemaphoreType.DMA((SC_BUFS,)),
                       pltpu.SemaphoreType.DMA((SC_BUFS,))],
        name="sc_dispatch",
    )(rows, idx)


def _sc_collect(rows, dest):
    n = dest.shape[1]
    width = rows.shape[1]
    per_w = n // V7X_SC_WORKERS
    n_ch = per_w // SC_WINDOW
    assert n_ch * SC_WINDOW * V7X_SC_WORKERS == n
    idx = dest.reshape(2, V7X_SC_WORKERS, n_ch, SC_WINDOW).transpose(1, 0, 2, 3)
    windows = [(k, j) for k in range(2) for j in range(n_ch)]

    def body(y_hbm, d_hbm, o_hbm, idx_v, buf, ld_sem, st_sem):
        wid = _sc_worker_id()
        pltpu.sync_copy(d_hbm.at[wid], idx_v)

        def gather(c):
            k, j = windows[c]
            return pltpu.async_copy(y_hbm.at[idx_v.at[k, j]], buf.at[c % SC_BUFS], ld_sem.at[c % SC_BUFS])

        def store(c):
            k, j = windows[c]
            dst = o_hbm.at[pl.ds(k * n + wid * per_w + j * SC_WINDOW, SC_WINDOW)]
            return pltpu.async_copy(buf.at[c % SC_BUFS], dst, st_sem.at[c % SC_BUFS])

        n_win = len(windows)
        loads = {c: gather(c) for c in range(min(SC_BUFS - 1, n_win))}
        stores, waited = {}, set()
        for c in range(n_win):
            loads[c].wait()
            nxt = c + SC_BUFS - 1
            if nxt < n_win:
                if nxt - SC_BUFS >= 0:
                    stores[nxt - SC_BUFS].wait()
                    waited.add(nxt - SC_BUFS)
                loads[nxt] = gather(nxt)
            stores[c] = store(c)
        for c in range(n_win):
            if c not in waited:
                stores[c].wait()

    return pl.kernel(
        body,
        out_type=jax.ShapeDtypeStruct((2 * n, width), jnp.int32),
        mesh=_sc_mesh(),
        scratch_types=[pltpu.VMEM((2, n_ch, SC_WINDOW), jnp.int32),
                       pltpu.VMEM((SC_BUFS, SC_WINDOW, width), jnp.int32),
                       pltpu.SemaphoreType.DMA((SC_BUFS,)),
                       pltpu.SemaphoreType.DMA((SC_BUFS,))],
        name="sc_collect",
    )(rows, idx)


def _expert_body(plan_ref, xs_hbm, w1_hbm, w3_hbm, w2_hbm, o_ref,
                 xbuf, w1s, w3s, w2s, xsem, sem):
    i = pl.program_id(0)
    n_used = plan_ref[PLAN_N_USED * V7X_LANES]
    n_runs = plan_ref[PLAN_N_RUNS * V7X_LANES]
    tme = xbuf.shape[1]
    run = plan_ref[PLAN_TILE_RUN * V7X_LANES + i]
    prev_run = plan_ref[PLAN_TILE_RUN * V7X_LANES + jnp.maximum(i - 1, 0)]
    first = jnp.logical_or(i == 0, run != prev_run)
    slot = run % EXPERT_W_BUFS

    def row_copy(t):
        s = t % EXPERT_ROW_BUFS
        start = t * tme if isinstance(t, int) else pl.multiple_of(t * tme, tme)
        src = xs_hbm.at[pl.ds(start, tme)]
        return pltpu.make_async_copy(src, xbuf.at[s], xsem.at[s])

    def weight_copies(r):
        e = plan_ref[PLAN_RUN_EXPERT * V7X_LANES + r]
        s = r % EXPERT_W_BUFS
        return (pltpu.make_async_copy(w1_hbm.at[e], w1s.at[s], sem.at[0, s]),
                pltpu.make_async_copy(w3_hbm.at[e], w3s.at[s], sem.at[1, s]),
                pltpu.make_async_copy(w2_hbm.at[e], w2s.at[s], sem.at[2, s]))

    @pl.when(i == 0)
    def _():
        for t in range(EXPERT_ROW_BUFS - 1):
            @pl.when(t < n_used)
            def _():
                row_copy(t).start()
        for r in range(EXPERT_W_BUFS - 1):
            @pl.when(r < n_runs)
            def _():
                for cp in weight_copies(r):
                    cp.start()

    @pl.when(jnp.logical_and(first, i < n_used))
    def _():
        @pl.when(run + (EXPERT_W_BUFS - 1) < n_runs)
        def _():
            for cp in weight_copies(run + (EXPERT_W_BUFS - 1)):
                cp.start()

        for cp in weight_copies(run):
            cp.wait()

    @pl.when(i < n_used)
    def _():
        @pl.when(i + (EXPERT_ROW_BUFS - 1) < n_used)
        def _():
            row_copy(i + (EXPERT_ROW_BUFS - 1)).start()

        row_copy(i).wait()
        lo, hi = _unpack_halves(xbuf[i % EXPERT_ROW_BUFS])
        lo = lo.astype(BF16)
        hi = hi.astype(BF16)
        half = D // 2
        a = _dot(lo, w1s[slot, 0:half, :]) + _dot(hi, w1s[slot, half:D, :])
        b = _dot(lo, w3s[slot, 0:half, :]) + _dot(hi, w3s[slot, half:D, :])
        z = (a * _sigmoid(a)) * b
        o_ref[...] = _pack_halves(_dot(z.astype(BF16), w2s[slot]))


def _experts(plan, xs, w1, w3, w2, tme):
    n_slots = xs.shape[0]
    grid_spec = pltpu.PrefetchScalarGridSpec(
        num_scalar_prefetch=1,
        grid=(n_slots // tme,),
        in_specs=[pl.BlockSpec(memory_space=pl.ANY),
                  pl.BlockSpec(memory_space=pl.ANY),
                  pl.BlockSpec(memory_space=pl.ANY),
                  pl.BlockSpec(memory_space=pl.ANY)],
        out_specs=pl.BlockSpec((tme, D // 2),
                               lambda i, plan: (jnp.clip(
                                   i, 0, jnp.maximum(plan[PLAN_N_USED * V7X_LANES] - 1, 0)), 0)),
        scratch_shapes=[pltpu.VMEM((EXPERT_ROW_BUFS, tme, D // 2), jnp.int32),
                        pltpu.VMEM((EXPERT_W_BUFS, D, D_EXPERT), BF16),
                        pltpu.VMEM((EXPERT_W_BUFS, D, D_EXPERT), BF16),
                        pltpu.VMEM((EXPERT_W_BUFS, D_EXPERT, D), BF16),
                        pltpu.SemaphoreType.DMA((EXPERT_ROW_BUFS,)),
                        pltpu.SemaphoreType.DMA((3, EXPERT_W_BUFS))],
    )
    return pl.pallas_call(
        _expert_body,
        grid_spec=grid_spec,
        out_shape=jax.ShapeDtypeStruct((n_slots, D // 2), jnp.int32),
        compiler_params=_params(("arbitrary",)),
        name="experts",
    )(plan, xs, w1, w3, w2)


def _final_body(x1_ref, mod_ref, y0_ref, y1_ref, wt_ref, gf_ref, o_ref):
    w0 = wt_ref[:, 0:1]
    w1 = wt_ref[:, 1:2]
    lo0, hi0 = _unpack_halves(y0_ref[...])
    lo1, hi1 = _unpack_halves(y1_ref[...])
    moe = jnp.concatenate([w0 * lo0 + w1 * lo1, w0 * hi0 + w1 * hi1], axis=1)
    x2 = x1_ref[...].astype(F32) + mod_ref[0, 5:6, :] * moe
    ms = jnp.mean(x2 * x2, axis=-1, keepdims=True)
    o_ref[...] = x2 * lax.rsqrt(ms + EPS) * gf_ref[...]


def _final(x1, mod, tiles_per_mod, yg, wts, g_final):
    n = x1.shape[0]
    tm = FIN_TM
    nt = n // tm
    return pl.pallas_call(
        _final_body,
        grid=(nt,),
        in_specs=[pl.BlockSpec((tm, D), lambda i: (i, 0)),
                  pl.BlockSpec((1, 6, D), lambda i: (i // tiles_per_mod, 0, 0)),
                  pl.BlockSpec((tm, D // 2), lambda i: (i, 0)),
                  pl.BlockSpec((tm, D // 2), lambda i: (i + nt, 0)),
                  pl.BlockSpec((tm, V7X_LANES), lambda i: (i, 0)),
                  _const_spec((1, D))],
        out_specs=pl.BlockSpec((tm, D), lambda i: (i, 0)),
        out_shape=jax.ShapeDtypeStruct((n, D), F32),
        compiler_params=_params(("parallel",)),
        name="final",
    )(x1, mod, yg, yg, wts, g_final)


def _plan_body(route_ref, cnt_ref, dest_ref, meta_ref, *, tme, n_slots):
    shift = tme.bit_length() - 1
    c = cnt_ref[...]
    erow = lax.broadcasted_iota(jnp.int32, c.shape, 0)

    def running_sum(v):
        for sh in (1, 2, 4, 8, 16):
            v = v + jnp.where(erow >= sh, pltpu.roll(v, sh, 0), 0)
        return v

    padded = lax.shift_left(lax.shift_right_logical(c + (tme - 1), shift), shift)
    pend = running_sum(padded)
    pstart = pend - padded
    run_of = running_sum((c > 0).astype(jnp.int32)) - 1

    acc = route_ref[2:4, :]
    ids = route_ref[0:2, :]
    for e in range(N_EXPERTS):
        acc = acc + jnp.where(ids == e, pstart[e:e + 1, 0:1], 0)
    dest_ref[0:2, :] = jnp.clip(acc, 0, n_slots - 1)
    dest_ref[2:EPG, :] = jnp.zeros((EPG - 2, acc.shape[1]), jnp.int32)

    lane = lax.broadcasted_iota(jnp.int32, (1, V7X_LANES), 1)
    tile_e = jnp.zeros((1, V7X_LANES), jnp.int32)
    for e in range(N_EXPERTS):
        tile_e = tile_e + (lane * tme >= pend[e:e + 1, 0:1]).astype(jnp.int32)
    tile_e = jnp.minimum(tile_e, N_EXPERTS - 1)
    tile_run = jnp.zeros((1, V7X_LANES), jnp.int32)
    run_e = jnp.zeros((1, V7X_LANES), jnp.int32)
    for e in range(N_EXPERTS):
        tile_run = tile_run + jnp.where(tile_e == e, run_of[e:e + 1, 0:1], 0)
        is_run = jnp.logical_and(lane == run_of[e:e + 1, 0:1], c[e:e + 1, 0:1] > 0)
        run_e = run_e + jnp.where(is_run, e, 0)
    n_runs = run_of[N_EXPERTS - 1:N_EXPERTS, :] + 1
    n_used = lax.shift_right_logical(pend[N_EXPERTS - 1:N_EXPERTS, :], shift)
    mrow = lax.broadcasted_iota(jnp.int32, (EPG, V7X_LANES), 0)
    meta_ref[...] = jnp.where(mrow == PLAN_TILE_RUN, tile_run, jnp.where(
        mrow == PLAN_RUN_EXPERT, run_e, jnp.where(
            mrow == PLAN_N_RUNS, n_runs, jnp.where(mrow == PLAN_N_USED, n_used, 0))))


def _slot_plan(route, cnt, n, tme):
    n_slots = ((2 * n + N_EXPERTS * (tme - 1)) // tme) * tme
    assert tme & (tme - 1) == 0 and n_slots // tme <= V7X_LANES
    dest, meta = pl.pallas_call(
        functools.partial(_plan_body, tme=tme, n_slots=n_slots),
        out_shape=[jax.ShapeDtypeStruct((EPG, n), jnp.int32),
                   jax.ShapeDtypeStruct((EPG, V7X_LANES), jnp.int32)],
        name="plan",
    )(route, cnt)
    return dest[0:2], meta.reshape(-1), n_slots


def _group(x, mod, mod_per_seq, h0, p, row_len, tme, cast_ws):
    n_b, seq_len, _ = x.shape
    n = n_b * seq_len
    xt = x.reshape(n, D)
    mod_seq = mod if mod_per_seq else jnp.broadcast_to(mod, (n_b, 6, D))
    xc = _xr_conv(x, mod_seq, p["g1"], p["w_in"], p["rnn_conv_w"], p["rnn_conv_b"])
    hf, hb, last = _rglru_scan(xc, h0, p["wg"], p["ba"], p["bx"], p["lam"])
    tiles_per_mod = (seq_len // MIX_TM) if mod_per_seq else (n // MIX_TM)
    outs = _mixer(xt, mod, tiles_per_mod, hf.reshape(n, D), hb.reshape(n, D),
                                     p["g1"], p["g2"], p["w_in"], p["conv_w"], p["wco"], p["wro"],
                                     p["wo"], p["wr"], p["br"], row_len, cast_ws)
    x1, h2, route, wts, cnt = outs[:5]
    casts = outs[5:]
    dest, plan, n_slots = _slot_plan(route, cnt, n, tme)
    xs = _sc_dispatch(h2, dest, n_slots)

    def finish(w1b, w3b, w2b):
        ys = _experts(plan, xs, w1b, w3b, w2b, tme)
        yg = _sc_collect(ys, dest)
        tiles_per_mod_f = (seq_len // FIN_TM) if mod_per_seq else (n // FIN_TM)
        y = _final(x1, mod, tiles_per_mod_f, yg, wts, p["g_final"])
        return y.reshape(n_b, seq_len, D)

    return finish, last, casts


def kernel(x_prompt, x_sample, state_rnn, c, c_ctx, w_ada, b_ada, g_norm1, g_norm2, w_in, conv_w, w_conv_out, rnn_conv_w, rnn_conv_b, w_gate_a, b_gate_a, w_gate_x, b_gate_x, lam, w_rnn_out, w_o, w_router_group, b_router_group, w_router_expert, b_router_expert, w1, w3, w2, g_final):
    assert w_ada.shape[0] == 1, "single layer"
    n_pb, n_sb = x_prompt.shape[0], x_sample.shape[0]

    cond = jnp.concatenate([c_ctx[None, :], c, jnp.zeros((16 - 1 - n_sb, D), F32)], axis=0)
    mod = _ada(cond, w_ada[0], b_ada[0]).reshape(16, 6, D)

    w_in_b = w_in[0].astype(BF16)
    pad_w = jnp.zeros((D, EPG - N_GROUPS), F32)
    wr = jnp.concatenate([w_router_group[0], pad_w, w_router_expert[0],
                          jnp.zeros((D, ROUTE_PAD - EPG - N_EXPERTS), F32)], axis=1)
    br = jnp.concatenate([b_router_group[0], jnp.full((EPG - N_GROUPS,), NEG_BIG, F32),
                          b_router_expert[0],
                          jnp.zeros((ROUTE_PAD - EPG - N_EXPERTS,), F32)]).reshape(1, ROUTE_PAD)
    wr_hi = wr.astype(BF16)
    p = dict(
        g1=g_norm1, g2=g_norm2, w_in=w_in_b,
        conv_w=conv_w[0], rnn_conv_w=rnn_conv_w[0], rnn_conv_b=rnn_conv_b,
        wg=(0.5 * jnp.concatenate([w_gate_a[0], w_gate_x[0]], axis=-1)).astype(BF16),
        ba=b_gate_a[0], bx=b_gate_x[0], lam=lam[0],
        wco=w_conv_out[0].astype(BF16), wro=w_rnn_out[0].astype(BF16), wo=w_o[0].astype(BF16),
        wr=jnp.concatenate([wr_hi, (wr - wr_hi.astype(F32)).astype(BF16)], axis=1), br=br,
        g_final=g_final.reshape(1, D),
    )

    h0_p = jnp.zeros((n_pb // SCAN_B, 2, SCAN_B, D), F32)
    finish_p, last, (w2b,) = _group(x_prompt, mod[0:1], False, h0_p, p, x_prompt.shape[1], 512,
                                    [w2[0]])
    state_new = last.transpose(0, 2, 1, 3).reshape(n_pb, 1, 2, D)

    h0_s = state_rnn[:, 0].reshape(n_sb // SCAN_B, SCAN_B, 2, D).transpose(0, 2, 1, 3)
    finish_s, _, (w1b, w3b) = _group(x_sample, mod[1:1 + n_sb], True, h0_s, p, GRID_W, 512,
                                     [w1[0], w3[0]])
    return (finish_p(w1b, w3b, w2b), finish_s(w1b, w3b, w2b), state_new)
```

```python
import functools

import jax
import jax.numpy as jnp
from jax import lax
from jax.experimental import pallas as pl
from jax.experimental.pallas import tpu as pltpu
from jax.experimental.pallas import tpu_sc as plsc

D = 1024
N_HEADS = 8
HEAD = D // N_HEADS
GRID_W = 64
RG_C = 8.0
N_GROUPS = 4
EPG = 8
N_EXPERTS = N_GROUPS * EPG
D_EXPERT = 512
EPS = 1e-6
F32 = jnp.float32
BF16 = jnp.bfloat16

V7X_LANES = 128
V7X_SUBLANES = 8
V7X_VMEM_LIMIT_BYTES = 56 * 1024 * 1024
V7X_SC_CORES = 2
V7X_SC_SUBCORES = 16
V7X_SC_WORKERS = V7X_SC_CORES * V7X_SC_SUBCORES
SC_WINDOW = 64
SC_BUFS = 3

XR_T = 256
XR_SUB = 32
XR_MB = 4
XR_LEFT = 2
XR_COL = 3
SCAN_T = 128
SCAN_B = V7X_SUBLANES
LOG2_E = 1.4426950408889634
TINY = 1e-30
MIX_TM = 512
FIN_TM = 1024
EXPERT_ROW_BUFS = 3
EXPERT_W_BUFS = 3
EXPERT_W_DMA_PRIORITY = 1
PLAN_TILE_RUN, PLAN_RUN_EXPERT, PLAN_N_RUNS, PLAN_N_USED = 0, 1, 2, 3
ROUTE_PAD = 128
NEG_BIG = -1e30


def _sigmoid(x):
    return 0.5 * jnp.tanh(0.5 * x) + 0.5


def _norm_mod(x, g, scale, shift):
    ms = jnp.mean(x * x, axis=-1, keepdims=True)
    return (x * lax.rsqrt(ms + EPS)) * (g * (1.0 + scale)) + shift


def _dot(a, b):
    return jnp.dot(a, b, preferred_element_type=F32)


def _params(sem, vmem=V7X_VMEM_LIMIT_BYTES):
    return pltpu.CompilerParams(dimension_semantics=sem, vmem_limit_bytes=vmem)


def _const_spec(shape):
    zeros = (0,) * len(shape)
    return pl.BlockSpec(shape, lambda *_: zeros, pipeline_mode=pl.Buffered(1))


def _ada_body(c_ref, w_ref, b_ref, o_ref):
    c = c_ref[...]
    s = (c * _sigmoid(c)).astype(BF16)
    o_ref[...] = _dot(s, w_ref[...].astype(BF16)) + b_ref[...]


def _ada(cond, w_ada, b_ada):
    rows = cond.shape[0]
    n_out = w_ada.shape[1]
    return pl.pallas_call(
        _ada_body,
        grid=(n_out // D,),
        in_specs=[pl.BlockSpec((rows, D), lambda i: (0, 0)),
                  pl.BlockSpec((D, D), lambda i: (0, i)),
                  pl.BlockSpec((1, D), lambda i: (0, i))],
        out_specs=pl.BlockSpec((rows, D), lambda i: (0, i)),
        out_shape=jax.ShapeDtypeStruct((rows, n_out), F32),
        compiler_params=_params(("parallel",)),
        name="ada",
    )(cond, w_ada, b_ada.reshape(1, n_out))


def _xr_body(x_ref, xn_ref, mod_ref, g_ref, w_ref, cw_ref, cb_ref, *rest):
    o_ref, xt = rest[-2:]
    j = pl.program_id(1)
    n_t = pl.num_programs(1)
    t_len = x_ref.shape[1]
    nb = SCAN_B
    body0 = XR_LEFT * nb

    @pl.when(j == 0)
    def _():
        xt[:, 0:body0, :] = jnp.zeros((N_HEADS, body0, HEAD), F32)

    @pl.when(j > 0)
    def _():
        xt[:, 0:body0, :] = xt[:, t_len * nb:t_len * nb + body0, :]

    g = g_ref[...]
    for b0 in range(0, nb, XR_MB):
        h = jnp.concatenate(
            [_norm_mod(x_ref[b], g, mod_ref[b, 1:2, :], mod_ref[b, 0:1, :]).astype(BF16)
             for b in range(b0, b0 + XR_MB)], axis=0)
        r = _dot(h, w_ref[...])
        for i in range(XR_MB):
            for s in range(N_HEADS):
                xt[s, pl.ds(body0 + b0 + i, t_len, stride=nb), :] = (
                    r[i * t_len:(i + 1) * t_len, s * HEAD:(s + 1) * HEAD])

    hn = _norm_mod(xn_ref[:, 0, :], g, mod_ref[:, 1, :], mod_ref[:, 0, :]).astype(BF16)
    rn = jnp.where(j < n_t - 1, _dot(hn, w_ref[...]), 0.0)
    tail = body0 + t_len * nb
    for s in range(N_HEADS):
        xt[s, tail:tail + nb, :] = rn[:, s * HEAD:(s + 1) * HEAD]

    sub = XR_SUB
    for s in range(N_HEADS):
        sl = slice(s * HEAD, (s + 1) * HEAD)
        for t0 in range(0, t_len, sub):
            y = cb_ref[:, sl]
            for k in range(4):
                r0 = (t0 + k) * nb
                y = y + xt[s, r0:r0 + sub * nb, :] * cw_ref[k:k + 1, sl]
            o_ref[t0:t0 + sub, :, sl] = y.reshape(sub, nb, HEAD)


def _xr_conv(x, mod_seq, g1, w_in, cw, cb, after=None):
    n_b, seq_len, _ = x.shape
    t_len = min(XR_T, seq_len)
    assert n_b % SCAN_B == 0 and seq_len % t_len == 0 and t_len % XR_SUB == 0
    n_t = seq_len // t_len
    nxt = t_len // V7X_SUBLANES
    last_blk = seq_len // V7X_SUBLANES - 1
    return pl.pallas_call(
        _xr_body,
        grid=(n_b // SCAN_B, n_t),
        in_specs=[pl.BlockSpec((SCAN_B, t_len, D), lambda g, j: (g, j, 0)),
                  pl.BlockSpec((SCAN_B, V7X_SUBLANES, D),
                               lambda g, j: (g, jnp.minimum((j + 1) * nxt, last_blk), 0)),
                  pl.BlockSpec((SCAN_B, 6, D), lambda g, j: (g, 0, 0)),
                  _const_spec((1, D)),
                  pl.BlockSpec((D, D), lambda g, j: (0, XR_COL), pipeline_mode=pl.Buffered(1)),
                  _const_spec((4, D)),
                  _const_spec((1, D))] + ([] if after is None else [_const_spec(after.shape)]),
        out_specs=pl.BlockSpec((t_len, SCAN_B, D), lambda g, j: (j, g, 0)),
        out_shape=jax.ShapeDtypeStruct((seq_len, n_b, D), F32),
        scratch_shapes=[pltpu.VMEM((N_HEADS, (t_len + XR_LEFT + 1) * SCAN_B, HEAD), F32)],
        compiler_params=_params(("parallel", "arbitrary")),
        name="xr_conv",
    )(x, x, mod_seq, g1, w_in, cw, cb, *(() if after is None else (after,)))


def _scan_body(xf_ref, xb_ref, h0_ref, wg_ref, ba_ref, bx_ref, lam_ref,
               hf_ref, hb_ref, last_ref, a_s, u_s, h_s, hc):
    j = pl.program_id(1)
    n_t = pl.num_programs(1)
    t_len = xf_ref.shape[0]
    rows = SCAN_B * t_len

    @pl.when(j == 0)
    def _():
        for d in range(2):
            for s in range(N_HEADS):
                hc[d, s] = h0_ref[0, d, :, s * HEAD:(s + 1) * HEAD]

    c2s = []
    for d in range(2):
        z = -lam_ref[d:d + 1, :]
        sp = jnp.maximum(z, 0.0) + jnp.log(1.0 + jnp.exp(-jnp.abs(z)))
        c2s.append((-0.5 * RG_C * LOG2_E) * sp)

    for hd in range(N_HEADS):
        sl = slice(hd * HEAD, (hd + 1) * HEAD)
        for d, x_ref in ((0, xf_ref), (1, xb_ref)):
            xh = x_ref[:, :, sl].reshape(rows, HEAD)
            g = _dot(xh.astype(BF16), wg_ref[d, hd])
            t_r = jnp.tanh(g[:, :HEAD] + 0.5 * ba_ref[d:d + 1, sl])
            t_i = jnp.tanh(g[:, HEAD:] + 0.5 * bx_ref[d:d + 1, sl])
            a = jnp.exp2(c2s[d][:, sl] * t_r + c2s[d][:, sl])
            q = 0.25 - 0.25 * (a * a)
            half_mult = q * lax.rsqrt(jnp.maximum(q, TINY))
            u = half_mult * ((t_i + 1.0) * xh)
            a_s[d, hd] = a
            u_s[d, hd] = u
        h_f = hc[0, hd]
        h_b = hc[1, hd]
        for t in range(t_len):
            rf = t * SCAN_B
            rb = (t_len - 1 - t) * SCAN_B
            h_f = a_s[0, hd, rf:rf + SCAN_B, :] * h_f + u_s[0, hd, rf:rf + SCAN_B, :]
            h_b = a_s[1, hd, rb:rb + SCAN_B, :] * h_b + u_s[1, hd, rb:rb + SCAN_B, :]
            h_s[0, hd, rf:rf + SCAN_B, :] = h_f
            h_s[1, hd, rb:rb + SCAN_B, :] = h_b
        hc[0, hd] = h_f
        hc[1, hd] = h_b

    for d, o_ref in ((0, hf_ref), (1, hb_ref)):
        for hd in range(N_HEADS):
            for b in range(SCAN_B):
                o_ref[b, :, hd * HEAD:(hd + 1) * HEAD] = (
                    h_s[d, hd, pl.ds(b, t_len, stride=SCAN_B), :].astype(BF16))

    @pl.when(j == n_t - 1)
    def _():
        for d in range(2):
            for s in range(N_HEADS):
                last_ref[0, d, :, s * HEAD:(s + 1) * HEAD] = hc[d, s]


def _rglru_scan(xc, h0, wg, ba, bx, lam):
    seq_len, n_b, _ = xc.shape
    n_g = n_b // SCAN_B
    n_t = seq_len // SCAN_T
    blk = (SCAN_B, SCAN_T, D)
    blk_in = (SCAN_T, SCAN_B, D)
    return pl.pallas_call(
        _scan_body,
        grid=(n_g, n_t),
        in_specs=[pl.BlockSpec(blk_in, lambda g, j: (j, g, 0)),
                  pl.BlockSpec(blk_in, lambda g, j: (n_t - 1 - j, g, 0)),
                  pl.BlockSpec((1, 2, SCAN_B, D), lambda g, j: (g, 0, 0, 0)),
                  _const_spec((2, N_HEADS, HEAD, 2 * HEAD)),
                  _const_spec((2, D)),
                  _const_spec((2, D)),
                  _const_spec((2, D))],
        out_specs=[pl.BlockSpec(blk, lambda g, j: (g, j, 0)),
                   pl.BlockSpec(blk, lambda g, j: (g, n_t - 1 - j, 0)),
                   pl.BlockSpec((1, 2, SCAN_B, D), lambda g, j: (g, 0, 0, 0))],
        out_shape=[jax.ShapeDtypeStruct((n_b, seq_len, D), BF16),
                   jax.ShapeDtypeStruct((n_b, seq_len, D), BF16),
                   jax.ShapeDtypeStruct((n_g, 2, SCAN_B, D), F32)],
        scratch_shapes=[pltpu.VMEM((2, N_HEADS, SCAN_T * SCAN_B, HEAD), F32),
                        pltpu.VMEM((2, N_HEADS, SCAN_T * SCAN_B, HEAD), F32),
                        pltpu.VMEM((2, N_HEADS, SCAN_T * SCAN_B, HEAD), F32),
                        pltpu.VMEM((2, N_HEADS, SCAN_B, HEAD), F32)],
        compiler_params=_params(("parallel", "arbitrary")),
        name="rglru_scan",
    )(xc, xc, h0, wg, ba, bx, lam)


def _pack_halves(v):
    half = v.shape[1] // 2
    lo = lax.bitcast_convert_type(v[:, :half].astype(BF16).astype(F32), jnp.uint32)
    hi = lax.bitcast_convert_type(v[:, half:].astype(BF16).astype(F32), jnp.uint32)
    return lax.bitcast_convert_type((lo >> 16) | (hi & jnp.uint32(0xFFFF0000)), jnp.int32)


def _unpack_halves(p):
    u = lax.bitcast_convert_type(p, jnp.uint32)
    lo = lax.bitcast_convert_type(u << 16, F32)
    hi = lax.bitcast_convert_type(u & jnp.uint32(0xFFFF0000), F32)
    return lo, hi


def _mix_body(x_ref, mod_ref, hf_ref, hb_ref, g1_ref, g2_ref, win_ref, cw_ref,
              wco_ref, wro_ref, wo_ref, wr_ref, br_ref, *rest, row_len, n_cast):
    cast_in = rest[:n_cast]
    x1_ref, h2_ref, route_ref, wts_ref, cnt_ref = rest[n_cast:n_cast + 5]
    cast_out = rest[n_cast + 5:2 * n_cast + 5]
    seen = rest[2 * n_cast + 5]
    x = x_ref[...]
    tm = x.shape[0]
    h = _norm_mod(x, g1_ref[...], mod_ref[0, 1:2, :], mod_ref[0, 0:1, :]).astype(BF16)

    def proj(k):
        return _dot(h, win_ref[:, k * D:(k + 1) * D])

    cv = proj(1) * proj(2)
    pos = lax.broadcasted_iota(jnp.int32, (tm, 1), 0) % row_len
    conv = cv * cw_ref[1:2, :]
    conv = conv + jnp.where(pos >= 1, pltpu.roll(cv, 1, 0), 0.0) * cw_ref[0:1, :]
    conv = conv + jnp.where(pos <= row_len - 2, pltpu.roll(cv, tm - 1, 0), 0.0) * cw_ref[2:3, :]
    y_a = _dot((proj(0) * conv).astype(BF16), wco_ref[...])
    merged = _sigmoid(proj(5)) * y_a

    hs = hf_ref[...].astype(F32) + hb_ref[...].astype(F32)
    y_b = _dot((hs * jax.nn.gelu(proj(4))).astype(BF16), wro_ref[...])
    merged = merged + _sigmoid(proj(6)) * y_b

    mix = _dot(merged.astype(BF16), wo_ref[...])
    x1 = x + mod_ref[0, 2:3, :] * mix
    x1_ref[...] = x1.astype(BF16)
    for src, dst in zip(cast_in, cast_out):
        dst[...] = src[...].astype(BF16)
    h2 = _norm_mod(x1, g2_ref[...], mod_ref[0, 4:5, :], mod_ref[0, 3:4, :])
    h2_ref[...] = _pack_halves(h2)

    h2_hi = h2.astype(BF16)
    h2_lo = (h2 - h2_hi.astype(F32)).astype(BF16)
    big = _dot(h2_hi, wr_ref[...])
    logits = (big[:, :ROUTE_PAD] + big[:, ROUTE_PAD:]
              + _dot(h2_lo, wr_ref[:, :ROUTE_PAD]) + br_ref[...])
    lt = logits.T
    row = lax.broadcasted_iota(jnp.int32, (EPG, tm), 0)
    lg = lt[0:EPG]
    mg = jnp.max(lg, axis=0, keepdims=True)
    p_grp = 1.0 / jnp.sum(jnp.exp(lg - mg), axis=0, keepdims=True)
    grp = jnp.min(jnp.where(lg == mg, row, EPG), axis=0, keepdims=True)
    le = lt[EPG * N_GROUPS:EPG * (N_GROUPS + 1)]
    for g in range(N_GROUPS - 2, -1, -1):
        le = jnp.where(grp == g, lt[EPG * (g + 1):EPG * (g + 2)], le)
    me = jnp.max(le, axis=0, keepdims=True)
    ee = jnp.exp(le - me)
    pe = ee / jnp.sum(ee, axis=0, keepdims=True)
    p1 = jnp.max(pe, axis=0, keepdims=True)
    i1 = jnp.min(jnp.where(pe == p1, row, EPG), axis=0, keepdims=True)
    pe2 = jnp.where(row == i1, -1.0, pe)
    p2 = jnp.max(pe2, axis=0, keepdims=True)
    i2 = jnp.min(jnp.where(pe2 == p2, row, EPG), axis=0, keepdims=True)
    den = p1 + p2
    e1 = grp * EPG + i1
    e2 = grp * EPG + i2

    @pl.when(pl.program_id(0) == 0)
    def _():
        seen[...] = jnp.zeros_like(seen)

    erow = lax.broadcasted_iota(jnp.int32, (N_EXPERTS, tm), 0)
    hit1 = erow == e1
    hit2 = erow == e2
    both = jnp.where(jnp.logical_or(hit1, hit2), 1.0, 0.0)
    tri = jnp.where(lax.broadcasted_iota(jnp.int32, (tm, tm), 0)
                    <= lax.broadcasted_iota(jnp.int32, (tm, tm), 1), 1.0, 0.0).astype(BF16)
    before = _dot(both.astype(BF16), tri) - both + seen[...]
    r1 = jnp.sum(jnp.where(hit1, before, 0.0), axis=0, keepdims=True).astype(jnp.int32)
    r2 = jnp.sum(jnp.where(hit2, before, 0.0), axis=0, keepdims=True).astype(jnp.int32)
    total = seen[...] + jnp.sum(both, axis=1, keepdims=True)
    seen[...] = total
    cnt_ref[...] = total[:, :V7X_LANES].astype(jnp.int32)

    route_ref[...] = jnp.where(row == 0, e1, jnp.where(row == 1, e2, jnp.where(
        row == 2, r1, jnp.where(row == 3, r2, 0))))
    w8 = jnp.where(row == 0, p_grp * p1 / den, jnp.where(row == 1, p_grp * p2 / den, 0.0))
    wts_ref[...] = jnp.concatenate([w8, jnp.zeros((V7X_LANES - EPG, tm), F32)], axis=0).T


def _mixer(x, mod, tiles_per_mod, hf, hb, g1, g2, w_in, cw, wco, wro, wo, wr, br, row_len, cast_ws):
    n = x.shape[0]
    tm = MIX_TM
    assert tm % row_len == 0 and n % tm == 0
    steps = n // tm
    assert N_EXPERTS % steps == 0
    epb = N_EXPERTS // steps
    cast_specs = [pl.BlockSpec((epb,) + w.shape[1:], lambda i: (i, 0, 0)) for w in cast_ws]
    cast_shapes = [jax.ShapeDtypeStruct(w.shape, BF16) for w in cast_ws]
    mod_map = lambda i: (i // tiles_per_mod, 0, 0)
    tok = lambda i: (i, 0)
    col = lambda i: (0, i)
    return pl.pallas_call(
        functools.partial(_mix_body, row_len=row_len, n_cast=len(cast_ws)),
        grid=(steps,),
        in_specs=[pl.BlockSpec((tm, D), tok),
                  pl.BlockSpec((1, 6, D), mod_map),
                  pl.BlockSpec((tm, D), tok),
                  pl.BlockSpec((tm, D), tok),
                  _const_spec((1, D)),
                  _const_spec((1, D)),
                  _const_spec(w_in.shape),
                  _const_spec((3, D)),
                  _const_spec((D, D)),
                  _const_spec((D, D)),
                  _const_spec((D, D)),
                  _const_spec((D, 2 * ROUTE_PAD)),
                  _const_spec((1, ROUTE_PAD))] + cast_specs,
        out_specs=[pl.BlockSpec((tm, D), tok),
                   pl.BlockSpec((tm, D // 2), tok),
                   pl.BlockSpec((EPG, tm), col),
                   pl.BlockSpec((tm, V7X_LANES), tok),
                   pl.BlockSpec((N_EXPERTS, V7X_LANES), lambda i: (0, 0))] + cast_specs,
        out_shape=[jax.ShapeDtypeStruct((n, D), BF16),
                   jax.ShapeDtypeStruct((n, D // 2), jnp.int32),
                   jax.ShapeDtypeStruct((EPG, n), jnp.int32),
                   jax.ShapeDtypeStruct((n, V7X_LANES), F32),
                   jax.ShapeDtypeStruct((N_EXPERTS, V7X_LANES), jnp.int32)] + cast_shapes,
        scratch_shapes=[pltpu.VMEM((N_EXPERTS, tm), F32)],
        compiler_params=_params(("arbitrary",)),
        name="mixer",
    )(x, mod, hf, hb, g1, g2, w_in, cw, wco, wro, wo, wr, br, *cast_ws)


def _sc_mesh():
    return plsc.VectorSubcoreMesh(core_axis_name="c", subcore_axis_name="s",
                                  num_cores=V7X_SC_CORES, num_subcores=V7X_SC_SUBCORES)


def _sc_worker_id():
    return lax.axis_index("s") * V7X_SC_CORES + lax.axis_index("c")


def _sc_dispatch(rows, dest, n_slots):
    n, width = rows.shape
    per_w = n // V7X_SC_WORKERS
    n_ch = per_w // SC_WINDOW
    assert n_ch * SC_WINDOW * V7X_SC_WORKERS == n
    idx = dest.reshape(2, V7X_SC_WORKERS, n_ch, SC_WINDOW).transpose(1, 0, 2, 3)

    def body(x_hbm, d_hbm, o_hbm, idx_v, buf, ld_sem, st_sem):
        wid = _sc_worker_id()
        pltpu.sync_copy(d_hbm.at[wid], idx_v)

        def load(j):
            src = x_hbm.at[pl.ds(wid * per_w + j * SC_WINDOW, SC_WINDOW)]
            return pltpu.async_copy(src, buf.at[j % SC_BUFS], ld_sem.at[j % SC_BUFS])

        def scatter(j):
            return [pltpu.async_copy(buf.at[j % SC_BUFS], o_hbm.at[idx_v.at[k, j]], st_sem.at[j % SC_BUFS])
                    for k in range(2)]

        loads = {j: load(j) for j in range(min(SC_BUFS - 1, n_ch))}
        stores, waited = {}, set()
        for j in range(n_ch):
            loads[j].wait()
            nxt = j + SC_BUFS - 1
            if nxt < n_ch:
                if nxt - SC_BUFS >= 0:
                    for cp in stores[nxt - SC_BUFS]:
                        cp.wait()
                    waited.add(nxt - SC_BUFS)
                loads[nxt] = load(nxt)
            stores[j] = scatter(j)
        for j in range(n_ch):
            if j not in waited:
                for cp in stores[j]:
                    cp.wait()

    return pl.kernel(
        body,
        out_type=jax.ShapeDtypeStruct((n_slots, width), jnp.int32),
        mesh=_sc_mesh(),
        scratch_types=[pltpu.VMEM((2, n_ch, SC_WINDOW), jnp.int32),
                       pltpu.VMEM((SC_BUFS, SC_WINDOW, width), jnp.int32),
                       pltpu.SemaphoreType.DMA((SC_BUFS,)),
                       pltpu.SemaphoreType.DMA((SC_BUFS,))],
        name="sc_dispatch",
    )(rows, idx)


def _sc_collect(rows, dest):
    n = dest.shape[1]
    width = rows.shape[1]
    per_w = n // V7X_SC_WORKERS
    n_ch = per_w // SC_WINDOW
    assert n_ch * SC_WINDOW * V7X_SC_WORKERS == n
    idx = dest.reshape(2, V7X_SC_WORKERS, n_ch, SC_WINDOW).transpose(1, 0, 2, 3)
    windows = [(k, j) for k in range(2) for j in range(n_ch)]

    def body(y_hbm, d_hbm, o_hbm, idx_v, buf, ld_sem, st_sem):
        wid = _sc_worker_id()
        pltpu.sync_copy(d_hbm.at[wid], idx_v)

        def gather(c):
            k, j = windows[c]
            return pltpu.async_copy(y_hbm.at[idx_v.at[k, j]], buf.at[c % SC_BUFS], ld_sem.at[c % SC_BUFS])

        def store(c):
            k, j = windows[c]
            dst = o_hbm.at[pl.ds(k * n + wid * per_w + j * SC_WINDOW, SC_WINDOW)]
            return pltpu.async_copy(buf.at[c % SC_BUFS], dst, st_sem.at[c % SC_BUFS])

        n_win = len(windows)
        loads = {c: gather(c) for c in range(min(SC_BUFS - 1, n_win))}
        stores, waited = {}, set()
        for c in range(n_win):
            loads[c].wait()
            nxt = c + SC_BUFS - 1
            if nxt < n_win:
                if nxt - SC_BUFS >= 0:
                    stores[nxt - SC_BUFS].wait()
                    waited.add(nxt - SC_BUFS)
                loads[nxt] = gather(nxt)
            stores[c] = store(c)
        for c in range(n_win):
            if c not in waited:
                stores[c].wait()

    return pl.kernel(
        body,
        out_type=jax.ShapeDtypeStruct((2 * n, width), jnp.int32),
        mesh=_sc_mesh(),
        scratch_types=[pltpu.VMEM((2, n_ch, SC_WINDOW), jnp.int32),
                       pltpu.VMEM((SC_BUFS, SC_WINDOW, width), jnp.int32),
                       pltpu.SemaphoreType.DMA((SC_BUFS,)),
                       pltpu.SemaphoreType.DMA((SC_BUFS,))],
        name="sc_collect",
    )(rows, idx)


def _expert_body(plan_ref, xs_hbm, w1_hbm, w3_hbm, w2_hbm, o_ref,
                 xbuf, w1s, w3s, w2s, xsem, sem):
    i = pl.program_id(0)
    n_used = plan_ref[PLAN_N_USED * V7X_LANES]
    n_runs = plan_ref[PLAN_N_RUNS * V7X_LANES]
    tme = xbuf.shape[1]
    run = plan_ref[PLAN_TILE_RUN * V7X_LANES + i]
    prev_run = plan_ref[PLAN_TILE_RUN * V7X_LANES + jnp.maximum(i - 1, 0)]
    first = jnp.logical_or(i == 0, run != prev_run)
    slot = run % EXPERT_W_BUFS

    def row_copy(t):
        s = t % EXPERT_ROW_BUFS
        start = t * tme if isinstance(t, int) else pl.multiple_of(t * tme, tme)
        src = xs_hbm.at[pl.ds(start, tme)]
        return pltpu.make_async_copy(src, xbuf.at[s], xsem.at[s])

    def weight_copies(r):
        e = plan_ref[PLAN_RUN_EXPERT * V7X_LANES + r]
        s = r % EXPERT_W_BUFS
        return (pltpu.make_async_copy(w1_hbm.at[e], w1s.at[s], sem.at[0, s]),
                pltpu.make_async_copy(w3_hbm.at[e], w3s.at[s], sem.at[1, s]),
                pltpu.make_async_copy(w2_hbm.at[e], w2s.at[s], sem.at[2, s]))

    @pl.when(i == 0)
    def _():
        for t in range(EXPERT_ROW_BUFS - 1):
            @pl.when(t < n_used)
            def _():
                row_copy(t).start()
        for r in range(EXPERT_W_BUFS - 1):
            @pl.when(r < n_runs)
            def _():
                for cp in weight_copies(r):
                    cp.start(priority=EXPERT_W_DMA_PRIORITY)

    @pl.when(jnp.logical_and(first, i < n_used))
    def _():
        @pl.when(run + (EXPERT_W_BUFS - 1) < n_runs)
        def _():
            for cp in weight_copies(run + (EXPERT_W_BUFS - 1)):
                cp.start(priority=EXPERT_W_DMA_PRIORITY)

        for cp in weight_copies(run):
            cp.wait()

    @pl.when(i < n_used)
    def _():
        @pl.when(i + (EXPERT_ROW_BUFS - 1) < n_used)
        def _():
            row_copy(i + (EXPERT_ROW_BUFS - 1)).start()

        row_copy(i).wait()
        lo, hi = _unpack_halves(xbuf[i % EXPERT_ROW_BUFS])
        lo = lo.astype(BF16)
        hi = hi.astype(BF16)
        half = D // 2
        a = _dot(lo, w1s[slot, 0:half, :]) + _dot(hi, w1s[slot, half:D, :])
        b = _dot(lo, w3s[slot, 0:half, :]) + _dot(hi, w3s[slot, half:D, :])
        z = (a * _sigmoid(a)) * b
        o_ref[...] = _pack_halves(_dot(z.astype(BF16), w2s[slot]))


def _experts(plan, xs, w1, w3, w2, tme):
    n_slots = xs.shape[0]
    grid_spec = pltpu.PrefetchScalarGridSpec(
        num_scalar_prefetch=1,
        grid=(n_slots // tme,),
        in_specs=[pl.BlockSpec(memory_space=pl.ANY),
                  pl.BlockSpec(memory_space=pl.ANY),
                  pl.BlockSpec(memory_space=pl.ANY),
                  pl.BlockSpec(memory_space=pl.ANY)],
        out_specs=pl.BlockSpec((tme, D // 2),
                               lambda i, plan: (jnp.clip(
                                   i, 0, jnp.maximum(plan[PLAN_N_USED * V7X_LANES] - 1, 0)), 0)),
        scratch_shapes=[pltpu.VMEM((EXPERT_ROW_BUFS, tme, D // 2), jnp.int32),
                        pltpu.VMEM((EXPERT_W_BUFS, D, D_EXPERT), BF16),
                        pltpu.VMEM((EXPERT_W_BUFS, D, D_EXPERT), BF16),
                        pltpu.VMEM((EXPERT_W_BUFS, D_EXPERT, D), BF16),
                        pltpu.SemaphoreType.DMA((EXPERT_ROW_BUFS,)),
                        pltpu.SemaphoreType.DMA((3, EXPERT_W_BUFS))],
    )
    return pl.pallas_call(
        _expert_body,
        grid_spec=grid_spec,
        out_shape=jax.ShapeDtypeStruct((n_slots, D // 2), jnp.int32),
        compiler_params=_params(("arbitrary",)),
        name="experts",
    )(plan, xs, w1, w3, w2)


def _final_body(x1_ref, mod_ref, y0_ref, y1_ref, wt_ref, gf_ref, o_ref):
    w0 = wt_ref[:, 0:1]
    w1 = wt_ref[:, 1:2]
    lo0, hi0 = _unpack_halves(y0_ref[...])
    lo1, hi1 = _unpack_halves(y1_ref[...])
    moe = jnp.concatenate([w0 * lo0 + w1 * lo1, w0 * hi0 + w1 * hi1], axis=1)
    x2 = x1_ref[...].astype(F32) + mod_ref[0, 5:6, :] * moe
    ms = jnp.mean(x2 * x2, axis=-1, keepdims=True)
    o_ref[...] = x2 * lax.rsqrt(ms + EPS) * gf_ref[...]


def _final(x1, mod, tiles_per_mod, yg, wts, g_final):
    n = x1.shape[0]
    tm = FIN_TM
    nt = n // tm
    return pl.pallas_call(
        _final_body,
        grid=(nt,),
        in_specs=[pl.BlockSpec((tm, D), lambda i: (i, 0)),
                  pl.BlockSpec((1, 6, D), lambda i: (i // tiles_per_mod, 0, 0)),
                  pl.BlockSpec((tm, D // 2), lambda i: (i, 0)),
                  pl.BlockSpec((tm, D // 2), lambda i: (i + nt, 0)),
                  pl.BlockSpec((tm, V7X_LANES), lambda i: (i, 0)),
                  _const_spec((1, D))],
        out_specs=pl.BlockSpec((tm, D), lambda i: (i, 0)),
        out_shape=jax.ShapeDtypeStruct((n, D), F32),
        compiler_params=_params(("parallel",)),
        name="final",
    )(x1, mod, yg, yg, wts, g_final)


def _plan_body(route_ref, cnt_ref, dest_ref, meta_ref, *, tme, n_slots):
    shift = tme.bit_length() - 1
    c = cnt_ref[...]
    erow = lax.broadcasted_iota(jnp.int32, c.shape, 0)

    def running_sum(v):
        for sh in (1, 2, 4, 8, 16):
            v = v + jnp.where(erow >= sh, pltpu.roll(v, sh, 0), 0)
        return v

    padded = lax.shift_left(lax.shift_right_logical(c + (tme - 1), shift), shift)
    pend = running_sum(padded)
    pstart = pend - padded
    run_of = running_sum((c > 0).astype(jnp.int32)) - 1

    acc = route_ref[2:4, :]
    ids = route_ref[0:2, :]
    for e in range(N_EXPERTS):
        acc = acc + jnp.where(ids == e, pstart[e:e + 1, 0:1], 0)
    dest_ref[0:2, :] = jnp.clip(acc, 0, n_slots - 1)
    dest_ref[2:EPG, :] = jnp.zeros((EPG - 2, acc.shape[1]), jnp.int32)

    lane = lax.broadcasted_iota(jnp.int32, (1, V7X_LANES), 1)
    tile_e = jnp.zeros((1, V7X_LANES), jnp.int32)
    for e in range(N_EXPERTS):
        tile_e = tile_e + (lane * tme >= pend[e:e + 1, 0:1]).astype(jnp.int32)
    tile_e = jnp.minimum(tile_e, N_EXPERTS - 1)
    tile_run = jnp.zeros((1, V7X_LANES), jnp.int32)
    run_e = jnp.zeros((1, V7X_LANES), jnp.int32)
    for e in range(N_EXPERTS):
        tile_run = tile_run + jnp.where(tile_e == e, run_of[e:e + 1, 0:1], 0)
        is_run = jnp.logical_and(lane == run_of[e:e + 1, 0:1], c[e:e + 1, 0:1] > 0)
        run_e = run_e + jnp.where(is_run, e, 0)
    n_runs = run_of[N_EXPERTS - 1:N_EXPERTS, :] + 1
    n_used = lax.shift_right_logical(pend[N_EXPERTS - 1:N_EXPERTS, :], shift)
    mrow = lax.broadcasted_iota(jnp.int32, (EPG, V7X_LANES), 0)
    meta_ref[...] = jnp.where(mrow == PLAN_TILE_RUN, tile_run, jnp.where(
        mrow == PLAN_RUN_EXPERT, run_e, jnp.where(
            mrow == PLAN_N_RUNS, n_runs, jnp.where(mrow == PLAN_N_USED, n_used, 0))))


def _slot_plan(route, cnt, n, tme):
    n_slots = ((2 * n + N_EXPERTS * (tme - 1)) // tme) * tme
    assert tme & (tme - 1) == 0 and n_slots // tme <= V7X_LANES
    dest, meta = pl.pallas_call(
        functools.partial(_plan_body, tme=tme, n_slots=n_slots),
        out_shape=[jax.ShapeDtypeStruct((EPG, n), jnp.int32),
                   jax.ShapeDtypeStruct((EPG, V7X_LANES), jnp.int32)],
        name="plan",
    )(route, cnt)
    return dest[0:2], meta.reshape(-1), n_slots


def _group(x, mod, mod_per_seq, h0, p, row_len, tme, cast_ws, after=None):
    n_b, seq_len, _ = x.shape
    n = n_b * seq_len
    xt = x.reshape(n, D)
    mod_seq = mod if mod_per_seq else jnp.broadcast_to(mod, (n_b, 6, D))
    xc = _xr_conv(x, mod_seq, p["g1"], p["w_in"], p["rnn_conv_w"], p["rnn_conv_b"], after)
    hf, hb, last = _rglru_scan(xc, h0, p["wg"], p["ba"], p["bx"], p["lam"])
    tiles_per_mod = (seq_len // MIX_TM) if mod_per_seq else (n // MIX_TM)
    outs = _mixer(xt, mod, tiles_per_mod, hf.reshape(n, D), hb.reshape(n, D),
                                     p["g1"], p["g2"], p["w_in"], p["conv_w"], p["wco"], p["wro"],
                                     p["wo"], p["wr"], p["br"], row_len, cast_ws)
    x1, h2, route, wts, cnt = outs[:5]
    casts = outs[5:]
    dest, plan, n_slots = _slot_plan(route, cnt, n, tme)
    xs = _sc_dispatch(h2, dest, n_slots)

    def finish(w1b, w3b, w2b):
        ys = _experts(plan, xs, w1b, w3b, w2b, tme)
        yg = _sc_collect(ys, dest)
        tiles_per_mod_f = (seq_len // FIN_TM) if mod_per_seq else (n // FIN_TM)
        y = _final(x1, mod, tiles_per_mod_f, yg, wts, p["g_final"])
        return y.reshape(n_b, seq_len, D)

    return finish, last, casts, cnt


def kernel(x_prompt, x_sample, state_rnn, c, c_ctx, w_ada, b_ada, g_norm1, g_norm2, w_in, conv_w, w_conv_out, rnn_conv_w, rnn_conv_b, w_gate_a, b_gate_a, w_gate_x, b_gate_x, lam, w_rnn_out, w_o, w_router_group, b_router_group, w_router_expert, b_router_expert, w1, w3, w2, g_final):
    assert w_ada.shape[0] == 1, "single layer"
    n_pb, n_sb = x_prompt.shape[0], x_sample.shape[0]

    cond = jnp.concatenate([c_ctx[None, :], c, jnp.zeros((16 - 1 - n_sb, D), F32)], axis=0)
    mod = _ada(cond, w_ada[0], b_ada[0]).reshape(16, 6, D)

    w_in_b = w_in[0].astype(BF16)
    pad_w = jnp.zeros((D, EPG - N_GROUPS), F32)
    wr = jnp.concatenate([w_router_group[0], pad_w, w_router_expert[0],
                          jnp.zeros((D, ROUTE_PAD - EPG - N_EXPERTS), F32)], axis=1)
    br = jnp.concatenate([b_router_group[0], jnp.full((EPG - N_GROUPS,), NEG_BIG, F32),
                          b_router_expert[0],
                          jnp.zeros((ROUTE_PAD - EPG - N_EXPERTS,), F32)]).reshape(1, ROUTE_PAD)
    wr_hi = wr.astype(BF16)
    p = dict(
        g1=g_norm1, g2=g_norm2, w_in=w_in_b,
        conv_w=conv_w[0], rnn_conv_w=rnn_conv_w[0], rnn_conv_b=rnn_conv_b,
        wg=(0.5 * jnp.concatenate([w_gate_a[0], w_gate_x[0]], axis=-1)).astype(BF16),
        ba=b_gate_a[0], bx=b_gate_x[0], lam=lam[0],
        wco=w_conv_out[0].astype(BF16), wro=w_rnn_out[0].astype(BF16), wo=w_o[0].astype(BF16),
        wr=jnp.concatenate([wr_hi, (wr - wr_hi.astype(F32)).astype(BF16)], axis=1), br=br,
        g_final=g_final.reshape(1, D),
    )

    h0_s = state_rnn[:, 0].reshape(n_sb // SCAN_B, SCAN_B, 2, D).transpose(0, 2, 1, 3)
    finish_s, _, (w1b, w3b), cnt_s = _group(x_sample, mod[1:1 + n_sb], True, h0_s, p, GRID_W, 512,
                                            [w1[0], w3[0]])

    h0_p = jnp.zeros((n_pb // SCAN_B, 2, SCAN_B, D), F32)
    finish_p, last, (w2b,), _ = _group(x_prompt, mod[0:1], False, h0_p, p, x_prompt.shape[1], 512,
                                       [w2[0]], after=cnt_s)
    state_new = last.transpose(0, 2, 1, 3).reshape(n_pb, 1, 2, D)

    y_sample = finish_s(w1b, w3b, w2b)
    y_prompt = finish_p(w1b, w3b, w2b)
    return (y_prompt, y_sample, state_new)
```
